```python
import jax, jax.numpy as jnp
from jax import lax
import numpy as np

D_MODEL = 1024
BATCH = 8
SEQ = 4096
DEPTH = 1

HEAD_DIM = 64
FOX_HEADS = 8
NSA_HEADS = 8
NSA_GROUPS = 2
NSA_HPG = NSA_HEADS // NSA_GROUPS
BRANCH_WIDTH = 512
N_BRANCH = 2
N_NSA_BRANCH = 3
ROPE_DIM = HEAD_DIM // 4
ROPE_THETA = 500000.0
Q_BLOCK = 128
CMP_LEN = 32
CMP_STRIDE = 16
CMP_HIDDEN = 2 * HEAD_DIM
SLC_LEN = 64
SLC_TOP = 16
WINDOW = 512
FORGET_BIAS_INIT = 4.0
FORCED_SCORE = 1e9
NEG_INF = -1e30
PEER_HEADS = 8
N_KEYS = 128
N_EXPERTS = N_KEYS * N_KEYS
PEER_QDIM = 256
PEER_TOPK = 16
PEER_CHUNK = 128
RMS_EPS = 1e-6

KV_WIDTH = NSA_GROUPS * HEAD_DIM
IN_SPLITS = (BRANCH_WIDTH, BRANCH_WIDTH, BRANCH_WIDTH, FOX_HEADS, BRANCH_WIDTH,
             KV_WIDTH, KV_WIDTH, KV_WIDTH, KV_WIDTH, KV_WIDTH, KV_WIDTH,
             NSA_HEADS * N_NSA_BRANCH, N_BRANCH * D_MODEL)
D_IN = sum(IN_SPLITS)

kernel_name = "fox_nsa_gated_hybrid_peer"


def rmsnorm(x, g):
    xf = x.astype(jnp.float32)
    y = xf * lax.rsqrt(jnp.mean(xf * xf, axis=-1, keepdims=True) + RMS_EPS)
    return (y * g.astype(jnp.float32)).astype(x.dtype)


def split_columns(t, sizes):
    offs = np.cumsum((0,) + tuple(sizes))
    return [t[..., int(a):int(b)] for a, b in zip(offs[:-1], offs[1:])]


def rope_tables(pos):
    inv = jnp.power(ROPE_THETA, -jnp.arange(0, ROPE_DIM, 2, dtype=jnp.float32) / ROPE_DIM)
    ang = pos[:, None] * inv[None, :]
    return jnp.cos(ang), jnp.sin(ang)


def apply_partial_rope(x, cos, sin):
    half = ROPE_DIM // 2
    xr = x[..., :ROPE_DIM].astype(jnp.float32)
    x1, x2 = xr[..., :half], xr[..., half:]
    c, s = cos[:, None, :], sin[:, None, :]
    rot = jnp.concatenate([x1 * c - x2 * s, x1 * s + x2 * c], axis=-1)
    return jnp.concatenate([rot.astype(x.dtype), x[..., ROPE_DIM:]], axis=-1)


def fox_attention(q, k, v, log_f):
    B_, S_, H, Dh = q.shape
    scale = Dh ** -0.5
    c = jnp.swapaxes(jnp.cumsum(log_f, axis=1), 1, 2)
    k_pos = jnp.arange(S_)

    def block(i):
        q0 = i * Q_BLOCK
        q_pos = q0 + jnp.arange(Q_BLOCK)
        qb = lax.dynamic_slice_in_dim(q, q0, Q_BLOCK, axis=1)
        cb = lax.dynamic_slice_in_dim(c, q0, Q_BLOCK, axis=2)
        logits = jnp.einsum('bqhd,bkhd->bhqk', qb, k).astype(jnp.float32) * scale
        logits = logits + (cb[..., :, None] - c[..., None, :])
        mask = k_pos[None, :] <= q_pos[:, None]
        p = jax.nn.softmax(jnp.where(mask, logits, NEG_INF), axis=-1)
        return jnp.einsum('bhqk,bkhd->bqhd', p.astype(v.dtype), v).reshape(B_, Q_BLOCK, H * Dh)

    out = lax.map(block, jnp.arange(S_ // Q_BLOCK))
    return jnp.moveaxis(out, 0, 1).reshape(B_, S_, H * Dh)


def compress_kv(k, v, cmp_pos, w1, w2):
    B_, S_ = k.shape[0], k.shape[1]
    n_cmp = (S_ - CMP_LEN) // CMP_STRIDE + 1
    starts = np.arange(n_cmp) * CMP_STRIDE
    idx = starts[:, None] + np.arange(CMP_LEN)[None, :]

    def phi(t, j):
        blocks = t[:, idx] + cmp_pos[j][None, None, :, None, :]
        flat = jnp.swapaxes(blocks, 2, 3).reshape(B_, n_cmp, NSA_GROUPS, CMP_LEN * HEAD_DIM)
        return jax.nn.gelu(flat @ w1[j], approximate=False) @ w2[j]

    return phi(k, 0), phi(v, 1), jnp.asarray(starts + CMP_LEN - 1, dtype=jnp.int32)


def nsa_attention(q, k_cmp, v_cmp, cmp_end, k_slc, v_slc, k_win, v_win, gates):
    B_, S_, H, Dh = q.shape
    G = NSA_GROUPS
    scale = Dh ** -0.5
    n_cmp = k_cmp.shape[1]
    n_slc = S_ // SLC_LEN
    top = min(SLC_TOP, n_slc)
    cs = np.arange(n_cmp)[:, None] * CMP_STRIDE
    ss = np.arange(n_slc)[None, :] * SLC_LEN
    ov = np.clip(np.minimum(cs + CMP_LEN, ss + SLC_LEN) - np.maximum(cs, ss), 0, None) / CMP_LEN
    overlap = jnp.asarray(ov, dtype=jnp.float32)
    kb = k_slc.reshape(B_, n_slc, SLC_LEN, G, Dh).transpose(0, 3, 1, 2, 4)
    vb = v_slc.reshape(B_, n_slc, SLC_LEN, G, Dh).transpose(0, 3, 1, 2, 4)
    pad = ((0, 0), (WINDOW, 0), (0, 0), (0, 0))
    kwp, vwp = jnp.pad(k_win, pad), jnp.pad(v_win, pad)
    bi = jnp.arange(B_)[:, None, None, None]
    gi = jnp.arange(G)[None, :, None, None]
    blk_ids = jnp.arange(n_slc)

    def block(i):
        q0 = i * Q_BLOCK
        q_pos = q0 + jnp.arange(Q_BLOCK)
        qb = lax.dynamic_slice_in_dim(q, q0, Q_BLOCK, axis=1).reshape(B_, Q_BLOCK, G, NSA_HPG, Dh)
        gb = lax.dynamic_slice_in_dim(gates, q0, Q_BLOCK, axis=1).reshape(B_, Q_BLOCK, G, NSA_HPG, N_NSA_BRANCH)
        lc = jnp.einsum('bqghd,bngd->bghqn', qb, k_cmp).astype(jnp.float32) * scale
        mc = cmp_end[None, :] <= q_pos[:, None]
        pc = jax.nn.softmax(jnp.where(mc, lc, NEG_INF), axis=-1) * mc
        o_cmp = jnp.einsum('bghqn,bngd->bqghd', pc.astype(v_cmp.dtype), v_cmp)
        imp = jnp.einsum('bghqn,nj->bgqj', pc, overlap)
        q_blk = q_pos // SLC_LEN
        forced = (blk_ids[None, :] == 0) | (blk_ids[None, :] == q_blk[:, None]) | (blk_ids[None, :] == q_blk[:, None] - 1)
        causal = blk_ids[None, :] <= q_blk[:, None]
        score = jnp.where(forced, FORCED_SCORE, jnp.where(causal, imp, -1.0))
        _, sel = lax.top_k(score, top)
        k_sel = kb[bi, gi, sel]
        v_sel = vb[bi, gi, sel]
        ls = jnp.einsum('bqghd,bgqkld->bghqkl', qb, k_sel).astype(jnp.float32) * scale
        tok = sel[..., None] * SLC_LEN + jnp.arange(SLC_LEN)
        ms = (tok <= q_pos[None, None, :, None, None])[:, :, None]
        ls = jnp.where(ms, ls, NEG_INF).reshape(B_, G, NSA_HPG, Q_BLOCK, top * SLC_LEN)
        ps = jax.nn.softmax(ls, axis=-1).reshape(B_, G, NSA_HPG, Q_BLOCK, top, SLC_LEN)
        o_slc = jnp.einsum('bghqkl,bgqkld->bqghd', ps.astype(v_sel.dtype), v_sel)
        kwb = lax.dynamic_slice_in_dim(kwp, q0, WINDOW + Q_BLOCK, axis=1)
        vwb = lax.dynamic_slice_in_dim(vwp, q0, WINDOW + Q_BLOCK, axis=1)
        k_pos = q0 - WINDOW + jnp.arange(WINDOW + Q_BLOCK)
        lw = jnp.einsum('bqghd,bkgd->bghqk', qb, kwb).astype(jnp.float32) * scale
        rel = q_pos[:, None] - k_pos[None, :]
        mw = (rel >= 0) & (rel < WINDOW) & (k_pos[None, :] >= 0)
        pw = jax.nn.softmax(jnp.where(mw, lw, NEG_INF), axis=-1)
        o_win = jnp.einsum('bghqk,bkgd->bqghd', pw.astype(vwb.dtype), vwb)
        o = gb[..., 0:1] * o_cmp + gb[..., 1:2] * o_slc + gb[..., 2:3] * o_win
        return o.astype(q.dtype).reshape(B_, Q_BLOCK, H * Dh)

    out = lax.map(block, jnp.arange(S_ // Q_BLOCK))
    return jnp.moveaxis(out, 0, 1).reshape(B_, S_, H * Dh)


def hybrid_mixer(h, w_in, fox_f_bias, cmp_pos, cmp_w1, cmp_w2, w_branch, w_out):
    B_, S_, _ = h.shape
    proj = h @ w_in
    (fq, fk, fv, f_logit, nq, kc, vc, ksl, vsl, kwn, vwn,
     nsa_gate_logit, merge_logit) = split_columns(proj, IN_SPLITS)
    heads = lambda t, n: t.reshape(B_, S_, n, HEAD_DIM)
    log_f = jax.nn.log_sigmoid(f_logit.astype(jnp.float32) + fox_f_bias.astype(jnp.float32))
    y_fox = fox_attention(heads(fq, FOX_HEADS), heads(fk, FOX_HEADS), heads(fv, FOX_HEADS), log_f)
    cos, sin = rope_tables(jnp.arange(S_, dtype=jnp.float32))
    q_nsa = apply_partial_rope(heads(nq, NSA_HEADS), cos, sin)
    k_slc = apply_partial_rope(heads(ksl, NSA_GROUPS), cos, sin)
    k_win = apply_partial_rope(heads(kwn, NSA_GROUPS), cos, sin)
    k_cmp, v_cmp, cmp_end = compress_kv(heads(kc, NSA_GROUPS), heads(vc, NSA_GROUPS), cmp_pos, cmp_w1, cmp_w2)
    k_cmp = apply_partial_rope(k_cmp, *rope_tables(cmp_end.astype(jnp.float32)))
    gates = jax.nn.sigmoid(nsa_gate_logit.astype(jnp.float32)).reshape(B_, S_, NSA_HEADS, N_NSA_BRANCH)
    y_nsa = nsa_attention(q_nsa, k_cmp, v_cmp, cmp_end, k_slc, heads(vsl, NSA_GROUPS),
                          k_win, heads(vwn, NSA_GROUPS), gates)
    g = jax.nn.sigmoid(merge_logit.astype(jnp.float32)).reshape(B_, S_, N_BRANCH, D_MODEL).astype(h.dtype)
    ys = jnp.stack([y_fox, y_nsa], axis=2)
    up = jnp.einsum('bsnc,ncd->bsnd', ys, w_branch)
    merged = jnp.sum(g * up, axis=2)
    return merged @ w_out


def peer_ffn(h, wq, subkeys, u, v):
    B_, S_, D = h.shape
    T = B_ * S_
    hf = h.reshape(T, D)
    q = (hf @ wq).reshape(T, PEER_HEADS, 2, PEER_QDIM // 2)
    s = jnp.einsum('thpd,hpnd->thpn', q, subkeys).astype(jnp.float32)
    s1, i1 = lax.top_k(s[:, :, 0], PEER_TOPK)
    s2, i2 = lax.top_k(s[:, :, 1], PEER_TOPK)
    cand = (s1[..., :, None] + s2[..., None, :]).reshape(T, PEER_HEADS, PEER_TOPK * PEER_TOPK)
    cand_idx = (i1[..., :, None] * N_KEYS + i2[..., None, :]).reshape(T, PEER_HEADS, PEER_TOPK * PEER_TOPK)
    top_s, pos = lax.top_k(cand, PEER_TOPK)
    idx = jnp.take_along_axis(cand_idx, pos, axis=-1)
    w = jax.nn.softmax(top_s, axis=-1)

    def chunk(args):
        x_c, idx_c, w_c = args
        act = jax.nn.gelu(jnp.einsum('cd,chkd->chk', x_c, u[idx_c]).astype(jnp.float32), approximate=False)
        coef = (w_c * act).astype(v.dtype)
        return jnp.einsum('chk,chkd->cd', coef, v[idx_c])

    nc = T // PEER_CHUNK
    out = lax.map(chunk, (hf.reshape(nc, PEER_CHUNK, D),
                          idx.reshape(nc, PEER_CHUNK, PEER_HEADS, PEER_TOPK),
                          w.reshape(nc, PEER_CHUNK, PEER_HEADS, PEER_TOPK)))
    return out.reshape(B_, S_, D)


def setup_inputs(seed: int = 0) -> dict:
    key = jax.random.key(seed)
    ks = jax.random.split(key, 16)
    f32 = jnp.float32
    nrm = lambda k, shape, scale: scale * jax.random.normal(k, shape, f32)
    L = DEPTH
    return {
        "x": jax.random.normal(ks[0], (BATCH, SEQ, D_MODEL), f32),
        "norm_mix": 1.0 + nrm(ks[1], (L, D_MODEL), 0.02),
        "w_in": nrm(ks[2], (L, D_MODEL, D_IN), D_MODEL ** -0.5),
        "fox_f_bias": FORGET_BIAS_INIT + nrm(ks[3], (L, FOX_HEADS), 0.1),
        "nsa_cmp_pos": nrm(ks[4], (L, 2, CMP_LEN, HEAD_DIM), 0.02),
        "nsa_cmp_w1": nrm(ks[5], (L, 2, CMP_LEN * HEAD_DIM, CMP_HIDDEN), (CMP_LEN * HEAD_DIM) ** -0.5),
        "nsa_cmp_w2": nrm(ks[6], (L, 2, CMP_HIDDEN, HEAD_DIM), CMP_HIDDEN ** -0.5),
        "w_branch": nrm(ks[7], (L, N_BRANCH, BRANCH_WIDTH, D_MODEL), BRANCH_WIDTH ** -0.5),
        "w_out": nrm(ks[8], (L, D_MODEL, D_MODEL), D_MODEL ** -0.5),
        "norm_ffn": 1.0 + nrm(ks[9], (L, D_MODEL), 0.02),
        "peer_wq": nrm(ks[10], (L, D_MODEL, PEER_HEADS * PEER_QDIM), D_MODEL ** -0.5),
        "peer_subkeys": nrm(ks[11], (L, PEER_HEADS, 2, N_KEYS, PEER_QDIM // 2), (PEER_QDIM // 2) ** -0.5),
        "peer_u": nrm(ks[12], (L, N_EXPERTS, D_MODEL), D_MODEL ** -0.5),
        "peer_v": nrm(ks[13], (L, N_EXPERTS, D_MODEL), (PEER_HEADS * PEER_TOPK) ** -0.5),
        "norm_final": 1.0 + nrm(ks[14], (D_MODEL,), 0.02),
    }


def reference(x, norm_mix, w_in, fox_f_bias, nsa_cmp_pos, nsa_cmp_w1, nsa_cmp_w2, w_branch, w_out,
              norm_ffn, peer_wq, peer_subkeys, peer_u, peer_v, norm_final):
    h = x
    for l in range(DEPTH):
        h = h + hybrid_mixer(rmsnorm(h, norm_mix[l]), w_in[l], fox_f_bias[l], nsa_cmp_pos[l],
                             nsa_cmp_w1[l], nsa_cmp_w2[l], w_branch[l], w_out[l])
        h = h + peer_ffn(rmsnorm(h, norm_ffn[l]), peer_wq[l], peer_subkeys[l], peer_u[l], peer_v[l])
    return rmsnorm(h, norm_final)
```

```python
import functools

import jax
import jax.numpy as jnp
import numpy as np
from jax import lax
from jax.experimental import pallas as pl
from jax.experimental.pallas import tpu as pltpu

D_MODEL = 1024
HEAD_DIM = 64
FOX_HEADS = 8
NSA_HEADS = 8
NSA_GROUPS = 2
NSA_HPG = NSA_HEADS // NSA_GROUPS
BRANCH_WIDTH = 512
N_BRANCH = 2
N_NSA_BRANCH = 3
ROPE_DIM = HEAD_DIM // 4
ROPE_THETA = 500000.0
Q_BLOCK = 128
CMP_LEN = 32
CMP_STRIDE = 16
CMP_HIDDEN = 2 * HEAD_DIM
SLC_LEN = 64
SLC_TOP = 16
WINDOW = 512
FORCED_SCORE = 1e9
NEG_INF = -1e30
PEER_HEADS = 8
N_KEYS = 128
N_EXPERTS = N_KEYS * N_KEYS
PEER_QDIM = 256
PEER_TOPK = 16
PEER_CHUNK = 128
RMS_EPS = 1e-6
KV_WIDTH = NSA_GROUPS * HEAD_DIM
IN_SPLITS = (BRANCH_WIDTH, BRANCH_WIDTH, BRANCH_WIDTH, FOX_HEADS, BRANCH_WIDTH,
             KV_WIDTH, KV_WIDTH, KV_WIDTH, KV_WIDTH, KV_WIDTH, KV_WIDTH,
             NSA_HEADS * N_NSA_BRANCH, N_BRANCH * D_MODEL)


def _rmsnorm_kernel(x_ref, g_ref, o_ref):
    x = x_ref[...]
    ms = jnp.mean(x * x, axis=-1, keepdims=True)
    o_ref[...] = x * lax.rsqrt(ms + RMS_EPS) * g_ref[...]


def rmsnorm_pallas(x2d, g, rows=512):
    t, d = x2d.shape
    return pl.pallas_call(
        _rmsnorm_kernel,
        grid=(t // rows,),
        in_specs=[pl.BlockSpec((rows, d), lambda i: (i, 0)),
                  pl.BlockSpec((1, d), lambda i: (0, 0))],
        out_specs=pl.BlockSpec((rows, d), lambda i: (i, 0)),
        out_shape=jax.ShapeDtypeStruct((t, d), jnp.float32),
    )(x2d, g.reshape(1, d))


def rmsnorm(x, g):
    xf = x.astype(jnp.float32)
    y = xf * lax.rsqrt(jnp.mean(xf * xf, axis=-1, keepdims=True) + RMS_EPS)
    return (y * g.astype(jnp.float32)).astype(x.dtype)


def split_columns(t, sizes):
    offs = np.cumsum((0,) + tuple(sizes))
    return [t[..., int(a):int(b)] for a, b in zip(offs[:-1], offs[1:])]


def rope_tables(pos):
    inv = jnp.power(ROPE_THETA, -jnp.arange(0, ROPE_DIM, 2, dtype=jnp.float32) / ROPE_DIM)
    ang = pos[:, None] * inv[None, :]
    return jnp.cos(ang), jnp.sin(ang)


def apply_partial_rope(x, cos, sin):
    half = ROPE_DIM // 2
    xr = x[..., :ROPE_DIM].astype(jnp.float32)
    x1, x2 = xr[..., :half], xr[..., half:]
    c, s = cos[:, None, :], sin[:, None, :]
    rot = jnp.concatenate([x1 * c - x2 * s, x1 * s + x2 * c], axis=-1)
    return jnp.concatenate([rot.astype(x.dtype), x[..., ROPE_DIM:]], axis=-1)


def fox_attention(q, k, v, log_f):
    B_, S_, H, Dh = q.shape
    scale = Dh ** -0.5
    c = jnp.swapaxes(jnp.cumsum(log_f, axis=1), 1, 2)
    k_pos = jnp.arange(S_)

    def block(i):
        q0 = i * Q_BLOCK
        q_pos = q0 + jnp.arange(Q_BLOCK)
        qb = lax.dynamic_slice_in_dim(q, q0, Q_BLOCK, axis=1)
        cb = lax.dynamic_slice_in_dim(c, q0, Q_BLOCK, axis=2)
        logits = jnp.einsum('bqhd,bkhd->bhqk', qb, k).astype(jnp.float32) * scale
        logits = logits + (cb[..., :, None] - c[..., None, :])
        mask = k_pos[None, :] <= q_pos[:, None]
        p = jax.nn.softmax(jnp.where(mask, logits, NEG_INF), axis=-1)
        return jnp.einsum('bhqk,bkhd->bqhd', p.astype(v.dtype), v).reshape(B_, Q_BLOCK, H * Dh)

    out = lax.map(block, jnp.arange(S_ // Q_BLOCK))
    return jnp.moveaxis(out, 0, 1).reshape(B_, S_, H * Dh)


def compress_kv(k, v, cmp_pos, w1, w2):
    B_, S_ = k.shape[0], k.shape[1]
    n_cmp = (S_ - CMP_LEN) // CMP_STRIDE + 1
    starts = np.arange(n_cmp) * CMP_STRIDE
    idx = starts[:, None] + np.arange(CMP_LEN)[None, :]

    def phi(t, j):
        blocks = t[:, idx] + cmp_pos[j][None, None, :, None, :]
        flat = jnp.swapaxes(blocks, 2, 3).reshape(B_, n_cmp, NSA_GROUPS, CMP_LEN * HEAD_DIM)
        return jax.nn.gelu(flat @ w1[j], approximate=False) @ w2[j]

    return phi(k, 0), phi(v, 1), jnp.asarray(starts + CMP_LEN - 1, dtype=jnp.int32)


def nsa_attention(q, k_cmp, v_cmp, cmp_end, k_slc, v_slc, k_win, v_win, gates):
    B_, S_, H, Dh = q.shape
    G = NSA_GROUPS
    scale = Dh ** -0.5
    n_cmp = k_cmp.shape[1]
    n_slc = S_ // SLC_LEN
    top = min(SLC_TOP, n_slc)
    cs = np.arange(n_cmp)[:, None] * CMP_STRIDE
    ss = np.arange(n_slc)[None, :] * SLC_LEN
    ov = np.clip(np.minimum(cs + CMP_LEN, ss + SLC_LEN) - np.maximum(cs, ss), 0, None) / CMP_LEN
    overlap = jnp.asarray(ov, dtype=jnp.float32)
    kb = k_slc.reshape(B_, n_slc, SLC_LEN, G, Dh).transpose(0, 3, 1, 2, 4)
    vb = v_slc.reshape(B_, n_slc, SLC_LEN, G, Dh).transpose(0, 3, 1, 2, 4)
    pad = ((0, 0), (WINDOW, 0), (0, 0), (0, 0))
    kwp, vwp = jnp.pad(k_win, pad), jnp.pad(v_win, pad)
    bi = jnp.arange(B_)[:, None, None, None]
    gi = jnp.arange(G)[None, :, None, None]
    blk_ids = jnp.arange(n_slc)

    def block(i):
        q0 = i * Q_BLOCK
        q_pos = q0 + jnp.arange(Q_BLOCK)
        qb = lax.dynamic_slice_in_dim(q, q0, Q_BLOCK, axis=1).reshape(B_, Q_BLOCK, G, NSA_HPG, Dh)
        gb = lax.dynamic_slice_in_dim(gates, q0, Q_BLOCK, axis=1).reshape(B_, Q_BLOCK, G, NSA_HPG, N_NSA_BRANCH)
        lc = jnp.einsum('bqghd,bngd->bghqn', qb, k_cmp).astype(jnp.float32) * scale
        mc = cmp_end[None, :] <= q_pos[:, None]
        pc = jax.nn.softmax(jnp.where(mc, lc, NEG_INF), axis=-1) * mc
        o_cmp = jnp.einsum('bghqn,bngd->bqghd', pc.astype(v_cmp.dtype), v_cmp)
        imp = jnp.einsum('bghqn,nj->bgqj', pc, overlap)
        q_blk = q_pos // SLC_LEN
        forced = (blk_ids[None, :] == 0) | (blk_ids[None, :] == q_blk[:, None]) | (blk_ids[None, :] == q_blk[:, None] - 1)
        causal = blk_ids[None, :] <= q_blk[:, None]
        score = jnp.where(forced, FORCED_SCORE, jnp.where(causal, imp, -1.0))
        _, sel = lax.top_k(score, top)
        k_sel = kb[bi, gi, sel]
        v_sel = vb[bi, gi, sel]
        ls = jnp.einsum('bqghd,bgqkld->bghqkl', qb, k_sel).astype(jnp.float32) * scale
        tok = sel[..., None] * SLC_LEN + jnp.arange(SLC_LEN)
        ms = (tok <= q_pos[None, None, :, None, None])[:, :, None]
        ls = jnp.where(ms, ls, NEG_INF).reshape(B_, G, NSA_HPG, Q_BLOCK, top * SLC_LEN)
        ps = jax.nn.softmax(ls, axis=-1).reshape(B_, G, NSA_HPG, Q_BLOCK, top, SLC_LEN)
        o_slc = jnp.einsum('bghqkl,bgqkld->bqghd', ps.astype(v_sel.dtype), v_sel)
        kwb = lax.dynamic_slice_in_dim(kwp, q0, WINDOW + Q_BLOCK, axis=1)
        vwb = lax.dynamic_slice_in_dim(vwp, q0, WINDOW + Q_BLOCK, axis=1)
        k_pos = q0 - WINDOW + jnp.arange(WINDOW + Q_BLOCK)
        lw = jnp.einsum('bqghd,bkgd->bghqk', qb, kwb).astype(jnp.float32) * scale
        rel = q_pos[:, None] - k_pos[None, :]
        mw = (rel >= 0) & (rel < WINDOW) & (k_pos[None, :] >= 0)
        pw = jax.nn.softmax(jnp.where(mw, lw, NEG_INF), axis=-1)
        o_win = jnp.einsum('bghqk,bkgd->bqghd', pw.astype(vwb.dtype), vwb)
        o = gb[..., 0:1] * o_cmp + gb[..., 1:2] * o_slc + gb[..., 2:3] * o_win
        return o.astype(q.dtype).reshape(B_, Q_BLOCK, H * Dh)

    out = lax.map(block, jnp.arange(S_ // Q_BLOCK))
    return jnp.moveaxis(out, 0, 1).reshape(B_, S_, H * Dh)


def hybrid_mixer(h, w_in, fox_f_bias, cmp_pos, cmp_w1, cmp_w2, w_branch, w_out):
    B_, S_, _ = h.shape
    proj = h @ w_in
    (fq, fk, fv, f_logit, nq, kc, vc, ksl, vsl, kwn, vwn,
     nsa_gate_logit, merge_logit) = split_columns(proj, IN_SPLITS)
    heads = lambda t, n: t.reshape(B_, S_, n, HEAD_DIM)
    log_f = jax.nn.log_sigmoid(f_logit.astype(jnp.float32) + fox_f_bias.astype(jnp.float32))
    y_fox = fox_attention(heads(fq, FOX_HEADS), heads(fk, FOX_HEADS), heads(fv, FOX_HEADS), log_f)
    cos, sin = rope_tables(jnp.arange(S_, dtype=jnp.float32))
    q_nsa = apply_partial_rope(heads(nq, NSA_HEADS), cos, sin)
    k_slc = apply_partial_rope(heads(ksl, NSA_GROUPS), cos, sin)
    k_win = apply_partial_rope(heads(kwn, NSA_GROUPS), cos, sin)
    k_cmp, v_cmp, cmp_end = compress_kv(heads(kc, NSA_GROUPS), heads(vc, NSA_GROUPS), cmp_pos, cmp_w1, cmp_w2)
    k_cmp = apply_partial_rope(k_cmp, *rope_tables(cmp_end.astype(jnp.float32)))
    gates = jax.nn.sigmoid(nsa_gate_logit.astype(jnp.float32)).reshape(B_, S_, NSA_HEADS, N_NSA_BRANCH)
    y_nsa = nsa_attention(q_nsa, k_cmp, v_cmp, cmp_end, k_slc, heads(vsl, NSA_GROUPS),
                          k_win, heads(vwn, NSA_GROUPS), gates)
    g = jax.nn.sigmoid(merge_logit.astype(jnp.float32)).reshape(B_, S_, N_BRANCH, D_MODEL).astype(h.dtype)
    ys = jnp.stack([y_fox, y_nsa], axis=2)
    up = jnp.einsum('bsnc,ncd->bsnd', ys, w_branch)
    merged = jnp.sum(g * up, axis=2)
    return merged @ w_out


def peer_ffn(h, wq, subkeys, u, v):
    B_, S_, D = h.shape
    T = B_ * S_
    hf = h.reshape(T, D)
    q = (hf @ wq).reshape(T, PEER_HEADS, 2, PEER_QDIM // 2)
    s = jnp.einsum('thpd,hpnd->thpn', q, subkeys).astype(jnp.float32)
    s1, i1 = lax.top_k(s[:, :, 0], PEER_TOPK)
    s2, i2 = lax.top_k(s[:, :, 1], PEER_TOPK)
    cand = (s1[..., :, None] + s2[..., None, :]).reshape(T, PEER_HEADS, PEER_TOPK * PEER_TOPK)
    cand_idx = (i1[..., :, None] * N_KEYS + i2[..., None, :]).reshape(T, PEER_HEADS, PEER_TOPK * PEER_TOPK)
    top_s, pos = lax.top_k(cand, PEER_TOPK)
    idx = jnp.take_along_axis(cand_idx, pos, axis=-1)
    w = jax.nn.softmax(top_s, axis=-1)

    def chunk(args):
        x_c, idx_c, w_c = args
        act = jax.nn.gelu(jnp.einsum('cd,chkd->chk', x_c, u[idx_c]).astype(jnp.float32), approximate=False)
        coef = (w_c * act).astype(v.dtype)
        return jnp.einsum('chk,chkd->cd', coef, v[idx_c])

    nc = T // PEER_CHUNK
    out = lax.map(chunk, (hf.reshape(nc, PEER_CHUNK, D),
                          idx.reshape(nc, PEER_CHUNK, PEER_HEADS, PEER_TOPK),
                          w.reshape(nc, PEER_CHUNK, PEER_HEADS, PEER_TOPK)))
    return out.reshape(B_, S_, D)


def kernel(x, norm_mix, w_in, fox_f_bias, nsa_cmp_pos, nsa_cmp_w1, nsa_cmp_w2, w_branch, w_out,
           norm_ffn, peer_wq, peer_subkeys, peer_u, peer_v, norm_final):
    B_, S_, D = x.shape
    h = x
    for l in range(norm_mix.shape[0]):
        h = h + hybrid_mixer(rmsnorm(h, norm_mix[l]), w_in[l], fox_f_bias[l], nsa_cmp_pos[l],
                             nsa_cmp_w1[l], nsa_cmp_w2[l], w_branch[l], w_out[l])
        h = h + peer_ffn(rmsnorm(h, norm_ffn[l]), peer_wq[l], peer_subkeys[l], peer_u[l], peer_v[l])
    return rmsnorm_pallas(h.reshape(B_ * S_, D), norm_final).reshape(B_, S_, D)
```

```python
import functools

import jax
import jax.numpy as jnp
import numpy as np
from jax import lax
from jax.experimental import pallas as pl
from jax.experimental.pallas import tpu as pltpu

D_MODEL = 1024
HEAD_DIM = 64
FOX_HEADS = 8
NSA_HEADS = 8
NSA_GROUPS = 2
NSA_HPG = NSA_HEADS // NSA_GROUPS
BRANCH_WIDTH = 512
N_BRANCH = 2
N_NSA_BRANCH = 3
ROPE_DIM = HEAD_DIM // 4
ROPE_THETA = 500000.0
Q_BLOCK = 128
CMP_LEN = 32
CMP_STRIDE = 16
CMP_HIDDEN = 2 * HEAD_DIM
SLC_LEN = 64
SLC_TOP = 16
WINDOW = 512
FORCED_SCORE = 1e9
NEG_INF = -1e30
PEER_HEADS = 8
N_KEYS = 128
N_EXPERTS = N_KEYS * N_KEYS
PEER_QDIM = 256
PEER_TOPK = 16
PEER_CHUNK = 128
RMS_EPS = 1e-6
KV_WIDTH = NSA_GROUPS * HEAD_DIM
IN_SPLITS = (BRANCH_WIDTH, BRANCH_WIDTH, BRANCH_WIDTH, FOX_HEADS, BRANCH_WIDTH,
             KV_WIDTH, KV_WIDTH, KV_WIDTH, KV_WIDTH, KV_WIDTH, KV_WIDTH,
             NSA_HEADS * N_NSA_BRANCH, N_BRANCH * D_MODEL)


def _rmsnorm_kernel(x_ref, g_ref, o_ref):
    x = x_ref[...]
    ms = jnp.mean(x * x, axis=-1, keepdims=True)
    o_ref[...] = x * lax.rsqrt(ms + RMS_EPS) * g_ref[...]


def rmsnorm_pallas(x2d, g, rows=512):
    t, d = x2d.shape
    return pl.pallas_call(
        _rmsnorm_kernel,
        grid=(t // rows,),
        in_specs=[pl.BlockSpec((rows, d), lambda i: (i, 0)),
                  pl.BlockSpec((1, d), lambda i: (0, 0))],
        out_specs=pl.BlockSpec((rows, d), lambda i: (i, 0)),
        out_shape=jax.ShapeDtypeStruct((t, d), jnp.float32),
    )(x2d, g.reshape(1, d))


PEER_SLOTS = PEER_HEADS * PEER_TOPK
PEER_TT = 64
HALF_D = D_MODEL // 2
ROW_SUB = HALF_D // 128
VMEM_LIMIT_PEER = 48 * 1024 * 1024


def pack_table(tab):
    n = tab.shape[0]
    b = lax.bitcast_convert_type(tab.astype(jnp.bfloat16), jnp.uint16).astype(jnp.uint32)
    w = b[:, :HALF_D] | (b[:, HALF_D:] << 16)
    return w.reshape(n, ROW_SUB, 128)


def _unpack(w):
    lo = lax.bitcast_convert_type(w << 16, jnp.float32)
    hi = lax.bitcast_convert_type(w & jnp.uint32(0xFFFF0000), jnp.float32)
    return lo, hi


def _peer_u_kernel(idx_ref, xl_ref, xh_ref, tab_ref, out_ref):
    lane = lax.broadcasted_iota(jnp.int32, (ROW_SUB, 128), 1)

    def tok(t, carry):
        xl = xl_ref[t]
        xh = xh_ref[t]
        acc = jnp.zeros((ROW_SUB, 128), jnp.float32)
        for s in range(PEER_SLOTS):
            lo, hi = _unpack(tab_ref[idx_ref[t, s]])
            part = jnp.sum(lo * xl + hi * xh, axis=1, keepdims=True)
            acc = jnp.where(lane == s, part, acc)
        out_ref[pl.ds(t, 1), :] = jnp.sum(acc, axis=0, keepdims=True)
        return carry

    lax.fori_loop(0, PEER_TT, tok, 0)


def _peer_v_kernel(idx_ref, coef_ref, tab_ref, olo_ref, ohi_ref):
    n_acc = 4

    def tok(t, carry):
        acc_lo = [jnp.zeros((ROW_SUB, 128), jnp.float32) for _ in range(n_acc)]
        acc_hi = [jnp.zeros((ROW_SUB, 128), jnp.float32) for _ in range(n_acc)]
        for s in range(PEER_SLOTS):
            lo, hi = _unpack(tab_ref[idx_ref[t, s]])
            c = coef_ref[t, s]
            acc_lo[s % n_acc] = acc_lo[s % n_acc] + c * lo
            acc_hi[s % n_acc] = acc_hi[s % n_acc] + c * hi
        olo_ref[t] = (acc_lo[0] + acc_lo[1]) + (acc_lo[2] + acc_lo[3])
        ohi_ref[t] = (acc_hi[0] + acc_hi[1]) + (acc_hi[2] + acc_hi[3])
        return carry

    lax.fori_loop(0, PEER_TT, tok, 0)


def _table_spec(n):
    return pl.BlockSpec((n, ROW_SUB, 128), lambda i: (0, 0, 0), pipeline_mode=pl.Buffered(1))


def peer_u(idx, x, tab):
    t = x.shape[0]
    xl = x[:, :HALF_D].reshape(t, ROW_SUB, 128)
    xh = x[:, HALF_D:].reshape(t, ROW_SUB, 128)
    return pl.pallas_call(
        _peer_u_kernel,
        grid=(t // PEER_TT,),
        in_specs=[pl.BlockSpec((PEER_TT, PEER_SLOTS), lambda i: (i, 0), memory_space=pltpu.SMEM),
                  pl.BlockSpec((PEER_TT, ROW_SUB, 128), lambda i: (i, 0, 0)),
                  pl.BlockSpec((PEER_TT, ROW_SUB, 128), lambda i: (i, 0, 0)),
                  _table_spec(tab.shape[0])],
        out_specs=pl.BlockSpec((PEER_TT, PEER_SLOTS), lambda i: (i, 0)),
        out_shape=jax.ShapeDtypeStruct((t, PEER_SLOTS), jnp.float32),
        compiler_params=pltpu.CompilerParams(vmem_limit_bytes=VMEM_LIMIT_PEER),
        name="peer_u",
    )(idx, xl, xh, tab)


def peer_v(idx, coef, tab):
    t = idx.shape[0]
    olo, ohi = pl.pallas_call(
        _peer_v_kernel,
        grid=(t // PEER_TT,),
        in_specs=[pl.BlockSpec((PEER_TT, PEER_SLOTS), lambda i: (i, 0), memory_space=pltpu.SMEM),
                  pl.BlockSpec((PEER_TT, PEER_SLOTS), lambda i: (i, 0), memory_space=pltpu.SMEM),
                  _table_spec(tab.shape[0])],
        out_specs=[pl.BlockSpec((PEER_TT, ROW_SUB, 128), lambda i: (i, 0, 0)),
                   pl.BlockSpec((PEER_TT, ROW_SUB, 128), lambda i: (i, 0, 0))],
        out_shape=[jax.ShapeDtypeStruct((t, ROW_SUB, 128), jnp.float32)] * 2,
        compiler_params=pltpu.CompilerParams(vmem_limit_bytes=VMEM_LIMIT_PEER),
        name="peer_v",
    )(idx, coef, tab)
    return jnp.concatenate([olo.reshape(t, HALF_D), ohi.reshape(t, HALF_D)], axis=1)


def rmsnorm(x, g):
    xf = x.astype(jnp.float32)
    y = xf * lax.rsqrt(jnp.mean(xf * xf, axis=-1, keepdims=True) + RMS_EPS)
    return (y * g.astype(jnp.float32)).astype(x.dtype)


def split_columns(t, sizes):
    offs = np.cumsum((0,) + tuple(sizes))
    return [t[..., int(a):int(b)] for a, b in zip(offs[:-1], offs[1:])]


def rope_tables(pos):
    inv = jnp.power(ROPE_THETA, -jnp.arange(0, ROPE_DIM, 2, dtype=jnp.float32) / ROPE_DIM)
    ang = pos[:, None] * inv[None, :]
    return jnp.cos(ang), jnp.sin(ang)


def apply_partial_rope(x, cos, sin):
    half = ROPE_DIM // 2
    xr = x[..., :ROPE_DIM].astype(jnp.float32)
    x1, x2 = xr[..., :half], xr[..., half:]
    c, s = cos[:, None, :], sin[:, None, :]
    rot = jnp.concatenate([x1 * c - x2 * s, x1 * s + x2 * c], axis=-1)
    return jnp.concatenate([rot.astype(x.dtype), x[..., ROPE_DIM:]], axis=-1)


def fox_attention(q, k, v, log_f):
    B_, S_, H, Dh = q.shape
    scale = Dh ** -0.5
    c = jnp.swapaxes(jnp.cumsum(log_f, axis=1), 1, 2)
    k_pos = jnp.arange(S_)

    def block(i):
        q0 = i * Q_BLOCK
        q_pos = q0 + jnp.arange(Q_BLOCK)
        qb = lax.dynamic_slice_in_dim(q, q0, Q_BLOCK, axis=1)
        cb = lax.dynamic_slice_in_dim(c, q0, Q_BLOCK, axis=2)
        logits = jnp.einsum('bqhd,bkhd->bhqk', qb, k).astype(jnp.float32) * scale
        logits = logits + (cb[..., :, None] - c[..., None, :])
        mask = k_pos[None, :] <= q_pos[:, None]
        p = jax.nn.softmax(jnp.where(mask, logits, NEG_INF), axis=-1)
        return jnp.einsum('bhqk,bkhd->bqhd', p.astype(v.dtype), v).reshape(B_, Q_BLOCK, H * Dh)

    out = lax.map(block, jnp.arange(S_ // Q_BLOCK))
    return jnp.moveaxis(out, 0, 1).reshape(B_, S_, H * Dh)


def compress_kv(k, v, cmp_pos, w1, w2):
    B_, S_ = k.shape[0], k.shape[1]
    n_cmp = (S_ - CMP_LEN) // CMP_STRIDE + 1
    starts = np.arange(n_cmp) * CMP_STRIDE
    idx = starts[:, None] + np.arange(CMP_LEN)[None, :]

    def phi(t, j):
        blocks = t[:, idx] + cmp_pos[j][None, None, :, None, :]
        flat = jnp.swapaxes(blocks, 2, 3).reshape(B_, n_cmp, NSA_GROUPS, CMP_LEN * HEAD_DIM)
        return jax.nn.gelu(flat @ w1[j], approximate=False) @ w2[j]

    return phi(k, 0), phi(v, 1), jnp.asarray(starts + CMP_LEN - 1, dtype=jnp.int32)


def nsa_attention(q, k_cmp, v_cmp, cmp_end, k_slc, v_slc, k_win, v_win, gates):
    B_, S_, H, Dh = q.shape
    G = NSA_GROUPS
    scale = Dh ** -0.5
    n_cmp = k_cmp.shape[1]
    n_slc = S_ // SLC_LEN
    top = min(SLC_TOP, n_slc)
    cs = np.arange(n_cmp)[:, None] * CMP_STRIDE
    ss = np.arange(n_slc)[None, :] * SLC_LEN
    ov = np.clip(np.minimum(cs + CMP_LEN, ss + SLC_LEN) - np.maximum(cs, ss), 0, None) / CMP_LEN
    overlap = jnp.asarray(ov, dtype=jnp.float32)
    kb = k_slc.reshape(B_, n_slc, SLC_LEN, G, Dh).transpose(0, 3, 1, 2, 4)
    vb = v_slc.reshape(B_, n_slc, SLC_LEN, G, Dh).transpose(0, 3, 1, 2, 4)
    pad = ((0, 0), (WINDOW, 0), (0, 0), (0, 0))
    kwp, vwp = jnp.pad(k_win, pad), jnp.pad(v_win, pad)
    bi = jnp.arange(B_)[:, None, None, None]
    gi = jnp.arange(G)[None, :, None, None]
    blk_ids = jnp.arange(n_slc)

    def block(i):
        q0 = i * Q_BLOCK
        q_pos = q0 + jnp.arange(Q_BLOCK)
        qb = lax.dynamic_slice_in_dim(q, q0, Q_BLOCK, axis=1).reshape(B_, Q_BLOCK, G, NSA_HPG, Dh)
        gb = lax.dynamic_slice_in_dim(gates, q0, Q_BLOCK, axis=1).reshape(B_, Q_BLOCK, G, NSA_HPG, N_NSA_BRANCH)
        lc = jnp.einsum('bqghd,bngd->bghqn', qb, k_cmp).astype(jnp.float32) * scale
        mc = cmp_end[None, :] <= q_pos[:, None]
        pc = jax.nn.softmax(jnp.where(mc, lc, NEG_INF), axis=-1) * mc
        o_cmp = jnp.einsum('bghqn,bngd->bqghd', pc.astype(v_cmp.dtype), v_cmp)
        imp = jnp.einsum('bghqn,nj->bgqj', pc, overlap)
        q_blk = q_pos // SLC_LEN
        forced = (blk_ids[None, :] == 0) | (blk_ids[None, :] == q_blk[:, None]) | (blk_ids[None, :] == q_blk[:, None] - 1)
        causal = blk_ids[None, :] <= q_blk[:, None]
        score = jnp.where(forced, FORCED_SCORE, jnp.where(causal, imp, -1.0))
        _, sel = lax.top_k(score, top)
        k_sel = kb[bi, gi, sel]
        v_sel = vb[bi, gi, sel]
        ls = jnp.einsum('bqghd,bgqkld->bghqkl', qb, k_sel).astype(jnp.float32) * scale
        tok = sel[..., None] * SLC_LEN + jnp.arange(SLC_LEN)
        ms = (tok <= q_pos[None, None, :, None, None])[:, :, None]
        ls = jnp.where(ms, ls, NEG_INF).reshape(B_, G, NSA_HPG, Q_BLOCK, top * SLC_LEN)
        ps = jax.nn.softmax(ls, axis=-1).reshape(B_, G, NSA_HPG, Q_BLOCK, top, SLC_LEN)
        o_slc = jnp.einsum('bghqkl,bgqkld->bqghd', ps.astype(v_sel.dtype), v_sel)
        kwb = lax.dynamic_slice_in_dim(kwp, q0, WINDOW + Q_BLOCK, axis=1)
        vwb = lax.dynamic_slice_in_dim(vwp, q0, WINDOW + Q_BLOCK, axis=1)
        k_pos = q0 - WINDOW + jnp.arange(WINDOW + Q_BLOCK)
        lw = jnp.einsum('bqghd,bkgd->bghqk', qb, kwb).astype(jnp.float32) * scale
        rel = q_pos[:, None] - k_pos[None, :]
        mw = (rel >= 0) & (rel < WINDOW) & (k_pos[None, :] >= 0)
        pw = jax.nn.softmax(jnp.where(mw, lw, NEG_INF), axis=-1)
        o_win = jnp.einsum('bghqk,bkgd->bqghd', pw.astype(vwb.dtype), vwb)
        o = gb[..., 0:1] * o_cmp + gb[..., 1:2] * o_slc + gb[..., 2:3] * o_win
        return o.astype(q.dtype).reshape(B_, Q_BLOCK, H * Dh)

    out = lax.map(block, jnp.arange(S_ // Q_BLOCK))
    return jnp.moveaxis(out, 0, 1).reshape(B_, S_, H * Dh)


def hybrid_mixer(h, w_in, fox_f_bias, cmp_pos, cmp_w1, cmp_w2, w_branch, w_out):
    B_, S_, _ = h.shape
    proj = h @ w_in
    (fq, fk, fv, f_logit, nq, kc, vc, ksl, vsl, kwn, vwn,
     nsa_gate_logit, merge_logit) = split_columns(proj, IN_SPLITS)
    heads = lambda t, n: t.reshape(B_, S_, n, HEAD_DIM)
    log_f = jax.nn.log_sigmoid(f_logit.astype(jnp.float32) + fox_f_bias.astype(jnp.float32))
    y_fox = fox_attention(heads(fq, FOX_HEADS), heads(fk, FOX_HEADS), heads(fv, FOX_HEADS), log_f)
    cos, sin = rope_tables(jnp.arange(S_, dtype=jnp.float32))
    q_nsa = apply_partial_rope(heads(nq, NSA_HEADS), cos, sin)
    k_slc = apply_partial_rope(heads(ksl, NSA_GROUPS), cos, sin)
    k_win = apply_partial_rope(heads(kwn, NSA_GROUPS), cos, sin)
    k_cmp, v_cmp, cmp_end = compress_kv(heads(kc, NSA_GROUPS), heads(vc, NSA_GROUPS), cmp_pos, cmp_w1, cmp_w2)
    k_cmp = apply_partial_rope(k_cmp, *rope_tables(cmp_end.astype(jnp.float32)))
    gates = jax.nn.sigmoid(nsa_gate_logit.astype(jnp.float32)).reshape(B_, S_, NSA_HEADS, N_NSA_BRANCH)
    y_nsa = nsa_attention(q_nsa, k_cmp, v_cmp, cmp_end, k_slc, heads(vsl, NSA_GROUPS),
                          k_win, heads(vwn, NSA_GROUPS), gates)
    g = jax.nn.sigmoid(merge_logit.astype(jnp.float32)).reshape(B_, S_, N_BRANCH, D_MODEL).astype(h.dtype)
    ys = jnp.stack([y_fox, y_nsa], axis=2)
    up = jnp.einsum('bsnc,ncd->bsnd', ys, w_branch)
    merged = jnp.sum(g * up, axis=2)
    return merged @ w_out


def peer_ffn(h, wq, subkeys, u, v):
    B_, S_, D = h.shape
    T = B_ * S_
    hf = h.reshape(T, D)
    q = (hf @ wq).reshape(T, PEER_HEADS, 2, PEER_QDIM // 2)
    s = jnp.einsum('thpd,hpnd->thpn', q, subkeys).astype(jnp.float32)
    s1, i1 = lax.top_k(s[:, :, 0], PEER_TOPK)
    s2, i2 = lax.top_k(s[:, :, 1], PEER_TOPK)
    cand = (s1[..., :, None] + s2[..., None, :]).reshape(T, PEER_HEADS, PEER_TOPK * PEER_TOPK)
    cand_idx = (i1[..., :, None] * N_KEYS + i2[..., None, :]).reshape(T, PEER_HEADS, PEER_TOPK * PEER_TOPK)
    top_s, pos = lax.top_k(cand, PEER_TOPK)
    idx = jnp.take_along_axis(cand_idx, pos, axis=-1)
    w = jax.nn.softmax(top_s, axis=-1)
    idx2 = idx.reshape(T, PEER_SLOTS).astype(jnp.int32)
    act = jax.nn.gelu(peer_u(idx2, hf, pack_table(u)), approximate=False)
    coef = w.reshape(T, PEER_SLOTS) * act
    out = peer_v(idx2, coef, pack_table(v))
    return out.reshape(B_, S_, D)


def kernel(x, norm_mix, w_in, fox_f_bias, nsa_cmp_pos, nsa_cmp_w1, nsa_cmp_w2, w_branch, w_out,
           norm_ffn, peer_wq, peer_subkeys, peer_u, peer_v, norm_final):
    B_, S_, D = x.shape
    h = x
    for l in range(norm_mix.shape[0]):
        h = h + hybrid_mixer(rmsnorm(h, norm_mix[l]), w_in[l], fox_f_bias[l], nsa_cmp_pos[l],
                             nsa_cmp_w1[l], nsa_cmp_w2[l], w_branch[l], w_out[l])
        h = h + peer_ffn(rmsnorm(h, norm_ffn[l]), peer_wq[l], peer_subkeys[l], peer_u[l], peer_v[l])
    return rmsnorm_pallas(h.reshape(B_ * S_, D), norm_final).reshape(B_, S_, D)
```

```python
import functools

import jax
import jax.numpy as jnp
import numpy as np
from jax import lax
from jax.experimental import pallas as pl
from jax.experimental.pallas import tpu as pltpu

D_MODEL = 1024
HEAD_DIM = 64
FOX_HEADS = 8
NSA_HEADS = 8
NSA_GROUPS = 2
NSA_HPG = NSA_HEADS // NSA_GROUPS
BRANCH_WIDTH = 512
N_BRANCH = 2
N_NSA_BRANCH = 3
ROPE_DIM = HEAD_DIM // 4
ROPE_THETA = 500000.0
Q_BLOCK = 128
CMP_LEN = 32
CMP_STRIDE = 16
CMP_HIDDEN = 2 * HEAD_DIM
SLC_LEN = 64
SLC_TOP = 16
WINDOW = 512
FORCED_SCORE = 1e9
NEG_INF = -1e30
PEER_HEADS = 8
N_KEYS = 128
N_EXPERTS = N_KEYS * N_KEYS
PEER_QDIM = 256
PEER_TOPK = 16
PEER_CHUNK = 128
RMS_EPS = 1e-6
KV_WIDTH = NSA_GROUPS * HEAD_DIM
IN_SPLITS = (BRANCH_WIDTH, BRANCH_WIDTH, BRANCH_WIDTH, FOX_HEADS, BRANCH_WIDTH,
             KV_WIDTH, KV_WIDTH, KV_WIDTH, KV_WIDTH, KV_WIDTH, KV_WIDTH,
             NSA_HEADS * N_NSA_BRANCH, N_BRANCH * D_MODEL)


def _rmsnorm_kernel(x_ref, g_ref, o_ref):
    x = x_ref[...]
    ms = jnp.mean(x * x, axis=-1, keepdims=True)
    o_ref[...] = x * lax.rsqrt(ms + RMS_EPS) * g_ref[...]


def rmsnorm_pallas(x2d, g, rows=512):
    t, d = x2d.shape
    return pl.pallas_call(
        _rmsnorm_kernel,
        grid=(t // rows,),
        in_specs=[pl.BlockSpec((rows, d), lambda i: (i, 0)),
                  pl.BlockSpec((1, d), lambda i: (0, 0))],
        out_specs=pl.BlockSpec((rows, d), lambda i: (i, 0)),
        out_shape=jax.ShapeDtypeStruct((t, d), jnp.float32),
    )(x2d, g.reshape(1, d))


PEER_SLOTS = PEER_HEADS * PEER_TOPK
PEER_TT = 64
HALF_D = D_MODEL // 2
ROW_SUB = HALF_D // 128
VMEM_LIMIT_PEER = 48 * 1024 * 1024


def pack_table(tab):
    n = tab.shape[0]
    b = lax.bitcast_convert_type(tab.astype(jnp.bfloat16), jnp.uint16).astype(jnp.uint32)
    w = b[:, :HALF_D] | (b[:, HALF_D:] << 16)
    return w.reshape(n, ROW_SUB, 128)


def _unpack(w):
    lo = lax.bitcast_convert_type(w << 16, jnp.float32)
    hi = lax.bitcast_convert_type(w & jnp.uint32(0xFFFF0000), jnp.float32)
    return lo, hi


def _peer_u_kernel(idx_ref, xl_ref, xh_ref, tab_ref, out_ref):
    lane = lax.broadcasted_iota(jnp.int32, (ROW_SUB, 128), 1)

    def tok(t, carry):
        xl = xl_ref[t]
        xh = xh_ref[t]
        acc = jnp.zeros((ROW_SUB, 128), jnp.float32)
        for s in range(PEER_SLOTS):
            lo, hi = _unpack(tab_ref[idx_ref[t, s]])
            part = jnp.sum(lo * xl + hi * xh, axis=1, keepdims=True)
            acc = jnp.where(lane == s, part, acc)
        out_ref[pl.ds(t, 1), :] = jnp.sum(acc, axis=0, keepdims=True)
        return carry

    lax.fori_loop(0, PEER_TT, tok, 0)


def _peer_v_kernel(idx_ref, coef_ref, tab_ref, olo_ref, ohi_ref):
    n_acc = 4

    def tok(t, carry):
        acc_lo = [jnp.zeros((ROW_SUB, 128), jnp.float32) for _ in range(n_acc)]
        acc_hi = [jnp.zeros((ROW_SUB, 128), jnp.float32) for _ in range(n_acc)]
        for s in range(PEER_SLOTS):
            lo, hi = _unpack(tab_ref[idx_ref[t, s]])
            c = coef_ref[t, s]
            acc_lo[s % n_acc] = acc_lo[s % n_acc] + c * lo
            acc_hi[s % n_acc] = acc_hi[s % n_acc] + c * hi
        olo_ref[t] = (acc_lo[0] + acc_lo[1]) + (acc_lo[2] + acc_lo[3])
        ohi_ref[t] = (acc_hi[0] + acc_hi[1]) + (acc_hi[2] + acc_hi[3])
        return carry

    lax.fori_loop(0, PEER_TT, tok, 0)


def _table_spec(n):
    return pl.BlockSpec((n, ROW_SUB, 128), lambda i: (0, 0, 0), pipeline_mode=pl.Buffered(1))


def peer_u(idx, x, tab):
    t = x.shape[0]
    xl = x[:, :HALF_D].reshape(t, ROW_SUB, 128)
    xh = x[:, HALF_D:].reshape(t, ROW_SUB, 128)
    return pl.pallas_call(
        _peer_u_kernel,
        grid=(t // PEER_TT,),
        in_specs=[pl.BlockSpec((PEER_TT, PEER_SLOTS), lambda i: (i, 0), memory_space=pltpu.SMEM),
                  pl.BlockSpec((PEER_TT, ROW_SUB, 128), lambda i: (i, 0, 0)),
                  pl.BlockSpec((PEER_TT, ROW_SUB, 128), lambda i: (i, 0, 0)),
                  _table_spec(tab.shape[0])],
        out_specs=pl.BlockSpec((PEER_TT, PEER_SLOTS), lambda i: (i, 0)),
        out_shape=jax.ShapeDtypeStruct((t, PEER_SLOTS), jnp.float32),
        compiler_params=pltpu.CompilerParams(vmem_limit_bytes=VMEM_LIMIT_PEER),
        name="peer_u",
    )(idx, xl, xh, tab)


def peer_v(idx, coef, tab):
    t = idx.shape[0]
    olo, ohi = pl.pallas_call(
        _peer_v_kernel,
        grid=(t // PEER_TT,),
        in_specs=[pl.BlockSpec((PEER_TT, PEER_SLOTS), lambda i: (i, 0), memory_space=pltpu.SMEM),
                  pl.BlockSpec((PEER_TT, PEER_SLOTS), lambda i: (i, 0), memory_space=pltpu.SMEM),
                  _table_spec(tab.shape[0])],
        out_specs=[pl.BlockSpec((PEER_TT, ROW_SUB, 128), lambda i: (i, 0, 0)),
                   pl.BlockSpec((PEER_TT, ROW_SUB, 128), lambda i: (i, 0, 0))],
        out_shape=[jax.ShapeDtypeStruct((t, ROW_SUB, 128), jnp.float32)] * 2,
        compiler_params=pltpu.CompilerParams(vmem_limit_bytes=VMEM_LIMIT_PEER),
        name="peer_v",
    )(idx, coef, tab)
    return jnp.concatenate([olo.reshape(t, HALF_D), ohi.reshape(t, HALF_D)], axis=1)


def rmsnorm(x, g):
    xf = x.astype(jnp.float32)
    y = xf * lax.rsqrt(jnp.mean(xf * xf, axis=-1, keepdims=True) + RMS_EPS)
    return (y * g.astype(jnp.float32)).astype(x.dtype)


def split_columns(t, sizes):
    offs = np.cumsum((0,) + tuple(sizes))
    return [t[..., int(a):int(b)] for a, b in zip(offs[:-1], offs[1:])]


def rope_tables(pos):
    inv = jnp.power(ROPE_THETA, -jnp.arange(0, ROPE_DIM, 2, dtype=jnp.float32) / ROPE_DIM)
    ang = pos[:, None] * inv[None, :]
    return jnp.cos(ang), jnp.sin(ang)


def apply_partial_rope(x, cos, sin):
    half = ROPE_DIM // 2
    xr = x[..., :ROPE_DIM].astype(jnp.float32)
    x1, x2 = xr[..., :half], xr[..., half:]
    c, s = cos[:, None, :], sin[:, None, :]
    rot = jnp.concatenate([x1 * c - x2 * s, x1 * s + x2 * c], axis=-1)
    return jnp.concatenate([rot.astype(x.dtype), x[..., ROPE_DIM:]], axis=-1)


KV_TILE = 128
BF16 = jnp.bfloat16
F32 = jnp.float32


def _flash_step(k, vt, qt, mask, carry):
    m, l, acc = carry
    s = jnp.dot(k, qt, preferred_element_type=F32)
    m_new = jnp.maximum(m, jnp.max(jnp.where(mask, s, NEG_INF), axis=0, keepdims=True))
    alpha = jnp.exp(m - m_new)
    p = jnp.where(mask, jnp.exp(s - m_new), 0.0)
    l = alpha * l + jnp.sum(p, axis=0, keepdims=True)
    acc = alpha * acc + jnp.dot(vt, p.astype(BF16), preferred_element_type=F32)
    return m_new, l, acc


def _flash_init(n):
    return (jnp.full((1, n), NEG_INF, F32), jnp.zeros((1, n), F32), jnp.zeros((HEAD_DIM, n), F32))


FOX_TQ = 256
FOX_DK = 256


def _fox_kernel(qt_ref, k_ref, vt_ref, o_ref):
    qi = pl.program_id(1)
    qt = qt_ref[...]
    qpos = qi * FOX_TQ + lax.broadcasted_iota(jnp.int32, (1, FOX_TQ), 1)
    krow = lax.broadcasted_iota(jnp.int32, (KV_TILE, FOX_TQ), 0)

    def step(kt, carry):
        mask = (kt * KV_TILE + krow) <= qpos
        return _flash_step(k_ref[kt], vt_ref[kt], qt, mask, carry)

    _, l, acc = lax.fori_loop(0, (qi + 1) * (FOX_TQ // KV_TILE), step, _flash_init(FOX_TQ))
    o_ref[...] = acc / l


def _trunc_bf16(x):
    bits = lax.bitcast_convert_type(x, jnp.uint32) & jnp.uint32(0xFFFF0000)
    return lax.bitcast_convert_type(bits, F32)


def _split3(c):
    c1 = _trunc_bf16(c)
    r = c - c1
    c2 = _trunc_bf16(r)
    return c1.astype(BF16), c2.astype(BF16), (r - c2).astype(BF16)


def _split2(x):
    hi = _trunc_bf16(x)
    return hi.astype(BF16), (x - hi).astype(BF16)


def _aug_q(q):
    hi, lo = _split2(q)
    return jnp.concatenate([hi, hi, lo], axis=-1)


def _aug_k(k):
    hi, lo = _split2(k)
    return jnp.concatenate([hi, lo, hi], axis=-1)


def fox_attention(q, k, v, log_f):
    B_, S_, H, Dh = q.shape
    n_qt, n_kt = S_ // FOX_TQ, S_ // KV_TILE
    c = jnp.cumsum(log_f, axis=1)
    c1, c2, c3 = _split3(c)
    one = jnp.ones_like(c1)
    pad = jnp.zeros((B_, S_, H, FOX_DK - 3 * Dh - 6), BF16)
    col = lambda t: t[..., None]
    qa = jnp.concatenate([_aug_q(q * (Dh ** -0.5)), col(one), col(one), col(one),
                          col(c1), col(c2), col(c3), pad], axis=-1)
    ka = jnp.concatenate([_aug_k(k), col(-c1), col(-c2), col(-c3),
                          col(one), col(one), col(one), pad], axis=-1)
    qt = qa.reshape(B_, n_qt, FOX_TQ, H, FOX_DK).transpose(0, 3, 1, 4, 2).reshape(B_ * H * n_qt, FOX_DK, FOX_TQ)
    kk = ka.reshape(B_, n_kt, KV_TILE, H, FOX_DK).transpose(0, 3, 1, 2, 4).reshape(B_ * H, n_kt, KV_TILE, FOX_DK)
    vt = v.astype(BF16).reshape(B_, n_kt, KV_TILE, H, Dh).transpose(0, 3, 1, 4, 2).reshape(B_ * H, n_kt, Dh, KV_TILE)
    out = pl.pallas_call(
        _fox_kernel,
        grid=(B_ * H, n_qt),
        in_specs=[pl.BlockSpec((None, FOX_DK, FOX_TQ), lambda b, i: (b * n_qt + i, 0, 0)),
                  pl.BlockSpec((None, n_kt, KV_TILE, FOX_DK), lambda b, i: (b, 0, 0, 0)),
                  pl.BlockSpec((None, n_kt, Dh, KV_TILE), lambda b, i: (b, 0, 0, 0))],
        out_specs=pl.BlockSpec((None, Dh, FOX_TQ), lambda b, i: (b * n_qt + i, 0, 0)),
        out_shape=jax.ShapeDtypeStruct((B_ * H * n_qt, Dh, FOX_TQ), F32),
        name="fox_attention",
    )(qt, kk, vt)
    return out.reshape(B_, H, n_qt, Dh, FOX_TQ).transpose(0, 2, 4, 1, 3).reshape(B_, S_, H * Dh)


def compress_kv(k, v, cmp_pos, w1, w2):
    B_, S_ = k.shape[0], k.shape[1]
    n_cmp = (S_ - CMP_LEN) // CMP_STRIDE + 1
    starts = np.arange(n_cmp) * CMP_STRIDE
    idx = starts[:, None] + np.arange(CMP_LEN)[None, :]

    def phi(t, j):
        blocks = t[:, idx] + cmp_pos[j][None, None, :, None, :]
        flat = jnp.swapaxes(blocks, 2, 3).reshape(B_, n_cmp, NSA_GROUPS, CMP_LEN * HEAD_DIM)
        return jax.nn.gelu(flat @ w1[j], approximate=False) @ w2[j]

    return phi(k, 0), phi(v, 1), jnp.asarray(starts + CMP_LEN - 1, dtype=jnp.int32)


NSA_TQ = 128
NSA_N = NSA_HPG * NSA_TQ
N_CMP_PAD = 256
N_SLC = 64
NSA_DK = 3 * HEAD_DIM


def _nsa_kernel(qt_ref, kc_ref, vct_ref, ks_ref, vst_ref, kw_ref, vwt_ref, gate_ref, ovl_ref, o_ref, sel_ref):
    qi = pl.program_id(1)
    q0 = qi * NSA_TQ
    qt = qt_ref[...]
    lane = lax.broadcasted_iota(jnp.int32, (1, NSA_N), 1)
    qpos = q0 + (lane & (NSA_TQ - 1))

    s = jnp.dot(kc_ref[...], qt, preferred_element_type=F32)
    cmp_end = lax.broadcasted_iota(jnp.int32, (N_CMP_PAD, NSA_N), 0) * CMP_STRIDE + (CMP_LEN - 1)
    mc = cmp_end <= qpos
    m = jnp.max(jnp.where(mc, s, NEG_INF), axis=0, keepdims=True)
    p = jnp.where(mc, jnp.exp(s - m), 0.0)
    l = jnp.sum(p, axis=0, keepdims=True)
    pc = p * jnp.where(l > 0.0, 1.0 / l, 0.0)
    o_cmp = jnp.dot(vct_ref[...], pc.astype(BF16), preferred_element_type=F32)

    pcs = (pc[:, 0:NSA_TQ] + pc[:, NSA_TQ:2 * NSA_TQ]) + (pc[:, 2 * NSA_TQ:3 * NSA_TQ] + pc[:, 3 * NSA_TQ:4 * NSA_TQ])
    pcs_hi = pcs.astype(BF16)
    pcs_lo = (pcs - pcs_hi.astype(F32)).astype(BF16)
    ovl = ovl_ref[...]
    imp = jnp.dot(ovl, pcs_hi, preferred_element_type=F32) + jnp.dot(ovl, pcs_lo, preferred_element_type=F32)
    blk = lax.broadcasted_iota(jnp.int32, (N_SLC, NSA_TQ), 0)
    q_blk = (q0 + lax.broadcasted_iota(jnp.int32, (N_SLC, NSA_TQ), 1)) >> 6
    forced = (blk == 0) | (blk == q_blk) | (blk == q_blk - 1)
    score = jnp.where(forced, FORCED_SCORE, jnp.where(blk <= q_blk, imp, -1.0))
    rank = jnp.zeros((N_SLC, NSA_TQ), jnp.int32)
    for i in range(N_SLC):
        row = score[i:i + 1, :]
        beats = (row > score) | ((row == score) & (blk > i))
        rank = rank + beats.astype(jnp.int32)
    sel = (rank < SLC_TOP).astype(F32)
    sel_ref[...] = jnp.concatenate([sel] * NSA_HPG, axis=1)

    krow = lax.broadcasted_iota(jnp.int32, (KV_TILE, NSA_N), 0)
    half = KV_TILE // 2

    def slc_step(kt, carry):
        sel_a = jnp.broadcast_to(sel_ref[pl.ds(2 * kt, 1), :], (half, NSA_N))
        sel_b = jnp.broadcast_to(sel_ref[pl.ds(2 * kt + 1, 1), :], (half, NSA_N))
        picked = jnp.concatenate([sel_a, sel_b], axis=0) > 0.0
        mask = picked & ((kt * KV_TILE + krow) <= qpos)
        return _flash_step(ks_ref[kt], vst_ref[kt], qt, mask, carry)

    _, l_s, acc_s = lax.fori_loop(0, qi + 1, slc_step, _flash_init(NSA_N))

    def win_step(kt, carry):
        rel = qpos - (kt * KV_TILE + krow)
        mask = (rel >= 0) & (rel < WINDOW)
        return _flash_step(kw_ref[kt], vwt_ref[kt], qt, mask, carry)

    _, l_w, acc_w = lax.fori_loop(jnp.maximum(qi - WINDOW // KV_TILE, 0), qi + 1, win_step, _flash_init(NSA_N))

    g = jax.nn.sigmoid(gate_ref[...])
    o_ref[...] = g[0:1] * o_cmp + g[1:2] * (acc_s / l_s) + g[2:3] * (acc_w / l_w)


def nsa_attention(q, k_cmp, v_cmp, k_slc, v_slc, k_win, v_win, gate_logit):
    B_, S_, H, Dh = q.shape
    G = NSA_GROUPS
    n_qt, n_kt = S_ // NSA_TQ, S_ // KV_TILE
    n_cmp = k_cmp.shape[1]
    cs = np.arange(N_CMP_PAD)[None, :] * CMP_STRIDE
    ss = np.arange(N_SLC)[:, None] * SLC_LEN
    ov = np.clip(np.minimum(cs + CMP_LEN, ss + SLC_LEN) - np.maximum(cs, ss), 0, None) / CMP_LEN
    ov[:, n_cmp:] = 0.0
    ovl = jnp.asarray(ov, dtype=BF16)
    qs = _aug_q(q * (Dh ** -0.5)).reshape(B_, n_qt, NSA_TQ, G, NSA_HPG, NSA_DK)
    qt = qs.transpose(0, 3, 1, 5, 4, 2).reshape(B_ * G * n_qt, NSA_DK, NSA_N)
    gt = gate_logit.astype(F32).reshape(B_, n_qt, NSA_TQ, G, NSA_HPG, N_NSA_BRANCH)
    gt = gt.transpose(0, 3, 1, 5, 4, 2).reshape(B_ * G * n_qt, N_NSA_BRANCH, NSA_N)
    keys = lambda t: _aug_k(t).reshape(B_, n_kt, KV_TILE, G, NSA_DK).transpose(0, 3, 1, 2, 4).reshape(B_ * G, n_kt, KV_TILE, NSA_DK)
    vals = lambda t: t.astype(BF16).reshape(B_, n_kt, KV_TILE, G, Dh).transpose(0, 3, 1, 4, 2).reshape(B_ * G, n_kt, Dh, KV_TILE)
    padc = ((0, 0), (0, N_CMP_PAD - n_cmp), (0, 0), (0, 0))
    kc = _aug_k(jnp.pad(k_cmp, padc)).transpose(0, 2, 1, 3).reshape(B_ * G, N_CMP_PAD, NSA_DK)
    vct = jnp.pad(v_cmp, padc).astype(BF16).transpose(0, 2, 3, 1).reshape(B_ * G, Dh, N_CMP_PAD)
    per_bg = lambda *blk: pl.BlockSpec((None,) + blk, lambda b, i: (b,) + (0,) * len(blk))
    per_tile = lambda *blk: pl.BlockSpec((None,) + blk, lambda b, i: (b * n_qt + i,) + (0,) * len(blk))
    out = pl.pallas_call(
        _nsa_kernel,
        grid=(B_ * G, n_qt),
        in_specs=[per_tile(NSA_DK, NSA_N),
                  per_bg(N_CMP_PAD, NSA_DK), per_bg(Dh, N_CMP_PAD),
                  per_bg(n_kt, KV_TILE, NSA_DK), per_bg(n_kt, Dh, KV_TILE),
                  per_bg(n_kt, KV_TILE, NSA_DK), per_bg(n_kt, Dh, KV_TILE),
                  per_tile(N_NSA_BRANCH, NSA_N),
                  pl.BlockSpec((N_SLC, N_CMP_PAD), lambda b, i: (0, 0))],
        out_specs=per_tile(Dh, NSA_N),
        out_shape=jax.ShapeDtypeStruct((B_ * G * n_qt, Dh, NSA_N), F32),
        scratch_shapes=[pltpu.VMEM((N_SLC, NSA_N), F32)],
        name="nsa_attention",
    )(qt, kc, vct, keys(k_slc), vals(v_slc), keys(k_win), vals(v_win), gt, ovl)
    return out.reshape(B_, G, n_qt, Dh, NSA_HPG, NSA_TQ).transpose(0, 2, 5, 1, 4, 3).reshape(B_, S_, H * Dh)


def hybrid_mixer(h, w_in, fox_f_bias, cmp_pos, cmp_w1, cmp_w2, w_branch, w_out):
    B_, S_, _ = h.shape
    proj = h @ w_in
    (fq, fk, fv, f_logit, nq, kc, vc, ksl, vsl, kwn, vwn,
     nsa_gate_logit, merge_logit) = split_columns(proj, IN_SPLITS)
    heads = lambda t, n: t.reshape(B_, S_, n, HEAD_DIM)
    log_f = jax.nn.log_sigmoid(f_logit.astype(jnp.float32) + fox_f_bias.astype(jnp.float32))
    y_fox = fox_attention(heads(fq, FOX_HEADS), heads(fk, FOX_HEADS), heads(fv, FOX_HEADS), log_f)
    cos, sin = rope_tables(jnp.arange(S_, dtype=jnp.float32))
    q_nsa = apply_partial_rope(heads(nq, NSA_HEADS), cos, sin)
    k_slc = apply_partial_rope(heads(ksl, NSA_GROUPS), cos, sin)
    k_win = apply_partial_rope(heads(kwn, NSA_GROUPS), cos, sin)
    k_cmp, v_cmp, cmp_end = compress_kv(heads(kc, NSA_GROUPS), heads(vc, NSA_GROUPS), cmp_pos, cmp_w1, cmp_w2)
    k_cmp = apply_partial_rope(k_cmp, *rope_tables(cmp_end.astype(jnp.float32)))
    y_nsa = nsa_attention(q_nsa, k_cmp, v_cmp, k_slc, heads(vsl, NSA_GROUPS), k_win, heads(vwn, NSA_GROUPS),
                          nsa_gate_logit.reshape(B_, S_, NSA_HEADS, N_NSA_BRANCH))
    g = jax.nn.sigmoid(merge_logit.astype(jnp.float32)).reshape(B_, S_, N_BRANCH, D_MODEL).astype(h.dtype)
    ys = jnp.stack([y_fox, y_nsa], axis=2)
    up = jnp.einsum('bsnc,ncd->bsnd', ys, w_branch)
    merged = jnp.sum(g * up, axis=2)
    return merged @ w_out


def peer_ffn(h, wq, subkeys, u, v):
    B_, S_, D = h.shape
    T = B_ * S_
    hf = h.reshape(T, D)
    q = (hf @ wq).reshape(T, PEER_HEADS, 2, PEER_QDIM // 2)
    s = jnp.einsum('thpd,hpnd->thpn', q, subkeys).astype(jnp.float32)
    s1, i1 = lax.top_k(s[:, :, 0], PEER_TOPK)
    s2, i2 = lax.top_k(s[:, :, 1], PEER_TOPK)
    cand = (s1[..., :, None] + s2[..., None, :]).reshape(T, PEER_HEADS, PEER_TOPK * PEER_TOPK)
    cand_idx = (i1[..., :, None] * N_KEYS + i2[..., None, :]).reshape(T, PEER_HEADS, PEER_TOPK * PEER_TOPK)
    top_s, pos = lax.top_k(cand, PEER_TOPK)
    idx = jnp.take_along_axis(cand_idx, pos, axis=-1)
    w = jax.nn.softmax(top_s, axis=-1)
    idx2 = idx.reshape(T, PEER_SLOTS).astype(jnp.int32)
    act = jax.nn.gelu(peer_u(idx2, hf, pack_table(u)), approximate=False)
    coef = w.reshape(T, PEER_SLOTS) * act
    out = peer_v(idx2, coef, pack_table(v))
    return out.reshape(B_, S_, D)


def kernel(x, norm_mix, w_in, fox_f_bias, nsa_cmp_pos, nsa_cmp_w1, nsa_cmp_w2, w_branch, w_out,
           norm_ffn, peer_wq, peer_subkeys, peer_u, peer_v, norm_final):
    B_, S_, D = x.shape
    h = x
    for l in range(norm_mix.shape[0]):
        h = h + hybrid_mixer(rmsnorm(h, norm_mix[l]), w_in[l], fox_f_bias[l], nsa_cmp_pos[l],
                             nsa_cmp_w1[l], nsa_cmp_w2[l], w_branch[l], w_out[l])
        h = h + peer_ffn(rmsnorm(h, norm_ffn[l]), peer_wq[l], peer_subkeys[l], peer_u[l], peer_v[l])
    return rmsnorm_pallas(h.reshape(B_ * S_, D), norm_final).reshape(B_, S_, D)
```

```python
import functools

import jax
import jax.numpy as jnp
import numpy as np
from jax import lax
from jax.experimental import pallas as pl
from jax.experimental.pallas import tpu as pltpu

D_MODEL = 1024
HEAD_DIM = 64
FOX_HEADS = 8
NSA_HEADS = 8
NSA_GROUPS = 2
NSA_HPG = NSA_HEADS // NSA_GROUPS
BRANCH_WIDTH = 512
N_BRANCH = 2
N_NSA_BRANCH = 3
ROPE_DIM = HEAD_DIM // 4
ROPE_THETA = 500000.0
Q_BLOCK = 128
CMP_LEN = 32
CMP_STRIDE = 16
CMP_HIDDEN = 2 * HEAD_DIM
SLC_LEN = 64
SLC_TOP = 16
WINDOW = 512
FORCED_SCORE = 1e9
NEG_INF = -1e30
PEER_HEADS = 8
N_KEYS = 128
N_EXPERTS = N_KEYS * N_KEYS
PEER_QDIM = 256
PEER_TOPK = 16
PEER_CHUNK = 128
RMS_EPS = 1e-6
KV_WIDTH = NSA_GROUPS * HEAD_DIM
IN_SPLITS = (BRANCH_WIDTH, BRANCH_WIDTH, BRANCH_WIDTH, FOX_HEADS, BRANCH_WIDTH,
             KV_WIDTH, KV_WIDTH, KV_WIDTH, KV_WIDTH, KV_WIDTH, KV_WIDTH,
             NSA_HEADS * N_NSA_BRANCH, N_BRANCH * D_MODEL)


def _rmsnorm_kernel(x_ref, g_ref, o_ref):
    x = x_ref[...]
    ms = jnp.mean(x * x, axis=-1, keepdims=True)
    o_ref[...] = x * lax.rsqrt(ms + RMS_EPS) * g_ref[...]


def rmsnorm_pallas(x2d, g, rows=512):
    t, d = x2d.shape
    return pl.pallas_call(
        _rmsnorm_kernel,
        grid=(t // rows,),
        in_specs=[pl.BlockSpec((rows, d), lambda i: (i, 0)),
                  pl.BlockSpec((1, d), lambda i: (0, 0))],
        out_specs=pl.BlockSpec((rows, d), lambda i: (i, 0)),
        out_shape=jax.ShapeDtypeStruct((t, d), jnp.float32),
    )(x2d, g.reshape(1, d))


PEER_SLOTS = PEER_HEADS * PEER_TOPK
PEER_TT = 64
HALF_D = D_MODEL // 2
ROW_SUB = HALF_D // 128
VMEM_LIMIT_PEER = 48 * 1024 * 1024


def pack_table(tab):
    n = tab.shape[0]
    b = lax.bitcast_convert_type(tab.astype(jnp.bfloat16), jnp.uint16).astype(jnp.uint32)
    w = b[:, :HALF_D] | (b[:, HALF_D:] << 16)
    return w.reshape(n, ROW_SUB, 128)


def _unpack(w):
    lo = lax.bitcast_convert_type(w << 16, jnp.float32)
    hi = lax.bitcast_convert_type(w & jnp.uint32(0xFFFF0000), jnp.float32)
    return lo, hi


def _peer_u_kernel(idx_ref, xl_ref, xh_ref, tab_ref, out_ref):
    lane = lax.broadcasted_iota(jnp.int32, (ROW_SUB, 128), 1)

    def tok(t, carry):
        xl = xl_ref[t]
        xh = xh_ref[t]
        acc = jnp.zeros((ROW_SUB, 128), jnp.float32)
        for s in range(PEER_SLOTS):
            lo, hi = _unpack(tab_ref[idx_ref[t, s]])
            part = jnp.sum(lo * xl + hi * xh, axis=1, keepdims=True)
            acc = jnp.where(lane == s, part, acc)
        out_ref[pl.ds(t, 1), :] = jnp.sum(acc, axis=0, keepdims=True)
        return carry

    lax.fori_loop(0, PEER_TT, tok, 0)


def _peer_v_kernel(idx_ref, coef_ref, tab_ref, olo_ref, ohi_ref):
    n_acc = 4

    def tok(t, carry):
        acc_lo = [jnp.zeros((ROW_SUB, 128), jnp.float32) for _ in range(n_acc)]
        acc_hi = [jnp.zeros((ROW_SUB, 128), jnp.float32) for _ in range(n_acc)]
        for s in range(PEER_SLOTS):
            lo, hi = _unpack(tab_ref[idx_ref[t, s]])
            c = coef_ref[t, s]
            acc_lo[s % n_acc] = acc_lo[s % n_acc] + c * lo
            acc_hi[s % n_acc] = acc_hi[s % n_acc] + c * hi
        olo_ref[t] = (acc_lo[0] + acc_lo[1]) + (acc_lo[2] + acc_lo[3])
        ohi_ref[t] = (acc_hi[0] + acc_hi[1]) + (acc_hi[2] + acc_hi[3])
        return carry

    lax.fori_loop(0, PEER_TT, tok, 0)


def _table_spec(n):
    return pl.BlockSpec((n, ROW_SUB, 128), lambda i: (0, 0, 0), pipeline_mode=pl.Buffered(1))


def peer_u(idx, x, tab):
    t = x.shape[0]
    xl = x[:, :HALF_D].reshape(t, ROW_SUB, 128)
    xh = x[:, HALF_D:].reshape(t, ROW_SUB, 128)
    return pl.pallas_call(
        _peer_u_kernel,
        grid=(t // PEER_TT,),
        in_specs=[pl.BlockSpec((PEER_TT, PEER_SLOTS), lambda i: (i, 0), memory_space=pltpu.SMEM),
                  pl.BlockSpec((PEER_TT, ROW_SUB, 128), lambda i: (i, 0, 0)),
                  pl.BlockSpec((PEER_TT, ROW_SUB, 128), lambda i: (i, 0, 0)),
                  _table_spec(tab.shape[0])],
        out_specs=pl.BlockSpec((PEER_TT, PEER_SLOTS), lambda i: (i, 0)),
        out_shape=jax.ShapeDtypeStruct((t, PEER_SLOTS), jnp.float32),
        compiler_params=pltpu.CompilerParams(vmem_limit_bytes=VMEM_LIMIT_PEER),
        name="peer_u",
    )(idx, xl, xh, tab)


def peer_v(idx, coef, tab):
    t = idx.shape[0]
    olo, ohi = pl.pallas_call(
        _peer_v_kernel,
        grid=(t // PEER_TT,),
        in_specs=[pl.BlockSpec((PEER_TT, PEER_SLOTS), lambda i: (i, 0), memory_space=pltpu.SMEM),
                  pl.BlockSpec((PEER_TT, PEER_SLOTS), lambda i: (i, 0), memory_space=pltpu.SMEM),
                  _table_spec(tab.shape[0])],
        out_specs=[pl.BlockSpec((PEER_TT, ROW_SUB, 128), lambda i: (i, 0, 0)),
                   pl.BlockSpec((PEER_TT, ROW_SUB, 128), lambda i: (i, 0, 0))],
        out_shape=[jax.ShapeDtypeStruct((t, ROW_SUB, 128), jnp.float32)] * 2,
        compiler_params=pltpu.CompilerParams(vmem_limit_bytes=VMEM_LIMIT_PEER),
        name="peer_v",
    )(idx, coef, tab)
    return jnp.concatenate([olo.reshape(t, HALF_D), ohi.reshape(t, HALF_D)], axis=1)


PK_TT = 256
PK_A_BLOCKS = ((0, 16), (1, 8), (2, 5), (3, 4))
PK_B_BLOCKS = ((0, 4, 16), (1, 4, 8), (2, 4, 5))
NEG_HUGE = -3.0e38


def _extract_top(blocks, flats, payloads, n):
    vals, picked = [], []
    for _ in range(n):
        m = functools.reduce(jnp.maximum, [jnp.max(b, axis=0, keepdims=True) for b in blocks])
        pos = functools.reduce(jnp.minimum, [jnp.min(jnp.where(b == m, f, 1e9), axis=0, keepdims=True)
                                             for b, f in zip(blocks, flats)])
        hit = [f == pos for f in flats]
        picked.append(functools.reduce(jnp.add, [jnp.sum(jnp.where(h, p, 0.0), axis=0, keepdims=True)
                                                 for h, p in zip(hit, payloads)]))
        blocks = [jnp.where(h, NEG_HUGE, b) for h, b in zip(hit, blocks)]
        vals.append(m)
    return vals, picked


def _peer_topk_kernel(q_ref, ka_ref, eid_ref, w_ref):
    key_row = lax.broadcasted_iota(jnp.int32, (N_KEYS, PK_TT), 0).astype(F32)
    riota = lambda n: lax.broadcasted_iota(jnp.int32, (n, PK_TT), 0).astype(F32)
    flats = ([a * PEER_TOPK + riota(nb) for a, nb in PK_A_BLOCKS]
             + [(a0 + riota(a1 - a0)) * PEER_TOPK + b for b, a0, a1 in PK_B_BLOCKS])

    def head(h, carry):
        tops = []
        for p in range(2):
            q = q_ref[2 * h + p]
            q_hi = _trunc_bf16(q)
            qa = jnp.concatenate([q_hi.astype(BF16), (q - q_hi).astype(BF16), q_hi.astype(BF16)], axis=1)
            s = lax.dot_general(ka_ref[2 * h + p], qa, (((1,), (1,)), ((), ())),
                                preferred_element_type=F32)
            vals, idxs = _extract_top([s], [key_row], [key_row], PEER_TOPK)
            tops.append((jnp.concatenate(vals, axis=0), jnp.concatenate(idxs, axis=0)))
        (s1, i1), (s2, i2) = tops
        blocks = ([s1[a:a + 1] + s2[0:nb] for a, nb in PK_A_BLOCKS]
                  + [s1[a0:a1] + s2[b:b + 1] for b, a0, a1 in PK_B_BLOCKS])
        eids = ([i1[a:a + 1] * N_KEYS + i2[0:nb] for a, nb in PK_A_BLOCKS]
                + [i1[a0:a1] * N_KEYS + i2[b:b + 1] for b, a0, a1 in PK_B_BLOCKS])
        vals, picked = _extract_top(blocks, flats, eids, PEER_TOPK)
        top_s = jnp.concatenate(vals, axis=0)
        e = jnp.exp(top_s - vals[0])
        w_ref[h] = e / jnp.sum(e, axis=0, keepdims=True)
        eid_ref[h] = jnp.concatenate(picked, axis=0).astype(jnp.int32)
        return carry

    lax.fori_loop(0, PEER_HEADS, head, 0)


def peer_topk(q, subkeys):
    t = q.shape[0]
    d = PEER_QDIM // 2
    hp = PEER_HEADS * 2
    qt = q.reshape(t, hp, d).transpose(1, 0, 2)
    k_hi, k_lo = _split2(subkeys.reshape(hp, N_KEYS, d))
    ka = jnp.concatenate([k_hi, k_hi, k_lo], axis=2)
    eid, w = pl.pallas_call(
        _peer_topk_kernel,
        grid=(t // PK_TT,),
        in_specs=[pl.BlockSpec((hp, PK_TT, d), lambda i: (0, i, 0)),
                  pl.BlockSpec((hp, N_KEYS, 3 * d), lambda i: (0, 0, 0))],
        out_specs=[pl.BlockSpec((PEER_HEADS, PEER_TOPK, PK_TT), lambda i: (0, 0, i)),
                   pl.BlockSpec((PEER_HEADS, PEER_TOPK, PK_TT), lambda i: (0, 0, i))],
        out_shape=[jax.ShapeDtypeStruct((PEER_HEADS, PEER_TOPK, t), jnp.int32),
                   jax.ShapeDtypeStruct((PEER_HEADS, PEER_TOPK, t), F32)],
        name="peer_topk",
    )(qt, ka)
    to_slots = lambda a: a.transpose(2, 0, 1).reshape(t, PEER_SLOTS)
    return to_slots(eid), to_slots(w)


def rmsnorm(x, g):
    xf = x.astype(jnp.float32)
    y = xf * lax.rsqrt(jnp.mean(xf * xf, axis=-1, keepdims=True) + RMS_EPS)
    return (y * g.astype(jnp.float32)).astype(x.dtype)


def split_columns(t, sizes):
    offs = np.cumsum((0,) + tuple(sizes))
    return [t[..., int(a):int(b)] for a, b in zip(offs[:-1], offs[1:])]


def rope_tables(pos):
    inv = jnp.power(ROPE_THETA, -jnp.arange(0, ROPE_DIM, 2, dtype=jnp.float32) / ROPE_DIM)
    ang = pos[:, None] * inv[None, :]
    return jnp.cos(ang), jnp.sin(ang)


def apply_partial_rope(x, cos, sin):
    half = ROPE_DIM // 2
    xr = x[..., :ROPE_DIM].astype(jnp.float32)
    x1, x2 = xr[..., :half], xr[..., half:]
    c, s = cos[:, None, :], sin[:, None, :]
    rot = jnp.concatenate([x1 * c - x2 * s, x1 * s + x2 * c], axis=-1)
    return jnp.concatenate([rot.astype(x.dtype), x[..., ROPE_DIM:]], axis=-1)


BF16 = jnp.bfloat16
F32 = jnp.float32


def _flash_step(k, vt, qt, mask, carry):
    m, l, acc = carry
    s = jnp.dot(k, qt, preferred_element_type=F32)
    s_vis = s if mask is None else jnp.where(mask, s, NEG_INF)
    m_new = jnp.maximum(m, jnp.max(s_vis, axis=0, keepdims=True))
    alpha = jnp.exp(m - m_new)
    p = jnp.exp(s - m_new)
    if mask is not None:
        p = jnp.where(mask, p, 0.0)
    l = alpha * l + jnp.sum(p, axis=0, keepdims=True)
    acc = alpha * acc + jnp.dot(vt, p.astype(BF16), preferred_element_type=F32)
    return m_new, l, acc


def _flash_init(n):
    return (jnp.full((1, n), NEG_INF, F32), jnp.zeros((1, n), F32), jnp.zeros((HEAD_DIM, n), F32))


FOX_TQ = 256
FOX_KV = 512
FOX_DK = 256


FOX_HP = 2


def _fox_kernel(qt_ref, k_ref, vt_ref, o_ref):
    qi = pl.program_id(1)
    q0 = qi * FOX_TQ
    qts = [qt_ref[h] for h in range(FOX_HP)]
    qpos = q0 + lax.broadcasted_iota(jnp.int32, (1, FOX_TQ), 1)
    krow = lax.broadcasted_iota(jnp.int32, (FOX_KV, FOX_TQ), 0)

    def full_step(kt, carry):
        return tuple(_flash_step(k_ref[h, kt], vt_ref[h, kt], qts[h], None, carry[h]) for h in range(FOX_HP))

    def diag_step(kt, carry):
        mask = (kt * FOX_KV + krow) <= qpos
        return tuple(_flash_step(k_ref[h, kt], vt_ref[h, kt], qts[h], mask, carry[h]) for h in range(FOX_HP))

    n_full = (q0 + 1) // FOX_KV
    n_all = (q0 + FOX_TQ + FOX_KV - 1) // FOX_KV
    carry = lax.fori_loop(0, n_full, full_step, tuple(_flash_init(FOX_TQ) for _ in range(FOX_HP)))
    carry = lax.fori_loop(n_full, n_all, diag_step, carry)
    for h in range(FOX_HP):
        _, l, acc = carry[h]
        o_ref[h] = acc / l


def _trunc_bf16(x):
    bits = lax.bitcast_convert_type(x, jnp.uint32) & jnp.uint32(0xFFFF0000)
    return lax.bitcast_convert_type(bits, F32)


def _split3(c):
    c1 = _trunc_bf16(c)
    r = c - c1
    c2 = _trunc_bf16(r)
    return c1.astype(BF16), c2.astype(BF16), (r - c2).astype(BF16)


def _split2(x):
    hi = _trunc_bf16(x)
    return hi.astype(BF16), (x - hi).astype(BF16)


def _aug_q(q):
    hi, lo = _split2(q)
    return jnp.concatenate([hi, hi, lo], axis=-1)


def _aug_k(k):
    hi, lo = _split2(k)
    return jnp.concatenate([hi, lo, hi], axis=-1)


def fox_attention(q, k, v, log_f):
    B_, S_, H, Dh = q.shape
    KV_TILE = FOX_KV
    n_qt, n_kt = S_ // FOX_TQ, S_ // KV_TILE
    c = jnp.cumsum(log_f, axis=1)
    c1, c2, c3 = _split3(c)
    one = jnp.ones_like(c1)
    pad = jnp.zeros((B_, S_, H, FOX_DK - 3 * Dh - 6), BF16)
    col = lambda t: t[..., None]
    qa = jnp.concatenate([_aug_q(q * (Dh ** -0.5)), col(one), col(one), col(one),
                          col(c1), col(c2), col(c3), pad], axis=-1)
    ka = jnp.concatenate([_aug_k(k), col(-c1), col(-c2), col(-c3),
                          col(one), col(one), col(one), pad], axis=-1)
    HP, HG = FOX_HP, H // FOX_HP
    qt = qa.reshape(B_, n_qt, FOX_TQ, HG, HP, FOX_DK).transpose(0, 3, 1, 4, 5, 2).reshape(B_ * HG * n_qt, HP, FOX_DK, FOX_TQ)
    kk = ka.reshape(B_, n_kt, KV_TILE, HG, HP, FOX_DK).transpose(0, 3, 4, 1, 2, 5).reshape(B_ * HG, HP, n_kt, KV_TILE, FOX_DK)
    vt = v.astype(BF16).reshape(B_, n_kt, KV_TILE, HG, HP, Dh).transpose(0, 3, 4, 1, 5, 2).reshape(B_ * HG, HP, n_kt, Dh, KV_TILE)
    out = pl.pallas_call(
        _fox_kernel,
        grid=(B_ * HG, n_qt),
        in_specs=[pl.BlockSpec((None, HP, FOX_DK, FOX_TQ), lambda b, i: (b * n_qt + i, 0, 0, 0)),
                  pl.BlockSpec((None, HP, n_kt, KV_TILE, FOX_DK), lambda b, i: (b, 0, 0, 0, 0)),
                  pl.BlockSpec((None, HP, n_kt, Dh, KV_TILE), lambda b, i: (b, 0, 0, 0, 0))],
        out_specs=pl.BlockSpec((None, HP, Dh, FOX_TQ), lambda b, i: (b * n_qt + i, 0, 0, 0)),
        out_shape=jax.ShapeDtypeStruct((B_ * HG * n_qt, HP, Dh, FOX_TQ), F32),
        name="fox_attention",
    )(qt, kk, vt)
    return out.reshape(B_, HG, n_qt, HP, Dh, FOX_TQ).transpose(0, 2, 5, 1, 3, 4).reshape(B_, S_, H * Dh)


def compress_kv(k, v, cmp_pos, w1, w2):
    B_, S_ = k.shape[0], k.shape[1]
    n_cmp = (S_ - CMP_LEN) // CMP_STRIDE + 1
    starts = np.arange(n_cmp) * CMP_STRIDE
    idx = starts[:, None] + np.arange(CMP_LEN)[None, :]

    def phi(t, j):
        blocks = t[:, idx] + cmp_pos[j][None, None, :, None, :]
        flat = jnp.swapaxes(blocks, 2, 3).reshape(B_, n_cmp, NSA_GROUPS, CMP_LEN * HEAD_DIM)
        return jax.nn.gelu(flat @ w1[j], approximate=False) @ w2[j]

    return phi(k, 0), phi(v, 1), jnp.asarray(starts + CMP_LEN - 1, dtype=jnp.int32)


NSA_TQ = 128
NSA_N = NSA_HPG * NSA_TQ
N_CMP_PAD = 256
N_SLC = 64
NSA_DK = 3 * HEAD_DIM
NSA_KV = 256


def _nsa_kernel(qt_ref, kc_ref, vct_ref, ks_ref, vst_ref, kw_ref, vwt_ref, gate_ref, ovl_ref, o_ref, sel_ref):
    qi = pl.program_id(1)
    q0 = qi * NSA_TQ
    qt = qt_ref[...]
    lane = lax.broadcasted_iota(jnp.int32, (1, NSA_N), 1)
    qpos = q0 + (lane & (NSA_TQ - 1))

    s = jnp.dot(kc_ref[...], qt, preferred_element_type=F32)
    cmp_end = lax.broadcasted_iota(jnp.int32, (N_CMP_PAD, NSA_N), 0) * CMP_STRIDE + (CMP_LEN - 1)
    mc = cmp_end <= qpos
    m = jnp.max(jnp.where(mc, s, NEG_INF), axis=0, keepdims=True)
    p = jnp.where(mc, jnp.exp(s - m), 0.0)
    l = jnp.sum(p, axis=0, keepdims=True)
    pc = p * jnp.where(l > 0.0, 1.0 / l, 0.0)
    o_cmp = jnp.dot(vct_ref[...], pc.astype(BF16), preferred_element_type=F32)

    pcs = (pc[:, 0:NSA_TQ] + pc[:, NSA_TQ:2 * NSA_TQ]) + (pc[:, 2 * NSA_TQ:3 * NSA_TQ] + pc[:, 3 * NSA_TQ:4 * NSA_TQ])
    pcs_hi = pcs.astype(BF16)
    pcs_lo = (pcs - pcs_hi.astype(F32)).astype(BF16)
    ovl = ovl_ref[...]
    imp = jnp.dot(ovl, pcs_hi, preferred_element_type=F32) + jnp.dot(ovl, pcs_lo, preferred_element_type=F32)
    blk = lax.broadcasted_iota(jnp.int32, (N_SLC, NSA_TQ), 0)
    q_blk = (q0 + lax.broadcasted_iota(jnp.int32, (N_SLC, NSA_TQ), 1)) >> 6
    forced = (blk == 0) | (blk == q_blk) | (blk == q_blk - 1)
    score = jnp.where(forced, FORCED_SCORE, jnp.where(blk <= q_blk, imp, -1.0))
    rank = jnp.zeros((N_SLC, NSA_TQ), jnp.int32)
    for i in range(N_SLC):
        row = score[i:i + 1, :]
        beats = (row > score) | ((row == score) & (blk > i))
        rank = rank + beats.astype(jnp.int32)
    sel = (rank < SLC_TOP).astype(F32)
    sel_ref[...] = jnp.concatenate([sel] * (NSA_HPG // 2), axis=1)

    hn = NSA_N // 2
    qts = (qt[:, :hn], qt[:, hn:])
    qpos_h = qpos[:, :hn]
    krow = lax.broadcasted_iota(jnp.int32, (NSA_KV, hn), 0)
    blocks_per_tile = NSA_KV // SLC_LEN
    last_tile = (q0 + NSA_TQ - 1) // NSA_KV
    first_win = jnp.maximum(q0 - (WINDOW - 1), 0) // NSA_KV

    def slc_chains(kt, carry):
        rows = [jnp.broadcast_to(sel_ref[pl.ds(blocks_per_tile * kt + j, 1), :], (SLC_LEN, hn))
                for j in range(blocks_per_tile)]
        mask = (jnp.concatenate(rows, axis=0) > 0.0) & ((kt * NSA_KV + krow) <= qpos_h)
        k, vt = ks_ref[kt], vst_ref[kt]
        return tuple(_flash_step(k, vt, qts[h], mask, carry[h]) for h in range(2))

    def win_chains(kt, carry):
        rel = qpos_h - (kt * NSA_KV + krow)
        mask = (rel >= 0) & (rel < WINDOW)
        k, vt = kw_ref[kt], vwt_ref[kt]
        return tuple(_flash_step(k, vt, qts[h], mask, carry[h]) for h in range(2))

    init2 = (_flash_init(hn), _flash_init(hn))
    slc = lax.fori_loop(0, first_win, slc_chains, init2)
    slc, win = lax.fori_loop(first_win, last_tile + 1,
                             lambda kt, c: (slc_chains(kt, c[0]), win_chains(kt, c[1])), (slc, init2))
    acc_s = jnp.concatenate([slc[0][2] / slc[0][1], slc[1][2] / slc[1][1]], axis=1)
    acc_w = jnp.concatenate([win[0][2] / win[0][1], win[1][2] / win[1][1]], axis=1)

    g = jax.nn.sigmoid(gate_ref[...])
    o_ref[...] = g[0:1] * o_cmp + g[1:2] * acc_s + g[2:3] * acc_w


def nsa_attention(q, k_cmp, v_cmp, k_slc, v_slc, k_win, v_win, gate_logit):
    B_, S_, H, Dh = q.shape
    G = NSA_GROUPS
    KV_TILE = NSA_KV
    n_qt, n_kt = S_ // NSA_TQ, S_ // KV_TILE
    n_cmp = k_cmp.shape[1]
    cs = np.arange(N_CMP_PAD)[None, :] * CMP_STRIDE
    ss = np.arange(N_SLC)[:, None] * SLC_LEN
    ov = np.clip(np.minimum(cs + CMP_LEN, ss + SLC_LEN) - np.maximum(cs, ss), 0, None) / CMP_LEN
    ov[:, n_cmp:] = 0.0
    ovl = jnp.asarray(ov, dtype=BF16)
    qs = _aug_q(q * (Dh ** -0.5)).reshape(B_, n_qt, NSA_TQ, G, NSA_HPG, NSA_DK)
    qt = qs.transpose(0, 3, 1, 5, 4, 2).reshape(B_ * G * n_qt, NSA_DK, NSA_N)
    gt = gate_logit.astype(F32).reshape(B_, n_qt, NSA_TQ, G, NSA_HPG, N_NSA_BRANCH)
    gt = gt.transpose(0, 3, 1, 5, 4, 2).reshape(B_ * G * n_qt, N_NSA_BRANCH, NSA_N)
    keys = lambda t: _aug_k(t).reshape(B_, n_kt, KV_TILE, G, NSA_DK).transpose(0, 3, 1, 2, 4).reshape(B_ * G, n_kt, KV_TILE, NSA_DK)
    vals = lambda t: t.astype(BF16).reshape(B_, n_kt, KV_TILE, G, Dh).transpose(0, 3, 1, 4, 2).reshape(B_ * G, n_kt, Dh, KV_TILE)
    padc = ((0, 0), (0, N_CMP_PAD - n_cmp), (0, 0), (0, 0))
    kc = _aug_k(jnp.pad(k_cmp, padc)).transpose(0, 2, 1, 3).reshape(B_ * G, N_CMP_PAD, NSA_DK)
    vct = jnp.pad(v_cmp, padc).astype(BF16).transpose(0, 2, 3, 1).reshape(B_ * G, Dh, N_CMP_PAD)
    per_bg = lambda *blk: pl.BlockSpec((None,) + blk, lambda b, i: (b,) + (0,) * len(blk))
    per_tile = lambda *blk: pl.BlockSpec((None,) + blk, lambda b, i: (b * n_qt + i,) + (0,) * len(blk))
    out = pl.pallas_call(
        _nsa_kernel,
        grid=(B_ * G, n_qt),
        in_specs=[per_tile(NSA_DK, NSA_N),
                  per_bg(N_CMP_PAD, NSA_DK), per_bg(Dh, N_CMP_PAD),
                  per_bg(n_kt, KV_TILE, NSA_DK), per_bg(n_kt, Dh, KV_TILE),
                  per_bg(n_kt, KV_TILE, NSA_DK), per_bg(n_kt, Dh, KV_TILE),
                  per_tile(N_NSA_BRANCH, NSA_N),
                  pl.BlockSpec((N_SLC, N_CMP_PAD), lambda b, i: (0, 0))],
        out_specs=per_tile(Dh, NSA_N),
        out_shape=jax.ShapeDtypeStruct((B_ * G * n_qt, Dh, NSA_N), F32),
        scratch_shapes=[pltpu.VMEM((N_SLC, NSA_N // 2), F32)],
        name="nsa_attention",
    )(qt, kc, vct, keys(k_slc), vals(v_slc), keys(k_win), vals(v_win), gt, ovl)
    return out.reshape(B_, G, n_qt, Dh, NSA_HPG, NSA_TQ).transpose(0, 2, 5, 1, 4, 3).reshape(B_, S_, H * Dh)


def hybrid_mixer(h, w_in, fox_f_bias, cmp_pos, cmp_w1, cmp_w2, w_branch, w_out):
    B_, S_, _ = h.shape
    proj = h @ w_in
    (fq, fk, fv, f_logit, nq, kc, vc, ksl, vsl, kwn, vwn,
     nsa_gate_logit, merge_logit) = split_columns(proj, IN_SPLITS)
    heads = lambda t, n: t.reshape(B_, S_, n, HEAD_DIM)
    log_f = jax.nn.log_sigmoid(f_logit.astype(jnp.float32) + fox_f_bias.astype(jnp.float32))
    y_fox = fox_attention(heads(fq, FOX_HEADS), heads(fk, FOX_HEADS), heads(fv, FOX_HEADS), log_f)
    cos, sin = rope_tables(jnp.arange(S_, dtype=jnp.float32))
    q_nsa = apply_partial_rope(heads(nq, NSA_HEADS), cos, sin)
    k_slc = apply_partial_rope(heads(ksl, NSA_GROUPS), cos, sin)
    k_win = apply_partial_rope(heads(kwn, NSA_GROUPS), cos, sin)
    k_cmp, v_cmp, cmp_end = compress_kv(heads(kc, NSA_GROUPS), heads(vc, NSA_GROUPS), cmp_pos, cmp_w1, cmp_w2)
    k_cmp = apply_partial_rope(k_cmp, *rope_tables(cmp_end.astype(jnp.float32)))
    y_nsa = nsa_attention(q_nsa, k_cmp, v_cmp, k_slc, heads(vsl, NSA_GROUPS), k_win, heads(vwn, NSA_GROUPS),
                          nsa_gate_logit.reshape(B_, S_, NSA_HEADS, N_NSA_BRANCH))
    g = jax.nn.sigmoid(merge_logit.astype(jnp.float32)).reshape(B_, S_, N_BRANCH, D_MODEL).astype(h.dtype)
    ys = jnp.stack([y_fox, y_nsa], axis=2)
    up = jnp.einsum('bsnc,ncd->bsnd', ys, w_branch)
    merged = jnp.sum(g * up, axis=2)
    return merged @ w_out


def peer_ffn(h, wq, subkeys, u, v):
    B_, S_, D = h.shape
    T = B_ * S_
    hf = h.reshape(T, D)
    idx2, w = peer_topk(hf @ wq, subkeys)
    act = jax.nn.gelu(peer_u(idx2, hf, pack_table(u)), approximate=False)
    out = peer_v(idx2, w * act, pack_table(v))
    return out.reshape(B_, S_, D)


def kernel(x, norm_mix, w_in, fox_f_bias, nsa_cmp_pos, nsa_cmp_w1, nsa_cmp_w2, w_branch, w_out,
           norm_ffn, peer_wq, peer_subkeys, peer_u, peer_v, norm_final):
    B_, S_, D = x.shape
    h = x
    for l in range(norm_mix.shape[0]):
        h = h + hybrid_mixer(rmsnorm(h, norm_mix[l]), w_in[l], fox_f_bias[l], nsa_cmp_pos[l],
                             nsa_cmp_w1[l], nsa_cmp_w2[l], w_branch[l], w_out[l])
        h = h + peer_ffn(rmsnorm(h, norm_ffn[l]), peer_wq[l], peer_subkeys[l], peer_u[l], peer_v[l])
    return rmsnorm_pallas(h.reshape(B_ * S_, D), norm_final).reshape(B_, S_, D)
```

```python
import functools

import jax
import jax.numpy as jnp
import numpy as np
from jax import lax
from jax.experimental import pallas as pl
from jax.experimental.pallas import tpu as pltpu

D_MODEL = 1024
HEAD_DIM = 64
FOX_HEADS = 8
NSA_HEADS = 8
NSA_GROUPS = 2
NSA_HPG = NSA_HEADS // NSA_GROUPS
BRANCH_WIDTH = 512
N_BRANCH = 2
N_NSA_BRANCH = 3
ROPE_DIM = HEAD_DIM // 4
ROPE_THETA = 500000.0
Q_BLOCK = 128
CMP_LEN = 32
CMP_STRIDE = 16
CMP_HIDDEN = 2 * HEAD_DIM
SLC_LEN = 64
SLC_TOP = 16
WINDOW = 512
FORCED_SCORE = 1e9
NEG_INF = -1e30
PEER_HEADS = 8
N_KEYS = 128
N_EXPERTS = N_KEYS * N_KEYS
PEER_QDIM = 256
PEER_TOPK = 16
PEER_CHUNK = 128
RMS_EPS = 1e-6
KV_WIDTH = NSA_GROUPS * HEAD_DIM
IN_SPLITS = (BRANCH_WIDTH, BRANCH_WIDTH, BRANCH_WIDTH, FOX_HEADS, BRANCH_WIDTH,
             KV_WIDTH, KV_WIDTH, KV_WIDTH, KV_WIDTH, KV_WIDTH, KV_WIDTH,
             NSA_HEADS * N_NSA_BRANCH, N_BRANCH * D_MODEL)


def _rmsnorm_kernel(x_ref, g_ref, o_ref):
    x = x_ref[...]
    ms = jnp.mean(x * x, axis=-1, keepdims=True)
    o_ref[...] = x * lax.rsqrt(ms + RMS_EPS) * g_ref[...]


def rmsnorm_pallas(x2d, g, rows=512):
    t, d = x2d.shape
    return pl.pallas_call(
        _rmsnorm_kernel,
        grid=(t // rows,),
        in_specs=[pl.BlockSpec((rows, d), lambda i: (i, 0)),
                  pl.BlockSpec((1, d), lambda i: (0, 0))],
        out_specs=pl.BlockSpec((rows, d), lambda i: (i, 0)),
        out_shape=jax.ShapeDtypeStruct((t, d), jnp.float32),
    )(x2d, g.reshape(1, d))


PEER_SLOTS = PEER_HEADS * PEER_TOPK
PEER_TT = 64
PEER_CHUNK_SLOTS = 16
HALF_D = D_MODEL // 2
ROW_SUB = HALF_D // 128
VMEM_LIMIT_PEER = 48 * 1024 * 1024


def pack_table(tab):
    n = tab.shape[0]
    b = lax.bitcast_convert_type(tab.astype(jnp.bfloat16), jnp.uint16).astype(jnp.uint32)
    w = b[:, :HALF_D] | (b[:, HALF_D:] << 16)
    return w.reshape(n, ROW_SUB, 128)


def _unpack(w):
    lo = lax.bitcast_convert_type(w << 16, jnp.float32)
    hi = lax.bitcast_convert_type(w & jnp.uint32(0xFFFF0000), jnp.float32)
    return lo, hi


def _peer_u_kernel(idx_ref, x_ref, tab_ref, out_ref):
    lane = lax.broadcasted_iota(jnp.int32, (ROW_SUB, 128), 1)

    def tok(t, carry):
        row = x_ref[pl.ds(t, 1), :]
        chunks = [row[:, c * 128:(c + 1) * 128] for c in range(2 * ROW_SUB)]
        xl = jnp.concatenate(chunks[:ROW_SUB], axis=0)
        xh = jnp.concatenate(chunks[ROW_SUB:], axis=0)
        acc = jnp.zeros((ROW_SUB, 128), jnp.float32)
        base = t * PEER_SLOTS
        for s in range(PEER_SLOTS):
            lo, hi = _unpack(tab_ref[idx_ref[base + s]])
            part = jnp.sum(lo * xl + hi * xh, axis=1, keepdims=True)
            acc = jnp.where(lane == s, part, acc)
        out_ref[pl.ds(t, 1), :] = jnp.sum(acc, axis=0, keepdims=True)
        return carry

    lax.fori_loop(0, PEER_TT, tok, 0)


def _peer_v_kernel(idx_ref, coef_ref, tab_ref, out_ref):
    n_acc = 4
    zero = jnp.zeros((ROW_SUB, 128), jnp.float32)

    def tok(t, carry):
        def chunk(c, accs):
            acc_lo, acc_hi = list(accs[:n_acc]), list(accs[n_acc:])
            base = t * PEER_SLOTS + c * PEER_CHUNK_SLOTS
            for j in range(PEER_CHUNK_SLOTS):
                lo, hi = _unpack(tab_ref[idx_ref[base + j]])
                cf = coef_ref[base + j]
                acc_lo[j % n_acc] = acc_lo[j % n_acc] + cf * lo
                acc_hi[j % n_acc] = acc_hi[j % n_acc] + cf * hi
            return tuple(acc_lo + acc_hi)

        accs = lax.fori_loop(0, PEER_SLOTS // PEER_CHUNK_SLOTS, chunk, (zero,) * (2 * n_acc))
        halves = ((accs[0] + accs[1]) + (accs[2] + accs[3]), (accs[4] + accs[5]) + (accs[6] + accs[7]))
        out_ref[pl.ds(t, 1), :] = jnp.concatenate(
            [halves[hlf][r:r + 1, :] for hlf in range(2) for r in range(ROW_SUB)], axis=1)
        return carry

    lax.fori_loop(0, PEER_TT, tok, 0)


def _smem_spec():
    return pl.BlockSpec((PEER_TT * PEER_SLOTS,), lambda i: (i,), memory_space=pltpu.SMEM)


def _table_spec(n):
    return pl.BlockSpec((n, ROW_SUB, 128), lambda i: (0, 0, 0), pipeline_mode=pl.Buffered(1))


def peer_u(idx, x, tab):
    t = x.shape[0]
    return pl.pallas_call(
        _peer_u_kernel,
        grid=(t // PEER_TT,),
        in_specs=[_smem_spec(), pl.BlockSpec((PEER_TT, D_MODEL), lambda i: (i, 0)), _table_spec(tab.shape[0])],
        out_specs=pl.BlockSpec((PEER_TT, PEER_SLOTS), lambda i: (i, 0)),
        out_shape=jax.ShapeDtypeStruct((t, PEER_SLOTS), jnp.float32),
        compiler_params=pltpu.CompilerParams(vmem_limit_bytes=VMEM_LIMIT_PEER),
        name="peer_u",
    )(idx.reshape(-1), x, tab)


def peer_v(idx, coef, tab):
    t = idx.shape[0]
    return pl.pallas_call(
        _peer_v_kernel,
        grid=(t // PEER_TT,),
        in_specs=[_smem_spec(), _smem_spec(), _table_spec(tab.shape[0])],
        out_specs=pl.BlockSpec((PEER_TT, D_MODEL), lambda i: (i, 0)),
        out_shape=jax.ShapeDtypeStruct((t, D_MODEL), jnp.float32),
        compiler_params=pltpu.CompilerParams(vmem_limit_bytes=VMEM_LIMIT_PEER),
        name="peer_v",
    )(idx.reshape(-1), coef.reshape(-1), tab)


PK_TT = 256
PK_A_BLOCKS = ((0, 16), (1, 8), (2, 5), (3, 4))
PK_B_BLOCKS = ((0, 4, 16), (1, 4, 8), (2, 4, 5))
NEG_HUGE = -3.0e38


def _extract_top(blocks, flats, payloads, n):
    vals, picked = [], []
    for _ in range(n):
        m = functools.reduce(jnp.maximum, [jnp.max(b, axis=0, keepdims=True) for b in blocks])
        pos = functools.reduce(jnp.minimum, [jnp.min(jnp.where(b == m, f, 1e9), axis=0, keepdims=True)
                                             for b, f in zip(blocks, flats)])
        hit = [f == pos for f in flats]
        picked.append(functools.reduce(jnp.add, [jnp.sum(jnp.where(h, p, 0.0), axis=0, keepdims=True)
                                                 for h, p in zip(hit, payloads)]))
        blocks = [jnp.where(h, NEG_HUGE, b) for h, b in zip(hit, blocks)]
        vals.append(m)
    return vals, picked


def _peer_topk_kernel(q_ref, ka_ref, eid_ref, w_ref):
    key_row = lax.broadcasted_iota(jnp.int32, (N_KEYS, PK_TT), 0).astype(F32)
    riota = lambda n: lax.broadcasted_iota(jnp.int32, (n, PK_TT), 0).astype(F32)
    flats = ([a * PEER_TOPK + riota(nb) for a, nb in PK_A_BLOCKS]
             + [(a0 + riota(a1 - a0)) * PEER_TOPK + b for b, a0, a1 in PK_B_BLOCKS])

    def head(h, carry):
        tops = []
        for p in range(2):
            q = q_ref[2 * h + p]
            q_hi = _trunc_bf16(q)
            qa = jnp.concatenate([q_hi.astype(BF16), (q - q_hi).astype(BF16), q_hi.astype(BF16)], axis=1)
            s = lax.dot_general(ka_ref[2 * h + p], qa, (((1,), (1,)), ((), ())),
                                preferred_element_type=F32)
            vals, idxs = _extract_top([s], [key_row], [key_row], PEER_TOPK)
            tops.append((jnp.concatenate(vals, axis=0), jnp.concatenate(idxs, axis=0)))
        (s1, i1), (s2, i2) = tops
        blocks = ([s1[a:a + 1] + s2[0:nb] for a, nb in PK_A_BLOCKS]
                  + [s1[a0:a1] + s2[b:b + 1] for b, a0, a1 in PK_B_BLOCKS])
        eids = ([i1[a:a + 1] * N_KEYS + i2[0:nb] for a, nb in PK_A_BLOCKS]
                + [i1[a0:a1] * N_KEYS + i2[b:b + 1] for b, a0, a1 in PK_B_BLOCKS])
        vals, picked = _extract_top(blocks, flats, eids, PEER_TOPK)
        top_s = jnp.concatenate(vals, axis=0)
        e = jnp.exp(top_s - vals[0])
        w_ref[h] = e / jnp.sum(e, axis=0, keepdims=True)
        eid_ref[h] = jnp.concatenate(picked, axis=0).astype(jnp.int32)
        return carry

    lax.fori_loop(0, PEER_HEADS, head, 0)


def peer_topk(qt, subkeys):
    hp, t, d = qt.shape
    k_hi, k_lo = _split2(subkeys.reshape(hp, N_KEYS, d))
    ka = jnp.concatenate([k_hi, k_hi, k_lo], axis=2)
    eid, w = pl.pallas_call(
        _peer_topk_kernel,
        grid=(t // PK_TT,),
        in_specs=[pl.BlockSpec((hp, PK_TT, d), lambda i: (0, i, 0)),
                  pl.BlockSpec((hp, N_KEYS, 3 * d), lambda i: (0, 0, 0))],
        out_specs=[pl.BlockSpec((PEER_HEADS, PEER_TOPK, PK_TT), lambda i: (0, 0, i)),
                   pl.BlockSpec((PEER_HEADS, PEER_TOPK, PK_TT), lambda i: (0, 0, i))],
        out_shape=[jax.ShapeDtypeStruct((PEER_HEADS, PEER_TOPK, t), jnp.int32),
                   jax.ShapeDtypeStruct((PEER_HEADS, PEER_TOPK, t), F32)],
        name="peer_topk",
    )(qt, ka)
    to_slots = lambda a: a.transpose(2, 0, 1).reshape(t, PEER_SLOTS)
    return to_slots(eid), to_slots(w)


def rmsnorm(x, g):
    xf = x.astype(jnp.float32)
    y = xf * lax.rsqrt(jnp.mean(xf * xf, axis=-1, keepdims=True) + RMS_EPS)
    return (y * g.astype(jnp.float32)).astype(x.dtype)


def split_columns(t, sizes):
    offs = np.cumsum((0,) + tuple(sizes))
    return [t[..., int(a):int(b)] for a, b in zip(offs[:-1], offs[1:])]


def rope_tables(pos):
    inv = jnp.power(ROPE_THETA, -jnp.arange(0, ROPE_DIM, 2, dtype=jnp.float32) / ROPE_DIM)
    ang = pos[:, None] * inv[None, :]
    return jnp.cos(ang), jnp.sin(ang)


def apply_partial_rope(x, cos, sin):
    half = ROPE_DIM // 2
    xr = x[..., :ROPE_DIM].astype(jnp.float32)
    x1, x2 = xr[..., :half], xr[..., half:]
    c, s = cos[:, None, :], sin[:, None, :]
    rot = jnp.concatenate([x1 * c - x2 * s, x1 * s + x2 * c], axis=-1)
    return jnp.concatenate([rot.astype(x.dtype), x[..., ROPE_DIM:]], axis=-1)


BF16 = jnp.bfloat16
F32 = jnp.float32


def _flash_step(k, vt, qt, mask, carry):
    m, l, acc = carry
    s = jnp.dot(k, qt, preferred_element_type=F32)
    s_vis = s if mask is None else jnp.where(mask, s, NEG_INF)
    m_new = jnp.maximum(m, jnp.max(s_vis, axis=0, keepdims=True))
    alpha = jnp.exp(m - m_new)
    p = jnp.exp(s - m_new)
    if mask is not None:
        p = jnp.where(mask, p, 0.0)
    l = alpha * l + jnp.sum(p, axis=0, keepdims=True)
    acc = alpha * acc + jnp.dot(vt, p.astype(BF16), preferred_element_type=F32)
    return m_new, l, acc


def _flash_init(n):
    return (jnp.full((1, n), NEG_INF, F32), jnp.zeros((1, n), F32), jnp.zeros((HEAD_DIM, n), F32))


FOX_TQ = 256
FOX_KV = 512
FOX_DK = 256


FOX_HP = 2


def _fox_kernel(qt_ref, k_ref, vt_ref, o_ref):
    qi = pl.program_id(1)
    q0 = qi * FOX_TQ
    qts = [qt_ref[h] for h in range(FOX_HP)]
    qpos = q0 + lax.broadcasted_iota(jnp.int32, (1, FOX_TQ), 1)
    krow = lax.broadcasted_iota(jnp.int32, (FOX_KV, FOX_TQ), 0)

    def full_step(kt, carry):
        return tuple(_flash_step(k_ref[h, kt], vt_ref[h, kt], qts[h], None, carry[h]) for h in range(FOX_HP))

    def diag_step(kt, carry):
        mask = (kt * FOX_KV + krow) <= qpos
        return tuple(_flash_step(k_ref[h, kt], vt_ref[h, kt], qts[h], mask, carry[h]) for h in range(FOX_HP))

    n_full = (q0 + 1) // FOX_KV
    n_all = (q0 + FOX_TQ + FOX_KV - 1) // FOX_KV
    carry = lax.fori_loop(0, n_full, full_step, tuple(_flash_init(FOX_TQ) for _ in range(FOX_HP)))
    carry = lax.fori_loop(n_full, n_all, diag_step, carry)
    for h in range(FOX_HP):
        _, l, acc = carry[h]
        o_ref[h] = acc / l


def _trunc_bf16(x):
    bits = lax.bitcast_convert_type(x, jnp.uint32) & jnp.uint32(0xFFFF0000)
    return lax.bitcast_convert_type(bits, F32)


def _split3(c):
    c1 = _trunc_bf16(c)
    r = c - c1
    c2 = _trunc_bf16(r)
    return c1.astype(BF16), c2.astype(BF16), (r - c2).astype(BF16)


def _split2(x):
    hi = _trunc_bf16(x)
    return hi.astype(BF16), (x - hi).astype(BF16)


def _aug_q(q):
    hi, lo = _split2(q)
    return jnp.concatenate([hi, hi, lo], axis=-1)


def _aug_k(k):
    hi, lo = _split2(k)
    return jnp.concatenate([hi, lo, hi], axis=-1)


def fox_attention(q, k, v, log_f):
    B_, S_, H, Dh = q.shape
    KV_TILE = FOX_KV
    n_qt, n_kt = S_ // FOX_TQ, S_ // KV_TILE
    c = jnp.cumsum(log_f, axis=1)
    c1, c2, c3 = _split3(c)
    one = jnp.ones_like(c1)
    pad = jnp.zeros((B_, S_, H, FOX_DK - 3 * Dh - 6), BF16)
    col = lambda t: t[..., None]
    qa = jnp.concatenate([_aug_q(q * (Dh ** -0.5)), col(one), col(one), col(one),
                          col(c1), col(c2), col(c3), pad], axis=-1)
    ka = jnp.concatenate([_aug_k(k), col(-c1), col(-c2), col(-c3),
                          col(one), col(one), col(one), pad], axis=-1)
    HP, HG = FOX_HP, H // FOX_HP
    qt = qa.reshape(B_, n_qt, FOX_TQ, HG, HP, FOX_DK).transpose(0, 3, 1, 4, 5, 2).reshape(B_ * HG * n_qt, HP, FOX_DK, FOX_TQ)
    kk = ka.reshape(B_, n_kt, KV_TILE, HG, HP, FOX_DK).transpose(0, 3, 4, 1, 2, 5).reshape(B_ * HG, HP, n_kt, KV_TILE, FOX_DK)
    vt = v.astype(BF16).reshape(B_, n_kt, KV_TILE, HG, HP, Dh).transpose(0, 3, 4, 1, 5, 2).reshape(B_ * HG, HP, n_kt, Dh, KV_TILE)
    out = pl.pallas_call(
        _fox_kernel,
        grid=(B_ * HG, n_qt),
        in_specs=[pl.BlockSpec((None, HP, FOX_DK, FOX_TQ), lambda b, i: (b * n_qt + i, 0, 0, 0)),
                  pl.BlockSpec((None, HP, n_kt, KV_TILE, FOX_DK), lambda b, i: (b, 0, 0, 0, 0)),
                  pl.BlockSpec((None, HP, n_kt, Dh, KV_TILE), lambda b, i: (b, 0, 0, 0, 0))],
        out_specs=pl.BlockSpec((None, HP, Dh, FOX_TQ), lambda b, i: (b * n_qt + i, 0, 0, 0)),
        out_shape=jax.ShapeDtypeStruct((B_ * HG * n_qt, HP, Dh, FOX_TQ), F32),
        name="fox_attention",
    )(qt, kk, vt)
    return out.reshape(B_, HG, n_qt, HP, Dh, FOX_TQ).transpose(0, 2, 5, 1, 3, 4).reshape(B_, S_, H * Dh)


def compress_kv(k, v, cmp_pos, w1, w2):
    B_, S_ = k.shape[0], k.shape[1]
    n_cmp = (S_ - CMP_LEN) // CMP_STRIDE + 1
    starts = np.arange(n_cmp) * CMP_STRIDE
    idx = starts[:, None] + np.arange(CMP_LEN)[None, :]

    def phi(t, j):
        blocks = t[:, idx] + cmp_pos[j][None, None, :, None, :]
        flat = jnp.swapaxes(blocks, 2, 3).reshape(B_, n_cmp, NSA_GROUPS, CMP_LEN * HEAD_DIM)
        return jax.nn.gelu(flat @ w1[j], approximate=False) @ w2[j]

    return phi(k, 0), phi(v, 1), jnp.asarray(starts + CMP_LEN - 1, dtype=jnp.int32)


NSA_TQ = 128
NSA_N = NSA_HPG * NSA_TQ
N_CMP_PAD = 256
N_SLC = 64
NSA_DK = 3 * HEAD_DIM
NSA_KV = 256


def _nsa_kernel(qt_ref, kc_ref, vct_ref, ks_ref, vst_ref, kw_ref, vwt_ref, gate_ref, ovl_ref, o_ref, sel_ref):
    qi = pl.program_id(1)
    q0 = qi * NSA_TQ
    qt = qt_ref[...]
    lane = lax.broadcasted_iota(jnp.int32, (1, NSA_N), 1)
    qpos = q0 + (lane & (NSA_TQ - 1))

    s = jnp.dot(kc_ref[...], qt, preferred_element_type=F32)
    cmp_end = lax.broadcasted_iota(jnp.int32, (N_CMP_PAD, NSA_N), 0) * CMP_STRIDE + (CMP_LEN - 1)
    mc = cmp_end <= qpos
    m = jnp.max(jnp.where(mc, s, NEG_INF), axis=0, keepdims=True)
    p = jnp.where(mc, jnp.exp(s - m), 0.0)
    l = jnp.sum(p, axis=0, keepdims=True)
    pc = p * jnp.where(l > 0.0, 1.0 / l, 0.0)
    o_cmp = jnp.dot(vct_ref[...], pc.astype(BF16), preferred_element_type=F32)

    pcs = (pc[:, 0:NSA_TQ] + pc[:, NSA_TQ:2 * NSA_TQ]) + (pc[:, 2 * NSA_TQ:3 * NSA_TQ] + pc[:, 3 * NSA_TQ:4 * NSA_TQ])
    pcs_hi = pcs.astype(BF16)
    pcs_lo = (pcs - pcs_hi.astype(F32)).astype(BF16)
    ovl = ovl_ref[...]
    imp = jnp.dot(ovl, pcs_hi, preferred_element_type=F32) + jnp.dot(ovl, pcs_lo, preferred_element_type=F32)
    blk = lax.broadcasted_iota(jnp.int32, (N_SLC, NSA_TQ), 0)
    q_blk = (q0 + lax.broadcasted_iota(jnp.int32, (N_SLC, NSA_TQ), 1)) >> 6
    forced = (blk == 0) | (blk == q_blk) | (blk == q_blk - 1)
    score = jnp.where(forced, FORCED_SCORE, jnp.where(blk <= q_blk, imp, -1.0))
    rank = jnp.zeros((N_SLC, NSA_TQ), jnp.int32)
    for i in range(N_SLC):
        row = score[i:i + 1, :]
        beats = (row > score) | ((row == score) & (blk > i))
        rank = rank + beats.astype(jnp.int32)
    sel = (rank < SLC_TOP).astype(F32)
    sel_ref[...] = jnp.concatenate([sel] * (NSA_HPG // 2), axis=1)

    hn = NSA_N // 2
    qts = (qt[:, :hn], qt[:, hn:])
    qpos_h = qpos[:, :hn]
    krow = lax.broadcasted_iota(jnp.int32, (NSA_KV, hn), 0)
    blocks_per_tile = NSA_KV // SLC_LEN
    last_tile = (q0 + NSA_TQ - 1) // NSA_KV
    first_win = jnp.maximum(q0 - (WINDOW - 1), 0) // NSA_KV

    def slc_chains(kt, carry):
        rows = [jnp.broadcast_to(sel_ref[pl.ds(blocks_per_tile * kt + j, 1), :], (SLC_LEN, hn))
                for j in range(blocks_per_tile)]
        mask = (jnp.concatenate(rows, axis=0) > 0.0) & ((kt * NSA_KV + krow) <= qpos_h)
        k, vt = ks_ref[kt], vst_ref[kt]
        return tuple(_flash_step(k, vt, qts[h], mask, carry[h]) for h in range(2))

    def win_chains(kt, carry):
        rel = qpos_h - (kt * NSA_KV + krow)
        mask = (rel >= 0) & (rel < WINDOW)
        k, vt = kw_ref[kt], vwt_ref[kt]
        return tuple(_flash_step(k, vt, qts[h], mask, carry[h]) for h in range(2))

    init2 = (_flash_init(hn), _flash_init(hn))
    slc = lax.fori_loop(0, first_win, slc_chains, init2)
    slc, win = lax.fori_loop(first_win, last_tile + 1,
                             lambda kt, c: (slc_chains(kt, c[0]), win_chains(kt, c[1])), (slc, init2))
    acc_s = jnp.concatenate([slc[0][2] / slc[0][1], slc[1][2] / slc[1][1]], axis=1)
    acc_w = jnp.concatenate([win[0][2] / win[0][1], win[1][2] / win[1][1]], axis=1)

    g = jax.nn.sigmoid(gate_ref[...])
    o_ref[...] = g[0:1] * o_cmp + g[1:2] * acc_s + g[2:3] * acc_w


def nsa_attention(q, k_cmp, v_cmp, k_slc, v_slc, k_win, v_win, gate_logit):
    B_, S_, H, Dh = q.shape
    G = NSA_GROUPS
    KV_TILE = NSA_KV
    n_qt, n_kt = S_ // NSA_TQ, S_ // KV_TILE
    n_cmp = k_cmp.shape[1]
    cs = np.arange(N_CMP_PAD)[None, :] * CMP_STRIDE
    ss = np.arange(N_SLC)[:, None] * SLC_LEN
    ov = np.clip(np.minimum(cs + CMP_LEN, ss + SLC_LEN) - np.maximum(cs, ss), 0, None) / CMP_LEN
    ov[:, n_cmp:] = 0.0
    ovl = jnp.asarray(ov, dtype=BF16)
    qs = _aug_q(q * (Dh ** -0.5)).reshape(B_, n_qt, NSA_TQ, G, NSA_HPG, NSA_DK)
    qt = qs.transpose(0, 3, 1, 5, 4, 2).reshape(B_ * G * n_qt, NSA_DK, NSA_N)
    gt = gate_logit.astype(F32).reshape(B_, n_qt, NSA_TQ, G, NSA_HPG, N_NSA_BRANCH)
    gt = gt.transpose(0, 3, 1, 5, 4, 2).reshape(B_ * G * n_qt, N_NSA_BRANCH, NSA_N)
    keys = lambda t: _aug_k(t).reshape(B_, n_kt, KV_TILE, G, NSA_DK).transpose(0, 3, 1, 2, 4).reshape(B_ * G, n_kt, KV_TILE, NSA_DK)
    vals = lambda t: t.astype(BF16).reshape(B_, n_kt, KV_TILE, G, Dh).transpose(0, 3, 1, 4, 2).reshape(B_ * G, n_kt, Dh, KV_TILE)
    padc = ((0, 0), (0, N_CMP_PAD - n_cmp), (0, 0), (0, 0))
    kc = _aug_k(jnp.pad(k_cmp, padc)).transpose(0, 2, 1, 3).reshape(B_ * G, N_CMP_PAD, NSA_DK)
    vct = jnp.pad(v_cmp, padc).astype(BF16).transpose(0, 2, 3, 1).reshape(B_ * G, Dh, N_CMP_PAD)
    per_bg = lambda *blk: pl.BlockSpec((None,) + blk, lambda b, i: (b,) + (0,) * len(blk))
    per_tile = lambda *blk: pl.BlockSpec((None,) + blk, lambda b, i: (b * n_qt + i,) + (0,) * len(blk))
    out = pl.pallas_call(
        _nsa_kernel,
        grid=(B_ * G, n_qt),
        in_specs=[per_tile(NSA_DK, NSA_N),
                  per_bg(N_CMP_PAD, NSA_DK), per_bg(Dh, N_CMP_PAD),
                  per_bg(n_kt, KV_TILE, NSA_DK), per_bg(n_kt, Dh, KV_TILE),
                  per_bg(n_kt, KV_TILE, NSA_DK), per_bg(n_kt, Dh, KV_TILE),
                  per_tile(N_NSA_BRANCH, NSA_N),
                  pl.BlockSpec((N_SLC, N_CMP_PAD), lambda b, i: (0, 0))],
        out_specs=per_tile(Dh, NSA_N),
        out_shape=jax.ShapeDtypeStruct((B_ * G * n_qt, Dh, NSA_N), F32),
        scratch_shapes=[pltpu.VMEM((N_SLC, NSA_N // 2), F32)],
        name="nsa_attention",
    )(qt, kc, vct, keys(k_slc), vals(v_slc), keys(k_win), vals(v_win), gt, ovl)
    return out.reshape(B_, G, n_qt, Dh, NSA_HPG, NSA_TQ).transpose(0, 2, 5, 1, 4, 3).reshape(B_, S_, H * Dh)


DENSE_TM = 256
SEC_QKV = 4 * BRANCH_WIDTH
SEC_KV = 6 * KV_WIDTH
SEC_MERGE = N_BRANCH * D_MODEL
SEC_SMALL = 256
IN_SECTIONS = (SEC_QKV, SEC_KV, SEC_MERGE, SEC_SMALL)


def _rms(x, g):
    return x * lax.rsqrt(jnp.mean(x * x, axis=-1, keepdims=True) + RMS_EPS) * g


def _row_spec(width):
    return pl.BlockSpec((DENSE_TM, width), lambda i: (i, 0))


def _whole_spec(shape):
    return pl.BlockSpec(shape, lambda i: (0,) * len(shape))


def _in_proj_kernel(x_ref, g_ref, w_ref, *o_refs):
    xn = _rms(x_ref[...], g_ref[...]).astype(BF16)
    off = 0
    for o_ref, width in zip(o_refs, IN_SECTIONS):
        o_ref[...] = jnp.dot(xn, w_ref[:, off:off + width], preferred_element_type=F32)
        off += width


def in_projection(x, g, w_in):
    t, d = x.shape
    fq, fk, fv, fl, nq, kc, vc, ksl, vsl, kwn, vwn, gl, ml = split_columns(w_in, IN_SPLITS)
    pad = lambda w: jnp.pad(w, ((0, 0), (0, 128 - w.shape[1])))
    w = jnp.concatenate([fq, fk, fv, nq, kc, vc, ksl, vsl, kwn, vwn, ml, pad(fl), pad(gl)], axis=1).astype(BF16)
    return pl.pallas_call(
        _in_proj_kernel,
        grid=(t // DENSE_TM,),
        in_specs=[_row_spec(d), _whole_spec((1, d)), _whole_spec(w.shape)],
        out_specs=[_row_spec(s) for s in IN_SECTIONS],
        out_shape=[jax.ShapeDtypeStruct((t, s), F32) for s in IN_SECTIONS],
        name="in_projection",
    )(x, g.reshape(1, d), w)


def _merge_kernel(x_ref, yf_ref, yn_ref, ml_ref, wb_ref, wo_ref, o_ref):
    g = jax.nn.sigmoid(ml_ref[...])
    up_f = jnp.dot(yf_ref[...].astype(BF16), wb_ref[0], preferred_element_type=F32)
    up_n = jnp.dot(yn_ref[...].astype(BF16), wb_ref[1], preferred_element_type=F32)
    merged = g[:, :D_MODEL] * up_f + g[:, D_MODEL:] * up_n
    o_ref[...] = x_ref[...] + jnp.dot(merged.astype(BF16), wo_ref[...], preferred_element_type=F32)


def merge_out(x, y_fox, y_nsa, merge_logit, w_branch, w_out):
    t, d = x.shape
    return pl.pallas_call(
        _merge_kernel,
        grid=(t // DENSE_TM,),
        in_specs=[_row_spec(d), _row_spec(BRANCH_WIDTH), _row_spec(BRANCH_WIDTH), _row_spec(SEC_MERGE),
                  _whole_spec(w_branch.shape), _whole_spec(w_out.shape)],
        out_specs=_row_spec(d),
        out_shape=jax.ShapeDtypeStruct((t, d), F32),
        name="merge_out",
    )(x, y_fox, y_nsa, merge_logit, w_branch.astype(BF16), w_out.astype(BF16))


def _peer_q_kernel(x_ref, g_ref, w_ref, hn_ref, q_ref):
    hn = _rms(x_ref[...], g_ref[...])
    hn_ref[...] = hn
    q = jnp.dot(hn.astype(BF16), w_ref[...], preferred_element_type=F32)
    d = PEER_QDIM // 2
    for j in range(2 * PEER_HEADS):
        q_ref[j] = q[:, j * d:(j + 1) * d]


def peer_query(x, g, wq):
    t, d = x.shape
    hp, dq = 2 * PEER_HEADS, PEER_QDIM // 2
    return pl.pallas_call(
        _peer_q_kernel,
        grid=(t // DENSE_TM,),
        in_specs=[_row_spec(d), _whole_spec((1, d)), _whole_spec(wq.shape)],
        out_specs=[_row_spec(d), pl.BlockSpec((hp, DENSE_TM, dq), lambda i: (0, i, 0))],
        out_shape=[jax.ShapeDtypeStruct((t, d), F32), jax.ShapeDtypeStruct((hp, t, dq), F32)],
        name="peer_query",
    )(x, g.reshape(1, d), wq.astype(BF16))


def _final_kernel(h_ref, p_ref, g_ref, o_ref):
    o_ref[...] = _rms(h_ref[...] + p_ref[...], g_ref[...])


def final_norm(h, p, g):
    t, d = h.shape
    return pl.pallas_call(
        _final_kernel,
        grid=(t // DENSE_TM,),
        in_specs=[_row_spec(d), _row_spec(d), _whole_spec((1, d))],
        out_specs=_row_spec(d),
        out_shape=jax.ShapeDtypeStruct((t, d), F32),
        name="final_norm",
    )(h, p, g.reshape(1, d))


N_CMP_ROWS = 256


def _compress_kernel(r_ref, pos_ref, w1_ref, w2_ref, o_ref):
    r = r_ref[...]
    nxt = jnp.concatenate([r[1:], jnp.zeros((1, r.shape[1]), F32)], axis=0)
    half = r.shape[1]
    hid = (jnp.dot((r + pos_ref[0:1, :]).astype(BF16), w1_ref[:half], preferred_element_type=F32)
           + jnp.dot((nxt + pos_ref[1:2, :]).astype(BF16), w1_ref[half:], preferred_element_type=F32))
    hid = 0.5 * hid * (1.0 + lax.erf(hid * (2.0 ** -0.5)))
    o_ref[...] = jnp.dot(hid.astype(BF16), w2_ref[...], preferred_element_type=F32)


def compress_kv_pallas(kv, cmp_pos, w1, w2):
    _, B_, S_, G, Dh = kv.shape
    n_cmp = (S_ - CMP_LEN) // CMP_STRIDE + 1
    width = CMP_STRIDE * Dh
    rows = kv.transpose(0, 1, 3, 2, 4).reshape(2, B_ * G, S_ // CMP_STRIDE, width)
    out = pl.pallas_call(
        _compress_kernel,
        grid=(2, B_ * G),
        in_specs=[pl.BlockSpec((None, None, N_CMP_ROWS, width), lambda j, b: (j, b, 0, 0)),
                  pl.BlockSpec((None, 2, width), lambda j, b: (j, 0, 0)),
                  pl.BlockSpec((None, CMP_LEN * Dh, CMP_HIDDEN), lambda j, b: (j, 0, 0)),
                  pl.BlockSpec((None, CMP_HIDDEN, Dh), lambda j, b: (j, 0, 0))],
        out_specs=pl.BlockSpec((None, None, N_CMP_ROWS, Dh), lambda j, b: (j, b, 0, 0)),
        out_shape=jax.ShapeDtypeStruct((2, B_ * G, N_CMP_ROWS, Dh), F32),
        name="compress_kv",
    )(rows, cmp_pos.reshape(2, 2, width), w1.astype(BF16), w2.astype(BF16))
    return out[:, :, :n_cmp].reshape(2, B_, G, n_cmp, Dh).transpose(0, 1, 3, 2, 4)


def hybrid_mixer(x, norm_g, w_in, fox_f_bias, cmp_pos, cmp_w1, cmp_w2, w_branch, w_out):
    B_, S_, D = x.shape
    xf = x.reshape(B_ * S_, D)
    qkv, kvs, merge_logit, small = in_projection(xf, norm_g, w_in)
    heads = lambda t, n: t.reshape(B_, S_, n, HEAD_DIM)
    fq, fk, fv, nq = (qkv[:, i * BRANCH_WIDTH:(i + 1) * BRANCH_WIDTH] for i in range(4))
    kc, vc, ksl, vsl, kwn, vwn = (heads(kvs[:, i * KV_WIDTH:(i + 1) * KV_WIDTH], NSA_GROUPS) for i in range(6))
    f_logit = small[:, :FOX_HEADS].reshape(B_, S_, FOX_HEADS)
    gate_logit = small[:, 128:128 + NSA_HEADS * N_NSA_BRANCH].reshape(B_, S_, NSA_HEADS, N_NSA_BRANCH)
    log_f = jax.nn.log_sigmoid(f_logit + fox_f_bias.astype(F32))
    y_fox = fox_attention(heads(fq, FOX_HEADS), heads(fk, FOX_HEADS), heads(fv, FOX_HEADS), log_f)
    cos, sin = rope_tables(jnp.arange(S_, dtype=F32))
    q_nsa = apply_partial_rope(heads(nq, NSA_HEADS), cos, sin)
    k_slc = apply_partial_rope(ksl, cos, sin)
    k_win = apply_partial_rope(kwn, cos, sin)
    cmp = compress_kv_pallas(jnp.stack([kc, vc]), cmp_pos, cmp_w1, cmp_w2)
    n_cmp = cmp.shape[2]
    cmp_end = jnp.arange(n_cmp, dtype=F32) * CMP_STRIDE + (CMP_LEN - 1)
    k_cmp = apply_partial_rope(cmp[0], *rope_tables(cmp_end))
    y_nsa = nsa_attention(q_nsa, k_cmp, cmp[1], k_slc, vsl, k_win, vwn, gate_logit)
    return merge_out(xf, y_fox.reshape(B_ * S_, -1), y_nsa.reshape(B_ * S_, -1), merge_logit, w_branch, w_out)


def peer_ffn(h, norm_g, wq, subkeys, u, v):
    hn, q = peer_query(h, norm_g, wq)
    idx2, w = peer_topk(q, subkeys)
    act = jax.nn.gelu(peer_u(idx2, hn, pack_table(u)), approximate=False)
    return peer_v(idx2, w * act, pack_table(v))


def kernel(x, norm_mix, w_in, fox_f_bias, nsa_cmp_pos, nsa_cmp_w1, nsa_cmp_w2, w_branch, w_out,
           norm_ffn, peer_wq, peer_subkeys, peer_u, peer_v, norm_final):
    B_, S_, D = x.shape
    assert norm_mix.shape[0] == 1, "single-layer trunk"
    h = hybrid_mixer(x, norm_mix[0], w_in[0], fox_f_bias[0], nsa_cmp_pos[0], nsa_cmp_w1[0], nsa_cmp_w2[0],
                     w_branch[0], w_out[0])
    p = peer_ffn(h, norm_ffn[0], peer_wq[0], peer_subkeys[0], peer_u[0], peer_v[0])
    return final_norm(h, p, norm_final).reshape(B_, S_, D)
```

```python
import functools

import jax
import jax.numpy as jnp
import numpy as np
from jax import lax
from jax.experimental import pallas as pl
from jax.experimental.pallas import tpu as pltpu

D_MODEL = 1024
HEAD_DIM = 64
FOX_HEADS = 8
NSA_HEADS = 8
NSA_GROUPS = 2
NSA_HPG = NSA_HEADS // NSA_GROUPS
BRANCH_WIDTH = 512
N_BRANCH = 2
N_NSA_BRANCH = 3
ROPE_DIM = HEAD_DIM // 4
ROPE_THETA = 500000.0
Q_BLOCK = 128
CMP_LEN = 32
CMP_STRIDE = 16
CMP_HIDDEN = 2 * HEAD_DIM
SLC_LEN = 64
SLC_TOP = 16
WINDOW = 512
FORCED_SCORE = 1e9
NEG_INF = -1e30
PEER_HEADS = 8
N_KEYS = 128
N_EXPERTS = N_KEYS * N_KEYS
PEER_QDIM = 256
PEER_TOPK = 16
PEER_CHUNK = 128
RMS_EPS = 1e-6
KV_WIDTH = NSA_GROUPS * HEAD_DIM
IN_SPLITS = (BRANCH_WIDTH, BRANCH_WIDTH, BRANCH_WIDTH, FOX_HEADS, BRANCH_WIDTH,
             KV_WIDTH, KV_WIDTH, KV_WIDTH, KV_WIDTH, KV_WIDTH, KV_WIDTH,
             NSA_HEADS * N_NSA_BRANCH, N_BRANCH * D_MODEL)


def _rmsnorm_kernel(x_ref, g_ref, o_ref):
    x = x_ref[...]
    ms = jnp.mean(x * x, axis=-1, keepdims=True)
    o_ref[...] = x * lax.rsqrt(ms + RMS_EPS) * g_ref[...]


def rmsnorm_pallas(x2d, g, rows=512):
    t, d = x2d.shape
    return pl.pallas_call(
        _rmsnorm_kernel,
        grid=(t // rows,),
        in_specs=[pl.BlockSpec((rows, d), lambda i: (i, 0)),
                  pl.BlockSpec((1, d), lambda i: (0, 0))],
        out_specs=pl.BlockSpec((rows, d), lambda i: (i, 0)),
        out_shape=jax.ShapeDtypeStruct((t, d), jnp.float32),
    )(x2d, g.reshape(1, d))


PEER_SLOTS = PEER_HEADS * PEER_TOPK
PEER_TT = 64
PEER_CHUNK_SLOTS = 16
HALF_D = D_MODEL // 2
ROW_SUB = HALF_D // 128
VMEM_LIMIT_PEER = 48 * 1024 * 1024


def pack_table(tab):
    n = tab.shape[0]
    b = lax.bitcast_convert_type(tab.astype(jnp.bfloat16), jnp.uint16).astype(jnp.uint32)
    b = b.reshape(n, ROW_SUB, 2, 128)
    return b[:, :, 0, :] | (b[:, :, 1, :] << 16)


def _unpack(w):
    lo = lax.bitcast_convert_type(w << 16, jnp.float32)
    hi = lax.bitcast_convert_type(w & jnp.uint32(0xFFFF0000), jnp.float32)
    return lo, hi


def _peer_u_kernel(idx_ref, x_ref, tab_ref, out_ref):
    lane = lax.broadcasted_iota(jnp.int32, (ROW_SUB, 128), 1)

    def tok(t, carry):
        row = x_ref[pl.ds(t, 1), :]
        chunks = [row[:, c * 128:(c + 1) * 128] for c in range(2 * ROW_SUB)]
        xl = jnp.concatenate(chunks[0::2], axis=0)
        xh = jnp.concatenate(chunks[1::2], axis=0)
        acc = jnp.zeros((ROW_SUB, 128), jnp.float32)
        base = t * PEER_SLOTS
        for s in range(PEER_SLOTS):
            lo, hi = _unpack(tab_ref[idx_ref[base + s]])
            part = jnp.sum(lo * xl + hi * xh, axis=1, keepdims=True)
            acc = jnp.where(lane == s, part, acc)
        out_ref[pl.ds(t, 1), :] = jnp.sum(acc, axis=0, keepdims=True)
        return carry

    lax.fori_loop(0, PEER_TT, tok, 0)


N_CHUNK = D_MODEL // 128
STAGE_ROWS = PEER_SLOTS * ROW_SUB
PEER_TOK_UNROLL = 2


def _gather_token(idx_ref, tab_ref, stage_ref, t):
    base = t * PEER_SLOTS
    for s in range(PEER_SLOTS):
        stage_ref[s * ROW_SUB:(s + 1) * ROW_SUB, :] = tab_ref[idx_ref[base + s]]
    return pltpu.bitcast(stage_ref[...], BF16)


def _peer_v_kernel(idx_ref, coef_ref, expand_ref, tab_ref, out_ref, stage_ref, ce_hi_ref, ce_lo_ref):
    coef = coef_ref[...]
    c_hi = _trunc_bf16(coef)
    expand = expand_ref[...]
    ce_hi_ref[...] = jnp.dot(c_hi.astype(BF16), expand, preferred_element_type=F32)
    ce_lo_ref[...] = jnp.dot((coef - c_hi).astype(BF16), expand, preferred_element_type=F32)
    width = PEER_SLOTS * N_CHUNK
    own_chunk = ((lax.broadcasted_iota(jnp.int32, (N_CHUNK, width), 1) & (N_CHUNK - 1))
                 == lax.broadcasted_iota(jnp.int32, (N_CHUNK, width), 0))

    def trip(i, carry):
        for j in range(PEER_TOK_UNROLL):
            t = i * PEER_TOK_UNROLL + j
            gathered = _gather_token(idx_ref, tab_ref, stage_ref.at[j], t)
            sel = lambda ref: jnp.where(own_chunk, ref[pl.ds(t, 1), :], 0.0).astype(BF16)
            c = jnp.concatenate([sel(ce_hi_ref), sel(ce_lo_ref)], axis=0)
            o = jnp.dot(c, gathered, preferred_element_type=F32)
            o = o[:N_CHUNK] + o[N_CHUNK:]
            out_ref[pl.ds(t, 1), :] = jnp.concatenate([o[r:r + 1, :] for r in range(N_CHUNK)], axis=1)
        return carry

    lax.fori_loop(0, PEER_TT // PEER_TOK_UNROLL, trip, 0)


def _smem_spec():
    return pl.BlockSpec((PEER_TT * PEER_SLOTS,), lambda i: (i,), memory_space=pltpu.SMEM)


def _table_spec(n):
    return pl.BlockSpec((n, ROW_SUB, 128), lambda i: (0, 0, 0), pipeline_mode=pl.Buffered(1))


def peer_u(idx, x, tab):
    t = x.shape[0]
    return pl.pallas_call(
        _peer_u_kernel,
        grid=(t // PEER_TT,),
        in_specs=[_smem_spec(), pl.BlockSpec((PEER_TT, D_MODEL), lambda i: (i, 0)), _table_spec(tab.shape[0])],
        out_specs=pl.BlockSpec((PEER_TT, PEER_SLOTS), lambda i: (i, 0)),
        out_shape=jax.ShapeDtypeStruct((t, PEER_SLOTS), jnp.float32),
        compiler_params=pltpu.CompilerParams(vmem_limit_bytes=VMEM_LIMIT_PEER),
        name="peer_u",
    )(idx.reshape(-1), x, tab)


def peer_v(idx, coef, tab):
    t = idx.shape[0]
    width = PEER_SLOTS * N_CHUNK
    expand = (jnp.arange(width)[None, :] // N_CHUNK == jnp.arange(PEER_SLOTS)[:, None]).astype(BF16)
    return pl.pallas_call(
        _peer_v_kernel,
        grid=(t // PEER_TT,),
        in_specs=[_smem_spec(), pl.BlockSpec((PEER_TT, PEER_SLOTS), lambda i: (i, 0)),
                  pl.BlockSpec((PEER_SLOTS, width), lambda i: (0, 0)), _table_spec(tab.shape[0])],
        out_specs=pl.BlockSpec((PEER_TT, D_MODEL), lambda i: (i, 0)),
        out_shape=jax.ShapeDtypeStruct((t, D_MODEL), jnp.float32),
        scratch_shapes=[pltpu.VMEM((PEER_TOK_UNROLL, STAGE_ROWS, 128), jnp.uint32),
                        pltpu.VMEM((PEER_TT, width), F32), pltpu.VMEM((PEER_TT, width), F32)],
        compiler_params=pltpu.CompilerParams(vmem_limit_bytes=VMEM_LIMIT_PEER),
        name="peer_v",
    )(idx.reshape(-1), coef, expand, tab)


PK_TT = 256
PK_A_BLOCKS = ((0, 16), (1, 8), (2, 5), (3, 4))
PK_B_BLOCKS = ((0, 4, 16), (1, 4, 8), (2, 4, 5))
NEG_HUGE = -3.0e38


def _extract_top(blocks, flats, payloads, n):
    vals, picked = [], []
    for _ in range(n):
        m = functools.reduce(jnp.maximum, [jnp.max(b, axis=0, keepdims=True) for b in blocks])
        pos = functools.reduce(jnp.minimum, [jnp.min(jnp.where(b == m, f, 1e9), axis=0, keepdims=True)
                                             for b, f in zip(blocks, flats)])
        hit = [f == pos for f in flats]
        if payloads is None:
            picked.append(pos)
        else:
            picked.append(functools.reduce(jnp.add, [jnp.sum(jnp.where(h, p, 0.0), axis=0, keepdims=True)
                                                     for h, p in zip(hit, payloads)]))
        blocks = [jnp.where(h, NEG_HUGE, b) for h, b in zip(hit, blocks)]
        vals.append(m)
    return vals, picked


def _peer_topk_kernel(q_ref, ka_ref, eid_ref, w_ref):
    key_row = lax.broadcasted_iota(jnp.int32, (N_KEYS, PK_TT), 0).astype(F32)
    riota = lambda n: lax.broadcasted_iota(jnp.int32, (n, PK_TT), 0).astype(F32)
    flats = ([a * PEER_TOPK + riota(nb) for a, nb in PK_A_BLOCKS]
             + [(a0 + riota(a1 - a0)) * PEER_TOPK + b for b, a0, a1 in PK_B_BLOCKS])

    def head(h, carry):
        tops = []
        for p in range(2):
            q = q_ref[2 * h + p]
            q_hi = _trunc_bf16(q)
            qa = jnp.concatenate([q_hi.astype(BF16), (q - q_hi).astype(BF16), q_hi.astype(BF16)], axis=1)
            s = lax.dot_general(ka_ref[2 * h + p], qa, (((1,), (1,)), ((), ())),
                                preferred_element_type=F32)
            vals, idxs = _extract_top([s], [key_row], None, PEER_TOPK)
            tops.append((jnp.concatenate(vals, axis=0), jnp.concatenate(idxs, axis=0)))
        (s1, i1), (s2, i2) = tops
        blocks = ([s1[a:a + 1] + s2[0:nb] for a, nb in PK_A_BLOCKS]
                  + [s1[a0:a1] + s2[b:b + 1] for b, a0, a1 in PK_B_BLOCKS])
        eids = ([i1[a:a + 1] * N_KEYS + i2[0:nb] for a, nb in PK_A_BLOCKS]
                + [i1[a0:a1] * N_KEYS + i2[b:b + 1] for b, a0, a1 in PK_B_BLOCKS])
        vals, picked = _extract_top(blocks, flats, eids, PEER_TOPK)
        top_s = jnp.concatenate(vals, axis=0)
        e = jnp.exp(top_s - vals[0])
        w_ref[h] = e / jnp.sum(e, axis=0, keepdims=True)
        eid_ref[h] = jnp.concatenate(picked, axis=0).astype(jnp.int32)
        return carry

    lax.fori_loop(0, PEER_HEADS, head, 0)


def peer_topk(qt, subkeys):
    hp, t, d = qt.shape
    k_hi, k_lo = _split2(subkeys.reshape(hp, N_KEYS, d))
    ka = jnp.concatenate([k_hi, k_hi, k_lo], axis=2)
    eid, w = pl.pallas_call(
        _peer_topk_kernel,
        grid=(t // PK_TT,),
        in_specs=[pl.BlockSpec((hp, PK_TT, d), lambda i: (0, i, 0)),
                  pl.BlockSpec((hp, N_KEYS, 3 * d), lambda i: (0, 0, 0))],
        out_specs=[pl.BlockSpec((PEER_HEADS, PEER_TOPK, PK_TT), lambda i: (0, 0, i)),
                   pl.BlockSpec((PEER_HEADS, PEER_TOPK, PK_TT), lambda i: (0, 0, i))],
        out_shape=[jax.ShapeDtypeStruct((PEER_HEADS, PEER_TOPK, t), jnp.int32),
                   jax.ShapeDtypeStruct((PEER_HEADS, PEER_TOPK, t), F32)],
        name="peer_topk",
    )(qt, ka)
    to_slots = lambda a: a.transpose(2, 0, 1).reshape(t, PEER_SLOTS)
    return to_slots(eid), to_slots(w)


def rmsnorm(x, g):
    xf = x.astype(jnp.float32)
    y = xf * lax.rsqrt(jnp.mean(xf * xf, axis=-1, keepdims=True) + RMS_EPS)
    return (y * g.astype(jnp.float32)).astype(x.dtype)


def split_columns(t, sizes):
    offs = np.cumsum((0,) + tuple(sizes))
    return [t[..., int(a):int(b)] for a, b in zip(offs[:-1], offs[1:])]


def rope_tables(pos):
    inv = jnp.power(ROPE_THETA, -jnp.arange(0, ROPE_DIM, 2, dtype=jnp.float32) / ROPE_DIM)
    ang = pos[:, None] * inv[None, :]
    return jnp.cos(ang), jnp.sin(ang)


def apply_partial_rope(x, cos, sin):
    half = ROPE_DIM // 2
    xr = x[..., :ROPE_DIM].astype(jnp.float32)
    x1, x2 = xr[..., :half], xr[..., half:]
    c, s = cos[:, None, :], sin[:, None, :]
    rot = jnp.concatenate([x1 * c - x2 * s, x1 * s + x2 * c], axis=-1)
    return jnp.concatenate([rot.astype(x.dtype), x[..., ROPE_DIM:]], axis=-1)


BF16 = jnp.bfloat16
F32 = jnp.float32


def _flash_step(k, vt, qt, mask, carry):
    m, l, acc = carry
    s = jnp.dot(k, qt, preferred_element_type=F32)
    s_vis = s if mask is None else jnp.where(mask, s, NEG_INF)
    m_new = jnp.maximum(m, jnp.max(s_vis, axis=0, keepdims=True))
    alpha = jnp.exp(m - m_new)
    p = jnp.exp(s - m_new)
    if mask is not None:
        p = jnp.where(mask, p, 0.0)
    l = alpha * l + jnp.sum(p, axis=0, keepdims=True)
    acc = alpha * acc + jnp.dot(vt, p.astype(BF16), preferred_element_type=F32)
    return m_new, l, acc


def _flash_init(n):
    return (jnp.full((1, n), NEG_INF, F32), jnp.zeros((1, n), F32), jnp.zeros((HEAD_DIM, n), F32))


FOX_TQ = 256
FOX_KV = 512
FOX_DK = 256


FOX_HP = 2


def _fox_kernel(qt_ref, k_ref, vt_ref, o_ref):
    qi = pl.program_id(1)
    q0 = qi * FOX_TQ
    qts = [qt_ref[h] for h in range(FOX_HP)]
    qpos = q0 + lax.broadcasted_iota(jnp.int32, (1, FOX_TQ), 1)
    krow = lax.broadcasted_iota(jnp.int32, (FOX_KV, FOX_TQ), 0)

    def full_step(kt, carry):
        return tuple(_flash_step(k_ref[h, kt], vt_ref[h, kt], qts[h], None, carry[h]) for h in range(FOX_HP))

    def diag_step(kt, carry):
        mask = (kt * FOX_KV + krow) <= qpos
        return tuple(_flash_step(k_ref[h, kt], vt_ref[h, kt], qts[h], mask, carry[h]) for h in range(FOX_HP))

    n_full = (q0 + 1) // FOX_KV
    n_all = (q0 + FOX_TQ + FOX_KV - 1) // FOX_KV
    carry = lax.fori_loop(0, n_full, full_step, tuple(_flash_init(FOX_TQ) for _ in range(FOX_HP)))
    carry = lax.fori_loop(n_full, n_all, diag_step, carry)
    for h in range(FOX_HP):
        _, l, acc = carry[h]
        o_ref[h] = acc / l


def _trunc_bf16(x):
    bits = lax.bitcast_convert_type(x, jnp.uint32) & jnp.uint32(0xFFFF0000)
    return lax.bitcast_convert_type(bits, F32)


def _split3(c):
    c1 = _trunc_bf16(c)
    r = c - c1
    c2 = _trunc_bf16(r)
    return c1.astype(BF16), c2.astype(BF16), (r - c2).astype(BF16)


def _split2(x):
    hi = _trunc_bf16(x)
    return hi.astype(BF16), (x - hi).astype(BF16)


def _aug_q(q):
    hi, lo = _split2(q)
    return jnp.concatenate([hi, hi, lo], axis=-1)


def _aug_k(k):
    hi, lo = _split2(k)
    return jnp.concatenate([hi, lo, hi], axis=-1)


def fox_attention(q, k, v, log_f):
    B_, S_, H, Dh = q.shape
    KV_TILE = FOX_KV
    n_qt, n_kt = S_ // FOX_TQ, S_ // KV_TILE
    c = jnp.cumsum(log_f, axis=1)
    c1, c2, c3 = _split3(c)
    one = jnp.ones_like(c1)
    pad = jnp.zeros((B_, S_, H, FOX_DK - 3 * Dh - 6), BF16)
    col = lambda t: t[..., None]
    qa = jnp.concatenate([_aug_q(q * (Dh ** -0.5)), col(one), col(one), col(one),
                          col(c1), col(c2), col(c3), pad], axis=-1)
    ka = jnp.concatenate([_aug_k(k), col(-c1), col(-c2), col(-c3),
                          col(one), col(one), col(one), pad], axis=-1)
    HP, HG = FOX_HP, H // FOX_HP
    qt = qa.reshape(B_, n_qt, FOX_TQ, HG, HP, FOX_DK).transpose(0, 3, 1, 4, 5, 2).reshape(B_ * HG * n_qt, HP, FOX_DK, FOX_TQ)
    kk = ka.reshape(B_, n_kt, KV_TILE, HG, HP, FOX_DK).transpose(0, 3, 4, 1, 2, 5).reshape(B_ * HG, HP, n_kt, KV_TILE, FOX_DK)
    vt = v.astype(BF16).reshape(B_, n_kt, KV_TILE, HG, HP, Dh).transpose(0, 3, 4, 1, 5, 2).reshape(B_ * HG, HP, n_kt, Dh, KV_TILE)
    out = pl.pallas_call(
        _fox_kernel,
        grid=(B_ * HG, n_qt),
        in_specs=[pl.BlockSpec((None, HP, FOX_DK, FOX_TQ), lambda b, i: (b * n_qt + i, 0, 0, 0)),
                  pl.BlockSpec((None, HP, n_kt, KV_TILE, FOX_DK), lambda b, i: (b, 0, 0, 0, 0)),
                  pl.BlockSpec((None, HP, n_kt, Dh, KV_TILE), lambda b, i: (b, 0, 0, 0, 0))],
        out_specs=pl.BlockSpec((None, HP, Dh, FOX_TQ), lambda b, i: (b * n_qt + i, 0, 0, 0)),
        out_shape=jax.ShapeDtypeStruct((B_ * HG * n_qt, HP, Dh, FOX_TQ), F32),
        name="fox_attention",
    )(qt, kk, vt)
    return out.reshape(B_, HG, n_qt, HP, Dh, FOX_TQ).transpose(0, 2, 5, 1, 3, 4).reshape(B_, S_, H * Dh)


def compress_kv(k, v, cmp_pos, w1, w2):
    B_, S_ = k.shape[0], k.shape[1]
    n_cmp = (S_ - CMP_LEN) // CMP_STRIDE + 1
    starts = np.arange(n_cmp) * CMP_STRIDE
    idx = starts[:, None] + np.arange(CMP_LEN)[None, :]

    def phi(t, j):
        blocks = t[:, idx] + cmp_pos[j][None, None, :, None, :]
        flat = jnp.swapaxes(blocks, 2, 3).reshape(B_, n_cmp, NSA_GROUPS, CMP_LEN * HEAD_DIM)
        return jax.nn.gelu(flat @ w1[j], approximate=False) @ w2[j]

    return phi(k, 0), phi(v, 1), jnp.asarray(starts + CMP_LEN - 1, dtype=jnp.int32)


NSA_TQ = 128
NSA_N = NSA_HPG * NSA_TQ
N_CMP_PAD = 256
N_SLC = 64
NSA_DK = 3 * HEAD_DIM
NSA_KV = 512
WIN_TILES = WINDOW // NSA_TQ + 1
WIN_KEYS = WIN_TILES * NSA_TQ


def _nsa_kernel(qt_ref, kc_ref, vct_ref, ks_ref, vst_ref, kw_ref, vwt_ref, gate_ref, ovl_ref, o_ref, sel_ref):
    qi = pl.program_id(1)
    q0 = qi * NSA_TQ
    qt = qt_ref[...]
    lane = lax.broadcasted_iota(jnp.int32, (1, NSA_N), 1)
    qpos = q0 + (lane & (NSA_TQ - 1))

    s = jnp.dot(kc_ref[...], qt, preferred_element_type=F32)
    cmp_end = lax.broadcasted_iota(jnp.int32, (N_CMP_PAD, NSA_N), 0) * CMP_STRIDE + (CMP_LEN - 1)
    mc = cmp_end <= qpos
    m = jnp.max(jnp.where(mc, s, NEG_INF), axis=0, keepdims=True)
    p = jnp.where(mc, jnp.exp(s - m), 0.0)
    l = jnp.sum(p, axis=0, keepdims=True)
    pc = p * jnp.where(l > 0.0, 1.0 / l, 0.0)
    o_cmp = jnp.dot(vct_ref[...], pc.astype(BF16), preferred_element_type=F32)

    pcs = (pc[:, 0:NSA_TQ] + pc[:, NSA_TQ:2 * NSA_TQ]) + (pc[:, 2 * NSA_TQ:3 * NSA_TQ] + pc[:, 3 * NSA_TQ:4 * NSA_TQ])
    pcs_hi = pcs.astype(BF16)
    pcs_lo = (pcs - pcs_hi.astype(F32)).astype(BF16)
    ovl = ovl_ref[...]
    imp = jnp.dot(ovl, pcs_hi, preferred_element_type=F32) + jnp.dot(ovl, pcs_lo, preferred_element_type=F32)
    blk = lax.broadcasted_iota(jnp.int32, (N_SLC, NSA_TQ), 0)
    q_blk = (q0 + lax.broadcasted_iota(jnp.int32, (N_SLC, NSA_TQ), 1)) >> 6
    forced = (blk == 0) | (blk == q_blk) | (blk == q_blk - 1)
    score = jnp.where(forced, FORCED_SCORE, jnp.where(blk <= q_blk, imp, -1.0))
    rank = jnp.zeros((N_SLC, NSA_TQ), jnp.int32)
    for i in range(N_SLC):
        row = score[i:i + 1, :]
        beats = (row > score) | ((row == score) & (blk > i))
        rank = rank + beats.astype(jnp.int32)
    sel = (rank < SLC_TOP).astype(F32)
    sel_ref[...] = jnp.concatenate([sel] * (NSA_HPG // 2), axis=1)

    hn = NSA_N // 2
    qts = (qt[:, :hn], qt[:, hn:])
    qpos_h = qpos[:, :hn]

    w0 = jnp.maximum(qi - WINDOW // NSA_TQ, 0)
    k_w = jnp.concatenate([kw_ref[w0 + i] for i in range(WIN_TILES)], axis=0)
    vt_w = jnp.concatenate([vwt_ref[w0 + i] for i in range(WIN_TILES)], axis=1)
    rel = qpos_h - (w0 * NSA_TQ + lax.broadcasted_iota(jnp.int32, (WIN_KEYS, hn), 0))
    in_win = (rel >= 0) & (rel < WINDOW)
    outs_w = []
    for h in range(2):
        s = jnp.dot(k_w, qts[h], preferred_element_type=F32)
        m = jnp.max(jnp.where(in_win, s, NEG_INF), axis=0, keepdims=True)
        p = jnp.where(in_win, jnp.exp(s - m), 0.0)
        l = jnp.sum(p, axis=0, keepdims=True)
        outs_w.append(jnp.dot(vt_w, p.astype(BF16), preferred_element_type=F32) / l)
    acc_w = jnp.concatenate(outs_w, axis=1)

    krow = lax.broadcasted_iota(jnp.int32, (NSA_KV, hn), 0)
    blocks_per_tile = NSA_KV // SLC_LEN

    def slc_chains(kt, carry):
        rows = [jnp.broadcast_to(sel_ref[pl.ds(blocks_per_tile * kt + j, 1), :], (SLC_LEN, hn))
                for j in range(blocks_per_tile)]
        mask = (jnp.concatenate(rows, axis=0) > 0.0) & ((kt * NSA_KV + krow) <= qpos_h)
        k, vt = ks_ref[kt], vst_ref[kt]
        return tuple(_flash_step(k, vt, qts[h], mask, carry[h]) for h in range(2))

    slc = lax.fori_loop(0, (q0 + NSA_TQ - 1) // NSA_KV + 1, slc_chains, (_flash_init(hn), _flash_init(hn)))
    acc_s = jnp.concatenate([slc[0][2] / slc[0][1], slc[1][2] / slc[1][1]], axis=1)

    g = jax.nn.sigmoid(gate_ref[...])
    o_ref[...] = g[0:1] * o_cmp + g[1:2] * acc_s + g[2:3] * acc_w


def nsa_attention(q, k_cmp, v_cmp, k_slc, v_slc, k_win, v_win, gate_logit):
    B_, S_, H, Dh = q.shape
    G = NSA_GROUPS
    n_qt = S_ // NSA_TQ
    n_cmp = k_cmp.shape[1]
    cs = np.arange(N_CMP_PAD)[None, :] * CMP_STRIDE
    ss = np.arange(N_SLC)[:, None] * SLC_LEN
    ov = np.clip(np.minimum(cs + CMP_LEN, ss + SLC_LEN) - np.maximum(cs, ss), 0, None) / CMP_LEN
    ov[:, n_cmp:] = 0.0
    ovl = jnp.asarray(ov, dtype=BF16)
    qs = _aug_q(q * (Dh ** -0.5)).reshape(B_, n_qt, NSA_TQ, G, NSA_HPG, NSA_DK)
    qt = qs.transpose(0, 3, 1, 5, 4, 2).reshape(B_ * G * n_qt, NSA_DK, NSA_N)
    gt = gate_logit.astype(F32).reshape(B_, n_qt, NSA_TQ, G, NSA_HPG, N_NSA_BRANCH)
    gt = gt.transpose(0, 3, 1, 5, 4, 2).reshape(B_ * G * n_qt, N_NSA_BRANCH, NSA_N)
    keys = lambda t, kv: _aug_k(t).reshape(B_, S_ // kv, kv, G, NSA_DK).transpose(0, 3, 1, 2, 4).reshape(B_ * G, S_ // kv, kv, NSA_DK)
    vals = lambda t, kv: t.astype(BF16).reshape(B_, S_ // kv, kv, G, Dh).transpose(0, 3, 1, 4, 2).reshape(B_ * G, S_ // kv, Dh, kv)
    padc = ((0, 0), (0, N_CMP_PAD - n_cmp), (0, 0), (0, 0))
    kc = _aug_k(jnp.pad(k_cmp, padc)).transpose(0, 2, 1, 3).reshape(B_ * G, N_CMP_PAD, NSA_DK)
    vct = jnp.pad(v_cmp, padc).astype(BF16).transpose(0, 2, 3, 1).reshape(B_ * G, Dh, N_CMP_PAD)
    per_bg = lambda *blk: pl.BlockSpec((None,) + blk, lambda b, i: (b,) + (0,) * len(blk))
    per_tile = lambda *blk: pl.BlockSpec((None,) + blk, lambda b, i: (b * n_qt + i,) + (0,) * len(blk))
    out = pl.pallas_call(
        _nsa_kernel,
        grid=(B_ * G, n_qt),
        in_specs=[per_tile(NSA_DK, NSA_N),
                  per_bg(N_CMP_PAD, NSA_DK), per_bg(Dh, N_CMP_PAD),
                  per_bg(S_ // NSA_KV, NSA_KV, NSA_DK), per_bg(S_ // NSA_KV, Dh, NSA_KV),
                  per_bg(n_qt, NSA_TQ, NSA_DK), per_bg(n_qt, Dh, NSA_TQ),
                  per_tile(N_NSA_BRANCH, NSA_N),
                  pl.BlockSpec((N_SLC, N_CMP_PAD), lambda b, i: (0, 0))],
        out_specs=per_tile(Dh, NSA_N),
        out_shape=jax.ShapeDtypeStruct((B_ * G * n_qt, Dh, NSA_N), F32),
        scratch_shapes=[pltpu.VMEM((N_SLC, NSA_N // 2), F32)],
        name="nsa_attention",
    )(qt, kc, vct, keys(k_slc, NSA_KV), vals(v_slc, NSA_KV), keys(k_win, NSA_TQ), vals(v_win, NSA_TQ), gt, ovl)
    return out.reshape(B_, G, n_qt, Dh, NSA_HPG, NSA_TQ).transpose(0, 2, 5, 1, 4, 3).reshape(B_, S_, H * Dh)


DENSE_TM = 256
SEC_QKV = 4 * BRANCH_WIDTH
SEC_KV = 6 * KV_WIDTH
SEC_MERGE = N_BRANCH * D_MODEL
SEC_SMALL = 256
IN_SECTIONS = (SEC_QKV, SEC_KV, SEC_MERGE, SEC_SMALL)


def _rms(x, g):
    return x * lax.rsqrt(jnp.mean(x * x, axis=-1, keepdims=True) + RMS_EPS) * g


def _row_spec(width):
    return pl.BlockSpec((DENSE_TM, width), lambda i: (i, 0))


def _whole_spec(shape):
    return pl.BlockSpec(shape, lambda i: (0,) * len(shape))


def _in_proj_kernel(x_ref, g_ref, w_ref, *o_refs):
    xn = _rms(x_ref[...], g_ref[...]).astype(BF16)
    off = 0
    for o_ref, width in zip(o_refs, IN_SECTIONS):
        o_ref[...] = jnp.dot(xn, w_ref[:, off:off + width], preferred_element_type=F32)
        off += width


def in_projection(x, g, w_in):
    t, d = x.shape
    fq, fk, fv, fl, nq, kc, vc, ksl, vsl, kwn, vwn, gl, ml = split_columns(w_in, IN_SPLITS)
    pad = lambda w: jnp.pad(w, ((0, 0), (0, 128 - w.shape[1])))
    w = jnp.concatenate([fq, fk, fv, nq, kc, vc, ksl, vsl, kwn, vwn, ml, pad(fl), pad(gl)], axis=1).astype(BF16)
    return pl.pallas_call(
        _in_proj_kernel,
        grid=(t // DENSE_TM,),
        in_specs=[_row_spec(d), _whole_spec((1, d)), _whole_spec(w.shape)],
        out_specs=[_row_spec(s) for s in IN_SECTIONS],
        out_shape=[jax.ShapeDtypeStruct((t, s), F32) for s in IN_SECTIONS],
        name="in_projection",
    )(x, g.reshape(1, d), w)


def _merge_kernel(x_ref, yf_ref, yn_ref, ml_ref, wb_ref, wo_ref, o_ref):
    g = jax.nn.sigmoid(ml_ref[...])
    up_f = jnp.dot(yf_ref[...].astype(BF16), wb_ref[0], preferred_element_type=F32)
    up_n = jnp.dot(yn_ref[...].astype(BF16), wb_ref[1], preferred_element_type=F32)
    merged = g[:, :D_MODEL] * up_f + g[:, D_MODEL:] * up_n
    o_ref[...] = x_ref[...] + jnp.dot(merged.astype(BF16), wo_ref[...], preferred_element_type=F32)


def merge_out(x, y_fox, y_nsa, merge_logit, w_branch, w_out):
    t, d = x.shape
    return pl.pallas_call(
        _merge_kernel,
        grid=(t // DENSE_TM,),
        in_specs=[_row_spec(d), _row_spec(BRANCH_WIDTH), _row_spec(BRANCH_WIDTH), _row_spec(SEC_MERGE),
                  _whole_spec(w_branch.shape), _whole_spec(w_out.shape)],
        out_specs=_row_spec(d),
        out_shape=jax.ShapeDtypeStruct((t, d), F32),
        name="merge_out",
    )(x, y_fox, y_nsa, merge_logit, w_branch.astype(BF16), w_out.astype(BF16))


def _peer_q_kernel(x_ref, g_ref, w_ref, hn_ref, q_ref):
    hn = _rms(x_ref[...], g_ref[...])
    hn_ref[...] = hn
    q = jnp.dot(hn.astype(BF16), w_ref[...], preferred_element_type=F32)
    d = PEER_QDIM // 2
    for j in range(2 * PEER_HEADS):
        q_ref[j] = q[:, j * d:(j + 1) * d]


def peer_query(x, g, wq):
    t, d = x.shape
    hp, dq = 2 * PEER_HEADS, PEER_QDIM // 2
    return pl.pallas_call(
        _peer_q_kernel,
        grid=(t // DENSE_TM,),
        in_specs=[_row_spec(d), _whole_spec((1, d)), _whole_spec(wq.shape)],
        out_specs=[_row_spec(d), pl.BlockSpec((hp, DENSE_TM, dq), lambda i: (0, i, 0))],
        out_shape=[jax.ShapeDtypeStruct((t, d), F32), jax.ShapeDtypeStruct((hp, t, dq), F32)],
        name="peer_query",
    )(x, g.reshape(1, d), wq.astype(BF16))


def _final_kernel(h_ref, p_ref, g_ref, o_ref):
    o_ref[...] = _rms(h_ref[...] + p_ref[...], g_ref[...])


def final_norm(h, p, g):
    t, d = h.shape
    return pl.pallas_call(
        _final_kernel,
        grid=(t // DENSE_TM,),
        in_specs=[_row_spec(d), _row_spec(d), _whole_spec((1, d))],
        out_specs=_row_spec(d),
        out_shape=jax.ShapeDtypeStruct((t, d), F32),
        name="final_norm",
    )(h, p, g.reshape(1, d))


N_CMP_ROWS = 256


def _compress_kernel(r_ref, pos_ref, w1_ref, w2_ref, o_ref):
    r = r_ref[...]
    nxt = jnp.concatenate([r[1:], jnp.zeros((1, r.shape[1]), F32)], axis=0)
    half = r.shape[1]
    hid = (jnp.dot((r + pos_ref[0:1, :]).astype(BF16), w1_ref[:half], preferred_element_type=F32)
           + jnp.dot((nxt + pos_ref[1:2, :]).astype(BF16), w1_ref[half:], preferred_element_type=F32))
    hid = 0.5 * hid * (1.0 + lax.erf(hid * (2.0 ** -0.5)))
    o_ref[...] = jnp.dot(hid.astype(BF16), w2_ref[...], preferred_element_type=F32)


def compress_kv_pallas(kv, cmp_pos, w1, w2):
    _, B_, S_, G, Dh = kv.shape
    n_cmp = (S_ - CMP_LEN) // CMP_STRIDE + 1
    width = CMP_STRIDE * Dh
    rows = kv.transpose(0, 1, 3, 2, 4).reshape(2, B_ * G, S_ // CMP_STRIDE, width)
    out = pl.pallas_call(
        _compress_kernel,
        grid=(2, B_ * G),
        in_specs=[pl.BlockSpec((None, None, N_CMP_ROWS, width), lambda j, b: (j, b, 0, 0)),
                  pl.BlockSpec((None, 2, width), lambda j, b: (j, 0, 0)),
                  pl.BlockSpec((None, CMP_LEN * Dh, CMP_HIDDEN), lambda j, b: (j, 0, 0)),
                  pl.BlockSpec((None, CMP_HIDDEN, Dh), lambda j, b: (j, 0, 0))],
        out_specs=pl.BlockSpec((None, None, N_CMP_ROWS, Dh), lambda j, b: (j, b, 0, 0)),
        out_shape=jax.ShapeDtypeStruct((2, B_ * G, N_CMP_ROWS, Dh), F32),
        name="compress_kv",
    )(rows, cmp_pos.reshape(2, 2, width), w1.astype(BF16), w2.astype(BF16))
    return out[:, :, :n_cmp].reshape(2, B_, G, n_cmp, Dh).transpose(0, 1, 3, 2, 4)


def hybrid_mixer(x, norm_g, w_in, fox_f_bias, cmp_pos, cmp_w1, cmp_w2, w_branch, w_out):
    B_, S_, D = x.shape
    xf = x.reshape(B_ * S_, D)
    qkv, kvs, merge_logit, small = in_projection(xf, norm_g, w_in)
    heads = lambda t, n: t.reshape(B_, S_, n, HEAD_DIM)
    fq, fk, fv, nq = (qkv[:, i * BRANCH_WIDTH:(i + 1) * BRANCH_WIDTH] for i in range(4))
    kc, vc, ksl, vsl, kwn, vwn = (heads(kvs[:, i * KV_WIDTH:(i + 1) * KV_WIDTH], NSA_GROUPS) for i in range(6))
    f_logit = small[:, :FOX_HEADS].reshape(B_, S_, FOX_HEADS)
    gate_logit = small[:, 128:128 + NSA_HEADS * N_NSA_BRANCH].reshape(B_, S_, NSA_HEADS, N_NSA_BRANCH)
    log_f = jax.nn.log_sigmoid(f_logit + fox_f_bias.astype(F32))
    y_fox = fox_attention(heads(fq, FOX_HEADS), heads(fk, FOX_HEADS), heads(fv, FOX_HEADS), log_f)
    cos, sin = rope_tables(jnp.arange(S_, dtype=F32))
    q_nsa = apply_partial_rope(heads(nq, NSA_HEADS), cos, sin)
    k_slc = apply_partial_rope(ksl, cos, sin)
    k_win = apply_partial_rope(kwn, cos, sin)
    cmp = compress_kv_pallas(jnp.stack([kc, vc]), cmp_pos, cmp_w1, cmp_w2)
    n_cmp = cmp.shape[2]
    cmp_end = jnp.arange(n_cmp, dtype=F32) * CMP_STRIDE + (CMP_LEN - 1)
    k_cmp = apply_partial_rope(cmp[0], *rope_tables(cmp_end))
    y_nsa = nsa_attention(q_nsa, k_cmp, cmp[1], k_slc, vsl, k_win, vwn, gate_logit)
    return merge_out(xf, y_fox.reshape(B_ * S_, -1), y_nsa.reshape(B_ * S_, -1), merge_logit, w_branch, w_out)


def peer_ffn(h, norm_g, wq, subkeys, u, v):
    hn, q = peer_query(h, norm_g, wq)
    idx2, w = peer_topk(q, subkeys)
    act = jax.nn.gelu(peer_u(idx2, hn, pack_table(u)), approximate=False)
    return peer_v(idx2, w * act, pack_table(v))


def kernel(x, norm_mix, w_in, fox_f_bias, nsa_cmp_pos, nsa_cmp_w1, nsa_cmp_w2, w_branch, w_out,
           norm_ffn, peer_wq, peer_subkeys, peer_u, peer_v, norm_final):
    B_, S_, D = x.shape
    assert norm_mix.shape[0] == 1, "single-layer trunk"
    h = hybrid_mixer(x, norm_mix[0], w_in[0], fox_f_bias[0], nsa_cmp_pos[0], nsa_cmp_w1[0], nsa_cmp_w2[0],
                     w_branch[0], w_out[0])
    p = peer_ffn(h, norm_ffn[0], peer_wq[0], peer_subkeys[0], peer_u[0], peer_v[0])
    return final_norm(h, p, norm_final).reshape(B_, S_, D)
```

```python
import functools

import jax
import jax.numpy as jnp
import numpy as np
from jax import lax
from jax.experimental import pallas as pl
from jax.experimental.pallas import tpu as pltpu

D_MODEL = 1024
HEAD_DIM = 64
FOX_HEADS = 8
NSA_HEADS = 8
NSA_GROUPS = 2
NSA_HPG = NSA_HEADS // NSA_GROUPS
BRANCH_WIDTH = 512
N_BRANCH = 2
N_NSA_BRANCH = 3
ROPE_DIM = HEAD_DIM // 4
ROPE_THETA = 500000.0
Q_BLOCK = 128
CMP_LEN = 32
CMP_STRIDE = 16
CMP_HIDDEN = 2 * HEAD_DIM
SLC_LEN = 64
SLC_TOP = 16
WINDOW = 512
FORCED_SCORE = 1e9
NEG_INF = -1e30
PEER_HEADS = 8
N_KEYS = 128
N_EXPERTS = N_KEYS * N_KEYS
PEER_QDIM = 256
PEER_TOPK = 16
PEER_CHUNK = 128
RMS_EPS = 1e-6
KV_WIDTH = NSA_GROUPS * HEAD_DIM
IN_SPLITS = (BRANCH_WIDTH, BRANCH_WIDTH, BRANCH_WIDTH, FOX_HEADS, BRANCH_WIDTH,
             KV_WIDTH, KV_WIDTH, KV_WIDTH, KV_WIDTH, KV_WIDTH, KV_WIDTH,
             NSA_HEADS * N_NSA_BRANCH, N_BRANCH * D_MODEL)


def _rmsnorm_kernel(x_ref, g_ref, o_ref):
    x = x_ref[...]
    ms = jnp.mean(x * x, axis=-1, keepdims=True)
    o_ref[...] = x * lax.rsqrt(ms + RMS_EPS) * g_ref[...]


def rmsnorm_pallas(x2d, g, rows=512):
    t, d = x2d.shape
    return pl.pallas_call(
        _rmsnorm_kernel,
        grid=(t // rows,),
        in_specs=[pl.BlockSpec((rows, d), lambda i: (i, 0)),
                  pl.BlockSpec((1, d), lambda i: (0, 0))],
        out_specs=pl.BlockSpec((rows, d), lambda i: (i, 0)),
        out_shape=jax.ShapeDtypeStruct((t, d), jnp.float32),
    )(x2d, g.reshape(1, d))


PEER_SLOTS = PEER_HEADS * PEER_TOPK
PEER_TT = 64
PEER_CHUNK_SLOTS = 16
HALF_D = D_MODEL // 2
ROW_SUB = HALF_D // 128
VMEM_LIMIT_PEER = 48 * 1024 * 1024


def pack_table(tab):
    n = tab.shape[0]
    b = lax.bitcast_convert_type(tab.astype(jnp.bfloat16), jnp.uint16).astype(jnp.uint32)
    b = b.reshape(n, ROW_SUB, 2, 128)
    return b[:, :, 0, :] | (b[:, :, 1, :] << 16)


def _unpack(w):
    lo = lax.bitcast_convert_type(w << 16, jnp.float32)
    hi = lax.bitcast_convert_type(w & jnp.uint32(0xFFFF0000), jnp.float32)
    return lo, hi


def _peer_u_kernel(idx_ref, x_ref, tab_ref, out_ref):
    lane = lax.broadcasted_iota(jnp.int32, (ROW_SUB, 128), 1)

    def tok(t, carry):
        row = x_ref[pl.ds(t, 1), :]
        chunks = [row[:, c * 128:(c + 1) * 128] for c in range(2 * ROW_SUB)]
        xl = jnp.concatenate(chunks[0::2], axis=0)
        xh = jnp.concatenate(chunks[1::2], axis=0)
        acc = jnp.zeros((ROW_SUB, 128), jnp.float32)
        base = t * PEER_SLOTS
        for s in range(PEER_SLOTS):
            lo, hi = _unpack(tab_ref[idx_ref[base + s]])
            part = jnp.sum(lo * xl + hi * xh, axis=1, keepdims=True)
            acc = jnp.where(lane == s, part, acc)
        out_ref[pl.ds(t, 1), :] = jnp.sum(acc, axis=0, keepdims=True)
        return carry

    lax.fori_loop(0, PEER_TT, tok, 0)


def _peer_v_kernel(idx_ref, coef_ref, tab_ref, out_ref):
    n_acc = 4
    zero = jnp.zeros((ROW_SUB, 128), jnp.float32)

    def tok(t, carry):
        def chunk(c, accs):
            acc_lo, acc_hi = list(accs[:n_acc]), list(accs[n_acc:])
            base = t * PEER_SLOTS + c * PEER_CHUNK_SLOTS
            for j in range(PEER_CHUNK_SLOTS):
                lo, hi = _unpack(tab_ref[idx_ref[base + j]])
                cf = coef_ref[base + j]
                acc_lo[j % n_acc] = acc_lo[j % n_acc] + cf * lo
                acc_hi[j % n_acc] = acc_hi[j % n_acc] + cf * hi
            return tuple(acc_lo + acc_hi)

        accs = lax.fori_loop(0, PEER_SLOTS // PEER_CHUNK_SLOTS, chunk, (zero,) * (2 * n_acc))
        even = (accs[0] + accs[1]) + (accs[2] + accs[3])
        odd = (accs[4] + accs[5]) + (accs[6] + accs[7])
        out_ref[pl.ds(t, 1), :] = jnp.concatenate(
            [half[r:r + 1, :] for r in range(ROW_SUB) for half in (even, odd)], axis=1)
        return carry

    lax.fori_loop(0, PEER_TT, tok, 0)


def _smem_spec():
    return pl.BlockSpec((PEER_TT * PEER_SLOTS,), lambda i: (i,), memory_space=pltpu.SMEM)


def _table_spec(n):
    return pl.BlockSpec((n, ROW_SUB, 128), lambda i: (0, 0, 0), pipeline_mode=pl.Buffered(1))


def peer_u(idx, x, tab):
    t = x.shape[0]
    return pl.pallas_call(
        _peer_u_kernel,
        grid=(t // PEER_TT,),
        in_specs=[_smem_spec(), pl.BlockSpec((PEER_TT, D_MODEL), lambda i: (i, 0)), _table_spec(tab.shape[0])],
        out_specs=pl.BlockSpec((PEER_TT, PEER_SLOTS), lambda i: (i, 0)),
        out_shape=jax.ShapeDtypeStruct((t, PEER_SLOTS), jnp.float32),
        compiler_params=pltpu.CompilerParams(vmem_limit_bytes=VMEM_LIMIT_PEER),
        name="peer_u",
    )(idx.reshape(-1), x, tab)


def peer_v(idx, coef, tab):
    t = idx.shape[0]
    return pl.pallas_call(
        _peer_v_kernel,
        grid=(t // PEER_TT,),
        in_specs=[_smem_spec(), _smem_spec(), _table_spec(tab.shape[0])],
        out_specs=pl.BlockSpec((PEER_TT, D_MODEL), lambda i: (i, 0)),
        out_shape=jax.ShapeDtypeStruct((t, D_MODEL), jnp.float32),
        compiler_params=pltpu.CompilerParams(vmem_limit_bytes=VMEM_LIMIT_PEER),
        name="peer_v",
    )(idx.reshape(-1), coef.reshape(-1), tab)


PK_TT = 256
PK_A_BLOCKS = ((0, 16), (1, 8), (2, 5), (3, 4))
PK_B_BLOCKS = ((0, 4, 16), (1, 4, 8), (2, 4, 5))
NEG_HUGE = -3.0e38


def _extract_top(blocks, flats, payloads, n):
    vals, picked = [], []
    for _ in range(n):
        m = functools.reduce(jnp.maximum, [jnp.max(b, axis=0, keepdims=True) for b in blocks])
        pos = functools.reduce(jnp.minimum, [jnp.min(jnp.where(b == m, f, 1e9), axis=0, keepdims=True)
                                             for b, f in zip(blocks, flats)])
        hit = [f == pos for f in flats]
        if payloads is None:
            picked.append(pos)
        else:
            picked.append(functools.reduce(jnp.add, [jnp.sum(jnp.where(h, p, 0.0), axis=0, keepdims=True)
                                                     for h, p in zip(hit, payloads)]))
        blocks = [jnp.where(h, NEG_HUGE, b) for h, b in zip(hit, blocks)]
        vals.append(m)
    return vals, picked


def _candidate_grid(first, second, combine):
    up8 = lambda n: -(-n // 8) * 8
    return jnp.concatenate([combine(first[a:a + 1], second[0:up8(nb)]) for a, nb in PK_A_BLOCKS]
                           + [combine(first[0:up8(a1)], second[b:b + 1]) for b, _, a1 in PK_B_BLOCKS], axis=0)


def _candidate_valid_and_flat():
    up8 = lambda n: -(-n // 8) * 8
    valid, flat = [], []
    for a, nb in PK_A_BLOCKS:
        r = lax.broadcasted_iota(jnp.int32, (up8(nb), PK_TT), 0)
        valid.append(r < nb)
        flat.append(a * PEER_TOPK + r)
    for b, a0, a1 in PK_B_BLOCKS:
        r = lax.broadcasted_iota(jnp.int32, (up8(a1), PK_TT), 0)
        valid.append((r >= a0) & (r < a1))
        flat.append(r * PEER_TOPK + b)
    valid = jnp.concatenate(valid, axis=0)
    return valid, jnp.where(valid, jnp.concatenate(flat, axis=0).astype(F32), 1e9)


def _peer_topk_kernel(q_ref, ka_ref, eid_ref, w_ref):
    key_row = lax.broadcasted_iota(jnp.int32, (N_KEYS, PK_TT), 0).astype(F32)
    valid, flat = _candidate_valid_and_flat()

    def head(h, carry):
        tops = []
        for p in range(2):
            q = q_ref[2 * h + p]
            q_hi = _trunc_bf16(q)
            qa = jnp.concatenate([q_hi.astype(BF16), (q - q_hi).astype(BF16), q_hi.astype(BF16)], axis=1)
            s = lax.dot_general(ka_ref[2 * h + p], qa, (((1,), (1,)), ((), ())),
                                preferred_element_type=F32)
            vals, idxs = _extract_top([s], [key_row], None, PEER_TOPK)
            tops.append((jnp.concatenate(vals, axis=0), jnp.concatenate(idxs, axis=0)))
        (s1, i1), (s2, i2) = tops
        cand = jnp.where(valid, _candidate_grid(s1, s2, lambda x, y: x + y), NEG_HUGE)
        eids = _candidate_grid(i1, i2, lambda x, y: x * N_KEYS + y)
        vals, picked = _extract_top([cand], [flat], [eids], PEER_TOPK)
        top_s = jnp.concatenate(vals, axis=0)
        e = jnp.exp(top_s - vals[0])
        w_ref[h] = e / jnp.sum(e, axis=0, keepdims=True)
        eid_ref[h] = jnp.concatenate(picked, axis=0).astype(jnp.int32)
        return carry

    lax.fori_loop(0, PEER_HEADS, head, 0)


def peer_topk(qt, subkeys):
    hp, t, d = qt.shape
    k_hi, k_lo = _split2(subkeys.reshape(hp, N_KEYS, d))
    ka = jnp.concatenate([k_hi, k_hi, k_lo], axis=2)
    eid, w = pl.pallas_call(
        _peer_topk_kernel,
        grid=(t // PK_TT,),
        in_specs=[pl.BlockSpec((hp, PK_TT, d), lambda i: (0, i, 0)),
                  pl.BlockSpec((hp, N_KEYS, 3 * d), lambda i: (0, 0, 0))],
        out_specs=[pl.BlockSpec((PEER_HEADS, PEER_TOPK, PK_TT), lambda i: (0, 0, i)),
                   pl.BlockSpec((PEER_HEADS, PEER_TOPK, PK_TT), lambda i: (0, 0, i))],
        out_shape=[jax.ShapeDtypeStruct((PEER_HEADS, PEER_TOPK, t), jnp.int32),
                   jax.ShapeDtypeStruct((PEER_HEADS, PEER_TOPK, t), F32)],
        name="peer_topk",
    )(qt, ka)
    to_slots = lambda a: a.transpose(2, 0, 1).reshape(t, PEER_SLOTS)
    return to_slots(eid), to_slots(w)


def rmsnorm(x, g):
    xf = x.astype(jnp.float32)
    y = xf * lax.rsqrt(jnp.mean(xf * xf, axis=-1, keepdims=True) + RMS_EPS)
    return (y * g.astype(jnp.float32)).astype(x.dtype)


def split_columns(t, sizes):
    offs = np.cumsum((0,) + tuple(sizes))
    return [t[..., int(a):int(b)] for a, b in zip(offs[:-1], offs[1:])]


def rope_tables(pos):
    inv = jnp.power(ROPE_THETA, -jnp.arange(0, ROPE_DIM, 2, dtype=jnp.float32) / ROPE_DIM)
    ang = pos[:, None] * inv[None, :]
    return jnp.cos(ang), jnp.sin(ang)


def apply_partial_rope(x, cos, sin):
    half = ROPE_DIM // 2
    xr = x[..., :ROPE_DIM].astype(jnp.float32)
    x1, x2 = xr[..., :half], xr[..., half:]
    c, s = cos[:, None, :], sin[:, None, :]
    rot = jnp.concatenate([x1 * c - x2 * s, x1 * s + x2 * c], axis=-1)
    return jnp.concatenate([rot.astype(x.dtype), x[..., ROPE_DIM:]], axis=-1)


BF16 = jnp.bfloat16
F32 = jnp.float32


def _flash_step(k, vt, qt, mask, carry):
    m, l, acc = carry
    s = jnp.dot(k, qt, preferred_element_type=F32)
    s_vis = s if mask is None else jnp.where(mask, s, NEG_INF)
    m_new = jnp.maximum(m, jnp.max(s_vis, axis=0, keepdims=True))
    alpha = jnp.exp(m - m_new)
    p = jnp.exp(s - m_new)
    if mask is not None:
        p = jnp.where(mask, p, 0.0)
    l = alpha * l + jnp.sum(p, axis=0, keepdims=True)
    acc = alpha * acc + jnp.dot(vt, p.astype(BF16), preferred_element_type=F32)
    return m_new, l, acc


def _flash_init(n):
    return (jnp.full((1, n), NEG_INF, F32), jnp.zeros((1, n), F32), jnp.zeros((HEAD_DIM, n), F32))


FOX_TQ = 256
FOX_KV = 512
FOX_DK = 256


FOX_HP = 2


def _fox_kernel(qt_ref, k_ref, vt_ref, o_ref):
    qi = pl.program_id(1)
    q0 = qi * FOX_TQ
    qts = [qt_ref[h] for h in range(FOX_HP)]
    qpos = q0 + lax.broadcasted_iota(jnp.int32, (1, FOX_TQ), 1)
    krow = lax.broadcasted_iota(jnp.int32, (FOX_KV, FOX_TQ), 0)

    def full_step(kt, carry):
        return tuple(_flash_step(k_ref[h, kt], vt_ref[h, kt], qts[h], None, carry[h]) for h in range(FOX_HP))

    def diag_step(kt, carry):
        mask = (kt * FOX_KV + krow) <= qpos
        return tuple(_flash_step(k_ref[h, kt], vt_ref[h, kt], qts[h], mask, carry[h]) for h in range(FOX_HP))

    n_full = (q0 + 1) // FOX_KV
    n_all = (q0 + FOX_TQ + FOX_KV - 1) // FOX_KV
    carry = lax.fori_loop(0, n_full, full_step, tuple(_flash_init(FOX_TQ) for _ in range(FOX_HP)))
    carry = lax.fori_loop(n_full, n_all, diag_step, carry)
    for h in range(FOX_HP):
        _, l, acc = carry[h]
        o_ref[h] = acc / l


def _trunc_bf16(x):
    bits = lax.bitcast_convert_type(x, jnp.uint32) & jnp.uint32(0xFFFF0000)
    return lax.bitcast_convert_type(bits, F32)


def _split3(c):
    c1 = _trunc_bf16(c)
    r = c - c1
    c2 = _trunc_bf16(r)
    return c1.astype(BF16), c2.astype(BF16), (r - c2).astype(BF16)


def _split2(x):
    hi = _trunc_bf16(x)
    return hi.astype(BF16), (x - hi).astype(BF16)


def _aug_q(q):
    hi, lo = _split2(q)
    return jnp.concatenate([hi, hi, lo], axis=-1)


def _aug_k(k):
    hi, lo = _split2(k)
    return jnp.concatenate([hi, lo, hi], axis=-1)


def fox_attention(q, k, v, log_f):
    B_, S_, H, Dh = q.shape
    KV_TILE = FOX_KV
    n_qt, n_kt = S_ // FOX_TQ, S_ // KV_TILE
    c = jnp.cumsum(log_f, axis=1)
    c1, c2, c3 = _split3(c)
    one = jnp.ones_like(c1)
    pad = jnp.zeros((B_, S_, H, FOX_DK - 3 * Dh - 6), BF16)
    col = lambda t: t[..., None]
    qa = jnp.concatenate([_aug_q(q * (Dh ** -0.5)), col(one), col(one), col(one),
                          col(c1), col(c2), col(c3), pad], axis=-1)
    ka = jnp.concatenate([_aug_k(k), col(-c1), col(-c2), col(-c3),
                          col(one), col(one), col(one), pad], axis=-1)
    HP, HG = FOX_HP, H // FOX_HP
    qt = qa.reshape(B_, n_qt, FOX_TQ, HG, HP, FOX_DK).transpose(0, 3, 1, 4, 5, 2).reshape(B_ * HG * n_qt, HP, FOX_DK, FOX_TQ)
    kk = ka.reshape(B_, n_kt, KV_TILE, HG, HP, FOX_DK).transpose(0, 3, 4, 1, 2, 5).reshape(B_ * HG, HP, n_kt, KV_TILE, FOX_DK)
    vt = v.astype(BF16).reshape(B_, n_kt, KV_TILE, HG, HP, Dh).transpose(0, 3, 4, 1, 5, 2).reshape(B_ * HG, HP, n_kt, Dh, KV_TILE)
    out = pl.pallas_call(
        _fox_kernel,
        grid=(B_ * HG, n_qt),
        in_specs=[pl.BlockSpec((None, HP, FOX_DK, FOX_TQ), lambda b, i: (b * n_qt + i, 0, 0, 0)),
                  pl.BlockSpec((None, HP, n_kt, KV_TILE, FOX_DK), lambda b, i: (b, 0, 0, 0, 0)),
                  pl.BlockSpec((None, HP, n_kt, Dh, KV_TILE), lambda b, i: (b, 0, 0, 0, 0))],
        out_specs=pl.BlockSpec((None, HP, Dh, FOX_TQ), lambda b, i: (b * n_qt + i, 0, 0, 0)),
        out_shape=jax.ShapeDtypeStruct((B_ * HG * n_qt, HP, Dh, FOX_TQ), F32),
        name="fox_attention",
    )(qt, kk, vt)
    return out.reshape(B_, HG, n_qt, HP, Dh, FOX_TQ).transpose(0, 2, 5, 1, 3, 4).reshape(B_, S_, H * Dh)


def compress_kv(k, v, cmp_pos, w1, w2):
    B_, S_ = k.shape[0], k.shape[1]
    n_cmp = (S_ - CMP_LEN) // CMP_STRIDE + 1
    starts = np.arange(n_cmp) * CMP_STRIDE
    idx = starts[:, None] + np.arange(CMP_LEN)[None, :]

    def phi(t, j):
        blocks = t[:, idx] + cmp_pos[j][None, None, :, None, :]
        flat = jnp.swapaxes(blocks, 2, 3).reshape(B_, n_cmp, NSA_GROUPS, CMP_LEN * HEAD_DIM)
        return jax.nn.gelu(flat @ w1[j], approximate=False) @ w2[j]

    return phi(k, 0), phi(v, 1), jnp.asarray(starts + CMP_LEN - 1, dtype=jnp.int32)


NSA_TQ = 128
NSA_N = NSA_HPG * NSA_TQ
N_CMP_PAD = 256
N_SLC = 64
NSA_DK = 3 * HEAD_DIM
NSA_KV = 512
WIN_TILES = WINDOW // NSA_TQ + 1
WIN_KEYS = WIN_TILES * NSA_TQ


def _nsa_kernel(qt_ref, kc_ref, vct_ref, ks_ref, vst_ref, kw_ref, vwt_ref, gate_ref, ovl_ref, o_ref, sel_ref):
    qi = pl.program_id(1)
    q0 = qi * NSA_TQ
    qt = qt_ref[...]
    lane = lax.broadcasted_iota(jnp.int32, (1, NSA_N), 1)
    qpos = q0 + (lane & (NSA_TQ - 1))

    s = jnp.dot(kc_ref[...], qt, preferred_element_type=F32)
    cmp_end = lax.broadcasted_iota(jnp.int32, (N_CMP_PAD, NSA_N), 0) * CMP_STRIDE + (CMP_LEN - 1)
    mc = cmp_end <= qpos
    m = jnp.max(jnp.where(mc, s, NEG_INF), axis=0, keepdims=True)
    p = jnp.where(mc, jnp.exp(s - m), 0.0)
    l = jnp.sum(p, axis=0, keepdims=True)
    pc = p * jnp.where(l > 0.0, 1.0 / l, 0.0)
    o_cmp = jnp.dot(vct_ref[...], pc.astype(BF16), preferred_element_type=F32)

    pcs = (pc[:, 0:NSA_TQ] + pc[:, NSA_TQ:2 * NSA_TQ]) + (pc[:, 2 * NSA_TQ:3 * NSA_TQ] + pc[:, 3 * NSA_TQ:4 * NSA_TQ])
    pcs_hi = pcs.astype(BF16)
    pcs_lo = (pcs - pcs_hi.astype(F32)).astype(BF16)
    ovl = ovl_ref[...]
    imp = jnp.dot(ovl, pcs_hi, preferred_element_type=F32) + jnp.dot(ovl, pcs_lo, preferred_element_type=F32)
    blk = lax.broadcasted_iota(jnp.int32, (N_SLC, NSA_TQ), 0)
    q_blk = (q0 + lax.broadcasted_iota(jnp.int32, (N_SLC, NSA_TQ), 1)) >> 6
    forced = (blk == 0) | (blk == q_blk) | (blk == q_blk - 1)
    score = jnp.where(forced, FORCED_SCORE, jnp.where(blk <= q_blk, imp, -1.0))
    rank = jnp.zeros((N_SLC, NSA_TQ), jnp.int32)
    for i in range(N_SLC):
        row = score[i:i + 1, :]
        beats = (row > score) | ((row == score) & (blk > i))
        rank = rank + beats.astype(jnp.int32)
    sel = (rank < SLC_TOP).astype(F32)
    sel_ref[...] = jnp.concatenate([sel] * (NSA_HPG // 2), axis=1)

    hn = NSA_N // 2
    qts = (qt[:, :hn], qt[:, hn:])
    qpos_h = qpos[:, :hn]

    w0 = jnp.maximum(qi - WINDOW // NSA_TQ, 0)
    k_w = jnp.concatenate([kw_ref[w0 + i] for i in range(WIN_TILES)], axis=0)
    vt_w = jnp.concatenate([vwt_ref[w0 + i] for i in range(WIN_TILES)], axis=1)
    rel = qpos_h - (w0 * NSA_TQ + lax.broadcasted_iota(jnp.int32, (WIN_KEYS, hn), 0))
    in_win = (rel >= 0) & (rel < WINDOW)
    outs_w = []
    for h in range(2):
        s = jnp.dot(k_w, qts[h], preferred_element_type=F32)
        m = jnp.max(jnp.where(in_win, s, NEG_INF), axis=0, keepdims=True)
        p = jnp.where(in_win, jnp.exp(s - m), 0.0)
        l = jnp.sum(p, axis=0, keepdims=True)
        outs_w.append(jnp.dot(vt_w, p.astype(BF16), preferred_element_type=F32) / l)
    acc_w = jnp.concatenate(outs_w, axis=1)

    krow = lax.broadcasted_iota(jnp.int32, (NSA_KV, hn), 0)
    blocks_per_tile = NSA_KV // SLC_LEN

    def slc_chains(kt, carry):
        rows = [jnp.broadcast_to(sel_ref[pl.ds(blocks_per_tile * kt + j, 1), :], (SLC_LEN, hn))
                for j in range(blocks_per_tile)]
        mask = (jnp.concatenate(rows, axis=0) > 0.0) & ((kt * NSA_KV + krow) <= qpos_h)
        k, vt = ks_ref[kt], vst_ref[kt]
        return tuple(_flash_step(k, vt, qts[h], mask, carry[h]) for h in range(2))

    slc = lax.fori_loop(0, (q0 + NSA_TQ - 1) // NSA_KV + 1, slc_chains, (_flash_init(hn), _flash_init(hn)))
    acc_s = jnp.concatenate([slc[0][2] / slc[0][1], slc[1][2] / slc[1][1]], axis=1)

    g = jax.nn.sigmoid(gate_ref[...])
    o_ref[...] = g[0:1] * o_cmp + g[1:2] * acc_s + g[2:3] * acc_w


def nsa_attention(q, k_cmp, v_cmp, k_slc, v_slc, k_win, v_win, gate_logit):
    B_, S_, H, Dh = q.shape
    G = NSA_GROUPS
    n_qt = S_ // NSA_TQ
    n_cmp = k_cmp.shape[1]
    cs = np.arange(N_CMP_PAD)[None, :] * CMP_STRIDE
    ss = np.arange(N_SLC)[:, None] * SLC_LEN
    ov = np.clip(np.minimum(cs + CMP_LEN, ss + SLC_LEN) - np.maximum(cs, ss), 0, None) / CMP_LEN
    ov[:, n_cmp:] = 0.0
    ovl = jnp.asarray(ov, dtype=BF16)
    qs = _aug_q(q * (Dh ** -0.5)).reshape(B_, n_qt, NSA_TQ, G, NSA_HPG, NSA_DK)
    qt = qs.transpose(0, 3, 1, 5, 4, 2).reshape(B_ * G * n_qt, NSA_DK, NSA_N)
    gt = gate_logit.astype(F32).reshape(B_, n_qt, NSA_TQ, G, NSA_HPG, N_NSA_BRANCH)
    gt = gt.transpose(0, 3, 1, 5, 4, 2).reshape(B_ * G * n_qt, N_NSA_BRANCH, NSA_N)
    keys = lambda t, kv: _aug_k(t).reshape(B_, S_ // kv, kv, G, NSA_DK).transpose(0, 3, 1, 2, 4).reshape(B_ * G, S_ // kv, kv, NSA_DK)
    vals = lambda t, kv: t.astype(BF16).reshape(B_, S_ // kv, kv, G, Dh).transpose(0, 3, 1, 4, 2).reshape(B_ * G, S_ // kv, Dh, kv)
    padc = ((0, 0), (0, N_CMP_PAD - n_cmp), (0, 0), (0, 0))
    kc = _aug_k(jnp.pad(k_cmp, padc)).transpose(0, 2, 1, 3).reshape(B_ * G, N_CMP_PAD, NSA_DK)
    vct = jnp.pad(v_cmp, padc).astype(BF16).transpose(0, 2, 3, 1).reshape(B_ * G, Dh, N_CMP_PAD)
    per_bg = lambda *blk: pl.BlockSpec((None,) + blk, lambda b, i: (b,) + (0,) * len(blk))
    per_tile = lambda *blk: pl.BlockSpec((None,) + blk, lambda b, i: (b * n_qt + i,) + (0,) * len(blk))
    out = pl.pallas_call(
        _nsa_kernel,
        grid=(B_ * G, n_qt),
        in_specs=[per_tile(NSA_DK, NSA_N),
                  per_bg(N_CMP_PAD, NSA_DK), per_bg(Dh, N_CMP_PAD),
                  per_bg(S_ // NSA_KV, NSA_KV, NSA_DK), per_bg(S_ // NSA_KV, Dh, NSA_KV),
                  per_bg(n_qt, NSA_TQ, NSA_DK), per_bg(n_qt, Dh, NSA_TQ),
                  per_tile(N_NSA_BRANCH, NSA_N),
                  pl.BlockSpec((N_SLC, N_CMP_PAD), lambda b, i: (0, 0))],
        out_specs=per_tile(Dh, NSA_N),
        out_shape=jax.ShapeDtypeStruct((B_ * G * n_qt, Dh, NSA_N), F32),
        scratch_shapes=[pltpu.VMEM((N_SLC, NSA_N // 2), F32)],
        name="nsa_attention",
    )(qt, kc, vct, keys(k_slc, NSA_KV), vals(v_slc, NSA_KV), keys(k_win, NSA_TQ), vals(v_win, NSA_TQ), gt, ovl)
    return out.reshape(B_, G, n_qt, Dh, NSA_HPG, NSA_TQ).transpose(0, 2, 5, 1, 4, 3).reshape(B_, S_, H * Dh)


DENSE_TM = 256
SEC_QKV = 4 * BRANCH_WIDTH
SEC_KV = 6 * KV_WIDTH
SEC_MERGE = N_BRANCH * D_MODEL
SEC_SMALL = 256
IN_SECTIONS = (SEC_QKV, SEC_KV, SEC_MERGE, SEC_SMALL)


def _rms(x, g):
    return x * lax.rsqrt(jnp.mean(x * x, axis=-1, keepdims=True) + RMS_EPS) * g


def _row_spec(width):
    return pl.BlockSpec((DENSE_TM, width), lambda i: (i, 0))


def _whole_spec(shape):
    return pl.BlockSpec(shape, lambda i: (0,) * len(shape))


def _in_proj_kernel(x_ref, g_ref, w_ref, *o_refs):
    xn = _rms(x_ref[...], g_ref[...]).astype(BF16)
    off = 0
    for o_ref, width in zip(o_refs, IN_SECTIONS):
        o_ref[...] = jnp.dot(xn, w_ref[:, off:off + width], preferred_element_type=F32)
        off += width


def in_projection(x, g, w_in):
    t, d = x.shape
    fq, fk, fv, fl, nq, kc, vc, ksl, vsl, kwn, vwn, gl, ml = split_columns(w_in, IN_SPLITS)
    pad = lambda w: jnp.pad(w, ((0, 0), (0, 128 - w.shape[1])))
    w = jnp.concatenate([fq, fk, fv, nq, kc, vc, ksl, vsl, kwn, vwn, ml, pad(fl), pad(gl)], axis=1).astype(BF16)
    return pl.pallas_call(
        _in_proj_kernel,
        grid=(t // DENSE_TM,),
        in_specs=[_row_spec(d), _whole_spec((1, d)), _whole_spec(w.shape)],
        out_specs=[_row_spec(s) for s in IN_SECTIONS],
        out_shape=[jax.ShapeDtypeStruct((t, s), F32) for s in IN_SECTIONS],
        name="in_projection",
    )(x, g.reshape(1, d), w)


def _merge_kernel(x_ref, yf_ref, yn_ref, ml_ref, wb_ref, wo_ref, o_ref):
    g = jax.nn.sigmoid(ml_ref[...])
    up_f = jnp.dot(yf_ref[...].astype(BF16), wb_ref[0], preferred_element_type=F32)
    up_n = jnp.dot(yn_ref[...].astype(BF16), wb_ref[1], preferred_element_type=F32)
    merged = g[:, :D_MODEL] * up_f + g[:, D_MODEL:] * up_n
    o_ref[...] = x_ref[...] + jnp.dot(merged.astype(BF16), wo_ref[...], preferred_element_type=F32)


def merge_out(x, y_fox, y_nsa, merge_logit, w_branch, w_out):
    t, d = x.shape
    return pl.pallas_call(
        _merge_kernel,
        grid=(t // DENSE_TM,),
        in_specs=[_row_spec(d), _row_spec(BRANCH_WIDTH), _row_spec(BRANCH_WIDTH), _row_spec(SEC_MERGE),
                  _whole_spec(w_branch.shape), _whole_spec(w_out.shape)],
        out_specs=_row_spec(d),
        out_shape=jax.ShapeDtypeStruct((t, d), F32),
        name="merge_out",
    )(x, y_fox, y_nsa, merge_logit, w_branch.astype(BF16), w_out.astype(BF16))


def _peer_q_kernel(x_ref, g_ref, w_ref, hn_ref, q_ref):
    hn = _rms(x_ref[...], g_ref[...])
    hn_ref[...] = hn
    q = jnp.dot(hn.astype(BF16), w_ref[...], preferred_element_type=F32)
    d = PEER_QDIM // 2
    for j in range(2 * PEER_HEADS):
        q_ref[j] = q[:, j * d:(j + 1) * d]


def peer_query(x, g, wq):
    t, d = x.shape
    hp, dq = 2 * PEER_HEADS, PEER_QDIM // 2
    return pl.pallas_call(
        _peer_q_kernel,
        grid=(t // DENSE_TM,),
        in_specs=[_row_spec(d), _whole_spec((1, d)), _whole_spec(wq.shape)],
        out_specs=[_row_spec(d), pl.BlockSpec((hp, DENSE_TM, dq), lambda i: (0, i, 0))],
        out_shape=[jax.ShapeDtypeStruct((t, d), F32), jax.ShapeDtypeStruct((hp, t, dq), F32)],
        name="peer_query",
    )(x, g.reshape(1, d), wq.astype(BF16))


def _final_kernel(h_ref, p_ref, g_ref, o_ref):
    o_ref[...] = _rms(h_ref[...] + p_ref[...], g_ref[...])


def final_norm(h, p, g):
    t, d = h.shape
    return pl.pallas_call(
        _final_kernel,
        grid=(t // DENSE_TM,),
        in_specs=[_row_spec(d), _row_spec(d), _whole_spec((1, d))],
        out_specs=_row_spec(d),
        out_shape=jax.ShapeDtypeStruct((t, d), F32),
        name="final_norm",
    )(h, p, g.reshape(1, d))


N_CMP_ROWS = 256


def _compress_kernel(r_ref, pos_ref, w1_ref, w2_ref, o_ref):
    r = r_ref[...]
    nxt = jnp.concatenate([r[1:], jnp.zeros((1, r.shape[1]), F32)], axis=0)
    half = r.shape[1]
    hid = (jnp.dot((r + pos_ref[0:1, :]).astype(BF16), w1_ref[:half], preferred_element_type=F32)
           + jnp.dot((nxt + pos_ref[1:2, :]).astype(BF16), w1_ref[half:], preferred_element_type=F32))
    hid = 0.5 * hid * (1.0 + lax.erf(hid * (2.0 ** -0.5)))
    o_ref[...] = jnp.dot(hid.astype(BF16), w2_ref[...], preferred_element_type=F32)


def compress_kv_pallas(kv, cmp_pos, w1, w2):
    _, B_, S_, G, Dh = kv.shape
    n_cmp = (S_ - CMP_LEN) // CMP_STRIDE + 1
    width = CMP_STRIDE * Dh
    rows = kv.transpose(0, 1, 3, 2, 4).reshape(2, B_ * G, S_ // CMP_STRIDE, width)
    out = pl.pallas_call(
        _compress_kernel,
        grid=(2, B_ * G),
        in_specs=[pl.BlockSpec((None, None, N_CMP_ROWS, width), lambda j, b: (j, b, 0, 0)),
                  pl.BlockSpec((None, 2, width), lambda j, b: (j, 0, 0)),
                  pl.BlockSpec((None, CMP_LEN * Dh, CMP_HIDDEN), lambda j, b: (j, 0, 0)),
                  pl.BlockSpec((None, CMP_HIDDEN, Dh), lambda j, b: (j, 0, 0))],
        out_specs=pl.BlockSpec((None, None, N_CMP_ROWS, Dh), lambda j, b: (j, b, 0, 0)),
        out_shape=jax.ShapeDtypeStruct((2, B_ * G, N_CMP_ROWS, Dh), F32),
        name="compress_kv",
    )(rows, cmp_pos.reshape(2, 2, width), w1.astype(BF16), w2.astype(BF16))
    return out[:, :, :n_cmp].reshape(2, B_, G, n_cmp, Dh).transpose(0, 1, 3, 2, 4)


def hybrid_mixer(x, norm_g, w_in, fox_f_bias, cmp_pos, cmp_w1, cmp_w2, w_branch, w_out):
    B_, S_, D = x.shape
    xf = x.reshape(B_ * S_, D)
    qkv, kvs, merge_logit, small = in_projection(xf, norm_g, w_in)
    heads = lambda t, n: t.reshape(B_, S_, n, HEAD_DIM)
    fq, fk, fv, nq = (qkv[:, i * BRANCH_WIDTH:(i + 1) * BRANCH_WIDTH] for i in range(4))
    kc, vc, ksl, vsl, kwn, vwn = (heads(kvs[:, i * KV_WIDTH:(i + 1) * KV_WIDTH], NSA_GROUPS) for i in range(6))
    f_logit = small[:, :FOX_HEADS].reshape(B_, S_, FOX_HEADS)
    gate_logit = small[:, 128:128 + NSA_HEADS * N_NSA_BRANCH].reshape(B_, S_, NSA_HEADS, N_NSA_BRANCH)
    log_f = jax.nn.log_sigmoid(f_logit + fox_f_bias.astype(F32))
    y_fox = fox_attention(heads(fq, FOX_HEADS), heads(fk, FOX_HEADS), heads(fv, FOX_HEADS), log_f)
    cos, sin = rope_tables(jnp.arange(S_, dtype=F32))
    q_nsa = apply_partial_rope(heads(nq, NSA_HEADS), cos, sin)
    k_slc = apply_partial_rope(ksl, cos, sin)
    k_win = apply_partial_rope(kwn, cos, sin)
    cmp = compress_kv_pallas(jnp.stack([kc, vc]), cmp_pos, cmp_w1, cmp_w2)
    n_cmp = cmp.shape[2]
    cmp_end = jnp.arange(n_cmp, dtype=F32) * CMP_STRIDE + (CMP_LEN - 1)
    k_cmp = apply_partial_rope(cmp[0], *rope_tables(cmp_end))
    y_nsa = nsa_attention(q_nsa, k_cmp, cmp[1], k_slc, vsl, k_win, vwn, gate_logit)
    return merge_out(xf, y_fox.reshape(B_ * S_, -1), y_nsa.reshape(B_ * S_, -1), merge_logit, w_branch, w_out)


def peer_ffn(h, norm_g, wq, subkeys, u, v):
    hn, q = peer_query(h, norm_g, wq)
    idx2, w = peer_topk(q, subkeys)
    act = jax.nn.gelu(peer_u(idx2, hn, pack_table(u)), approximate=False)
    return peer_v(idx2, w * act, pack_table(v))


def kernel(x, norm_mix, w_in, fox_f_bias, nsa_cmp_pos, nsa_cmp_w1, nsa_cmp_w2, w_branch, w_out,
           norm_ffn, peer_wq, peer_subkeys, peer_u, peer_v, norm_final):
    B_, S_, D = x.shape
    assert norm_mix.shape[0] == 1, "single-layer trunk"
    h = hybrid_mixer(x, norm_mix[0], w_in[0], fox_f_bias[0], nsa_cmp_pos[0], nsa_cmp_w1[0], nsa_cmp_w2[0],
                     w_branch[0], w_out[0])
    p = peer_ffn(h, norm_ffn[0], peer_wq[0], peer_subkeys[0], peer_u[0], peer_v[0])
    return final_norm(h, p, norm_final).reshape(B_, S_, D)
```

```python
import functools

import jax
import jax.numpy as jnp
import numpy as np
from jax import lax
from jax.experimental import pallas as pl
from jax.experimental.pallas import tpu as pltpu

D_MODEL = 1024
HEAD_DIM = 64
FOX_HEADS = 8
NSA_HEADS = 8
NSA_GROUPS = 2
NSA_HPG = NSA_HEADS // NSA_GROUPS
BRANCH_WIDTH = 512
N_BRANCH = 2
N_NSA_BRANCH = 3
ROPE_DIM = HEAD_DIM // 4
ROPE_THETA = 500000.0
Q_BLOCK = 128
CMP_LEN = 32
CMP_STRIDE = 16
CMP_HIDDEN = 2 * HEAD_DIM
SLC_LEN = 64
SLC_TOP = 16
WINDOW = 512
FORCED_SCORE = 1e9
NEG_INF = -1e30
PEER_HEADS = 8
N_KEYS = 128
N_EXPERTS = N_KEYS * N_KEYS
PEER_QDIM = 256
PEER_TOPK = 16
PEER_CHUNK = 128
RMS_EPS = 1e-6
KV_WIDTH = NSA_GROUPS * HEAD_DIM
IN_SPLITS = (BRANCH_WIDTH, BRANCH_WIDTH, BRANCH_WIDTH, FOX_HEADS, BRANCH_WIDTH,
             KV_WIDTH, KV_WIDTH, KV_WIDTH, KV_WIDTH, KV_WIDTH, KV_WIDTH,
             NSA_HEADS * N_NSA_BRANCH, N_BRANCH * D_MODEL)


def _rmsnorm_kernel(x_ref, g_ref, o_ref):
    x = x_ref[...]
    ms = jnp.mean(x * x, axis=-1, keepdims=True)
    o_ref[...] = x * lax.rsqrt(ms + RMS_EPS) * g_ref[...]


def rmsnorm_pallas(x2d, g, rows=512):
    t, d = x2d.shape
    return pl.pallas_call(
        _rmsnorm_kernel,
        grid=(t // rows,),
        in_specs=[pl.BlockSpec((rows, d), lambda i: (i, 0)),
                  pl.BlockSpec((1, d), lambda i: (0, 0))],
        out_specs=pl.BlockSpec((rows, d), lambda i: (i, 0)),
        out_shape=jax.ShapeDtypeStruct((t, d), jnp.float32),
    )(x2d, g.reshape(1, d))


PEER_SLOTS = PEER_HEADS * PEER_TOPK
PEER_TT = 64
PEER_CHUNK_SLOTS = 16
HALF_D = D_MODEL // 2
ROW_SUB = HALF_D // 128
VMEM_LIMIT_PEER = 48 * 1024 * 1024


def pack_table(tab):
    n = tab.shape[0]
    b = lax.bitcast_convert_type(tab.astype(jnp.bfloat16), jnp.uint16).astype(jnp.uint32)
    b = b.reshape(n, ROW_SUB, 2, 128)
    return b[:, :, 0, :] | (b[:, :, 1, :] << 16)


def _unpack(w):
    lo = lax.bitcast_convert_type(w << 16, jnp.float32)
    hi = lax.bitcast_convert_type(w & jnp.uint32(0xFFFF0000), jnp.float32)
    return lo, hi


def _peer_u_kernel(idx_ref, x_ref, tab_ref, out_ref):
    lane = lax.broadcasted_iota(jnp.int32, (ROW_SUB, 128), 1)

    def tok(t, carry):
        row = x_ref[pl.ds(t, 1), :]
        chunks = [row[:, c * 128:(c + 1) * 128] for c in range(2 * ROW_SUB)]
        xl = jnp.concatenate(chunks[0::2], axis=0)
        xh = jnp.concatenate(chunks[1::2], axis=0)
        acc = jnp.zeros((ROW_SUB, 128), jnp.float32)
        base = t * PEER_SLOTS
        for s in range(PEER_SLOTS):
            lo, hi = _unpack(tab_ref[idx_ref[base + s]])
            part = jnp.sum(lo * xl + hi * xh, axis=1, keepdims=True)
            acc = jnp.where(lane == s, part, acc)
        out_ref[pl.ds(t, 1), :] = jnp.sum(acc, axis=0, keepdims=True)
        return carry

    lax.fori_loop(0, PEER_TT, tok, 0)


def _peer_v_kernel(idx_ref, coef_ref, tab_ref, out_ref):
    n_acc = 4
    zero = jnp.zeros((ROW_SUB, 128), jnp.float32)

    def tok(t, carry):
        def chunk(c, accs):
            acc_lo, acc_hi = list(accs[:n_acc]), list(accs[n_acc:])
            base = t * PEER_SLOTS + c * PEER_CHUNK_SLOTS
            for j in range(PEER_CHUNK_SLOTS):
                lo, hi = _unpack(tab_ref[idx_ref[base + j]])
                cf = coef_ref[base + j]
                acc_lo[j % n_acc] = acc_lo[j % n_acc] + cf * lo
                acc_hi[j % n_acc] = acc_hi[j % n_acc] + cf * hi
            return tuple(acc_lo + acc_hi)

        accs = lax.fori_loop(0, PEER_SLOTS // PEER_CHUNK_SLOTS, chunk, (zero,) * (2 * n_acc))
        even = (accs[0] + accs[1]) + (accs[2] + accs[3])
        odd = (accs[4] + accs[5]) + (accs[6] + accs[7])
        out_ref[pl.ds(t, 1), :] = jnp.concatenate(
            [half[r:r + 1, :] for r in range(ROW_SUB) for half in (even, odd)], axis=1)
        return carry

    lax.fori_loop(0, PEER_TT, tok, 0)


def _smem_spec():
    return pl.BlockSpec((PEER_TT * PEER_SLOTS,), lambda i: (i,), memory_space=pltpu.SMEM)


def _table_spec(n):
    return pl.BlockSpec((n, ROW_SUB, 128), lambda i: (0, 0, 0), pipeline_mode=pl.Buffered(1))


def peer_u(idx, x, tab):
    t = x.shape[0]
    return pl.pallas_call(
        _peer_u_kernel,
        grid=(t // PEER_TT,),
        in_specs=[_smem_spec(), pl.BlockSpec((PEER_TT, D_MODEL), lambda i: (i, 0)), _table_spec(tab.shape[0])],
        out_specs=pl.BlockSpec((PEER_TT, PEER_SLOTS), lambda i: (i, 0)),
        out_shape=jax.ShapeDtypeStruct((t, PEER_SLOTS), jnp.float32),
        compiler_params=pltpu.CompilerParams(vmem_limit_bytes=VMEM_LIMIT_PEER),
        name="peer_u",
    )(idx.reshape(-1), x, tab)


def peer_v(idx, coef, tab):
    t = idx.shape[0]
    return pl.pallas_call(
        _peer_v_kernel,
        grid=(t // PEER_TT,),
        in_specs=[_smem_spec(), _smem_spec(), _table_spec(tab.shape[0])],
        out_specs=pl.BlockSpec((PEER_TT, D_MODEL), lambda i: (i, 0)),
        out_shape=jax.ShapeDtypeStruct((t, D_MODEL), jnp.float32),
        compiler_params=pltpu.CompilerParams(vmem_limit_bytes=VMEM_LIMIT_PEER),
        name="peer_v",
    )(idx.reshape(-1), coef.reshape(-1), tab)


PK_TT = 256
PK_A_BLOCKS = ((0, 16), (1, 8), (2, 5), (3, 4))
PK_B_BLOCKS = ((0, 4, 16), (1, 4, 8), (2, 4, 5))
NEG_HUGE = -3.0e38


def _extract_top(blocks, flats, payloads, n):
    vals, picked = [], []
    for _ in range(n):
        m = functools.reduce(jnp.maximum, [jnp.max(b, axis=0, keepdims=True) for b in blocks])
        pos = functools.reduce(jnp.minimum, [jnp.min(jnp.where(b == m, f, 1e9), axis=0, keepdims=True)
                                             for b, f in zip(blocks, flats)])
        hit = [f == pos for f in flats]
        if payloads is None:
            picked.append(pos)
        else:
            picked.append(functools.reduce(jnp.add, [jnp.sum(jnp.where(h, p, 0.0), axis=0, keepdims=True)
                                                     for h, p in zip(hit, payloads)]))
        blocks = [jnp.where(h, NEG_HUGE, b) for h, b in zip(hit, blocks)]
        vals.append(m)
    return vals, picked


def _candidate_grid(first, second, combine):
    up8 = lambda n: -(-n // 8) * 8
    return jnp.concatenate([combine(first[a:a + 1], second[0:up8(nb)]) for a, nb in PK_A_BLOCKS]
                           + [combine(first[0:up8(a1)], second[b:b + 1]) for b, _, a1 in PK_B_BLOCKS], axis=0)


def _candidate_valid_and_flat():
    up8 = lambda n: -(-n // 8) * 8
    valid, flat = [], []
    for a, nb in PK_A_BLOCKS:
        r = lax.broadcasted_iota(jnp.int32, (up8(nb), PK_TT), 0)
        valid.append(r < nb)
        flat.append(a * PEER_TOPK + r)
    for b, a0, a1 in PK_B_BLOCKS:
        r = lax.broadcasted_iota(jnp.int32, (up8(a1), PK_TT), 0)
        valid.append((r >= a0) & (r < a1))
        flat.append(r * PEER_TOPK + b)
    valid = jnp.concatenate(valid, axis=0)
    return valid, jnp.where(valid, jnp.concatenate(flat, axis=0).astype(F32), 1e9)


def _peer_topk_kernel(q_ref, ka_ref, eid_ref, w_ref):
    key_row = lax.broadcasted_iota(jnp.int32, (N_KEYS, PK_TT), 0).astype(F32)
    valid, flat = _candidate_valid_and_flat()

    def head(h, carry):
        tops = []
        for p in range(2):
            q = q_ref[2 * h + p]
            q_hi = _trunc_bf16(q)
            qa = jnp.concatenate([q_hi.astype(BF16), (q - q_hi).astype(BF16), q_hi.astype(BF16)], axis=1)
            s = lax.dot_general(ka_ref[2 * h + p], qa, (((1,), (1,)), ((), ())),
                                preferred_element_type=F32)
            vals, idxs = _extract_top([s], [key_row], None, PEER_TOPK)
            tops.append((jnp.concatenate(vals, axis=0), jnp.concatenate(idxs, axis=0)))
        (s1, i1), (s2, i2) = tops
        cand = jnp.where(valid, _candidate_grid(s1, s2, lambda x, y: x + y), NEG_HUGE)
        eids = _candidate_grid(i1, i2, lambda x, y: x * N_KEYS + y)
        vals, picked = _extract_top([cand], [flat], [eids], PEER_TOPK)
        top_s = jnp.concatenate(vals, axis=0)
        e = jnp.exp(top_s - vals[0])
        w_ref[h] = e / jnp.sum(e, axis=0, keepdims=True)
        eid_ref[h] = jnp.concatenate(picked, axis=0).astype(jnp.int32)
        return carry

    lax.fori_loop(0, PEER_HEADS, head, 0)


def peer_topk(qt, subkeys):
    hp, t, d = qt.shape
    k_hi, k_lo = _split2(subkeys.reshape(hp, N_KEYS, d))
    ka = jnp.concatenate([k_hi, k_hi, k_lo], axis=2)
    eid, w = pl.pallas_call(
        _peer_topk_kernel,
        grid=(t // PK_TT,),
        in_specs=[pl.BlockSpec((hp, PK_TT, d), lambda i: (0, i, 0)),
                  pl.BlockSpec((hp, N_KEYS, 3 * d), lambda i: (0, 0, 0))],
        out_specs=[pl.BlockSpec((PEER_HEADS, PEER_TOPK, PK_TT), lambda i: (0, 0, i)),
                   pl.BlockSpec((PEER_HEADS, PEER_TOPK, PK_TT), lambda i: (0, 0, i))],
        out_shape=[jax.ShapeDtypeStruct((PEER_HEADS, PEER_TOPK, t), jnp.int32),
                   jax.ShapeDtypeStruct((PEER_HEADS, PEER_TOPK, t), F32)],
        name="peer_topk",
    )(qt, ka)
    to_slots = lambda a: a.transpose(2, 0, 1).reshape(t, PEER_SLOTS)
    return to_slots(eid), to_slots(w)


def rmsnorm(x, g):
    xf = x.astype(jnp.float32)
    y = xf * lax.rsqrt(jnp.mean(xf * xf, axis=-1, keepdims=True) + RMS_EPS)
    return (y * g.astype(jnp.float32)).astype(x.dtype)


def split_columns(t, sizes):
    offs = np.cumsum((0,) + tuple(sizes))
    return [t[..., int(a):int(b)] for a, b in zip(offs[:-1], offs[1:])]


def rope_tables(pos):
    inv = jnp.power(ROPE_THETA, -jnp.arange(0, ROPE_DIM, 2, dtype=jnp.float32) / ROPE_DIM)
    ang = pos[:, None] * inv[None, :]
    return jnp.cos(ang), jnp.sin(ang)


def apply_partial_rope(x, cos, sin):
    half = ROPE_DIM // 2
    xr = x[..., :ROPE_DIM].astype(jnp.float32)
    x1, x2 = xr[..., :half], xr[..., half:]
    c, s = cos[:, None, :], sin[:, None, :]
    rot = jnp.concatenate([x1 * c - x2 * s, x1 * s + x2 * c], axis=-1)
    return jnp.concatenate([rot.astype(x.dtype), x[..., ROPE_DIM:]], axis=-1)


BF16 = jnp.bfloat16
F32 = jnp.float32


def _flash_step(k, vt, qt, mask, carry):
    m, l, acc = carry
    s = jnp.dot(k, qt, preferred_element_type=F32)
    s_vis = s if mask is None else jnp.where(mask, s, NEG_INF)
    m_new = jnp.maximum(m, jnp.max(s_vis, axis=0, keepdims=True))
    alpha = jnp.exp(m - m_new)
    p = jnp.exp(s - m_new)
    if mask is not None:
        p = jnp.where(mask, p, 0.0)
    l = alpha * l + jnp.sum(p, axis=0, keepdims=True)
    acc = alpha * acc + jnp.dot(vt, p.astype(BF16), preferred_element_type=F32)
    return m_new, l, acc


def _flash_steps(ks, vts, qts, mask, carries):
    ss = [jnp.dot(k, qt, preferred_element_type=F32) for k, qt in zip(ks, qts)]
    stats, ps = [], []
    for s, (m, l, _) in zip(ss, carries):
        s_vis = s if mask is None else jnp.where(mask, s, NEG_INF)
        m_new = jnp.maximum(m, jnp.max(s_vis, axis=0, keepdims=True))
        alpha = jnp.exp(m - m_new)
        p = jnp.exp(s - m_new)
        if mask is not None:
            p = jnp.where(mask, p, 0.0)
        stats.append((m_new, alpha, alpha * l + jnp.sum(p, axis=0, keepdims=True)))
        ps.append(p.astype(BF16))
    return tuple((m_new, l, alpha * acc + jnp.dot(vt, p, preferred_element_type=F32))
                 for (m_new, alpha, l), p, vt, (_, _, acc) in zip(stats, ps, vts, carries))


def _flash_init(n):
    return (jnp.full((1, n), NEG_INF, F32), jnp.zeros((1, n), F32), jnp.zeros((HEAD_DIM, n), F32))


FOX_TQ = 256
FOX_KV = 512
FOX_DK = 256


FOX_HP = 4


def _fox_kernel(qt_ref, k_ref, vt_ref, o_ref):
    qi = pl.program_id(1)
    q0 = qi * FOX_TQ
    qts = [qt_ref[h] for h in range(FOX_HP)]
    qpos = q0 + lax.broadcasted_iota(jnp.int32, (1, FOX_TQ), 1)
    krow = lax.broadcasted_iota(jnp.int32, (FOX_KV, FOX_TQ), 0)

    def full_step(kt, carry):
        return _flash_steps([k_ref[h, kt] for h in range(FOX_HP)], [vt_ref[h, kt] for h in range(FOX_HP)],
                            qts, None, carry)

    def diag_step(kt, carry):
        mask = (kt * FOX_KV + krow) <= qpos
        return _flash_steps([k_ref[h, kt] for h in range(FOX_HP)], [vt_ref[h, kt] for h in range(FOX_HP)],
                            qts, mask, carry)

    n_full = (q0 + 1) // FOX_KV
    n_all = (q0 + FOX_TQ + FOX_KV - 1) // FOX_KV
    carry = lax.fori_loop(0, n_full, full_step, tuple(_flash_init(FOX_TQ) for _ in range(FOX_HP)))
    carry = lax.fori_loop(n_full, n_all, diag_step, carry)
    for h in range(FOX_HP):
        _, l, acc = carry[h]
        o_ref[h] = acc / l


def _trunc_bf16(x):
    bits = lax.bitcast_convert_type(x, jnp.uint32) & jnp.uint32(0xFFFF0000)
    return lax.bitcast_convert_type(bits, F32)


def _split3(c):
    c1 = _trunc_bf16(c)
    r = c - c1
    c2 = _trunc_bf16(r)
    return c1.astype(BF16), c2.astype(BF16), (r - c2).astype(BF16)


def _split2(x):
    hi = _trunc_bf16(x)
    return hi.astype(BF16), (x - hi).astype(BF16)


def _aug_q(q):
    hi, lo = _split2(q)
    return jnp.concatenate([hi, hi, lo], axis=-1)


def _aug_k(k):
    hi, lo = _split2(k)
    return jnp.concatenate([hi, lo, hi], axis=-1)


def fox_attention(q, k, v, log_f):
    B_, S_, H, Dh = q.shape
    KV_TILE = FOX_KV
    n_qt, n_kt = S_ // FOX_TQ, S_ // KV_TILE
    c = jnp.cumsum(log_f, axis=1)
    c1, c2, c3 = _split3(c)
    one = jnp.ones_like(c1)
    pad = jnp.zeros((B_, S_, H, FOX_DK - 3 * Dh - 6), BF16)
    col = lambda t: t[..., None]
    qa = jnp.concatenate([_aug_q(q * (Dh ** -0.5)), col(one), col(one), col(one),
                          col(c1), col(c2), col(c3), pad], axis=-1)
    ka = jnp.concatenate([_aug_k(k), col(-c1), col(-c2), col(-c3),
                          col(one), col(one), col(one), pad], axis=-1)
    HP, HG = FOX_HP, H // FOX_HP
    qt = qa.reshape(B_, n_qt, FOX_TQ, HG, HP, FOX_DK).transpose(0, 3, 1, 4, 5, 2).reshape(B_ * HG * n_qt, HP, FOX_DK, FOX_TQ)
    kk = ka.reshape(B_, n_kt, KV_TILE, HG, HP, FOX_DK).transpose(0, 3, 4, 1, 2, 5).reshape(B_ * HG, HP, n_kt, KV_TILE, FOX_DK)
    vt = v.astype(BF16).reshape(B_, n_kt, KV_TILE, HG, HP, Dh).transpose(0, 3, 4, 1, 5, 2).reshape(B_ * HG, HP, n_kt, Dh, KV_TILE)
    out = pl.pallas_call(
        _fox_kernel,
        grid=(B_ * HG, n_qt),
        in_specs=[pl.BlockSpec((None, HP, FOX_DK, FOX_TQ), lambda b, i: (b * n_qt + i, 0, 0, 0)),
                  pl.BlockSpec((None, HP, n_kt, KV_TILE, FOX_DK), lambda b, i: (b, 0, 0, 0, 0)),
                  pl.BlockSpec((None, HP, n_kt, Dh, KV_TILE), lambda b, i: (b, 0, 0, 0, 0))],
        out_specs=pl.BlockSpec((None, HP, Dh, FOX_TQ), lambda b, i: (b * n_qt + i, 0, 0, 0)),
        out_shape=jax.ShapeDtypeStruct((B_ * HG * n_qt, HP, Dh, FOX_TQ), F32),
        name="fox_attention",
    )(qt, kk, vt)
    return out.reshape(B_, HG, n_qt, HP, Dh, FOX_TQ).transpose(0, 2, 5, 1, 3, 4).reshape(B_, S_, H * Dh)


def compress_kv(k, v, cmp_pos, w1, w2):
    B_, S_ = k.shape[0], k.shape[1]
    n_cmp = (S_ - CMP_LEN) // CMP_STRIDE + 1
    starts = np.arange(n_cmp) * CMP_STRIDE
    idx = starts[:, None] + np.arange(CMP_LEN)[None, :]

    def phi(t, j):
        blocks = t[:, idx] + cmp_pos[j][None, None, :, None, :]
        flat = jnp.swapaxes(blocks, 2, 3).reshape(B_, n_cmp, NSA_GROUPS, CMP_LEN * HEAD_DIM)
        return jax.nn.gelu(flat @ w1[j], approximate=False) @ w2[j]

    return phi(k, 0), phi(v, 1), jnp.asarray(starts + CMP_LEN - 1, dtype=jnp.int32)


NSA_TQ = 128
NSA_N = NSA_HPG * NSA_TQ
N_CMP_PAD = 256
N_SLC = 64
NSA_DK = 3 * HEAD_DIM
NSA_KV = 512
WIN_TILES = WINDOW // NSA_TQ + 1
WIN_KEYS = WIN_TILES * NSA_TQ


def _nsa_kernel(qt_ref, kc_ref, vct_ref, ks_ref, vst_ref, kw_ref, vwt_ref, gate_ref, ovl_ref, o_ref, sel_ref):
    qi = pl.program_id(1)
    q0 = qi * NSA_TQ
    qt = qt_ref[...]
    lane = lax.broadcasted_iota(jnp.int32, (1, NSA_N), 1)
    qpos = q0 + (lane & (NSA_TQ - 1))

    s = jnp.dot(kc_ref[...], qt, preferred_element_type=F32)
    cmp_end = lax.broadcasted_iota(jnp.int32, (N_CMP_PAD, NSA_N), 0) * CMP_STRIDE + (CMP_LEN - 1)
    mc = cmp_end <= qpos
    m = jnp.max(jnp.where(mc, s, NEG_INF), axis=0, keepdims=True)
    p = jnp.where(mc, jnp.exp(s - m), 0.0)
    l = jnp.sum(p, axis=0, keepdims=True)
    pc = p * jnp.where(l > 0.0, 1.0 / l, 0.0)
    o_cmp = jnp.dot(vct_ref[...], pc.astype(BF16), preferred_element_type=F32)

    pcs = (pc[:, 0:NSA_TQ] + pc[:, NSA_TQ:2 * NSA_TQ]) + (pc[:, 2 * NSA_TQ:3 * NSA_TQ] + pc[:, 3 * NSA_TQ:4 * NSA_TQ])
    pcs_hi = pcs.astype(BF16)
    pcs_lo = (pcs - pcs_hi.astype(F32)).astype(BF16)
    ovl = ovl_ref[...]
    imp = jnp.dot(ovl, pcs_hi, preferred_element_type=F32) + jnp.dot(ovl, pcs_lo, preferred_element_type=F32)
    blk = lax.broadcasted_iota(jnp.int32, (N_SLC, NSA_TQ), 0)
    q_blk = (q0 + lax.broadcasted_iota(jnp.int32, (N_SLC, NSA_TQ), 1)) >> 6
    forced = (blk == 0) | (blk == q_blk) | (blk == q_blk - 1)
    score = jnp.where(forced, FORCED_SCORE, jnp.where(blk <= q_blk, imp, -1.0))
    rank = jnp.zeros((N_SLC, NSA_TQ), jnp.int32)
    for i in range(N_SLC):
        row = score[i:i + 1, :]
        beats = (row > score) | ((row == score) & (blk > i))
        rank = rank + beats.astype(jnp.int32)
    sel = (rank < SLC_TOP).astype(F32)
    sel_ref[...] = jnp.concatenate([sel] * (NSA_HPG // 2), axis=1)

    hn = NSA_N // 2
    qts = (qt[:, :hn], qt[:, hn:])
    qpos_h = qpos[:, :hn]

    w0 = jnp.maximum(qi - WINDOW // NSA_TQ, 0)
    k_w = jnp.concatenate([kw_ref[w0 + i] for i in range(WIN_TILES)], axis=0)
    vt_w = jnp.concatenate([vwt_ref[w0 + i] for i in range(WIN_TILES)], axis=1)
    rel = qpos_h - (w0 * NSA_TQ + lax.broadcasted_iota(jnp.int32, (WIN_KEYS, hn), 0))
    in_win = (rel >= 0) & (rel < WINDOW)
    outs_w = []
    for h in range(2):
        s = jnp.dot(k_w, qts[h], preferred_element_type=F32)
        m = jnp.max(jnp.where(in_win, s, NEG_INF), axis=0, keepdims=True)
        p = jnp.where(in_win, jnp.exp(s - m), 0.0)
        l = jnp.sum(p, axis=0, keepdims=True)
        outs_w.append(jnp.dot(vt_w, p.astype(BF16), preferred_element_type=F32) / l)
    acc_w = jnp.concatenate(outs_w, axis=1)

    krow = lax.broadcasted_iota(jnp.int32, (NSA_KV, hn), 0)
    blocks_per_tile = NSA_KV // SLC_LEN

    def slc_chains(kt, carry):
        rows = [jnp.broadcast_to(sel_ref[pl.ds(blocks_per_tile * kt + j, 1), :], (SLC_LEN, hn))
                for j in range(blocks_per_tile)]
        mask = (jnp.concatenate(rows, axis=0) > 0.0) & ((kt * NSA_KV + krow) <= qpos_h)
        k, vt = ks_ref[kt], vst_ref[kt]
        return _flash_steps([k, k], [vt, vt], qts, mask, carry)

    slc = lax.fori_loop(0, (q0 + NSA_TQ - 1) // NSA_KV + 1, slc_chains, (_flash_init(hn), _flash_init(hn)))
    acc_s = jnp.concatenate([slc[0][2] / slc[0][1], slc[1][2] / slc[1][1]], axis=1)

    g = jax.nn.sigmoid(gate_ref[...])
    o_ref[...] = g[0:1] * o_cmp + g[1:2] * acc_s + g[2:3] * acc_w


def nsa_attention(q, k_cmp, v_cmp, k_slc, v_slc, k_win, v_win, gate_logit):
    B_, S_, H, Dh = q.shape
    G = NSA_GROUPS
    n_qt = S_ // NSA_TQ
    n_cmp = k_cmp.shape[1]
    cs = np.arange(N_CMP_PAD)[None, :] * CMP_STRIDE
    ss = np.arange(N_SLC)[:, None] * SLC_LEN
    ov = np.clip(np.minimum(cs + CMP_LEN, ss + SLC_LEN) - np.maximum(cs, ss), 0, None) / CMP_LEN
    ov[:, n_cmp:] = 0.0
    ovl = jnp.asarray(ov, dtype=BF16)
    qs = _aug_q(q * (Dh ** -0.5)).reshape(B_, n_qt, NSA_TQ, G, NSA_HPG, NSA_DK)
    qt = qs.transpose(0, 3, 1, 5, 4, 2).reshape(B_ * G * n_qt, NSA_DK, NSA_N)
    gt = gate_logit.astype(F32).reshape(B_, n_qt, NSA_TQ, G, NSA_HPG, N_NSA_BRANCH)
    gt = gt.transpose(0, 3, 1, 5, 4, 2).reshape(B_ * G * n_qt, N_NSA_BRANCH, NSA_N)
    keys = lambda t, kv: _aug_k(t).reshape(B_, S_ // kv, kv, G, NSA_DK).transpose(0, 3, 1, 2, 4).reshape(B_ * G, S_ // kv, kv, NSA_DK)
    vals = lambda t, kv: t.astype(BF16).reshape(B_, S_ // kv, kv, G, Dh).transpose(0, 3, 1, 4, 2).reshape(B_ * G, S_ // kv, Dh, kv)
    padc = ((0, 0), (0, N_CMP_PAD - n_cmp), (0, 0), (0, 0))
    kc = _aug_k(jnp.pad(k_cmp, padc)).transpose(0, 2, 1, 3).reshape(B_ * G, N_CMP_PAD, NSA_DK)
    vct = jnp.pad(v_cmp, padc).astype(BF16).transpose(0, 2, 3, 1).reshape(B_ * G, Dh, N_CMP_PAD)
    per_bg = lambda *blk: pl.BlockSpec((None,) + blk, lambda b, i: (b,) + (0,) * len(blk))
    per_tile = lambda *blk: pl.BlockSpec((None,) + blk, lambda b, i: (b * n_qt + i,) + (0,) * len(blk))
    out = pl.pallas_call(
        _nsa_kernel,
        grid=(B_ * G, n_qt),
        in_specs=[per_tile(NSA_DK, NSA_N),
                  per_bg(N_CMP_PAD, NSA_DK), per_bg(Dh, N_CMP_PAD),
                  per_bg(S_ // NSA_KV, NSA_KV, NSA_DK), per_bg(S_ // NSA_KV, Dh, NSA_KV),
                  per_bg(n_qt, NSA_TQ, NSA_DK), per_bg(n_qt, Dh, NSA_TQ),
                  per_tile(N_NSA_BRANCH, NSA_N),
                  pl.BlockSpec((N_SLC, N_CMP_PAD), lambda b, i: (0, 0))],
        out_specs=per_tile(Dh, NSA_N),
        out_shape=jax.ShapeDtypeStruct((B_ * G * n_qt, Dh, NSA_N), F32),
        scratch_shapes=[pltpu.VMEM((N_SLC, NSA_N // 2), F32)],
        name="nsa_attention",
    )(qt, kc, vct, keys(k_slc, NSA_KV), vals(v_slc, NSA_KV), keys(k_win, NSA_TQ), vals(v_win, NSA_TQ), gt, ovl)
    return out.reshape(B_, G, n_qt, Dh, NSA_HPG, NSA_TQ).transpose(0, 2, 5, 1, 4, 3).reshape(B_, S_, H * Dh)


DENSE_TM = 256
SEC_QKV = 4 * BRANCH_WIDTH
SEC_KV = 6 * KV_WIDTH
SEC_MERGE = N_BRANCH * D_MODEL
SEC_SMALL = 256
IN_SECTIONS = (SEC_QKV, SEC_KV, SEC_MERGE, SEC_SMALL)


def _rms(x, g):
    return x * lax.rsqrt(jnp.mean(x * x, axis=-1, keepdims=True) + RMS_EPS) * g


def _row_spec(width):
    return pl.BlockSpec((DENSE_TM, width), lambda i: (i, 0))


def _whole_spec(shape):
    return pl.BlockSpec(shape, lambda i: (0,) * len(shape))


def _in_proj_kernel(x_ref, g_ref, w_ref, *o_refs):
    xn = _rms(x_ref[...], g_ref[...]).astype(BF16)
    off = 0
    for o_ref, width in zip(o_refs, IN_SECTIONS):
        o_ref[...] = jnp.dot(xn, w_ref[:, off:off + width], preferred_element_type=F32)
        off += width


def in_projection(x, g, w_in):
    t, d = x.shape
    fq, fk, fv, fl, nq, kc, vc, ksl, vsl, kwn, vwn, gl, ml = split_columns(w_in, IN_SPLITS)
    pad = lambda w: jnp.pad(w, ((0, 0), (0, 128 - w.shape[1])))
    w = jnp.concatenate([fq, fk, fv, nq, kc, vc, ksl, vsl, kwn, vwn, ml, pad(fl), pad(gl)], axis=1).astype(BF16)
    return pl.pallas_call(
        _in_proj_kernel,
        grid=(t // DENSE_TM,),
        in_specs=[_row_spec(d), _whole_spec((1, d)), _whole_spec(w.shape)],
        out_specs=[_row_spec(s) for s in IN_SECTIONS],
        out_shape=[jax.ShapeDtypeStruct((t, s), F32) for s in IN_SECTIONS],
        name="in_projection",
    )(x, g.reshape(1, d), w)


def _merge_kernel(x_ref, yf_ref, yn_ref, ml_ref, wb_ref, wo_ref, o_ref):
    g = jax.nn.sigmoid(ml_ref[...])
    up_f = jnp.dot(yf_ref[...].astype(BF16), wb_ref[0], preferred_element_type=F32)
    up_n = jnp.dot(yn_ref[...].astype(BF16), wb_ref[1], preferred_element_type=F32)
    merged = g[:, :D_MODEL] * up_f + g[:, D_MODEL:] * up_n
    o_ref[...] = x_ref[...] + jnp.dot(merged.astype(BF16), wo_ref[...], preferred_element_type=F32)


def merge_out(x, y_fox, y_nsa, merge_logit, w_branch, w_out):
    t, d = x.shape
    return pl.pallas_call(
        _merge_kernel,
        grid=(t // DENSE_TM,),
        in_specs=[_row_spec(d), _row_spec(BRANCH_WIDTH), _row_spec(BRANCH_WIDTH), _row_spec(SEC_MERGE),
                  _whole_spec(w_branch.shape), _whole_spec(w_out.shape)],
        out_specs=_row_spec(d),
        out_shape=jax.ShapeDtypeStruct((t, d), F32),
        name="merge_out",
    )(x, y_fox, y_nsa, merge_logit, w_branch.astype(BF16), w_out.astype(BF16))


def _peer_q_kernel(x_ref, g_ref, w_ref, hn_ref, q_ref):
    hn = _rms(x_ref[...], g_ref[...])
    hn_ref[...] = hn
    q = jnp.dot(hn.astype(BF16), w_ref[...], preferred_element_type=F32)
    d = PEER_QDIM // 2
    for j in range(2 * PEER_HEADS):
        q_ref[j] = q[:, j * d:(j + 1) * d]


def peer_query(x, g, wq):
    t, d = x.shape
    hp, dq = 2 * PEER_HEADS, PEER_QDIM // 2
    return pl.pallas_call(
        _peer_q_kernel,
        grid=(t // DENSE_TM,),
        in_specs=[_row_spec(d), _whole_spec((1, d)), _whole_spec(wq.shape)],
        out_specs=[_row_spec(d), pl.BlockSpec((hp, DENSE_TM, dq), lambda i: (0, i, 0))],
        out_shape=[jax.ShapeDtypeStruct((t, d), F32), jax.ShapeDtypeStruct((hp, t, dq), F32)],
        name="peer_query",
    )(x, g.reshape(1, d), wq.astype(BF16))


def _final_kernel(h_ref, p_ref, g_ref, o_ref):
    o_ref[...] = _rms(h_ref[...] + p_ref[...], g_ref[...])


def final_norm(h, p, g):
    t, d = h.shape
    return pl.pallas_call(
        _final_kernel,
        grid=(t // DENSE_TM,),
        in_specs=[_row_spec(d), _row_spec(d), _whole_spec((1, d))],
        out_specs=_row_spec(d),
        out_shape=jax.ShapeDtypeStruct((t, d), F32),
        name="final_norm",
    )(h, p, g.reshape(1, d))


N_CMP_ROWS = 256


def _compress_kernel(r_ref, pos_ref, w1_ref, w2_ref, o_ref):
    r = r_ref[...]
    nxt = jnp.concatenate([r[1:], jnp.zeros((1, r.shape[1]), F32)], axis=0)
    half = r.shape[1]
    hid = (jnp.dot((r + pos_ref[0:1, :]).astype(BF16), w1_ref[:half], preferred_element_type=F32)
           + jnp.dot((nxt + pos_ref[1:2, :]).astype(BF16), w1_ref[half:], preferred_element_type=F32))
    hid = 0.5 * hid * (1.0 + lax.erf(hid * (2.0 ** -0.5)))
    o_ref[...] = jnp.dot(hid.astype(BF16), w2_ref[...], preferred_element_type=F32)


def compress_kv_pallas(kv, cmp_pos, w1, w2):
    _, B_, S_, G, Dh = kv.shape
    n_cmp = (S_ - CMP_LEN) // CMP_STRIDE + 1
    width = CMP_STRIDE * Dh
    rows = kv.transpose(0, 1, 3, 2, 4).reshape(2, B_ * G, S_ // CMP_STRIDE, width)
    out = pl.pallas_call(
        _compress_kernel,
        grid=(2, B_ * G),
        in_specs=[pl.BlockSpec((None, None, N_CMP_ROWS, width), lambda j, b: (j, b, 0, 0)),
                  pl.BlockSpec((None, 2, width), lambda j, b: (j, 0, 0)),
                  pl.BlockSpec((None, CMP_LEN * Dh, CMP_HIDDEN), lambda j, b: (j, 0, 0)),
                  pl.BlockSpec((None, CMP_HIDDEN, Dh), lambda j, b: (j, 0, 0))],
        out_specs=pl.BlockSpec((None, None, N_CMP_ROWS, Dh), lambda j, b: (j, b, 0, 0)),
        out_shape=jax.ShapeDtypeStruct((2, B_ * G, N_CMP_ROWS, Dh), F32),
        name="compress_kv",
    )(rows, cmp_pos.reshape(2, 2, width), w1.astype(BF16), w2.astype(BF16))
    return out[:, :, :n_cmp].reshape(2, B_, G, n_cmp, Dh).transpose(0, 1, 3, 2, 4)


def hybrid_mixer(x, norm_g, w_in, fox_f_bias, cmp_pos, cmp_w1, cmp_w2, w_branch, w_out):
    B_, S_, D = x.shape
    xf = x.reshape(B_ * S_, D)
    qkv, kvs, merge_logit, small = in_projection(xf, norm_g, w_in)
    heads = lambda t, n: t.reshape(B_, S_, n, HEAD_DIM)
    fq, fk, fv, nq = (qkv[:, i * BRANCH_WIDTH:(i + 1) * BRANCH_WIDTH] for i in range(4))
    kc, vc, ksl, vsl, kwn, vwn = (heads(kvs[:, i * KV_WIDTH:(i + 1) * KV_WIDTH], NSA_GROUPS) for i in range(6))
    f_logit = small[:, :FOX_HEADS].reshape(B_, S_, FOX_HEADS)
    gate_logit = small[:, 128:128 + NSA_HEADS * N_NSA_BRANCH].reshape(B_, S_, NSA_HEADS, N_NSA_BRANCH)
    log_f = jax.nn.log_sigmoid(f_logit + fox_f_bias.astype(F32))
    y_fox = fox_attention(heads(fq, FOX_HEADS), heads(fk, FOX_HEADS), heads(fv, FOX_HEADS), log_f)
    cos, sin = rope_tables(jnp.arange(S_, dtype=F32))
    q_nsa = apply_partial_rope(heads(nq, NSA_HEADS), cos, sin)
    k_slc = apply_partial_rope(ksl, cos, sin)
    k_win = apply_partial_rope(kwn, cos, sin)
    cmp = compress_kv_pallas(jnp.stack([kc, vc]), cmp_pos, cmp_w1, cmp_w2)
    n_cmp = cmp.shape[2]
    cmp_end = jnp.arange(n_cmp, dtype=F32) * CMP_STRIDE + (CMP_LEN - 1)
    k_cmp = apply_partial_rope(cmp[0], *rope_tables(cmp_end))
    y_nsa = nsa_attention(q_nsa, k_cmp, cmp[1], k_slc, vsl, k_win, vwn, gate_logit)
    return merge_out(xf, y_fox.reshape(B_ * S_, -1), y_nsa.reshape(B_ * S_, -1), merge_logit, w_branch, w_out)


def peer_ffn(h, norm_g, wq, subkeys, u, v):
    hn, q = peer_query(h, norm_g, wq)
    idx2, w = peer_topk(q, subkeys)
    act = jax.nn.gelu(peer_u(idx2, hn, pack_table(u)), approximate=False)
    return peer_v(idx2, w * act, pack_table(v))


def kernel(x, norm_mix, w_in, fox_f_bias, nsa_cmp_pos, nsa_cmp_w1, nsa_cmp_w2, w_branch, w_out,
           norm_ffn, peer_wq, peer_subkeys, peer_u, peer_v, norm_final):
    B_, S_, D = x.shape
    assert norm_mix.shape[0] == 1, "single-layer trunk"
    h = hybrid_mixer(x, norm_mix[0], w_in[0], fox_f_bias[0], nsa_cmp_pos[0], nsa_cmp_w1[0], nsa_cmp_w2[0],
                     w_branch[0], w_out[0])
    p = peer_ffn(h, norm_ffn[0], peer_wq[0], peer_subkeys[0], peer_u[0], peer_v[0])
    return final_norm(h, p, norm_final).reshape(B_, S_, D)
```

```python
import functools

import jax
import jax.numpy as jnp
import numpy as np
from jax import lax
from jax.experimental import pallas as pl
from jax.experimental.pallas import tpu as pltpu

D_MODEL = 1024
HEAD_DIM = 64
FOX_HEADS = 8
NSA_HEADS = 8
NSA_GROUPS = 2
NSA_HPG = NSA_HEADS // NSA_GROUPS
BRANCH_WIDTH = 512
N_BRANCH = 2
N_NSA_BRANCH = 3
ROPE_DIM = HEAD_DIM // 4
ROPE_THETA = 500000.0
Q_BLOCK = 128
CMP_LEN = 32
CMP_STRIDE = 16
CMP_HIDDEN = 2 * HEAD_DIM
SLC_LEN = 64
SLC_TOP = 16
WINDOW = 512
FORCED_SCORE = 1e9
NEG_INF = -1e30
PEER_HEADS = 8
N_KEYS = 128
N_EXPERTS = N_KEYS * N_KEYS
PEER_QDIM = 256
PEER_TOPK = 16
PEER_CHUNK = 128
RMS_EPS = 1e-6
KV_WIDTH = NSA_GROUPS * HEAD_DIM
IN_SPLITS = (BRANCH_WIDTH, BRANCH_WIDTH, BRANCH_WIDTH, FOX_HEADS, BRANCH_WIDTH,
             KV_WIDTH, KV_WIDTH, KV_WIDTH, KV_WIDTH, KV_WIDTH, KV_WIDTH,
             NSA_HEADS * N_NSA_BRANCH, N_BRANCH * D_MODEL)


def _rmsnorm_kernel(x_ref, g_ref, o_ref):
    x = x_ref[...]
    ms = jnp.mean(x * x, axis=-1, keepdims=True)
    o_ref[...] = x * lax.rsqrt(ms + RMS_EPS) * g_ref[...]


def rmsnorm_pallas(x2d, g, rows=512):
    t, d = x2d.shape
    return pl.pallas_call(
        _rmsnorm_kernel,
        grid=(t // rows,),
        in_specs=[pl.BlockSpec((rows, d), lambda i: (i, 0)),
                  pl.BlockSpec((1, d), lambda i: (0, 0))],
        out_specs=pl.BlockSpec((rows, d), lambda i: (i, 0)),
        out_shape=jax.ShapeDtypeStruct((t, d), jnp.float32),
    )(x2d, g.reshape(1, d))


PEER_SLOTS = PEER_HEADS * PEER_TOPK
PEER_TT = 64
PEER_CHUNK_SLOTS = 32
PEER_U_UNROLL = 4
HALF_D = D_MODEL // 2
ROW_SUB = HALF_D // 128
VMEM_LIMIT_PEER = 48 * 1024 * 1024


def pack_table(tab):
    n = tab.shape[0]
    b = lax.bitcast_convert_type(tab.astype(jnp.bfloat16), jnp.uint16).astype(jnp.uint32)
    b = b.reshape(n, ROW_SUB, 2, 128)
    return (b[:, :, 0, :] | (b[:, :, 1, :] << 16)).reshape(n * ROW_SUB, 128)


def _expert_slab(tab_ref, row):
    return tab_ref[pl.ds(pl.multiple_of(row, ROW_SUB), ROW_SUB), :]


def _unpack(w):
    lo = lax.bitcast_convert_type(w << 16, jnp.float32)
    hi = lax.bitcast_convert_type(w & jnp.uint32(0xFFFF0000), jnp.float32)
    return lo, hi


def _peer_u_kernel(idx_ref, x_ref, tab_ref, out_ref):
    lane = lax.broadcasted_iota(jnp.int32, (2 * ROW_SUB, 128), 1)
    lower = lax.broadcasted_iota(jnp.int32, (2 * ROW_SUB, 128), 0) >= ROW_SUB
    pair_lane = jnp.where(lower, lane - PEER_SLOTS // 2, lane)

    def token(t):
        row = x_ref[pl.ds(t, 1), :]
        chunks = [row[:, c * 128:(c + 1) * 128] for c in range(2 * ROW_SUB)]
        xl = jnp.concatenate(chunks[0::2] * 2, axis=0)
        xh = jnp.concatenate(chunks[1::2] * 2, axis=0)
        acc = jnp.zeros((2 * ROW_SUB, 128), jnp.float32)
        base = t * PEER_SLOTS
        for s in range(PEER_SLOTS // 2):
            w = jnp.concatenate([_expert_slab(tab_ref, idx_ref[base + s]),
                                 _expert_slab(tab_ref, idx_ref[base + s + PEER_SLOTS // 2])], axis=0)
            lo, hi = _unpack(w)
            part = jnp.sum(lo * xl + hi * xh, axis=1, keepdims=True)
            acc = jnp.where(pair_lane == s, part, acc)
        out_ref[pl.ds(t, 1), :] = jnp.sum(acc, axis=0, keepdims=True)

    def trip(i, carry):
        for j in range(PEER_U_UNROLL):
            token(i * PEER_U_UNROLL + j)
        return carry

    lax.fori_loop(0, PEER_TT // PEER_U_UNROLL, trip, 0)


def _peer_v_kernel(idx_ref, coef_ref, tab_ref, out_ref):
    n_acc = 2
    zero = jnp.zeros((2 * ROW_SUB, 128), jnp.float32)
    lower = lax.broadcasted_iota(jnp.int32, (2 * ROW_SUB, 128), 0) >= ROW_SUB
    half_slots = PEER_SLOTS // 2
    pairs_per_chunk = PEER_CHUNK_SLOTS // 2

    def tok(t, carry):
        def chunk(c, accs):
            acc_lo, acc_hi = list(accs[:n_acc]), list(accs[n_acc:])
            base = t * PEER_SLOTS + c * pairs_per_chunk
            for j in range(pairs_per_chunk):
                w = jnp.concatenate([_expert_slab(tab_ref, idx_ref[base + j]),
                                     _expert_slab(tab_ref, idx_ref[base + j + half_slots])], axis=0)
                lo, hi = _unpack(w)
                cf = jnp.where(lower, coef_ref[base + j + half_slots], coef_ref[base + j])
                acc_lo[j % n_acc] = acc_lo[j % n_acc] + cf * lo
                acc_hi[j % n_acc] = acc_hi[j % n_acc] + cf * hi
            return tuple(acc_lo + acc_hi)

        accs = lax.fori_loop(0, half_slots // pairs_per_chunk, chunk, (zero,) * (2 * n_acc))
        even = accs[0] + accs[1]
        odd = accs[2] + accs[3]
        even = even[:ROW_SUB] + even[ROW_SUB:]
        odd = odd[:ROW_SUB] + odd[ROW_SUB:]
        out_ref[pl.ds(t, 1), :] = jnp.concatenate(
            [half[r:r + 1, :] for r in range(ROW_SUB) for half in (even, odd)], axis=1)
        return carry

    lax.fori_loop(0, PEER_TT, tok, 0)


def _smem_spec():
    return pl.BlockSpec((PEER_TT * PEER_SLOTS,), lambda i: (i,), memory_space=pltpu.SMEM)


def _table_spec(rows):
    return pl.BlockSpec((rows, 128), lambda i: (0, 0), pipeline_mode=pl.Buffered(1))


def peer_u(idx, x, tab):
    t = x.shape[0]
    return pl.pallas_call(
        _peer_u_kernel,
        grid=(t // PEER_TT,),
        in_specs=[_smem_spec(), pl.BlockSpec((PEER_TT, D_MODEL), lambda i: (i, 0)), _table_spec(tab.shape[0])],
        out_specs=pl.BlockSpec((PEER_TT, PEER_SLOTS), lambda i: (i, 0)),
        out_shape=jax.ShapeDtypeStruct((t, PEER_SLOTS), jnp.float32),
        compiler_params=pltpu.CompilerParams(vmem_limit_bytes=VMEM_LIMIT_PEER),
        name="peer_u",
    )(idx.reshape(-1), x, tab)


def peer_v(idx, coef, tab):
    t = idx.shape[0]
    return pl.pallas_call(
        _peer_v_kernel,
        grid=(t // PEER_TT,),
        in_specs=[_smem_spec(), _smem_spec(), _table_spec(tab.shape[0])],
        out_specs=pl.BlockSpec((PEER_TT, D_MODEL), lambda i: (i, 0)),
        out_shape=jax.ShapeDtypeStruct((t, D_MODEL), jnp.float32),
        compiler_params=pltpu.CompilerParams(vmem_limit_bytes=VMEM_LIMIT_PEER),
        name="peer_v",
    )(idx.reshape(-1), coef.reshape(-1), tab)


PK_TT = 256
PK_A_BLOCKS = ((0, 16), (1, 8), (2, 5), (3, 4))
PK_B_BLOCKS = ((0, 4, 16), (1, 4, 8), (2, 4, 5))
NEG_HUGE = -3.0e38
PK_HEADS_PER_TRIP = 2


def _extract_top(problems, flat, n):
    vals = [[] for _ in problems]
    picked = [[] for _ in problems]
    cur = [v for v, _ in problems]
    for _ in range(n):
        for i, (_, payload) in enumerate(problems):
            m = jnp.max(cur[i], axis=0, keepdims=True)
            pos = jnp.min(jnp.where(cur[i] == m, flat, 1e9), axis=0, keepdims=True)
            hit = flat == pos
            picked[i].append(pos if payload is None else jnp.sum(jnp.where(hit, payload, 0.0), axis=0, keepdims=True))
            cur[i] = jnp.where(hit, NEG_HUGE, cur[i])
            vals[i].append(m)
    return [jnp.concatenate(v, axis=0) for v in vals], [jnp.concatenate(p, axis=0) for p in picked]


def _candidate_grid(first, second, combine):
    up8 = lambda n: -(-n // 8) * 8
    return jnp.concatenate([combine(first[a:a + 1], second[0:up8(nb)]) for a, nb in PK_A_BLOCKS]
                           + [combine(first[0:up8(a1)], second[b:b + 1]) for b, _, a1 in PK_B_BLOCKS], axis=0)


def _candidate_valid_and_flat():
    up8 = lambda n: -(-n // 8) * 8
    valid, flat = [], []
    for a, nb in PK_A_BLOCKS:
        r = lax.broadcasted_iota(jnp.int32, (up8(nb), PK_TT), 0)
        valid.append(r < nb)
        flat.append(a * PEER_TOPK + r)
    for b, a0, a1 in PK_B_BLOCKS:
        r = lax.broadcasted_iota(jnp.int32, (up8(a1), PK_TT), 0)
        valid.append((r >= a0) & (r < a1))
        flat.append(r * PEER_TOPK + b)
    valid = jnp.concatenate(valid, axis=0)
    return valid, jnp.where(valid, jnp.concatenate(flat, axis=0).astype(F32), 1e9)


def _peer_topk_kernel(q_ref, ka_ref, eid_ref, w_ref):
    key_row = lax.broadcasted_iota(jnp.int32, (N_KEYS, PK_TT), 0).astype(F32)
    valid, flat = _candidate_valid_and_flat()

    def heads(i, carry):
        scores = []
        for j in range(2 * PK_HEADS_PER_TRIP):
            q = q_ref[2 * PK_HEADS_PER_TRIP * i + j]
            q_hi = _trunc_bf16(q)
            qa = jnp.concatenate([q_hi.astype(BF16), (q - q_hi).astype(BF16), q_hi.astype(BF16)], axis=1)
            scores.append(lax.dot_general(ka_ref[2 * PK_HEADS_PER_TRIP * i + j], qa, (((1,), (1,)), ((), ())),
                                          preferred_element_type=F32))
        sub_s, sub_i = _extract_top([(s, None) for s in scores], key_row, PEER_TOPK)
        cands = []
        for j in range(PK_HEADS_PER_TRIP):
            s1, s2, i1, i2 = sub_s[2 * j], sub_s[2 * j + 1], sub_i[2 * j], sub_i[2 * j + 1]
            cands.append((jnp.where(valid, _candidate_grid(s1, s2, lambda x, y: x + y), NEG_HUGE),
                          _candidate_grid(i1, i2, lambda x, y: x * N_KEYS + y)))
        top_s, top_e = _extract_top(cands, flat, PEER_TOPK)
        for j in range(PK_HEADS_PER_TRIP):
            e = jnp.exp(top_s[j] - top_s[j][0:1])
            w_ref[PK_HEADS_PER_TRIP * i + j] = e / jnp.sum(e, axis=0, keepdims=True)
            eid_ref[PK_HEADS_PER_TRIP * i + j] = top_e[j].astype(jnp.int32) * ROW_SUB
        return carry

    lax.fori_loop(0, PEER_HEADS // PK_HEADS_PER_TRIP, heads, 0)


def peer_topk(qt, subkeys):
    hp, t, d = qt.shape
    k_hi, k_lo = _split2(subkeys.reshape(hp, N_KEYS, d))
    ka = jnp.concatenate([k_hi, k_hi, k_lo], axis=2)
    eid, w = pl.pallas_call(
        _peer_topk_kernel,
        grid=(t // PK_TT,),
        in_specs=[pl.BlockSpec((hp, PK_TT, d), lambda i: (0, i, 0)),
                  pl.BlockSpec((hp, N_KEYS, 3 * d), lambda i: (0, 0, 0))],
        out_specs=[pl.BlockSpec((PEER_HEADS, PEER_TOPK, PK_TT), lambda i: (0, 0, i)),
                   pl.BlockSpec((PEER_HEADS, PEER_TOPK, PK_TT), lambda i: (0, 0, i))],
        out_shape=[jax.ShapeDtypeStruct((PEER_HEADS, PEER_TOPK, t), jnp.int32),
                   jax.ShapeDtypeStruct((PEER_HEADS, PEER_TOPK, t), F32)],
        name="peer_topk",
    )(qt, ka)
    to_slots = lambda a: a.transpose(2, 0, 1).reshape(t, PEER_SLOTS)
    return to_slots(eid), to_slots(w)


def rmsnorm(x, g):
    xf = x.astype(jnp.float32)
    y = xf * lax.rsqrt(jnp.mean(xf * xf, axis=-1, keepdims=True) + RMS_EPS)
    return (y * g.astype(jnp.float32)).astype(x.dtype)


def split_columns(t, sizes):
    offs = np.cumsum((0,) + tuple(sizes))
    return [t[..., int(a):int(b)] for a, b in zip(offs[:-1], offs[1:])]


def rope_tables(pos):
    inv = jnp.power(ROPE_THETA, -jnp.arange(0, ROPE_DIM, 2, dtype=jnp.float32) / ROPE_DIM)
    ang = pos[:, None] * inv[None, :]
    return jnp.cos(ang), jnp.sin(ang)


def apply_partial_rope(x, cos, sin):
    half = ROPE_DIM // 2
    xr = x[..., :ROPE_DIM].astype(jnp.float32)
    x1, x2 = xr[..., :half], xr[..., half:]
    c, s = cos[:, None, :], sin[:, None, :]
    rot = jnp.concatenate([x1 * c - x2 * s, x1 * s + x2 * c], axis=-1)
    return jnp.concatenate([rot.astype(x.dtype), x[..., ROPE_DIM:]], axis=-1)


BF16 = jnp.bfloat16
F32 = jnp.float32


def _flash_step(k, vt, qt, mask, carry):
    m, l, acc = carry
    s = jnp.dot(k, qt, preferred_element_type=F32)
    s_vis = s if mask is None else jnp.where(mask, s, NEG_INF)
    m_new = jnp.maximum(m, jnp.max(s_vis, axis=0, keepdims=True))
    alpha = jnp.exp(m - m_new)
    p = jnp.exp(s - m_new)
    if mask is not None:
        p = jnp.where(mask, p, 0.0)
    l = alpha * l + jnp.sum(p, axis=0, keepdims=True)
    acc = alpha * acc + jnp.dot(vt, p.astype(BF16), preferred_element_type=F32)
    return m_new, l, acc


def _flash_steps(ks, vts, qts, mask, carries):
    ss = [jnp.dot(k, qt, preferred_element_type=F32) for k, qt in zip(ks, qts)]
    stats, ps = [], []
    for s, (m, l, _) in zip(ss, carries):
        s_vis = s if mask is None else jnp.where(mask, s, NEG_INF)
        m_new = jnp.maximum(m, jnp.max(s_vis, axis=0, keepdims=True))
        alpha = jnp.exp(m - m_new)
        p = jnp.exp(s - m_new)
        if mask is not None:
            p = jnp.where(mask, p, 0.0)
        stats.append((m_new, alpha, alpha * l + jnp.sum(p, axis=0, keepdims=True)))
        ps.append(p.astype(BF16))
    return tuple((m_new, l, alpha * acc + jnp.dot(vt, p, preferred_element_type=F32))
                 for (m_new, alpha, l), p, vt, (_, _, acc) in zip(stats, ps, vts, carries))


def _flash_init(n):
    return (jnp.full((1, n), NEG_INF, F32), jnp.zeros((1, n), F32), jnp.zeros((HEAD_DIM, n), F32))


FOX_TQ = 256
FOX_KV = 512
FOX_DK = 256


FOX_HP = 4


def _fox_kernel(qt_ref, k_ref, vt_ref, o_ref):
    qi = pl.program_id(1)
    q0 = qi * FOX_TQ
    qts = [qt_ref[h] for h in range(FOX_HP)]
    qpos = q0 + lax.broadcasted_iota(jnp.int32, (1, FOX_TQ), 1)
    krow = lax.broadcasted_iota(jnp.int32, (FOX_KV, FOX_TQ), 0)

    def full_step(kt, carry):
        return _flash_steps([k_ref[h, kt] for h in range(FOX_HP)], [vt_ref[h, kt] for h in range(FOX_HP)],
                            qts, None, carry)

    def diag_step(kt, carry):
        mask = (kt * FOX_KV + krow) <= qpos
        return _flash_steps([k_ref[h, kt] for h in range(FOX_HP)], [vt_ref[h, kt] for h in range(FOX_HP)],
                            qts, mask, carry)

    n_full = (q0 + 1) // FOX_KV
    n_all = (q0 + FOX_TQ + FOX_KV - 1) // FOX_KV
    carry = lax.fori_loop(0, n_full, full_step, tuple(_flash_init(FOX_TQ) for _ in range(FOX_HP)))
    carry = lax.fori_loop(n_full, n_all, diag_step, carry)
    for h in range(FOX_HP):
        _, l, acc = carry[h]
        o_ref[h] = acc / l


def _trunc_bf16(x):
    bits = lax.bitcast_convert_type(x, jnp.uint32) & jnp.uint32(0xFFFF0000)
    return lax.bitcast_convert_type(bits, F32)


def _split3(c):
    c1 = _trunc_bf16(c)
    r = c - c1
    c2 = _trunc_bf16(r)
    return c1.astype(BF16), c2.astype(BF16), (r - c2).astype(BF16)


def _split2(x):
    hi = _trunc_bf16(x)
    return hi.astype(BF16), (x - hi).astype(BF16)


def _aug_q(q):
    hi, lo = _split2(q)
    return jnp.concatenate([hi, hi, lo], axis=-1)


def _aug_k(k):
    hi, lo = _split2(k)
    return jnp.concatenate([hi, lo, hi], axis=-1)


def fox_attention(q, k, v, log_f):
    B_, S_, H, Dh = q.shape
    KV_TILE = FOX_KV
    n_qt, n_kt = S_ // FOX_TQ, S_ // KV_TILE
    c = jnp.cumsum(log_f, axis=1)
    c1, c2, c3 = _split3(c)
    one = jnp.ones_like(c1)
    pad = jnp.zeros((B_, S_, H, FOX_DK - 3 * Dh - 6), BF16)
    col = lambda t: t[..., None]
    qa = jnp.concatenate([_aug_q(q * (Dh ** -0.5)), col(one), col(one), col(one),
                          col(c1), col(c2), col(c3), pad], axis=-1)
    ka = jnp.concatenate([_aug_k(k), col(-c1), col(-c2), col(-c3),
                          col(one), col(one), col(one), pad], axis=-1)
    HP, HG = FOX_HP, H // FOX_HP
    qt = qa.reshape(B_, n_qt, FOX_TQ, HG, HP, FOX_DK).transpose(0, 3, 1, 4, 5, 2).reshape(B_ * HG * n_qt, HP, FOX_DK, FOX_TQ)
    kk = ka.reshape(B_, n_kt, KV_TILE, HG, HP, FOX_DK).transpose(0, 3, 4, 1, 2, 5).reshape(B_ * HG, HP, n_kt, KV_TILE, FOX_DK)
    vt = v.astype(BF16).reshape(B_, n_kt, KV_TILE, HG, HP, Dh).transpose(0, 3, 4, 1, 5, 2).reshape(B_ * HG, HP, n_kt, Dh, KV_TILE)
    out = pl.pallas_call(
        _fox_kernel,
        grid=(B_ * HG, n_qt),
        in_specs=[pl.BlockSpec((None, HP, FOX_DK, FOX_TQ), lambda b, i: (b * n_qt + i, 0, 0, 0)),
                  pl.BlockSpec((None, HP, n_kt, KV_TILE, FOX_DK), lambda b, i: (b, 0, 0, 0, 0)),
                  pl.BlockSpec((None, HP, n_kt, Dh, KV_TILE), lambda b, i: (b, 0, 0, 0, 0))],
        out_specs=pl.BlockSpec((None, HP, Dh, FOX_TQ), lambda b, i: (b * n_qt + i, 0, 0, 0)),
        out_shape=jax.ShapeDtypeStruct((B_ * HG * n_qt, HP, Dh, FOX_TQ), F32),
        name="fox_attention",
    )(qt, kk, vt)
    return out.reshape(B_, HG, n_qt, HP, Dh, FOX_TQ).transpose(0, 2, 5, 1, 3, 4).reshape(B_, S_, H * Dh)


def compress_kv(k, v, cmp_pos, w1, w2):
    B_, S_ = k.shape[0], k.shape[1]
    n_cmp = (S_ - CMP_LEN) // CMP_STRIDE + 1
    starts = np.arange(n_cmp) * CMP_STRIDE
    idx = starts[:, None] + np.arange(CMP_LEN)[None, :]

    def phi(t, j):
        blocks = t[:, idx] + cmp_pos[j][None, None, :, None, :]
        flat = jnp.swapaxes(blocks, 2, 3).reshape(B_, n_cmp, NSA_GROUPS, CMP_LEN * HEAD_DIM)
        return jax.nn.gelu(flat @ w1[j], approximate=False) @ w2[j]

    return phi(k, 0), phi(v, 1), jnp.asarray(starts + CMP_LEN - 1, dtype=jnp.int32)


NSA_TQ = 128
NSA_N = NSA_HPG * NSA_TQ
N_CMP_PAD = 256
N_SLC = 64
NSA_DK = 3 * HEAD_DIM
NSA_KV = 512
WIN_TILES = WINDOW // NSA_TQ + 1
WIN_KEYS = WIN_TILES * NSA_TQ


def _nsa_kernel(qt_ref, kc_ref, vct_ref, ks_ref, vst_ref, kw_ref, vwt_ref, gate_ref, ovl_ref, o_ref, sel_ref):
    qi = pl.program_id(1)
    q0 = qi * NSA_TQ
    qt = qt_ref[...]
    lane = lax.broadcasted_iota(jnp.int32, (1, NSA_N), 1)
    qpos = q0 + (lane & (NSA_TQ - 1))

    hn = NSA_N // 2
    qts = (qt[:, :hn], qt[:, hn:])
    qpos_h = qpos[:, :hn]

    w0 = jnp.maximum(qi - WINDOW // NSA_TQ, 0)
    k_w = jnp.concatenate([kw_ref[w0 + i] for i in range(WIN_TILES)], axis=0)
    vt_w = jnp.concatenate([vwt_ref[w0 + i] for i in range(WIN_TILES)], axis=1)
    s = jnp.dot(kc_ref[...], qt, preferred_element_type=F32)
    s_w = [jnp.dot(k_w, qts[h], preferred_element_type=F32) for h in range(2)]

    cmp_end = lax.broadcasted_iota(jnp.int32, (N_CMP_PAD, NSA_N), 0) * CMP_STRIDE + (CMP_LEN - 1)
    mc = cmp_end <= qpos
    m = jnp.max(jnp.where(mc, s, NEG_INF), axis=0, keepdims=True)
    p = jnp.where(mc, jnp.exp(s - m), 0.0)
    l = jnp.sum(p, axis=0, keepdims=True)
    pc = p * jnp.where(l > 0.0, 1.0 / l, 0.0)
    o_cmp = jnp.dot(vct_ref[...], pc.astype(BF16), preferred_element_type=F32)
    pcs = (pc[:, 0:NSA_TQ] + pc[:, NSA_TQ:2 * NSA_TQ]) + (pc[:, 2 * NSA_TQ:3 * NSA_TQ] + pc[:, 3 * NSA_TQ:4 * NSA_TQ])
    pcs_hi = pcs.astype(BF16)
    pcs_lo = (pcs - pcs_hi.astype(F32)).astype(BF16)
    ovl = ovl_ref[...]
    imp = jnp.dot(ovl, pcs_hi, preferred_element_type=F32) + jnp.dot(ovl, pcs_lo, preferred_element_type=F32)

    rel = qpos_h - (w0 * NSA_TQ + lax.broadcasted_iota(jnp.int32, (WIN_KEYS, hn), 0))
    in_win = (rel >= 0) & (rel < WINDOW)
    outs_w = []
    for h in range(2):
        m = jnp.max(jnp.where(in_win, s_w[h], NEG_INF), axis=0, keepdims=True)
        p = jnp.where(in_win, jnp.exp(s_w[h] - m), 0.0)
        l = jnp.sum(p, axis=0, keepdims=True)
        outs_w.append(jnp.dot(vt_w, p.astype(BF16), preferred_element_type=F32) / l)
    acc_w = jnp.concatenate(outs_w, axis=1)

    blk = lax.broadcasted_iota(jnp.int32, (N_SLC, NSA_TQ), 0)
    q_blk = (q0 + lax.broadcasted_iota(jnp.int32, (N_SLC, NSA_TQ), 1)) >> 6
    forced = (blk == 0) | (blk == q_blk) | (blk == q_blk - 1)
    score = jnp.where(forced, FORCED_SCORE, jnp.where(blk <= q_blk, imp, -1.0))
    rank = jnp.zeros((N_SLC, NSA_TQ), jnp.int32)
    for i in range(N_SLC):
        row = score[i:i + 1, :]
        beats = (row > score) | ((row == score) & (blk > i))
        rank = rank + beats.astype(jnp.int32)
    sel = (rank < SLC_TOP).astype(F32)
    sel_ref[...] = jnp.concatenate([sel] * (NSA_HPG // 2), axis=1)

    krow = lax.broadcasted_iota(jnp.int32, (NSA_KV, hn), 0)
    blocks_per_tile = NSA_KV // SLC_LEN

    def slc_chains(kt, carry):
        rows = [jnp.broadcast_to(sel_ref[pl.ds(blocks_per_tile * kt + j, 1), :], (SLC_LEN, hn))
                for j in range(blocks_per_tile)]
        mask = (jnp.concatenate(rows, axis=0) > 0.0) & ((kt * NSA_KV + krow) <= qpos_h)
        k, vt = ks_ref[kt], vst_ref[kt]
        return _flash_steps([k, k], [vt, vt], qts, mask, carry)

    slc = lax.fori_loop(0, (q0 + NSA_TQ - 1) // NSA_KV + 1, slc_chains, (_flash_init(hn), _flash_init(hn)))
    acc_s = jnp.concatenate([slc[0][2] / slc[0][1], slc[1][2] / slc[1][1]], axis=1)

    g = jax.nn.sigmoid(gate_ref[...])
    o_ref[...] = g[0:1] * o_cmp + g[1:2] * acc_s + g[2:3] * acc_w


def nsa_attention(q, k_cmp, v_cmp, k_slc, v_slc, k_win, v_win, gate_logit):
    B_, S_, H, Dh = q.shape
    G = NSA_GROUPS
    n_qt = S_ // NSA_TQ
    n_cmp = k_cmp.shape[1]
    cs = np.arange(N_CMP_PAD)[None, :] * CMP_STRIDE
    ss = np.arange(N_SLC)[:, None] * SLC_LEN
    ov = np.clip(np.minimum(cs + CMP_LEN, ss + SLC_LEN) - np.maximum(cs, ss), 0, None) / CMP_LEN
    ov[:, n_cmp:] = 0.0
    ovl = jnp.asarray(ov, dtype=BF16)
    qs = _aug_q(q * (Dh ** -0.5)).reshape(B_, n_qt, NSA_TQ, G, NSA_HPG, NSA_DK)
    qt = qs.transpose(0, 3, 1, 5, 4, 2).reshape(B_ * G * n_qt, NSA_DK, NSA_N)
    gt = gate_logit.astype(F32).reshape(B_, n_qt, NSA_TQ, G, NSA_HPG, N_NSA_BRANCH)
    gt = gt.transpose(0, 3, 1, 5, 4, 2).reshape(B_ * G * n_qt, N_NSA_BRANCH, NSA_N)
    keys = lambda t, kv: _aug_k(t).reshape(B_, S_ // kv, kv, G, NSA_DK).transpose(0, 3, 1, 2, 4).reshape(B_ * G, S_ // kv, kv, NSA_DK)
    vals = lambda t, kv: t.astype(BF16).reshape(B_, S_ // kv, kv, G, Dh).transpose(0, 3, 1, 4, 2).reshape(B_ * G, S_ // kv, Dh, kv)
    padc = ((0, 0), (0, N_CMP_PAD - n_cmp), (0, 0), (0, 0))
    kc = _aug_k(jnp.pad(k_cmp, padc)).transpose(0, 2, 1, 3).reshape(B_ * G, N_CMP_PAD, NSA_DK)
    vct = jnp.pad(v_cmp, padc).astype(BF16).transpose(0, 2, 3, 1).reshape(B_ * G, Dh, N_CMP_PAD)
    per_bg = lambda *blk: pl.BlockSpec((None,) + blk, lambda b, i: (b,) + (0,) * len(blk))
    per_tile = lambda *blk: pl.BlockSpec((None,) + blk, lambda b, i: (b * n_qt + i,) + (0,) * len(blk))
    out = pl.pallas_call(
        _nsa_kernel,
        grid=(B_ * G, n_qt),
        in_specs=[per_tile(NSA_DK, NSA_N),
                  per_bg(N_CMP_PAD, NSA_DK), per_bg(Dh, N_CMP_PAD),
                  per_bg(S_ // NSA_KV, NSA_KV, NSA_DK), per_bg(S_ // NSA_KV, Dh, NSA_KV),
                  per_bg(n_qt, NSA_TQ, NSA_DK), per_bg(n_qt, Dh, NSA_TQ),
                  per_tile(N_NSA_BRANCH, NSA_N),
                  pl.BlockSpec((N_SLC, N_CMP_PAD), lambda b, i: (0, 0))],
        out_specs=per_tile(Dh, NSA_N),
        out_shape=jax.ShapeDtypeStruct((B_ * G * n_qt, Dh, NSA_N), F32),
        scratch_shapes=[pltpu.VMEM((N_SLC, NSA_N // 2), F32)],
        name="nsa_attention",
    )(qt, kc, vct, keys(k_slc, NSA_KV), vals(v_slc, NSA_KV), keys(k_win, NSA_TQ), vals(v_win, NSA_TQ), gt, ovl)
    return out.reshape(B_, G, n_qt, Dh, NSA_HPG, NSA_TQ).transpose(0, 2, 5, 1, 4, 3).reshape(B_, S_, H * Dh)


DENSE_TM = 256
SEC_QKV = 4 * BRANCH_WIDTH
SEC_KV = 6 * KV_WIDTH
SEC_MERGE = N_BRANCH * D_MODEL
SEC_SMALL = 256
IN_SECTIONS = (SEC_QKV, SEC_KV, SEC_MERGE, SEC_SMALL)


def _rms(x, g):
    return x * lax.rsqrt(jnp.mean(x * x, axis=-1, keepdims=True) + RMS_EPS) * g


def _row_spec(width):
    return pl.BlockSpec((DENSE_TM, width), lambda i: (i, 0))


def _whole_spec(shape):
    return pl.BlockSpec(shape, lambda i: (0,) * len(shape))


def _in_proj_kernel(x_ref, g_ref, w_ref, *o_refs):
    xn = _rms(x_ref[...], g_ref[...]).astype(BF16)
    off = 0
    for o_ref, width in zip(o_refs, IN_SECTIONS):
        o_ref[...] = jnp.dot(xn, w_ref[:, off:off + width], preferred_element_type=F32)
        off += width


def in_projection(x, g, w_in):
    t, d = x.shape
    fq, fk, fv, fl, nq, kc, vc, ksl, vsl, kwn, vwn, gl, ml = split_columns(w_in, IN_SPLITS)
    pad = lambda w: jnp.pad(w, ((0, 0), (0, 128 - w.shape[1])))
    w = jnp.concatenate([fq, fk, fv, nq, kc, vc, ksl, vsl, kwn, vwn, ml, pad(fl), pad(gl)], axis=1).astype(BF16)
    return pl.pallas_call(
        _in_proj_kernel,
        grid=(t // DENSE_TM,),
        in_specs=[_row_spec(d), _whole_spec((1, d)), _whole_spec(w.shape)],
        out_specs=[_row_spec(s) for s in IN_SECTIONS],
        out_shape=[jax.ShapeDtypeStruct((t, s), F32) for s in IN_SECTIONS],
        name="in_projection",
    )(x, g.reshape(1, d), w)


def _merge_kernel(x_ref, yf_ref, yn_ref, ml_ref, wb_ref, wo_ref, o_ref):
    g = jax.nn.sigmoid(ml_ref[...])
    up_f = jnp.dot(yf_ref[...].astype(BF16), wb_ref[0], preferred_element_type=F32)
    up_n = jnp.dot(yn_ref[...].astype(BF16), wb_ref[1], preferred_element_type=F32)
    merged = g[:, :D_MODEL] * up_f + g[:, D_MODEL:] * up_n
    o_ref[...] = x_ref[...] + jnp.dot(merged.astype(BF16), wo_ref[...], preferred_element_type=F32)


def merge_out(x, y_fox, y_nsa, merge_logit, w_branch, w_out):
    t, d = x.shape
    return pl.pallas_call(
        _merge_kernel,
        grid=(t // DENSE_TM,),
        in_specs=[_row_spec(d), _row_spec(BRANCH_WIDTH), _row_spec(BRANCH_WIDTH), _row_spec(SEC_MERGE),
                  _whole_spec(w_branch.shape), _whole_spec(w_out.shape)],
        out_specs=_row_spec(d),
        out_shape=jax.ShapeDtypeStruct((t, d), F32),
        name="merge_out",
    )(x, y_fox, y_nsa, merge_logit, w_branch.astype(BF16), w_out.astype(BF16))


def _peer_q_kernel(x_ref, g_ref, w_ref, hn_ref, q_ref):
    hn = _rms(x_ref[...], g_ref[...])
    hn_ref[...] = hn
    q = jnp.dot(hn.astype(BF16), w_ref[...], preferred_element_type=F32)
    d = PEER_QDIM // 2
    for j in range(2 * PEER_HEADS):
        q_ref[j] = q[:, j * d:(j + 1) * d]


def peer_query(x, g, wq):
    t, d = x.shape
    hp, dq = 2 * PEER_HEADS, PEER_QDIM // 2
    return pl.pallas_call(
        _peer_q_kernel,
        grid=(t // DENSE_TM,),
        in_specs=[_row_spec(d), _whole_spec((1, d)), _whole_spec(wq.shape)],
        out_specs=[_row_spec(d), pl.BlockSpec((hp, DENSE_TM, dq), lambda i: (0, i, 0))],
        out_shape=[jax.ShapeDtypeStruct((t, d), F32), jax.ShapeDtypeStruct((hp, t, dq), F32)],
        name="peer_query",
    )(x, g.reshape(1, d), wq.astype(BF16))


def _final_kernel(h_ref, p_ref, g_ref, o_ref):
    o_ref[...] = _rms(h_ref[...] + p_ref[...], g_ref[...])


def final_norm(h, p, g):
    t, d = h.shape
    return pl.pallas_call(
        _final_kernel,
        grid=(t // DENSE_TM,),
        in_specs=[_row_spec(d), _row_spec(d), _whole_spec((1, d))],
        out_specs=_row_spec(d),
        out_shape=jax.ShapeDtypeStruct((t, d), F32),
        name="final_norm",
    )(h, p, g.reshape(1, d))


N_CMP_ROWS = 256


def _compress_kernel(r_ref, pos_ref, w1_ref, w2_ref, o_ref):
    r = r_ref[...]
    nxt = jnp.concatenate([r[1:], jnp.zeros((1, r.shape[1]), F32)], axis=0)
    half = r.shape[1]
    hid = (jnp.dot((r + pos_ref[0:1, :]).astype(BF16), w1_ref[:half], preferred_element_type=F32)
           + jnp.dot((nxt + pos_ref[1:2, :]).astype(BF16), w1_ref[half:], preferred_element_type=F32))
    hid = 0.5 * hid * (1.0 + lax.erf(hid * (2.0 ** -0.5)))
    o_ref[...] = jnp.dot(hid.astype(BF16), w2_ref[...], preferred_element_type=F32)


def compress_kv_pallas(kv, cmp_pos, w1, w2):
    _, B_, S_, G, Dh = kv.shape
    n_cmp = (S_ - CMP_LEN) // CMP_STRIDE + 1
    width = CMP_STRIDE * Dh
    rows = kv.transpose(0, 1, 3, 2, 4).reshape(2, B_ * G, S_ // CMP_STRIDE, width)
    out = pl.pallas_call(
        _compress_kernel,
        grid=(2, B_ * G),
        in_specs=[pl.BlockSpec((None, None, N_CMP_ROWS, width), lambda j, b: (j, b, 0, 0)),
                  pl.BlockSpec((None, 2, width), lambda j, b: (j, 0, 0)),
                  pl.BlockSpec((None, CMP_LEN * Dh, CMP_HIDDEN), lambda j, b: (j, 0, 0)),
                  pl.BlockSpec((None, CMP_HIDDEN, Dh), lambda j, b: (j, 0, 0))],
        out_specs=pl.BlockSpec((None, None, N_CMP_ROWS, Dh), lambda j, b: (j, b, 0, 0)),
        out_shape=jax.ShapeDtypeStruct((2, B_ * G, N_CMP_ROWS, Dh), F32),
        name="compress_kv",
    )(rows, cmp_pos.reshape(2, 2, width), w1.astype(BF16), w2.astype(BF16))
    return out[:, :, :n_cmp].reshape(2, B_, G, n_cmp, Dh).transpose(0, 1, 3, 2, 4)


def hybrid_mixer(x, norm_g, w_in, fox_f_bias, cmp_pos, cmp_w1, cmp_w2, w_branch, w_out):
    B_, S_, D = x.shape
    xf = x.reshape(B_ * S_, D)
    qkv, kvs, merge_logit, small = in_projection(xf, norm_g, w_in)
    heads = lambda t, n: t.reshape(B_, S_, n, HEAD_DIM)
    fq, fk, fv, nq = (qkv[:, i * BRANCH_WIDTH:(i + 1) * BRANCH_WIDTH] for i in range(4))
    kc, vc, ksl, vsl, kwn, vwn = (heads(kvs[:, i * KV_WIDTH:(i + 1) * KV_WIDTH], NSA_GROUPS) for i in range(6))
    f_logit = small[:, :FOX_HEADS].reshape(B_, S_, FOX_HEADS)
    gate_logit = small[:, 128:128 + NSA_HEADS * N_NSA_BRANCH].reshape(B_, S_, NSA_HEADS, N_NSA_BRANCH)
    log_f = jax.nn.log_sigmoid(f_logit + fox_f_bias.astype(F32))
    y_fox = fox_attention(heads(fq, FOX_HEADS), heads(fk, FOX_HEADS), heads(fv, FOX_HEADS), log_f)
    cos, sin = rope_tables(jnp.arange(S_, dtype=F32))
    q_nsa = apply_partial_rope(heads(nq, NSA_HEADS), cos, sin)
    k_slc = apply_partial_rope(ksl, cos, sin)
    k_win = apply_partial_rope(kwn, cos, sin)
    cmp = compress_kv_pallas(jnp.stack([kc, vc]), cmp_pos, cmp_w1, cmp_w2)
    n_cmp = cmp.shape[2]
    cmp_end = jnp.arange(n_cmp, dtype=F32) * CMP_STRIDE + (CMP_LEN - 1)
    k_cmp = apply_partial_rope(cmp[0], *rope_tables(cmp_end))
    y_nsa = nsa_attention(q_nsa, k_cmp, cmp[1], k_slc, vsl, k_win, vwn, gate_logit)
    return merge_out(xf, y_fox.reshape(B_ * S_, -1), y_nsa.reshape(B_ * S_, -1), merge_logit, w_branch, w_out)


def peer_ffn(h, norm_g, wq, subkeys, u, v):
    hn, q = peer_query(h, norm_g, wq)
    idx2, w = peer_topk(q, subkeys)
    act = jax.nn.gelu(peer_u(idx2, hn, pack_table(u)), approximate=False)
    return peer_v(idx2, w * act, pack_table(v))


def kernel(x, norm_mix, w_in, fox_f_bias, nsa_cmp_pos, nsa_cmp_w1, nsa_cmp_w2, w_branch, w_out,
           norm_ffn, peer_wq, peer_subkeys, peer_u, peer_v, norm_final):
    B_, S_, D = x.shape
    assert norm_mix.shape[0] == 1, "single-layer trunk"
    h = hybrid_mixer(x, norm_mix[0], w_in[0], fox_f_bias[0], nsa_cmp_pos[0], nsa_cmp_w1[0], nsa_cmp_w2[0],
                     w_branch[0], w_out[0])
    p = peer_ffn(h, norm_ffn[0], peer_wq[0], peer_subkeys[0], peer_u[0], peer_v[0])
    return final_norm(h, p, norm_final).reshape(B_, S_, D)
```

```python
import functools

import jax
import jax.numpy as jnp
import numpy as np
from jax import lax
from jax.experimental import pallas as pl
from jax.experimental.pallas import tpu as pltpu

D_MODEL = 1024
HEAD_DIM = 64
FOX_HEADS = 8
NSA_HEADS = 8
NSA_GROUPS = 2
NSA_HPG = NSA_HEADS // NSA_GROUPS
BRANCH_WIDTH = 512
N_BRANCH = 2
N_NSA_BRANCH = 3
ROPE_DIM = HEAD_DIM // 4
ROPE_THETA = 500000.0
Q_BLOCK = 128
CMP_LEN = 32
CMP_STRIDE = 16
CMP_HIDDEN = 2 * HEAD_DIM
SLC_LEN = 64
SLC_TOP = 16
WINDOW = 512
FORCED_SCORE = 1e9
NEG_INF = -1e30
PEER_HEADS = 8
N_KEYS = 128
N_EXPERTS = N_KEYS * N_KEYS
PEER_QDIM = 256
PEER_TOPK = 16
PEER_CHUNK = 128
RMS_EPS = 1e-6
KV_WIDTH = NSA_GROUPS * HEAD_DIM
IN_SPLITS = (BRANCH_WIDTH, BRANCH_WIDTH, BRANCH_WIDTH, FOX_HEADS, BRANCH_WIDTH,
             KV_WIDTH, KV_WIDTH, KV_WIDTH, KV_WIDTH, KV_WIDTH, KV_WIDTH,
             NSA_HEADS * N_NSA_BRANCH, N_BRANCH * D_MODEL)


def _rmsnorm_kernel(x_ref, g_ref, o_ref):
    x = x_ref[...]
    ms = jnp.mean(x * x, axis=-1, keepdims=True)
    o_ref[...] = x * lax.rsqrt(ms + RMS_EPS) * g_ref[...]


def rmsnorm_pallas(x2d, g, rows=512):
    t, d = x2d.shape
    return pl.pallas_call(
        _rmsnorm_kernel,
        grid=(t // rows,),
        in_specs=[pl.BlockSpec((rows, d), lambda i: (i, 0)),
                  pl.BlockSpec((1, d), lambda i: (0, 0))],
        out_specs=pl.BlockSpec((rows, d), lambda i: (i, 0)),
        out_shape=jax.ShapeDtypeStruct((t, d), jnp.float32),
    )(x2d, g.reshape(1, d))


PEER_SLOTS = PEER_HEADS * PEER_TOPK
PEER_TT = 64
PEER_CHUNK_SLOTS = 32
PEER_U_UNROLL = 4
PEER_V_UNROLL = 2
HALF_D = D_MODEL // 2
ROW_SUB = HALF_D // 128
VMEM_LIMIT_PEER = 48 * 1024 * 1024


def pack_table(tab):
    n = tab.shape[0]
    b = lax.bitcast_convert_type(tab.astype(jnp.bfloat16), jnp.uint16).astype(jnp.uint32)
    b = b.reshape(n, ROW_SUB, 2, 128)
    return (b[:, :, 0, :] | (b[:, :, 1, :] << 16)).reshape(n * ROW_SUB, 128)


def _expert_slab(tab_ref, row):
    return tab_ref[pl.ds(pl.multiple_of(row, ROW_SUB), ROW_SUB), :]


def _unpack(w):
    lo = lax.bitcast_convert_type(w << 16, jnp.float32)
    hi = lax.bitcast_convert_type(w & jnp.uint32(0xFFFF0000), jnp.float32)
    return lo, hi


def _peer_u_kernel(idx_a_ref, idx_b_ref, x_ref, tab_ref, out_ref):
    lane = lax.broadcasted_iota(jnp.int32, (2 * ROW_SUB, 128), 1)
    lower = lax.broadcasted_iota(jnp.int32, (2 * ROW_SUB, 128), 0) >= ROW_SUB
    pair_lane = jnp.where(lower, lane - PEER_SLOTS // 2, lane)

    def token(t):
        row = x_ref[pl.ds(t, 1), :]
        chunks = [row[:, c * 128:(c + 1) * 128] for c in range(2 * ROW_SUB)]
        xl = jnp.concatenate(chunks[0::2] * 2, axis=0)
        xh = jnp.concatenate(chunks[1::2] * 2, axis=0)
        acc = jnp.zeros((2 * ROW_SUB, 128), jnp.float32)
        base = t * (PEER_SLOTS // 2)
        for s in range(PEER_SLOTS // 2):
            w = jnp.concatenate([_expert_slab(tab_ref, idx_a_ref[base + s]),
                                 _expert_slab(tab_ref, idx_b_ref[base + s])], axis=0)
            lo, hi = _unpack(w)
            part = jnp.sum(lo * xl + hi * xh, axis=1, keepdims=True)
            acc = jnp.where(pair_lane == s, part, acc)
        out_ref[pl.ds(t, 1), :] = jnp.sum(acc, axis=0, keepdims=True)

    def trip(i, carry):
        for j in range(PEER_U_UNROLL):
            token(i * PEER_U_UNROLL + j)
        return carry

    lax.fori_loop(0, PEER_TT // PEER_U_UNROLL, trip, 0)


def _peer_v_kernel(idx_a_ref, idx_b_ref, coef_a_ref, coef_b_ref, tab_ref, out_ref):
    n_acc = 2
    zero = jnp.zeros((2 * ROW_SUB, 128), jnp.float32)
    lower = lax.broadcasted_iota(jnp.int32, (2 * ROW_SUB, 128), 0) >= ROW_SUB
    half_slots = PEER_SLOTS // 2
    pairs_per_chunk = PEER_CHUNK_SLOTS // 2

    def trip(i, carry):
        def chunk(c, accs):
            accs = list(accs)
            for u in range(PEER_V_UNROLL):
                base = (i * PEER_V_UNROLL + u) * half_slots + c * pairs_per_chunk
                for j in range(pairs_per_chunk):
                    w = jnp.concatenate([_expert_slab(tab_ref, idx_a_ref[base + j]),
                                         _expert_slab(tab_ref, idx_b_ref[base + j])], axis=0)
                    lo, hi = _unpack(w)
                    cf = jnp.where(lower, coef_b_ref[base + j], coef_a_ref[base + j])
                    k = 2 * n_acc * u + j % n_acc
                    accs[k] = accs[k] + cf * lo
                    accs[k + n_acc] = accs[k + n_acc] + cf * hi
            return tuple(accs)

        accs = lax.fori_loop(0, half_slots // pairs_per_chunk, chunk, (zero,) * (2 * n_acc * PEER_V_UNROLL))
        for u in range(PEER_V_UNROLL):
            a = accs[2 * n_acc * u:2 * n_acc * (u + 1)]
            even = a[0] + a[1]
            odd = a[2] + a[3]
            even = even[:ROW_SUB] + even[ROW_SUB:]
            odd = odd[:ROW_SUB] + odd[ROW_SUB:]
            out_ref[pl.ds(i * PEER_V_UNROLL + u, 1), :] = jnp.concatenate(
                [half[r:r + 1, :] for r in range(ROW_SUB) for half in (even, odd)], axis=1)
        return carry

    lax.fori_loop(0, PEER_TT // PEER_V_UNROLL, trip, 0)


def _smem_spec():
    return pl.BlockSpec((PEER_TT * PEER_SLOTS // 2,), lambda i: (i,), memory_space=pltpu.SMEM)


def _slot_halves(a):
    return a[:, :PEER_SLOTS // 2].reshape(-1), a[:, PEER_SLOTS // 2:].reshape(-1)


def _table_spec(rows):
    return pl.BlockSpec((rows, 128), lambda i: (0, 0), pipeline_mode=pl.Buffered(1))


def peer_u(idx, x, tab):
    t = x.shape[0]
    return pl.pallas_call(
        _peer_u_kernel,
        grid=(t // PEER_TT,),
        in_specs=[_smem_spec(), _smem_spec(), pl.BlockSpec((PEER_TT, D_MODEL), lambda i: (i, 0)),
                  _table_spec(tab.shape[0])],
        out_specs=pl.BlockSpec((PEER_TT, PEER_SLOTS), lambda i: (i, 0)),
        out_shape=jax.ShapeDtypeStruct((t, PEER_SLOTS), jnp.float32),
        compiler_params=pltpu.CompilerParams(vmem_limit_bytes=VMEM_LIMIT_PEER),
        name="peer_u",
    )(*_slot_halves(idx), x, tab)


def peer_v(idx, coef, tab):
    t = idx.shape[0]
    return pl.pallas_call(
        _peer_v_kernel,
        grid=(t // PEER_TT,),
        in_specs=[_smem_spec()] * 4 + [_table_spec(tab.shape[0])],
        out_specs=pl.BlockSpec((PEER_TT, D_MODEL), lambda i: (i, 0)),
        out_shape=jax.ShapeDtypeStruct((t, D_MODEL), jnp.float32),
        compiler_params=pltpu.CompilerParams(vmem_limit_bytes=VMEM_LIMIT_PEER),
        name="peer_v",
    )(*_slot_halves(idx), *_slot_halves(coef), tab)


PK_TT = 256
PK_A_BLOCKS = ((0, 16), (1, 8), (2, 5), (3, 4))
PK_B_BLOCKS = ((0, 4, 16), (1, 4, 8), (2, 4, 5))
NEG_HUGE = -3.0e38
PK_HEADS_PER_TRIP = 2


def _extract_top(problems, flat, n):
    vals = [[] for _ in problems]
    picked = [[] for _ in problems]
    cur = [v for v, _ in problems]
    for _ in range(n):
        for i, (_, payload) in enumerate(problems):
            m = jnp.max(cur[i], axis=0, keepdims=True)
            pos = jnp.min(jnp.where(cur[i] == m, flat, 1e9), axis=0, keepdims=True)
            hit = flat == pos
            picked[i].append(pos if payload is None else jnp.sum(jnp.where(hit, payload, 0.0), axis=0, keepdims=True))
            cur[i] = jnp.where(hit, NEG_HUGE, cur[i])
            vals[i].append(m)
    return [jnp.concatenate(v, axis=0) for v in vals], [jnp.concatenate(p, axis=0) for p in picked]


def _candidate_grid(first, second, combine):
    up8 = lambda n: -(-n // 8) * 8
    return jnp.concatenate([combine(first[a:a + 1], second[0:up8(nb)]) for a, nb in PK_A_BLOCKS]
                           + [combine(first[0:up8(a1)], second[b:b + 1]) for b, _, a1 in PK_B_BLOCKS], axis=0)


def _candidate_valid_and_flat():
    up8 = lambda n: -(-n // 8) * 8
    valid, flat = [], []
    for a, nb in PK_A_BLOCKS:
        r = lax.broadcasted_iota(jnp.int32, (up8(nb), PK_TT), 0)
        valid.append(r < nb)
        flat.append(a * PEER_TOPK + r)
    for b, a0, a1 in PK_B_BLOCKS:
        r = lax.broadcasted_iota(jnp.int32, (up8(a1), PK_TT), 0)
        valid.append((r >= a0) & (r < a1))
        flat.append(r * PEER_TOPK + b)
    valid = jnp.concatenate(valid, axis=0)
    return valid, jnp.where(valid, jnp.concatenate(flat, axis=0).astype(F32), 1e9)


def _peer_topk_kernel(q_ref, ka_ref, eid_ref, w_ref):
    key_row = lax.broadcasted_iota(jnp.int32, (N_KEYS, PK_TT), 0).astype(F32)
    valid, flat = _candidate_valid_and_flat()

    def heads(i, carry):
        scores = []
        for j in range(2 * PK_HEADS_PER_TRIP):
            q = q_ref[2 * PK_HEADS_PER_TRIP * i + j]
            q_hi = _trunc_bf16(q)
            qa = jnp.concatenate([q_hi.astype(BF16), (q - q_hi).astype(BF16), q_hi.astype(BF16)], axis=1)
            scores.append(lax.dot_general(ka_ref[2 * PK_HEADS_PER_TRIP * i + j], qa, (((1,), (1,)), ((), ())),
                                          preferred_element_type=F32))
        sub_s, sub_i = _extract_top([(s, None) for s in scores], key_row, PEER_TOPK)
        cands = []
        for j in range(PK_HEADS_PER_TRIP):
            s1, s2, i1, i2 = sub_s[2 * j], sub_s[2 * j + 1], sub_i[2 * j], sub_i[2 * j + 1]
            cands.append((jnp.where(valid, _candidate_grid(s1, s2, lambda x, y: x + y), NEG_HUGE),
                          _candidate_grid(i1, i2, lambda x, y: x * N_KEYS + y)))
        top_s, top_e = _extract_top(cands, flat, PEER_TOPK)
        for j in range(PK_HEADS_PER_TRIP):
            e = jnp.exp(top_s[j] - top_s[j][0:1])
            w_ref[PK_HEADS_PER_TRIP * i + j] = e / jnp.sum(e, axis=0, keepdims=True)
            eid_ref[PK_HEADS_PER_TRIP * i + j] = top_e[j].astype(jnp.int32) * ROW_SUB
        return carry

    lax.fori_loop(0, PEER_HEADS // PK_HEADS_PER_TRIP, heads, 0)


def peer_topk(qt, subkeys):
    hp, t, d = qt.shape
    k_hi, k_lo = _split2(subkeys.reshape(hp, N_KEYS, d))
    ka = jnp.concatenate([k_hi, k_hi, k_lo], axis=2)
    eid, w = pl.pallas_call(
        _peer_topk_kernel,
        grid=(t // PK_TT,),
        in_specs=[pl.BlockSpec((hp, PK_TT, d), lambda i: (0, i, 0)),
                  pl.BlockSpec((hp, N_KEYS, 3 * d), lambda i: (0, 0, 0))],
        out_specs=[pl.BlockSpec((PEER_HEADS, PEER_TOPK, PK_TT), lambda i: (0, 0, i)),
                   pl.BlockSpec((PEER_HEADS, PEER_TOPK, PK_TT), lambda i: (0, 0, i))],
        out_shape=[jax.ShapeDtypeStruct((PEER_HEADS, PEER_TOPK, t), jnp.int32),
                   jax.ShapeDtypeStruct((PEER_HEADS, PEER_TOPK, t), F32)],
        name="peer_topk",
    )(qt, ka)
    to_slots = lambda a: a.transpose(2, 0, 1).reshape(t, PEER_SLOTS)
    return to_slots(eid), to_slots(w)


def rmsnorm(x, g):
    xf = x.astype(jnp.float32)
    y = xf * lax.rsqrt(jnp.mean(xf * xf, axis=-1, keepdims=True) + RMS_EPS)
    return (y * g.astype(jnp.float32)).astype(x.dtype)


def split_columns(t, sizes):
    offs = np.cumsum((0,) + tuple(sizes))
    return [t[..., int(a):int(b)] for a, b in zip(offs[:-1], offs[1:])]


def rope_tables(pos):
    inv = jnp.power(ROPE_THETA, -jnp.arange(0, ROPE_DIM, 2, dtype=jnp.float32) / ROPE_DIM)
    ang = pos[:, None] * inv[None, :]
    return jnp.cos(ang), jnp.sin(ang)


def apply_partial_rope(x, cos, sin):
    half = ROPE_DIM // 2
    xr = x[..., :ROPE_DIM].astype(jnp.float32)
    x1, x2 = xr[..., :half], xr[..., half:]
    c, s = cos[:, None, :], sin[:, None, :]
    rot = jnp.concatenate([x1 * c - x2 * s, x1 * s + x2 * c], axis=-1)
    return jnp.concatenate([rot.astype(x.dtype), x[..., ROPE_DIM:]], axis=-1)


BF16 = jnp.bfloat16
F32 = jnp.float32


def _flash_step(k, vt, qt, mask, carry):
    m, l, acc = carry
    s = jnp.dot(k, qt, preferred_element_type=F32)
    s_vis = s if mask is None else jnp.where(mask, s, NEG_INF)
    m_new = jnp.maximum(m, jnp.max(s_vis, axis=0, keepdims=True))
    alpha = jnp.exp(m - m_new)
    p = jnp.exp(s - m_new)
    if mask is not None:
        p = jnp.where(mask, p, 0.0)
    l = alpha * l + jnp.sum(p, axis=0, keepdims=True)
    acc = alpha * acc + jnp.dot(vt, p.astype(BF16), preferred_element_type=F32)
    return m_new, l, acc


def _flash_steps(ks, vts, qts, mask, carries):
    ss = [jnp.dot(k, qt, preferred_element_type=F32) for k, qt in zip(ks, qts)]
    stats, ps = [], []
    for s, (m, l, _) in zip(ss, carries):
        s_vis = s if mask is None else jnp.where(mask, s, NEG_INF)
        m_new = jnp.maximum(m, jnp.max(s_vis, axis=0, keepdims=True))
        alpha = jnp.exp(m - m_new)
        p = jnp.exp(s - m_new)
        if mask is not None:
            p = jnp.where(mask, p, 0.0)
        stats.append((m_new, alpha, alpha * l + jnp.sum(p, axis=0, keepdims=True)))
        ps.append(p.astype(BF16))
    return tuple((m_new, l, alpha * acc + jnp.dot(vt, p, preferred_element_type=F32))
                 for (m_new, alpha, l), p, vt, (_, _, acc) in zip(stats, ps, vts, carries))


def _flash_init(n):
    return (jnp.full((1, n), NEG_INF, F32), jnp.zeros((1, n), F32), jnp.zeros((HEAD_DIM, n), F32))


FOX_TQ = 256
FOX_KV = 512
FOX_DK = 256


FOX_HP = 4


def _fox_kernel(qt_ref, k_ref, vt_ref, o_ref):
    qi = pl.program_id(1)
    q0 = qi * FOX_TQ
    qts = [qt_ref[h] for h in range(FOX_HP)]
    qpos = q0 + lax.broadcasted_iota(jnp.int32, (1, FOX_TQ), 1)
    krow = lax.broadcasted_iota(jnp.int32, (FOX_KV, FOX_TQ), 0)

    def full_step(kt, carry):
        return _flash_steps([k_ref[h, kt] for h in range(FOX_HP)], [vt_ref[h, kt] for h in range(FOX_HP)],
                            qts, None, carry)

    def diag_step(kt, carry):
        mask = (kt * FOX_KV + krow) <= qpos
        return _flash_steps([k_ref[h, kt] for h in range(FOX_HP)], [vt_ref[h, kt] for h in range(FOX_HP)],
                            qts, mask, carry)

    n_full = (q0 + 1) // FOX_KV
    n_all = (q0 + FOX_TQ + FOX_KV - 1) // FOX_KV
    carry = lax.fori_loop(0, n_full, full_step, tuple(_flash_init(FOX_TQ) for _ in range(FOX_HP)))
    carry = lax.fori_loop(n_full, n_all, diag_step, carry)
    for h in range(FOX_HP):
        _, l, acc = carry[h]
        o_ref[h] = acc / l


def _trunc_bf16(x):
    bits = lax.bitcast_convert_type(x, jnp.uint32) & jnp.uint32(0xFFFF0000)
    return lax.bitcast_convert_type(bits, F32)


def _split3(c):
    c1 = _trunc_bf16(c)
    r = c - c1
    c2 = _trunc_bf16(r)
    return c1.astype(BF16), c2.astype(BF16), (r - c2).astype(BF16)


def _split2(x):
    hi = _trunc_bf16(x)
    return hi.astype(BF16), (x - hi).astype(BF16)


def _aug_q(q):
    hi, lo = _split2(q)
    return jnp.concatenate([hi, hi, lo], axis=-1)


def _aug_k(k):
    hi, lo = _split2(k)
    return jnp.concatenate([hi, lo, hi], axis=-1)


def fox_attention(q, k, v, log_f):
    B_, S_, H, Dh = q.shape
    KV_TILE = FOX_KV
    n_qt, n_kt = S_ // FOX_TQ, S_ // KV_TILE
    c = jnp.cumsum(log_f, axis=1)
    c1, c2, c3 = _split3(c)
    one = jnp.ones_like(c1)
    pad = jnp.zeros((B_, S_, H, FOX_DK - 3 * Dh - 6), BF16)
    col = lambda t: t[..., None]
    qa = jnp.concatenate([_aug_q(q * (Dh ** -0.5)), col(one), col(one), col(one),
                          col(c1), col(c2), col(c3), pad], axis=-1)
    ka = jnp.concatenate([_aug_k(k), col(-c1), col(-c2), col(-c3),
                          col(one), col(one), col(one), pad], axis=-1)
    HP, HG = FOX_HP, H // FOX_HP
    qt = qa.reshape(B_, n_qt, FOX_TQ, HG, HP, FOX_DK).transpose(0, 3, 1, 4, 5, 2).reshape(B_ * HG * n_qt, HP, FOX_DK, FOX_TQ)
    kk = ka.reshape(B_, n_kt, KV_TILE, HG, HP, FOX_DK).transpose(0, 3, 4, 1, 2, 5).reshape(B_ * HG, HP, n_kt, KV_TILE, FOX_DK)
    vt = v.astype(BF16).reshape(B_, n_kt, KV_TILE, HG, HP, Dh).transpose(0, 3, 4, 1, 5, 2).reshape(B_ * HG, HP, n_kt, Dh, KV_TILE)
    out = pl.pallas_call(
        _fox_kernel,
        grid=(B_ * HG, n_qt),
        in_specs=[pl.BlockSpec((None, HP, FOX_DK, FOX_TQ), lambda b, i: (b * n_qt + i, 0, 0, 0)),
                  pl.BlockSpec((None, HP, n_kt, KV_TILE, FOX_DK), lambda b, i: (b, 0, 0, 0, 0)),
                  pl.BlockSpec((None, HP, n_kt, Dh, KV_TILE), lambda b, i: (b, 0, 0, 0, 0))],
        out_specs=pl.BlockSpec((None, HP, Dh, FOX_TQ), lambda b, i: (b * n_qt + i, 0, 0, 0)),
        out_shape=jax.ShapeDtypeStruct((B_ * HG * n_qt, HP, Dh, FOX_TQ), F32),
        name="fox_attention",
    )(qt, kk, vt)
    return out.reshape(B_, HG, n_qt, HP, Dh, FOX_TQ).transpose(0, 2, 5, 1, 3, 4).reshape(B_, S_, H * Dh)


def compress_kv(k, v, cmp_pos, w1, w2):
    B_, S_ = k.shape[0], k.shape[1]
    n_cmp = (S_ - CMP_LEN) // CMP_STRIDE + 1
    starts = np.arange(n_cmp) * CMP_STRIDE
    idx = starts[:, None] + np.arange(CMP_LEN)[None, :]

    def phi(t, j):
        blocks = t[:, idx] + cmp_pos[j][None, None, :, None, :]
        flat = jnp.swapaxes(blocks, 2, 3).reshape(B_, n_cmp, NSA_GROUPS, CMP_LEN * HEAD_DIM)
        return jax.nn.gelu(flat @ w1[j], approximate=False) @ w2[j]

    return phi(k, 0), phi(v, 1), jnp.asarray(starts + CMP_LEN - 1, dtype=jnp.int32)


NSA_TQ = 128
NSA_N = NSA_HPG * NSA_TQ
N_CMP_PAD = 256
N_SLC = 64
NSA_DK = 3 * HEAD_DIM
NSA_KV = 512
WIN_TILES = WINDOW // NSA_TQ + 1
WIN_KEYS = WIN_TILES * NSA_TQ


def _nsa_kernel(qt_ref, kc_ref, vct_ref, ks_ref, vst_ref, kw_ref, vwt_ref, gate_ref, ovl_ref, o_ref, sel_ref):
    qi = pl.program_id(1)
    q0 = qi * NSA_TQ
    qt = qt_ref[...]
    lane = lax.broadcasted_iota(jnp.int32, (1, NSA_N), 1)
    qpos = q0 + (lane & (NSA_TQ - 1))

    hn = NSA_N // 2
    qts = (qt[:, :hn], qt[:, hn:])
    qpos_h = qpos[:, :hn]

    w0 = jnp.maximum(qi - WINDOW // NSA_TQ, 0)
    k_w = jnp.concatenate([kw_ref[w0 + i] for i in range(WIN_TILES)], axis=0)
    vt_w = jnp.concatenate([vwt_ref[w0 + i] for i in range(WIN_TILES)], axis=1)
    s = jnp.dot(kc_ref[...], qt, preferred_element_type=F32)
    s_w = [jnp.dot(k_w, qts[h], preferred_element_type=F32) for h in range(2)]

    cmp_end = lax.broadcasted_iota(jnp.int32, (N_CMP_PAD, NSA_N), 0) * CMP_STRIDE + (CMP_LEN - 1)
    mc = cmp_end <= qpos
    m = jnp.max(jnp.where(mc, s, NEG_INF), axis=0, keepdims=True)
    p = jnp.where(mc, jnp.exp(s - m), 0.0)
    l = jnp.sum(p, axis=0, keepdims=True)
    pc = p * jnp.where(l > 0.0, 1.0 / l, 0.0)
    o_cmp = jnp.dot(vct_ref[...], pc.astype(BF16), preferred_element_type=F32)
    pcs = (pc[:, 0:NSA_TQ] + pc[:, NSA_TQ:2 * NSA_TQ]) + (pc[:, 2 * NSA_TQ:3 * NSA_TQ] + pc[:, 3 * NSA_TQ:4 * NSA_TQ])
    pcs_hi = pcs.astype(BF16)
    pcs_lo = (pcs - pcs_hi.astype(F32)).astype(BF16)
    ovl = ovl_ref[...]
    imp = jnp.dot(ovl, pcs_hi, preferred_element_type=F32) + jnp.dot(ovl, pcs_lo, preferred_element_type=F32)

    rel = qpos_h - (w0 * NSA_TQ + lax.broadcasted_iota(jnp.int32, (WIN_KEYS, hn), 0))
    in_win = (rel >= 0) & (rel < WINDOW)
    outs_w = []
    for h in range(2):
        m = jnp.max(jnp.where(in_win, s_w[h], NEG_INF), axis=0, keepdims=True)
        p = jnp.where(in_win, jnp.exp(s_w[h] - m), 0.0)
        l = jnp.sum(p, axis=0, keepdims=True)
        outs_w.append(jnp.dot(vt_w, p.astype(BF16), preferred_element_type=F32) / l)
    acc_w = jnp.concatenate(outs_w, axis=1)

    blk = lax.broadcasted_iota(jnp.int32, (N_SLC, NSA_TQ), 0)
    q_blk = (q0 + lax.broadcasted_iota(jnp.int32, (N_SLC, NSA_TQ), 1)) >> 6
    forced = (blk == 0) | (blk == q_blk) | (blk == q_blk - 1)
    score = jnp.where(forced, FORCED_SCORE, jnp.where(blk <= q_blk, imp, -1.0))
    rank = jnp.zeros((N_SLC, NSA_TQ), jnp.int32)
    for i in range(N_SLC):
        row = score[i:i + 1, :]
        beats = (row > score) | ((row == score) & (blk > i))
        rank = rank + beats.astype(jnp.int32)
    sel = (rank < SLC_TOP).astype(F32)
    sel_ref[...] = jnp.concatenate([sel] * (NSA_HPG // 2), axis=1)

    krow = lax.broadcasted_iota(jnp.int32, (NSA_KV, hn), 0)
    blocks_per_tile = NSA_KV // SLC_LEN

    def slc_chains(kt, carry):
        rows = [jnp.broadcast_to(sel_ref[pl.ds(blocks_per_tile * kt + j, 1), :], (SLC_LEN, hn))
                for j in range(blocks_per_tile)]
        mask = (jnp.concatenate(rows, axis=0) > 0.0) & ((kt * NSA_KV + krow) <= qpos_h)
        k, vt = ks_ref[kt], vst_ref[kt]
        return _flash_steps([k, k], [vt, vt], qts, mask, carry)

    slc = lax.fori_loop(0, (q0 + NSA_TQ - 1) // NSA_KV + 1, slc_chains, (_flash_init(hn), _flash_init(hn)))
    acc_s = jnp.concatenate([slc[0][2] / slc[0][1], slc[1][2] / slc[1][1]], axis=1)

    g = jax.nn.sigmoid(gate_ref[...])
    o_ref[...] = g[0:1] * o_cmp + g[1:2] * acc_s + g[2:3] * acc_w


def nsa_attention(q, k_cmp, v_cmp, k_slc, v_slc, k_win, v_win, gate_logit):
    B_, S_, H, Dh = q.shape
    G = NSA_GROUPS
    n_qt = S_ // NSA_TQ
    n_cmp = k_cmp.shape[1]
    cs = np.arange(N_CMP_PAD)[None, :] * CMP_STRIDE
    ss = np.arange(N_SLC)[:, None] * SLC_LEN
    ov = np.clip(np.minimum(cs + CMP_LEN, ss + SLC_LEN) - np.maximum(cs, ss), 0, None) / CMP_LEN
    ov[:, n_cmp:] = 0.0
    ovl = jnp.asarray(ov, dtype=BF16)
    qs = _aug_q(q * (Dh ** -0.5)).reshape(B_, n_qt, NSA_TQ, G, NSA_HPG, NSA_DK)
    qt = qs.transpose(0, 3, 1, 5, 4, 2).reshape(B_ * G * n_qt, NSA_DK, NSA_N)
    gt = gate_logit.astype(F32).reshape(B_, n_qt, NSA_TQ, G, NSA_HPG, N_NSA_BRANCH)
    gt = gt.transpose(0, 3, 1, 5, 4, 2).reshape(B_ * G * n_qt, N_NSA_BRANCH, NSA_N)
    keys = lambda t, kv: _aug_k(t).reshape(B_, S_ // kv, kv, G, NSA_DK).transpose(0, 3, 1, 2, 4).reshape(B_ * G, S_ // kv, kv, NSA_DK)
    vals = lambda t, kv: t.astype(BF16).reshape(B_, S_ // kv, kv, G, Dh).transpose(0, 3, 1, 4, 2).reshape(B_ * G, S_ // kv, Dh, kv)
    padc = ((0, 0), (0, N_CMP_PAD - n_cmp), (0, 0), (0, 0))
    kc = _aug_k(jnp.pad(k_cmp, padc)).transpose(0, 2, 1, 3).reshape(B_ * G, N_CMP_PAD, NSA_DK)
    vct = jnp.pad(v_cmp, padc).astype(BF16).transpose(0, 2, 3, 1).reshape(B_ * G, Dh, N_CMP_PAD)
    per_bg = lambda *blk: pl.BlockSpec((None,) + blk, lambda b, i: (b,) + (0,) * len(blk))
    per_tile = lambda *blk: pl.BlockSpec((None,) + blk, lambda b, i: (b * n_qt + i,) + (0,) * len(blk))
    out = pl.pallas_call(
        _nsa_kernel,
        grid=(B_ * G, n_qt),
        in_specs=[per_tile(NSA_DK, NSA_N),
                  per_bg(N_CMP_PAD, NSA_DK), per_bg(Dh, N_CMP_PAD),
                  per_bg(S_ // NSA_KV, NSA_KV, NSA_DK), per_bg(S_ // NSA_KV, Dh, NSA_KV),
                  per_bg(n_qt, NSA_TQ, NSA_DK), per_bg(n_qt, Dh, NSA_TQ),
                  per_tile(N_NSA_BRANCH, NSA_N),
                  pl.BlockSpec((N_SLC, N_CMP_PAD), lambda b, i: (0, 0))],
        out_specs=per_tile(Dh, NSA_N),
        out_shape=jax.ShapeDtypeStruct((B_ * G * n_qt, Dh, NSA_N), F32),
        scratch_shapes=[pltpu.VMEM((N_SLC, NSA_N // 2), F32)],
        name="nsa_attention",
    )(qt, kc, vct, keys(k_slc, NSA_KV), vals(v_slc, NSA_KV), keys(k_win, NSA_TQ), vals(v_win, NSA_TQ), gt, ovl)
    return out.reshape(B_, G, n_qt, Dh, NSA_HPG, NSA_TQ).transpose(0, 2, 5, 1, 4, 3).reshape(B_, S_, H * Dh)


DENSE_TM = 256
SEC_QKV = 4 * BRANCH_WIDTH
SEC_KV = 6 * KV_WIDTH
SEC_MERGE = N_BRANCH * D_MODEL
SEC_SMALL = 256
IN_SECTIONS = (SEC_QKV, SEC_KV, SEC_MERGE, SEC_SMALL)


def _rms(x, g):
    return x * lax.rsqrt(jnp.mean(x * x, axis=-1, keepdims=True) + RMS_EPS) * g


def _row_spec(width):
    return pl.BlockSpec((DENSE_TM, width), lambda i: (i, 0))


def _whole_spec(shape):
    return pl.BlockSpec(shape, lambda i: (0,) * len(shape))


def _in_proj_kernel(x_ref, g_ref, w_ref, *o_refs):
    xn = _rms(x_ref[...], g_ref[...]).astype(BF16)
    off = 0
    for o_ref, width in zip(o_refs, IN_SECTIONS):
        o_ref[...] = jnp.dot(xn, w_ref[:, off:off + width], preferred_element_type=F32)
        off += width


def in_projection(x, g, w_in):
    t, d = x.shape
    fq, fk, fv, fl, nq, kc, vc, ksl, vsl, kwn, vwn, gl, ml = split_columns(w_in, IN_SPLITS)
    pad = lambda w: jnp.pad(w, ((0, 0), (0, 128 - w.shape[1])))
    w = jnp.concatenate([fq, fk, fv, nq, kc, vc, ksl, vsl, kwn, vwn, ml, pad(fl), pad(gl)], axis=1).astype(BF16)
    return pl.pallas_call(
        _in_proj_kernel,
        grid=(t // DENSE_TM,),
        in_specs=[_row_spec(d), _whole_spec((1, d)), _whole_spec(w.shape)],
        out_specs=[_row_spec(s) for s in IN_SECTIONS],
        out_shape=[jax.ShapeDtypeStruct((t, s), F32) for s in IN_SECTIONS],
        name="in_projection",
    )(x, g.reshape(1, d), w)


def _merge_kernel(x_ref, yf_ref, yn_ref, ml_ref, wb_ref, wo_ref, o_ref):
    g = jax.nn.sigmoid(ml_ref[...])
    up_f = jnp.dot(yf_ref[...].astype(BF16), wb_ref[0], preferred_element_type=F32)
    up_n = jnp.dot(yn_ref[...].astype(BF16), wb_ref[1], preferred_element_type=F32)
    merged = g[:, :D_MODEL] * up_f + g[:, D_MODEL:] * up_n
    o_ref[...] = x_ref[...] + jnp.dot(merged.astype(BF16), wo_ref[...], preferred_element_type=F32)


def merge_out(x, y_fox, y_nsa, merge_logit, w_branch, w_out):
    t, d = x.shape
    return pl.pallas_call(
        _merge_kernel,
        grid=(t // DENSE_TM,),
        in_specs=[_row_spec(d), _row_spec(BRANCH_WIDTH), _row_spec(BRANCH_WIDTH), _row_spec(SEC_MERGE),
                  _whole_spec(w_branch.shape), _whole_spec(w_out.shape)],
        out_specs=_row_spec(d),
        out_shape=jax.ShapeDtypeStruct((t, d), F32),
        name="merge_out",
    )(x, y_fox, y_nsa, merge_logit, w_branch.astype(BF16), w_out.astype(BF16))


def _peer_q_kernel(x_ref, g_ref, w_ref, hn_ref, q_ref):
    hn = _rms(x_ref[...], g_ref[...])
    hn_ref[...] = hn
    q = jnp.dot(hn.astype(BF16), w_ref[...], preferred_element_type=F32)
    d = PEER_QDIM // 2
    for j in range(2 * PEER_HEADS):
        q_ref[j] = q[:, j * d:(j + 1) * d]


def peer_query(x, g, wq):
    t, d = x.shape
    hp, dq = 2 * PEER_HEADS, PEER_QDIM // 2
    return pl.pallas_call(
        _peer_q_kernel,
        grid=(t // DENSE_TM,),
        in_specs=[_row_spec(d), _whole_spec((1, d)), _whole_spec(wq.shape)],
        out_specs=[_row_spec(d), pl.BlockSpec((hp, DENSE_TM, dq), lambda i: (0, i, 0))],
        out_shape=[jax.ShapeDtypeStruct((t, d), F32), jax.ShapeDtypeStruct((hp, t, dq), F32)],
        name="peer_query",
    )(x, g.reshape(1, d), wq.astype(BF16))


def _final_kernel(h_ref, p_ref, g_ref, o_ref):
    o_ref[...] = _rms(h_ref[...] + p_ref[...], g_ref[...])


def final_norm(h, p, g):
    t, d = h.shape
    return pl.pallas_call(
        _final_kernel,
        grid=(t // DENSE_TM,),
        in_specs=[_row_spec(d), _row_spec(d), _whole_spec((1, d))],
        out_specs=_row_spec(d),
        out_shape=jax.ShapeDtypeStruct((t, d), F32),
        name="final_norm",
    )(h, p, g.reshape(1, d))


N_CMP_ROWS = 256


def _compress_kernel(r_ref, pos_ref, w1_ref, w2_ref, o_ref):
    r = r_ref[...]
    nxt = jnp.concatenate([r[1:], jnp.zeros((1, r.shape[1]), F32)], axis=0)
    half = r.shape[1]
    hid = (jnp.dot((r + pos_ref[0:1, :]).astype(BF16), w1_ref[:half], preferred_element_type=F32)
           + jnp.dot((nxt + pos_ref[1:2, :]).astype(BF16), w1_ref[half:], preferred_element_type=F32))
    hid = 0.5 * hid * (1.0 + lax.erf(hid * (2.0 ** -0.5)))
    o_ref[...] = jnp.dot(hid.astype(BF16), w2_ref[...], preferred_element_type=F32)


def compress_kv_pallas(kv, cmp_pos, w1, w2):
    _, B_, S_, G, Dh = kv.shape
    n_cmp = (S_ - CMP_LEN) // CMP_STRIDE + 1
    width = CMP_STRIDE * Dh
    rows = kv.transpose(0, 1, 3, 2, 4).reshape(2, B_ * G, S_ // CMP_STRIDE, width)
    out = pl.pallas_call(
        _compress_kernel,
        grid=(2, B_ * G),
        in_specs=[pl.BlockSpec((None, None, N_CMP_ROWS, width), lambda j, b: (j, b, 0, 0)),
                  pl.BlockSpec((None, 2, width), lambda j, b: (j, 0, 0)),
                  pl.BlockSpec((None, CMP_LEN * Dh, CMP_HIDDEN), lambda j, b: (j, 0, 0)),
                  pl.BlockSpec((None, CMP_HIDDEN, Dh), lambda j, b: (j, 0, 0))],
        out_specs=pl.BlockSpec((None, None, N_CMP_ROWS, Dh), lambda j, b: (j, b, 0, 0)),
        out_shape=jax.ShapeDtypeStruct((2, B_ * G, N_CMP_ROWS, Dh), F32),
        name="compress_kv",
    )(rows, cmp_pos.reshape(2, 2, width), w1.astype(BF16), w2.astype(BF16))
    return out[:, :, :n_cmp].reshape(2, B_, G, n_cmp, Dh).transpose(0, 1, 3, 2, 4)


def hybrid_mixer(x, norm_g, w_in, fox_f_bias, cmp_pos, cmp_w1, cmp_w2, w_branch, w_out):
    B_, S_, D = x.shape
    xf = x.reshape(B_ * S_, D)
    qkv, kvs, merge_logit, small = in_projection(xf, norm_g, w_in)
    heads = lambda t, n: t.reshape(B_, S_, n, HEAD_DIM)
    fq, fk, fv, nq = (qkv[:, i * BRANCH_WIDTH:(i + 1) * BRANCH_WIDTH] for i in range(4))
    kc, vc, ksl, vsl, kwn, vwn = (heads(kvs[:, i * KV_WIDTH:(i + 1) * KV_WIDTH], NSA_GROUPS) for i in range(6))
    f_logit = small[:, :FOX_HEADS].reshape(B_, S_, FOX_HEADS)
    gate_logit = small[:, 128:128 + NSA_HEADS * N_NSA_BRANCH].reshape(B_, S_, NSA_HEADS, N_NSA_BRANCH)
    log_f = jax.nn.log_sigmoid(f_logit + fox_f_bias.astype(F32))
    y_fox = fox_attention(heads(fq, FOX_HEADS), heads(fk, FOX_HEADS), heads(fv, FOX_HEADS), log_f)
    cos, sin = rope_tables(jnp.arange(S_, dtype=F32))
    q_nsa = apply_partial_rope(heads(nq, NSA_HEADS), cos, sin)
    k_slc = apply_partial_rope(ksl, cos, sin)
    k_win = apply_partial_rope(kwn, cos, sin)
    cmp = compress_kv_pallas(jnp.stack([kc, vc]), cmp_pos, cmp_w1, cmp_w2)
    n_cmp = cmp.shape[2]
    cmp_end = jnp.arange(n_cmp, dtype=F32) * CMP_STRIDE + (CMP_LEN - 1)
    k_cmp = apply_partial_rope(cmp[0], *rope_tables(cmp_end))
    y_nsa = nsa_attention(q_nsa, k_cmp, cmp[1], k_slc, vsl, k_win, vwn, gate_logit)
    return merge_out(xf, y_fox.reshape(B_ * S_, -1), y_nsa.reshape(B_ * S_, -1), merge_logit, w_branch, w_out)


def peer_ffn(h, norm_g, wq, subkeys, u, v):
    hn, q = peer_query(h, norm_g, wq)
    idx2, w = peer_topk(q, subkeys)
    act = jax.nn.gelu(peer_u(idx2, hn, pack_table(u)), approximate=False)
    return peer_v(idx2, w * act, pack_table(v))


def kernel(x, norm_mix, w_in, fox_f_bias, nsa_cmp_pos, nsa_cmp_w1, nsa_cmp_w2, w_branch, w_out,
           norm_ffn, peer_wq, peer_subkeys, peer_u, peer_v, norm_final):
    B_, S_, D = x.shape
    assert norm_mix.shape[0] == 1, "single-layer trunk"
    h = hybrid_mixer(x, norm_mix[0], w_in[0], fox_f_bias[0], nsa_cmp_pos[0], nsa_cmp_w1[0], nsa_cmp_w2[0],
                     w_branch[0], w_out[0])
    p = peer_ffn(h, norm_ffn[0], peer_wq[0], peer_subkeys[0], peer_u[0], peer_v[0])
    return final_norm(h, p, norm_final).reshape(B_, S_, D)
```

```python
import functools

import jax
import jax.numpy as jnp
import numpy as np
from jax import lax
from jax.experimental import pallas as pl
from jax.experimental.pallas import tpu as pltpu

D_MODEL = 1024
HEAD_DIM = 64
FOX_HEADS = 8
NSA_HEADS = 8
NSA_GROUPS = 2
NSA_HPG = NSA_HEADS // NSA_GROUPS
BRANCH_WIDTH = 512
N_BRANCH = 2
N_NSA_BRANCH = 3
ROPE_DIM = HEAD_DIM // 4
ROPE_THETA = 500000.0
Q_BLOCK = 128
CMP_LEN = 32
CMP_STRIDE = 16
CMP_HIDDEN = 2 * HEAD_DIM
SLC_LEN = 64
SLC_TOP = 16
WINDOW = 512
FORCED_SCORE = 1e9
NEG_INF = -1e30
PEER_HEADS = 8
N_KEYS = 128
N_EXPERTS = N_KEYS * N_KEYS
PEER_QDIM = 256
PEER_TOPK = 16
PEER_CHUNK = 128
RMS_EPS = 1e-6
KV_WIDTH = NSA_GROUPS * HEAD_DIM
IN_SPLITS = (BRANCH_WIDTH, BRANCH_WIDTH, BRANCH_WIDTH, FOX_HEADS, BRANCH_WIDTH,
             KV_WIDTH, KV_WIDTH, KV_WIDTH, KV_WIDTH, KV_WIDTH, KV_WIDTH,
             NSA_HEADS * N_NSA_BRANCH, N_BRANCH * D_MODEL)


def _rmsnorm_kernel(x_ref, g_ref, o_ref):
    x = x_ref[...]
    ms = jnp.mean(x * x, axis=-1, keepdims=True)
    o_ref[...] = x * lax.rsqrt(ms + RMS_EPS) * g_ref[...]


def rmsnorm_pallas(x2d, g, rows=512):
    t, d = x2d.shape
    return pl.pallas_call(
        _rmsnorm_kernel,
        grid=(t // rows,),
        in_specs=[pl.BlockSpec((rows, d), lambda i: (i, 0)),
                  pl.BlockSpec((1, d), lambda i: (0, 0))],
        out_specs=pl.BlockSpec((rows, d), lambda i: (i, 0)),
        out_shape=jax.ShapeDtypeStruct((t, d), jnp.float32),
    )(x2d, g.reshape(1, d))


PEER_SLOTS = PEER_HEADS * PEER_TOPK
PEER_TT = 64
PEER_CHUNK_SLOTS = 32
PEER_U_UNROLL = 4
PEER_V_UNROLL = 2
HALF_D = D_MODEL // 2
ROW_SUB = HALF_D // 128
VMEM_LIMIT_PEER = 48 * 1024 * 1024


def pack_table(tab):
    n, d = tab.shape
    return pl.pallas_call(
        _pack_kernel,
        grid=(n // PACK_TM,),
        in_specs=[pl.BlockSpec((PACK_TM, d), lambda i: (i, 0))],
        out_specs=pl.BlockSpec((PACK_TM * ROW_SUB, 128), lambda i: (i, 0)),
        out_shape=jax.ShapeDtypeStruct((n * ROW_SUB, 128), jnp.uint32),
        name="pack_table",
    )(tab)


PACK_TM = 256


def _pack_kernel(x_ref, o_ref):
    bits = lax.bitcast_convert_type(x_ref[...].astype(jnp.bfloat16).astype(jnp.float32), jnp.uint32)
    for r in range(ROW_SUB):
        even = bits[:, (2 * r) * 128:(2 * r + 1) * 128]
        odd = bits[:, (2 * r + 1) * 128:(2 * r + 2) * 128]
        o_ref[pl.ds(r, PACK_TM, stride=ROW_SUB), :] = (even >> 16) | (odd & jnp.uint32(0xFFFF0000))


def _expert_slab(tab_ref, row):
    return tab_ref[pl.ds(pl.multiple_of(row, ROW_SUB), ROW_SUB), :]


def _unpack(w):
    lo = lax.bitcast_convert_type(w << 16, jnp.float32)
    hi = lax.bitcast_convert_type(w & jnp.uint32(0xFFFF0000), jnp.float32)
    return lo, hi


def _peer_u_kernel(idx_a_ref, idx_b_ref, x_ref, tab_ref, out_ref):
    lane = lax.broadcasted_iota(jnp.int32, (2 * ROW_SUB, 128), 1)
    lower = lax.broadcasted_iota(jnp.int32, (2 * ROW_SUB, 128), 0) >= ROW_SUB
    pair_lane = jnp.where(lower, lane - PEER_SLOTS // 2, lane)

    def token(t):
        row = x_ref[pl.ds(t, 1), :]
        chunks = [row[:, c * 128:(c + 1) * 128] for c in range(2 * ROW_SUB)]
        xl = jnp.concatenate(chunks[0::2] * 2, axis=0)
        xh = jnp.concatenate(chunks[1::2] * 2, axis=0)
        acc = jnp.zeros((2 * ROW_SUB, 128), jnp.float32)
        base = t * (PEER_SLOTS // 2)
        for s in range(PEER_SLOTS // 2):
            w = jnp.concatenate([_expert_slab(tab_ref, idx_a_ref[base + s]),
                                 _expert_slab(tab_ref, idx_b_ref[base + s])], axis=0)
            lo, hi = _unpack(w)
            part = jnp.sum(lo * xl + hi * xh, axis=1, keepdims=True)
            acc = jnp.where(pair_lane == s, part, acc)
        out_ref[pl.ds(t, 1), :] = jnp.sum(acc, axis=0, keepdims=True)

    def trip(i, carry):
        for j in range(PEER_U_UNROLL):
            token(i * PEER_U_UNROLL + j)
        return carry

    lax.fori_loop(0, PEER_TT // PEER_U_UNROLL, trip, 0)


def _peer_v_kernel(idx_a_ref, idx_b_ref, coef_a_ref, coef_b_ref, tab_ref, out_ref):
    n_acc = 2
    zero = jnp.zeros((2 * ROW_SUB, 128), jnp.float32)
    lower = lax.broadcasted_iota(jnp.int32, (2 * ROW_SUB, 128), 0) >= ROW_SUB
    half_slots = PEER_SLOTS // 2
    pairs_per_chunk = PEER_CHUNK_SLOTS // 2

    def trip(i, carry):
        def chunk(c, accs):
            accs = list(accs)
            for u in range(PEER_V_UNROLL):
                base = (i * PEER_V_UNROLL + u) * half_slots + c * pairs_per_chunk
                for j in range(pairs_per_chunk):
                    w = jnp.concatenate([_expert_slab(tab_ref, idx_a_ref[base + j]),
                                         _expert_slab(tab_ref, idx_b_ref[base + j])], axis=0)
                    lo, hi = _unpack(w)
                    cf = jnp.where(lower, coef_b_ref[base + j], coef_a_ref[base + j])
                    k = 2 * n_acc * u + j % n_acc
                    accs[k] = accs[k] + cf * lo
                    accs[k + n_acc] = accs[k + n_acc] + cf * hi
            return tuple(accs)

        accs = lax.fori_loop(0, half_slots // pairs_per_chunk, chunk, (zero,) * (2 * n_acc * PEER_V_UNROLL))
        for u in range(PEER_V_UNROLL):
            a = accs[2 * n_acc * u:2 * n_acc * (u + 1)]
            even = a[0] + a[1]
            odd = a[2] + a[3]
            even = even[:ROW_SUB] + even[ROW_SUB:]
            odd = odd[:ROW_SUB] + odd[ROW_SUB:]
            out_ref[pl.ds(i * PEER_V_UNROLL + u, 1), :] = jnp.concatenate(
                [half[r:r + 1, :] for r in range(ROW_SUB) for half in (even, odd)], axis=1)
        return carry

    lax.fori_loop(0, PEER_TT // PEER_V_UNROLL, trip, 0)


def _smem_spec():
    return pl.BlockSpec((PEER_TT * PEER_SLOTS // 2,), lambda i: (i,), memory_space=pltpu.SMEM)


def _slot_halves(a):
    return a[:, :PEER_SLOTS // 2].reshape(-1), a[:, PEER_SLOTS // 2:].reshape(-1)


def _table_spec(rows):
    return pl.BlockSpec((rows, 128), lambda i: (0, 0), pipeline_mode=pl.Buffered(1))


def peer_u(idx, x, tab):
    t = x.shape[0]
    return pl.pallas_call(
        _peer_u_kernel,
        grid=(t // PEER_TT,),
        in_specs=[_smem_spec(), _smem_spec(), pl.BlockSpec((PEER_TT, D_MODEL), lambda i: (i, 0)),
                  _table_spec(tab.shape[0])],
        out_specs=pl.BlockSpec((PEER_TT, PEER_SLOTS), lambda i: (i, 0)),
        out_shape=jax.ShapeDtypeStruct((t, PEER_SLOTS), jnp.float32),
        compiler_params=pltpu.CompilerParams(vmem_limit_bytes=VMEM_LIMIT_PEER),
        name="peer_u",
    )(*_slot_halves(idx), x, tab)


def peer_v(idx, coef, tab):
    t = idx.shape[0]
    return pl.pallas_call(
        _peer_v_kernel,
        grid=(t // PEER_TT,),
        in_specs=[_smem_spec()] * 4 + [_table_spec(tab.shape[0])],
        out_specs=pl.BlockSpec((PEER_TT, D_MODEL), lambda i: (i, 0)),
        out_shape=jax.ShapeDtypeStruct((t, D_MODEL), jnp.float32),
        compiler_params=pltpu.CompilerParams(vmem_limit_bytes=VMEM_LIMIT_PEER),
        name="peer_v",
    )(*_slot_halves(idx), *_slot_halves(coef), tab)


PK_TT = 256
PK_A_BLOCKS = ((0, 16), (1, 8), (2, 5), (3, 4))
PK_B_BLOCKS = ((0, 4, 16), (1, 4, 8), (2, 4, 5))
NEG_HUGE = -3.0e38
PK_HEADS_PER_TRIP = 2


def _extract_top(problems, flat, n):
    vals = [[] for _ in problems]
    picked = [[] for _ in problems]
    cur = [v for v, _ in problems]
    for _ in range(n):
        for i, (_, payload) in enumerate(problems):
            m = jnp.max(cur[i], axis=0, keepdims=True)
            pos = jnp.min(jnp.where(cur[i] == m, flat, 1e9), axis=0, keepdims=True)
            hit = flat == pos
            picked[i].append(pos if payload is None else jnp.sum(jnp.where(hit, payload, 0.0), axis=0, keepdims=True))
            cur[i] = jnp.where(hit, NEG_HUGE, cur[i])
            vals[i].append(m)
    return [jnp.concatenate(v, axis=0) for v in vals], [jnp.concatenate(p, axis=0) for p in picked]


def _candidate_grid(first, second, combine):
    up8 = lambda n: -(-n // 8) * 8
    return jnp.concatenate([combine(first[a:a + 1], second[0:up8(nb)]) for a, nb in PK_A_BLOCKS]
                           + [combine(first[0:up8(a1)], second[b:b + 1]) for b, _, a1 in PK_B_BLOCKS], axis=0)


def _candidate_valid_and_flat():
    up8 = lambda n: -(-n // 8) * 8
    valid, flat = [], []
    for a, nb in PK_A_BLOCKS:
        r = lax.broadcasted_iota(jnp.int32, (up8(nb), PK_TT), 0)
        valid.append(r < nb)
        flat.append(a * PEER_TOPK + r)
    for b, a0, a1 in PK_B_BLOCKS:
        r = lax.broadcasted_iota(jnp.int32, (up8(a1), PK_TT), 0)
        valid.append((r >= a0) & (r < a1))
        flat.append(r * PEER_TOPK + b)
    valid = jnp.concatenate(valid, axis=0)
    return valid, jnp.where(valid, jnp.concatenate(flat, axis=0).astype(F32), 1e9)


def _peer_topk_kernel(q_ref, ka_ref, eid_ref, w_ref):
    key_row = lax.broadcasted_iota(jnp.int32, (N_KEYS, PK_TT), 0).astype(F32)
    valid, flat = _candidate_valid_and_flat()

    def heads(i, carry):
        scores = []
        for j in range(2 * PK_HEADS_PER_TRIP):
            q = q_ref[2 * PK_HEADS_PER_TRIP * i + j]
            q_hi = _trunc_bf16(q)
            qa = jnp.concatenate([q_hi.astype(BF16), (q - q_hi).astype(BF16), q_hi.astype(BF16)], axis=1)
            scores.append(lax.dot_general(ka_ref[2 * PK_HEADS_PER_TRIP * i + j], qa, (((1,), (1,)), ((), ())),
                                          preferred_element_type=F32))
        sub_s, sub_i = _extract_top([(s, None) for s in scores], key_row, PEER_TOPK)
        cands = []
        for j in range(PK_HEADS_PER_TRIP):
            s1, s2, i1, i2 = sub_s[2 * j], sub_s[2 * j + 1], sub_i[2 * j], sub_i[2 * j + 1]
            cands.append((jnp.where(valid, _candidate_grid(s1, s2, lambda x, y: x + y), NEG_HUGE),
                          _candidate_grid(i1, i2, lambda x, y: x * N_KEYS + y)))
        top_s, top_e = _extract_top(cands, flat, PEER_TOPK)
        for j in range(PK_HEADS_PER_TRIP):
            e = jnp.exp(top_s[j] - top_s[j][0:1])
            w_ref[PK_HEADS_PER_TRIP * i + j] = e / jnp.sum(e, axis=0, keepdims=True)
            eid_ref[PK_HEADS_PER_TRIP * i + j] = top_e[j].astype(jnp.int32) * ROW_SUB
        return carry

    lax.fori_loop(0, PEER_HEADS // PK_HEADS_PER_TRIP, heads, 0)


def peer_topk(qt, subkeys):
    hp, t, d = qt.shape
    k_hi, k_lo = _split2(subkeys.reshape(hp, N_KEYS, d))
    ka = jnp.concatenate([k_hi, k_hi, k_lo], axis=2)
    eid, w = pl.pallas_call(
        _peer_topk_kernel,
        grid=(t // PK_TT,),
        in_specs=[pl.BlockSpec((hp, PK_TT, d), lambda i: (0, i, 0)),
                  pl.BlockSpec((hp, N_KEYS, 3 * d), lambda i: (0, 0, 0))],
        out_specs=[pl.BlockSpec((PEER_HEADS, PEER_TOPK, PK_TT), lambda i: (0, 0, i)),
                   pl.BlockSpec((PEER_HEADS, PEER_TOPK, PK_TT), lambda i: (0, 0, i))],
        out_shape=[jax.ShapeDtypeStruct((PEER_HEADS, PEER_TOPK, t), jnp.int32),
                   jax.ShapeDtypeStruct((PEER_HEADS, PEER_TOPK, t), F32)],
        name="peer_topk",
    )(qt, ka)
    to_slots = lambda a: a.transpose(2, 0, 1).reshape(t, PEER_SLOTS)
    return to_slots(eid), to_slots(w)


def rmsnorm(x, g):
    xf = x.astype(jnp.float32)
    y = xf * lax.rsqrt(jnp.mean(xf * xf, axis=-1, keepdims=True) + RMS_EPS)
    return (y * g.astype(jnp.float32)).astype(x.dtype)


def split_columns(t, sizes):
    offs = np.cumsum((0,) + tuple(sizes))
    return [t[..., int(a):int(b)] for a, b in zip(offs[:-1], offs[1:])]


def rope_tables(pos):
    inv = jnp.power(ROPE_THETA, -jnp.arange(0, ROPE_DIM, 2, dtype=jnp.float32) / ROPE_DIM)
    ang = pos[:, None] * inv[None, :]
    return jnp.cos(ang), jnp.sin(ang)


def apply_partial_rope(x, cos, sin):
    half = ROPE_DIM // 2
    xr = x[..., :ROPE_DIM].astype(jnp.float32)
    x1, x2 = xr[..., :half], xr[..., half:]
    c, s = cos[:, None, :], sin[:, None, :]
    rot = jnp.concatenate([x1 * c - x2 * s, x1 * s + x2 * c], axis=-1)
    return jnp.concatenate([rot.astype(x.dtype), x[..., ROPE_DIM:]], axis=-1)


BF16 = jnp.bfloat16
F32 = jnp.float32


def _logits(ks, qts):
    return [jnp.dot(k, qt, preferred_element_type=F32) for k, qt in zip(ks, qts)]


def _flash_update(ss, vts, mask, carries):
    stats, ps = [], []
    for s, (m, l, _) in zip(ss, carries):
        s_vis = s if mask is None else jnp.where(mask, s, NEG_INF)
        m_new = jnp.maximum(m, jnp.max(s_vis, axis=0, keepdims=True))
        alpha = jnp.exp(m - m_new)
        p = jnp.exp(s - m_new)
        if mask is not None:
            p = jnp.where(mask, p, 0.0)
        stats.append((m_new, alpha, alpha * l + jnp.sum(p, axis=0, keepdims=True)))
        ps.append(p.astype(BF16))
    return tuple((m_new, l, alpha * acc + jnp.dot(vt, p, preferred_element_type=F32))
                 for (m_new, alpha, l), p, vt, (_, _, acc) in zip(stats, ps, vts, carries))


def _causal_flash(key_tile, value_tile, qts, n_full, last_mask, n):
    def body(kt, carries):
        return _flash_update(_logits(key_tile(kt), qts), value_tile(kt), None, carries)

    carries = lax.fori_loop(0, n_full, body, tuple(_flash_init(n) for _ in qts))
    return _flash_update(_logits(key_tile(n_full), qts), value_tile(n_full), last_mask, carries)


def _flash_init(n):
    return (jnp.full((1, n), NEG_INF, F32), jnp.zeros((1, n), F32), jnp.zeros((HEAD_DIM, n), F32))


FOX_TQ = 256
FOX_KV = 512
FOX_DK = 256


FOX_HP = 4


def _fox_kernel(qt_ref, k_ref, vt_ref, o_ref):
    qi = pl.program_id(1)
    q0 = qi * FOX_TQ
    qts = [qt_ref[h] for h in range(FOX_HP)]
    qpos = q0 + lax.broadcasted_iota(jnp.int32, (1, FOX_TQ), 1)
    krow = lax.broadcasted_iota(jnp.int32, (FOX_KV, FOX_TQ), 0)

    n_full = q0 // FOX_KV
    causal = (n_full * FOX_KV + krow) <= qpos
    carry = _causal_flash(lambda kt: [k_ref[h, kt] for h in range(FOX_HP)],
                          lambda kt: [vt_ref[h, kt] for h in range(FOX_HP)], qts, n_full, causal, FOX_TQ)
    for h in range(FOX_HP):
        _, l, acc = carry[h]
        o_ref[h] = acc / l


def _trunc_bf16(x):
    bits = lax.bitcast_convert_type(x, jnp.uint32) & jnp.uint32(0xFFFF0000)
    return lax.bitcast_convert_type(bits, F32)


def _split3(c):
    c1 = _trunc_bf16(c)
    r = c - c1
    c2 = _trunc_bf16(r)
    return c1.astype(BF16), c2.astype(BF16), (r - c2).astype(BF16)


def _split2(x):
    hi = _trunc_bf16(x)
    return hi.astype(BF16), (x - hi).astype(BF16)


def _aug_q(q):
    hi, lo = _split2(q)
    return jnp.concatenate([hi, hi, lo], axis=-1)


def _aug_k(k):
    hi, lo = _split2(k)
    return jnp.concatenate([hi, lo, hi], axis=-1)


def fox_attention(q, k, v, log_f):
    B_, S_, H, Dh = q.shape
    KV_TILE = FOX_KV
    n_qt, n_kt = S_ // FOX_TQ, S_ // KV_TILE
    c = jnp.cumsum(log_f, axis=1)
    c1, c2, c3 = _split3(c)
    one = jnp.ones_like(c1)
    pad = jnp.zeros((B_, S_, H, FOX_DK - 3 * Dh - 6), BF16)
    col = lambda t: t[..., None]
    qa = jnp.concatenate([_aug_q(q * (Dh ** -0.5)), col(one), col(one), col(one),
                          col(c1), col(c2), col(c3), pad], axis=-1)
    ka = jnp.concatenate([_aug_k(k), col(-c1), col(-c2), col(-c3),
                          col(one), col(one), col(one), pad], axis=-1)
    HP, HG = FOX_HP, H // FOX_HP
    qt = qa.reshape(B_, n_qt, FOX_TQ, HG, HP, FOX_DK).transpose(0, 3, 1, 4, 5, 2).reshape(B_ * HG * n_qt, HP, FOX_DK, FOX_TQ)
    kk = ka.reshape(B_, n_kt, KV_TILE, HG, HP, FOX_DK).transpose(0, 3, 4, 1, 2, 5).reshape(B_ * HG, HP, n_kt, KV_TILE, FOX_DK)
    vt = v.astype(BF16).reshape(B_, n_kt, KV_TILE, HG, HP, Dh).transpose(0, 3, 4, 1, 5, 2).reshape(B_ * HG, HP, n_kt, Dh, KV_TILE)
    out = pl.pallas_call(
        _fox_kernel,
        grid=(B_ * HG, n_qt),
        in_specs=[pl.BlockSpec((None, HP, FOX_DK, FOX_TQ), lambda b, i: (b * n_qt + i, 0, 0, 0)),
                  pl.BlockSpec((None, HP, n_kt, KV_TILE, FOX_DK), lambda b, i: (b, 0, 0, 0, 0)),
                  pl.BlockSpec((None, HP, n_kt, Dh, KV_TILE), lambda b, i: (b, 0, 0, 0, 0))],
        out_specs=pl.BlockSpec((None, HP, Dh, FOX_TQ), lambda b, i: (b * n_qt + i, 0, 0, 0)),
        out_shape=jax.ShapeDtypeStruct((B_ * HG * n_qt, HP, Dh, FOX_TQ), F32),
        name="fox_attention",
    )(qt, kk, vt)
    return out.reshape(B_, HG, n_qt, HP, Dh, FOX_TQ).transpose(0, 2, 5, 1, 3, 4).reshape(B_, S_, H * Dh)


def compress_kv(k, v, cmp_pos, w1, w2):
    B_, S_ = k.shape[0], k.shape[1]
    n_cmp = (S_ - CMP_LEN) // CMP_STRIDE + 1
    starts = np.arange(n_cmp) * CMP_STRIDE
    idx = starts[:, None] + np.arange(CMP_LEN)[None, :]

    def phi(t, j):
        blocks = t[:, idx] + cmp_pos[j][None, None, :, None, :]
        flat = jnp.swapaxes(blocks, 2, 3).reshape(B_, n_cmp, NSA_GROUPS, CMP_LEN * HEAD_DIM)
        return jax.nn.gelu(flat @ w1[j], approximate=False) @ w2[j]

    return phi(k, 0), phi(v, 1), jnp.asarray(starts + CMP_LEN - 1, dtype=jnp.int32)


NSA_TQ = 128
NSA_N = NSA_HPG * NSA_TQ
N_CMP_PAD = 256
N_SLC = 64
NSA_DK = 3 * HEAD_DIM
NSA_KV = 512
WIN_TILES = WINDOW // NSA_TQ + 1
WIN_KEYS = WIN_TILES * NSA_TQ


def _nsa_kernel(qt_ref, kc_ref, vct_ref, ks_ref, vst_ref, kw_ref, vwt_ref, gate_ref, ovl_ref, o_ref):
    qi = pl.program_id(1)
    q0 = qi * NSA_TQ
    qt = qt_ref[...]
    lane = lax.broadcasted_iota(jnp.int32, (1, NSA_N), 1)
    qpos = q0 + (lane & (NSA_TQ - 1))

    hn = NSA_N // 2
    qts = (qt[:, :hn], qt[:, hn:])
    qpos_h = qpos[:, :hn]

    w0 = jnp.maximum(qi - WINDOW // NSA_TQ, 0)
    k_w = jnp.concatenate([kw_ref[w0 + i] for i in range(WIN_TILES)], axis=0)
    vt_w = jnp.concatenate([vwt_ref[w0 + i] for i in range(WIN_TILES)], axis=1)
    s = jnp.dot(kc_ref[...], qt, preferred_element_type=F32)
    s_w = [jnp.dot(k_w, qts[h], preferred_element_type=F32) for h in range(2)]

    cmp_end = lax.broadcasted_iota(jnp.int32, (N_CMP_PAD, NSA_N), 0) * CMP_STRIDE + (CMP_LEN - 1)
    mc = cmp_end <= qpos
    m = jnp.max(jnp.where(mc, s, NEG_INF), axis=0, keepdims=True)
    p = jnp.where(mc, jnp.exp(s - m), 0.0)
    l = jnp.sum(p, axis=0, keepdims=True)
    pc = p * jnp.where(l > 0.0, 1.0 / l, 0.0)
    o_cmp = jnp.dot(vct_ref[...], pc.astype(BF16), preferred_element_type=F32)
    pcs = (pc[:, 0:NSA_TQ] + pc[:, NSA_TQ:2 * NSA_TQ]) + (pc[:, 2 * NSA_TQ:3 * NSA_TQ] + pc[:, 3 * NSA_TQ:4 * NSA_TQ])
    pcs_hi = pcs.astype(BF16)
    pcs_lo = (pcs - pcs_hi.astype(F32)).astype(BF16)
    ovl = ovl_ref[...]
    imp = jnp.dot(ovl, pcs_hi, preferred_element_type=F32) + jnp.dot(ovl, pcs_lo, preferred_element_type=F32)

    rel = qpos_h - (w0 * NSA_TQ + lax.broadcasted_iota(jnp.int32, (WIN_KEYS, hn), 0))
    in_win = (rel >= 0) & (rel < WINDOW)
    outs_w = []
    for h in range(2):
        m = jnp.max(jnp.where(in_win, s_w[h], NEG_INF), axis=0, keepdims=True)
        p = jnp.where(in_win, jnp.exp(s_w[h] - m), 0.0)
        l = jnp.sum(p, axis=0, keepdims=True)
        outs_w.append(jnp.dot(vt_w, p.astype(BF16), preferred_element_type=F32) / l)
    acc_w = jnp.concatenate(outs_w, axis=1)

    blk = lax.broadcasted_iota(jnp.int32, (N_SLC, NSA_TQ), 0)
    q_blk = (q0 + lax.broadcasted_iota(jnp.int32, (N_SLC, NSA_TQ), 1)) >> 6
    forced = (blk == 0) | (blk == q_blk) | (blk == q_blk - 1)
    score = jnp.where(forced, FORCED_SCORE, jnp.where(blk <= q_blk, imp, -1.0))
    rank = jnp.zeros((N_SLC, NSA_TQ), jnp.int32)
    for i in range(N_SLC):
        row = score[i:i + 1, :]
        beats = (row > score) | ((row == score) & (blk > i))
        rank = rank + beats.astype(jnp.int32)
    bias = jnp.where(rank < SLC_TOP, 0.0, NEG_INF).astype(BF16)
    bias = jnp.concatenate([bias] * (NSA_HPG // 2), axis=1)
    qts_s = [jnp.concatenate([qh, bias], axis=0) for qh in qts]
    krow = lax.broadcasted_iota(jnp.int32, (NSA_KV, hn), 0)

    n_full = q0 // NSA_KV
    causal = (n_full * NSA_KV + krow) <= qpos_h
    slc = _causal_flash(lambda kt: [ks_ref[kt]] * 2, lambda kt: [vst_ref[kt]] * 2, qts_s, n_full, causal, hn)
    acc_s = jnp.concatenate([slc[0][2] / slc[0][1], slc[1][2] / slc[1][1]], axis=1)

    g = jax.nn.sigmoid(gate_ref[...])
    o_ref[...] = g[0:1] * o_cmp + g[1:2] * acc_s + g[2:3] * acc_w


def nsa_attention(q, k_cmp, v_cmp, k_slc, v_slc, k_win, v_win, gate_logit):
    B_, S_, H, Dh = q.shape
    G = NSA_GROUPS
    n_qt = S_ // NSA_TQ
    n_cmp = k_cmp.shape[1]
    cs = np.arange(N_CMP_PAD)[None, :] * CMP_STRIDE
    ss = np.arange(N_SLC)[:, None] * SLC_LEN
    ov = np.clip(np.minimum(cs + CMP_LEN, ss + SLC_LEN) - np.maximum(cs, ss), 0, None) / CMP_LEN
    ov[:, n_cmp:] = 0.0
    ovl = jnp.asarray(ov, dtype=BF16)
    qs = _aug_q(q * (Dh ** -0.5)).reshape(B_, n_qt, NSA_TQ, G, NSA_HPG, NSA_DK)
    qt = qs.transpose(0, 3, 1, 5, 4, 2).reshape(B_ * G * n_qt, NSA_DK, NSA_N)
    gt = gate_logit.astype(F32).reshape(B_, n_qt, NSA_TQ, G, NSA_HPG, N_NSA_BRANCH)
    gt = gt.transpose(0, 3, 1, 5, 4, 2).reshape(B_ * G * n_qt, N_NSA_BRANCH, NSA_N)
    tiles = lambda t, kv: t.reshape(B_, S_ // kv, kv, G, -1).transpose(0, 3, 1, 2, 4).reshape(B_ * G, S_ // kv, kv, t.shape[-1])
    keys = lambda t, kv: tiles(_aug_k(t), kv)
    block_onehot = (jnp.arange(S_)[:, None] // SLC_LEN == jnp.arange(N_SLC)[None, :]).astype(BF16)
    k_slc_aug = jnp.concatenate([_aug_k(k_slc), jnp.broadcast_to(block_onehot[None, :, None, :], (B_, S_, G, N_SLC))], axis=-1)
    vals = lambda t, kv: t.astype(BF16).reshape(B_, S_ // kv, kv, G, Dh).transpose(0, 3, 1, 4, 2).reshape(B_ * G, S_ // kv, Dh, kv)
    padc = ((0, 0), (0, N_CMP_PAD - n_cmp), (0, 0), (0, 0))
    kc = _aug_k(jnp.pad(k_cmp, padc)).transpose(0, 2, 1, 3).reshape(B_ * G, N_CMP_PAD, NSA_DK)
    vct = jnp.pad(v_cmp, padc).astype(BF16).transpose(0, 2, 3, 1).reshape(B_ * G, Dh, N_CMP_PAD)
    per_bg = lambda *blk: pl.BlockSpec((None,) + blk, lambda b, i: (b,) + (0,) * len(blk))
    per_tile = lambda *blk: pl.BlockSpec((None,) + blk, lambda b, i: (b * n_qt + i,) + (0,) * len(blk))
    out = pl.pallas_call(
        _nsa_kernel,
        grid=(B_ * G, n_qt),
        in_specs=[per_tile(NSA_DK, NSA_N),
                  per_bg(N_CMP_PAD, NSA_DK), per_bg(Dh, N_CMP_PAD),
                  per_bg(S_ // NSA_KV, NSA_KV, NSA_DK + N_SLC), per_bg(S_ // NSA_KV, Dh, NSA_KV),
                  per_bg(n_qt, NSA_TQ, NSA_DK), per_bg(n_qt, Dh, NSA_TQ),
                  per_tile(N_NSA_BRANCH, NSA_N),
                  pl.BlockSpec((N_SLC, N_CMP_PAD), lambda b, i: (0, 0))],
        out_specs=per_tile(Dh, NSA_N),
        out_shape=jax.ShapeDtypeStruct((B_ * G * n_qt, Dh, NSA_N), F32),
        name="nsa_attention",
    )(qt, kc, vct, tiles(k_slc_aug, NSA_KV), vals(v_slc, NSA_KV), keys(k_win, NSA_TQ), vals(v_win, NSA_TQ), gt, ovl)
    return out.reshape(B_, G, n_qt, Dh, NSA_HPG, NSA_TQ).transpose(0, 2, 5, 1, 4, 3).reshape(B_, S_, H * Dh)


DENSE_TM = 256
SEC_QKV = 4 * BRANCH_WIDTH
SEC_KV = 6 * KV_WIDTH
SEC_MERGE = N_BRANCH * D_MODEL
SEC_SMALL = 256
IN_SECTIONS = (SEC_QKV, SEC_KV, SEC_MERGE, SEC_SMALL)


def _rms(x, g):
    return x * lax.rsqrt(jnp.mean(x * x, axis=-1, keepdims=True) + RMS_EPS) * g


def _row_spec(width):
    return pl.BlockSpec((DENSE_TM, width), lambda i: (i, 0))


def _whole_spec(shape):
    return pl.BlockSpec(shape, lambda i: (0,) * len(shape))


def _in_proj_kernel(x_ref, g_ref, w_ref, *o_refs):
    xn = _rms(x_ref[...], g_ref[...]).astype(BF16)
    off = 0
    for o_ref, width in zip(o_refs, IN_SECTIONS):
        o_ref[...] = jnp.dot(xn, w_ref[:, off:off + width], preferred_element_type=F32)
        off += width


def in_projection(x, g, w_in):
    t, d = x.shape
    fq, fk, fv, fl, nq, kc, vc, ksl, vsl, kwn, vwn, gl, ml = split_columns(w_in, IN_SPLITS)
    pad = lambda w: jnp.pad(w, ((0, 0), (0, 128 - w.shape[1])))
    w = jnp.concatenate([fq, fk, fv, nq, kc, vc, ksl, vsl, kwn, vwn, ml, pad(fl), pad(gl)], axis=1).astype(BF16)
    return pl.pallas_call(
        _in_proj_kernel,
        grid=(t // DENSE_TM,),
        in_specs=[_row_spec(d), _whole_spec((1, d)), _whole_spec(w.shape)],
        out_specs=[_row_spec(s) for s in IN_SECTIONS],
        out_shape=[jax.ShapeDtypeStruct((t, s), F32) for s in IN_SECTIONS],
        name="in_projection",
    )(x, g.reshape(1, d), w)


def _merge_kernel(x_ref, yf_ref, yn_ref, ml_ref, wb_ref, wo_ref, o_ref):
    g = jax.nn.sigmoid(ml_ref[...])
    up_f = jnp.dot(yf_ref[...].astype(BF16), wb_ref[0], preferred_element_type=F32)
    up_n = jnp.dot(yn_ref[...].astype(BF16), wb_ref[1], preferred_element_type=F32)
    merged = g[:, :D_MODEL] * up_f + g[:, D_MODEL:] * up_n
    o_ref[...] = x_ref[...] + jnp.dot(merged.astype(BF16), wo_ref[...], preferred_element_type=F32)


def merge_out(x, y_fox, y_nsa, merge_logit, w_branch, w_out):
    t, d = x.shape
    return pl.pallas_call(
        _merge_kernel,
        grid=(t // DENSE_TM,),
        in_specs=[_row_spec(d), _row_spec(BRANCH_WIDTH), _row_spec(BRANCH_WIDTH), _row_spec(SEC_MERGE),
                  _whole_spec(w_branch.shape), _whole_spec(w_out.shape)],
        out_specs=_row_spec(d),
        out_shape=jax.ShapeDtypeStruct((t, d), F32),
        name="merge_out",
    )(x, y_fox, y_nsa, merge_logit, w_branch.astype(BF16), w_out.astype(BF16))


def _peer_q_kernel(x_ref, g_ref, w_ref, hn_ref, q_ref):
    hn = _rms(x_ref[...], g_ref[...])
    hn_ref[...] = hn
    q = jnp.dot(hn.astype(BF16), w_ref[...], preferred_element_type=F32)
    d = PEER_QDIM // 2
    for j in range(2 * PEER_HEADS):
        q_ref[j] = q[:, j * d:(j + 1) * d]


def peer_query(x, g, wq):
    t, d = x.shape
    hp, dq = 2 * PEER_HEADS, PEER_QDIM // 2
    return pl.pallas_call(
        _peer_q_kernel,
        grid=(t // DENSE_TM,),
        in_specs=[_row_spec(d), _whole_spec((1, d)), _whole_spec(wq.shape)],
        out_specs=[_row_spec(d), pl.BlockSpec((hp, DENSE_TM, dq), lambda i: (0, i, 0))],
        out_shape=[jax.ShapeDtypeStruct((t, d), F32), jax.ShapeDtypeStruct((hp, t, dq), F32)],
        name="peer_query",
    )(x, g.reshape(1, d), wq.astype(BF16))


def _final_kernel(h_ref, p_ref, g_ref, o_ref):
    o_ref[...] = _rms(h_ref[...] + p_ref[...], g_ref[...])


def final_norm(h, p, g):
    t, d = h.shape
    return pl.pallas_call(
        _final_kernel,
        grid=(t // DENSE_TM,),
        in_specs=[_row_spec(d), _row_spec(d), _whole_spec((1, d))],
        out_specs=_row_spec(d),
        out_shape=jax.ShapeDtypeStruct((t, d), F32),
        name="final_norm",
    )(h, p, g.reshape(1, d))


N_CMP_ROWS = 256


def _compress_kernel(r_ref, pos_ref, w1_ref, w2_ref, o_ref):
    r = r_ref[...]
    nxt = jnp.concatenate([r[1:], jnp.zeros((1, r.shape[1]), F32)], axis=0)
    half = r.shape[1]
    hid = (jnp.dot((r + pos_ref[0:1, :]).astype(BF16), w1_ref[:half], preferred_element_type=F32)
           + jnp.dot((nxt + pos_ref[1:2, :]).astype(BF16), w1_ref[half:], preferred_element_type=F32))
    hid = 0.5 * hid * (1.0 + lax.erf(hid * (2.0 ** -0.5)))
    o_ref[...] = jnp.dot(hid.astype(BF16), w2_ref[...], preferred_element_type=F32)


def compress_kv_pallas(kv, cmp_pos, w1, w2):
    _, B_, S_, G, Dh = kv.shape
    n_cmp = (S_ - CMP_LEN) // CMP_STRIDE + 1
    width = CMP_STRIDE * Dh
    rows = kv.transpose(0, 1, 3, 2, 4).reshape(2, B_ * G, S_ // CMP_STRIDE, width)
    out = pl.pallas_call(
        _compress_kernel,
        grid=(2, B_ * G),
        in_specs=[pl.BlockSpec((None, None, N_CMP_ROWS, width), lambda j, b: (j, b, 0, 0)),
                  pl.BlockSpec((None, 2, width), lambda j, b: (j, 0, 0)),
                  pl.BlockSpec((None, CMP_LEN * Dh, CMP_HIDDEN), lambda j, b: (j, 0, 0)),
                  pl.BlockSpec((None, CMP_HIDDEN, Dh), lambda j, b: (j, 0, 0))],
        out_specs=pl.BlockSpec((None, None, N_CMP_ROWS, Dh), lambda j, b: (j, b, 0, 0)),
        out_shape=jax.ShapeDtypeStruct((2, B_ * G, N_CMP_ROWS, Dh), F32),
        name="compress_kv",
    )(rows, cmp_pos.reshape(2, 2, width), w1.astype(BF16), w2.astype(BF16))
    return out[:, :, :n_cmp].reshape(2, B_, G, n_cmp, Dh).transpose(0, 1, 3, 2, 4)


def hybrid_mixer(x, norm_g, w_in, fox_f_bias, cmp_pos, cmp_w1, cmp_w2, w_branch, w_out):
    B_, S_, D = x.shape
    xf = x.reshape(B_ * S_, D)
    qkv, kvs, merge_logit, small = in_projection(xf, norm_g, w_in)
    heads = lambda t, n: t.reshape(B_, S_, n, HEAD_DIM)
    fq, fk, fv, nq = (qkv[:, i * BRANCH_WIDTH:(i + 1) * BRANCH_WIDTH] for i in range(4))
    kc, vc, ksl, vsl, kwn, vwn = (heads(kvs[:, i * KV_WIDTH:(i + 1) * KV_WIDTH], NSA_GROUPS) for i in range(6))
    f_logit = small[:, :FOX_HEADS].reshape(B_, S_, FOX_HEADS)
    gate_logit = small[:, 128:128 + NSA_HEADS * N_NSA_BRANCH].reshape(B_, S_, NSA_HEADS, N_NSA_BRANCH)
    log_f = jax.nn.log_sigmoid(f_logit + fox_f_bias.astype(F32))
    y_fox = fox_attention(heads(fq, FOX_HEADS), heads(fk, FOX_HEADS), heads(fv, FOX_HEADS), log_f)
    cos, sin = rope_tables(jnp.arange(S_, dtype=F32))
    q_nsa = apply_partial_rope(heads(nq, NSA_HEADS), cos, sin)
    k_slc = apply_partial_rope(ksl, cos, sin)
    k_win = apply_partial_rope(kwn, cos, sin)
    cmp = compress_kv_pallas(jnp.stack([kc, vc]), cmp_pos, cmp_w1, cmp_w2)
    n_cmp = cmp.shape[2]
    cmp_end = jnp.arange(n_cmp, dtype=F32) * CMP_STRIDE + (CMP_LEN - 1)
    k_cmp = apply_partial_rope(cmp[0], *rope_tables(cmp_end))
    y_nsa = nsa_attention(q_nsa, k_cmp, cmp[1], k_slc, vsl, k_win, vwn, gate_logit)
    return merge_out(xf, y_fox.reshape(B_ * S_, -1), y_nsa.reshape(B_ * S_, -1), merge_logit, w_branch, w_out)


def peer_ffn(h, norm_g, wq, subkeys, u, v):
    hn, q = peer_query(h, norm_g, wq)
    idx2, w = peer_topk(q, subkeys)
    act = jax.nn.gelu(peer_u(idx2, hn, pack_table(u)), approximate=False)
    return peer_v(idx2, w * act, pack_table(v))


def kernel(x, norm_mix, w_in, fox_f_bias, nsa_cmp_pos, nsa_cmp_w1, nsa_cmp_w2, w_branch, w_out,
           norm_ffn, peer_wq, peer_subkeys, peer_u, peer_v, norm_final):
    B_, S_, D = x.shape
    assert norm_mix.shape[0] == 1, "single-layer trunk"
    h = hybrid_mixer(x, norm_mix[0], w_in[0], fox_f_bias[0], nsa_cmp_pos[0], nsa_cmp_w1[0], nsa_cmp_w2[0],
                     w_branch[0], w_out[0])
    p = peer_ffn(h, norm_ffn[0], peer_wq[0], peer_subkeys[0], peer_u[0], peer_v[0])
    return final_norm(h, p, norm_final).reshape(B_, S_, D)
```

```python
import jax
import jax.numpy as jnp
import numpy as np
from jax import lax
from jax.experimental import pallas as pl
from jax.experimental.pallas import tpu as pltpu

D_MODEL = 1024
HEAD_DIM = 64
FOX_HEADS = 8
NSA_HEADS = 8
NSA_GROUPS = 2
NSA_HPG = NSA_HEADS // NSA_GROUPS
BRANCH_WIDTH = 512
N_BRANCH = 2
N_NSA_BRANCH = 3
ROPE_DIM = HEAD_DIM // 4
ROPE_THETA = 500000.0
CMP_LEN = 32
CMP_STRIDE = 16
CMP_HIDDEN = 2 * HEAD_DIM
SLC_LEN = 64
SLC_TOP = 16
WINDOW = 512
FORCED_SCORE = 1e9
NEG_INF = -1e30
PEER_HEADS = 8
N_KEYS = 128
PEER_QDIM = 256
PEER_TOPK = 16
RMS_EPS = 1e-6
KV_WIDTH = NSA_GROUPS * HEAD_DIM
IN_SPLITS = (BRANCH_WIDTH, BRANCH_WIDTH, BRANCH_WIDTH, FOX_HEADS, BRANCH_WIDTH,
             KV_WIDTH, KV_WIDTH, KV_WIDTH, KV_WIDTH, KV_WIDTH, KV_WIDTH,
             NSA_HEADS * N_NSA_BRANCH, N_BRANCH * D_MODEL)


BF16 = jnp.bfloat16
F32 = jnp.float32

PEER_SLOTS = PEER_HEADS * PEER_TOPK
PEER_TT = 64
PEER_CHUNK_SLOTS = 32
PEER_U_UNROLL = 4
PEER_V_UNROLL = 2
ROW_SUB = D_MODEL // 2 // 128
VMEM_LIMIT_PEER = N_KEYS * N_KEYS * D_MODEL * 2 + 16 * 1024 * 1024


def pack_table(tab):
    n, d = tab.shape
    return pl.pallas_call(
        _pack_kernel,
        grid=(n // PACK_TM,),
        in_specs=[pl.BlockSpec((PACK_TM, d), lambda i: (i, 0))],
        out_specs=pl.BlockSpec((PACK_TM * ROW_SUB, 128), lambda i: (i, 0)),
        out_shape=jax.ShapeDtypeStruct((n * ROW_SUB, 128), jnp.uint32),
        name="pack_table",
    )(tab)


PACK_TM = 256


def _pack_kernel(x_ref, o_ref):
    bits = lax.bitcast_convert_type(x_ref[...].astype(jnp.bfloat16).astype(jnp.float32), jnp.uint32)
    for r in range(ROW_SUB):
        even = bits[:, (2 * r) * 128:(2 * r + 1) * 128]
        odd = bits[:, (2 * r + 1) * 128:(2 * r + 2) * 128]
        o_ref[pl.ds(r, PACK_TM, stride=ROW_SUB), :] = (even >> 16) | (odd & jnp.uint32(0xFFFF0000))


def _expert_slab(tab_ref, row):
    return tab_ref[pl.ds(pl.multiple_of(row, ROW_SUB), ROW_SUB), :]


def _unpack(w):
    lo = lax.bitcast_convert_type(w << 16, jnp.float32)
    hi = lax.bitcast_convert_type(w & jnp.uint32(0xFFFF0000), jnp.float32)
    return lo, hi


def _gelu(x):
    return 0.5 * x * (1.0 + lax.erf(x * (2.0 ** -0.5)))


def _peer_u_kernel(idx_a_ref, idx_b_ref, x_ref, w_ref, tab_ref, out_ref):
    lane = lax.broadcasted_iota(jnp.int32, (2 * ROW_SUB, 128), 1)
    lower = lax.broadcasted_iota(jnp.int32, (2 * ROW_SUB, 128), 0) >= ROW_SUB
    pair_lane = jnp.where(lower, lane - PEER_SLOTS // 2, lane)

    def token(t):
        row = x_ref[pl.ds(t, 1), :]
        chunks = [row[:, c * 128:(c + 1) * 128] for c in range(2 * ROW_SUB)]
        xl = jnp.concatenate(chunks[0::2] * 2, axis=0)
        xh = jnp.concatenate(chunks[1::2] * 2, axis=0)
        acc = jnp.zeros((2 * ROW_SUB, 128), jnp.float32)
        base = t * (PEER_SLOTS // 2)
        for s in range(PEER_SLOTS // 2):
            w = jnp.concatenate([_expert_slab(tab_ref, idx_a_ref[base + s]),
                                 _expert_slab(tab_ref, idx_b_ref[base + s])], axis=0)
            lo, hi = _unpack(w)
            part = jnp.sum(lo * xl + hi * xh, axis=1, keepdims=True)
            acc = jnp.where(pair_lane == s, part, acc)
        out_ref[pl.ds(t, 1), :] = w_ref[pl.ds(t, 1), :] * _gelu(jnp.sum(acc, axis=0, keepdims=True))

    def trip(i, carry):
        for j in range(PEER_U_UNROLL):
            token(i * PEER_U_UNROLL + j)
        return carry

    lax.fori_loop(0, PEER_TT // PEER_U_UNROLL, trip, 0)


def _peer_v_kernel(idx_a_ref, idx_b_ref, coef_a_ref, coef_b_ref, tab_ref, out_ref):
    n_acc = 2
    zero = jnp.zeros((2 * ROW_SUB, 128), jnp.float32)
    lower = lax.broadcasted_iota(jnp.int32, (2 * ROW_SUB, 128), 0) >= ROW_SUB
    half_slots = PEER_SLOTS // 2
    pairs_per_chunk = PEER_CHUNK_SLOTS // 2

    def trip(i, carry):
        def chunk(c, accs):
            accs = list(accs)
            for u in range(PEER_V_UNROLL):
                base = (i * PEER_V_UNROLL + u) * half_slots + c * pairs_per_chunk
                for j in range(pairs_per_chunk):
                    w = jnp.concatenate([_expert_slab(tab_ref, idx_a_ref[base + j]),
                                         _expert_slab(tab_ref, idx_b_ref[base + j])], axis=0)
                    lo, hi = _unpack(w)
                    cf = jnp.where(lower, coef_b_ref[base + j], coef_a_ref[base + j])
                    k = 2 * n_acc * u + j % n_acc
                    accs[k] = accs[k] + cf * lo
                    accs[k + n_acc] = accs[k + n_acc] + cf * hi
            return tuple(accs)

        accs = lax.fori_loop(0, half_slots // pairs_per_chunk, chunk, (zero,) * (2 * n_acc * PEER_V_UNROLL))
        for u in range(PEER_V_UNROLL):
            a = accs[2 * n_acc * u:2 * n_acc * (u + 1)]
            even = a[0] + a[1]
            odd = a[2] + a[3]
            even = even[:ROW_SUB] + even[ROW_SUB:]
            odd = odd[:ROW_SUB] + odd[ROW_SUB:]
            out_ref[pl.ds(i * PEER_V_UNROLL + u, 1), :] = jnp.concatenate(
                [half[r:r + 1, :] for r in range(ROW_SUB) for half in (even, odd)], axis=1)
        return carry

    lax.fori_loop(0, PEER_TT // PEER_V_UNROLL, trip, 0)


def _smem_spec():
    return pl.BlockSpec((PEER_TT * PEER_SLOTS // 2,), lambda i: (i,), memory_space=pltpu.SMEM)


def _slot_halves(a):
    return a[:, :PEER_SLOTS // 2].reshape(-1), a[:, PEER_SLOTS // 2:].reshape(-1)


def _table_spec(rows):
    return pl.BlockSpec((rows, 128), lambda i: (0, 0), pipeline_mode=pl.Buffered(1))


def peer_u(idx, x, w, tab):
    t = x.shape[0]
    return pl.pallas_call(
        _peer_u_kernel,
        grid=(t // PEER_TT,),
        in_specs=[_smem_spec(), _smem_spec(), pl.BlockSpec((PEER_TT, D_MODEL), lambda i: (i, 0)),
                  pl.BlockSpec((PEER_TT, PEER_SLOTS), lambda i: (i, 0)), _table_spec(tab.shape[0])],
        out_specs=pl.BlockSpec((PEER_TT, PEER_SLOTS), lambda i: (i, 0)),
        out_shape=jax.ShapeDtypeStruct((t, PEER_SLOTS), jnp.float32),
        compiler_params=pltpu.CompilerParams(vmem_limit_bytes=VMEM_LIMIT_PEER),
        name="peer_u",
    )(*_slot_halves(idx), x, w, tab)


def peer_v(idx, coef, tab):
    t = idx.shape[0]
    return pl.pallas_call(
        _peer_v_kernel,
        grid=(t // PEER_TT,),
        in_specs=[_smem_spec()] * 4 + [_table_spec(tab.shape[0])],
        out_specs=pl.BlockSpec((PEER_TT, D_MODEL), lambda i: (i, 0)),
        out_shape=jax.ShapeDtypeStruct((t, D_MODEL), jnp.float32),
        compiler_params=pltpu.CompilerParams(vmem_limit_bytes=VMEM_LIMIT_PEER),
        name="peer_v",
    )(*_slot_halves(idx), *_slot_halves(coef), tab)


PK_TT = 256
PK_A_BLOCKS = ((0, 16), (1, 8), (2, 5), (3, 4))
PK_B_BLOCKS = ((0, 4, 16), (1, 4, 8), (2, 4, 5))
NEG_HUGE = -3.0e38
PK_HEADS_PER_TRIP = 2


def _extract_top(problems, flat, n):
    vals = [[] for _ in problems]
    picked = [[] for _ in problems]
    cur = [v for v, _ in problems]
    for _ in range(n):
        for i, (_, payload) in enumerate(problems):
            m = jnp.max(cur[i], axis=0, keepdims=True)
            pos = jnp.min(jnp.where(cur[i] == m, flat, 1e9), axis=0, keepdims=True)
            hit = flat == pos
            picked[i].append(pos if payload is None else jnp.sum(jnp.where(hit, payload, 0.0), axis=0, keepdims=True))
            cur[i] = jnp.where(hit, NEG_HUGE, cur[i])
            vals[i].append(m)
    return [jnp.concatenate(v, axis=0) for v in vals], [jnp.concatenate(p, axis=0) for p in picked]


def _candidate_grid(first, second, combine):
    up8 = lambda n: -(-n // 8) * 8
    return jnp.concatenate([combine(first[a:a + 1], second[0:up8(nb)]) for a, nb in PK_A_BLOCKS]
                           + [combine(first[0:up8(a1)], second[b:b + 1]) for b, _, a1 in PK_B_BLOCKS], axis=0)


def _candidate_valid_and_flat():
    up8 = lambda n: -(-n // 8) * 8
    valid, flat = [], []
    for a, nb in PK_A_BLOCKS:
        r = lax.broadcasted_iota(jnp.int32, (up8(nb), PK_TT), 0)
        valid.append(r < nb)
        flat.append(a * PEER_TOPK + r)
    for b, a0, a1 in PK_B_BLOCKS:
        r = lax.broadcasted_iota(jnp.int32, (up8(a1), PK_TT), 0)
        valid.append((r >= a0) & (r < a1))
        flat.append(r * PEER_TOPK + b)
    valid = jnp.concatenate(valid, axis=0)
    return valid, jnp.where(valid, jnp.concatenate(flat, axis=0).astype(F32), 1e9)


def _peer_topk_kernel(q_ref, ka_ref, eid_ref, w_ref):
    key_row = lax.broadcasted_iota(jnp.int32, (N_KEYS, PK_TT), 0).astype(F32)
    valid, flat = _candidate_valid_and_flat()

    def heads(i, carry):
        scores = []
        for j in range(2 * PK_HEADS_PER_TRIP):
            q = q_ref[2 * PK_HEADS_PER_TRIP * i + j]
            q_hi = _trunc_bf16(q)
            qa = jnp.concatenate([q_hi.astype(BF16), (q - q_hi).astype(BF16), q_hi.astype(BF16)], axis=1)
            scores.append(lax.dot_general(ka_ref[2 * PK_HEADS_PER_TRIP * i + j], qa, (((1,), (1,)), ((), ())),
                                          preferred_element_type=F32))
        sub_s, sub_i = _extract_top([(s, None) for s in scores], key_row, PEER_TOPK)
        cands = []
        for j in range(PK_HEADS_PER_TRIP):
            s1, s2, i1, i2 = sub_s[2 * j], sub_s[2 * j + 1], sub_i[2 * j], sub_i[2 * j + 1]
            cands.append((jnp.where(valid, _candidate_grid(s1, s2, lambda x, y: x + y), NEG_HUGE),
                          _candidate_grid(i1, i2, lambda x, y: x * N_KEYS + y)))
        top_s, top_e = _extract_top(cands, flat, PEER_TOPK)
        for j in range(PK_HEADS_PER_TRIP):
            e = jnp.exp(top_s[j] - top_s[j][0:1])
            w_ref[PK_HEADS_PER_TRIP * i + j] = e / jnp.sum(e, axis=0, keepdims=True)
            eid_ref[PK_HEADS_PER_TRIP * i + j] = top_e[j].astype(jnp.int32) * ROW_SUB
        return carry

    lax.fori_loop(0, PEER_HEADS // PK_HEADS_PER_TRIP, heads, 0)


def peer_topk(qt, subkeys):
    hp, t, d = qt.shape
    k_hi, k_lo = _split2(subkeys.reshape(hp, N_KEYS, d))
    ka = jnp.concatenate([k_hi, k_hi, k_lo], axis=2)
    eid, w = pl.pallas_call(
        _peer_topk_kernel,
        grid=(t // PK_TT,),
        in_specs=[pl.BlockSpec((hp, PK_TT, d), lambda i: (0, i, 0)),
                  pl.BlockSpec((hp, N_KEYS, 3 * d), lambda i: (0, 0, 0))],
        out_specs=[pl.BlockSpec((PEER_HEADS, PEER_TOPK, PK_TT), lambda i: (0, 0, i)),
                   pl.BlockSpec((PEER_HEADS, PEER_TOPK, PK_TT), lambda i: (0, 0, i))],
        out_shape=[jax.ShapeDtypeStruct((PEER_HEADS, PEER_TOPK, t), jnp.int32),
                   jax.ShapeDtypeStruct((PEER_HEADS, PEER_TOPK, t), F32)],
        name="peer_topk",
    )(qt, ka)
    to_slots = lambda a: a.transpose(2, 0, 1).reshape(t, PEER_SLOTS)
    return to_slots(eid), to_slots(w)


def split_columns(t, sizes):
    offs = np.cumsum((0,) + tuple(sizes))
    return [t[..., int(a):int(b)] for a, b in zip(offs[:-1], offs[1:])]


def rope_tables(pos):
    inv = jnp.power(ROPE_THETA, -jnp.arange(0, ROPE_DIM, 2, dtype=jnp.float32) / ROPE_DIM)
    ang = pos[:, None] * inv[None, :]
    return jnp.cos(ang), jnp.sin(ang)


def apply_partial_rope(x, cos, sin):
    half = ROPE_DIM // 2
    xr = x[..., :ROPE_DIM].astype(jnp.float32)
    x1, x2 = xr[..., :half], xr[..., half:]
    c, s = cos[:, None, :], sin[:, None, :]
    rot = jnp.concatenate([x1 * c - x2 * s, x1 * s + x2 * c], axis=-1)
    return jnp.concatenate([rot.astype(x.dtype), x[..., ROPE_DIM:]], axis=-1)


def _logits(ks, qts):
    return [jnp.dot(k, qt, preferred_element_type=F32) for k, qt in zip(ks, qts)]


def _flash_update(ss, vts, mask, carries):
    stats, ps = [], []
    for s, (m, l, _) in zip(ss, carries):
        s_vis = s if mask is None else jnp.where(mask, s, NEG_INF)
        m_new = jnp.maximum(m, jnp.max(s_vis, axis=0, keepdims=True))
        alpha = jnp.exp(m - m_new)
        p = jnp.exp(s - m_new)
        if mask is not None:
            p = jnp.where(mask, p, 0.0)
        stats.append((m_new, alpha, alpha * l + jnp.sum(p, axis=0, keepdims=True)))
        ps.append(p.astype(BF16))
    return tuple((m_new, l, alpha * acc + jnp.dot(vt, p, preferred_element_type=F32))
                 for (m_new, alpha, l), p, vt, (_, _, acc) in zip(stats, ps, vts, carries))


def _causal_flash(key_tile, value_tile, qts, n_full, last_mask, n):
    def body(kt, carries):
        return _flash_update(_logits(key_tile(kt), qts), value_tile(kt), None, carries)

    carries = lax.fori_loop(0, n_full, body, tuple(_flash_init(n) for _ in qts))
    return _flash_update(_logits(key_tile(n_full), qts), value_tile(n_full), last_mask, carries)


def _flash_init(n):
    return (jnp.full((1, n), NEG_INF, F32), jnp.zeros((1, n), F32), jnp.zeros((HEAD_DIM, n), F32))


FOX_TQ = 256
FOX_KV = 512
FOX_DK = 256


FOX_HP = 4


def _fox_kernel(qt_ref, k_ref, vt_ref, o_ref):
    qi = pl.program_id(1)
    q0 = qi * FOX_TQ
    qts = [qt_ref[h] for h in range(FOX_HP)]
    qpos = q0 + lax.broadcasted_iota(jnp.int32, (1, FOX_TQ), 1)
    krow = lax.broadcasted_iota(jnp.int32, (FOX_KV, FOX_TQ), 0)

    n_full = q0 // FOX_KV
    causal = (n_full * FOX_KV + krow) <= qpos
    carry = _causal_flash(lambda kt: [k_ref[h, kt] for h in range(FOX_HP)],
                          lambda kt: [vt_ref[h, kt] for h in range(FOX_HP)], qts, n_full, causal, FOX_TQ)
    for h in range(FOX_HP):
        _, l, acc = carry[h]
        o_ref[h] = acc / l


def _trunc_bf16(x):
    bits = lax.bitcast_convert_type(x, jnp.uint32) & jnp.uint32(0xFFFF0000)
    return lax.bitcast_convert_type(bits, F32)


def _split3(c):
    c1 = _trunc_bf16(c)
    r = c - c1
    c2 = _trunc_bf16(r)
    return c1.astype(BF16), c2.astype(BF16), (r - c2).astype(BF16)


def _split2(x):
    hi = _trunc_bf16(x)
    return hi.astype(BF16), (x - hi).astype(BF16)


def _aug_q(q):
    hi, lo = _split2(q)
    return jnp.concatenate([hi, hi, lo], axis=-1)


def _aug_k(k):
    hi, lo = _split2(k)
    return jnp.concatenate([hi, lo, hi], axis=-1)


def fox_attention(q, k, v, log_f):
    B_, S_, H, Dh = q.shape
    KV_TILE = FOX_KV
    n_qt, n_kt = S_ // FOX_TQ, S_ // KV_TILE
    c = jnp.cumsum(log_f, axis=1)
    c1, c2, c3 = _split3(c)
    one = jnp.ones_like(c1)
    pad = jnp.zeros((B_, S_, H, FOX_DK - 3 * Dh - 6), BF16)
    col = lambda t: t[..., None]
    qa = jnp.concatenate([_aug_q(q * (Dh ** -0.5)), col(one), col(one), col(one),
                          col(c1), col(c2), col(c3), pad], axis=-1)
    ka = jnp.concatenate([_aug_k(k), col(-c1), col(-c2), col(-c3),
                          col(one), col(one), col(one), pad], axis=-1)
    HP, HG = FOX_HP, H // FOX_HP
    qt = qa.reshape(B_, n_qt, FOX_TQ, HG, HP, FOX_DK).transpose(0, 3, 1, 4, 5, 2).reshape(B_ * HG * n_qt, HP, FOX_DK, FOX_TQ)
    kk = ka.reshape(B_, n_kt, KV_TILE, HG, HP, FOX_DK).transpose(0, 3, 4, 1, 2, 5).reshape(B_ * HG, HP, n_kt, KV_TILE, FOX_DK)
    vt = v.astype(BF16).reshape(B_, n_kt, KV_TILE, HG, HP, Dh).transpose(0, 3, 4, 1, 5, 2).reshape(B_ * HG, HP, n_kt, Dh, KV_TILE)
    out = pl.pallas_call(
        _fox_kernel,
        grid=(B_ * HG, n_qt),
        in_specs=[pl.BlockSpec((None, HP, FOX_DK, FOX_TQ), lambda b, i: (b * n_qt + i, 0, 0, 0)),
                  pl.BlockSpec((None, HP, n_kt, KV_TILE, FOX_DK), lambda b, i: (b, 0, 0, 0, 0)),
                  pl.BlockSpec((None, HP, n_kt, Dh, KV_TILE), lambda b, i: (b, 0, 0, 0, 0))],
        out_specs=pl.BlockSpec((None, HP, Dh, FOX_TQ), lambda b, i: (b * n_qt + i, 0, 0, 0)),
        out_shape=jax.ShapeDtypeStruct((B_ * HG * n_qt, HP, Dh, FOX_TQ), F32),
        name="fox_attention",
    )(qt, kk, vt)
    return out.reshape(B_, HG, n_qt, HP, Dh, FOX_TQ).transpose(0, 2, 5, 1, 3, 4).reshape(B_, S_, H * Dh)


NSA_TQ = 128
NSA_N = NSA_HPG * NSA_TQ
N_CMP_PAD = 256
N_SLC = 64
NSA_DK = 3 * HEAD_DIM
NSA_KV = 512
WIN_TILES = WINDOW // NSA_TQ + 1
WIN_KEYS = WIN_TILES * NSA_TQ


def _nsa_kernel(qt_ref, kc_ref, vct_ref, ks_ref, vst_ref, kw_ref, vwt_ref, gate_ref, ovl_ref, o_ref):
    qi = pl.program_id(1)
    q0 = qi * NSA_TQ
    qt = qt_ref[...]
    lane = lax.broadcasted_iota(jnp.int32, (1, NSA_N), 1)
    qpos = q0 + (lane & (NSA_TQ - 1))

    hn = NSA_N // 2
    qts = (qt[:, :hn], qt[:, hn:])
    qpos_h = qpos[:, :hn]

    w0 = jnp.maximum(qi - WINDOW // NSA_TQ, 0)
    k_w = jnp.concatenate([kw_ref[w0 + i] for i in range(WIN_TILES)], axis=0)
    vt_w = jnp.concatenate([vwt_ref[w0 + i] for i in range(WIN_TILES)], axis=1)
    s = jnp.dot(kc_ref[...], qt, preferred_element_type=F32)
    s_w = [jnp.dot(k_w, qts[h], preferred_element_type=F32) for h in range(2)]

    cmp_end = lax.broadcasted_iota(jnp.int32, (N_CMP_PAD, NSA_N), 0) * CMP_STRIDE + (CMP_LEN - 1)
    mc = cmp_end <= qpos
    m = jnp.max(jnp.where(mc, s, NEG_INF), axis=0, keepdims=True)
    p = jnp.where(mc, jnp.exp(s - m), 0.0)
    l = jnp.sum(p, axis=0, keepdims=True)
    pc = p * jnp.where(l > 0.0, 1.0 / l, 0.0)
    o_cmp = jnp.dot(vct_ref[...], pc.astype(BF16), preferred_element_type=F32)
    pcs = (pc[:, 0:NSA_TQ] + pc[:, NSA_TQ:2 * NSA_TQ]) + (pc[:, 2 * NSA_TQ:3 * NSA_TQ] + pc[:, 3 * NSA_TQ:4 * NSA_TQ])
    pcs_hi = pcs.astype(BF16)
    pcs_lo = (pcs - pcs_hi.astype(F32)).astype(BF16)
    ovl = ovl_ref[...]
    imp = jnp.dot(ovl, pcs_hi, preferred_element_type=F32) + jnp.dot(ovl, pcs_lo, preferred_element_type=F32)

    rel = qpos_h - (w0 * NSA_TQ + lax.broadcasted_iota(jnp.int32, (WIN_KEYS, hn), 0))
    in_win = (rel >= 0) & (rel < WINDOW)
    outs_w = []
    for h in range(2):
        m = jnp.max(jnp.where(in_win, s_w[h], NEG_INF), axis=0, keepdims=True)
        p = jnp.where(in_win, jnp.exp(s_w[h] - m), 0.0)
        l = jnp.sum(p, axis=0, keepdims=True)
        outs_w.append(jnp.dot(vt_w, p.astype(BF16), preferred_element_type=F32) / l)
    acc_w = jnp.concatenate(outs_w, axis=1)

    blk = lax.broadcasted_iota(jnp.int32, (N_SLC, NSA_TQ), 0)
    q_blk = (q0 + lax.broadcasted_iota(jnp.int32, (N_SLC, NSA_TQ), 1)) >> 6
    forced = (blk == 0) | (blk == q_blk) | (blk == q_blk - 1)
    score = jnp.where(forced, FORCED_SCORE, jnp.where(blk <= q_blk, imp, -1.0))
    rank = jnp.zeros((N_SLC, NSA_TQ), jnp.int32)
    for i in range(N_SLC):
        row = score[i:i + 1, :]
        beats = (row > score) | ((row == score) & (blk > i))
        rank = rank + beats.astype(jnp.int32)
    bias = jnp.where(rank < SLC_TOP, 0.0, NEG_INF).astype(BF16)
    bias = jnp.concatenate([bias] * (NSA_HPG // 2), axis=1)
    qts_s = [jnp.concatenate([qh, bias], axis=0) for qh in qts]
    krow = lax.broadcasted_iota(jnp.int32, (NSA_KV, hn), 0)

    n_full = q0 // NSA_KV
    causal = (n_full * NSA_KV + krow) <= qpos_h
    slc = _causal_flash(lambda kt: [ks_ref[kt]] * 2, lambda kt: [vst_ref[kt]] * 2, qts_s, n_full, causal, hn)
    acc_s = jnp.concatenate([slc[0][2] / slc[0][1], slc[1][2] / slc[1][1]], axis=1)

    g = jax.nn.sigmoid(gate_ref[...])
    o_ref[...] = g[0:1] * o_cmp + g[1:2] * acc_s + g[2:3] * acc_w


def nsa_attention(q, k_cmp, v_cmp, k_slc, v_slc, k_win, v_win, gate_logit):
    B_, S_, H, Dh = q.shape
    G = NSA_GROUPS
    n_qt = S_ // NSA_TQ
    n_cmp = k_cmp.shape[1]
    cs = np.arange(N_CMP_PAD)[None, :] * CMP_STRIDE
    ss = np.arange(N_SLC)[:, None] * SLC_LEN
    ov = np.clip(np.minimum(cs + CMP_LEN, ss + SLC_LEN) - np.maximum(cs, ss), 0, None) / CMP_LEN
    ov[:, n_cmp:] = 0.0
    ovl = jnp.asarray(ov, dtype=BF16)
    qs = _aug_q(q * (Dh ** -0.5)).reshape(B_, n_qt, NSA_TQ, G, NSA_HPG, NSA_DK)
    qt = qs.transpose(0, 3, 1, 5, 4, 2).reshape(B_ * G * n_qt, NSA_DK, NSA_N)
    gt = gate_logit.astype(F32).reshape(B_, n_qt, NSA_TQ, G, NSA_HPG, N_NSA_BRANCH)
    gt = gt.transpose(0, 3, 1, 5, 4, 2).reshape(B_ * G * n_qt, N_NSA_BRANCH, NSA_N)
    tiles = lambda t, kv: t.reshape(B_, S_ // kv, kv, G, -1).transpose(0, 3, 1, 2, 4).reshape(B_ * G, S_ // kv, kv, t.shape[-1])
    keys = lambda t, kv: tiles(_aug_k(t), kv)
    block_onehot = (jnp.arange(S_)[:, None] // SLC_LEN == jnp.arange(N_SLC)[None, :]).astype(BF16)
    k_slc_aug = jnp.concatenate([_aug_k(k_slc), jnp.broadcast_to(block_onehot[None, :, None, :], (B_, S_, G, N_SLC))], axis=-1)
    vals = lambda t, kv: t.astype(BF16).reshape(B_, S_ // kv, kv, G, Dh).transpose(0, 3, 1, 4, 2).reshape(B_ * G, S_ // kv, Dh, kv)
    padc = ((0, 0), (0, N_CMP_PAD - n_cmp), (0, 0), (0, 0))
    kc = _aug_k(jnp.pad(k_cmp, padc)).transpose(0, 2, 1, 3).reshape(B_ * G, N_CMP_PAD, NSA_DK)
    vct = jnp.pad(v_cmp, padc).astype(BF16).transpose(0, 2, 3, 1).reshape(B_ * G, Dh, N_CMP_PAD)
    per_bg = lambda *blk: pl.BlockSpec((None,) + blk, lambda b, i: (b,) + (0,) * len(blk))
    per_tile = lambda *blk: pl.BlockSpec((None,) + blk, lambda b, i: (b * n_qt + i,) + (0,) * len(blk))
    out = pl.pallas_call(
        _nsa_kernel,
        grid=(B_ * G, n_qt),
        in_specs=[per_tile(NSA_DK, NSA_N),
                  per_bg(N_CMP_PAD, NSA_DK), per_bg(Dh, N_CMP_PAD),
                  per_bg(S_ // NSA_KV, NSA_KV, NSA_DK + N_SLC), per_bg(S_ // NSA_KV, Dh, NSA_KV),
                  per_bg(n_qt, NSA_TQ, NSA_DK), per_bg(n_qt, Dh, NSA_TQ),
                  per_tile(N_NSA_BRANCH, NSA_N),
                  pl.BlockSpec((N_SLC, N_CMP_PAD), lambda b, i: (0, 0))],
        out_specs=per_tile(Dh, NSA_N),
        out_shape=jax.ShapeDtypeStruct((B_ * G * n_qt, Dh, NSA_N), F32),
        name="nsa_attention",
    )(qt, kc, vct, tiles(k_slc_aug, NSA_KV), vals(v_slc, NSA_KV), keys(k_win, NSA_TQ), vals(v_win, NSA_TQ), gt, ovl)
    return out.reshape(B_, G, n_qt, Dh, NSA_HPG, NSA_TQ).transpose(0, 2, 5, 1, 4, 3).reshape(B_, S_, H * Dh)


DENSE_TM = 256
SEC_QKV = 4 * BRANCH_WIDTH
SEC_KV = 6 * KV_WIDTH
SEC_MERGE = N_BRANCH * D_MODEL
SEC_SMALL = 256
IN_SECTIONS = (SEC_QKV, SEC_KV, SEC_MERGE, SEC_SMALL)


def _rms(x, g):
    return x * lax.rsqrt(jnp.mean(x * x, axis=-1, keepdims=True) + RMS_EPS) * g


def _row_spec(width):
    return pl.BlockSpec((DENSE_TM, width), lambda i: (i, 0))


def _whole_spec(shape):
    return pl.BlockSpec(shape, lambda i: (0,) * len(shape))


def _in_proj_kernel(x_ref, g_ref, w_ref, *o_refs):
    xn = _rms(x_ref[...], g_ref[...]).astype(BF16)
    off = 0
    for o_ref, width in zip(o_refs, IN_SECTIONS):
        o_ref[...] = jnp.dot(xn, w_ref[:, off:off + width], preferred_element_type=F32)
        off += width


def in_projection(x, g, w_in):
    t, d = x.shape
    fq, fk, fv, fl, nq, kc, vc, ksl, vsl, kwn, vwn, gl, ml = split_columns(w_in, IN_SPLITS)
    pad = lambda w: jnp.pad(w, ((0, 0), (0, 128 - w.shape[1])))
    w = jnp.concatenate([fq, fk, fv, nq, kc, vc, ksl, vsl, kwn, vwn, ml, pad(fl), pad(gl)], axis=1).astype(BF16)
    return pl.pallas_call(
        _in_proj_kernel,
        grid=(t // DENSE_TM,),
        in_specs=[_row_spec(d), _whole_spec((1, d)), _whole_spec(w.shape)],
        out_specs=[_row_spec(s) for s in IN_SECTIONS],
        out_shape=[jax.ShapeDtypeStruct((t, s), F32) for s in IN_SECTIONS],
        name="in_projection",
    )(x, g.reshape(1, d), w)


def _merge_kernel(x_ref, yf_ref, yn_ref, ml_ref, wb_ref, wo_ref, o_ref):
    g = jax.nn.sigmoid(ml_ref[...])
    up_f = jnp.dot(yf_ref[...].astype(BF16), wb_ref[0], preferred_element_type=F32)
    up_n = jnp.dot(yn_ref[...].astype(BF16), wb_ref[1], preferred_element_type=F32)
    merged = g[:, :D_MODEL] * up_f + g[:, D_MODEL:] * up_n
    o_ref[...] = x_ref[...] + jnp.dot(merged.astype(BF16), wo_ref[...], preferred_element_type=F32)


def merge_out(x, y_fox, y_nsa, merge_logit, w_branch, w_out):
    t, d = x.shape
    return pl.pallas_call(
        _merge_kernel,
        grid=(t // DENSE_TM,),
        in_specs=[_row_spec(d), _row_spec(BRANCH_WIDTH), _row_spec(BRANCH_WIDTH), _row_spec(SEC_MERGE),
                  _whole_spec(w_branch.shape), _whole_spec(w_out.shape)],
        out_specs=_row_spec(d),
        out_shape=jax.ShapeDtypeStruct((t, d), F32),
        name="merge_out",
    )(x, y_fox, y_nsa, merge_logit, w_branch.astype(BF16), w_out.astype(BF16))


def _peer_q_kernel(x_ref, g_ref, w_ref, hn_ref, q_ref):
    hn = _rms(x_ref[...], g_ref[...])
    hn_ref[...] = hn
    q = jnp.dot(hn.astype(BF16), w_ref[...], preferred_element_type=F32)
    d = PEER_QDIM // 2
    for j in range(2 * PEER_HEADS):
        q_ref[j] = q[:, j * d:(j + 1) * d]


def peer_query(x, g, wq):
    t, d = x.shape
    hp, dq = 2 * PEER_HEADS, PEER_QDIM // 2
    return pl.pallas_call(
        _peer_q_kernel,
        grid=(t // DENSE_TM,),
        in_specs=[_row_spec(d), _whole_spec((1, d)), _whole_spec(wq.shape)],
        out_specs=[_row_spec(d), pl.BlockSpec((hp, DENSE_TM, dq), lambda i: (0, i, 0))],
        out_shape=[jax.ShapeDtypeStruct((t, d), F32), jax.ShapeDtypeStruct((hp, t, dq), F32)],
        name="peer_query",
    )(x, g.reshape(1, d), wq.astype(BF16))


def _final_kernel(h_ref, p_ref, g_ref, o_ref):
    o_ref[...] = _rms(h_ref[...] + p_ref[...], g_ref[...])


def final_norm(h, p, g):
    t, d = h.shape
    return pl.pallas_call(
        _final_kernel,
        grid=(t // DENSE_TM,),
        in_specs=[_row_spec(d), _row_spec(d), _whole_spec((1, d))],
        out_specs=_row_spec(d),
        out_shape=jax.ShapeDtypeStruct((t, d), F32),
        name="final_norm",
    )(h, p, g.reshape(1, d))


N_CMP_ROWS = 256


def _compress_kernel(r_ref, pos_ref, w1_ref, w2_ref, o_ref):
    r = r_ref[...]
    nxt = jnp.concatenate([r[1:], jnp.zeros((1, r.shape[1]), F32)], axis=0)
    half = r.shape[1]
    hid = (jnp.dot((r + pos_ref[0:1, :]).astype(BF16), w1_ref[:half], preferred_element_type=F32)
           + jnp.dot((nxt + pos_ref[1:2, :]).astype(BF16), w1_ref[half:], preferred_element_type=F32))
    o_ref[...] = jnp.dot(_gelu(hid).astype(BF16), w2_ref[...], preferred_element_type=F32)


def compress_kv_pallas(kv, cmp_pos, w1, w2):
    _, B_, S_, G, Dh = kv.shape
    n_cmp = (S_ - CMP_LEN) // CMP_STRIDE + 1
    width = CMP_STRIDE * Dh
    rows = kv.transpose(0, 1, 3, 2, 4).reshape(2, B_ * G, S_ // CMP_STRIDE, width)
    out = pl.pallas_call(
        _compress_kernel,
        grid=(2, B_ * G),
        in_specs=[pl.BlockSpec((None, None, N_CMP_ROWS, width), lambda j, b: (j, b, 0, 0)),
                  pl.BlockSpec((None, 2, width), lambda j, b: (j, 0, 0)),
                  pl.BlockSpec((None, CMP_LEN * Dh, CMP_HIDDEN), lambda j, b: (j, 0, 0)),
                  pl.BlockSpec((None, CMP_HIDDEN, Dh), lambda j, b: (j, 0, 0))],
        out_specs=pl.BlockSpec((None, None, N_CMP_ROWS, Dh), lambda j, b: (j, b, 0, 0)),
        out_shape=jax.ShapeDtypeStruct((2, B_ * G, N_CMP_ROWS, Dh), F32),
        name="compress_kv",
    )(rows, cmp_pos.reshape(2, 2, width), w1.astype(BF16), w2.astype(BF16))
    return out[:, :, :n_cmp].reshape(2, B_, G, n_cmp, Dh).transpose(0, 1, 3, 2, 4)


def hybrid_mixer(x, norm_g, w_in, fox_f_bias, cmp_pos, cmp_w1, cmp_w2, w_branch, w_out):
    B_, S_, D = x.shape
    xf = x.reshape(B_ * S_, D)
    qkv, kvs, merge_logit, small = in_projection(xf, norm_g, w_in)
    heads = lambda t, n: t.reshape(B_, S_, n, HEAD_DIM)
    fq, fk, fv, nq = (qkv[:, i * BRANCH_WIDTH:(i + 1) * BRANCH_WIDTH] for i in range(4))
    kc, vc, ksl, vsl, kwn, vwn = (heads(kvs[:, i * KV_WIDTH:(i + 1) * KV_WIDTH], NSA_GROUPS) for i in range(6))
    f_logit = small[:, :FOX_HEADS].reshape(B_, S_, FOX_HEADS)
    gate_logit = small[:, 128:128 + NSA_HEADS * N_NSA_BRANCH].reshape(B_, S_, NSA_HEADS, N_NSA_BRANCH)
    log_f = jax.nn.log_sigmoid(f_logit + fox_f_bias.astype(F32))
    y_fox = fox_attention(heads(fq, FOX_HEADS), heads(fk, FOX_HEADS), heads(fv, FOX_HEADS), log_f)
    cos, sin = rope_tables(jnp.arange(S_, dtype=F32))
    q_nsa = apply_partial_rope(heads(nq, NSA_HEADS), cos, sin)
    k_slc = apply_partial_rope(ksl, cos, sin)
    k_win = apply_partial_rope(kwn, cos, sin)
    cmp = compress_kv_pallas(jnp.stack([kc, vc]), cmp_pos, cmp_w1, cmp_w2)
    n_cmp = cmp.shape[2]
    cmp_end = jnp.arange(n_cmp, dtype=F32) * CMP_STRIDE + (CMP_LEN - 1)
    k_cmp = apply_partial_rope(cmp[0], *rope_tables(cmp_end))
    y_nsa = nsa_attention(q_nsa, k_cmp, cmp[1], k_slc, vsl, k_win, vwn, gate_logit)
    return merge_out(xf, y_fox.reshape(B_ * S_, -1), y_nsa.reshape(B_ * S_, -1), merge_logit, w_branch, w_out)


def peer_ffn(h, norm_g, wq, subkeys, u, v):
    hn, q = peer_query(h, norm_g, wq)
    idx2, w = peer_topk(q, subkeys)
    coef = peer_u(idx2, hn, w, pack_table(u))
    return peer_v(idx2, coef, pack_table(v))


def kernel(x, norm_mix, w_in, fox_f_bias, nsa_cmp_pos, nsa_cmp_w1, nsa_cmp_w2, w_branch, w_out,
           norm_ffn, peer_wq, peer_subkeys, peer_u, peer_v, norm_final):
    B_, S_, D = x.shape
    assert norm_mix.shape[0] == 1, "single-layer trunk"
    h = hybrid_mixer(x, norm_mix[0], w_in[0], fox_f_bias[0], nsa_cmp_pos[0], nsa_cmp_w1[0], nsa_cmp_w2[0],
                     w_branch[0], w_out[0])
    p = peer_ffn(h, norm_ffn[0], peer_wq[0], peer_subkeys[0], peer_u[0], peer_v[0])
    return final_norm(h, p, norm_final).reshape(B_, S_, D)
```

```python
import jax
import jax.numpy as jnp
import numpy as np
from jax import lax
from jax.experimental import pallas as pl
from jax.experimental.pallas import tpu as pltpu

D_MODEL = 1024
HEAD_DIM = 64
FOX_HEADS = 8
NSA_HEADS = 8
NSA_GROUPS = 2
NSA_HPG = NSA_HEADS // NSA_GROUPS
BRANCH_WIDTH = 512
N_BRANCH = 2
N_NSA_BRANCH = 3
ROPE_DIM = HEAD_DIM // 4
ROPE_THETA = 500000.0
CMP_LEN = 32
CMP_STRIDE = 16
CMP_HIDDEN = 2 * HEAD_DIM
SLC_LEN = 64
SLC_TOP = 16
WINDOW = 512
FORCED_SCORE = 1e9
NEG_INF = -1e30
PEER_HEADS = 8
N_KEYS = 128
PEER_QDIM = 256
PEER_TOPK = 16
RMS_EPS = 1e-6
KV_WIDTH = NSA_GROUPS * HEAD_DIM
IN_SPLITS = (BRANCH_WIDTH, BRANCH_WIDTH, BRANCH_WIDTH, FOX_HEADS, BRANCH_WIDTH,
             KV_WIDTH, KV_WIDTH, KV_WIDTH, KV_WIDTH, KV_WIDTH, KV_WIDTH,
             NSA_HEADS * N_NSA_BRANCH, N_BRANCH * D_MODEL)


BF16 = jnp.bfloat16
F32 = jnp.float32

PEER_SLOTS = PEER_HEADS * PEER_TOPK
PEER_TT = 64
PEER_CHUNK_SLOTS = 32
PEER_U_UNROLL = 4
PEER_V_UNROLL = 8
ROW_SUB = D_MODEL // 2 // 128
VMEM_LIMIT_PEER = N_KEYS * N_KEYS * D_MODEL * 2 + 16 * 1024 * 1024


def pack_table(tab):
    n, d = tab.shape
    return pl.pallas_call(
        _pack_kernel,
        grid=(n // PACK_TM,),
        in_specs=[pl.BlockSpec((PACK_TM, d), lambda i: (i, 0))],
        out_specs=pl.BlockSpec((PACK_TM * ROW_SUB, 128), lambda i: (i, 0)),
        out_shape=jax.ShapeDtypeStruct((n * ROW_SUB, 128), jnp.uint32),
        name="pack_table",
    )(tab)


PACK_TM = 256


def _pack_kernel(x_ref, o_ref):
    bits = lax.bitcast_convert_type(x_ref[...].astype(jnp.bfloat16).astype(jnp.float32), jnp.uint32)
    for r in range(ROW_SUB):
        even = bits[:, (2 * r) * 128:(2 * r + 1) * 128]
        odd = bits[:, (2 * r + 1) * 128:(2 * r + 2) * 128]
        o_ref[pl.ds(r, PACK_TM, stride=ROW_SUB), :] = (even >> 16) | (odd & jnp.uint32(0xFFFF0000))


def _expert_slab(tab_ref, row):
    return tab_ref[pl.ds(pl.multiple_of(row, ROW_SUB), ROW_SUB), :]


def _unpack(w):
    lo = lax.bitcast_convert_type(w << 16, jnp.float32)
    hi = lax.bitcast_convert_type(w & jnp.uint32(0xFFFF0000), jnp.float32)
    return lo, hi


def _gelu(x):
    return 0.5 * x * (1.0 + lax.erf(x * (2.0 ** -0.5)))


def _peer_u_kernel(idx_a_ref, idx_b_ref, x_ref, w_ref, tab_ref, out_ref):
    lane = lax.broadcasted_iota(jnp.int32, (2 * ROW_SUB, 128), 1)
    lower = lax.broadcasted_iota(jnp.int32, (2 * ROW_SUB, 128), 0) >= ROW_SUB
    pair_lane = jnp.where(lower, lane - PEER_SLOTS // 2, lane)

    def token(t):
        row = x_ref[pl.ds(t, 1), :]
        chunks = [row[:, c * 128:(c + 1) * 128] for c in range(2 * ROW_SUB)]
        xl = jnp.concatenate(chunks[0::2] * 2, axis=0)
        xh = jnp.concatenate(chunks[1::2] * 2, axis=0)
        acc = jnp.zeros((2 * ROW_SUB, 128), jnp.float32)
        base = t * (PEER_SLOTS // 2)
        for s in range(PEER_SLOTS // 2):
            w = jnp.concatenate([_expert_slab(tab_ref, idx_a_ref[base + s]),
                                 _expert_slab(tab_ref, idx_b_ref[base + s])], axis=0)
            lo, hi = _unpack(w)
            part = jnp.sum(lo * xl + hi * xh, axis=1, keepdims=True)
            acc = jnp.where(pair_lane == s, part, acc)
        out_ref[pl.ds(t, 1), :] = w_ref[pl.ds(t, 1), :] * _gelu(jnp.sum(acc, axis=0, keepdims=True))

    def trip(i, carry):
        for j in range(PEER_U_UNROLL):
            token(i * PEER_U_UNROLL + j)
        return carry

    lax.fori_loop(0, PEER_TT // PEER_U_UNROLL, trip, 0)


N_CHUNK = 2 * ROW_SUB


def _peer_v_kernel(idx_a_ref, idx_b_ref, coef_ref, expand_ref, tab_ref, out_ref, ce_hi_ref, ce_lo_ref):
    half_slots = PEER_SLOTS // 2
    width = PEER_SLOTS * N_CHUNK
    coef = coef_ref[...]
    c_hi = _trunc_bf16(coef)
    expand = expand_ref[...]
    ce_hi_ref[...] = jnp.dot(c_hi.astype(BF16), expand, preferred_element_type=F32)
    ce_lo_ref[...] = jnp.dot((coef - c_hi).astype(BF16), expand, preferred_element_type=F32)
    own_chunk = ((lax.broadcasted_iota(jnp.int32, (N_CHUNK, width), 1) & (N_CHUNK - 1))
                 == lax.broadcasted_iota(jnp.int32, (N_CHUNK, width), 0))

    def token(t):
        base = t * half_slots
        slabs = [jnp.concatenate([_expert_slab(tab_ref, idx_a_ref[base + s]),
                                  _expert_slab(tab_ref, idx_b_ref[base + s])], axis=0) for s in range(half_slots)]
        gathered = pltpu.bitcast(jnp.concatenate(slabs, axis=0), BF16)
        sel = lambda ref: jnp.where(own_chunk, ref[pl.ds(t, 1), :], 0.0).astype(BF16)
        c = jnp.concatenate([sel(ce_hi_ref), sel(ce_lo_ref)], axis=0)
        o = jnp.dot(c, gathered, preferred_element_type=F32)
        o = o[:N_CHUNK] + o[N_CHUNK:]
        out_ref[pl.ds(t, 1), :] = jnp.concatenate([o[r:r + 1, :] for r in range(N_CHUNK)], axis=1)

    def trip(i, carry):
        for j in range(PEER_V_UNROLL):
            token(i * PEER_V_UNROLL + j)
        return carry

    lax.fori_loop(0, PEER_TT // PEER_V_UNROLL, trip, 0)


def _smem_spec():
    return pl.BlockSpec((PEER_TT * PEER_SLOTS // 2,), lambda i: (i,), memory_space=pltpu.SMEM)


def _slot_halves(a):
    return a[:, :PEER_SLOTS // 2].reshape(-1), a[:, PEER_SLOTS // 2:].reshape(-1)


def _table_spec(rows):
    return pl.BlockSpec((rows, 128), lambda i: (0, 0), pipeline_mode=pl.Buffered(1))


def peer_u(idx, x, w, tab):
    t = x.shape[0]
    return pl.pallas_call(
        _peer_u_kernel,
        grid=(t // PEER_TT,),
        in_specs=[_smem_spec(), _smem_spec(), pl.BlockSpec((PEER_TT, D_MODEL), lambda i: (i, 0)),
                  pl.BlockSpec((PEER_TT, PEER_SLOTS), lambda i: (i, 0)), _table_spec(tab.shape[0])],
        out_specs=pl.BlockSpec((PEER_TT, PEER_SLOTS), lambda i: (i, 0)),
        out_shape=jax.ShapeDtypeStruct((t, PEER_SLOTS), jnp.float32),
        compiler_params=pltpu.CompilerParams(vmem_limit_bytes=VMEM_LIMIT_PEER),
        name="peer_u",
    )(*_slot_halves(idx), x, w, tab)


def peer_v(idx, coef, tab):
    t = idx.shape[0]
    width = PEER_SLOTS * N_CHUNK
    half = PEER_SLOTS // 2
    col_slot = np.arange(width) // N_CHUNK
    col_slot = (col_slot // 2) + half * (col_slot % 2)
    expand = jnp.asarray(col_slot[None, :] == np.arange(PEER_SLOTS)[:, None], dtype=BF16)
    return pl.pallas_call(
        _peer_v_kernel,
        grid=(t // PEER_TT,),
        in_specs=[_smem_spec(), _smem_spec(), pl.BlockSpec((PEER_TT, PEER_SLOTS), lambda i: (i, 0)),
                  pl.BlockSpec((PEER_SLOTS, width), lambda i: (0, 0)), _table_spec(tab.shape[0])],
        out_specs=pl.BlockSpec((PEER_TT, D_MODEL), lambda i: (i, 0)),
        out_shape=jax.ShapeDtypeStruct((t, D_MODEL), jnp.float32),
        scratch_shapes=[pltpu.VMEM((PEER_TT, width), F32), pltpu.VMEM((PEER_TT, width), F32)],
        compiler_params=pltpu.CompilerParams(vmem_limit_bytes=VMEM_LIMIT_PEER),
        name="peer_v",
    )(*_slot_halves(idx), coef, expand, tab)


PK_TT = 256
PK_A_BLOCKS = ((0, 16), (1, 8), (2, 5), (3, 4))
PK_B_BLOCKS = ((0, 4, 16), (1, 4, 8), (2, 4, 5))
NEG_HUGE = -3.0e38
PK_HEADS_PER_TRIP = 2


def _extract_top(problems, flat, n):
    vals = [[] for _ in problems]
    picked = [[] for _ in problems]
    cur = [v for v, _ in problems]
    for _ in range(n):
        for i, (_, payload) in enumerate(problems):
            m = jnp.max(cur[i], axis=0, keepdims=True)
            pos = jnp.min(jnp.where(cur[i] == m, flat, 1e9), axis=0, keepdims=True)
            hit = flat == pos
            picked[i].append(pos if payload is None else jnp.sum(jnp.where(hit, payload, 0.0), axis=0, keepdims=True))
            cur[i] = jnp.where(hit, NEG_HUGE, cur[i])
            vals[i].append(m)
    return [jnp.concatenate(v, axis=0) for v in vals], [jnp.concatenate(p, axis=0) for p in picked]


def _candidate_grid(first, second, combine):
    up8 = lambda n: -(-n // 8) * 8
    return jnp.concatenate([combine(first[a:a + 1], second[0:up8(nb)]) for a, nb in PK_A_BLOCKS]
                           + [combine(first[0:up8(a1)], second[b:b + 1]) for b, _, a1 in PK_B_BLOCKS], axis=0)


def _candidate_valid_and_flat():
    up8 = lambda n: -(-n // 8) * 8
    valid, flat = [], []
    for a, nb in PK_A_BLOCKS:
        r = lax.broadcasted_iota(jnp.int32, (up8(nb), PK_TT), 0)
        valid.append(r < nb)
        flat.append(a * PEER_TOPK + r)
    for b, a0, a1 in PK_B_BLOCKS:
        r = lax.broadcasted_iota(jnp.int32, (up8(a1), PK_TT), 0)
        valid.append((r >= a0) & (r < a1))
        flat.append(r * PEER_TOPK + b)
    valid = jnp.concatenate(valid, axis=0)
    return valid, jnp.where(valid, jnp.concatenate(flat, axis=0).astype(F32), 1e9)


def _peer_topk_kernel(q_ref, ka_ref, eid_ref, w_ref):
    key_row = lax.broadcasted_iota(jnp.int32, (N_KEYS, PK_TT), 0).astype(F32)
    valid, flat = _candidate_valid_and_flat()

    def heads(i, carry):
        scores = []
        for j in range(2 * PK_HEADS_PER_TRIP):
            q = q_ref[2 * PK_HEADS_PER_TRIP * i + j]
            q_hi = _trunc_bf16(q)
            qa = jnp.concatenate([q_hi.astype(BF16), (q - q_hi).astype(BF16), q_hi.astype(BF16)], axis=1)
            scores.append(lax.dot_general(ka_ref[2 * PK_HEADS_PER_TRIP * i + j], qa, (((1,), (1,)), ((), ())),
                                          preferred_element_type=F32))
        sub_s, sub_i = _extract_top([(s, None) for s in scores], key_row, PEER_TOPK)
        cands = []
        for j in range(PK_HEADS_PER_TRIP):
            s1, s2, i1, i2 = sub_s[2 * j], sub_s[2 * j + 1], sub_i[2 * j], sub_i[2 * j + 1]
            cands.append((jnp.where(valid, _candidate_grid(s1, s2, lambda x, y: x + y), NEG_HUGE),
                          _candidate_grid(i1, i2, lambda x, y: x * N_KEYS + y)))
        top_s, top_e = _extract_top(cands, flat, PEER_TOPK)
        for j in range(PK_HEADS_PER_TRIP):
            e = jnp.exp(top_s[j] - top_s[j][0:1])
            w_ref[PK_HEADS_PER_TRIP * i + j] = e / jnp.sum(e, axis=0, keepdims=True)
            eid_ref[PK_HEADS_PER_TRIP * i + j] = top_e[j].astype(jnp.int32) * ROW_SUB
        return carry

    lax.fori_loop(0, PEER_HEADS // PK_HEADS_PER_TRIP, heads, 0)


def peer_topk(qt, subkeys):
    hp, t, d = qt.shape
    k_hi, k_lo = _split2(subkeys.reshape(hp, N_KEYS, d))
    ka = jnp.concatenate([k_hi, k_hi, k_lo], axis=2)
    eid, w = pl.pallas_call(
        _peer_topk_kernel,
        grid=(t // PK_TT,),
        in_specs=[pl.BlockSpec((hp, PK_TT, d), lambda i: (0, i, 0)),
                  pl.BlockSpec((hp, N_KEYS, 3 * d), lambda i: (0, 0, 0))],
        out_specs=[pl.BlockSpec((PEER_HEADS, PEER_TOPK, PK_TT), lambda i: (0, 0, i)),
                   pl.BlockSpec((PEER_HEADS, PEER_TOPK, PK_TT), lambda i: (0, 0, i))],
        out_shape=[jax.ShapeDtypeStruct((PEER_HEADS, PEER_TOPK, t), jnp.int32),
                   jax.ShapeDtypeStruct((PEER_HEADS, PEER_TOPK, t), F32)],
        name="peer_topk",
    )(qt, ka)
    to_slots = lambda a: a.transpose(2, 0, 1).reshape(t, PEER_SLOTS)
    return to_slots(eid), to_slots(w)


def split_columns(t, sizes):
    offs = np.cumsum((0,) + tuple(sizes))
    return [t[..., int(a):int(b)] for a, b in zip(offs[:-1], offs[1:])]


def rope_tables(pos):
    inv = jnp.power(ROPE_THETA, -jnp.arange(0, ROPE_DIM, 2, dtype=jnp.float32) / ROPE_DIM)
    ang = pos[:, None] * inv[None, :]
    return jnp.cos(ang), jnp.sin(ang)


def apply_partial_rope(x, cos, sin):
    half = ROPE_DIM // 2
    xr = x[..., :ROPE_DIM].astype(jnp.float32)
    x1, x2 = xr[..., :half], xr[..., half:]
    c, s = cos[:, None, :], sin[:, None, :]
    rot = jnp.concatenate([x1 * c - x2 * s, x1 * s + x2 * c], axis=-1)
    return jnp.concatenate([rot.astype(x.dtype), x[..., ROPE_DIM:]], axis=-1)


def _logits(ks, qts):
    return [jnp.dot(k, qt, preferred_element_type=F32) for k, qt in zip(ks, qts)]


def _flash_update(ss, vts, mask, carries):
    stats, ps = [], []
    for s, (m, l, _) in zip(ss, carries):
        s_vis = s if mask is None else jnp.where(mask, s, NEG_INF)
        m_new = jnp.maximum(m, jnp.max(s_vis, axis=0, keepdims=True))
        alpha = jnp.exp(m - m_new)
        p = jnp.exp(s - m_new)
        if mask is not None:
            p = jnp.where(mask, p, 0.0)
        stats.append((m_new, alpha, alpha * l + jnp.sum(p, axis=0, keepdims=True)))
        ps.append(p.astype(BF16))
    return tuple((m_new, l, alpha * acc + jnp.dot(vt, p, preferred_element_type=F32))
                 for (m_new, alpha, l), p, vt, (_, _, acc) in zip(stats, ps, vts, carries))


def _causal_flash(key_tile, value_tile, qts, n_full, last_mask, n):
    def body(kt, carries):
        return _flash_update(_logits(key_tile(kt), qts), value_tile(kt), None, carries)

    carries = lax.fori_loop(0, n_full, body, tuple(_flash_init(n) for _ in qts))
    return _flash_update(_logits(key_tile(n_full), qts), value_tile(n_full), last_mask, carries)


def _flash_init(n):
    return (jnp.full((1, n), NEG_INF, F32), jnp.zeros((1, n), F32), jnp.zeros((HEAD_DIM, n), F32))


FOX_TQ = 256
FOX_KV = 512
FOX_DK = 256


FOX_HP = 4


def _fox_kernel(qt_ref, k_ref, vt_ref, o_ref):
    qi = pl.program_id(1)
    q0 = qi * FOX_TQ
    qts = [qt_ref[h] for h in range(FOX_HP)]
    qpos = q0 + lax.broadcasted_iota(jnp.int32, (1, FOX_TQ), 1)
    krow = lax.broadcasted_iota(jnp.int32, (FOX_KV, FOX_TQ), 0)

    n_full = q0 // FOX_KV
    causal = (n_full * FOX_KV + krow) <= qpos
    carry = _causal_flash(lambda kt: [k_ref[h, kt] for h in range(FOX_HP)],
                          lambda kt: [vt_ref[h, kt] for h in range(FOX_HP)], qts, n_full, causal, FOX_TQ)
    for h in range(FOX_HP):
        _, l, acc = carry[h]
        o_ref[h] = acc / l


def _trunc_bf16(x):
    bits = lax.bitcast_convert_type(x, jnp.uint32) & jnp.uint32(0xFFFF0000)
    return lax.bitcast_convert_type(bits, F32)


def _split3(c):
    c1 = _trunc_bf16(c)
    r = c - c1
    c2 = _trunc_bf16(r)
    return c1.astype(BF16), c2.astype(BF16), (r - c2).astype(BF16)


def _split2(x):
    hi = _trunc_bf16(x)
    return hi.astype(BF16), (x - hi).astype(BF16)


def _aug_q(q):
    hi, lo = _split2(q)
    return jnp.concatenate([hi, hi, lo], axis=-1)


def _aug_k(k):
    hi, lo = _split2(k)
    return jnp.concatenate([hi, lo, hi], axis=-1)


def fox_attention(q, k, v, log_f):
    B_, S_, H, Dh = q.shape
    KV_TILE = FOX_KV
    n_qt, n_kt = S_ // FOX_TQ, S_ // KV_TILE
    c = jnp.cumsum(log_f, axis=1)
    c1, c2, c3 = _split3(c)
    one = jnp.ones_like(c1)
    pad = jnp.zeros((B_, S_, H, FOX_DK - 3 * Dh - 6), BF16)
    col = lambda t: t[..., None]
    qa = jnp.concatenate([_aug_q(q * (Dh ** -0.5)), col(one), col(one), col(one),
                          col(c1), col(c2), col(c3), pad], axis=-1)
    ka = jnp.concatenate([_aug_k(k), col(-c1), col(-c2), col(-c3),
                          col(one), col(one), col(one), pad], axis=-1)
    HP, HG = FOX_HP, H // FOX_HP
    qt = qa.reshape(B_, n_qt, FOX_TQ, HG, HP, FOX_DK).transpose(0, 3, 1, 4, 5, 2).reshape(B_ * HG * n_qt, HP, FOX_DK, FOX_TQ)
    kk = ka.reshape(B_, n_kt, KV_TILE, HG, HP, FOX_DK).transpose(0, 3, 4, 1, 2, 5).reshape(B_ * HG, HP, n_kt, KV_TILE, FOX_DK)
    vt = v.astype(BF16).reshape(B_, n_kt, KV_TILE, HG, HP, Dh).transpose(0, 3, 4, 1, 5, 2).reshape(B_ * HG, HP, n_kt, Dh, KV_TILE)
    out = pl.pallas_call(
        _fox_kernel,
        grid=(B_ * HG, n_qt),
        in_specs=[pl.BlockSpec((None, HP, FOX_DK, FOX_TQ), lambda b, i: (b * n_qt + i, 0, 0, 0)),
                  pl.BlockSpec((None, HP, n_kt, KV_TILE, FOX_DK), lambda b, i: (b, 0, 0, 0, 0)),
                  pl.BlockSpec((None, HP, n_kt, Dh, KV_TILE), lambda b, i: (b, 0, 0, 0, 0))],
        out_specs=pl.BlockSpec((None, HP, Dh, FOX_TQ), lambda b, i: (b * n_qt + i, 0, 0, 0)),
        out_shape=jax.ShapeDtypeStruct((B_ * HG * n_qt, HP, Dh, FOX_TQ), F32),
        name="fox_attention",
    )(qt, kk, vt)
    return out.reshape(B_, HG, n_qt, HP, Dh, FOX_TQ).transpose(0, 2, 5, 1, 3, 4).reshape(B_, S_, H * Dh)


NSA_TQ = 128
NSA_N = NSA_HPG * NSA_TQ
N_CMP_PAD = 256
N_SLC = 64
NSA_DK = 3 * HEAD_DIM
NSA_KV = 512
WIN_TILES = WINDOW // NSA_TQ + 1
WIN_KEYS = WIN_TILES * NSA_TQ


def _nsa_kernel(qt_ref, kc_ref, vct_ref, ks_ref, vst_ref, kw_ref, vwt_ref, gate_ref, ovl_ref, o_ref):
    qi = pl.program_id(1)
    q0 = qi * NSA_TQ
    qt = qt_ref[...]
    lane = lax.broadcasted_iota(jnp.int32, (1, NSA_N), 1)
    qpos = q0 + (lane & (NSA_TQ - 1))

    hn = NSA_N // 2
    qts = (qt[:, :hn], qt[:, hn:])
    qpos_h = qpos[:, :hn]

    w0 = jnp.maximum(qi - WINDOW // NSA_TQ, 0)
    k_w = jnp.concatenate([kw_ref[w0 + i] for i in range(WIN_TILES)], axis=0)
    vt_w = jnp.concatenate([vwt_ref[w0 + i] for i in range(WIN_TILES)], axis=1)
    s = jnp.dot(kc_ref[...], qt, preferred_element_type=F32)
    s_w = [jnp.dot(k_w, qts[h], preferred_element_type=F32) for h in range(2)]

    cmp_end = lax.broadcasted_iota(jnp.int32, (N_CMP_PAD, NSA_N), 0) * CMP_STRIDE + (CMP_LEN - 1)
    mc = cmp_end <= qpos
    m = jnp.max(jnp.where(mc, s, NEG_INF), axis=0, keepdims=True)
    p = jnp.where(mc, jnp.exp(s - m), 0.0)
    l = jnp.sum(p, axis=0, keepdims=True)
    pc = p * jnp.where(l > 0.0, 1.0 / l, 0.0)
    o_cmp = jnp.dot(vct_ref[...], pc.astype(BF16), preferred_element_type=F32)
    pcs = (pc[:, 0:NSA_TQ] + pc[:, NSA_TQ:2 * NSA_TQ]) + (pc[:, 2 * NSA_TQ:3 * NSA_TQ] + pc[:, 3 * NSA_TQ:4 * NSA_TQ])
    pcs_hi = pcs.astype(BF16)
    pcs_lo = (pcs - pcs_hi.astype(F32)).astype(BF16)
    ovl = ovl_ref[...]
    imp = jnp.dot(ovl, pcs_hi, preferred_element_type=F32) + jnp.dot(ovl, pcs_lo, preferred_element_type=F32)

    rel = qpos_h - (w0 * NSA_TQ + lax.broadcasted_iota(jnp.int32, (WIN_KEYS, hn), 0))
    in_win = (rel >= 0) & (rel < WINDOW)
    outs_w = []
    for h in range(2):
        m = jnp.max(jnp.where(in_win, s_w[h], NEG_INF), axis=0, keepdims=True)
        p = jnp.where(in_win, jnp.exp(s_w[h] - m), 0.0)
        l = jnp.sum(p, axis=0, keepdims=True)
        outs_w.append(jnp.dot(vt_w, p.astype(BF16), preferred_element_type=F32) / l)
    acc_w = jnp.concatenate(outs_w, axis=1)

    blk = lax.broadcasted_iota(jnp.int32, (N_SLC, NSA_TQ), 0)
    q_blk = (q0 + lax.broadcasted_iota(jnp.int32, (N_SLC, NSA_TQ), 1)) >> 6
    forced = (blk == 0) | (blk == q_blk) | (blk == q_blk - 1)
    score = jnp.where(forced, FORCED_SCORE, jnp.where(blk <= q_blk, imp, -1.0))
    rank = jnp.zeros((N_SLC, NSA_TQ), jnp.int32)
    for i in range(N_SLC):
        row = score[i:i + 1, :]
        beats = (row > score) | ((row == score) & (blk > i))
        rank = rank + beats.astype(jnp.int32)
    bias = jnp.where(rank < SLC_TOP, 0.0, NEG_INF).astype(BF16)
    bias = jnp.concatenate([bias] * (NSA_HPG // 2), axis=1)
    qts_s = [jnp.concatenate([qh, bias], axis=0) for qh in qts]
    krow = lax.broadcasted_iota(jnp.int32, (NSA_KV, hn), 0)

    n_full = q0 // NSA_KV
    causal = (n_full * NSA_KV + krow) <= qpos_h
    slc = _causal_flash(lambda kt: [ks_ref[kt]] * 2, lambda kt: [vst_ref[kt]] * 2, qts_s, n_full, causal, hn)
    acc_s = jnp.concatenate([slc[0][2] / slc[0][1], slc[1][2] / slc[1][1]], axis=1)

    g = jax.nn.sigmoid(gate_ref[...])
    o_ref[...] = g[0:1] * o_cmp + g[1:2] * acc_s + g[2:3] * acc_w


def nsa_attention(q, k_cmp, v_cmp, k_slc, v_slc, k_win, v_win, gate_logit):
    B_, S_, H, Dh = q.shape
    G = NSA_GROUPS
    n_qt = S_ // NSA_TQ
    n_cmp = k_cmp.shape[1]
    cs = np.arange(N_CMP_PAD)[None, :] * CMP_STRIDE
    ss = np.arange(N_SLC)[:, None] * SLC_LEN
    ov = np.clip(np.minimum(cs + CMP_LEN, ss + SLC_LEN) - np.maximum(cs, ss), 0, None) / CMP_LEN
    ov[:, n_cmp:] = 0.0
    ovl = jnp.asarray(ov, dtype=BF16)
    qs = _aug_q(q * (Dh ** -0.5)).reshape(B_, n_qt, NSA_TQ, G, NSA_HPG, NSA_DK)
    qt = qs.transpose(0, 3, 1, 5, 4, 2).reshape(B_ * G * n_qt, NSA_DK, NSA_N)
    gt = gate_logit.astype(F32).reshape(B_, n_qt, NSA_TQ, G, NSA_HPG, N_NSA_BRANCH)
    gt = gt.transpose(0, 3, 1, 5, 4, 2).reshape(B_ * G * n_qt, N_NSA_BRANCH, NSA_N)
    tiles = lambda t, kv: t.reshape(B_, S_ // kv, kv, G, -1).transpose(0, 3, 1, 2, 4).reshape(B_ * G, S_ // kv, kv, t.shape[-1])
    keys = lambda t, kv: tiles(_aug_k(t), kv)
    block_onehot = (jnp.arange(S_)[:, None] // SLC_LEN == jnp.arange(N_SLC)[None, :]).astype(BF16)
    k_slc_aug = jnp.concatenate([_aug_k(k_slc), jnp.broadcast_to(block_onehot[None, :, None, :], (B_, S_, G, N_SLC))], axis=-1)
    vals = lambda t, kv: t.astype(BF16).reshape(B_, S_ // kv, kv, G, Dh).transpose(0, 3, 1, 4, 2).reshape(B_ * G, S_ // kv, Dh, kv)
    padc = ((0, 0), (0, N_CMP_PAD - n_cmp), (0, 0), (0, 0))
    kc = _aug_k(jnp.pad(k_cmp, padc)).transpose(0, 2, 1, 3).reshape(B_ * G, N_CMP_PAD, NSA_DK)
    vct = jnp.pad(v_cmp, padc).astype(BF16).transpose(0, 2, 3, 1).reshape(B_ * G, Dh, N_CMP_PAD)
    per_bg = lambda *blk: pl.BlockSpec((None,) + blk, lambda b, i: (b,) + (0,) * len(blk))
    per_tile = lambda *blk: pl.BlockSpec((None,) + blk, lambda b, i: (b * n_qt + i,) + (0,) * len(blk))
    out = pl.pallas_call(
        _nsa_kernel,
        grid=(B_ * G, n_qt),
        in_specs=[per_tile(NSA_DK, NSA_N),
                  per_bg(N_CMP_PAD, NSA_DK), per_bg(Dh, N_CMP_PAD),
                  per_bg(S_ // NSA_KV, NSA_KV, NSA_DK + N_SLC), per_bg(S_ // NSA_KV, Dh, NSA_KV),
                  per_bg(n_qt, NSA_TQ, NSA_DK), per_bg(n_qt, Dh, NSA_TQ),
                  per_tile(N_NSA_BRANCH, NSA_N),
                  pl.BlockSpec((N_SLC, N_CMP_PAD), lambda b, i: (0, 0))],
        out_specs=per_tile(Dh, NSA_N),
        out_shape=jax.ShapeDtypeStruct((B_ * G * n_qt, Dh, NSA_N), F32),
        name="nsa_attention",
    )(qt, kc, vct, tiles(k_slc_aug, NSA_KV), vals(v_slc, NSA_KV), keys(k_win, NSA_TQ), vals(v_win, NSA_TQ), gt, ovl)
    return out.reshape(B_, G, n_qt, Dh, NSA_HPG, NSA_TQ).transpose(0, 2, 5, 1, 4, 3).reshape(B_, S_, H * Dh)


DENSE_TM = 256
SEC_QKV = 4 * BRANCH_WIDTH
SEC_KV = 6 * KV_WIDTH
SEC_MERGE = N_BRANCH * D_MODEL
SEC_SMALL = 256
IN_SECTIONS = (SEC_QKV, SEC_KV, SEC_MERGE, SEC_SMALL)


def _rms(x, g):
    return x * lax.rsqrt(jnp.mean(x * x, axis=-1, keepdims=True) + RMS_EPS) * g


def _row_spec(width):
    return pl.BlockSpec((DENSE_TM, width), lambda i: (i, 0))


def _whole_spec(shape):
    return pl.BlockSpec(shape, lambda i: (0,) * len(shape))


def _in_proj_kernel(x_ref, g_ref, w_ref, *o_refs):
    xn = _rms(x_ref[...], g_ref[...]).astype(BF16)
    off = 0
    for o_ref, width in zip(o_refs, IN_SECTIONS):
        o_ref[...] = jnp.dot(xn, w_ref[:, off:off + width], preferred_element_type=F32)
        off += width


def in_projection(x, g, w_in):
    t, d = x.shape
    fq, fk, fv, fl, nq, kc, vc, ksl, vsl, kwn, vwn, gl, ml = split_columns(w_in, IN_SPLITS)
    pad = lambda w: jnp.pad(w, ((0, 0), (0, 128 - w.shape[1])))
    w = jnp.concatenate([fq, fk, fv, nq, kc, vc, ksl, vsl, kwn, vwn, ml, pad(fl), pad(gl)], axis=1).astype(BF16)
    return pl.pallas_call(
        _in_proj_kernel,
        grid=(t // DENSE_TM,),
        in_specs=[_row_spec(d), _whole_spec((1, d)), _whole_spec(w.shape)],
        out_specs=[_row_spec(s) for s in IN_SECTIONS],
        out_shape=[jax.ShapeDtypeStruct((t, s), F32) for s in IN_SECTIONS],
        name="in_projection",
    )(x, g.reshape(1, d), w)


def _merge_kernel(x_ref, yf_ref, yn_ref, ml_ref, wb_ref, wo_ref, o_ref):
    g = jax.nn.sigmoid(ml_ref[...])
    up_f = jnp.dot(yf_ref[...].astype(BF16), wb_ref[0], preferred_element_type=F32)
    up_n = jnp.dot(yn_ref[...].astype(BF16), wb_ref[1], preferred_element_type=F32)
    merged = g[:, :D_MODEL] * up_f + g[:, D_MODEL:] * up_n
    o_ref[...] = x_ref[...] + jnp.dot(merged.astype(BF16), wo_ref[...], preferred_element_type=F32)


def merge_out(x, y_fox, y_nsa, merge_logit, w_branch, w_out):
    t, d = x.shape
    return pl.pallas_call(
        _merge_kernel,
        grid=(t // DENSE_TM,),
        in_specs=[_row_spec(d), _row_spec(BRANCH_WIDTH), _row_spec(BRANCH_WIDTH), _row_spec(SEC_MERGE),
                  _whole_spec(w_branch.shape), _whole_spec(w_out.shape)],
        out_specs=_row_spec(d),
        out_shape=jax.ShapeDtypeStruct((t, d), F32),
        name="merge_out",
    )(x, y_fox, y_nsa, merge_logit, w_branch.astype(BF16), w_out.astype(BF16))


def _peer_q_kernel(x_ref, g_ref, w_ref, hn_ref, q_ref):
    hn = _rms(x_ref[...], g_ref[...])
    hn_ref[...] = hn
    q = jnp.dot(hn.astype(BF16), w_ref[...], preferred_element_type=F32)
    d = PEER_QDIM // 2
    for j in range(2 * PEER_HEADS):
        q_ref[j] = q[:, j * d:(j + 1) * d]


def peer_query(x, g, wq):
    t, d = x.shape
    hp, dq = 2 * PEER_HEADS, PEER_QDIM // 2
    return pl.pallas_call(
        _peer_q_kernel,
        grid=(t // DENSE_TM,),
        in_specs=[_row_spec(d), _whole_spec((1, d)), _whole_spec(wq.shape)],
        out_specs=[_row_spec(d), pl.BlockSpec((hp, DENSE_TM, dq), lambda i: (0, i, 0))],
        out_shape=[jax.ShapeDtypeStruct((t, d), F32), jax.ShapeDtypeStruct((hp, t, dq), F32)],
        name="peer_query",
    )(x, g.reshape(1, d), wq.astype(BF16))


def _final_kernel(h_ref, p_ref, g_ref, o_ref):
    o_ref[...] = _rms(h_ref[...] + p_ref[...], g_ref[...])


def final_norm(h, p, g):
    t, d = h.shape
    return pl.pallas_call(
        _final_kernel,
        grid=(t // DENSE_TM,),
        in_specs=[_row_spec(d), _row_spec(d), _whole_spec((1, d))],
        out_specs=_row_spec(d),
        out_shape=jax.ShapeDtypeStruct((t, d), F32),
        name="final_norm",
    )(h, p, g.reshape(1, d))


N_CMP_ROWS = 256


def _compress_kernel(r_ref, pos_ref, w1_ref, w2_ref, o_ref):
    r = r_ref[...]
    nxt = jnp.concatenate([r[1:], jnp.zeros((1, r.shape[1]), F32)], axis=0)
    half = r.shape[1]
    hid = (jnp.dot((r + pos_ref[0:1, :]).astype(BF16), w1_ref[:half], preferred_element_type=F32)
           + jnp.dot((nxt + pos_ref[1:2, :]).astype(BF16), w1_ref[half:], preferred_element_type=F32))
    o_ref[...] = jnp.dot(_gelu(hid).astype(BF16), w2_ref[...], preferred_element_type=F32)


def compress_kv_pallas(kv, cmp_pos, w1, w2):
    _, B_, S_, G, Dh = kv.shape
    n_cmp = (S_ - CMP_LEN) // CMP_STRIDE + 1
    width = CMP_STRIDE * Dh
    rows = kv.transpose(0, 1, 3, 2, 4).reshape(2, B_ * G, S_ // CMP_STRIDE, width)
    out = pl.pallas_call(
        _compress_kernel,
        grid=(2, B_ * G),
        in_specs=[pl.BlockSpec((None, None, N_CMP_ROWS, width), lambda j, b: (j, b, 0, 0)),
                  pl.BlockSpec((None, 2, width), lambda j, b: (j, 0, 0)),
                  pl.BlockSpec((None, CMP_LEN * Dh, CMP_HIDDEN), lambda j, b: (j, 0, 0)),
                  pl.BlockSpec((None, CMP_HIDDEN, Dh), lambda j, b: (j, 0, 0))],
        out_specs=pl.BlockSpec((None, None, N_CMP_ROWS, Dh), lambda j, b: (j, b, 0, 0)),
        out_shape=jax.ShapeDtypeStruct((2, B_ * G, N_CMP_ROWS, Dh), F32),
        name="compress_kv",
    )(rows, cmp_pos.reshape(2, 2, width), w1.astype(BF16), w2.astype(BF16))
    return out[:, :, :n_cmp].reshape(2, B_, G, n_cmp, Dh).transpose(0, 1, 3, 2, 4)


def hybrid_mixer(x, norm_g, w_in, fox_f_bias, cmp_pos, cmp_w1, cmp_w2, w_branch, w_out):
    B_, S_, D = x.shape
    xf = x.reshape(B_ * S_, D)
    qkv, kvs, merge_logit, small = in_projection(xf, norm_g, w_in)
    heads = lambda t, n: t.reshape(B_, S_, n, HEAD_DIM)
    fq, fk, fv, nq = (qkv[:, i * BRANCH_WIDTH:(i + 1) * BRANCH_WIDTH] for i in range(4))
    kc, vc, ksl, vsl, kwn, vwn = (heads(kvs[:, i * KV_WIDTH:(i + 1) * KV_WIDTH], NSA_GROUPS) for i in range(6))
    f_logit = small[:, :FOX_HEADS].reshape(B_, S_, FOX_HEADS)
    gate_logit = small[:, 128:128 + NSA_HEADS * N_NSA_BRANCH].reshape(B_, S_, NSA_HEADS, N_NSA_BRANCH)
    log_f = jax.nn.log_sigmoid(f_logit + fox_f_bias.astype(F32))
    y_fox = fox_attention(heads(fq, FOX_HEADS), heads(fk, FOX_HEADS), heads(fv, FOX_HEADS), log_f)
    cos, sin = rope_tables(jnp.arange(S_, dtype=F32))
    q_nsa = apply_partial_rope(heads(nq, NSA_HEADS), cos, sin)
    k_slc = apply_partial_rope(ksl, cos, sin)
    k_win = apply_partial_rope(kwn, cos, sin)
    cmp = compress_kv_pallas(jnp.stack([kc, vc]), cmp_pos, cmp_w1, cmp_w2)
    n_cmp = cmp.shape[2]
    cmp_end = jnp.arange(n_cmp, dtype=F32) * CMP_STRIDE + (CMP_LEN - 1)
    k_cmp = apply_partial_rope(cmp[0], *rope_tables(cmp_end))
    y_nsa = nsa_attention(q_nsa, k_cmp, cmp[1], k_slc, vsl, k_win, vwn, gate_logit)
    return merge_out(xf, y_fox.reshape(B_ * S_, -1), y_nsa.reshape(B_ * S_, -1), merge_logit, w_branch, w_out)


def peer_ffn(h, norm_g, wq, subkeys, u, v):
    hn, q = peer_query(h, norm_g, wq)
    idx2, w = peer_topk(q, subkeys)
    coef = peer_u(idx2, hn, w, pack_table(u))
    return peer_v(idx2, coef, pack_table(v))


def kernel(x, norm_mix, w_in, fox_f_bias, nsa_cmp_pos, nsa_cmp_w1, nsa_cmp_w2, w_branch, w_out,
           norm_ffn, peer_wq, peer_subkeys, peer_u, peer_v, norm_final):
    B_, S_, D = x.shape
    assert norm_mix.shape[0] == 1, "single-layer trunk"
    h = hybrid_mixer(x, norm_mix[0], w_in[0], fox_f_bias[0], nsa_cmp_pos[0], nsa_cmp_w1[0], nsa_cmp_w2[0],
                     w_branch[0], w_out[0])
    p = peer_ffn(h, norm_ffn[0], peer_wq[0], peer_subkeys[0], peer_u[0], peer_v[0])
    return final_norm(h, p, norm_final).reshape(B_, S_, D)
```

```python
import jax
import jax.numpy as jnp
import numpy as np
from jax import lax
from jax.experimental import pallas as pl
from jax.experimental.pallas import tpu as pltpu

D_MODEL = 1024
HEAD_DIM = 64
FOX_HEADS = 8
NSA_HEADS = 8
NSA_GROUPS = 2
NSA_HPG = NSA_HEADS // NSA_GROUPS
BRANCH_WIDTH = 512
N_BRANCH = 2
N_NSA_BRANCH = 3
ROPE_DIM = HEAD_DIM // 4
ROPE_THETA = 500000.0
CMP_LEN = 32
CMP_STRIDE = 16
CMP_HIDDEN = 2 * HEAD_DIM
SLC_LEN = 64
SLC_TOP = 16
WINDOW = 512
FORCED_SCORE = 1e9
NEG_INF = -1e30
PEER_HEADS = 8
N_KEYS = 128
PEER_QDIM = 256
PEER_TOPK = 16
RMS_EPS = 1e-6
KV_WIDTH = NSA_GROUPS * HEAD_DIM
IN_SPLITS = (BRANCH_WIDTH, BRANCH_WIDTH, BRANCH_WIDTH, FOX_HEADS, BRANCH_WIDTH,
             KV_WIDTH, KV_WIDTH, KV_WIDTH, KV_WIDTH, KV_WIDTH, KV_WIDTH,
             NSA_HEADS * N_NSA_BRANCH, N_BRANCH * D_MODEL)


BF16 = jnp.bfloat16
F32 = jnp.float32

PEER_SLOTS = PEER_HEADS * PEER_TOPK
PEER_TT = 64
PEER_CHUNK_SLOTS = 32
PEER_U_UNROLL = 8
PEER_V_UNROLL = 8
ROW_SUB = D_MODEL // 2 // 128
VMEM_LIMIT_PEER = N_KEYS * N_KEYS * D_MODEL * 2 + 16 * 1024 * 1024


def pack_table(tab):
    n, d = tab.shape
    return pl.pallas_call(
        _pack_kernel,
        grid=(n // PACK_TM,),
        in_specs=[pl.BlockSpec((PACK_TM, d), lambda i: (i, 0))],
        out_specs=pl.BlockSpec((PACK_TM * ROW_SUB, 128), lambda i: (i, 0)),
        out_shape=jax.ShapeDtypeStruct((n * ROW_SUB, 128), jnp.uint32),
        name="pack_table",
    )(tab)


PACK_TM = 256


def _pack_kernel(x_ref, o_ref):
    bits = lax.bitcast_convert_type(x_ref[...].astype(jnp.bfloat16).astype(jnp.float32), jnp.uint32)
    for r in range(ROW_SUB):
        even = bits[:, (2 * r) * 128:(2 * r + 1) * 128]
        odd = bits[:, (2 * r + 1) * 128:(2 * r + 2) * 128]
        o_ref[pl.ds(r, PACK_TM, stride=ROW_SUB), :] = (even >> 16) | (odd & jnp.uint32(0xFFFF0000))


def _expert_slab(tab_ref, row):
    return tab_ref[pl.ds(pl.multiple_of(row, ROW_SUB), ROW_SUB), :]


def _unpack(w):
    lo = lax.bitcast_convert_type(w << 16, jnp.float32)
    hi = lax.bitcast_convert_type(w & jnp.uint32(0xFFFF0000), jnp.float32)
    return lo, hi


def _gelu(x):
    return 0.5 * x * (1.0 + lax.erf(x * (2.0 ** -0.5)))


def _peer_u_kernel(idx_a_ref, idx_b_ref, x_ref, w_ref, tab_ref, out_ref):
    lane = lax.broadcasted_iota(jnp.int32, (2 * ROW_SUB, 128), 1)
    lower = lax.broadcasted_iota(jnp.int32, (2 * ROW_SUB, 128), 0) >= ROW_SUB
    pair_lane = jnp.where(lower, lane - PEER_SLOTS // 2, lane)

    def token(t):
        row = x_ref[pl.ds(t, 1), :]
        chunks = [row[:, c * 128:(c + 1) * 128] for c in range(2 * ROW_SUB)]
        xl = jnp.concatenate(chunks[0::2] * 2, axis=0)
        xh = jnp.concatenate(chunks[1::2] * 2, axis=0)
        acc = jnp.zeros((2 * ROW_SUB, 128), jnp.float32)
        base = t * (PEER_SLOTS // 2)
        for s in range(PEER_SLOTS // 2):
            w = jnp.concatenate([_expert_slab(tab_ref, idx_a_ref[base + s]),
                                 _expert_slab(tab_ref, idx_b_ref[base + s])], axis=0)
            lo, hi = _unpack(w)
            part = jnp.sum(lo * xl + hi * xh, axis=1, keepdims=True)
            acc = jnp.where(pair_lane == s, part, acc)
        out_ref[pl.ds(t, 1), :] = w_ref[pl.ds(t, 1), :] * _gelu(jnp.sum(acc, axis=0, keepdims=True))

    def trip(i, carry):
        for j in range(PEER_U_UNROLL):
            token(i * PEER_U_UNROLL + j)
        return carry

    lax.fori_loop(0, PEER_TT // PEER_U_UNROLL, trip, 0)


N_CHUNK = 2 * ROW_SUB


def _peer_v_kernel(idx_a_ref, idx_b_ref, coef_ref, expand_ref, tab_ref, out_ref, ce_hi_ref, ce_lo_ref):
    half_slots = PEER_SLOTS // 2
    width = PEER_SLOTS * N_CHUNK
    coef = coef_ref[...]
    c_hi = _trunc_bf16(coef)
    expand = expand_ref[...]
    ce_hi_ref[...] = jnp.dot(c_hi.astype(BF16), expand, preferred_element_type=F32)
    ce_lo_ref[...] = jnp.dot((coef - c_hi).astype(BF16), expand, preferred_element_type=F32)
    own_chunk = ((lax.broadcasted_iota(jnp.int32, (N_CHUNK, width), 1) & (N_CHUNK - 1))
                 == lax.broadcasted_iota(jnp.int32, (N_CHUNK, width), 0))

    def token(t):
        base = t * half_slots
        slabs = [jnp.concatenate([_expert_slab(tab_ref, idx_a_ref[base + s]),
                                  _expert_slab(tab_ref, idx_b_ref[base + s])], axis=0) for s in range(half_slots)]
        gathered = pltpu.bitcast(jnp.concatenate(slabs, axis=0), BF16)
        sel = lambda ref: jnp.where(own_chunk, ref[pl.ds(t, 1), :], 0.0).astype(BF16)
        c = jnp.concatenate([sel(ce_hi_ref), sel(ce_lo_ref)], axis=0)
        o = jnp.dot(c, gathered, preferred_element_type=F32)
        o = o[:N_CHUNK] + o[N_CHUNK:]
        out_ref[pl.ds(t, 1), :] = jnp.concatenate([o[r:r + 1, :] for r in range(N_CHUNK)], axis=1)

    def trip(i, carry):
        for j in range(PEER_V_UNROLL):
            token(i * PEER_V_UNROLL + j)
        return carry

    lax.fori_loop(0, PEER_TT // PEER_V_UNROLL, trip, 0)


def _smem_spec():
    return pl.BlockSpec((PEER_TT * PEER_SLOTS // 2,), lambda i: (i,), memory_space=pltpu.SMEM)


def _slot_halves(a):
    return a[:, :PEER_SLOTS // 2].reshape(-1), a[:, PEER_SLOTS // 2:].reshape(-1)


def _table_spec(rows):
    return pl.BlockSpec((rows, 128), lambda i: (0, 0), pipeline_mode=pl.Buffered(1))


def peer_u(idx, x, w, tab):
    t = x.shape[0]
    return pl.pallas_call(
        _peer_u_kernel,
        grid=(t // PEER_TT,),
        in_specs=[_smem_spec(), _smem_spec(), pl.BlockSpec((PEER_TT, D_MODEL), lambda i: (i, 0)),
                  pl.BlockSpec((PEER_TT, PEER_SLOTS), lambda i: (i, 0)), _table_spec(tab.shape[0])],
        out_specs=pl.BlockSpec((PEER_TT, PEER_SLOTS), lambda i: (i, 0)),
        out_shape=jax.ShapeDtypeStruct((t, PEER_SLOTS), jnp.float32),
        compiler_params=pltpu.CompilerParams(vmem_limit_bytes=VMEM_LIMIT_PEER),
        name="peer_u",
    )(*_slot_halves(idx), x, w, tab)


def peer_v(idx, coef, tab):
    t = idx.shape[0]
    width = PEER_SLOTS * N_CHUNK
    half = PEER_SLOTS // 2
    col_slot = np.arange(width) // N_CHUNK
    col_slot = (col_slot // 2) + half * (col_slot % 2)
    expand = jnp.asarray(col_slot[None, :] == np.arange(PEER_SLOTS)[:, None], dtype=BF16)
    return pl.pallas_call(
        _peer_v_kernel,
        grid=(t // PEER_TT,),
        in_specs=[_smem_spec(), _smem_spec(), pl.BlockSpec((PEER_TT, PEER_SLOTS), lambda i: (i, 0)),
                  pl.BlockSpec((PEER_SLOTS, width), lambda i: (0, 0)), _table_spec(tab.shape[0])],
        out_specs=pl.BlockSpec((PEER_TT, D_MODEL), lambda i: (i, 0)),
        out_shape=jax.ShapeDtypeStruct((t, D_MODEL), jnp.float32),
        scratch_shapes=[pltpu.VMEM((PEER_TT, width), F32), pltpu.VMEM((PEER_TT, width), F32)],
        compiler_params=pltpu.CompilerParams(vmem_limit_bytes=VMEM_LIMIT_PEER),
        name="peer_v",
    )(*_slot_halves(idx), coef, expand, tab)


PK_TT = 256
PK_A_BLOCKS = ((0, 16), (1, 8), (2, 5), (3, 4))
PK_B_BLOCKS = ((0, 4, 16), (1, 4, 8), (2, 4, 5))
NEG_HUGE = -3.0e38
PK_HEADS_PER_TRIP = 2


def _extract_top(problems, flat, n):
    vals = [[] for _ in problems]
    picked = [[] for _ in problems]
    cur = [v for v, _ in problems]
    for _ in range(n):
        for i, (_, payload) in enumerate(problems):
            m = jnp.max(cur[i], axis=0, keepdims=True)
            pos = jnp.min(jnp.where(cur[i] == m, flat, 1e9), axis=0, keepdims=True)
            hit = flat == pos
            picked[i].append(pos if payload is None else jnp.sum(jnp.where(hit, payload, 0.0), axis=0, keepdims=True))
            cur[i] = jnp.where(hit, NEG_HUGE, cur[i])
            vals[i].append(m)
    return [jnp.concatenate(v, axis=0) for v in vals], [jnp.concatenate(p, axis=0) for p in picked]


def _candidate_grid(first, second, combine):
    up8 = lambda n: -(-n // 8) * 8
    return jnp.concatenate([combine(first[a:a + 1], second[0:up8(nb)]) for a, nb in PK_A_BLOCKS]
                           + [combine(first[0:up8(a1)], second[b:b + 1]) for b, _, a1 in PK_B_BLOCKS], axis=0)


def _candidate_valid_and_flat():
    up8 = lambda n: -(-n // 8) * 8
    valid, flat = [], []
    for a, nb in PK_A_BLOCKS:
        r = lax.broadcasted_iota(jnp.int32, (up8(nb), PK_TT), 0)
        valid.append(r < nb)
        flat.append(a * PEER_TOPK + r)
    for b, a0, a1 in PK_B_BLOCKS:
        r = lax.broadcasted_iota(jnp.int32, (up8(a1), PK_TT), 0)
        valid.append((r >= a0) & (r < a1))
        flat.append(r * PEER_TOPK + b)
    valid = jnp.concatenate(valid, axis=0)
    return valid, jnp.where(valid, jnp.concatenate(flat, axis=0).astype(F32), 1e9)


def _peer_topk_kernel(q_ref, ka_ref, eid_ref, w_ref):
    key_row = lax.broadcasted_iota(jnp.int32, (N_KEYS, PK_TT), 0).astype(F32)
    valid, flat = _candidate_valid_and_flat()

    def heads(i, carry):
        scores = []
        for j in range(2 * PK_HEADS_PER_TRIP):
            q = q_ref[2 * PK_HEADS_PER_TRIP * i + j]
            q_hi = _trunc_bf16(q)
            qa = jnp.concatenate([q_hi.astype(BF16), (q - q_hi).astype(BF16), q_hi.astype(BF16)], axis=1)
            scores.append(lax.dot_general(ka_ref[2 * PK_HEADS_PER_TRIP * i + j], qa, (((1,), (1,)), ((), ())),
                                          preferred_element_type=F32))
        sub_s, sub_i = _extract_top([(s, None) for s in scores], key_row, PEER_TOPK)
        cands = []
        for j in range(PK_HEADS_PER_TRIP):
            s1, s2, i1, i2 = sub_s[2 * j], sub_s[2 * j + 1], sub_i[2 * j], sub_i[2 * j + 1]
            cands.append((jnp.where(valid, _candidate_grid(s1, s2, lambda x, y: x + y), NEG_HUGE),
                          _candidate_grid(i1, i2, lambda x, y: x * N_KEYS + y)))
        top_s, top_e = _extract_top(cands, flat, PEER_TOPK)
        for j in range(PK_HEADS_PER_TRIP):
            e = jnp.exp(top_s[j] - top_s[j][0:1])
            w_ref[PK_HEADS_PER_TRIP * i + j] = e / jnp.sum(e, axis=0, keepdims=True)
            eid_ref[PK_HEADS_PER_TRIP * i + j] = top_e[j].astype(jnp.int32) * ROW_SUB
        return carry

    lax.fori_loop(0, PEER_HEADS // PK_HEADS_PER_TRIP, heads, 0)


def peer_topk(qt, subkeys):
    hp, t, d = qt.shape
    k_hi, k_lo = _split2(subkeys.reshape(hp, N_KEYS, d))
    ka = jnp.concatenate([k_hi, k_hi, k_lo], axis=2)
    eid, w = pl.pallas_call(
        _peer_topk_kernel,
        grid=(t // PK_TT,),
        in_specs=[pl.BlockSpec((hp, PK_TT, d), lambda i: (0, i, 0)),
                  pl.BlockSpec((hp, N_KEYS, 3 * d), lambda i: (0, 0, 0))],
        out_specs=[pl.BlockSpec((PEER_HEADS, PEER_TOPK, PK_TT), lambda i: (0, 0, i)),
                   pl.BlockSpec((PEER_HEADS, PEER_TOPK, PK_TT), lambda i: (0, 0, i))],
        out_shape=[jax.ShapeDtypeStruct((PEER_HEADS, PEER_TOPK, t), jnp.int32),
                   jax.ShapeDtypeStruct((PEER_HEADS, PEER_TOPK, t), F32)],
        name="peer_topk",
    )(qt, ka)
    to_slots = lambda a: a.transpose(2, 0, 1).reshape(t, PEER_SLOTS)
    return to_slots(eid), to_slots(w)


def split_columns(t, sizes):
    offs = np.cumsum((0,) + tuple(sizes))
    return [t[..., int(a):int(b)] for a, b in zip(offs[:-1], offs[1:])]


def rope_tables(pos):
    inv = jnp.power(ROPE_THETA, -jnp.arange(0, ROPE_DIM, 2, dtype=jnp.float32) / ROPE_DIM)
    ang = pos[:, None] * inv[None, :]
    return jnp.cos(ang), jnp.sin(ang)


def apply_partial_rope(x, cos, sin):
    half = ROPE_DIM // 2
    xr = x[..., :ROPE_DIM].astype(jnp.float32)
    x1, x2 = xr[..., :half], xr[..., half:]
    c, s = cos[:, None, :], sin[:, None, :]
    rot = jnp.concatenate([x1 * c - x2 * s, x1 * s + x2 * c], axis=-1)
    return jnp.concatenate([rot.astype(x.dtype), x[..., ROPE_DIM:]], axis=-1)


def _logits(ks, qts):
    return [jnp.dot(k, qt, preferred_element_type=F32) for k, qt in zip(ks, qts)]


def _flash_update(ss, vts, mask, carries):
    stats, ps = [], []
    for s, (m, l, _) in zip(ss, carries):
        s_vis = s if mask is None else jnp.where(mask, s, NEG_INF)
        m_new = jnp.maximum(m, jnp.max(s_vis, axis=0, keepdims=True))
        alpha = jnp.exp(m - m_new)
        p = jnp.exp(s - m_new)
        if mask is not None:
            p = jnp.where(mask, p, 0.0)
        stats.append((m_new, alpha, alpha * l + jnp.sum(p, axis=0, keepdims=True)))
        ps.append(p.astype(BF16))
    return tuple((m_new, l, alpha * acc + jnp.dot(vt, p, preferred_element_type=F32))
                 for (m_new, alpha, l), p, vt, (_, _, acc) in zip(stats, ps, vts, carries))


def _causal_flash(key_tile, value_tile, qts, n_full, last_mask, n):
    def body(kt, carries):
        return _flash_update(_logits(key_tile(kt), qts), value_tile(kt), None, carries)

    carries = lax.fori_loop(0, n_full, body, tuple(_flash_init(n) for _ in qts))
    return _flash_update(_logits(key_tile(n_full), qts), value_tile(n_full), last_mask, carries)


def _flash_init(n):
    return (jnp.full((1, n), NEG_INF, F32), jnp.zeros((1, n), F32), jnp.zeros((HEAD_DIM, n), F32))


FOX_TQ = 256
FOX_KV = 512
FOX_DK = 256


FOX_HP = 4


def _fox_kernel(qt_ref, k_ref, vt_ref, o_ref):
    qi = pl.program_id(1)
    q0 = qi * FOX_TQ
    qts = [qt_ref[h] for h in range(FOX_HP)]
    qpos = q0 + lax.broadcasted_iota(jnp.int32, (1, FOX_TQ), 1)
    krow = lax.broadcasted_iota(jnp.int32, (FOX_KV, FOX_TQ), 0)

    n_full = q0 // FOX_KV
    causal = (n_full * FOX_KV + krow) <= qpos
    carry = _causal_flash(lambda kt: [k_ref[h, kt] for h in range(FOX_HP)],
                          lambda kt: [vt_ref[h, kt] for h in range(FOX_HP)], qts, n_full, causal, FOX_TQ)
    for h in range(FOX_HP):
        _, l, acc = carry[h]
        o_ref[h] = acc / l


def _trunc_bf16(x):
    bits = lax.bitcast_convert_type(x, jnp.uint32) & jnp.uint32(0xFFFF0000)
    return lax.bitcast_convert_type(bits, F32)


def _split3(c):
    c1 = _trunc_bf16(c)
    r = c - c1
    c2 = _trunc_bf16(r)
    return c1.astype(BF16), c2.astype(BF16), (r - c2).astype(BF16)


def _split2(x):
    hi = _trunc_bf16(x)
    return hi.astype(BF16), (x - hi).astype(BF16)


def _aug_q(q):
    hi, lo = _split2(q)
    return jnp.concatenate([hi, hi, lo], axis=-1)


def _aug_k(k):
    hi, lo = _split2(k)
    return jnp.concatenate([hi, lo, hi], axis=-1)


def fox_attention(q, k, v, log_f):
    B_, S_, H, Dh = q.shape
    KV_TILE = FOX_KV
    n_qt, n_kt = S_ // FOX_TQ, S_ // KV_TILE
    c = jnp.cumsum(log_f, axis=1)
    c1, c2, c3 = _split3(c)
    j = jnp.arange(FOX_DK - 3 * Dh)
    pick = lambda t: t[..., None].astype(F32)
    terms = lambda first: jnp.where(j == first, pick(c1), jnp.where(j == first + 1, pick(c2), pick(c3)))
    q_extra = jnp.where(j < 3, 1.0, jnp.where(j < 6, terms(3), 0.0)).astype(BF16)
    k_extra = jnp.where(j < 3, -terms(0), jnp.where(j < 6, 1.0, 0.0)).astype(BF16)
    qa = jnp.concatenate([_aug_q(q * (Dh ** -0.5)), q_extra], axis=-1)
    ka = jnp.concatenate([_aug_k(k), k_extra], axis=-1)
    HP, HG = FOX_HP, H // FOX_HP
    qt = qa.reshape(B_, n_qt, FOX_TQ, HG, HP, FOX_DK).transpose(0, 3, 1, 4, 5, 2).reshape(B_ * HG * n_qt, HP, FOX_DK, FOX_TQ)
    kk = ka.reshape(B_, n_kt, KV_TILE, HG, HP, FOX_DK).transpose(0, 3, 4, 1, 2, 5).reshape(B_ * HG, HP, n_kt, KV_TILE, FOX_DK)
    vt = v.astype(BF16).reshape(B_, n_kt, KV_TILE, HG, HP, Dh).transpose(0, 3, 4, 1, 5, 2).reshape(B_ * HG, HP, n_kt, Dh, KV_TILE)
    out = pl.pallas_call(
        _fox_kernel,
        grid=(B_ * HG, n_qt),
        in_specs=[pl.BlockSpec((None, HP, FOX_DK, FOX_TQ), lambda b, i: (b * n_qt + i, 0, 0, 0)),
                  pl.BlockSpec((None, HP, n_kt, KV_TILE, FOX_DK), lambda b, i: (b, 0, 0, 0, 0)),
                  pl.BlockSpec((None, HP, n_kt, Dh, KV_TILE), lambda b, i: (b, 0, 0, 0, 0))],
        out_specs=pl.BlockSpec((None, HP, Dh, FOX_TQ), lambda b, i: (b * n_qt + i, 0, 0, 0)),
        out_shape=jax.ShapeDtypeStruct((B_ * HG * n_qt, HP, Dh, FOX_TQ), F32),
        name="fox_attention",
    )(qt, kk, vt)
    return out.reshape(B_, HG, n_qt, HP, Dh, FOX_TQ).transpose(0, 2, 5, 1, 3, 4).reshape(B_, S_, H * Dh)


NSA_TQ = 128
NSA_N = NSA_HPG * NSA_TQ
N_CMP_PAD = 256
N_SLC = 64
NSA_DK = 3 * HEAD_DIM
NSA_KV = 512
WIN_TILES = WINDOW // NSA_TQ + 1
WIN_KEYS = WIN_TILES * NSA_TQ


def _nsa_kernel(qt_ref, kc_ref, vct_ref, ks_ref, vst_ref, kw_ref, vwt_ref, gate_ref, ovl_ref, o_ref):
    qi = pl.program_id(1)
    q0 = qi * NSA_TQ
    qt = qt_ref[...]
    lane = lax.broadcasted_iota(jnp.int32, (1, NSA_N), 1)
    qpos = q0 + (lane & (NSA_TQ - 1))

    hn = NSA_N // 2
    qts = (qt[:, :hn], qt[:, hn:])
    qpos_h = qpos[:, :hn]

    w0 = jnp.maximum(qi - WINDOW // NSA_TQ, 0)
    k_w = jnp.concatenate([kw_ref[w0 + i] for i in range(WIN_TILES)], axis=0)
    vt_w = jnp.concatenate([vwt_ref[w0 + i] for i in range(WIN_TILES)], axis=1)
    s = jnp.dot(kc_ref[...], qt, preferred_element_type=F32)
    s_w = [jnp.dot(k_w, qts[h], preferred_element_type=F32) for h in range(2)]

    cmp_end = lax.broadcasted_iota(jnp.int32, (N_CMP_PAD, NSA_N), 0) * CMP_STRIDE + (CMP_LEN - 1)
    mc = cmp_end <= qpos
    m = jnp.max(jnp.where(mc, s, NEG_INF), axis=0, keepdims=True)
    p = jnp.where(mc, jnp.exp(s - m), 0.0)
    l = jnp.sum(p, axis=0, keepdims=True)
    pc = p * jnp.where(l > 0.0, 1.0 / l, 0.0)
    o_cmp = jnp.dot(vct_ref[...], pc.astype(BF16), preferred_element_type=F32)
    pcs = (pc[:, 0:NSA_TQ] + pc[:, NSA_TQ:2 * NSA_TQ]) + (pc[:, 2 * NSA_TQ:3 * NSA_TQ] + pc[:, 3 * NSA_TQ:4 * NSA_TQ])
    pcs_hi = pcs.astype(BF16)
    pcs_lo = (pcs - pcs_hi.astype(F32)).astype(BF16)
    ovl = ovl_ref[...]
    imp = jnp.dot(ovl, pcs_hi, preferred_element_type=F32) + jnp.dot(ovl, pcs_lo, preferred_element_type=F32)

    rel = qpos_h - (w0 * NSA_TQ + lax.broadcasted_iota(jnp.int32, (WIN_KEYS, hn), 0))
    in_win = (rel >= 0) & (rel < WINDOW)
    outs_w = []
    for h in range(2):
        m = jnp.max(jnp.where(in_win, s_w[h], NEG_INF), axis=0, keepdims=True)
        p = jnp.where(in_win, jnp.exp(s_w[h] - m), 0.0)
        l = jnp.sum(p, axis=0, keepdims=True)
        outs_w.append(jnp.dot(vt_w, p.astype(BF16), preferred_element_type=F32) / l)
    acc_w = jnp.concatenate(outs_w, axis=1)

    blk = lax.broadcasted_iota(jnp.int32, (N_SLC, NSA_TQ), 0)
    q_blk = (q0 + lax.broadcasted_iota(jnp.int32, (N_SLC, NSA_TQ), 1)) >> 6
    forced = (blk == 0) | (blk == q_blk) | (blk == q_blk - 1)
    score = jnp.where(forced, FORCED_SCORE, jnp.where(blk <= q_blk, imp, -1.0))
    rank = jnp.zeros((N_SLC, NSA_TQ), jnp.int32)
    for i in range(N_SLC):
        row = score[i:i + 1, :]
        beats = (row > score) | ((row == score) & (blk > i))
        rank = rank + beats.astype(jnp.int32)
    bias = jnp.where(rank < SLC_TOP, 0.0, NEG_INF).astype(BF16)
    bias = jnp.concatenate([bias] * (NSA_HPG // 2), axis=1)
    qts_s = [jnp.concatenate([qh, bias], axis=0) for qh in qts]
    krow = lax.broadcasted_iota(jnp.int32, (NSA_KV, hn), 0)

    n_full = q0 // NSA_KV
    causal = (n_full * NSA_KV + krow) <= qpos_h
    slc = _causal_flash(lambda kt: [ks_ref[kt]] * 2, lambda kt: [vst_ref[kt]] * 2, qts_s, n_full, causal, hn)
    acc_s = jnp.concatenate([slc[0][2] / slc[0][1], slc[1][2] / slc[1][1]], axis=1)

    g = jax.nn.sigmoid(gate_ref[...])
    o_ref[...] = g[0:1] * o_cmp + g[1:2] * acc_s + g[2:3] * acc_w


def nsa_attention(q, k_cmp, v_cmp, k_slc, v_slc, k_win, v_win, gate_logit):
    B_, S_, H, Dh = q.shape
    G = NSA_GROUPS
    n_qt = S_ // NSA_TQ
    n_cmp = k_cmp.shape[1]
    cs = np.arange(N_CMP_PAD)[None, :] * CMP_STRIDE
    ss = np.arange(N_SLC)[:, None] * SLC_LEN
    ov = np.clip(np.minimum(cs + CMP_LEN, ss + SLC_LEN) - np.maximum(cs, ss), 0, None) / CMP_LEN
    ov[:, n_cmp:] = 0.0
    ovl = jnp.asarray(ov, dtype=BF16)
    qs = _aug_q(q * (Dh ** -0.5)).reshape(B_, n_qt, NSA_TQ, G, NSA_HPG, NSA_DK)
    qt = qs.transpose(0, 3, 1, 5, 4, 2).reshape(B_ * G * n_qt, NSA_DK, NSA_N)
    gt = gate_logit.astype(F32).reshape(B_, n_qt, NSA_TQ, G, NSA_HPG, N_NSA_BRANCH)
    gt = gt.transpose(0, 3, 1, 5, 4, 2).reshape(B_ * G * n_qt, N_NSA_BRANCH, NSA_N)
    tiles = lambda t, kv: t.reshape(B_, S_ // kv, kv, G, -1).transpose(0, 3, 1, 2, 4).reshape(B_ * G, S_ // kv, kv, t.shape[-1])
    keys = lambda t, kv: tiles(_aug_k(t), kv)
    block_onehot = (jnp.arange(S_)[:, None] // SLC_LEN == jnp.arange(N_SLC)[None, :]).astype(BF16)
    k_slc_aug = jnp.concatenate([_aug_k(k_slc), jnp.broadcast_to(block_onehot[None, :, None, :], (B_, S_, G, N_SLC))], axis=-1)
    vals = lambda t, kv: t.astype(BF16).reshape(B_, S_ // kv, kv, G, Dh).transpose(0, 3, 1, 4, 2).reshape(B_ * G, S_ // kv, Dh, kv)
    padc = ((0, 0), (0, N_CMP_PAD - n_cmp), (0, 0), (0, 0))
    kc = _aug_k(jnp.pad(k_cmp, padc)).transpose(0, 2, 1, 3).reshape(B_ * G, N_CMP_PAD, NSA_DK)
    vct = jnp.pad(v_cmp, padc).astype(BF16).transpose(0, 2, 3, 1).reshape(B_ * G, Dh, N_CMP_PAD)
    per_bg = lambda *blk: pl.BlockSpec((None,) + blk, lambda b, i: (b,) + (0,) * len(blk))
    per_tile = lambda *blk: pl.BlockSpec((None,) + blk, lambda b, i: (b * n_qt + i,) + (0,) * len(blk))
    out = pl.pallas_call(
        _nsa_kernel,
        grid=(B_ * G, n_qt),
        in_specs=[per_tile(NSA_DK, NSA_N),
                  per_bg(N_CMP_PAD, NSA_DK), per_bg(Dh, N_CMP_PAD),
                  per_bg(S_ // NSA_KV, NSA_KV, NSA_DK + N_SLC), per_bg(S_ // NSA_KV, Dh, NSA_KV),
                  per_bg(n_qt, NSA_TQ, NSA_DK), per_bg(n_qt, Dh, NSA_TQ),
                  per_tile(N_NSA_BRANCH, NSA_N),
                  pl.BlockSpec((N_SLC, N_CMP_PAD), lambda b, i: (0, 0))],
        out_specs=per_tile(Dh, NSA_N),
        out_shape=jax.ShapeDtypeStruct((B_ * G * n_qt, Dh, NSA_N), F32),
        name="nsa_attention",
    )(qt, kc, vct, tiles(k_slc_aug, NSA_KV), vals(v_slc, NSA_KV), keys(k_win, NSA_TQ), vals(v_win, NSA_TQ), gt, ovl)
    return out.reshape(B_, G, n_qt, Dh, NSA_HPG, NSA_TQ).transpose(0, 2, 5, 1, 4, 3).reshape(B_, S_, H * Dh)


DENSE_TM = 256
SEC_QKV = 4 * BRANCH_WIDTH
SEC_KV = 6 * KV_WIDTH
SEC_MERGE = N_BRANCH * D_MODEL
SEC_SMALL = 256
IN_SECTIONS = (SEC_QKV, SEC_KV, SEC_MERGE, SEC_SMALL)


def _rms(x, g):
    return x * lax.rsqrt(jnp.mean(x * x, axis=-1, keepdims=True) + RMS_EPS) * g


def _row_spec(width):
    return pl.BlockSpec((DENSE_TM, width), lambda i: (i, 0))


def _whole_spec(shape):
    return pl.BlockSpec(shape, lambda i: (0,) * len(shape))


def _in_proj_kernel(x_ref, g_ref, w_ref, *o_refs):
    xn = _rms(x_ref[...], g_ref[...]).astype(BF16)
    off = 0
    for o_ref, width in zip(o_refs, IN_SECTIONS):
        o_ref[...] = jnp.dot(xn, w_ref[:, off:off + width], preferred_element_type=F32)
        off += width


def in_projection(x, g, w_in):
    t, d = x.shape
    fq, fk, fv, fl, nq, kc, vc, ksl, vsl, kwn, vwn, gl, ml = split_columns(w_in, IN_SPLITS)
    pad = lambda w: jnp.pad(w, ((0, 0), (0, 128 - w.shape[1])))
    w = jnp.concatenate([fq, fk, fv, nq, kc, vc, ksl, vsl, kwn, vwn, ml, pad(fl), pad(gl)], axis=1).astype(BF16)
    return pl.pallas_call(
        _in_proj_kernel,
        grid=(t // DENSE_TM,),
        in_specs=[_row_spec(d), _whole_spec((1, d)), _whole_spec(w.shape)],
        out_specs=[_row_spec(s) for s in IN_SECTIONS],
        out_shape=[jax.ShapeDtypeStruct((t, s), F32) for s in IN_SECTIONS],
        name="in_projection",
    )(x, g.reshape(1, d), w)


def _merge_kernel(x_ref, yf_ref, yn_ref, ml_ref, wb_ref, wo_ref, o_ref):
    g = jax.nn.sigmoid(ml_ref[...])
    up_f = jnp.dot(yf_ref[...].astype(BF16), wb_ref[0], preferred_element_type=F32)
    up_n = jnp.dot(yn_ref[...].astype(BF16), wb_ref[1], preferred_element_type=F32)
    merged = g[:, :D_MODEL] * up_f + g[:, D_MODEL:] * up_n
    o_ref[...] = x_ref[...] + jnp.dot(merged.astype(BF16), wo_ref[...], preferred_element_type=F32)


def merge_out(x, y_fox, y_nsa, merge_logit, w_branch, w_out):
    t, d = x.shape
    return pl.pallas_call(
        _merge_kernel,
        grid=(t // DENSE_TM,),
        in_specs=[_row_spec(d), _row_spec(BRANCH_WIDTH), _row_spec(BRANCH_WIDTH), _row_spec(SEC_MERGE),
                  _whole_spec(w_branch.shape), _whole_spec(w_out.shape)],
        out_specs=_row_spec(d),
        out_shape=jax.ShapeDtypeStruct((t, d), F32),
        name="merge_out",
    )(x, y_fox, y_nsa, merge_logit, w_branch.astype(BF16), w_out.astype(BF16))


def _peer_q_kernel(x_ref, g_ref, w_ref, hn_ref, q_ref):
    hn = _rms(x_ref[...], g_ref[...])
    hn_ref[...] = hn
    q = jnp.dot(hn.astype(BF16), w_ref[...], preferred_element_type=F32)
    d = PEER_QDIM // 2
    for j in range(2 * PEER_HEADS):
        q_ref[j] = q[:, j * d:(j + 1) * d]


def peer_query(x, g, wq):
    t, d = x.shape
    hp, dq = 2 * PEER_HEADS, PEER_QDIM // 2
    return pl.pallas_call(
        _peer_q_kernel,
        grid=(t // DENSE_TM,),
        in_specs=[_row_spec(d), _whole_spec((1, d)), _whole_spec(wq.shape)],
        out_specs=[_row_spec(d), pl.BlockSpec((hp, DENSE_TM, dq), lambda i: (0, i, 0))],
        out_shape=[jax.ShapeDtypeStruct((t, d), F32), jax.ShapeDtypeStruct((hp, t, dq), F32)],
        name="peer_query",
    )(x, g.reshape(1, d), wq.astype(BF16))


def _final_kernel(h_ref, p_ref, g_ref, o_ref):
    o_ref[...] = _rms(h_ref[...] + p_ref[...], g_ref[...])


def final_norm(h, p, g):
    t, d = h.shape
    return pl.pallas_call(
        _final_kernel,
        grid=(t // DENSE_TM,),
        in_specs=[_row_spec(d), _row_spec(d), _whole_spec((1, d))],
        out_specs=_row_spec(d),
        out_shape=jax.ShapeDtypeStruct((t, d), F32),
        name="final_norm",
    )(h, p, g.reshape(1, d))


N_CMP_ROWS = 256


def _compress_kernel(r_ref, pos_ref, w1_ref, w2_ref, o_ref):
    r = r_ref[...]
    nxt = jnp.concatenate([r[1:], jnp.zeros((1, r.shape[1]), F32)], axis=0)
    half = r.shape[1]
    hid = (jnp.dot((r + pos_ref[0:1, :]).astype(BF16), w1_ref[:half], preferred_element_type=F32)
           + jnp.dot((nxt + pos_ref[1:2, :]).astype(BF16), w1_ref[half:], preferred_element_type=F32))
    o_ref[...] = jnp.dot(_gelu(hid).astype(BF16), w2_ref[...], preferred_element_type=F32)


def compress_kv_pallas(kv, cmp_pos, w1, w2):
    _, B_, S_, G, Dh = kv.shape
    n_cmp = (S_ - CMP_LEN) // CMP_STRIDE + 1
    width = CMP_STRIDE * Dh
    rows = kv.transpose(0, 1, 3, 2, 4).reshape(2, B_ * G, S_ // CMP_STRIDE, width)
    out = pl.pallas_call(
        _compress_kernel,
        grid=(2, B_ * G),
        in_specs=[pl.BlockSpec((None, None, N_CMP_ROWS, width), lambda j, b: (j, b, 0, 0)),
                  pl.BlockSpec((None, 2, width), lambda j, b: (j, 0, 0)),
                  pl.BlockSpec((None, CMP_LEN * Dh, CMP_HIDDEN), lambda j, b: (j, 0, 0)),
                  pl.BlockSpec((None, CMP_HIDDEN, Dh), lambda j, b: (j, 0, 0))],
        out_specs=pl.BlockSpec((None, None, N_CMP_ROWS, Dh), lambda j, b: (j, b, 0, 0)),
        out_shape=jax.ShapeDtypeStruct((2, B_ * G, N_CMP_ROWS, Dh), F32),
        name="compress_kv",
    )(rows, cmp_pos.reshape(2, 2, width), w1.astype(BF16), w2.astype(BF16))
    return out[:, :, :n_cmp].reshape(2, B_, G, n_cmp, Dh).transpose(0, 1, 3, 2, 4)


def hybrid_mixer(x, norm_g, w_in, fox_f_bias, cmp_pos, cmp_w1, cmp_w2, w_branch, w_out):
    B_, S_, D = x.shape
    xf = x.reshape(B_ * S_, D)
    qkv, kvs, merge_logit, small = in_projection(xf, norm_g, w_in)
    heads = lambda t, n: t.reshape(B_, S_, n, HEAD_DIM)
    fq, fk, fv, nq = (qkv[:, i * BRANCH_WIDTH:(i + 1) * BRANCH_WIDTH] for i in range(4))
    kc, vc, ksl, vsl, kwn, vwn = (heads(kvs[:, i * KV_WIDTH:(i + 1) * KV_WIDTH], NSA_GROUPS) for i in range(6))
    f_logit = small[:, :FOX_HEADS].reshape(B_, S_, FOX_HEADS)
    gate_logit = small[:, 128:128 + NSA_HEADS * N_NSA_BRANCH].reshape(B_, S_, NSA_HEADS, N_NSA_BRANCH)
    log_f = jax.nn.log_sigmoid(f_logit + fox_f_bias.astype(F32))
    y_fox = fox_attention(heads(fq, FOX_HEADS), heads(fk, FOX_HEADS), heads(fv, FOX_HEADS), log_f)
    cos, sin = rope_tables(jnp.arange(S_, dtype=F32))
    q_nsa = apply_partial_rope(heads(nq, NSA_HEADS), cos, sin)
    k_slc = apply_partial_rope(ksl, cos, sin)
    k_win = apply_partial_rope(kwn, cos, sin)
    cmp = compress_kv_pallas(jnp.stack([kc, vc]), cmp_pos, cmp_w1, cmp_w2)
    n_cmp = cmp.shape[2]
    cmp_end = jnp.arange(n_cmp, dtype=F32) * CMP_STRIDE + (CMP_LEN - 1)
    k_cmp = apply_partial_rope(cmp[0], *rope_tables(cmp_end))
    y_nsa = nsa_attention(q_nsa, k_cmp, cmp[1], k_slc, vsl, k_win, vwn, gate_logit)
    return merge_out(xf, y_fox.reshape(B_ * S_, -1), y_nsa.reshape(B_ * S_, -1), merge_logit, w_branch, w_out)


def peer_ffn(h, norm_g, wq, subkeys, u, v):
    hn, q = peer_query(h, norm_g, wq)
    idx2, w = peer_topk(q, subkeys)
    coef = peer_u(idx2, hn, w, pack_table(u))
    return peer_v(idx2, coef, pack_table(v))


def kernel(x, norm_mix, w_in, fox_f_bias, nsa_cmp_pos, nsa_cmp_w1, nsa_cmp_w2, w_branch, w_out,
           norm_ffn, peer_wq, peer_subkeys, peer_u, peer_v, norm_final):
    B_, S_, D = x.shape
    assert norm_mix.shape[0] == 1, "single-layer trunk"
    h = hybrid_mixer(x, norm_mix[0], w_in[0], fox_f_bias[0], nsa_cmp_pos[0], nsa_cmp_w1[0], nsa_cmp_w2[0],
                     w_branch[0], w_out[0])
    p = peer_ffn(h, norm_ffn[0], peer_wq[0], peer_subkeys[0], peer_u[0], peer_v[0])
    return final_norm(h, p, norm_final).reshape(B_, S_, D)
```

```python
import jax
import jax.numpy as jnp
import numpy as np
from jax import lax
from jax.experimental import pallas as pl
from jax.experimental.pallas import tpu as pltpu

D_MODEL = 1024
HEAD_DIM = 64
FOX_HEADS = 8
NSA_HEADS = 8
NSA_GROUPS = 2
NSA_HPG = NSA_HEADS // NSA_GROUPS
BRANCH_WIDTH = 512
N_BRANCH = 2
N_NSA_BRANCH = 3
ROPE_DIM = HEAD_DIM // 4
ROPE_THETA = 500000.0
CMP_LEN = 32
CMP_STRIDE = 16
CMP_HIDDEN = 2 * HEAD_DIM
SLC_LEN = 64
SLC_TOP = 16
WINDOW = 512
FORCED_SCORE = 1e9
NEG_INF = -1e30
PEER_HEADS = 8
N_KEYS = 128
PEER_QDIM = 256
PEER_TOPK = 16
RMS_EPS = 1e-6
KV_WIDTH = NSA_GROUPS * HEAD_DIM
IN_SPLITS = (BRANCH_WIDTH, BRANCH_WIDTH, BRANCH_WIDTH, FOX_HEADS, BRANCH_WIDTH,
             KV_WIDTH, KV_WIDTH, KV_WIDTH, KV_WIDTH, KV_WIDTH, KV_WIDTH,
             NSA_HEADS * N_NSA_BRANCH, N_BRANCH * D_MODEL)


BF16 = jnp.bfloat16
F32 = jnp.float32

PEER_SLOTS = PEER_HEADS * PEER_TOPK
PEER_TT = 64
PEER_CHUNK_SLOTS = 32
PEER_U_UNROLL = 8
PEER_V_UNROLL = 8
ROW_SUB = D_MODEL // 2 // 128
VMEM_LIMIT_PEER = N_KEYS * N_KEYS * D_MODEL * 2 + 16 * 1024 * 1024


def pack_table(tab):
    n, d = tab.shape
    return pl.pallas_call(
        _pack_kernel,
        grid=(n // PACK_TM,),
        in_specs=[pl.BlockSpec((PACK_TM, d), lambda i: (i, 0))],
        out_specs=pl.BlockSpec((PACK_TM * ROW_SUB, 128), lambda i: (i, 0)),
        out_shape=jax.ShapeDtypeStruct((n * ROW_SUB, 128), jnp.uint32),
        name="pack_table",
    )(tab)


PACK_TM = 256


def _pack_kernel(x_ref, o_ref):
    bits = lax.bitcast_convert_type(x_ref[...].astype(jnp.bfloat16).astype(jnp.float32), jnp.uint32)
    for r in range(ROW_SUB):
        even = bits[:, (2 * r) * 128:(2 * r + 1) * 128]
        odd = bits[:, (2 * r + 1) * 128:(2 * r + 2) * 128]
        o_ref[pl.ds(r, PACK_TM, stride=ROW_SUB), :] = (even >> 16) | (odd & jnp.uint32(0xFFFF0000))


def _expert_slab(tab_ref, row):
    return tab_ref[pl.ds(pl.multiple_of(row, ROW_SUB), ROW_SUB), :]


def _unpack(w):
    lo = lax.bitcast_convert_type(w << 16, jnp.float32)
    hi = lax.bitcast_convert_type(w & jnp.uint32(0xFFFF0000), jnp.float32)
    return lo, hi


def _gelu(x):
    return 0.5 * x * (1.0 + lax.erf(x * (2.0 ** -0.5)))


def _peer_u_kernel(idx_a_ref, idx_b_ref, x_ref, w_ref, tab_ref, out_ref):
    lane = lax.broadcasted_iota(jnp.int32, (2 * ROW_SUB, 128), 1)
    lower = lax.broadcasted_iota(jnp.int32, (2 * ROW_SUB, 128), 0) >= ROW_SUB
    pair_lane = jnp.where(lower, lane - PEER_SLOTS // 2, lane)

    def token(t):
        row = x_ref[pl.ds(t, 1), :]
        chunks = [row[:, c * 128:(c + 1) * 128] for c in range(2 * ROW_SUB)]
        xl = jnp.concatenate(chunks[0::2] * 2, axis=0)
        xh = jnp.concatenate(chunks[1::2] * 2, axis=0)
        acc = jnp.zeros((2 * ROW_SUB, 128), jnp.float32)
        base = t * (PEER_SLOTS // 2)
        for s in range(PEER_SLOTS // 2):
            w = jnp.concatenate([_expert_slab(tab_ref, idx_a_ref[base + s]),
                                 _expert_slab(tab_ref, idx_b_ref[base + s])], axis=0)
            lo, hi = _unpack(w)
            part = jnp.sum(lo * xl + hi * xh, axis=1, keepdims=True)
            acc = jnp.where(pair_lane == s, part, acc)
        out_ref[pl.ds(t, 1), :] = w_ref[pl.ds(t, 1), :] * _gelu(jnp.sum(acc, axis=0, keepdims=True))

    def trip(i, carry):
        for j in range(PEER_U_UNROLL):
            token(i * PEER_U_UNROLL + j)
        return carry

    lax.fori_loop(0, PEER_TT // PEER_U_UNROLL, trip, 0)


N_CHUNK = 2 * ROW_SUB


def _peer_v_kernel(idx_a_ref, idx_b_ref, coef_ref, expand_ref, tab_ref, out_ref, ce_hi_ref, ce_lo_ref):
    half_slots = PEER_SLOTS // 2
    width = PEER_SLOTS * N_CHUNK
    coef = coef_ref[...]
    c_hi = _trunc_bf16(coef)
    expand = expand_ref[...]
    ce_hi_ref[...] = jnp.dot(c_hi.astype(BF16), expand, preferred_element_type=F32)
    ce_lo_ref[...] = jnp.dot((coef - c_hi).astype(BF16), expand, preferred_element_type=F32)
    own_chunk = ((lax.broadcasted_iota(jnp.int32, (N_CHUNK, width), 1) & (N_CHUNK - 1))
                 == lax.broadcasted_iota(jnp.int32, (N_CHUNK, width), 0))

    def token(t):
        base = t * half_slots
        slabs = [jnp.concatenate([_expert_slab(tab_ref, idx_a_ref[base + s]),
                                  _expert_slab(tab_ref, idx_b_ref[base + s])], axis=0) for s in range(half_slots)]
        gathered = pltpu.bitcast(jnp.concatenate(slabs, axis=0), BF16)
        sel = lambda ref: jnp.where(own_chunk, ref[pl.ds(t, 1), :], 0.0).astype(BF16)
        c = jnp.concatenate([sel(ce_hi_ref), sel(ce_lo_ref)], axis=0)
        o = jnp.dot(c, gathered, preferred_element_type=F32)
        o = o[:N_CHUNK] + o[N_CHUNK:]
        out_ref[pl.ds(t, 1), :] = jnp.concatenate([o[r:r + 1, :] for r in range(N_CHUNK)], axis=1)

    def trip(i, carry):
        for j in range(PEER_V_UNROLL):
            token(i * PEER_V_UNROLL + j)
        return carry

    lax.fori_loop(0, PEER_TT // PEER_V_UNROLL, trip, 0)


def _smem_spec():
    return pl.BlockSpec((PEER_TT * PEER_SLOTS // 2,), lambda i: (i,), memory_space=pltpu.SMEM)


def _slot_halves(a):
    return a[:, :PEER_SLOTS // 2].reshape(-1), a[:, PEER_SLOTS // 2:].reshape(-1)


def _table_spec(rows):
    return pl.BlockSpec((rows, 128), lambda i: (0, 0), pipeline_mode=pl.Buffered(1))


def peer_u(idx, x, w, tab):
    t = x.shape[0]
    return pl.pallas_call(
        _peer_u_kernel,
        grid=(t // PEER_TT,),
        in_specs=[_smem_spec(), _smem_spec(), pl.BlockSpec((PEER_TT, D_MODEL), lambda i: (i, 0)),
                  pl.BlockSpec((PEER_TT, PEER_SLOTS), lambda i: (i, 0)), _table_spec(tab.shape[0])],
        out_specs=pl.BlockSpec((PEER_TT, PEER_SLOTS), lambda i: (i, 0)),
        out_shape=jax.ShapeDtypeStruct((t, PEER_SLOTS), jnp.float32),
        compiler_params=pltpu.CompilerParams(vmem_limit_bytes=VMEM_LIMIT_PEER),
        name="peer_u",
    )(*_slot_halves(idx), x, w, tab)


def peer_v(idx, coef, tab):
    t = idx.shape[0]
    width = PEER_SLOTS * N_CHUNK
    half = PEER_SLOTS // 2
    col_slot = np.arange(width) // N_CHUNK
    col_slot = (col_slot // 2) + half * (col_slot % 2)
    expand = jnp.asarray(col_slot[None, :] == np.arange(PEER_SLOTS)[:, None], dtype=BF16)
    return pl.pallas_call(
        _peer_v_kernel,
        grid=(t // PEER_TT,),
        in_specs=[_smem_spec(), _smem_spec(), pl.BlockSpec((PEER_TT, PEER_SLOTS), lambda i: (i, 0)),
                  pl.BlockSpec((PEER_SLOTS, width), lambda i: (0, 0)), _table_spec(tab.shape[0])],
        out_specs=pl.BlockSpec((PEER_TT, D_MODEL), lambda i: (i, 0)),
        out_shape=jax.ShapeDtypeStruct((t, D_MODEL), jnp.float32),
        scratch_shapes=[pltpu.VMEM((PEER_TT, width), F32), pltpu.VMEM((PEER_TT, width), F32)],
        compiler_params=pltpu.CompilerParams(vmem_limit_bytes=VMEM_LIMIT_PEER),
        name="peer_v",
    )(*_slot_halves(idx), coef, expand, tab)


PK_TT = 256
PK_A_BLOCKS = ((0, 16), (1, 8), (2, 5), (3, 4))
PK_B_BLOCKS = ((0, 4, 16), (1, 4, 8), (2, 4, 5))
NEG_HUGE = -3.0e38
PK_HEADS_PER_TRIP = 2


def _extract_top(problems, flat, n):
    vals = [[] for _ in problems]
    picked = [[] for _ in problems]
    cur = [v for v, _ in problems]
    for _ in range(n):
        for i, (_, payload) in enumerate(problems):
            m = jnp.max(cur[i], axis=0, keepdims=True)
            pos = jnp.min(jnp.where(cur[i] == m, flat, 1e9), axis=0, keepdims=True)
            hit = flat == pos
            picked[i].append(pos if payload is None else jnp.sum(jnp.where(hit, payload, 0.0), axis=0, keepdims=True))
            cur[i] = jnp.where(hit, NEG_HUGE, cur[i])
            vals[i].append(m)
    return [jnp.concatenate(v, axis=0) for v in vals], [jnp.concatenate(p, axis=0) for p in picked]


def _candidate_grid(first, second, combine):
    up8 = lambda n: -(-n // 8) * 8
    return jnp.concatenate([combine(first[a:a + 1], second[0:up8(nb)]) for a, nb in PK_A_BLOCKS]
                           + [combine(first[0:up8(a1)], second[b:b + 1]) for b, _, a1 in PK_B_BLOCKS], axis=0)


def _candidate_valid_and_flat():
    up8 = lambda n: -(-n // 8) * 8
    valid, flat = [], []
    for a, nb in PK_A_BLOCKS:
        r = lax.broadcasted_iota(jnp.int32, (up8(nb), PK_TT), 0)
        valid.append(r < nb)
        flat.append(a * PEER_TOPK + r)
    for b, a0, a1 in PK_B_BLOCKS:
        r = lax.broadcasted_iota(jnp.int32, (up8(a1), PK_TT), 0)
        valid.append((r >= a0) & (r < a1))
        flat.append(r * PEER_TOPK + b)
    valid = jnp.concatenate(valid, axis=0)
    return valid, jnp.where(valid, jnp.concatenate(flat, axis=0).astype(F32), 1e9)


def _peer_topk_kernel(q_ref, ka_ref, eid_ref, w_ref):
    key_row = lax.broadcasted_iota(jnp.int32, (N_KEYS, PK_TT), 0).astype(F32)
    valid, flat = _candidate_valid_and_flat()

    def heads(i, carry):
        scores = []
        for j in range(2 * PK_HEADS_PER_TRIP):
            q = q_ref[2 * PK_HEADS_PER_TRIP * i + j]
            q_hi = _trunc_bf16(q)
            qa = jnp.concatenate([q_hi.astype(BF16), (q - q_hi).astype(BF16), q_hi.astype(BF16)], axis=1)
            scores.append(lax.dot_general(ka_ref[2 * PK_HEADS_PER_TRIP * i + j], qa, (((1,), (1,)), ((), ())),
                                          preferred_element_type=F32))
        sub_s, sub_i = _extract_top([(s, None) for s in scores], key_row, PEER_TOPK)
        cands = []
        for j in range(PK_HEADS_PER_TRIP):
            s1, s2, i1, i2 = sub_s[2 * j], sub_s[2 * j + 1], sub_i[2 * j], sub_i[2 * j + 1]
            cands.append((jnp.where(valid, _candidate_grid(s1, s2, lambda x, y: x + y), NEG_HUGE),
                          _candidate_grid(i1, i2, lambda x, y: x * N_KEYS + y)))
        top_s, top_e = _extract_top(cands, flat, PEER_TOPK)
        for j in range(PK_HEADS_PER_TRIP):
            e = jnp.exp(top_s[j] - top_s[j][0:1])
            w_ref[PK_HEADS_PER_TRIP * i + j] = e / jnp.sum(e, axis=0, keepdims=True)
            eid_ref[PK_HEADS_PER_TRIP * i + j] = top_e[j].astype(jnp.int32) * ROW_SUB
        return carry

    lax.fori_loop(0, PEER_HEADS // PK_HEADS_PER_TRIP, heads, 0)


def peer_topk(qt, subkeys):
    hp, t, d = qt.shape
    k_hi, k_lo = _split2(subkeys.reshape(hp, N_KEYS, d))
    ka = jnp.concatenate([k_hi, k_hi, k_lo], axis=2)
    eid, w = pl.pallas_call(
        _peer_topk_kernel,
        grid=(t // PK_TT,),
        in_specs=[pl.BlockSpec((hp, PK_TT, d), lambda i: (0, i, 0)),
                  pl.BlockSpec((hp, N_KEYS, 3 * d), lambda i: (0, 0, 0))],
        out_specs=[pl.BlockSpec((PEER_HEADS, PEER_TOPK, PK_TT), lambda i: (0, 0, i)),
                   pl.BlockSpec((PEER_HEADS, PEER_TOPK, PK_TT), lambda i: (0, 0, i))],
        out_shape=[jax.ShapeDtypeStruct((PEER_HEADS, PEER_TOPK, t), jnp.int32),
                   jax.ShapeDtypeStruct((PEER_HEADS, PEER_TOPK, t), F32)],
        name="peer_topk",
    )(qt, ka)
    to_slots = lambda a: a.transpose(2, 0, 1).reshape(t, PEER_SLOTS)
    return to_slots(eid), to_slots(w)


def split_columns(t, sizes):
    offs = np.cumsum((0,) + tuple(sizes))
    return [t[..., int(a):int(b)] for a, b in zip(offs[:-1], offs[1:])]


def rope_tables(pos):
    inv = jnp.power(ROPE_THETA, -jnp.arange(0, ROPE_DIM, 2, dtype=jnp.float32) / ROPE_DIM)
    ang = pos[:, None] * inv[None, :]
    return jnp.cos(ang), jnp.sin(ang)


def apply_partial_rope(x, cos, sin):
    half = ROPE_DIM // 2
    xr = x[..., :ROPE_DIM].astype(jnp.float32)
    x1, x2 = xr[..., :half], xr[..., half:]
    c, s = cos[:, None, :], sin[:, None, :]
    rot = jnp.concatenate([x1 * c - x2 * s, x1 * s + x2 * c], axis=-1)
    return jnp.concatenate([rot.astype(x.dtype), x[..., ROPE_DIM:]], axis=-1)


def _logits(ks, qts):
    return [jnp.dot(k, qt, preferred_element_type=F32) for k, qt in zip(ks, qts)]


def _flash_update(ss, vts, mask, carries):
    stats, ps = [], []
    for s, (m, l, _) in zip(ss, carries):
        s_vis = s if mask is None else jnp.where(mask, s, NEG_INF)
        m_new = jnp.maximum(m, jnp.max(s_vis, axis=0, keepdims=True))
        alpha = jnp.exp(m - m_new)
        p = jnp.exp(s - m_new)
        if mask is not None:
            p = jnp.where(mask, p, 0.0)
        stats.append((m_new, alpha, alpha * l + jnp.sum(p, axis=0, keepdims=True)))
        ps.append(p.astype(BF16))
    return tuple((m_new, l, alpha * acc + jnp.dot(vt, p, preferred_element_type=F32))
                 for (m_new, alpha, l), p, vt, (_, _, acc) in zip(stats, ps, vts, carries))


def _causal_flash(key_tile, value_tile, qts, n_full, last_mask, n):
    def body(kt, carries):
        return _flash_update(_logits(key_tile(kt), qts), value_tile(kt), None, carries)

    carries = lax.fori_loop(0, n_full, body, tuple(_flash_init(n) for _ in qts))
    return _flash_update(_logits(key_tile(n_full), qts), value_tile(n_full), last_mask, carries)


def _flash_init(n):
    return (jnp.full((1, n), NEG_INF, F32), jnp.zeros((1, n), F32), jnp.zeros((HEAD_DIM, n), F32))


FOX_TQ = 256
FOX_KV = 512
FOX_DK = 256


FOX_HP = 4


def _fox_kernel(qt_ref, k_ref, vt_ref, o_ref):
    qi = pl.program_id(1)
    q0 = qi * FOX_TQ
    qts = [qt_ref[h] for h in range(FOX_HP)]
    qpos = q0 + lax.broadcasted_iota(jnp.int32, (1, FOX_TQ), 1)
    krow = lax.broadcasted_iota(jnp.int32, (FOX_KV, FOX_TQ), 0)

    n_full = q0 // FOX_KV
    causal = (n_full * FOX_KV + krow) <= qpos
    carry = _causal_flash(lambda kt: [k_ref[h, kt] for h in range(FOX_HP)],
                          lambda kt: [vt_ref[h, kt] for h in range(FOX_HP)], qts, n_full, causal, FOX_TQ)
    for h in range(FOX_HP):
        _, l, acc = carry[h]
        o_ref[h] = acc / l


def _trunc_bf16(x):
    bits = lax.bitcast_convert_type(x, jnp.uint32) & jnp.uint32(0xFFFF0000)
    return lax.bitcast_convert_type(bits, F32)


def _split3(c):
    c1 = _trunc_bf16(c)
    r = c - c1
    c2 = _trunc_bf16(r)
    return c1.astype(BF16), c2.astype(BF16), (r - c2).astype(BF16)


def _split2(x):
    hi = _trunc_bf16(x)
    return hi.astype(BF16), (x - hi).astype(BF16)


def _aug_q(q):
    hi, lo = _split2(q)
    return jnp.concatenate([hi, hi, lo], axis=-1)


def _aug_k(k):
    hi, lo = _split2(k)
    return jnp.concatenate([hi, lo, hi], axis=-1)


def fox_attention(q, k, v, log_f):
    B_, S_, H, Dh = q.shape
    KV_TILE = FOX_KV
    n_qt, n_kt = S_ // FOX_TQ, S_ // KV_TILE
    c = jnp.cumsum(log_f, axis=1)
    c1, c2, c3 = _split3(c)
    j = jnp.arange(FOX_DK - 3 * Dh)
    pick = lambda t: t[..., None].astype(F32)
    terms = lambda first: jnp.where(j == first, pick(c1), jnp.where(j == first + 1, pick(c2), pick(c3)))
    q_extra = jnp.where(j < 3, 1.0, jnp.where(j < 6, terms(3), 0.0)).astype(BF16)
    k_extra = jnp.where(j < 3, -terms(0), jnp.where(j < 6, 1.0, 0.0)).astype(BF16)
    qa = jnp.concatenate([_aug_q(q * (Dh ** -0.5)), q_extra], axis=-1)
    ka = jnp.concatenate([_aug_k(k), k_extra], axis=-1)
    HP, HG = FOX_HP, H // FOX_HP
    qt = qa.reshape(B_, n_qt, FOX_TQ, HG, HP, FOX_DK).transpose(0, 3, 1, 4, 5, 2).reshape(B_ * HG * n_qt, HP, FOX_DK, FOX_TQ)
    kk = ka.reshape(B_, n_kt, KV_TILE, HG, HP, FOX_DK).transpose(0, 3, 4, 1, 2, 5).reshape(B_ * HG, HP, n_kt, KV_TILE, FOX_DK)
    vt = v.astype(BF16).reshape(B_, n_kt, KV_TILE, HG, HP, Dh).transpose(0, 3, 4, 1, 5, 2).reshape(B_ * HG, HP, n_kt, Dh, KV_TILE)
    out = pl.pallas_call(
        _fox_kernel,
        grid=(B_ * HG, n_qt),
        in_specs=[pl.BlockSpec((None, HP, FOX_DK, FOX_TQ), lambda b, i: (b * n_qt + i, 0, 0, 0)),
                  pl.BlockSpec((None, HP, n_kt, KV_TILE, FOX_DK), lambda b, i: (b, 0, 0, 0, 0)),
                  pl.BlockSpec((None, HP, n_kt, Dh, KV_TILE), lambda b, i: (b, 0, 0, 0, 0))],
        out_specs=pl.BlockSpec((None, HP, Dh, FOX_TQ), lambda b, i: (b * n_qt + i, 0, 0, 0)),
        out_shape=jax.ShapeDtypeStruct((B_ * HG * n_qt, HP, Dh, FOX_TQ), F32),
        name="fox_attention",
    )(qt, kk, vt)
    return out.reshape(B_, HG, n_qt, HP, Dh, FOX_TQ)


NSA_TQ = 128
NSA_N = NSA_HPG * NSA_TQ
N_CMP_PAD = 256
N_SLC = 64
NSA_DK = 3 * HEAD_DIM
NSA_KV = 512
WIN_TILES = WINDOW // NSA_TQ + 1
WIN_KEYS = WIN_TILES * NSA_TQ


def _nsa_kernel(qt_ref, kc_ref, vct_ref, ks_ref, vst_ref, kw_ref, vwt_ref, gate_ref, ovl_ref, o_ref):
    qi = pl.program_id(1)
    q0 = qi * NSA_TQ
    qt = qt_ref[...]
    lane = lax.broadcasted_iota(jnp.int32, (1, NSA_N), 1)
    qpos = q0 + (lane & (NSA_TQ - 1))

    hn = NSA_N // 2
    qts = (qt[:, :hn], qt[:, hn:])
    qpos_h = qpos[:, :hn]

    w0 = jnp.maximum(qi - WINDOW // NSA_TQ, 0)
    k_w = jnp.concatenate([kw_ref[w0 + i] for i in range(WIN_TILES)], axis=0)
    vt_w = jnp.concatenate([vwt_ref[w0 + i] for i in range(WIN_TILES)], axis=1)
    s = jnp.dot(kc_ref[...], qt, preferred_element_type=F32)
    s_w = [jnp.dot(k_w, qts[h], preferred_element_type=F32) for h in range(2)]

    cmp_end = lax.broadcasted_iota(jnp.int32, (N_CMP_PAD, NSA_N), 0) * CMP_STRIDE + (CMP_LEN - 1)
    mc = cmp_end <= qpos
    m = jnp.max(jnp.where(mc, s, NEG_INF), axis=0, keepdims=True)
    p = jnp.where(mc, jnp.exp(s - m), 0.0)
    l = jnp.sum(p, axis=0, keepdims=True)
    pc = p * jnp.where(l > 0.0, 1.0 / l, 0.0)
    o_cmp = jnp.dot(vct_ref[...], pc.astype(BF16), preferred_element_type=F32)
    pcs = (pc[:, 0:NSA_TQ] + pc[:, NSA_TQ:2 * NSA_TQ]) + (pc[:, 2 * NSA_TQ:3 * NSA_TQ] + pc[:, 3 * NSA_TQ:4 * NSA_TQ])
    pcs_hi = pcs.astype(BF16)
    pcs_lo = (pcs - pcs_hi.astype(F32)).astype(BF16)
    ovl = ovl_ref[...]
    imp = jnp.dot(ovl, pcs_hi, preferred_element_type=F32) + jnp.dot(ovl, pcs_lo, preferred_element_type=F32)

    rel = qpos_h - (w0 * NSA_TQ + lax.broadcasted_iota(jnp.int32, (WIN_KEYS, hn), 0))
    in_win = (rel >= 0) & (rel < WINDOW)
    outs_w = []
    for h in range(2):
        m = jnp.max(jnp.where(in_win, s_w[h], NEG_INF), axis=0, keepdims=True)
        p = jnp.where(in_win, jnp.exp(s_w[h] - m), 0.0)
        l = jnp.sum(p, axis=0, keepdims=True)
        outs_w.append(jnp.dot(vt_w, p.astype(BF16), preferred_element_type=F32) / l)
    acc_w = jnp.concatenate(outs_w, axis=1)

    blk = lax.broadcasted_iota(jnp.int32, (N_SLC, NSA_TQ), 0)
    q_blk = (q0 + lax.broadcasted_iota(jnp.int32, (N_SLC, NSA_TQ), 1)) >> 6
    forced = (blk == 0) | (blk == q_blk) | (blk == q_blk - 1)
    score = jnp.where(forced, FORCED_SCORE, jnp.where(blk <= q_blk, imp, -1.0))
    rank = jnp.zeros((N_SLC, NSA_TQ), jnp.int32)
    for i in range(N_SLC):
        row = score[i:i + 1, :]
        beats = (row > score) | ((row == score) & (blk > i))
        rank = rank + beats.astype(jnp.int32)
    bias = jnp.where(rank < SLC_TOP, 0.0, NEG_INF).astype(BF16)
    bias = jnp.concatenate([bias] * (NSA_HPG // 2), axis=1)
    qts_s = [jnp.concatenate([qh, bias], axis=0) for qh in qts]
    krow = lax.broadcasted_iota(jnp.int32, (NSA_KV, hn), 0)

    n_full = q0 // NSA_KV
    causal = (n_full * NSA_KV + krow) <= qpos_h
    slc = _causal_flash(lambda kt: [ks_ref[kt]] * 2, lambda kt: [vst_ref[kt]] * 2, qts_s, n_full, causal, hn)
    acc_s = jnp.concatenate([slc[0][2] / slc[0][1], slc[1][2] / slc[1][1]], axis=1)

    g = jax.nn.sigmoid(gate_ref[...])
    o_ref[...] = g[0:1] * o_cmp + g[1:2] * acc_s + g[2:3] * acc_w


def nsa_attention(q, k_cmp, v_cmp, k_slc, v_slc, k_win, v_win, gate_logit):
    B_, S_, H, Dh = q.shape
    G = NSA_GROUPS
    n_qt = S_ // NSA_TQ
    n_cmp = k_cmp.shape[1]
    cs = np.arange(N_CMP_PAD)[None, :] * CMP_STRIDE
    ss = np.arange(N_SLC)[:, None] * SLC_LEN
    ov = np.clip(np.minimum(cs + CMP_LEN, ss + SLC_LEN) - np.maximum(cs, ss), 0, None) / CMP_LEN
    ov[:, n_cmp:] = 0.0
    ovl = jnp.asarray(ov, dtype=BF16)
    qs = _aug_q(q * (Dh ** -0.5)).reshape(B_, n_qt, NSA_TQ, G, NSA_HPG, NSA_DK)
    qt = qs.transpose(0, 3, 1, 5, 4, 2).reshape(B_ * G * n_qt, NSA_DK, NSA_N)
    gt = gate_logit.astype(F32).reshape(B_, n_qt, NSA_TQ, G, NSA_HPG, N_NSA_BRANCH)
    gt = gt.transpose(0, 3, 1, 5, 4, 2).reshape(B_ * G * n_qt, N_NSA_BRANCH, NSA_N)
    tiles = lambda t, kv: t.reshape(B_, S_ // kv, kv, G, -1).transpose(0, 3, 1, 2, 4).reshape(B_ * G, S_ // kv, kv, t.shape[-1])
    keys = lambda t, kv: tiles(_aug_k(t), kv)
    block_onehot = (jnp.arange(S_)[:, None] // SLC_LEN == jnp.arange(N_SLC)[None, :]).astype(BF16)
    k_slc_aug = jnp.concatenate([_aug_k(k_slc), jnp.broadcast_to(block_onehot[None, :, None, :], (B_, S_, G, N_SLC))], axis=-1)
    vals = lambda t, kv: t.astype(BF16).reshape(B_, S_ // kv, kv, G, Dh).transpose(0, 3, 1, 4, 2).reshape(B_ * G, S_ // kv, Dh, kv)
    padc = ((0, 0), (0, N_CMP_PAD - n_cmp), (0, 0), (0, 0))
    kc = _aug_k(jnp.pad(k_cmp, padc)).transpose(0, 2, 1, 3).reshape(B_ * G, N_CMP_PAD, NSA_DK)
    vct = jnp.pad(v_cmp, padc).astype(BF16).transpose(0, 2, 3, 1).reshape(B_ * G, Dh, N_CMP_PAD)
    per_bg = lambda *blk: pl.BlockSpec((None,) + blk, lambda b, i: (b,) + (0,) * len(blk))
    per_tile = lambda *blk: pl.BlockSpec((None,) + blk, lambda b, i: (b * n_qt + i,) + (0,) * len(blk))
    out = pl.pallas_call(
        _nsa_kernel,
        grid=(B_ * G, n_qt),
        in_specs=[per_tile(NSA_DK, NSA_N),
                  per_bg(N_CMP_PAD, NSA_DK), per_bg(Dh, N_CMP_PAD),
                  per_bg(S_ // NSA_KV, NSA_KV, NSA_DK + N_SLC), per_bg(S_ // NSA_KV, Dh, NSA_KV),
                  per_bg(n_qt, NSA_TQ, NSA_DK), per_bg(n_qt, Dh, NSA_TQ),
                  per_tile(N_NSA_BRANCH, NSA_N),
                  pl.BlockSpec((N_SLC, N_CMP_PAD), lambda b, i: (0, 0))],
        out_specs=per_tile(Dh, NSA_N),
        out_shape=jax.ShapeDtypeStruct((B_ * G * n_qt, Dh, NSA_N), F32),
        name="nsa_attention",
    )(qt, kc, vct, tiles(k_slc_aug, NSA_KV), vals(v_slc, NSA_KV), keys(k_win, NSA_TQ), vals(v_win, NSA_TQ), gt, ovl)
    return out.reshape(B_, G, n_qt, Dh, NSA_N)


DENSE_TM = 256
SEC_QKV = 4 * BRANCH_WIDTH
SEC_KV = 6 * KV_WIDTH
SEC_MERGE = N_BRANCH * D_MODEL
SEC_SMALL = 256
IN_SECTIONS = (SEC_QKV, SEC_KV, SEC_MERGE, SEC_SMALL)


def _rms(x, g):
    return x * lax.rsqrt(jnp.mean(x * x, axis=-1, keepdims=True) + RMS_EPS) * g


def _row_spec(width):
    return pl.BlockSpec((DENSE_TM, width), lambda i: (i, 0))


def _whole_spec(shape):
    return pl.BlockSpec(shape, lambda i: (0,) * len(shape))


def _in_proj_kernel(x_ref, g_ref, w_ref, *o_refs):
    xn = _rms(x_ref[...], g_ref[...]).astype(BF16)
    off = 0
    for o_ref, width in zip(o_refs, IN_SECTIONS):
        o_ref[...] = jnp.dot(xn, w_ref[:, off:off + width], preferred_element_type=F32)
        off += width


def in_projection(x, g, w_in):
    t, d = x.shape
    fq, fk, fv, fl, nq, kc, vc, ksl, vsl, kwn, vwn, gl, ml = split_columns(w_in, IN_SPLITS)
    pad = lambda w: jnp.pad(w, ((0, 0), (0, 128 - w.shape[1])))
    w = jnp.concatenate([fq, fk, fv, nq, kc, vc, ksl, vsl, kwn, vwn, ml, pad(fl), pad(gl)], axis=1).astype(BF16)
    return pl.pallas_call(
        _in_proj_kernel,
        grid=(t // DENSE_TM,),
        in_specs=[_row_spec(d), _whole_spec((1, d)), _whole_spec(w.shape)],
        out_specs=[_row_spec(s) for s in IN_SECTIONS],
        out_shape=[jax.ShapeDtypeStruct((t, s), F32) for s in IN_SECTIONS],
        name="in_projection",
    )(x, g.reshape(1, d), w)


def _merge_kernel(x_ref, yf_ref, yn_ref, ml_ref, wb_ref, wo_ref, o_ref):
    yf_t = yf_ref[...].reshape(BRANCH_WIDTH, DENSE_TM)
    yn_t = jnp.concatenate(
        [jnp.concatenate([yn_ref[g, j, :, h * NSA_TQ:(h + 1) * NSA_TQ]
                          for g in range(NSA_GROUPS) for h in range(NSA_HPG)], axis=0)
         for j in range(DENSE_TM // NSA_TQ)], axis=1)
    g = jax.nn.sigmoid(ml_ref[...])
    up_f = jnp.dot(yf_t.T.astype(BF16), wb_ref[0], preferred_element_type=F32)
    up_n = jnp.dot(yn_t.T.astype(BF16), wb_ref[1], preferred_element_type=F32)
    merged = g[:, :D_MODEL] * up_f + g[:, D_MODEL:] * up_n
    o_ref[...] = x_ref[...] + jnp.dot(merged.astype(BF16), wo_ref[...], preferred_element_type=F32)


def merge_out(x, y_fox, y_nsa, merge_logit, w_branch, w_out):
    t, d = x.shape
    assert FOX_TQ == DENSE_TM and DENSE_TM % NSA_TQ == 0
    tiles_per_seq = y_fox.shape[2]
    nsa_per_tile = DENSE_TM // NSA_TQ
    yf_spec = pl.BlockSpec((None,) + y_fox.shape[1:2] + (None,) + y_fox.shape[3:],
                           lambda i: (i // tiles_per_seq, 0, i % tiles_per_seq, 0, 0, 0))
    yn_spec = pl.BlockSpec((None, NSA_GROUPS, nsa_per_tile, HEAD_DIM, NSA_N),
                           lambda i: (i // tiles_per_seq, 0, i % tiles_per_seq, 0, 0))
    return pl.pallas_call(
        _merge_kernel,
        grid=(t // DENSE_TM,),
        in_specs=[_row_spec(d), yf_spec, yn_spec, _row_spec(SEC_MERGE),
                  _whole_spec(w_branch.shape), _whole_spec(w_out.shape)],
        out_specs=_row_spec(d),
        out_shape=jax.ShapeDtypeStruct((t, d), F32),
        name="merge_out",
    )(x, y_fox, y_nsa, merge_logit, w_branch.astype(BF16), w_out.astype(BF16))


def _peer_q_kernel(x_ref, g_ref, w_ref, hn_ref, q_ref):
    hn = _rms(x_ref[...], g_ref[...])
    hn_ref[...] = hn
    q = jnp.dot(hn.astype(BF16), w_ref[...], preferred_element_type=F32)
    d = PEER_QDIM // 2
    for j in range(2 * PEER_HEADS):
        q_ref[j] = q[:, j * d:(j + 1) * d]


def peer_query(x, g, wq):
    t, d = x.shape
    hp, dq = 2 * PEER_HEADS, PEER_QDIM // 2
    return pl.pallas_call(
        _peer_q_kernel,
        grid=(t // DENSE_TM,),
        in_specs=[_row_spec(d), _whole_spec((1, d)), _whole_spec(wq.shape)],
        out_specs=[_row_spec(d), pl.BlockSpec((hp, DENSE_TM, dq), lambda i: (0, i, 0))],
        out_shape=[jax.ShapeDtypeStruct((t, d), F32), jax.ShapeDtypeStruct((hp, t, dq), F32)],
        name="peer_query",
    )(x, g.reshape(1, d), wq.astype(BF16))


def _final_kernel(h_ref, p_ref, g_ref, o_ref):
    o_ref[...] = _rms(h_ref[...] + p_ref[...], g_ref[...])


def final_norm(h, p, g):
    t, d = h.shape
    return pl.pallas_call(
        _final_kernel,
        grid=(t // DENSE_TM,),
        in_specs=[_row_spec(d), _row_spec(d), _whole_spec((1, d))],
        out_specs=_row_spec(d),
        out_shape=jax.ShapeDtypeStruct((t, d), F32),
        name="final_norm",
    )(h, p, g.reshape(1, d))


N_CMP_ROWS = 256


def _compress_kernel(r_ref, pos_ref, w1_ref, w2_ref, o_ref):
    r = r_ref[...]
    nxt = jnp.concatenate([r[1:], jnp.zeros((1, r.shape[1]), F32)], axis=0)
    half = r.shape[1]
    hid = (jnp.dot((r + pos_ref[0:1, :]).astype(BF16), w1_ref[:half], preferred_element_type=F32)
           + jnp.dot((nxt + pos_ref[1:2, :]).astype(BF16), w1_ref[half:], preferred_element_type=F32))
    o_ref[...] = jnp.dot(_gelu(hid).astype(BF16), w2_ref[...], preferred_element_type=F32)


def compress_kv_pallas(kv, cmp_pos, w1, w2):
    _, B_, S_, G, Dh = kv.shape
    n_cmp = (S_ - CMP_LEN) // CMP_STRIDE + 1
    width = CMP_STRIDE * Dh
    rows = kv.transpose(0, 1, 3, 2, 4).reshape(2, B_ * G, S_ // CMP_STRIDE, width)
    out = pl.pallas_call(
        _compress_kernel,
        grid=(2, B_ * G),
        in_specs=[pl.BlockSpec((None, None, N_CMP_ROWS, width), lambda j, b: (j, b, 0, 0)),
                  pl.BlockSpec((None, 2, width), lambda j, b: (j, 0, 0)),
                  pl.BlockSpec((None, CMP_LEN * Dh, CMP_HIDDEN), lambda j, b: (j, 0, 0)),
                  pl.BlockSpec((None, CMP_HIDDEN, Dh), lambda j, b: (j, 0, 0))],
        out_specs=pl.BlockSpec((None, None, N_CMP_ROWS, Dh), lambda j, b: (j, b, 0, 0)),
        out_shape=jax.ShapeDtypeStruct((2, B_ * G, N_CMP_ROWS, Dh), F32),
        name="compress_kv",
    )(rows, cmp_pos.reshape(2, 2, width), w1.astype(BF16), w2.astype(BF16))
    return out[:, :, :n_cmp].reshape(2, B_, G, n_cmp, Dh).transpose(0, 1, 3, 2, 4)


def hybrid_mixer(x, norm_g, w_in, fox_f_bias, cmp_pos, cmp_w1, cmp_w2, w_branch, w_out):
    B_, S_, D = x.shape
    xf = x.reshape(B_ * S_, D)
    qkv, kvs, merge_logit, small = in_projection(xf, norm_g, w_in)
    heads = lambda t, n: t.reshape(B_, S_, n, HEAD_DIM)
    fq, fk, fv, nq = (qkv[:, i * BRANCH_WIDTH:(i + 1) * BRANCH_WIDTH] for i in range(4))
    kc, vc, ksl, vsl, kwn, vwn = (heads(kvs[:, i * KV_WIDTH:(i + 1) * KV_WIDTH], NSA_GROUPS) for i in range(6))
    f_logit = small[:, :FOX_HEADS].reshape(B_, S_, FOX_HEADS)
    gate_logit = small[:, 128:128 + NSA_HEADS * N_NSA_BRANCH].reshape(B_, S_, NSA_HEADS, N_NSA_BRANCH)
    log_f = jax.nn.log_sigmoid(f_logit + fox_f_bias.astype(F32))
    y_fox = fox_attention(heads(fq, FOX_HEADS), heads(fk, FOX_HEADS), heads(fv, FOX_HEADS), log_f)
    cos, sin = rope_tables(jnp.arange(S_, dtype=F32))
    q_nsa = apply_partial_rope(heads(nq, NSA_HEADS), cos, sin)
    k_slc = apply_partial_rope(ksl, cos, sin)
    k_win = apply_partial_rope(kwn, cos, sin)
    cmp = compress_kv_pallas(jnp.stack([kc, vc]), cmp_pos, cmp_w1, cmp_w2)
    n_cmp = cmp.shape[2]
    cmp_end = jnp.arange(n_cmp, dtype=F32) * CMP_STRIDE + (CMP_LEN - 1)
    k_cmp = apply_partial_rope(cmp[0], *rope_tables(cmp_end))
    y_nsa = nsa_attention(q_nsa, k_cmp, cmp[1], k_slc, vsl, k_win, vwn, gate_logit)
    return merge_out(xf, y_fox, y_nsa, merge_logit, w_branch, w_out)


def peer_ffn(h, norm_g, wq, subkeys, u, v):
    hn, q = peer_query(h, norm_g, wq)
    idx2, w = peer_topk(q, subkeys)
    coef = peer_u(idx2, hn, w, pack_table(u))
    return peer_v(idx2, coef, pack_table(v))


def kernel(x, norm_mix, w_in, fox_f_bias, nsa_cmp_pos, nsa_cmp_w1, nsa_cmp_w2, w_branch, w_out,
           norm_ffn, peer_wq, peer_subkeys, peer_u, peer_v, norm_final):
    B_, S_, D = x.shape
    assert norm_mix.shape[0] == 1, "single-layer trunk"
    h = hybrid_mixer(x, norm_mix[0], w_in[0], fox_f_bias[0], nsa_cmp_pos[0], nsa_cmp_w1[0], nsa_cmp_w2[0],
                     w_branch[0], w_out[0])
    p = peer_ffn(h, norm_ffn[0], peer_wq[0], peer_subkeys[0], peer_u[0], peer_v[0])
    return final_norm(h, p, norm_final).reshape(B_, S_, D)
```

```python
import jax
import jax.numpy as jnp
import numpy as np
from jax import lax
from jax.experimental import pallas as pl
from jax.experimental.pallas import tpu as pltpu

D_MODEL = 1024
HEAD_DIM = 64
FOX_HEADS = 8
NSA_HEADS = 8
NSA_GROUPS = 2
NSA_HPG = NSA_HEADS // NSA_GROUPS
BRANCH_WIDTH = 512
N_BRANCH = 2
N_NSA_BRANCH = 3
ROPE_DIM = HEAD_DIM // 4
ROPE_THETA = 500000.0
CMP_LEN = 32
CMP_STRIDE = 16
CMP_HIDDEN = 2 * HEAD_DIM
SLC_LEN = 64
SLC_TOP = 16
WINDOW = 512
FORCED_SCORE = 1e9
NEG_INF = -1e30
PEER_HEADS = 8
N_KEYS = 128
PEER_QDIM = 256
PEER_TOPK = 16
RMS_EPS = 1e-6
KV_WIDTH = NSA_GROUPS * HEAD_DIM
IN_SPLITS = (BRANCH_WIDTH, BRANCH_WIDTH, BRANCH_WIDTH, FOX_HEADS, BRANCH_WIDTH,
             KV_WIDTH, KV_WIDTH, KV_WIDTH, KV_WIDTH, KV_WIDTH, KV_WIDTH,
             NSA_HEADS * N_NSA_BRANCH, N_BRANCH * D_MODEL)


BF16 = jnp.bfloat16
F32 = jnp.float32

PEER_SLOTS = PEER_HEADS * PEER_TOPK
PEER_TT = 64
PEER_CHUNK_SLOTS = 32
PEER_U_UNROLL = 8
PEER_V_UNROLL = 8
ROW_SUB = D_MODEL // 2 // 128
VMEM_LIMIT_PEER = N_KEYS * N_KEYS * D_MODEL * 2 + 16 * 1024 * 1024


def pack_table(tab):
    n, d = tab.shape
    return pl.pallas_call(
        _pack_kernel,
        grid=(n // PACK_TM,),
        in_specs=[pl.BlockSpec((PACK_TM, d), lambda i: (i, 0))],
        out_specs=pl.BlockSpec((PACK_TM * ROW_SUB, 128), lambda i: (i, 0)),
        out_shape=jax.ShapeDtypeStruct((n * ROW_SUB, 128), jnp.uint32),
        name="pack_table",
    )(tab)


PACK_TM = 256


def _pack_kernel(x_ref, o_ref):
    bits = lax.bitcast_convert_type(x_ref[...].astype(jnp.bfloat16).astype(jnp.float32), jnp.uint32)
    for r in range(ROW_SUB):
        even = bits[:, (2 * r) * 128:(2 * r + 1) * 128]
        odd = bits[:, (2 * r + 1) * 128:(2 * r + 2) * 128]
        o_ref[pl.ds(r, PACK_TM, stride=ROW_SUB), :] = (even >> 16) | (odd & jnp.uint32(0xFFFF0000))


def _expert_slab(tab_ref, row):
    return tab_ref[pl.ds(pl.multiple_of(row, ROW_SUB), ROW_SUB), :]


def _unpack(w):
    lo = lax.bitcast_convert_type(w << 16, jnp.float32)
    hi = lax.bitcast_convert_type(w & jnp.uint32(0xFFFF0000), jnp.float32)
    return lo, hi


def _gelu(x):
    return 0.5 * x * (1.0 + lax.erf(x * (2.0 ** -0.5)))


def _peer_u_kernel(idx_a_ref, idx_b_ref, x_ref, w_ref, tab_ref, out_ref):
    lane = lax.broadcasted_iota(jnp.int32, (2 * ROW_SUB, 128), 1)
    lower = lax.broadcasted_iota(jnp.int32, (2 * ROW_SUB, 128), 0) >= ROW_SUB
    pair_lane = jnp.where(lower, lane - PEER_SLOTS // 2, lane)

    def token(t):
        row = x_ref[pl.ds(t, 1), :]
        chunks = [row[:, c * 128:(c + 1) * 128] for c in range(2 * ROW_SUB)]
        xl = jnp.concatenate(chunks[0::2] * 2, axis=0)
        xh = jnp.concatenate(chunks[1::2] * 2, axis=0)
        acc = jnp.zeros((2 * ROW_SUB, 128), jnp.float32)
        base = t * (PEER_SLOTS // 2)
        for s in range(PEER_SLOTS // 2):
            w = jnp.concatenate([_expert_slab(tab_ref, idx_a_ref[base + s]),
                                 _expert_slab(tab_ref, idx_b_ref[base + s])], axis=0)
            lo, hi = _unpack(w)
            part = jnp.sum(lo * xl + hi * xh, axis=1, keepdims=True)
            acc = jnp.where(pair_lane == s, part, acc)
        out_ref[pl.ds(t, 1), :] = w_ref[pl.ds(t, 1), :] * _gelu(jnp.sum(acc, axis=0, keepdims=True))

    def trip(i, carry):
        for j in range(PEER_U_UNROLL):
            token(i * PEER_U_UNROLL + j)
        return carry

    lax.fori_loop(0, PEER_TT // PEER_U_UNROLL, trip, 0)


N_CHUNK = 2 * ROW_SUB


def _peer_v_kernel(idx_a_ref, idx_b_ref, coef_ref, expand_ref, tab_ref, out_ref, ce_hi_ref, ce_lo_ref):
    half_slots = PEER_SLOTS // 2
    width = PEER_SLOTS * N_CHUNK
    coef = coef_ref[...]
    c_hi = _trunc_bf16(coef)
    expand = expand_ref[...]
    ce_hi_ref[...] = jnp.dot(c_hi.astype(BF16), expand, preferred_element_type=F32)
    ce_lo_ref[...] = jnp.dot((coef - c_hi).astype(BF16), expand, preferred_element_type=F32)
    own_chunk = ((lax.broadcasted_iota(jnp.int32, (N_CHUNK, width), 1) & (N_CHUNK - 1))
                 == lax.broadcasted_iota(jnp.int32, (N_CHUNK, width), 0))

    def token(t):
        base = t * half_slots
        slabs = [jnp.concatenate([_expert_slab(tab_ref, idx_a_ref[base + s]),
                                  _expert_slab(tab_ref, idx_b_ref[base + s])], axis=0) for s in range(half_slots)]
        gathered = pltpu.bitcast(jnp.concatenate(slabs, axis=0), BF16)
        sel = lambda ref: jnp.where(own_chunk, ref[pl.ds(t, 1), :], 0.0).astype(BF16)
        c = jnp.concatenate([sel(ce_hi_ref), sel(ce_lo_ref)], axis=0)
        o = jnp.dot(c, gathered, preferred_element_type=F32)
        o = o[:N_CHUNK] + o[N_CHUNK:]
        out_ref[pl.ds(t, 1), :] = jnp.concatenate([o[r:r + 1, :] for r in range(N_CHUNK)], axis=1)

    def trip(i, carry):
        for j in range(PEER_V_UNROLL):
            token(i * PEER_V_UNROLL + j)
        return carry

    lax.fori_loop(0, PEER_TT // PEER_V_UNROLL, trip, 0)


def _smem_spec():
    return pl.BlockSpec((PEER_TT * PEER_SLOTS // 2,), lambda i: (i,), memory_space=pltpu.SMEM)


def _slot_halves(a):
    return a[:, :PEER_SLOTS // 2].reshape(-1), a[:, PEER_SLOTS // 2:].reshape(-1)


def _table_spec(rows):
    return pl.BlockSpec((rows, 128), lambda i: (0, 0), pipeline_mode=pl.Buffered(1))


def peer_u(idx, x, w, tab):
    t = x.shape[0]
    return pl.pallas_call(
        _peer_u_kernel,
        grid=(t // PEER_TT,),
        in_specs=[_smem_spec(), _smem_spec(), pl.BlockSpec((PEER_TT, D_MODEL), lambda i: (i, 0)),
                  pl.BlockSpec((PEER_TT, PEER_SLOTS), lambda i: (i, 0)), _table_spec(tab.shape[0])],
        out_specs=pl.BlockSpec((PEER_TT, PEER_SLOTS), lambda i: (i, 0)),
        out_shape=jax.ShapeDtypeStruct((t, PEER_SLOTS), jnp.float32),
        compiler_params=pltpu.CompilerParams(vmem_limit_bytes=VMEM_LIMIT_PEER),
        name="peer_u",
    )(*_slot_halves(idx), x, w, tab)


def peer_v(idx, coef, tab):
    t = idx.shape[0]
    width = PEER_SLOTS * N_CHUNK
    half = PEER_SLOTS // 2
    col_slot = np.arange(width) // N_CHUNK
    col_slot = (col_slot // 2) + half * (col_slot % 2)
    expand = jnp.asarray(col_slot[None, :] == np.arange(PEER_SLOTS)[:, None], dtype=BF16)
    return pl.pallas_call(
        _peer_v_kernel,
        grid=(t // PEER_TT,),
        in_specs=[_smem_spec(), _smem_spec(), pl.BlockSpec((PEER_TT, PEER_SLOTS), lambda i: (i, 0)),
                  pl.BlockSpec((PEER_SLOTS, width), lambda i: (0, 0)), _table_spec(tab.shape[0])],
        out_specs=pl.BlockSpec((PEER_TT, D_MODEL), lambda i: (i, 0)),
        out_shape=jax.ShapeDtypeStruct((t, D_MODEL), jnp.float32),
        scratch_shapes=[pltpu.VMEM((PEER_TT, width), F32), pltpu.VMEM((PEER_TT, width), F32)],
        compiler_params=pltpu.CompilerParams(vmem_limit_bytes=VMEM_LIMIT_PEER),
        name="peer_v",
    )(*_slot_halves(idx), coef, expand, tab)


PK_TT = 256
PK_A_BLOCKS = ((0, 16), (1, 8), (2, 5), (3, 4))
PK_B_BLOCKS = ((0, 4, 16), (1, 4, 8), (2, 4, 5))
NEG_HUGE = -3.0e38
PK_HEADS_PER_TRIP = 2


def _extract_top(problems, flat, n):
    vals = [[] for _ in problems]
    picked = [[] for _ in problems]
    cur = [v for v, _ in problems]
    for _ in range(n):
        for i, (_, payload) in enumerate(problems):
            m = jnp.max(cur[i], axis=0, keepdims=True)
            pos = jnp.min(jnp.where(cur[i] == m, flat, 1e9), axis=0, keepdims=True)
            hit = flat == pos
            picked[i].append(pos if payload is None else jnp.sum(jnp.where(hit, payload, 0.0), axis=0, keepdims=True))
            cur[i] = jnp.where(hit, NEG_HUGE, cur[i])
            vals[i].append(m)
    return [jnp.concatenate(v, axis=0) for v in vals], [jnp.concatenate(p, axis=0) for p in picked]


def _candidate_grid(first, second, combine):
    up8 = lambda n: -(-n // 8) * 8
    return jnp.concatenate([combine(first[a:a + 1], second[0:up8(nb)]) for a, nb in PK_A_BLOCKS]
                           + [combine(first[0:up8(a1)], second[b:b + 1]) for b, _, a1 in PK_B_BLOCKS], axis=0)


def _candidate_valid_and_flat():
    up8 = lambda n: -(-n // 8) * 8
    valid, flat = [], []
    for a, nb in PK_A_BLOCKS:
        r = lax.broadcasted_iota(jnp.int32, (up8(nb), PK_TT), 0)
        valid.append(r < nb)
        flat.append(a * PEER_TOPK + r)
    for b, a0, a1 in PK_B_BLOCKS:
        r = lax.broadcasted_iota(jnp.int32, (up8(a1), PK_TT), 0)
        valid.append((r >= a0) & (r < a1))
        flat.append(r * PEER_TOPK + b)
    valid = jnp.concatenate(valid, axis=0)
    return valid, jnp.where(valid, jnp.concatenate(flat, axis=0).astype(F32), 1e9)


def _peer_topk_kernel(q_ref, ka_ref, eid_ref, w_ref):
    key_row = lax.broadcasted_iota(jnp.int32, (N_KEYS, PK_TT), 0).astype(F32)
    valid, flat = _candidate_valid_and_flat()

    def heads(i, carry):
        scores = []
        for j in range(2 * PK_HEADS_PER_TRIP):
            q = q_ref[2 * PK_HEADS_PER_TRIP * i + j]
            q_hi = _trunc_bf16(q)
            qa = jnp.concatenate([q_hi.astype(BF16), (q - q_hi).astype(BF16), q_hi.astype(BF16)], axis=1)
            scores.append(lax.dot_general(ka_ref[2 * PK_HEADS_PER_TRIP * i + j], qa, (((1,), (1,)), ((), ())),
                                          preferred_element_type=F32))
        sub_s, sub_i = _extract_top([(s, None) for s in scores], key_row, PEER_TOPK)
        cands = []
        for j in range(PK_HEADS_PER_TRIP):
            s1, s2, i1, i2 = sub_s[2 * j], sub_s[2 * j + 1], sub_i[2 * j], sub_i[2 * j + 1]
            cands.append((jnp.where(valid, _candidate_grid(s1, s2, lambda x, y: x + y), NEG_HUGE),
                          _candidate_grid(i1, i2, lambda x, y: x * N_KEYS + y)))
        top_s, top_e = _extract_top(cands, flat, PEER_TOPK)
        for j in range(PK_HEADS_PER_TRIP):
            e = jnp.exp(top_s[j] - top_s[j][0:1])
            w_ref[PK_HEADS_PER_TRIP * i + j] = e / jnp.sum(e, axis=0, keepdims=True)
            eid_ref[PK_HEADS_PER_TRIP * i + j] = top_e[j].astype(jnp.int32) * ROW_SUB
        return carry

    lax.fori_loop(0, PEER_HEADS // PK_HEADS_PER_TRIP, heads, 0)


def peer_topk(qt, subkeys):
    hp, t, d = qt.shape
    k_hi, k_lo = _split2(subkeys.reshape(hp, N_KEYS, d))
    ka = jnp.concatenate([k_hi, k_hi, k_lo], axis=2)
    eid, w = pl.pallas_call(
        _peer_topk_kernel,
        grid=(t // PK_TT,),
        in_specs=[pl.BlockSpec((hp, PK_TT, d), lambda i: (0, i, 0)),
                  pl.BlockSpec((hp, N_KEYS, 3 * d), lambda i: (0, 0, 0))],
        out_specs=[pl.BlockSpec((PEER_HEADS, PEER_TOPK, PK_TT), lambda i: (0, 0, i)),
                   pl.BlockSpec((PEER_HEADS, PEER_TOPK, PK_TT), lambda i: (0, 0, i))],
        out_shape=[jax.ShapeDtypeStruct((PEER_HEADS, PEER_TOPK, t), jnp.int32),
                   jax.ShapeDtypeStruct((PEER_HEADS, PEER_TOPK, t), F32)],
        name="peer_topk",
    )(qt, ka)
    to_slots = lambda a: a.transpose(2, 0, 1).reshape(t, PEER_SLOTS)
    return to_slots(eid), to_slots(w)


def split_columns(t, sizes):
    offs = np.cumsum((0,) + tuple(sizes))
    return [t[..., int(a):int(b)] for a, b in zip(offs[:-1], offs[1:])]


def rope_tables(pos):
    inv = jnp.power(ROPE_THETA, -jnp.arange(0, ROPE_DIM, 2, dtype=jnp.float32) / ROPE_DIM)
    ang = pos[:, None] * inv[None, :]
    return jnp.cos(ang), jnp.sin(ang)


def apply_partial_rope(x, cos, sin):
    half = ROPE_DIM // 2
    xr = x[..., :ROPE_DIM].astype(jnp.float32)
    x1, x2 = xr[..., :half], xr[..., half:]
    c, s = cos[:, None, :], sin[:, None, :]
    rot = jnp.concatenate([x1 * c - x2 * s, x1 * s + x2 * c], axis=-1)
    return jnp.concatenate([rot.astype(x.dtype), x[..., ROPE_DIM:]], axis=-1)


def _logits(ks, qts):
    return [jnp.dot(k, qt, preferred_element_type=F32) for k, qt in zip(ks, qts)]


def _flash_update(ss, vts, mask, carries):
    stats, ps = [], []
    for s, (m, l, _) in zip(ss, carries):
        s_vis = s if mask is None else jnp.where(mask, s, NEG_INF)
        m_new = jnp.maximum(m, jnp.max(s_vis, axis=0, keepdims=True))
        alpha = jnp.exp(m - m_new)
        p = jnp.exp(s - m_new)
        if mask is not None:
            p = jnp.where(mask, p, 0.0)
        stats.append((m_new, alpha, alpha * l + jnp.sum(p, axis=0, keepdims=True)))
        ps.append(p.astype(BF16))
    return tuple((m_new, l, alpha * acc + jnp.dot(vt, p, preferred_element_type=F32))
                 for (m_new, alpha, l), p, vt, (_, _, acc) in zip(stats, ps, vts, carries))


def _causal_flash(key_tile, value_tile, qts, n_full, last_mask, n):
    def body(kt, carries):
        return _flash_update(_logits(key_tile(kt), qts), value_tile(kt), None, carries)

    carries = lax.fori_loop(0, n_full, body, tuple(_flash_init(n) for _ in qts))
    return _flash_update(_logits(key_tile(n_full), qts), value_tile(n_full), last_mask, carries)


def _flash_init(n):
    return (jnp.full((1, n), NEG_INF, F32), jnp.zeros((1, n), F32), jnp.zeros((HEAD_DIM, n), F32))


FOX_TQ = 256
FOX_KV = 512
FOX_DK = 256


FOX_HP = 4


def _fox_kernel(qt_ref, k_ref, vt_ref, o_ref):
    qi = pl.program_id(1)
    q0 = qi * FOX_TQ
    qts = [qt_ref[h] for h in range(FOX_HP)]
    qpos = q0 + lax.broadcasted_iota(jnp.int32, (1, FOX_TQ), 1)
    krow = lax.broadcasted_iota(jnp.int32, (FOX_KV, FOX_TQ), 0)

    n_full = q0 // FOX_KV
    causal = (n_full * FOX_KV + krow) <= qpos
    def key_tile(kt):
        rows = pl.ds(pl.multiple_of(kt * FOX_KV, FOX_KV), FOX_KV)
        return [k_ref[rows, h * FOX_DK:(h + 1) * FOX_DK] for h in range(FOX_HP)]

    carry = _causal_flash(key_tile, lambda kt: [vt_ref[h, kt] for h in range(FOX_HP)], qts, n_full, causal, FOX_TQ)
    for h in range(FOX_HP):
        _, l, acc = carry[h]
        o_ref[h] = acc / l


def _trunc_bf16(x):
    bits = lax.bitcast_convert_type(x, jnp.uint32) & jnp.uint32(0xFFFF0000)
    return lax.bitcast_convert_type(bits, F32)


def _split3(c):
    c1 = _trunc_bf16(c)
    r = c - c1
    c2 = _trunc_bf16(r)
    return c1.astype(BF16), c2.astype(BF16), (r - c2).astype(BF16)


def _split2(x):
    hi = _trunc_bf16(x)
    return hi.astype(BF16), (x - hi).astype(BF16)


def _aug_q(q):
    hi, lo = _split2(q)
    return jnp.concatenate([hi, hi, lo], axis=-1)


def _aug_k(k):
    hi, lo = _split2(k)
    return jnp.concatenate([hi, lo, hi], axis=-1)


def fox_attention(q, k, v, log_f):
    B_, S_, H, Dh = q.shape
    KV_TILE = FOX_KV
    n_qt, n_kt = S_ // FOX_TQ, S_ // KV_TILE
    c = jnp.cumsum(log_f, axis=1)
    c1, c2, c3 = _split3(c)
    j = jnp.arange(FOX_DK - 3 * Dh)
    pick = lambda t: t[..., None].astype(F32)
    terms = lambda first: jnp.where(j == first, pick(c1), jnp.where(j == first + 1, pick(c2), pick(c3)))
    q_extra = jnp.where(j < 3, 1.0, jnp.where(j < 6, terms(3), 0.0)).astype(BF16)
    k_extra = jnp.where(j < 3, -terms(0), jnp.where(j < 6, 1.0, 0.0)).astype(BF16)
    qa = jnp.concatenate([_aug_q(q * (Dh ** -0.5)), q_extra], axis=-1)
    ka = jnp.concatenate([_aug_k(k), k_extra], axis=-1)
    HP, HG = FOX_HP, H // FOX_HP
    qt = qa.reshape(B_, n_qt, FOX_TQ, HG, HP, FOX_DK).transpose(0, 3, 1, 4, 5, 2).reshape(B_ * HG * n_qt, HP, FOX_DK, FOX_TQ)
    kk = ka.reshape(B_, S_, H * FOX_DK)
    vt = v.astype(BF16).reshape(B_, n_kt, KV_TILE, HG, HP, Dh).transpose(0, 3, 4, 1, 5, 2).reshape(B_ * HG, HP, n_kt, Dh, KV_TILE)
    out = pl.pallas_call(
        _fox_kernel,
        grid=(B_ * HG, n_qt),
        in_specs=[pl.BlockSpec((None, HP, FOX_DK, FOX_TQ), lambda b, i: (b * n_qt + i, 0, 0, 0)),
                  pl.BlockSpec((None, S_, HP * FOX_DK), lambda b, i: (b // HG, 0, b % HG)),
                  pl.BlockSpec((None, HP, n_kt, Dh, KV_TILE), lambda b, i: (b, 0, 0, 0, 0))],
        out_specs=pl.BlockSpec((None, HP, Dh, FOX_TQ), lambda b, i: (b * n_qt + i, 0, 0, 0)),
        out_shape=jax.ShapeDtypeStruct((B_ * HG * n_qt, HP, Dh, FOX_TQ), F32),
        name="fox_attention",
    )(qt, kk, vt)
    return out.reshape(B_, HG, n_qt, HP, Dh, FOX_TQ)


NSA_TQ = 128
NSA_N = NSA_HPG * NSA_TQ
N_CMP_PAD = 256
N_SLC = 64
NSA_DK = 3 * HEAD_DIM
NSA_KV = 512
WIN_TILES = WINDOW // NSA_TQ + 1
WIN_KEYS = WIN_TILES * NSA_TQ


def _nsa_kernel(qt_ref, kc_ref, vct_ref, ks_ref, vst_ref, kw_ref, vwt_ref, gate_ref, ovl_ref, o_ref):
    qi = pl.program_id(1)
    q0 = qi * NSA_TQ
    qt = qt_ref[...]
    lane = lax.broadcasted_iota(jnp.int32, (1, NSA_N), 1)
    qpos = q0 + (lane & (NSA_TQ - 1))

    hn = NSA_N // 2
    qts = (qt[:, :hn], qt[:, hn:])
    qpos_h = qpos[:, :hn]

    w0 = jnp.maximum(qi - WINDOW // NSA_TQ, 0)
    k_w = kw_ref[pl.ds(pl.multiple_of(w0 * NSA_TQ, NSA_TQ), WIN_KEYS), :NSA_DK]
    vt_w = jnp.concatenate([vwt_ref[w0 + i] for i in range(WIN_TILES)], axis=1)
    s = jnp.dot(kc_ref[...], qt, preferred_element_type=F32)
    s_w = [jnp.dot(k_w, qts[h], preferred_element_type=F32) for h in range(2)]

    cmp_end = lax.broadcasted_iota(jnp.int32, (N_CMP_PAD, NSA_N), 0) * CMP_STRIDE + (CMP_LEN - 1)
    mc = cmp_end <= qpos
    m = jnp.max(jnp.where(mc, s, NEG_INF), axis=0, keepdims=True)
    p = jnp.where(mc, jnp.exp(s - m), 0.0)
    l = jnp.sum(p, axis=0, keepdims=True)
    pc = p * jnp.where(l > 0.0, 1.0 / l, 0.0)
    o_cmp = jnp.dot(vct_ref[...], pc.astype(BF16), preferred_element_type=F32)
    pcs = (pc[:, 0:NSA_TQ] + pc[:, NSA_TQ:2 * NSA_TQ]) + (pc[:, 2 * NSA_TQ:3 * NSA_TQ] + pc[:, 3 * NSA_TQ:4 * NSA_TQ])
    pcs_hi = pcs.astype(BF16)
    pcs_lo = (pcs - pcs_hi.astype(F32)).astype(BF16)
    ovl = ovl_ref[...]
    imp = jnp.dot(ovl, pcs_hi, preferred_element_type=F32) + jnp.dot(ovl, pcs_lo, preferred_element_type=F32)

    rel = qpos_h - (w0 * NSA_TQ + lax.broadcasted_iota(jnp.int32, (WIN_KEYS, hn), 0))
    in_win = (rel >= 0) & (rel < WINDOW)
    outs_w = []
    for h in range(2):
        m = jnp.max(jnp.where(in_win, s_w[h], NEG_INF), axis=0, keepdims=True)
        p = jnp.where(in_win, jnp.exp(s_w[h] - m), 0.0)
        l = jnp.sum(p, axis=0, keepdims=True)
        outs_w.append(jnp.dot(vt_w, p.astype(BF16), preferred_element_type=F32) / l)
    acc_w = jnp.concatenate(outs_w, axis=1)

    blk = lax.broadcasted_iota(jnp.int32, (N_SLC, NSA_TQ), 0)
    q_blk = (q0 + lax.broadcasted_iota(jnp.int32, (N_SLC, NSA_TQ), 1)) >> 6
    forced = (blk == 0) | (blk == q_blk) | (blk == q_blk - 1)
    score = jnp.where(forced, FORCED_SCORE, jnp.where(blk <= q_blk, imp, -1.0))
    rank = jnp.zeros((N_SLC, NSA_TQ), jnp.int32)
    for i in range(N_SLC):
        row = score[i:i + 1, :]
        beats = (row > score) | ((row == score) & (blk > i))
        rank = rank + beats.astype(jnp.int32)
    bias = jnp.where(rank < SLC_TOP, 0.0, NEG_INF).astype(BF16)
    bias = jnp.concatenate([bias] * (NSA_HPG // 2), axis=1)
    qts_s = [jnp.concatenate([qh, bias], axis=0) for qh in qts]
    krow = lax.broadcasted_iota(jnp.int32, (NSA_KV, hn), 0)

    n_full = q0 // NSA_KV
    causal = (n_full * NSA_KV + krow) <= qpos_h
    key_tile = lambda kt: [ks_ref[pl.ds(pl.multiple_of(kt * NSA_KV, NSA_KV), NSA_KV), :]] * 2
    slc = _causal_flash(key_tile, lambda kt: [vst_ref[kt]] * 2, qts_s, n_full, causal, hn)
    acc_s = jnp.concatenate([slc[0][2] / slc[0][1], slc[1][2] / slc[1][1]], axis=1)

    g = jax.nn.sigmoid(gate_ref[...])
    o_ref[...] = g[0:1] * o_cmp + g[1:2] * acc_s + g[2:3] * acc_w


def nsa_attention(q, k_cmp, v_cmp, k_slc, v_slc, k_win, v_win, gate_logit):
    B_, S_, H, Dh = q.shape
    G = NSA_GROUPS
    n_qt = S_ // NSA_TQ
    n_cmp = k_cmp.shape[1]
    cs = np.arange(N_CMP_PAD)[None, :] * CMP_STRIDE
    ss = np.arange(N_SLC)[:, None] * SLC_LEN
    ov = np.clip(np.minimum(cs + CMP_LEN, ss + SLC_LEN) - np.maximum(cs, ss), 0, None) / CMP_LEN
    ov[:, n_cmp:] = 0.0
    ovl = jnp.asarray(ov, dtype=BF16)
    qs = _aug_q(q * (Dh ** -0.5)).reshape(B_, n_qt, NSA_TQ, G, NSA_HPG, NSA_DK)
    qt = qs.transpose(0, 3, 1, 5, 4, 2).reshape(B_ * G * n_qt, NSA_DK, NSA_N)
    gt = gate_logit.astype(F32).reshape(B_, n_qt, NSA_TQ, G, NSA_HPG, N_NSA_BRANCH)
    gt = gt.transpose(0, 3, 1, 5, 4, 2).reshape(B_ * G * n_qt, N_NSA_BRANCH, NSA_N)
    key_w = NSA_DK + N_SLC
    block_onehot = (jnp.arange(S_)[:, None] // SLC_LEN == jnp.arange(N_SLC)[None, :]).astype(BF16)
    k_slc_aug = jnp.concatenate([_aug_k(k_slc), jnp.broadcast_to(block_onehot[None, :, None, :], (B_, S_, G, N_SLC))],
                                axis=-1).reshape(B_, S_, G * key_w)
    k_win_aug = jnp.concatenate([_aug_k(k_win), jnp.zeros((B_, S_, G, key_w - NSA_DK), BF16)],
                                axis=-1).reshape(B_, S_, G * key_w)
    per_group = pl.BlockSpec((None, S_, key_w), lambda b, i: (b // G, 0, b % G))
    vals = lambda t, kv: t.astype(BF16).reshape(B_, S_ // kv, kv, G, Dh).transpose(0, 3, 1, 4, 2).reshape(B_ * G, S_ // kv, Dh, kv)
    padc = ((0, 0), (0, N_CMP_PAD - n_cmp), (0, 0), (0, 0))
    kc = _aug_k(jnp.pad(k_cmp, padc)).transpose(0, 2, 1, 3).reshape(B_ * G, N_CMP_PAD, NSA_DK)
    vct = jnp.pad(v_cmp, padc).astype(BF16).transpose(0, 2, 3, 1).reshape(B_ * G, Dh, N_CMP_PAD)
    per_bg = lambda *blk: pl.BlockSpec((None,) + blk, lambda b, i: (b,) + (0,) * len(blk))
    per_tile = lambda *blk: pl.BlockSpec((None,) + blk, lambda b, i: (b * n_qt + i,) + (0,) * len(blk))
    out = pl.pallas_call(
        _nsa_kernel,
        grid=(B_ * G, n_qt),
        in_specs=[per_tile(NSA_DK, NSA_N),
                  per_bg(N_CMP_PAD, NSA_DK), per_bg(Dh, N_CMP_PAD),
                  per_group, per_bg(S_ // NSA_KV, Dh, NSA_KV),
                  per_group, per_bg(n_qt, Dh, NSA_TQ),
                  per_tile(N_NSA_BRANCH, NSA_N),
                  pl.BlockSpec((N_SLC, N_CMP_PAD), lambda b, i: (0, 0))],
        out_specs=per_tile(Dh, NSA_N),
        out_shape=jax.ShapeDtypeStruct((B_ * G * n_qt, Dh, NSA_N), F32),
        name="nsa_attention",
    )(qt, kc, vct, k_slc_aug, vals(v_slc, NSA_KV), k_win_aug, vals(v_win, NSA_TQ), gt, ovl)
    return out.reshape(B_, G, n_qt, Dh, NSA_N)


DENSE_TM = 256
SEC_QKV = 4 * BRANCH_WIDTH
SEC_KV = 6 * KV_WIDTH
SEC_MERGE = N_BRANCH * D_MODEL
SEC_SMALL = 256
IN_SECTIONS = (SEC_QKV, SEC_KV, SEC_MERGE, SEC_SMALL)


def _rms(x, g):
    return x * lax.rsqrt(jnp.mean(x * x, axis=-1, keepdims=True) + RMS_EPS) * g


def _row_spec(width):
    return pl.BlockSpec((DENSE_TM, width), lambda i: (i, 0))


def _whole_spec(shape):
    return pl.BlockSpec(shape, lambda i: (0,) * len(shape))


def _in_proj_kernel(x_ref, g_ref, w_ref, *o_refs):
    xn = _rms(x_ref[...], g_ref[...]).astype(BF16)
    off = 0
    for o_ref, width in zip(o_refs, IN_SECTIONS):
        o_ref[...] = jnp.dot(xn, w_ref[:, off:off + width], preferred_element_type=F32)
        off += width


def in_projection(x, g, w_in):
    t, d = x.shape
    fq, fk, fv, fl, nq, kc, vc, ksl, vsl, kwn, vwn, gl, ml = split_columns(w_in, IN_SPLITS)
    pad = lambda w: jnp.pad(w, ((0, 0), (0, 128 - w.shape[1])))
    w = jnp.concatenate([fq, fk, fv, nq, kc, vc, ksl, vsl, kwn, vwn, ml, pad(fl), pad(gl)], axis=1).astype(BF16)
    return pl.pallas_call(
        _in_proj_kernel,
        grid=(t // DENSE_TM,),
        in_specs=[_row_spec(d), _whole_spec((1, d)), _whole_spec(w.shape)],
        out_specs=[_row_spec(s) for s in IN_SECTIONS],
        out_shape=[jax.ShapeDtypeStruct((t, s), F32) for s in IN_SECTIONS],
        name="in_projection",
    )(x, g.reshape(1, d), w)


def _merge_kernel(x_ref, yf_ref, yn_ref, ml_ref, wb_ref, wo_ref, o_ref):
    yf_t = yf_ref[...].reshape(BRANCH_WIDTH, DENSE_TM)
    yn_t = jnp.concatenate(
        [jnp.concatenate([yn_ref[g, j, :, h * NSA_TQ:(h + 1) * NSA_TQ]
                          for g in range(NSA_GROUPS) for h in range(NSA_HPG)], axis=0)
         for j in range(DENSE_TM // NSA_TQ)], axis=1)
    g = jax.nn.sigmoid(ml_ref[...])
    up_f = jnp.dot(yf_t.T.astype(BF16), wb_ref[0], preferred_element_type=F32)
    up_n = jnp.dot(yn_t.T.astype(BF16), wb_ref[1], preferred_element_type=F32)
    merged = g[:, :D_MODEL] * up_f + g[:, D_MODEL:] * up_n
    o_ref[...] = x_ref[...] + jnp.dot(merged.astype(BF16), wo_ref[...], preferred_element_type=F32)


def merge_out(x, y_fox, y_nsa, merge_logit, w_branch, w_out):
    t, d = x.shape
    assert FOX_TQ == DENSE_TM and DENSE_TM % NSA_TQ == 0
    tiles_per_seq = y_fox.shape[2]
    nsa_per_tile = DENSE_TM // NSA_TQ
    yf_spec = pl.BlockSpec((None,) + y_fox.shape[1:2] + (None,) + y_fox.shape[3:],
                           lambda i: (i // tiles_per_seq, 0, i % tiles_per_seq, 0, 0, 0))
    yn_spec = pl.BlockSpec((None, NSA_GROUPS, nsa_per_tile, HEAD_DIM, NSA_N),
                           lambda i: (i // tiles_per_seq, 0, i % tiles_per_seq, 0, 0))
    return pl.pallas_call(
        _merge_kernel,
        grid=(t // DENSE_TM,),
        in_specs=[_row_spec(d), yf_spec, yn_spec, _row_spec(SEC_MERGE),
                  _whole_spec(w_branch.shape), _whole_spec(w_out.shape)],
        out_specs=_row_spec(d),
        out_shape=jax.ShapeDtypeStruct((t, d), F32),
        name="merge_out",
    )(x, y_fox, y_nsa, merge_logit, w_branch.astype(BF16), w_out.astype(BF16))


def _peer_q_kernel(x_ref, g_ref, w_ref, hn_ref, q_ref):
    hn = _rms(x_ref[...], g_ref[...])
    hn_ref[...] = hn
    q = jnp.dot(hn.astype(BF16), w_ref[...], preferred_element_type=F32)
    d = PEER_QDIM // 2
    for j in range(2 * PEER_HEADS):
        q_ref[j] = q[:, j * d:(j + 1) * d]


def peer_query(x, g, wq):
    t, d = x.shape
    hp, dq = 2 * PEER_HEADS, PEER_QDIM // 2
    return pl.pallas_call(
        _peer_q_kernel,
        grid=(t // DENSE_TM,),
        in_specs=[_row_spec(d), _whole_spec((1, d)), _whole_spec(wq.shape)],
        out_specs=[_row_spec(d), pl.BlockSpec((hp, DENSE_TM, dq), lambda i: (0, i, 0))],
        out_shape=[jax.ShapeDtypeStruct((t, d), F32), jax.ShapeDtypeStruct((hp, t, dq), F32)],
        name="peer_query",
    )(x, g.reshape(1, d), wq.astype(BF16))


def _final_kernel(h_ref, p_ref, g_ref, o_ref):
    o_ref[...] = _rms(h_ref[...] + p_ref[...], g_ref[...])


def final_norm(h, p, g):
    t, d = h.shape
    return pl.pallas_call(
        _final_kernel,
        grid=(t // DENSE_TM,),
        in_specs=[_row_spec(d), _row_spec(d), _whole_spec((1, d))],
        out_specs=_row_spec(d),
        out_shape=jax.ShapeDtypeStruct((t, d), F32),
        name="final_norm",
    )(h, p, g.reshape(1, d))


N_CMP_ROWS = 256


def _compress_kernel(r_ref, pos_ref, w1_ref, w2_ref, o_ref):
    r = r_ref[...]
    nxt = jnp.concatenate([r[1:], jnp.zeros((1, r.shape[1]), F32)], axis=0)
    half = r.shape[1]
    hid = (jnp.dot((r + pos_ref[0:1, :]).astype(BF16), w1_ref[:half], preferred_element_type=F32)
           + jnp.dot((nxt + pos_ref[1:2, :]).astype(BF16), w1_ref[half:], preferred_element_type=F32))
    o_ref[...] = jnp.dot(_gelu(hid).astype(BF16), w2_ref[...], preferred_element_type=F32)


def compress_kv_pallas(kv, cmp_pos, w1, w2):
    _, B_, S_, G, Dh = kv.shape
    n_cmp = (S_ - CMP_LEN) // CMP_STRIDE + 1
    width = CMP_STRIDE * Dh
    rows = kv.transpose(0, 1, 3, 2, 4).reshape(2, B_ * G, S_ // CMP_STRIDE, width)
    out = pl.pallas_call(
        _compress_kernel,
        grid=(2, B_ * G),
        in_specs=[pl.BlockSpec((None, None, N_CMP_ROWS, width), lambda j, b: (j, b, 0, 0)),
                  pl.BlockSpec((None, 2, width), lambda j, b: (j, 0, 0)),
                  pl.BlockSpec((None, CMP_LEN * Dh, CMP_HIDDEN), lambda j, b: (j, 0, 0)),
                  pl.BlockSpec((None, CMP_HIDDEN, Dh), lambda j, b: (j, 0, 0))],
        out_specs=pl.BlockSpec((None, None, N_CMP_ROWS, Dh), lambda j, b: (j, b, 0, 0)),
        out_shape=jax.ShapeDtypeStruct((2, B_ * G, N_CMP_ROWS, Dh), F32),
        name="compress_kv",
    )(rows, cmp_pos.reshape(2, 2, width), w1.astype(BF16), w2.astype(BF16))
    return out[:, :, :n_cmp].reshape(2, B_, G, n_cmp, Dh).transpose(0, 1, 3, 2, 4)


def hybrid_mixer(x, norm_g, w_in, fox_f_bias, cmp_pos, cmp_w1, cmp_w2, w_branch, w_out):
    B_, S_, D = x.shape
    xf = x.reshape(B_ * S_, D)
    qkv, kvs, merge_logit, small = in_projection(xf, norm_g, w_in)
    heads = lambda t, n: t.reshape(B_, S_, n, HEAD_DIM)
    fq, fk, fv, nq = (qkv[:, i * BRANCH_WIDTH:(i + 1) * BRANCH_WIDTH] for i in range(4))
    kc, vc, ksl, vsl, kwn, vwn = (heads(kvs[:, i * KV_WIDTH:(i + 1) * KV_WIDTH], NSA_GROUPS) for i in range(6))
    f_logit = small[:, :FOX_HEADS].reshape(B_, S_, FOX_HEADS)
    gate_logit = small[:, 128:128 + NSA_HEADS * N_NSA_BRANCH].reshape(B_, S_, NSA_HEADS, N_NSA_BRANCH)
    log_f = jax.nn.log_sigmoid(f_logit + fox_f_bias.astype(F32))
    y_fox = fox_attention(heads(fq, FOX_HEADS), heads(fk, FOX_HEADS), heads(fv, FOX_HEADS), log_f)
    cos, sin = rope_tables(jnp.arange(S_, dtype=F32))
    q_nsa = apply_partial_rope(heads(nq, NSA_HEADS), cos, sin)
    k_slc = apply_partial_rope(ksl, cos, sin)
    k_win = apply_partial_rope(kwn, cos, sin)
    cmp = compress_kv_pallas(jnp.stack([kc, vc]), cmp_pos, cmp_w1, cmp_w2)
    n_cmp = cmp.shape[2]
    cmp_end = jnp.arange(n_cmp, dtype=F32) * CMP_STRIDE + (CMP_LEN - 1)
    k_cmp = apply_partial_rope(cmp[0], *rope_tables(cmp_end))
    y_nsa = nsa_attention(q_nsa, k_cmp, cmp[1], k_slc, vsl, k_win, vwn, gate_logit)
    return merge_out(xf, y_fox, y_nsa, merge_logit, w_branch, w_out)


def peer_ffn(h, norm_g, wq, subkeys, u, v):
    hn, q = peer_query(h, norm_g, wq)
    idx2, w = peer_topk(q, subkeys)
    coef = peer_u(idx2, hn, w, pack_table(u))
    return peer_v(idx2, coef, pack_table(v))


def kernel(x, norm_mix, w_in, fox_f_bias, nsa_cmp_pos, nsa_cmp_w1, nsa_cmp_w2, w_branch, w_out,
           norm_ffn, peer_wq, peer_subkeys, peer_u, peer_v, norm_final):
    B_, S_, D = x.shape
    assert norm_mix.shape[0] == 1, "single-layer trunk"
    h = hybrid_mixer(x, norm_mix[0], w_in[0], fox_f_bias[0], nsa_cmp_pos[0], nsa_cmp_w1[0], nsa_cmp_w2[0],
                     w_branch[0], w_out[0])
    p = peer_ffn(h, norm_ffn[0], peer_wq[0], peer_subkeys[0], peer_u[0], peer_v[0])
    return final_norm(h, p, norm_final).reshape(B_, S_, D)
```

```python
import jax
import jax.numpy as jnp
import numpy as np
from jax import lax
from jax.experimental import pallas as pl
from jax.experimental.pallas import tpu as pltpu

D_MODEL = 1024
HEAD_DIM = 64
FOX_HEADS = 8
NSA_HEADS = 8
NSA_GROUPS = 2
NSA_HPG = NSA_HEADS // NSA_GROUPS
BRANCH_WIDTH = 512
N_BRANCH = 2
N_NSA_BRANCH = 3
ROPE_DIM = HEAD_DIM // 4
ROPE_THETA = 500000.0
CMP_LEN = 32
CMP_STRIDE = 16
CMP_HIDDEN = 2 * HEAD_DIM
SLC_LEN = 64
SLC_TOP = 16
WINDOW = 512
FORCED_SCORE = 1e9
NEG_INF = -1e30
PEER_HEADS = 8
N_KEYS = 128
PEER_QDIM = 256
PEER_TOPK = 16
RMS_EPS = 1e-6
KV_WIDTH = NSA_GROUPS * HEAD_DIM
IN_SPLITS = (BRANCH_WIDTH, BRANCH_WIDTH, BRANCH_WIDTH, FOX_HEADS, BRANCH_WIDTH,
             KV_WIDTH, KV_WIDTH, KV_WIDTH, KV_WIDTH, KV_WIDTH, KV_WIDTH,
             NSA_HEADS * N_NSA_BRANCH, N_BRANCH * D_MODEL)


BF16 = jnp.bfloat16
F32 = jnp.float32

PEER_SLOTS = PEER_HEADS * PEER_TOPK
PEER_TT = 64
PEER_CHUNK_SLOTS = 32
PEER_U_UNROLL = 8
PEER_V_UNROLL = 8
ROW_SUB = D_MODEL // 2 // 128
VMEM_LIMIT_PEER = N_KEYS * N_KEYS * D_MODEL * 2 + 16 * 1024 * 1024


def pack_table(tab):
    n, d = tab.shape
    return pl.pallas_call(
        _pack_kernel,
        grid=(n // PACK_TM,),
        in_specs=[pl.BlockSpec((PACK_TM, d), lambda i: (i, 0))],
        out_specs=pl.BlockSpec((PACK_TM * ROW_SUB, 128), lambda i: (i, 0)),
        out_shape=jax.ShapeDtypeStruct((n * ROW_SUB, 128), jnp.uint32),
        name="pack_table",
    )(tab)


PACK_TM = 256


def _pack_kernel(x_ref, o_ref):
    bits = lax.bitcast_convert_type(x_ref[...].astype(jnp.bfloat16).astype(jnp.float32), jnp.uint32)
    for r in range(ROW_SUB):
        even = bits[:, (2 * r) * 128:(2 * r + 1) * 128]
        odd = bits[:, (2 * r + 1) * 128:(2 * r + 2) * 128]
        o_ref[pl.ds(r, PACK_TM, stride=ROW_SUB), :] = (even >> 16) | (odd & jnp.uint32(0xFFFF0000))


def _expert_slab(tab_ref, row):
    return tab_ref[pl.ds(pl.multiple_of(row, ROW_SUB), ROW_SUB), :]


def _unpack(w):
    lo = lax.bitcast_convert_type(w << 16, jnp.float32)
    hi = lax.bitcast_convert_type(w & jnp.uint32(0xFFFF0000), jnp.float32)
    return lo, hi


def _gelu(x):
    return 0.5 * x * (1.0 + lax.erf(x * (2.0 ** -0.5)))


def _peer_u_kernel(idx_a_ref, idx_b_ref, x_ref, w_ref, tab_ref, out_ref):
    lane = lax.broadcasted_iota(jnp.int32, (2 * ROW_SUB, 128), 1)
    lower = lax.broadcasted_iota(jnp.int32, (2 * ROW_SUB, 128), 0) >= ROW_SUB
    pair_lane = jnp.where(lower, lane - PEER_SLOTS // 2, lane)

    def token(t):
        row = x_ref[pl.ds(t, 1), :]
        chunks = [row[:, c * 128:(c + 1) * 128] for c in range(2 * ROW_SUB)]
        xl = jnp.concatenate(chunks[0::2] * 2, axis=0)
        xh = jnp.concatenate(chunks[1::2] * 2, axis=0)
        acc = jnp.zeros((2 * ROW_SUB, 128), jnp.float32)
        base = t * (PEER_SLOTS // 2)
        for s in range(PEER_SLOTS // 2):
            w = jnp.concatenate([_expert_slab(tab_ref, idx_a_ref[base + s]),
                                 _expert_slab(tab_ref, idx_b_ref[base + s])], axis=0)
            lo, hi = _unpack(w)
            part = jnp.sum(lo * xl + hi * xh, axis=1, keepdims=True)
            acc = jnp.where(pair_lane == s, part, acc)
        out_ref[pl.ds(t, 1), :] = w_ref[pl.ds(t, 1), :] * _gelu(jnp.sum(acc, axis=0, keepdims=True))

    def trip(i, carry):
        for j in range(PEER_U_UNROLL):
            token(i * PEER_U_UNROLL + j)
        return carry

    lax.fori_loop(0, PEER_TT // PEER_U_UNROLL, trip, 0)


N_CHUNK = 2 * ROW_SUB


def _peer_v_kernel(idx_a_ref, idx_b_ref, coef_ref, expand_ref, tab_ref, out_ref, ce_hi_ref, ce_lo_ref):
    half_slots = PEER_SLOTS // 2
    width = PEER_SLOTS * N_CHUNK
    coef = coef_ref[...]
    c_hi = _trunc_bf16(coef)
    expand = expand_ref[...]
    ce_hi_ref[...] = jnp.dot(c_hi.astype(BF16), expand, preferred_element_type=F32)
    ce_lo_ref[...] = jnp.dot((coef - c_hi).astype(BF16), expand, preferred_element_type=F32)
    own_chunk = ((lax.broadcasted_iota(jnp.int32, (N_CHUNK, width), 1) & (N_CHUNK - 1))
                 == lax.broadcasted_iota(jnp.int32, (N_CHUNK, width), 0))

    def token(t):
        base = t * half_slots
        slabs = [jnp.concatenate([_expert_slab(tab_ref, idx_a_ref[base + s]),
                                  _expert_slab(tab_ref, idx_b_ref[base + s])], axis=0) for s in range(half_slots)]
        gathered = pltpu.bitcast(jnp.concatenate(slabs, axis=0), BF16)
        sel = lambda ref: jnp.where(own_chunk, ref[pl.ds(t, 1), :], 0.0).astype(BF16)
        c = jnp.concatenate([sel(ce_hi_ref), sel(ce_lo_ref)], axis=0)
        o = jnp.dot(c, gathered, preferred_element_type=F32)
        o = o[:N_CHUNK] + o[N_CHUNK:]
        out_ref[pl.ds(t, 1), :] = jnp.concatenate([o[r:r + 1, :] for r in range(N_CHUNK)], axis=1)

    def trip(i, carry):
        for j in range(PEER_V_UNROLL):
            token(i * PEER_V_UNROLL + j)
        return carry

    lax.fori_loop(0, PEER_TT // PEER_V_UNROLL, trip, 0)


def _smem_spec():
    return pl.BlockSpec((PEER_TT * PEER_SLOTS // 2,), lambda i: (i,), memory_space=pltpu.SMEM)


def _slot_halves(a):
    return a[:, :PEER_SLOTS // 2].reshape(-1), a[:, PEER_SLOTS // 2:].reshape(-1)


def _table_spec(rows):
    return pl.BlockSpec((rows, 128), lambda i: (0, 0), pipeline_mode=pl.Buffered(1))


def peer_u(idx, x, w, tab):
    t = x.shape[0]
    return pl.pallas_call(
        _peer_u_kernel,
        grid=(t // PEER_TT,),
        in_specs=[_smem_spec(), _smem_spec(), pl.BlockSpec((PEER_TT, D_MODEL), lambda i: (i, 0)),
                  pl.BlockSpec((PEER_TT, PEER_SLOTS), lambda i: (i, 0)), _table_spec(tab.shape[0])],
        out_specs=pl.BlockSpec((PEER_TT, PEER_SLOTS), lambda i: (i, 0)),
        out_shape=jax.ShapeDtypeStruct((t, PEER_SLOTS), jnp.float32),
        compiler_params=pltpu.CompilerParams(vmem_limit_bytes=VMEM_LIMIT_PEER),
        name="peer_u",
    )(*_slot_halves(idx), x, w, tab)


def peer_v(idx, coef, tab):
    t = idx.shape[0]
    width = PEER_SLOTS * N_CHUNK
    half = PEER_SLOTS // 2
    col_slot = np.arange(width) // N_CHUNK
    col_slot = (col_slot // 2) + half * (col_slot % 2)
    expand = jnp.asarray(col_slot[None, :] == np.arange(PEER_SLOTS)[:, None], dtype=BF16)
    return pl.pallas_call(
        _peer_v_kernel,
        grid=(t // PEER_TT,),
        in_specs=[_smem_spec(), _smem_spec(), pl.BlockSpec((PEER_TT, PEER_SLOTS), lambda i: (i, 0)),
                  pl.BlockSpec((PEER_SLOTS, width), lambda i: (0, 0)), _table_spec(tab.shape[0])],
        out_specs=pl.BlockSpec((PEER_TT, D_MODEL), lambda i: (i, 0)),
        out_shape=jax.ShapeDtypeStruct((t, D_MODEL), jnp.float32),
        scratch_shapes=[pltpu.VMEM((PEER_TT, width), F32), pltpu.VMEM((PEER_TT, width), F32)],
        compiler_params=pltpu.CompilerParams(vmem_limit_bytes=VMEM_LIMIT_PEER),
        name="peer_v",
    )(*_slot_halves(idx), coef, expand, tab)


PK_TT = 256
PK_A_BLOCKS = ((0, 16), (1, 8), (2, 5), (3, 4))
PK_B_BLOCKS = ((0, 4, 16), (1, 4, 8), (2, 4, 5))
NEG_HUGE = -3.0e38
PK_HEADS_PER_TRIP = 4


def _extract_top(problems, flat, n):
    vals = [[] for _ in problems]
    picked = [[] for _ in problems]
    cur = [v for v, _ in problems]
    for _ in range(n):
        for i, (_, payload) in enumerate(problems):
            m = jnp.max(cur[i], axis=0, keepdims=True)
            pos = jnp.min(jnp.where(cur[i] == m, flat, 1e9), axis=0, keepdims=True)
            hit = flat == pos
            picked[i].append(pos if payload is None else jnp.sum(jnp.where(hit, payload, 0.0), axis=0, keepdims=True))
            cur[i] = jnp.where(hit, NEG_HUGE, cur[i])
            vals[i].append(m)
    return [jnp.concatenate(v, axis=0) for v in vals], [jnp.concatenate(p, axis=0) for p in picked]


def _candidate_grid(first, second, combine):
    up8 = lambda n: -(-n // 8) * 8
    return jnp.concatenate([combine(first[a:a + 1], second[0:up8(nb)]) for a, nb in PK_A_BLOCKS]
                           + [combine(first[0:up8(a1)], second[b:b + 1]) for b, _, a1 in PK_B_BLOCKS], axis=0)


def _candidate_valid_and_flat():
    up8 = lambda n: -(-n // 8) * 8
    valid, flat = [], []
    for a, nb in PK_A_BLOCKS:
        r = lax.broadcasted_iota(jnp.int32, (up8(nb), PK_TT), 0)
        valid.append(r < nb)
        flat.append(a * PEER_TOPK + r)
    for b, a0, a1 in PK_B_BLOCKS:
        r = lax.broadcasted_iota(jnp.int32, (up8(a1), PK_TT), 0)
        valid.append((r >= a0) & (r < a1))
        flat.append(r * PEER_TOPK + b)
    valid = jnp.concatenate(valid, axis=0)
    return valid, jnp.where(valid, jnp.concatenate(flat, axis=0).astype(F32), 1e9)


def _peer_topk_kernel(q_ref, ka_ref, eid_ref, w_ref):
    key_row = lax.broadcasted_iota(jnp.int32, (N_KEYS, PK_TT), 0).astype(F32)
    valid, flat = _candidate_valid_and_flat()

    def heads(i, carry):
        scores = []
        for j in range(2 * PK_HEADS_PER_TRIP):
            q = q_ref[2 * PK_HEADS_PER_TRIP * i + j]
            q_hi = _trunc_bf16(q)
            qa = jnp.concatenate([q_hi.astype(BF16), (q - q_hi).astype(BF16), q_hi.astype(BF16)], axis=1)
            scores.append(lax.dot_general(ka_ref[2 * PK_HEADS_PER_TRIP * i + j], qa, (((1,), (1,)), ((), ())),
                                          preferred_element_type=F32))
        sub_s, sub_i = _extract_top([(s, None) for s in scores], key_row, PEER_TOPK)
        cands = []
        for j in range(PK_HEADS_PER_TRIP):
            s1, s2, i1, i2 = sub_s[2 * j], sub_s[2 * j + 1], sub_i[2 * j], sub_i[2 * j + 1]
            cands.append((jnp.where(valid, _candidate_grid(s1, s2, lambda x, y: x + y), NEG_HUGE),
                          _candidate_grid(i1, i2, lambda x, y: x * N_KEYS + y)))
        top_s, top_e = _extract_top(cands, flat, PEER_TOPK)
        for j in range(PK_HEADS_PER_TRIP):
            e = jnp.exp(top_s[j] - top_s[j][0:1])
            w_ref[PK_HEADS_PER_TRIP * i + j] = e / jnp.sum(e, axis=0, keepdims=True)
            eid_ref[PK_HEADS_PER_TRIP * i + j] = top_e[j].astype(jnp.int32) * ROW_SUB
        return carry

    lax.fori_loop(0, PEER_HEADS // PK_HEADS_PER_TRIP, heads, 0)


def peer_topk(qt, subkeys):
    hp, t, d = qt.shape
    k_hi, k_lo = _split2(subkeys.reshape(hp, N_KEYS, d))
    ka = jnp.concatenate([k_hi, k_hi, k_lo], axis=2)
    eid, w = pl.pallas_call(
        _peer_topk_kernel,
        grid=(t // PK_TT,),
        in_specs=[pl.BlockSpec((hp, PK_TT, d), lambda i: (0, i, 0)),
                  pl.BlockSpec((hp, N_KEYS, 3 * d), lambda i: (0, 0, 0))],
        out_specs=[pl.BlockSpec((PEER_HEADS, PEER_TOPK, PK_TT), lambda i: (0, 0, i)),
                   pl.BlockSpec((PEER_HEADS, PEER_TOPK, PK_TT), lambda i: (0, 0, i))],
        out_shape=[jax.ShapeDtypeStruct((PEER_HEADS, PEER_TOPK, t), jnp.int32),
                   jax.ShapeDtypeStruct((PEER_HEADS, PEER_TOPK, t), F32)],
        name="peer_topk",
    )(qt, ka)
    to_slots = lambda a: a.transpose(2, 0, 1).reshape(t, PEER_SLOTS)
    return to_slots(eid), to_slots(w)


def split_columns(t, sizes):
    offs = np.cumsum((0,) + tuple(sizes))
    return [t[..., int(a):int(b)] for a, b in zip(offs[:-1], offs[1:])]


def rope_tables(pos):
    inv = jnp.power(ROPE_THETA, -jnp.arange(0, ROPE_DIM, 2, dtype=jnp.float32) / ROPE_DIM)
    ang = pos[:, None] * inv[None, :]
    return jnp.cos(ang), jnp.sin(ang)


def apply_partial_rope(x, cos, sin):
    half = ROPE_DIM // 2
    xr = x[..., :ROPE_DIM].astype(jnp.float32)
    x1, x2 = xr[..., :half], xr[..., half:]
    c, s = cos[:, None, :], sin[:, None, :]
    rot = jnp.concatenate([x1 * c - x2 * s, x1 * s + x2 * c], axis=-1)
    return jnp.concatenate([rot.astype(x.dtype), x[..., ROPE_DIM:]], axis=-1)


def _logits(ks, qts):
    return [jnp.dot(k, qt, preferred_element_type=F32) for k, qt in zip(ks, qts)]


def _flash_update(ss, vts, mask, carries):
    stats, ps = [], []
    for s, (m, l, _) in zip(ss, carries):
        s_vis = s if mask is None else jnp.where(mask, s, NEG_INF)
        m_new = jnp.maximum(m, jnp.max(s_vis, axis=0, keepdims=True))
        alpha = jnp.exp(m - m_new)
        p = jnp.exp(s - m_new)
        if mask is not None:
            p = jnp.where(mask, p, 0.0)
        stats.append((m_new, alpha, alpha * l + jnp.sum(p, axis=0, keepdims=True)))
        ps.append(p.astype(BF16))
    return tuple((m_new, l, alpha * acc + jnp.dot(vt, p, preferred_element_type=F32))
                 for (m_new, alpha, l), p, vt, (_, _, acc) in zip(stats, ps, vts, carries))


def _causal_flash(key_tile, value_tile, qts, n_full, last_mask, n):
    def body(kt, carries):
        return _flash_update(_logits(key_tile(kt), qts), value_tile(kt), None, carries)

    carries = lax.fori_loop(0, n_full, body, tuple(_flash_init(n) for _ in qts))
    return _flash_update(_logits(key_tile(n_full), qts), value_tile(n_full), last_mask, carries)


def _flash_init(n):
    return (jnp.full((1, n), NEG_INF, F32), jnp.zeros((1, n), F32), jnp.zeros((HEAD_DIM, n), F32))


FOX_TQ = 256
FOX_KV = 512
FOX_DK = 256


FOX_HP = 4


def _fox_kernel(qt_ref, k_ref, vt_ref, o_ref):
    qi = pl.program_id(1)
    q0 = qi * FOX_TQ
    qts = [qt_ref[h] for h in range(FOX_HP)]
    qpos = q0 + lax.broadcasted_iota(jnp.int32, (1, FOX_TQ), 1)
    krow = lax.broadcasted_iota(jnp.int32, (FOX_KV, FOX_TQ), 0)

    n_full = q0 // FOX_KV
    causal = (n_full * FOX_KV + krow) <= qpos
    carry = _causal_flash(lambda kt: [k_ref[h, kt] for h in range(FOX_HP)],
                          lambda kt: [vt_ref[h, kt] for h in range(FOX_HP)], qts, n_full, causal, FOX_TQ)
    for h in range(FOX_HP):
        _, l, acc = carry[h]
        o_ref[h] = acc / l


def _trunc_bf16(x):
    bits = lax.bitcast_convert_type(x, jnp.uint32) & jnp.uint32(0xFFFF0000)
    return lax.bitcast_convert_type(bits, F32)


def _split3(c):
    c1 = _trunc_bf16(c)
    r = c - c1
    c2 = _trunc_bf16(r)
    return c1.astype(BF16), c2.astype(BF16), (r - c2).astype(BF16)


def _split2(x):
    hi = _trunc_bf16(x)
    return hi.astype(BF16), (x - hi).astype(BF16)


def _aug_q(q):
    hi, lo = _split2(q)
    return jnp.concatenate([hi, hi, lo], axis=-1)


def _aug_k(k):
    hi, lo = _split2(k)
    return jnp.concatenate([hi, lo, hi], axis=-1)


def fox_attention(q, k, v, log_f):
    B_, S_, H, Dh = q.shape
    KV_TILE = FOX_KV
    n_qt, n_kt = S_ // FOX_TQ, S_ // KV_TILE
    c = jnp.cumsum(log_f, axis=1)
    c1, c2, c3 = _split3(c)
    j = jnp.arange(FOX_DK - 3 * Dh)
    pick = lambda t: t[..., None].astype(F32)
    terms = lambda first: jnp.where(j == first, pick(c1), jnp.where(j == first + 1, pick(c2), pick(c3)))
    q_extra = jnp.where(j < 3, 1.0, jnp.where(j < 6, terms(3), 0.0)).astype(BF16)
    k_extra = jnp.where(j < 3, -terms(0), jnp.where(j < 6, 1.0, 0.0)).astype(BF16)
    qa = jnp.concatenate([_aug_q(q * (Dh ** -0.5)), q_extra], axis=-1)
    ka = jnp.concatenate([_aug_k(k), k_extra], axis=-1)
    HP, HG = FOX_HP, H // FOX_HP
    qt = qa.reshape(B_, n_qt, FOX_TQ, HG, HP, FOX_DK).transpose(0, 3, 1, 4, 5, 2).reshape(B_ * HG * n_qt, HP, FOX_DK, FOX_TQ)
    kk = ka.reshape(B_, n_kt, KV_TILE, HG, HP, FOX_DK).transpose(0, 3, 4, 1, 2, 5).reshape(B_ * HG, HP, n_kt, KV_TILE, FOX_DK)
    vt = v.astype(BF16).reshape(B_, n_kt, KV_TILE, HG, HP, Dh).transpose(0, 3, 4, 1, 5, 2).reshape(B_ * HG, HP, n_kt, Dh, KV_TILE)
    out = pl.pallas_call(
        _fox_kernel,
        grid=(B_ * HG, n_qt),
        in_specs=[pl.BlockSpec((None, HP, FOX_DK, FOX_TQ), lambda b, i: (b * n_qt + i, 0, 0, 0)),
                  pl.BlockSpec((None, HP, n_kt, KV_TILE, FOX_DK), lambda b, i: (b, 0, 0, 0, 0)),
                  pl.BlockSpec((None, HP, n_kt, Dh, KV_TILE), lambda b, i: (b, 0, 0, 0, 0))],
        out_specs=pl.BlockSpec((None, HP, Dh, FOX_TQ), lambda b, i: (b * n_qt + i, 0, 0, 0)),
        out_shape=jax.ShapeDtypeStruct((B_ * HG * n_qt, HP, Dh, FOX_TQ), F32),
        name="fox_attention",
    )(qt, kk, vt)
    return out.reshape(B_, HG, n_qt, HP, Dh, FOX_TQ)


NSA_TQ = 128
NSA_N = NSA_HPG * NSA_TQ
N_CMP_PAD = 256
N_SLC = 64
NSA_DK = 3 * HEAD_DIM
NSA_KV = 512
WIN_TILES = WINDOW // NSA_TQ + 1
WIN_KEYS = WIN_TILES * NSA_TQ


def _nsa_kernel(qt_ref, kc_ref, vct_ref, ks_ref, vst_ref, kw_ref, vwt_ref, gate_ref, ovl_ref, o_ref):
    qi = pl.program_id(1)
    q0 = qi * NSA_TQ
    qt = qt_ref[...]
    lane = lax.broadcasted_iota(jnp.int32, (1, NSA_N), 1)
    qpos = q0 + (lane & (NSA_TQ - 1))

    hn = NSA_N // 2
    qts = (qt[:, :hn], qt[:, hn:])
    qpos_h = qpos[:, :hn]

    w0 = jnp.maximum(qi - WINDOW // NSA_TQ, 0)
    k_w = jnp.concatenate([kw_ref[w0 + i] for i in range(WIN_TILES)], axis=0)
    vt_w = jnp.concatenate([vwt_ref[w0 + i] for i in range(WIN_TILES)], axis=1)
    s = jnp.dot(kc_ref[...], qt, preferred_element_type=F32)
    s_w = [jnp.dot(k_w, qts[h], preferred_element_type=F32) for h in range(2)]

    cmp_end = lax.broadcasted_iota(jnp.int32, (N_CMP_PAD, NSA_N), 0) * CMP_STRIDE + (CMP_LEN - 1)
    mc = cmp_end <= qpos
    m = jnp.max(jnp.where(mc, s, NEG_INF), axis=0, keepdims=True)
    p = jnp.where(mc, jnp.exp(s - m), 0.0)
    l = jnp.sum(p, axis=0, keepdims=True)
    pc = p * jnp.where(l > 0.0, 1.0 / l, 0.0)
    o_cmp = jnp.dot(vct_ref[...], pc.astype(BF16), preferred_element_type=F32)
    pcs = (pc[:, 0:NSA_TQ] + pc[:, NSA_TQ:2 * NSA_TQ]) + (pc[:, 2 * NSA_TQ:3 * NSA_TQ] + pc[:, 3 * NSA_TQ:4 * NSA_TQ])
    pcs_hi = pcs.astype(BF16)
    pcs_lo = (pcs - pcs_hi.astype(F32)).astype(BF16)
    ovl = ovl_ref[...]
    imp = jnp.dot(ovl, pcs_hi, preferred_element_type=F32) + jnp.dot(ovl, pcs_lo, preferred_element_type=F32)

    rel = qpos_h - (w0 * NSA_TQ + lax.broadcasted_iota(jnp.int32, (WIN_KEYS, hn), 0))
    in_win = (rel >= 0) & (rel < WINDOW)
    outs_w = []
    for h in range(2):
        m = jnp.max(jnp.where(in_win, s_w[h], NEG_INF), axis=0, keepdims=True)
        p = jnp.where(in_win, jnp.exp(s_w[h] - m), 0.0)
        l = jnp.sum(p, axis=0, keepdims=True)
        outs_w.append(jnp.dot(vt_w, p.astype(BF16), preferred_element_type=F32) / l)
    acc_w = jnp.concatenate(outs_w, axis=1)

    blk = lax.broadcasted_iota(jnp.int32, (N_SLC, NSA_TQ), 0)
    q_blk = (q0 + lax.broadcasted_iota(jnp.int32, (N_SLC, NSA_TQ), 1)) >> 6
    forced = (blk == 0) | (blk == q_blk) | (blk == q_blk - 1)
    score = jnp.where(forced, FORCED_SCORE, jnp.where(blk <= q_blk, imp, -1.0))
    rank = jnp.zeros((N_SLC, NSA_TQ), jnp.int32)
    for i in range(N_SLC):
        row = score[i:i + 1, :]
        beats = (row > score) | ((row == score) & (blk > i))
        rank = rank + beats.astype(jnp.int32)
    bias = jnp.where(rank < SLC_TOP, 0.0, NEG_INF).astype(BF16)
    bias = jnp.concatenate([bias] * (NSA_HPG // 2), axis=1)
    qts_s = [jnp.concatenate([qh, bias], axis=0) for qh in qts]
    krow = lax.broadcasted_iota(jnp.int32, (NSA_KV, hn), 0)

    n_full = q0 // NSA_KV
    causal = (n_full * NSA_KV + krow) <= qpos_h
    slc = _causal_flash(lambda kt: [ks_ref[kt]] * 2, lambda kt: [vst_ref[kt]] * 2, qts_s, n_full, causal, hn)
    acc_s = jnp.concatenate([slc[0][2] / slc[0][1], slc[1][2] / slc[1][1]], axis=1)

    g = jax.nn.sigmoid(gate_ref[...])
    o_ref[...] = g[0:1] * o_cmp + g[1:2] * acc_s + g[2:3] * acc_w


def nsa_attention(q, k_cmp, v_cmp, k_slc, v_slc, k_win, v_win, gate_logit):
    B_, S_, H, Dh = q.shape
    G = NSA_GROUPS
    n_qt = S_ // NSA_TQ
    n_cmp = k_cmp.shape[1]
    cs = np.arange(N_CMP_PAD)[None, :] * CMP_STRIDE
    ss = np.arange(N_SLC)[:, None] * SLC_LEN
    ov = np.clip(np.minimum(cs + CMP_LEN, ss + SLC_LEN) - np.maximum(cs, ss), 0, None) / CMP_LEN
    ov[:, n_cmp:] = 0.0
    ovl = jnp.asarray(ov, dtype=BF16)
    qs = _aug_q(q * (Dh ** -0.5)).reshape(B_, n_qt, NSA_TQ, G, NSA_HPG, NSA_DK)
    qt = qs.transpose(0, 3, 1, 5, 4, 2).reshape(B_ * G * n_qt, NSA_DK, NSA_N)
    gt = gate_logit.astype(F32).reshape(B_, n_qt, NSA_TQ, G, NSA_HPG, N_NSA_BRANCH)
    gt = gt.transpose(0, 3, 1, 5, 4, 2).reshape(B_ * G * n_qt, N_NSA_BRANCH, NSA_N)
    tiles = lambda t, kv: t.reshape(B_, S_ // kv, kv, G, -1).transpose(0, 3, 1, 2, 4).reshape(B_ * G, S_ // kv, kv, t.shape[-1])
    keys = lambda t, kv: tiles(_aug_k(t), kv)
    block_onehot = (jnp.arange(S_)[:, None] // SLC_LEN == jnp.arange(N_SLC)[None, :]).astype(BF16)
    k_slc_aug = jnp.concatenate([_aug_k(k_slc), jnp.broadcast_to(block_onehot[None, :, None, :], (B_, S_, G, N_SLC))], axis=-1)
    vals = lambda t, kv: t.astype(BF16).reshape(B_, S_ // kv, kv, G, Dh).transpose(0, 3, 1, 4, 2).reshape(B_ * G, S_ // kv, Dh, kv)
    padc = ((0, 0), (0, N_CMP_PAD - n_cmp), (0, 0), (0, 0))
    kc = _aug_k(jnp.pad(k_cmp, padc)).transpose(0, 2, 1, 3).reshape(B_ * G, N_CMP_PAD, NSA_DK)
    vct = jnp.pad(v_cmp, padc).astype(BF16).transpose(0, 2, 3, 1).reshape(B_ * G, Dh, N_CMP_PAD)
    per_bg = lambda *blk: pl.BlockSpec((None,) + blk, lambda b, i: (b,) + (0,) * len(blk))
    per_tile = lambda *blk: pl.BlockSpec((None,) + blk, lambda b, i: (b * n_qt + i,) + (0,) * len(blk))
    out = pl.pallas_call(
        _nsa_kernel,
        grid=(B_ * G, n_qt),
        in_specs=[per_tile(NSA_DK, NSA_N),
                  per_bg(N_CMP_PAD, NSA_DK), per_bg(Dh, N_CMP_PAD),
                  per_bg(S_ // NSA_KV, NSA_KV, NSA_DK + N_SLC), per_bg(S_ // NSA_KV, Dh, NSA_KV),
                  per_bg(n_qt, NSA_TQ, NSA_DK), per_bg(n_qt, Dh, NSA_TQ),
                  per_tile(N_NSA_BRANCH, NSA_N),
                  pl.BlockSpec((N_SLC, N_CMP_PAD), lambda b, i: (0, 0))],
        out_specs=per_tile(Dh, NSA_N),
        out_shape=jax.ShapeDtypeStruct((B_ * G * n_qt, Dh, NSA_N), F32),
        name="nsa_attention",
    )(qt, kc, vct, tiles(k_slc_aug, NSA_KV), vals(v_slc, NSA_KV), keys(k_win, NSA_TQ), vals(v_win, NSA_TQ), gt, ovl)
    return out.reshape(B_, G, n_qt, Dh, NSA_N)


DENSE_TM = 256
SEC_MERGE = N_BRANCH * D_MODEL
SEC_SMALL = 256
IN_SECTIONS = (BRANCH_WIDTH,) * 4 + (KV_WIDTH,) * 6 + (SEC_MERGE, SEC_SMALL)


def _rms(x, g):
    return x * lax.rsqrt(jnp.mean(x * x, axis=-1, keepdims=True) + RMS_EPS) * g


def _row_spec(width):
    return pl.BlockSpec((DENSE_TM, width), lambda i: (i, 0))


def _whole_spec(shape):
    return pl.BlockSpec(shape, lambda i: (0,) * len(shape))


def _in_proj_kernel(x_ref, g_ref, w_ref, *o_refs):
    xn = _rms(x_ref[...], g_ref[...]).astype(BF16)
    off = 0
    for o_ref, width in zip(o_refs, IN_SECTIONS):
        o_ref[...] = jnp.dot(xn, w_ref[:, off:off + width], preferred_element_type=F32)
        off += width


def in_projection(x, g, w_in):
    t, d = x.shape
    fq, fk, fv, fl, nq, kc, vc, ksl, vsl, kwn, vwn, gl, ml = split_columns(w_in, IN_SPLITS)
    pad = lambda w: jnp.pad(w, ((0, 0), (0, 128 - w.shape[1])))
    w = jnp.concatenate([fq, fk, fv, nq, kc, vc, ksl, vsl, kwn, vwn, ml, pad(fl), pad(gl)], axis=1).astype(BF16)
    return pl.pallas_call(
        _in_proj_kernel,
        grid=(t // DENSE_TM,),
        in_specs=[_row_spec(d), _whole_spec((1, d)), _whole_spec(w.shape)],
        out_specs=[_row_spec(s) for s in IN_SECTIONS],
        out_shape=[jax.ShapeDtypeStruct((t, s), F32) for s in IN_SECTIONS],
        name="in_projection",
    )(x, g.reshape(1, d), w)


def _merge_kernel(x_ref, yf_ref, yn_ref, ml_ref, wb_ref, wo_ref, o_ref):
    yf_t = yf_ref[...].reshape(BRANCH_WIDTH, DENSE_TM)
    yn_t = jnp.concatenate(
        [jnp.concatenate([yn_ref[g, j, :, h * NSA_TQ:(h + 1) * NSA_TQ]
                          for g in range(NSA_GROUPS) for h in range(NSA_HPG)], axis=0)
         for j in range(DENSE_TM // NSA_TQ)], axis=1)
    g = jax.nn.sigmoid(ml_ref[...])
    up_f = jnp.dot(yf_t.T.astype(BF16), wb_ref[0], preferred_element_type=F32)
    up_n = jnp.dot(yn_t.T.astype(BF16), wb_ref[1], preferred_element_type=F32)
    merged = g[:, :D_MODEL] * up_f + g[:, D_MODEL:] * up_n
    o_ref[...] = x_ref[...] + jnp.dot(merged.astype(BF16), wo_ref[...], preferred_element_type=F32)


def merge_out(x, y_fox, y_nsa, merge_logit, w_branch, w_out):
    t, d = x.shape
    assert FOX_TQ == DENSE_TM and DENSE_TM % NSA_TQ == 0
    tiles_per_seq = y_fox.shape[2]
    nsa_per_tile = DENSE_TM // NSA_TQ
    yf_spec = pl.BlockSpec((None,) + y_fox.shape[1:2] + (None,) + y_fox.shape[3:],
                           lambda i: (i // tiles_per_seq, 0, i % tiles_per_seq, 0, 0, 0))
    yn_spec = pl.BlockSpec((None, NSA_GROUPS, nsa_per_tile, HEAD_DIM, NSA_N),
                           lambda i: (i // tiles_per_seq, 0, i % tiles_per_seq, 0, 0))
    return pl.pallas_call(
        _merge_kernel,
        grid=(t // DENSE_TM,),
        in_specs=[_row_spec(d), yf_spec, yn_spec, _row_spec(SEC_MERGE),
                  _whole_spec(w_branch.shape), _whole_spec(w_out.shape)],
        out_specs=_row_spec(d),
        out_shape=jax.ShapeDtypeStruct((t, d), F32),
        name="merge_out",
    )(x, y_fox, y_nsa, merge_logit, w_branch.astype(BF16), w_out.astype(BF16))


def _peer_q_kernel(x_ref, g_ref, w_ref, hn_ref, q_ref):
    hn = _rms(x_ref[...], g_ref[...])
    hn_ref[...] = hn
    q = jnp.dot(hn.astype(BF16), w_ref[...], preferred_element_type=F32)
    d = PEER_QDIM // 2
    for j in range(2 * PEER_HEADS):
        q_ref[j] = q[:, j * d:(j + 1) * d]


def peer_query(x, g, wq):
    t, d = x.shape
    hp, dq = 2 * PEER_HEADS, PEER_QDIM // 2
    return pl.pallas_call(
        _peer_q_kernel,
        grid=(t // DENSE_TM,),
        in_specs=[_row_spec(d), _whole_spec((1, d)), _whole_spec(wq.shape)],
        out_specs=[_row_spec(d), pl.BlockSpec((hp, DENSE_TM, dq), lambda i: (0, i, 0))],
        out_shape=[jax.ShapeDtypeStruct((t, d), F32), jax.ShapeDtypeStruct((hp, t, dq), F32)],
        name="peer_query",
    )(x, g.reshape(1, d), wq.astype(BF16))


def _final_kernel(h_ref, p_ref, g_ref, o_ref):
    o_ref[...] = _rms(h_ref[...] + p_ref[...], g_ref[...])


def final_norm(h, p, g):
    t, d = h.shape
    return pl.pallas_call(
        _final_kernel,
        grid=(t // DENSE_TM,),
        in_specs=[_row_spec(d), _row_spec(d), _whole_spec((1, d))],
        out_specs=_row_spec(d),
        out_shape=jax.ShapeDtypeStruct((t, d), F32),
        name="final_norm",
    )(h, p, g.reshape(1, d))


N_CMP_ROWS = 256


def _compress_kernel(r_ref, pos_ref, w1_ref, w2_ref, o_ref):
    r = r_ref[...]
    nxt = jnp.concatenate([r[1:], jnp.zeros((1, r.shape[1]), F32)], axis=0)
    half = r.shape[1]
    hid = (jnp.dot((r + pos_ref[0:1, :]).astype(BF16), w1_ref[:half], preferred_element_type=F32)
           + jnp.dot((nxt + pos_ref[1:2, :]).astype(BF16), w1_ref[half:], preferred_element_type=F32))
    o_ref[...] = jnp.dot(_gelu(hid).astype(BF16), w2_ref[...], preferred_element_type=F32)


def compress_kv_pallas(kv, cmp_pos, w1, w2):
    _, B_, S_, G, Dh = kv.shape
    n_cmp = (S_ - CMP_LEN) // CMP_STRIDE + 1
    width = CMP_STRIDE * Dh
    rows = kv.transpose(0, 1, 3, 2, 4).reshape(2, B_ * G, S_ // CMP_STRIDE, width)
    out = pl.pallas_call(
        _compress_kernel,
        grid=(2, B_ * G),
        in_specs=[pl.BlockSpec((None, None, N_CMP_ROWS, width), lambda j, b: (j, b, 0, 0)),
                  pl.BlockSpec((None, 2, width), lambda j, b: (j, 0, 0)),
                  pl.BlockSpec((None, CMP_LEN * Dh, CMP_HIDDEN), lambda j, b: (j, 0, 0)),
                  pl.BlockSpec((None, CMP_HIDDEN, Dh), lambda j, b: (j, 0, 0))],
        out_specs=pl.BlockSpec((None, None, N_CMP_ROWS, Dh), lambda j, b: (j, b, 0, 0)),
        out_shape=jax.ShapeDtypeStruct((2, B_ * G, N_CMP_ROWS, Dh), F32),
        name="compress_kv",
    )(rows, cmp_pos.reshape(2, 2, width), w1.astype(BF16), w2.astype(BF16))
    return out[:, :, :n_cmp].reshape(2, B_, G, n_cmp, Dh).transpose(0, 1, 3, 2, 4)


def hybrid_mixer(x, norm_g, w_in, fox_f_bias, cmp_pos, cmp_w1, cmp_w2, w_branch, w_out):
    B_, S_, D = x.shape
    xf = x.reshape(B_ * S_, D)
    secs = in_projection(xf, norm_g, w_in)
    heads = lambda t, n: t.reshape(B_, S_, n, HEAD_DIM)
    fq, fk, fv, nq = secs[:4]
    kc, vc, ksl, vsl, kwn, vwn = (heads(t, NSA_GROUPS) for t in secs[4:10])
    merge_logit, small = secs[10:]
    f_logit = small[:, :FOX_HEADS].reshape(B_, S_, FOX_HEADS)
    gate_logit = small[:, 128:128 + NSA_HEADS * N_NSA_BRANCH].reshape(B_, S_, NSA_HEADS, N_NSA_BRANCH)
    log_f = jax.nn.log_sigmoid(f_logit + fox_f_bias.astype(F32))
    y_fox = fox_attention(heads(fq, FOX_HEADS), heads(fk, FOX_HEADS), heads(fv, FOX_HEADS), log_f)
    cos, sin = rope_tables(jnp.arange(S_, dtype=F32))
    q_nsa = apply_partial_rope(heads(nq, NSA_HEADS), cos, sin)
    k_slc = apply_partial_rope(ksl, cos, sin)
    k_win = apply_partial_rope(kwn, cos, sin)
    cmp = compress_kv_pallas(jnp.stack([kc, vc]), cmp_pos, cmp_w1, cmp_w2)
    n_cmp = cmp.shape[2]
    cmp_end = jnp.arange(n_cmp, dtype=F32) * CMP_STRIDE + (CMP_LEN - 1)
    k_cmp = apply_partial_rope(cmp[0], *rope_tables(cmp_end))
    y_nsa = nsa_attention(q_nsa, k_cmp, cmp[1], k_slc, vsl, k_win, vwn, gate_logit)
    return merge_out(xf, y_fox, y_nsa, merge_logit, w_branch, w_out)


def peer_ffn(h, norm_g, wq, subkeys, u, v):
    hn, q = peer_query(h, norm_g, wq)
    idx2, w = peer_topk(q, subkeys)
    coef = peer_u(idx2, hn, w, pack_table(u))
    return peer_v(idx2, coef, pack_table(v))


def kernel(x, norm_mix, w_in, fox_f_bias, nsa_cmp_pos, nsa_cmp_w1, nsa_cmp_w2, w_branch, w_out,
           norm_ffn, peer_wq, peer_subkeys, peer_u, peer_v, norm_final):
    B_, S_, D = x.shape
    assert norm_mix.shape[0] == 1, "single-layer trunk"
    h = hybrid_mixer(x, norm_mix[0], w_in[0], fox_f_bias[0], nsa_cmp_pos[0], nsa_cmp_w1[0], nsa_cmp_w2[0],
                     w_branch[0], w_out[0])
    p = peer_ffn(h, norm_ffn[0], peer_wq[0], peer_subkeys[0], peer_u[0], peer_v[0])
    return final_norm(h, p, norm_final).reshape(B_, S_, D)
```

```python
import jax
import jax.numpy as jnp
import numpy as np
from jax import lax
from jax.experimental import pallas as pl
from jax.experimental.pallas import tpu as pltpu

D_MODEL = 1024
HEAD_DIM = 64
FOX_HEADS = 8
NSA_HEADS = 8
NSA_GROUPS = 2
NSA_HPG = NSA_HEADS // NSA_GROUPS
BRANCH_WIDTH = 512
N_BRANCH = 2
N_NSA_BRANCH = 3
ROPE_DIM = HEAD_DIM // 4
ROPE_THETA = 500000.0
CMP_LEN = 32
CMP_STRIDE = 16
CMP_HIDDEN = 2 * HEAD_DIM
SLC_LEN = 64
SLC_TOP = 16
WINDOW = 512
FORCED_SCORE = 1e9
NEG_INF = -1e30
PEER_HEADS = 8
N_KEYS = 128
PEER_QDIM = 256
PEER_TOPK = 16
RMS_EPS = 1e-6
KV_WIDTH = NSA_GROUPS * HEAD_DIM
IN_SPLITS = (BRANCH_WIDTH, BRANCH_WIDTH, BRANCH_WIDTH, FOX_HEADS, BRANCH_WIDTH,
             KV_WIDTH, KV_WIDTH, KV_WIDTH, KV_WIDTH, KV_WIDTH, KV_WIDTH,
             NSA_HEADS * N_NSA_BRANCH, N_BRANCH * D_MODEL)


BF16 = jnp.bfloat16
F32 = jnp.float32

PEER_SLOTS = PEER_HEADS * PEER_TOPK
PEER_TT = 128
PEER_CHUNK_SLOTS = 32
PEER_U_UNROLL = 8
PEER_V_UNROLL = 8
ROW_SUB = D_MODEL // 2 // 128
VMEM_LIMIT_PEER = N_KEYS * N_KEYS * D_MODEL * 2 + 16 * 1024 * 1024


def pack_table(tab):
    n, d = tab.shape
    return pl.pallas_call(
        _pack_kernel,
        grid=(n // PACK_TM,),
        in_specs=[pl.BlockSpec((PACK_TM, d), lambda i: (i, 0))],
        out_specs=pl.BlockSpec((PACK_TM * ROW_SUB, 128), lambda i: (i, 0)),
        out_shape=jax.ShapeDtypeStruct((n * ROW_SUB, 128), jnp.uint32),
        name="pack_table",
    )(tab)


PACK_TM = 256


def _pack_kernel(x_ref, o_ref):
    bits = lax.bitcast_convert_type(x_ref[...].astype(jnp.bfloat16).astype(jnp.float32), jnp.uint32)
    for r in range(ROW_SUB):
        even = bits[:, (2 * r) * 128:(2 * r + 1) * 128]
        odd = bits[:, (2 * r + 1) * 128:(2 * r + 2) * 128]
        o_ref[pl.ds(r, PACK_TM, stride=ROW_SUB), :] = (even >> 16) | (odd & jnp.uint32(0xFFFF0000))


def _expert_slab(tab_ref, row):
    return tab_ref[pl.ds(pl.multiple_of(row, ROW_SUB), ROW_SUB), :]


def _unpack(w):
    lo = lax.bitcast_convert_type(w << 16, jnp.float32)
    hi = lax.bitcast_convert_type(w & jnp.uint32(0xFFFF0000), jnp.float32)
    return lo, hi


def _gelu(x):
    return 0.5 * x * (1.0 + lax.erf(x * (2.0 ** -0.5)))


def _peer_u_kernel(idx_a_ref, idx_b_ref, x_ref, w_ref, tab_ref, out_ref):
    lane = lax.broadcasted_iota(jnp.int32, (2 * ROW_SUB, 128), 1)
    lower = lax.broadcasted_iota(jnp.int32, (2 * ROW_SUB, 128), 0) >= ROW_SUB
    pair_lane = jnp.where(lower, lane - PEER_SLOTS // 2, lane)

    def token(t):
        row = x_ref[pl.ds(t, 1), :]
        chunks = [row[:, c * 128:(c + 1) * 128] for c in range(2 * ROW_SUB)]
        xl = jnp.concatenate(chunks[0::2] * 2, axis=0)
        xh = jnp.concatenate(chunks[1::2] * 2, axis=0)
        acc = jnp.zeros((2 * ROW_SUB, 128), jnp.float32)
        base = t * (PEER_SLOTS // 2)
        for s in range(PEER_SLOTS // 2):
            w = jnp.concatenate([_expert_slab(tab_ref, idx_a_ref[base + s]),
                                 _expert_slab(tab_ref, idx_b_ref[base + s])], axis=0)
            lo, hi = _unpack(w)
            part = jnp.sum(lo * xl + hi * xh, axis=1, keepdims=True)
            acc = jnp.where(pair_lane == s, part, acc)
        out_ref[pl.ds(t, 1), :] = w_ref[pl.ds(t, 1), :] * _gelu(jnp.sum(acc, axis=0, keepdims=True))

    def trip(i, carry):
        for j in range(PEER_U_UNROLL):
            token(i * PEER_U_UNROLL + j)
        return carry

    lax.fori_loop(0, PEER_TT // PEER_U_UNROLL, trip, 0)


N_CHUNK = 2 * ROW_SUB


def _peer_v_kernel(idx_a_ref, idx_b_ref, coef_ref, expand_ref, tab_ref, out_ref, ce_hi_ref, ce_lo_ref):
    half_slots = PEER_SLOTS // 2
    width = PEER_SLOTS * N_CHUNK
    coef = coef_ref[...]
    c_hi = _trunc_bf16(coef)
    expand = expand_ref[...]
    ce_hi_ref[...] = jnp.dot(c_hi.astype(BF16), expand, preferred_element_type=F32)
    ce_lo_ref[...] = jnp.dot((coef - c_hi).astype(BF16), expand, preferred_element_type=F32)
    own_chunk = ((lax.broadcasted_iota(jnp.int32, (N_CHUNK, width), 1) & (N_CHUNK - 1))
                 == lax.broadcasted_iota(jnp.int32, (N_CHUNK, width), 0))

    def token(t):
        base = t * half_slots
        slabs = [jnp.concatenate([_expert_slab(tab_ref, idx_a_ref[base + s]),
                                  _expert_slab(tab_ref, idx_b_ref[base + s])], axis=0) for s in range(half_slots)]
        gathered = pltpu.bitcast(jnp.concatenate(slabs, axis=0), BF16)
        sel = lambda ref: jnp.where(own_chunk, ref[pl.ds(t, 1), :], 0.0).astype(BF16)
        c = jnp.concatenate([sel(ce_hi_ref), sel(ce_lo_ref)], axis=0)
        o = jnp.dot(c, gathered, preferred_element_type=F32)
        o = o[:N_CHUNK] + o[N_CHUNK:]
        out_ref[pl.ds(t, 1), :] = jnp.concatenate([o[r:r + 1, :] for r in range(N_CHUNK)], axis=1)

    def trip(i, carry):
        for j in range(PEER_V_UNROLL):
            token(i * PEER_V_UNROLL + j)
        return carry

    lax.fori_loop(0, PEER_TT // PEER_V_UNROLL, trip, 0)


def _smem_spec():
    return pl.BlockSpec((PEER_TT * PEER_SLOTS // 2,), lambda i: (i,), memory_space=pltpu.SMEM)


def _slot_halves(a):
    return a[:, :PEER_SLOTS // 2].reshape(-1), a[:, PEER_SLOTS // 2:].reshape(-1)


def _table_spec(rows):
    return pl.BlockSpec((rows, 128), lambda i: (0, 0), pipeline_mode=pl.Buffered(1))


def peer_u(idx, x, w, tab):
    t = x.shape[0]
    return pl.pallas_call(
        _peer_u_kernel,
        grid=(t // PEER_TT,),
        in_specs=[_smem_spec(), _smem_spec(), pl.BlockSpec((PEER_TT, D_MODEL), lambda i: (i, 0)),
                  pl.BlockSpec((PEER_TT, PEER_SLOTS), lambda i: (i, 0)), _table_spec(tab.shape[0])],
        out_specs=pl.BlockSpec((PEER_TT, PEER_SLOTS), lambda i: (i, 0)),
        out_shape=jax.ShapeDtypeStruct((t, PEER_SLOTS), jnp.float32),
        compiler_params=pltpu.CompilerParams(vmem_limit_bytes=VMEM_LIMIT_PEER),
        name="peer_u",
    )(*_slot_halves(idx), x, w, tab)


def peer_v(idx, coef, tab):
    t = idx.shape[0]
    width = PEER_SLOTS * N_CHUNK
    half = PEER_SLOTS // 2
    col_slot = np.arange(width) // N_CHUNK
    col_slot = (col_slot // 2) + half * (col_slot % 2)
    expand = jnp.asarray(col_slot[None, :] == np.arange(PEER_SLOTS)[:, None], dtype=BF16)
    return pl.pallas_call(
        _peer_v_kernel,
        grid=(t // PEER_TT,),
        in_specs=[_smem_spec(), _smem_spec(), pl.BlockSpec((PEER_TT, PEER_SLOTS), lambda i: (i, 0)),
                  pl.BlockSpec((PEER_SLOTS, width), lambda i: (0, 0)), _table_spec(tab.shape[0])],
        out_specs=pl.BlockSpec((PEER_TT, D_MODEL), lambda i: (i, 0)),
        out_shape=jax.ShapeDtypeStruct((t, D_MODEL), jnp.float32),
        scratch_shapes=[pltpu.VMEM((PEER_TT, width), F32), pltpu.VMEM((PEER_TT, width), F32)],
        compiler_params=pltpu.CompilerParams(vmem_limit_bytes=VMEM_LIMIT_PEER),
        name="peer_v",
    )(*_slot_halves(idx), coef, expand, tab)


PK_TT = 256
PK_A_BLOCKS = ((0, 16), (1, 8), (2, 5), (3, 4))
PK_B_BLOCKS = ((0, 4, 16), (1, 4, 8), (2, 4, 5))
NEG_HUGE = -3.0e38
PK_HEADS_PER_TRIP = 4


def _extract_top(problems, flat, n):
    vals = [[] for _ in problems]
    picked = [[] for _ in problems]
    cur = [v for v, _ in problems]
    for _ in range(n):
        for i, (_, payload) in enumerate(problems):
            m = jnp.max(cur[i], axis=0, keepdims=True)
            pos = jnp.min(jnp.where(cur[i] == m, flat, 1e9), axis=0, keepdims=True)
            hit = flat == pos
            picked[i].append(pos if payload is None else jnp.sum(jnp.where(hit, payload, 0.0), axis=0, keepdims=True))
            cur[i] = jnp.where(hit, NEG_HUGE, cur[i])
            vals[i].append(m)
    return [jnp.concatenate(v, axis=0) for v in vals], [jnp.concatenate(p, axis=0) for p in picked]


def _candidate_grid(first, second, combine):
    up8 = lambda n: -(-n // 8) * 8
    return jnp.concatenate([combine(first[a:a + 1], second[0:up8(nb)]) for a, nb in PK_A_BLOCKS]
                           + [combine(first[0:up8(a1)], second[b:b + 1]) for b, _, a1 in PK_B_BLOCKS], axis=0)


def _candidate_valid_and_flat():
    up8 = lambda n: -(-n // 8) * 8
    valid, flat = [], []
    for a, nb in PK_A_BLOCKS:
        r = lax.broadcasted_iota(jnp.int32, (up8(nb), PK_TT), 0)
        valid.append(r < nb)
        flat.append(a * PEER_TOPK + r)
    for b, a0, a1 in PK_B_BLOCKS:
        r = lax.broadcasted_iota(jnp.int32, (up8(a1), PK_TT), 0)
        valid.append((r >= a0) & (r < a1))
        flat.append(r * PEER_TOPK + b)
    valid = jnp.concatenate(valid, axis=0)
    return valid, jnp.where(valid, jnp.concatenate(flat, axis=0).astype(F32), 1e9)


def _peer_topk_kernel(q_ref, ka_ref, eid_ref, w_ref):
    key_row = lax.broadcasted_iota(jnp.int32, (N_KEYS, PK_TT), 0).astype(F32)
    valid, flat = _candidate_valid_and_flat()

    def heads(i, carry):
        scores = []
        for j in range(2 * PK_HEADS_PER_TRIP):
            q = q_ref[2 * PK_HEADS_PER_TRIP * i + j]
            q_hi = _trunc_bf16(q)
            qa = jnp.concatenate([q_hi.astype(BF16), (q - q_hi).astype(BF16), q_hi.astype(BF16)], axis=1)
            scores.append(lax.dot_general(ka_ref[2 * PK_HEADS_PER_TRIP * i + j], qa, (((1,), (1,)), ((), ())),
                                          preferred_element_type=F32))
        sub_s, sub_i = _extract_top([(s, None) for s in scores], key_row, PEER_TOPK)
        cands = []
        for j in range(PK_HEADS_PER_TRIP):
            s1, s2, i1, i2 = sub_s[2 * j], sub_s[2 * j + 1], sub_i[2 * j], sub_i[2 * j + 1]
            cands.append((jnp.where(valid, _candidate_grid(s1, s2, lambda x, y: x + y), NEG_HUGE),
                          _candidate_grid(i1, i2, lambda x, y: x * N_KEYS + y)))
        top_s, top_e = _extract_top(cands, flat, PEER_TOPK)
        for j in range(PK_HEADS_PER_TRIP):
            e = jnp.exp(top_s[j] - top_s[j][0:1])
            w_ref[PK_HEADS_PER_TRIP * i + j] = e / jnp.sum(e, axis=0, keepdims=True)
            eid_ref[PK_HEADS_PER_TRIP * i + j] = top_e[j].astype(jnp.int32) * ROW_SUB
        return carry

    lax.fori_loop(0, PEER_HEADS // PK_HEADS_PER_TRIP, heads, 0)


def peer_topk(qt, subkeys):
    hp, t, d = qt.shape
    k_hi, k_lo = _split2(subkeys.reshape(hp, N_KEYS, d))
    ka = jnp.concatenate([k_hi, k_hi, k_lo], axis=2)
    eid, w = pl.pallas_call(
        _peer_topk_kernel,
        grid=(t // PK_TT,),
        in_specs=[pl.BlockSpec((hp, PK_TT, d), lambda i: (0, i, 0)),
                  pl.BlockSpec((hp, N_KEYS, 3 * d), lambda i: (0, 0, 0))],
        out_specs=[pl.BlockSpec((PEER_HEADS, PEER_TOPK, PK_TT), lambda i: (0, 0, i)),
                   pl.BlockSpec((PEER_HEADS, PEER_TOPK, PK_TT), lambda i: (0, 0, i))],
        out_shape=[jax.ShapeDtypeStruct((PEER_HEADS, PEER_TOPK, t), jnp.int32),
                   jax.ShapeDtypeStruct((PEER_HEADS, PEER_TOPK, t), F32)],
        name="peer_topk",
    )(qt, ka)
    to_slots = lambda a: a.transpose(2, 0, 1).reshape(t, PEER_SLOTS)
    return to_slots(eid), to_slots(w)


def split_columns(t, sizes):
    offs = np.cumsum((0,) + tuple(sizes))
    return [t[..., int(a):int(b)] for a, b in zip(offs[:-1], offs[1:])]


def rope_tables(pos):
    inv = jnp.power(ROPE_THETA, -jnp.arange(0, ROPE_DIM, 2, dtype=jnp.float32) / ROPE_DIM)
    ang = pos[:, None] * inv[None, :]
    return jnp.cos(ang), jnp.sin(ang)


def apply_partial_rope(x, cos, sin):
    half = ROPE_DIM // 2
    xr = x[..., :ROPE_DIM].astype(jnp.float32)
    x1, x2 = xr[..., :half], xr[..., half:]
    c, s = cos[:, None, :], sin[:, None, :]
    rot = jnp.concatenate([x1 * c - x2 * s, x1 * s + x2 * c], axis=-1)
    return jnp.concatenate([rot.astype(x.dtype), x[..., ROPE_DIM:]], axis=-1)


LOG2_E = 1.4426950408889634
LOGIT_SCALE = HEAD_DIM ** -0.5 * LOG2_E


def _logits(ks, qts):
    return [jnp.dot(k, qt, preferred_element_type=F32) for k, qt in zip(ks, qts)]


def _flash_update(ss, vts, mask, carries):
    stats, ps = [], []
    for s, (m, l, _) in zip(ss, carries):
        s_vis = s if mask is None else jnp.where(mask, s, NEG_INF)
        m_new = jnp.maximum(m, jnp.max(s_vis, axis=0, keepdims=True))
        alpha = jnp.exp2(m - m_new)
        p = jnp.exp2(s - m_new)
        if mask is not None:
            p = jnp.where(mask, p, 0.0)
        stats.append((m_new, alpha, alpha * l + jnp.sum(p, axis=0, keepdims=True)))
        ps.append(p.astype(BF16))
    return tuple((m_new, l, alpha * acc + jnp.dot(vt, p, preferred_element_type=F32))
                 for (m_new, alpha, l), p, vt, (_, _, acc) in zip(stats, ps, vts, carries))


def _causal_flash(key_tile, value_tile, qts, n_full, last_mask, n):
    def body(kt, carries):
        return _flash_update(_logits(key_tile(kt), qts), value_tile(kt), None, carries)

    carries = lax.fori_loop(0, n_full, body, tuple(_flash_init(n) for _ in qts))
    return _flash_update(_logits(key_tile(n_full), qts), value_tile(n_full), last_mask, carries)


def _flash_init(n):
    return (jnp.full((1, n), NEG_INF, F32), jnp.zeros((1, n), F32), jnp.zeros((HEAD_DIM, n), F32))


FOX_TQ = 256
FOX_KV = 512
FOX_DK = 256


FOX_HP = 4


def _fox_kernel(qt_ref, k_ref, vt_ref, o_ref):
    qi = pl.program_id(1)
    q0 = qi * FOX_TQ
    qts = [qt_ref[h] for h in range(FOX_HP)]
    qpos = q0 + lax.broadcasted_iota(jnp.int32, (1, FOX_TQ), 1)
    krow = lax.broadcasted_iota(jnp.int32, (FOX_KV, FOX_TQ), 0)

    n_full = q0 // FOX_KV
    causal = (n_full * FOX_KV + krow) <= qpos
    carry = _causal_flash(lambda kt: [k_ref[h, kt] for h in range(FOX_HP)],
                          lambda kt: [vt_ref[h, kt] for h in range(FOX_HP)], qts, n_full, causal, FOX_TQ)
    for h in range(FOX_HP):
        _, l, acc = carry[h]
        o_ref[h] = acc / l


def _trunc_bf16(x):
    bits = lax.bitcast_convert_type(x, jnp.uint32) & jnp.uint32(0xFFFF0000)
    return lax.bitcast_convert_type(bits, F32)


def _split3(c):
    c1 = _trunc_bf16(c)
    r = c - c1
    c2 = _trunc_bf16(r)
    return c1.astype(BF16), c2.astype(BF16), (r - c2).astype(BF16)


def _split2(x):
    hi = _trunc_bf16(x)
    return hi.astype(BF16), (x - hi).astype(BF16)


def _aug_q(q):
    hi, lo = _split2(q)
    return jnp.concatenate([hi, hi, lo], axis=-1)


def _aug_k(k):
    hi, lo = _split2(k)
    return jnp.concatenate([hi, lo, hi], axis=-1)


def fox_attention(q, k, v, log_f):
    B_, S_, H, Dh = q.shape
    KV_TILE = FOX_KV
    n_qt, n_kt = S_ // FOX_TQ, S_ // KV_TILE
    c = jnp.cumsum(log_f, axis=1)
    c1, c2, c3 = _split3(c * LOG2_E)
    j = jnp.arange(FOX_DK - 3 * Dh)
    pick = lambda t: t[..., None].astype(F32)
    terms = lambda first: jnp.where(j == first, pick(c1), jnp.where(j == first + 1, pick(c2), pick(c3)))
    q_extra = jnp.where(j < 3, 1.0, jnp.where(j < 6, terms(3), 0.0)).astype(BF16)
    k_extra = jnp.where(j < 3, -terms(0), jnp.where(j < 6, 1.0, 0.0)).astype(BF16)
    qa = jnp.concatenate([_aug_q(q * LOGIT_SCALE), q_extra], axis=-1)
    ka = jnp.concatenate([_aug_k(k), k_extra], axis=-1)
    HP, HG = FOX_HP, H // FOX_HP
    qt = qa.reshape(B_, n_qt, FOX_TQ, HG, HP, FOX_DK).transpose(0, 3, 1, 4, 5, 2).reshape(B_ * HG * n_qt, HP, FOX_DK, FOX_TQ)
    kk = ka.reshape(B_, n_kt, KV_TILE, HG, HP, FOX_DK).transpose(0, 3, 4, 1, 2, 5).reshape(B_ * HG, HP, n_kt, KV_TILE, FOX_DK)
    vt = v.astype(BF16).reshape(B_, n_kt, KV_TILE, HG, HP, Dh).transpose(0, 3, 4, 1, 5, 2).reshape(B_ * HG, HP, n_kt, Dh, KV_TILE)
    out = pl.pallas_call(
        _fox_kernel,
        grid=(B_ * HG, n_qt),
        in_specs=[pl.BlockSpec((None, HP, FOX_DK, FOX_TQ), lambda b, i: (b * n_qt + i, 0, 0, 0)),
                  pl.BlockSpec((None, HP, n_kt, KV_TILE, FOX_DK), lambda b, i: (b, 0, 0, 0, 0)),
                  pl.BlockSpec((None, HP, n_kt, Dh, KV_TILE), lambda b, i: (b, 0, 0, 0, 0))],
        out_specs=pl.BlockSpec((None, HP, Dh, FOX_TQ), lambda b, i: (b * n_qt + i, 0, 0, 0)),
        out_shape=jax.ShapeDtypeStruct((B_ * HG * n_qt, HP, Dh, FOX_TQ), F32),
        name="fox_attention",
    )(qt, kk, vt)
    return out.reshape(B_, HG, n_qt, HP, Dh, FOX_TQ)


NSA_TQ = 128
NSA_N = NSA_HPG * NSA_TQ
N_CMP_PAD = 256
N_SLC = 64
NSA_DK = 3 * HEAD_DIM
NSA_KV = 512
WIN_TILES = WINDOW // NSA_TQ + 1
WIN_KEYS = WIN_TILES * NSA_TQ


def _nsa_kernel(qt_ref, kc_ref, vct_ref, ks_ref, vst_ref, kw_ref, vwt_ref, gate_ref, ovl_ref, o_ref):
    qi = pl.program_id(1)
    q0 = qi * NSA_TQ
    qt = qt_ref[...]
    lane = lax.broadcasted_iota(jnp.int32, (1, NSA_N), 1)
    qpos = q0 + (lane & (NSA_TQ - 1))

    hn = NSA_N // 2
    qts = (qt[:, :hn], qt[:, hn:])
    qpos_h = qpos[:, :hn]

    w0 = jnp.maximum(qi - WINDOW // NSA_TQ, 0)
    k_w = jnp.concatenate([kw_ref[w0 + i] for i in range(WIN_TILES)], axis=0)
    vt_w = jnp.concatenate([vwt_ref[w0 + i] for i in range(WIN_TILES)], axis=1)
    s = jnp.dot(kc_ref[...], qt, preferred_element_type=F32)
    s_w = [jnp.dot(k_w, qts[h], preferred_element_type=F32) for h in range(2)]

    cmp_end = lax.broadcasted_iota(jnp.int32, (N_CMP_PAD, NSA_N), 0) * CMP_STRIDE + (CMP_LEN - 1)
    mc = cmp_end <= qpos
    m = jnp.max(jnp.where(mc, s, NEG_INF), axis=0, keepdims=True)
    p = jnp.where(mc, jnp.exp2(s - m), 0.0)
    l = jnp.sum(p, axis=0, keepdims=True)
    pc = p * jnp.where(l > 0.0, 1.0 / l, 0.0)
    o_cmp = jnp.dot(vct_ref[...], pc.astype(BF16), preferred_element_type=F32)
    pcs = (pc[:, 0:NSA_TQ] + pc[:, NSA_TQ:2 * NSA_TQ]) + (pc[:, 2 * NSA_TQ:3 * NSA_TQ] + pc[:, 3 * NSA_TQ:4 * NSA_TQ])
    pcs_hi = pcs.astype(BF16)
    pcs_lo = (pcs - pcs_hi.astype(F32)).astype(BF16)
    ovl = ovl_ref[...]
    imp = jnp.dot(ovl, pcs_hi, preferred_element_type=F32) + jnp.dot(ovl, pcs_lo, preferred_element_type=F32)

    rel = qpos_h - (w0 * NSA_TQ + lax.broadcasted_iota(jnp.int32, (WIN_KEYS, hn), 0))
    in_win = (rel >= 0) & (rel < WINDOW)
    outs_w = []
    for h in range(2):
        m = jnp.max(jnp.where(in_win, s_w[h], NEG_INF), axis=0, keepdims=True)
        p = jnp.where(in_win, jnp.exp2(s_w[h] - m), 0.0)
        l = jnp.sum(p, axis=0, keepdims=True)
        outs_w.append(jnp.dot(vt_w, p.astype(BF16), preferred_element_type=F32) / l)
    acc_w = jnp.concatenate(outs_w, axis=1)

    blk = lax.broadcasted_iota(jnp.int32, (N_SLC, NSA_TQ), 0)
    q_blk = (q0 + lax.broadcasted_iota(jnp.int32, (N_SLC, NSA_TQ), 1)) >> 6
    forced = (blk == 0) | (blk == q_blk) | (blk == q_blk - 1)
    score = jnp.where(forced, FORCED_SCORE, jnp.where(blk <= q_blk, imp, -1.0))
    rank = jnp.zeros((N_SLC, NSA_TQ), jnp.int32)
    for i in range(N_SLC):
        row = score[i:i + 1, :]
        beats = (row > score) | ((row == score) & (blk > i))
        rank = rank + beats.astype(jnp.int32)
    bias = jnp.where(rank < SLC_TOP, 0.0, NEG_INF).astype(BF16)
    bias = jnp.concatenate([bias] * (NSA_HPG // 2), axis=1)
    qts_s = [jnp.concatenate([qh, bias], axis=0) for qh in qts]
    krow = lax.broadcasted_iota(jnp.int32, (NSA_KV, hn), 0)

    n_full = q0 // NSA_KV
    causal = (n_full * NSA_KV + krow) <= qpos_h
    slc = _causal_flash(lambda kt: [ks_ref[kt]] * 2, lambda kt: [vst_ref[kt]] * 2, qts_s, n_full, causal, hn)
    acc_s = jnp.concatenate([slc[0][2] / slc[0][1], slc[1][2] / slc[1][1]], axis=1)

    g = jax.nn.sigmoid(gate_ref[...])
    o_ref[...] = g[0:1] * o_cmp + g[1:2] * acc_s + g[2:3] * acc_w


def nsa_attention(q, k_cmp, v_cmp, k_slc, v_slc, k_win, v_win, gate_logit):
    B_, S_, H, Dh = q.shape
    G = NSA_GROUPS
    n_qt = S_ // NSA_TQ
    n_cmp = k_cmp.shape[1]
    cs = np.arange(N_CMP_PAD)[None, :] * CMP_STRIDE
    ss = np.arange(N_SLC)[:, None] * SLC_LEN
    ov = np.clip(np.minimum(cs + CMP_LEN, ss + SLC_LEN) - np.maximum(cs, ss), 0, None) / CMP_LEN
    ov[:, n_cmp:] = 0.0
    ovl = jnp.asarray(ov, dtype=BF16)
    qs = _aug_q(q * LOGIT_SCALE).reshape(B_, n_qt, NSA_TQ, G, NSA_HPG, NSA_DK)
    qt = qs.transpose(0, 3, 1, 5, 4, 2).reshape(B_ * G * n_qt, NSA_DK, NSA_N)
    gt = gate_logit.astype(F32).reshape(B_, n_qt, NSA_TQ, G, NSA_HPG, N_NSA_BRANCH)
    gt = gt.transpose(0, 3, 1, 5, 4, 2).reshape(B_ * G * n_qt, N_NSA_BRANCH, NSA_N)
    tiles = lambda t, kv: t.reshape(B_, S_ // kv, kv, G, -1).transpose(0, 3, 1, 2, 4).reshape(B_ * G, S_ // kv, kv, t.shape[-1])
    keys = lambda t, kv: tiles(_aug_k(t), kv)
    block_onehot = (jnp.arange(S_)[:, None] // SLC_LEN == jnp.arange(N_SLC)[None, :]).astype(BF16)
    k_slc_aug = jnp.concatenate([_aug_k(k_slc), jnp.broadcast_to(block_onehot[None, :, None, :], (B_, S_, G, N_SLC))], axis=-1)
    vals = lambda t, kv: t.astype(BF16).reshape(B_, S_ // kv, kv, G, Dh).transpose(0, 3, 1, 4, 2).reshape(B_ * G, S_ // kv, Dh, kv)
    padc = ((0, 0), (0, N_CMP_PAD - n_cmp), (0, 0), (0, 0))
    kc = _aug_k(jnp.pad(k_cmp, padc)).transpose(0, 2, 1, 3).reshape(B_ * G, N_CMP_PAD, NSA_DK)
    vct = jnp.pad(v_cmp, padc).astype(BF16).transpose(0, 2, 3, 1).reshape(B_ * G, Dh, N_CMP_PAD)
    per_bg = lambda *blk: pl.BlockSpec((None,) + blk, lambda b, i: (b,) + (0,) * len(blk))
    per_tile = lambda *blk: pl.BlockSpec((None,) + blk, lambda b, i: (b * n_qt + i,) + (0,) * len(blk))
    out = pl.pallas_call(
        _nsa_kernel,
        grid=(B_ * G, n_qt),
        in_specs=[per_tile(NSA_DK, NSA_N),
                  per_bg(N_CMP_PAD, NSA_DK), per_bg(Dh, N_CMP_PAD),
                  per_bg(S_ // NSA_KV, NSA_KV, NSA_DK + N_SLC), per_bg(S_ // NSA_KV, Dh, NSA_KV),
                  per_bg(n_qt, NSA_TQ, NSA_DK), per_bg(n_qt, Dh, NSA_TQ),
                  per_tile(N_NSA_BRANCH, NSA_N),
                  pl.BlockSpec((N_SLC, N_CMP_PAD), lambda b, i: (0, 0))],
        out_specs=per_tile(Dh, NSA_N),
        out_shape=jax.ShapeDtypeStruct((B_ * G * n_qt, Dh, NSA_N), F32),
        name="nsa_attention",
    )(qt, kc, vct, tiles(k_slc_aug, NSA_KV), vals(v_slc, NSA_KV), keys(k_win, NSA_TQ), vals(v_win, NSA_TQ), gt, ovl)
    return out.reshape(B_, G, n_qt, Dh, NSA_N)


DENSE_TM = 256
SEC_MERGE = N_BRANCH * D_MODEL
SEC_SMALL = 256
IN_SECTIONS = (BRANCH_WIDTH,) * 4 + (KV_WIDTH,) * 6 + (SEC_MERGE, SEC_SMALL)


def _rms(x, g):
    return x * lax.rsqrt(jnp.mean(x * x, axis=-1, keepdims=True) + RMS_EPS) * g


def _row_spec(width):
    return pl.BlockSpec((DENSE_TM, width), lambda i: (i, 0))


def _whole_spec(shape):
    return pl.BlockSpec(shape, lambda i: (0,) * len(shape))


def _in_proj_kernel(x_ref, g_ref, w_ref, *o_refs):
    xn = _rms(x_ref[...], g_ref[...]).astype(BF16)
    off = 0
    for o_ref, width in zip(o_refs, IN_SECTIONS):
        o_ref[...] = jnp.dot(xn, w_ref[:, off:off + width], preferred_element_type=F32)
        off += width


def in_projection(x, g, w_in):
    t, d = x.shape
    fq, fk, fv, fl, nq, kc, vc, ksl, vsl, kwn, vwn, gl, ml = split_columns(w_in, IN_SPLITS)
    pad = lambda w: jnp.pad(w, ((0, 0), (0, 128 - w.shape[1])))
    w = jnp.concatenate([fq, fk, fv, nq, kc, vc, ksl, vsl, kwn, vwn, ml, pad(fl), pad(gl)], axis=1).astype(BF16)
    return pl.pallas_call(
        _in_proj_kernel,
        grid=(t // DENSE_TM,),
        in_specs=[_row_spec(d), _whole_spec((1, d)), _whole_spec(w.shape)],
        out_specs=[_row_spec(s) for s in IN_SECTIONS],
        out_shape=[jax.ShapeDtypeStruct((t, s), F32) for s in IN_SECTIONS],
        name="in_projection",
    )(x, g.reshape(1, d), w)


def _merge_kernel(x_ref, yf_ref, yn_ref, ml_ref, wb_ref, wo_ref, o_ref):
    yf_t = yf_ref[...].reshape(BRANCH_WIDTH, DENSE_TM)
    yn_t = jnp.concatenate(
        [jnp.concatenate([yn_ref[g, j, :, h * NSA_TQ:(h + 1) * NSA_TQ]
                          for g in range(NSA_GROUPS) for h in range(NSA_HPG)], axis=0)
         for j in range(DENSE_TM // NSA_TQ)], axis=1)
    g = jax.nn.sigmoid(ml_ref[...])
    up_f = jnp.dot(yf_t.T.astype(BF16), wb_ref[0], preferred_element_type=F32)
    up_n = jnp.dot(yn_t.T.astype(BF16), wb_ref[1], preferred_element_type=F32)
    merged = g[:, :D_MODEL] * up_f + g[:, D_MODEL:] * up_n
    o_ref[...] = x_ref[...] + jnp.dot(merged.astype(BF16), wo_ref[...], preferred_element_type=F32)


def merge_out(x, y_fox, y_nsa, merge_logit, w_branch, w_out):
    t, d = x.shape
    assert FOX_TQ == DENSE_TM and DENSE_TM % NSA_TQ == 0
    tiles_per_seq = y_fox.shape[2]
    nsa_per_tile = DENSE_TM // NSA_TQ
    yf_spec = pl.BlockSpec((None,) + y_fox.shape[1:2] + (None,) + y_fox.shape[3:],
                           lambda i: (i // tiles_per_seq, 0, i % tiles_per_seq, 0, 0, 0))
    yn_spec = pl.BlockSpec((None, NSA_GROUPS, nsa_per_tile, HEAD_DIM, NSA_N),
                           lambda i: (i // tiles_per_seq, 0, i % tiles_per_seq, 0, 0))
    return pl.pallas_call(
        _merge_kernel,
        grid=(t // DENSE_TM,),
        in_specs=[_row_spec(d), yf_spec, yn_spec, _row_spec(SEC_MERGE),
                  _whole_spec(w_branch.shape), _whole_spec(w_out.shape)],
        out_specs=_row_spec(d),
        out_shape=jax.ShapeDtypeStruct((t, d), F32),
        name="merge_out",
    )(x, y_fox, y_nsa, merge_logit, w_branch.astype(BF16), w_out.astype(BF16))


def _peer_q_kernel(x_ref, g_ref, w_ref, hn_ref, q_ref):
    hn = _rms(x_ref[...], g_ref[...])
    hn_ref[...] = hn
    q = jnp.dot(hn.astype(BF16), w_ref[...], preferred_element_type=F32)
    d = PEER_QDIM // 2
    for j in range(2 * PEER_HEADS):
        q_ref[j] = q[:, j * d:(j + 1) * d]


def peer_query(x, g, wq):
    t, d = x.shape
    hp, dq = 2 * PEER_HEADS, PEER_QDIM // 2
    return pl.pallas_call(
        _peer_q_kernel,
        grid=(t // DENSE_TM,),
        in_specs=[_row_spec(d), _whole_spec((1, d)), _whole_spec(wq.shape)],
        out_specs=[_row_spec(d), pl.BlockSpec((hp, DENSE_TM, dq), lambda i: (0, i, 0))],
        out_shape=[jax.ShapeDtypeStruct((t, d), F32), jax.ShapeDtypeStruct((hp, t, dq), F32)],
        name="peer_query",
    )(x, g.reshape(1, d), wq.astype(BF16))


def _final_kernel(h_ref, p_ref, g_ref, o_ref):
    o_ref[...] = _rms(h_ref[...] + p_ref[...], g_ref[...])


def final_norm(h, p, g):
    t, d = h.shape
    return pl.pallas_call(
        _final_kernel,
        grid=(t // DENSE_TM,),
        in_specs=[_row_spec(d), _row_spec(d), _whole_spec((1, d))],
        out_specs=_row_spec(d),
        out_shape=jax.ShapeDtypeStruct((t, d), F32),
        name="final_norm",
    )(h, p, g.reshape(1, d))


N_CMP_ROWS = 256


def _compress_kernel(r_ref, pos_ref, w1_ref, w2_ref, o_ref):
    r = r_ref[...]
    nxt = jnp.concatenate([r[1:], jnp.zeros((1, r.shape[1]), F32)], axis=0)
    half = r.shape[1]
    hid = (jnp.dot((r + pos_ref[0:1, :]).astype(BF16), w1_ref[:half], preferred_element_type=F32)
           + jnp.dot((nxt + pos_ref[1:2, :]).astype(BF16), w1_ref[half:], preferred_element_type=F32))
    o_ref[...] = jnp.dot(_gelu(hid).astype(BF16), w2_ref[...], preferred_element_type=F32)


def compress_kv_pallas(kv, cmp_pos, w1, w2):
    _, B_, S_, G, Dh = kv.shape
    n_cmp = (S_ - CMP_LEN) // CMP_STRIDE + 1
    width = CMP_STRIDE * Dh
    rows = kv.transpose(0, 1, 3, 2, 4).reshape(2, B_ * G, S_ // CMP_STRIDE, width)
    out = pl.pallas_call(
        _compress_kernel,
        grid=(2, B_ * G),
        in_specs=[pl.BlockSpec((None, None, N_CMP_ROWS, width), lambda j, b: (j, b, 0, 0)),
                  pl.BlockSpec((None, 2, width), lambda j, b: (j, 0, 0)),
                  pl.BlockSpec((None, CMP_LEN * Dh, CMP_HIDDEN), lambda j, b: (j, 0, 0)),
                  pl.BlockSpec((None, CMP_HIDDEN, Dh), lambda j, b: (j, 0, 0))],
        out_specs=pl.BlockSpec((None, None, N_CMP_ROWS, Dh), lambda j, b: (j, b, 0, 0)),
        out_shape=jax.ShapeDtypeStruct((2, B_ * G, N_CMP_ROWS, Dh), F32),
        name="compress_kv",
    )(rows, cmp_pos.reshape(2, 2, width), w1.astype(BF16), w2.astype(BF16))
    return out[:, :, :n_cmp].reshape(2, B_, G, n_cmp, Dh).transpose(0, 1, 3, 2, 4)


def hybrid_mixer(x, norm_g, w_in, fox_f_bias, cmp_pos, cmp_w1, cmp_w2, w_branch, w_out):
    B_, S_, D = x.shape
    xf = x.reshape(B_ * S_, D)
    secs = in_projection(xf, norm_g, w_in)
    heads = lambda t, n: t.reshape(B_, S_, n, HEAD_DIM)
    fq, fk, fv, nq = secs[:4]
    kc, vc, ksl, vsl, kwn, vwn = (heads(t, NSA_GROUPS) for t in secs[4:10])
    merge_logit, small = secs[10:]
    f_logit = small[:, :FOX_HEADS].reshape(B_, S_, FOX_HEADS)
    gate_logit = small[:, 128:128 + NSA_HEADS * N_NSA_BRANCH].reshape(B_, S_, NSA_HEADS, N_NSA_BRANCH)
    log_f = jax.nn.log_sigmoid(f_logit + fox_f_bias.astype(F32))
    y_fox = fox_attention(heads(fq, FOX_HEADS), heads(fk, FOX_HEADS), heads(fv, FOX_HEADS), log_f)
    cos, sin = rope_tables(jnp.arange(S_, dtype=F32))
    q_nsa = apply_partial_rope(heads(nq, NSA_HEADS), cos, sin)
    k_slc = apply_partial_rope(ksl, cos, sin)
    k_win = apply_partial_rope(kwn, cos, sin)
    cmp = compress_kv_pallas(jnp.stack([kc, vc]), cmp_pos, cmp_w1, cmp_w2)
    n_cmp = cmp.shape[2]
    cmp_end = jnp.arange(n_cmp, dtype=F32) * CMP_STRIDE + (CMP_LEN - 1)
    k_cmp = apply_partial_rope(cmp[0], *rope_tables(cmp_end))
    y_nsa = nsa_attention(q_nsa, k_cmp, cmp[1], k_slc, vsl, k_win, vwn, gate_logit)
    return merge_out(xf, y_fox, y_nsa, merge_logit, w_branch, w_out)


def peer_ffn(h, norm_g, wq, subkeys, u, v):
    hn, q = peer_query(h, norm_g, wq)
    idx2, w = peer_topk(q, subkeys)
    coef = peer_u(idx2, hn, w, pack_table(u))
    return peer_v(idx2, coef, pack_table(v))


def kernel(x, norm_mix, w_in, fox_f_bias, nsa_cmp_pos, nsa_cmp_w1, nsa_cmp_w2, w_branch, w_out,
           norm_ffn, peer_wq, peer_subkeys, peer_u, peer_v, norm_final):
    B_, S_, D = x.shape
    assert norm_mix.shape[0] == 1, "single-layer trunk"
    h = hybrid_mixer(x, norm_mix[0], w_in[0], fox_f_bias[0], nsa_cmp_pos[0], nsa_cmp_w1[0], nsa_cmp_w2[0],
                     w_branch[0], w_out[0])
    p = peer_ffn(h, norm_ffn[0], peer_wq[0], peer_subkeys[0], peer_u[0], peer_v[0])
    return final_norm(h, p, norm_final).reshape(B_, S_, D)
```

```python
import jax
import jax.numpy as jnp
import numpy as np
from jax import lax
from jax.experimental import pallas as pl
from jax.experimental.pallas import tpu as pltpu

D_MODEL = 1024
HEAD_DIM = 64
FOX_HEADS = 8
NSA_HEADS = 8
NSA_GROUPS = 2
NSA_HPG = NSA_HEADS // NSA_GROUPS
BRANCH_WIDTH = 512
N_BRANCH = 2
N_NSA_BRANCH = 3
ROPE_DIM = HEAD_DIM // 4
ROPE_THETA = 500000.0
CMP_LEN = 32
CMP_STRIDE = 16
CMP_HIDDEN = 2 * HEAD_DIM
SLC_LEN = 64
SLC_TOP = 16
WINDOW = 512
FORCED_SCORE = 1e9
NEG_INF = -1e30
PEER_HEADS = 8
N_KEYS = 128
PEER_QDIM = 256
PEER_TOPK = 16
RMS_EPS = 1e-6
KV_WIDTH = NSA_GROUPS * HEAD_DIM
IN_SPLITS = (BRANCH_WIDTH, BRANCH_WIDTH, BRANCH_WIDTH, FOX_HEADS, BRANCH_WIDTH,
             KV_WIDTH, KV_WIDTH, KV_WIDTH, KV_WIDTH, KV_WIDTH, KV_WIDTH,
             NSA_HEADS * N_NSA_BRANCH, N_BRANCH * D_MODEL)


BF16 = jnp.bfloat16
F32 = jnp.float32

PEER_SLOTS = PEER_HEADS * PEER_TOPK
PEER_TT = 128
PEER_CHUNK_SLOTS = 32
PEER_U_UNROLL = 8
PEER_V_UNROLL = 8
ROW_SUB = D_MODEL // 2 // 128
VMEM_LIMIT_PEER = N_KEYS * N_KEYS * D_MODEL * 2 + 16 * 1024 * 1024


def pack_table(tab):
    n, d = tab.shape
    return pl.pallas_call(
        _pack_kernel,
        grid=(n // PACK_TM,),
        in_specs=[pl.BlockSpec((PACK_TM, d), lambda i: (i, 0))],
        out_specs=pl.BlockSpec((PACK_TM * ROW_SUB, 128), lambda i: (i, 0)),
        out_shape=jax.ShapeDtypeStruct((n * ROW_SUB, 128), jnp.uint32),
        name="pack_table",
    )(tab)


PACK_TM = 256


def _pack_kernel(x_ref, o_ref):
    bits = lax.bitcast_convert_type(x_ref[...].astype(jnp.bfloat16).astype(jnp.float32), jnp.uint32)
    for r in range(ROW_SUB):
        even = bits[:, (2 * r) * 128:(2 * r + 1) * 128]
        odd = bits[:, (2 * r + 1) * 128:(2 * r + 2) * 128]
        o_ref[pl.ds(r, PACK_TM, stride=ROW_SUB), :] = (even >> 16) | (odd & jnp.uint32(0xFFFF0000))


def _expert_slab(tab_ref, row):
    return tab_ref[pl.ds(pl.multiple_of(row, ROW_SUB), ROW_SUB), :]


def _unpack(w):
    lo = lax.bitcast_convert_type(w << 16, jnp.float32)
    hi = lax.bitcast_convert_type(w & jnp.uint32(0xFFFF0000), jnp.float32)
    return lo, hi


def _gelu(x):
    return 0.5 * x * (1.0 + lax.erf(x * (2.0 ** -0.5)))


def _peer_u_kernel(idx_a_ref, idx_b_ref, x_ref, w_ref, tab_ref, out_ref):
    lane = lax.broadcasted_iota(jnp.int32, (2 * ROW_SUB, 128), 1)
    lower = lax.broadcasted_iota(jnp.int32, (2 * ROW_SUB, 128), 0) >= ROW_SUB
    pair_lane = jnp.where(lower, lane - PEER_SLOTS // 2, lane)

    def token(t):
        row = x_ref[pl.ds(t, 1), :]
        chunks = [row[:, c * 128:(c + 1) * 128] for c in range(2 * ROW_SUB)]
        xl = jnp.concatenate(chunks[0::2] * 2, axis=0)
        xh = jnp.concatenate(chunks[1::2] * 2, axis=0)
        acc = jnp.zeros((2 * ROW_SUB, 128), jnp.float32)
        base = t * (PEER_SLOTS // 2)
        for s in range(PEER_SLOTS // 2):
            w = jnp.concatenate([_expert_slab(tab_ref, idx_a_ref[base + s]),
                                 _expert_slab(tab_ref, idx_b_ref[base + s])], axis=0)
            lo, hi = _unpack(w)
            part = jnp.sum(lo * xl + hi * xh, axis=1, keepdims=True)
            acc = jnp.where(pair_lane == s, part, acc)
        out_ref[pl.ds(t, 1), :] = w_ref[pl.ds(t, 1), :] * _gelu(jnp.sum(acc, axis=0, keepdims=True))

    def trip(i, carry):
        for j in range(PEER_U_UNROLL):
            token(i * PEER_U_UNROLL + j)
        return carry

    lax.fori_loop(0, PEER_TT // PEER_U_UNROLL, trip, 0)


N_CHUNK = 2 * ROW_SUB


def _peer_v_kernel(idx_a_ref, idx_b_ref, coef_ref, expand_ref, h_ref, g_ref, tab_ref, out_ref,
                   ce_hi_ref, ce_lo_ref):
    half_slots = PEER_SLOTS // 2
    width = PEER_SLOTS * N_CHUNK
    coef = coef_ref[...]
    c_hi = _trunc_bf16(coef)
    expand = expand_ref[...]
    ce_hi_ref[...] = jnp.dot(c_hi.astype(BF16), expand, preferred_element_type=F32)
    ce_lo_ref[...] = jnp.dot((coef - c_hi).astype(BF16), expand, preferred_element_type=F32)
    own_chunk = ((lax.broadcasted_iota(jnp.int32, (N_CHUNK, width), 1) & (N_CHUNK - 1))
                 == lax.broadcasted_iota(jnp.int32, (N_CHUNK, width), 0))

    def token(t):
        base = t * half_slots
        slabs = [jnp.concatenate([_expert_slab(tab_ref, idx_a_ref[base + s]),
                                  _expert_slab(tab_ref, idx_b_ref[base + s])], axis=0) for s in range(half_slots)]
        gathered = pltpu.bitcast(jnp.concatenate(slabs, axis=0), BF16)
        sel = lambda ref: jnp.where(own_chunk, ref[pl.ds(t, 1), :], 0.0).astype(BF16)
        c = jnp.concatenate([sel(ce_hi_ref), sel(ce_lo_ref)], axis=0)
        o = jnp.dot(c, gathered, preferred_element_type=F32)
        o = o[:N_CHUNK] + o[N_CHUNK:]
        out_ref[pl.ds(t, 1), :] = jnp.concatenate([o[r:r + 1, :] for r in range(N_CHUNK)], axis=1)

    def trip(i, carry):
        for j in range(PEER_V_UNROLL):
            token(i * PEER_V_UNROLL + j)
        return carry

    lax.fori_loop(0, PEER_TT // PEER_V_UNROLL, trip, 0)
    out_ref[...] = _rms(h_ref[...] + out_ref[...], g_ref[...])


def _smem_spec():
    return pl.BlockSpec((PEER_TT * PEER_SLOTS // 2,), lambda i: (i,), memory_space=pltpu.SMEM)


def _slot_halves(a):
    return a[:, :PEER_SLOTS // 2].reshape(-1), a[:, PEER_SLOTS // 2:].reshape(-1)


def _table_spec(rows):
    return pl.BlockSpec((rows, 128), lambda i: (0, 0), pipeline_mode=pl.Buffered(1))


def peer_u(idx, x, w, tab):
    t = x.shape[0]
    return pl.pallas_call(
        _peer_u_kernel,
        grid=(t // PEER_TT,),
        in_specs=[_smem_spec(), _smem_spec(), pl.BlockSpec((PEER_TT, D_MODEL), lambda i: (i, 0)),
                  pl.BlockSpec((PEER_TT, PEER_SLOTS), lambda i: (i, 0)), _table_spec(tab.shape[0])],
        out_specs=pl.BlockSpec((PEER_TT, PEER_SLOTS), lambda i: (i, 0)),
        out_shape=jax.ShapeDtypeStruct((t, PEER_SLOTS), jnp.float32),
        compiler_params=pltpu.CompilerParams(vmem_limit_bytes=VMEM_LIMIT_PEER),
        name="peer_u",
    )(*_slot_halves(idx), x, w, tab)


def peer_v(idx, coef, tab, h, g):
    t = idx.shape[0]
    width = PEER_SLOTS * N_CHUNK
    half = PEER_SLOTS // 2
    col_slot = np.arange(width) // N_CHUNK
    col_slot = (col_slot // 2) + half * (col_slot % 2)
    expand = jnp.asarray(col_slot[None, :] == np.arange(PEER_SLOTS)[:, None], dtype=BF16)
    return pl.pallas_call(
        _peer_v_kernel,
        grid=(t // PEER_TT,),
        in_specs=[_smem_spec(), _smem_spec(), pl.BlockSpec((PEER_TT, PEER_SLOTS), lambda i: (i, 0)),
                  pl.BlockSpec((PEER_SLOTS, width), lambda i: (0, 0)),
                  pl.BlockSpec((PEER_TT, D_MODEL), lambda i: (i, 0)), pl.BlockSpec((1, D_MODEL), lambda i: (0, 0)),
                  _table_spec(tab.shape[0])],
        out_specs=pl.BlockSpec((PEER_TT, D_MODEL), lambda i: (i, 0)),
        out_shape=jax.ShapeDtypeStruct((t, D_MODEL), jnp.float32),
        scratch_shapes=[pltpu.VMEM((PEER_TT, width), F32), pltpu.VMEM((PEER_TT, width), F32)],
        compiler_params=pltpu.CompilerParams(vmem_limit_bytes=VMEM_LIMIT_PEER),
        name="peer_v",
    )(*_slot_halves(idx), coef, expand, h, g.reshape(1, D_MODEL), tab)


PK_TT = 128
PK_A_BLOCKS = ((0, 16), (1, 8), (2, 5), (3, 4))
PK_B_BLOCKS = ((0, 4, 16), (1, 4, 8), (2, 4, 5))
NEG_HUGE = -3.0e38
PK_HEADS_PER_TRIP = 8


def _extract_top(problems, flat, n):
    vals = [[] for _ in problems]
    picked = [[] for _ in problems]
    cur = [v for v, _ in problems]
    for _ in range(n):
        for i, (_, payload) in enumerate(problems):
            m = jnp.max(cur[i], axis=0, keepdims=True)
            pos = jnp.min(jnp.where(cur[i] == m, flat, 1e9), axis=0, keepdims=True)
            hit = flat == pos
            picked[i].append(pos if payload is None else jnp.sum(jnp.where(hit, payload, 0.0), axis=0, keepdims=True))
            cur[i] = jnp.where(hit, NEG_HUGE, cur[i])
            vals[i].append(m)
    return [jnp.concatenate(v, axis=0) for v in vals], [jnp.concatenate(p, axis=0) for p in picked]


def _candidate_grid(first, second, combine):
    up8 = lambda n: -(-n // 8) * 8
    return jnp.concatenate([combine(first[a:a + 1], second[0:up8(nb)]) for a, nb in PK_A_BLOCKS]
                           + [combine(first[0:up8(a1)], second[b:b + 1]) for b, _, a1 in PK_B_BLOCKS], axis=0)


def _candidate_valid_and_flat():
    up8 = lambda n: -(-n // 8) * 8
    valid, flat = [], []
    for a, nb in PK_A_BLOCKS:
        r = lax.broadcasted_iota(jnp.int32, (up8(nb), PK_TT), 0)
        valid.append(r < nb)
        flat.append(a * PEER_TOPK + r)
    for b, a0, a1 in PK_B_BLOCKS:
        r = lax.broadcasted_iota(jnp.int32, (up8(a1), PK_TT), 0)
        valid.append((r >= a0) & (r < a1))
        flat.append(r * PEER_TOPK + b)
    valid = jnp.concatenate(valid, axis=0)
    return valid, jnp.where(valid, jnp.concatenate(flat, axis=0).astype(F32), 1e9)


def _peer_topk_kernel(q_ref, ka_ref, eid_ref, w_ref):
    key_row = lax.broadcasted_iota(jnp.int32, (N_KEYS, PK_TT), 0).astype(F32)
    valid, flat = _candidate_valid_and_flat()

    def heads(i, carry):
        scores = []
        for j in range(2 * PK_HEADS_PER_TRIP):
            q = q_ref[2 * PK_HEADS_PER_TRIP * i + j]
            q_hi = _trunc_bf16(q)
            qa = jnp.concatenate([q_hi.astype(BF16), (q - q_hi).astype(BF16), q_hi.astype(BF16)], axis=1)
            scores.append(lax.dot_general(ka_ref[2 * PK_HEADS_PER_TRIP * i + j], qa, (((1,), (1,)), ((), ())),
                                          preferred_element_type=F32))
        sub_s, sub_i = _extract_top([(s, None) for s in scores], key_row, PEER_TOPK)
        cands = []
        for j in range(PK_HEADS_PER_TRIP):
            s1, s2, i1, i2 = sub_s[2 * j], sub_s[2 * j + 1], sub_i[2 * j], sub_i[2 * j + 1]
            cands.append((jnp.where(valid, _candidate_grid(s1, s2, lambda x, y: x + y), NEG_HUGE),
                          _candidate_grid(i1, i2, lambda x, y: x * N_KEYS + y)))
        top_s, top_e = _extract_top(cands, flat, PEER_TOPK)
        for j in range(PK_HEADS_PER_TRIP):
            e = jnp.exp(top_s[j] - top_s[j][0:1])
            w_ref[PK_HEADS_PER_TRIP * i + j] = e / jnp.sum(e, axis=0, keepdims=True)
            eid_ref[PK_HEADS_PER_TRIP * i + j] = top_e[j].astype(jnp.int32) * ROW_SUB
        return carry

    lax.fori_loop(0, PEER_HEADS // PK_HEADS_PER_TRIP, heads, 0)


def peer_topk(qt, subkeys):
    hp, t, d = qt.shape
    k_hi, k_lo = _split2(subkeys.reshape(hp, N_KEYS, d))
    ka = jnp.concatenate([k_hi, k_hi, k_lo], axis=2)
    eid, w = pl.pallas_call(
        _peer_topk_kernel,
        grid=(t // PK_TT,),
        in_specs=[pl.BlockSpec((hp, PK_TT, d), lambda i: (0, i, 0)),
                  pl.BlockSpec((hp, N_KEYS, 3 * d), lambda i: (0, 0, 0))],
        out_specs=[pl.BlockSpec((PEER_HEADS, PEER_TOPK, PK_TT), lambda i: (0, 0, i)),
                   pl.BlockSpec((PEER_HEADS, PEER_TOPK, PK_TT), lambda i: (0, 0, i))],
        out_shape=[jax.ShapeDtypeStruct((PEER_HEADS, PEER_TOPK, t), jnp.int32),
                   jax.ShapeDtypeStruct((PEER_HEADS, PEER_TOPK, t), F32)],
        name="peer_topk",
    )(qt, ka)
    to_slots = lambda a: a.transpose(2, 0, 1).reshape(t, PEER_SLOTS)
    return to_slots(eid), to_slots(w)


def split_columns(t, sizes):
    offs = np.cumsum((0,) + tuple(sizes))
    return [t[..., int(a):int(b)] for a, b in zip(offs[:-1], offs[1:])]


def rope_tables(pos):
    inv = jnp.power(ROPE_THETA, -jnp.arange(0, ROPE_DIM, 2, dtype=jnp.float32) / ROPE_DIM)
    ang = pos[:, None] * inv[None, :]
    return jnp.cos(ang), jnp.sin(ang)


def apply_partial_rope(x, cos, sin):
    half = ROPE_DIM // 2
    xr = x[..., :ROPE_DIM].astype(jnp.float32)
    x1, x2 = xr[..., :half], xr[..., half:]
    c, s = cos[:, None, :], sin[:, None, :]
    rot = jnp.concatenate([x1 * c - x2 * s, x1 * s + x2 * c], axis=-1)
    return jnp.concatenate([rot.astype(x.dtype), x[..., ROPE_DIM:]], axis=-1)


LOG2_E = 1.4426950408889634
LOGIT_SCALE = HEAD_DIM ** -0.5 * LOG2_E


def _logits(ks, qts):
    return [jnp.dot(k, qt, preferred_element_type=F32) for k, qt in zip(ks, qts)]


def _flash_update(ss, vts, mask, carries):
    stats, ps = [], []
    for s, (m, l, _) in zip(ss, carries):
        s_vis = s if mask is None else jnp.where(mask, s, NEG_INF)
        m_new = jnp.maximum(m, jnp.max(s_vis, axis=0, keepdims=True))
        alpha = jnp.exp2(m - m_new)
        p = jnp.exp2(s - m_new)
        if mask is not None:
            p = jnp.where(mask, p, 0.0)
        stats.append((m_new, alpha, alpha * l + jnp.sum(p, axis=0, keepdims=True)))
        ps.append(p.astype(BF16))
    return tuple((m_new, l, alpha * acc + jnp.dot(vt, p, preferred_element_type=F32))
                 for (m_new, alpha, l), p, vt, (_, _, acc) in zip(stats, ps, vts, carries))


def _causal_flash(key_tile, value_tile, qts, n_full, last_mask, n):
    def body(kt, carries):
        return _flash_update(_logits(key_tile(kt), qts), value_tile(kt), None, carries)

    carries = lax.fori_loop(0, n_full, body, tuple(_flash_init(n) for _ in qts))
    return _flash_update(_logits(key_tile(n_full), qts), value_tile(n_full), last_mask, carries)


def _flash_init(n):
    return (jnp.full((1, n), NEG_INF, F32), jnp.zeros((1, n), F32), jnp.zeros((HEAD_DIM, n), F32))


FOX_TQ = 256
FOX_KV = 512
FOX_DK = 256


FOX_HP = 4


def _fox_kernel(qt_ref, k_ref, vt_ref, o_ref):
    qi = pl.program_id(1)
    q0 = qi * FOX_TQ
    qts = [qt_ref[h] for h in range(FOX_HP)]
    qpos = q0 + lax.broadcasted_iota(jnp.int32, (1, FOX_TQ), 1)
    krow = lax.broadcasted_iota(jnp.int32, (FOX_KV, FOX_TQ), 0)

    n_full = q0 // FOX_KV
    causal = (n_full * FOX_KV + krow) <= qpos
    carry = _causal_flash(lambda kt: [k_ref[h, kt] for h in range(FOX_HP)],
                          lambda kt: [vt_ref[h, kt] for h in range(FOX_HP)], qts, n_full, causal, FOX_TQ)
    for h in range(FOX_HP):
        _, l, acc = carry[h]
        o_ref[h] = acc / l


def _trunc_bf16(x):
    bits = lax.bitcast_convert_type(x, jnp.uint32) & jnp.uint32(0xFFFF0000)
    return lax.bitcast_convert_type(bits, F32)


def _split3(c):
    c1 = _trunc_bf16(c)
    r = c - c1
    c2 = _trunc_bf16(r)
    return c1.astype(BF16), c2.astype(BF16), (r - c2).astype(BF16)


def _split2(x):
    hi = _trunc_bf16(x)
    return hi.astype(BF16), (x - hi).astype(BF16)


def _aug_q(q):
    hi, lo = _split2(q)
    return jnp.concatenate([hi, hi, lo], axis=-1)


def _aug_k(k):
    hi, lo = _split2(k)
    return jnp.concatenate([hi, lo, hi], axis=-1)


def fox_attention(q, k, v, log_f):
    B_, S_, H, Dh = q.shape
    KV_TILE = FOX_KV
    n_qt, n_kt = S_ // FOX_TQ, S_ // KV_TILE
    c = jnp.cumsum(log_f, axis=1)
    c1, c2, c3 = _split3(c * LOG2_E)
    j = jnp.arange(FOX_DK - 3 * Dh)
    pick = lambda t: t[..., None].astype(F32)
    terms = lambda first: jnp.where(j == first, pick(c1), jnp.where(j == first + 1, pick(c2), pick(c3)))
    q_extra = jnp.where(j < 3, 1.0, jnp.where(j < 6, terms(3), 0.0)).astype(BF16)
    k_extra = jnp.where(j < 3, -terms(0), jnp.where(j < 6, 1.0, 0.0)).astype(BF16)
    qa = jnp.concatenate([_aug_q(q * LOGIT_SCALE), q_extra], axis=-1)
    ka = jnp.concatenate([_aug_k(k), k_extra], axis=-1)
    HP, HG = FOX_HP, H // FOX_HP
    qt = qa.reshape(B_, n_qt, FOX_TQ, HG, HP, FOX_DK).transpose(0, 3, 1, 4, 5, 2).reshape(B_ * HG * n_qt, HP, FOX_DK, FOX_TQ)
    kk = ka.reshape(B_, n_kt, KV_TILE, HG, HP, FOX_DK).transpose(0, 3, 4, 1, 2, 5).reshape(B_ * HG, HP, n_kt, KV_TILE, FOX_DK)
    vt = v.astype(BF16).reshape(B_, n_kt, KV_TILE, HG, HP, Dh).transpose(0, 3, 4, 1, 5, 2).reshape(B_ * HG, HP, n_kt, Dh, KV_TILE)
    out = pl.pallas_call(
        _fox_kernel,
        grid=(B_ * HG, n_qt),
        in_specs=[pl.BlockSpec((None, HP, FOX_DK, FOX_TQ), lambda b, i: (b * n_qt + i, 0, 0, 0)),
                  pl.BlockSpec((None, HP, n_kt, KV_TILE, FOX_DK), lambda b, i: (b, 0, 0, 0, 0)),
                  pl.BlockSpec((None, HP, n_kt, Dh, KV_TILE), lambda b, i: (b, 0, 0, 0, 0))],
        out_specs=pl.BlockSpec((None, HP, Dh, FOX_TQ), lambda b, i: (b * n_qt + i, 0, 0, 0)),
        out_shape=jax.ShapeDtypeStruct((B_ * HG * n_qt, HP, Dh, FOX_TQ), F32),
        name="fox_attention",
    )(qt, kk, vt)
    return out.reshape(B_, HG, n_qt, HP, Dh, FOX_TQ)


NSA_TQ = 128
NSA_N = NSA_HPG * NSA_TQ
N_CMP_PAD = 256
N_SLC = 64
NSA_DK = 3 * HEAD_DIM
NSA_KV = 512
WIN_TILES = WINDOW // NSA_TQ + 1
WIN_KEYS = WIN_TILES * NSA_TQ


def _nsa_kernel(qt_ref, kc_ref, vct_ref, ks_ref, vst_ref, kw_ref, vwt_ref, gate_ref, ovl_ref, o_ref):
    qi = pl.program_id(1)
    q0 = qi * NSA_TQ
    qt = qt_ref[...]
    lane = lax.broadcasted_iota(jnp.int32, (1, NSA_N), 1)
    qpos = q0 + (lane & (NSA_TQ - 1))

    hn = NSA_N // 2
    qts = (qt[:, :hn], qt[:, hn:])
    qpos_h = qpos[:, :hn]

    w0 = jnp.maximum(qi - WINDOW // NSA_TQ, 0)
    k_w = jnp.concatenate([kw_ref[w0 + i] for i in range(WIN_TILES)], axis=0)
    vt_w = jnp.concatenate([vwt_ref[w0 + i] for i in range(WIN_TILES)], axis=1)
    s = jnp.dot(kc_ref[...], qt, preferred_element_type=F32)
    s_w = [jnp.dot(k_w, qts[h], preferred_element_type=F32) for h in range(2)]

    cmp_end = lax.broadcasted_iota(jnp.int32, (N_CMP_PAD, NSA_N), 0) * CMP_STRIDE + (CMP_LEN - 1)
    mc = cmp_end <= qpos
    m = jnp.max(jnp.where(mc, s, NEG_INF), axis=0, keepdims=True)
    p = jnp.where(mc, jnp.exp2(s - m), 0.0)
    l = jnp.sum(p, axis=0, keepdims=True)
    pc = p * jnp.where(l > 0.0, 1.0 / l, 0.0)
    o_cmp = jnp.dot(vct_ref[...], pc.astype(BF16), preferred_element_type=F32)
    pcs = (pc[:, 0:NSA_TQ] + pc[:, NSA_TQ:2 * NSA_TQ]) + (pc[:, 2 * NSA_TQ:3 * NSA_TQ] + pc[:, 3 * NSA_TQ:4 * NSA_TQ])
    pcs_hi = pcs.astype(BF16)
    pcs_lo = (pcs - pcs_hi.astype(F32)).astype(BF16)
    ovl = ovl_ref[...]
    imp = jnp.dot(ovl, pcs_hi, preferred_element_type=F32) + jnp.dot(ovl, pcs_lo, preferred_element_type=F32)

    rel = qpos_h - (w0 * NSA_TQ + lax.broadcasted_iota(jnp.int32, (WIN_KEYS, hn), 0))
    in_win = (rel >= 0) & (rel < WINDOW)
    outs_w = []
    for h in range(2):
        m = jnp.max(jnp.where(in_win, s_w[h], NEG_INF), axis=0, keepdims=True)
        p = jnp.where(in_win, jnp.exp2(s_w[h] - m), 0.0)
        l = jnp.sum(p, axis=0, keepdims=True)
        outs_w.append(jnp.dot(vt_w, p.astype(BF16), preferred_element_type=F32) / l)
    acc_w = jnp.concatenate(outs_w, axis=1)

    blk = lax.broadcasted_iota(jnp.int32, (N_SLC, NSA_TQ), 0)
    q_blk = (q0 + lax.broadcasted_iota(jnp.int32, (N_SLC, NSA_TQ), 1)) >> 6
    forced = (blk == 0) | (blk == q_blk) | (blk == q_blk - 1)
    score = jnp.where(forced, FORCED_SCORE, jnp.where(blk <= q_blk, imp, -1.0))
    rank = jnp.zeros((N_SLC, NSA_TQ), jnp.int32)
    for i in range(N_SLC):
        row = score[i:i + 1, :]
        beats = (row > score) | ((row == score) & (blk > i))
        rank = rank + beats.astype(jnp.int32)
    bias = jnp.where(rank < SLC_TOP, 0.0, NEG_INF).astype(BF16)
    bias = jnp.concatenate([bias] * (NSA_HPG // 2), axis=1)
    qts_s = [jnp.concatenate([qh, bias], axis=0) for qh in qts]
    krow = lax.broadcasted_iota(jnp.int32, (NSA_KV, hn), 0)

    n_full = q0 // NSA_KV
    causal = (n_full * NSA_KV + krow) <= qpos_h
    slc = _causal_flash(lambda kt: [ks_ref[kt]] * 2, lambda kt: [vst_ref[kt]] * 2, qts_s, n_full, causal, hn)
    acc_s = jnp.concatenate([slc[0][2] / slc[0][1], slc[1][2] / slc[1][1]], axis=1)

    g = jax.nn.sigmoid(gate_ref[...])
    o_ref[...] = g[0:1] * o_cmp + g[1:2] * acc_s + g[2:3] * acc_w


def nsa_attention(q, k_cmp, v_cmp, k_slc, v_slc, k_win, v_win, gate_logit):
    B_, S_, H, Dh = q.shape
    G = NSA_GROUPS
    n_qt = S_ // NSA_TQ
    n_cmp = k_cmp.shape[1]
    cs = np.arange(N_CMP_PAD)[None, :] * CMP_STRIDE
    ss = np.arange(N_SLC)[:, None] * SLC_LEN
    ov = np.clip(np.minimum(cs + CMP_LEN, ss + SLC_LEN) - np.maximum(cs, ss), 0, None) / CMP_LEN
    ov[:, n_cmp:] = 0.0
    ovl = jnp.asarray(ov, dtype=BF16)
    qs = _aug_q(q * LOGIT_SCALE).reshape(B_, n_qt, NSA_TQ, G, NSA_HPG, NSA_DK)
    qt = qs.transpose(0, 3, 1, 5, 4, 2).reshape(B_ * G * n_qt, NSA_DK, NSA_N)
    gt = gate_logit.astype(F32).reshape(B_, n_qt, NSA_TQ, G, NSA_HPG, N_NSA_BRANCH)
    gt = gt.transpose(0, 3, 1, 5, 4, 2).reshape(B_ * G * n_qt, N_NSA_BRANCH, NSA_N)
    tiles = lambda t, kv: t.reshape(B_, S_ // kv, kv, G, -1).transpose(0, 3, 1, 2, 4).reshape(B_ * G, S_ // kv, kv, t.shape[-1])
    keys = lambda t, kv: tiles(_aug_k(t), kv)
    block_onehot = (jnp.arange(S_)[:, None] // SLC_LEN == jnp.arange(N_SLC)[None, :]).astype(BF16)
    k_slc_aug = jnp.concatenate([_aug_k(k_slc), jnp.broadcast_to(block_onehot[None, :, None, :], (B_, S_, G, N_SLC))], axis=-1)
    vals = lambda t, kv: t.astype(BF16).reshape(B_, S_ // kv, kv, G, Dh).transpose(0, 3, 1, 4, 2).reshape(B_ * G, S_ // kv, Dh, kv)
    padc = ((0, 0), (0, N_CMP_PAD - n_cmp), (0, 0), (0, 0))
    kc = _aug_k(jnp.pad(k_cmp, padc)).transpose(0, 2, 1, 3).reshape(B_ * G, N_CMP_PAD, NSA_DK)
    vct = jnp.pad(v_cmp, padc).astype(BF16).transpose(0, 2, 3, 1).reshape(B_ * G, Dh, N_CMP_PAD)
    per_bg = lambda *blk: pl.BlockSpec((None,) + blk, lambda b, i: (b,) + (0,) * len(blk))
    per_tile = lambda *blk: pl.BlockSpec((None,) + blk, lambda b, i: (b * n_qt + i,) + (0,) * len(blk))
    out = pl.pallas_call(
        _nsa_kernel,
        grid=(B_ * G, n_qt),
        in_specs=[per_tile(NSA_DK, NSA_N),
                  per_bg(N_CMP_PAD, NSA_DK), per_bg(Dh, N_CMP_PAD),
                  per_bg(S_ // NSA_KV, NSA_KV, NSA_DK + N_SLC), per_bg(S_ // NSA_KV, Dh, NSA_KV),
                  per_bg(n_qt, NSA_TQ, NSA_DK), per_bg(n_qt, Dh, NSA_TQ),
                  per_tile(N_NSA_BRANCH, NSA_N),
                  pl.BlockSpec((N_SLC, N_CMP_PAD), lambda b, i: (0, 0))],
        out_specs=per_tile(Dh, NSA_N),
        out_shape=jax.ShapeDtypeStruct((B_ * G * n_qt, Dh, NSA_N), F32),
        name="nsa_attention",
    )(qt, kc, vct, tiles(k_slc_aug, NSA_KV), vals(v_slc, NSA_KV), keys(k_win, NSA_TQ), vals(v_win, NSA_TQ), gt, ovl)
    return out.reshape(B_, G, n_qt, Dh, NSA_N)


DENSE_TM = 256
SEC_MERGE = N_BRANCH * D_MODEL
SEC_SMALL = 256
IN_SECTIONS = (BRANCH_WIDTH,) * 4 + (KV_WIDTH,) * 6 + (SEC_MERGE, SEC_SMALL)


def _rms(x, g):
    return x * lax.rsqrt(jnp.mean(x * x, axis=-1, keepdims=True) + RMS_EPS) * g


def _row_spec(width):
    return pl.BlockSpec((DENSE_TM, width), lambda i: (i, 0))


def _whole_spec(shape):
    return pl.BlockSpec(shape, lambda i: (0,) * len(shape))


def _in_proj_kernel(x_ref, g_ref, w_ref, *o_refs):
    xn = _rms(x_ref[...], g_ref[...]).astype(BF16)
    off = 0
    for o_ref, width in zip(o_refs, IN_SECTIONS):
        o_ref[...] = jnp.dot(xn, w_ref[:, off:off + width], preferred_element_type=F32)
        off += width


def in_projection(x, g, w_in):
    t, d = x.shape
    fq, fk, fv, fl, nq, kc, vc, ksl, vsl, kwn, vwn, gl, ml = split_columns(w_in, IN_SPLITS)
    pad = lambda w: jnp.pad(w, ((0, 0), (0, 128 - w.shape[1])))
    w = jnp.concatenate([fq, fk, fv, nq, kc, vc, ksl, vsl, kwn, vwn, ml, pad(fl), pad(gl)], axis=1).astype(BF16)
    return pl.pallas_call(
        _in_proj_kernel,
        grid=(t // DENSE_TM,),
        in_specs=[_row_spec(d), _whole_spec((1, d)), _whole_spec(w.shape)],
        out_specs=[_row_spec(s) for s in IN_SECTIONS],
        out_shape=[jax.ShapeDtypeStruct((t, s), F32) for s in IN_SECTIONS],
        name="in_projection",
    )(x, g.reshape(1, d), w)


def _merge_kernel(x_ref, yf_ref, yn_ref, ml_ref, wb_ref, wo_ref, o_ref):
    yf_t = yf_ref[...].reshape(BRANCH_WIDTH, DENSE_TM)
    yn_t = jnp.concatenate(
        [jnp.concatenate([yn_ref[g, j, :, h * NSA_TQ:(h + 1) * NSA_TQ]
                          for g in range(NSA_GROUPS) for h in range(NSA_HPG)], axis=0)
         for j in range(DENSE_TM // NSA_TQ)], axis=1)
    g = jax.nn.sigmoid(ml_ref[...])
    up_f = jnp.dot(yf_t.T.astype(BF16), wb_ref[0], preferred_element_type=F32)
    up_n = jnp.dot(yn_t.T.astype(BF16), wb_ref[1], preferred_element_type=F32)
    merged = g[:, :D_MODEL] * up_f + g[:, D_MODEL:] * up_n
    o_ref[...] = x_ref[...] + jnp.dot(merged.astype(BF16), wo_ref[...], preferred_element_type=F32)


def merge_out(x, y_fox, y_nsa, merge_logit, w_branch, w_out):
    t, d = x.shape
    assert FOX_TQ == DENSE_TM and DENSE_TM % NSA_TQ == 0
    tiles_per_seq = y_fox.shape[2]
    nsa_per_tile = DENSE_TM // NSA_TQ
    yf_spec = pl.BlockSpec((None,) + y_fox.shape[1:2] + (None,) + y_fox.shape[3:],
                           lambda i: (i // tiles_per_seq, 0, i % tiles_per_seq, 0, 0, 0))
    yn_spec = pl.BlockSpec((None, NSA_GROUPS, nsa_per_tile, HEAD_DIM, NSA_N),
                           lambda i: (i // tiles_per_seq, 0, i % tiles_per_seq, 0, 0))
    return pl.pallas_call(
        _merge_kernel,
        grid=(t // DENSE_TM,),
        in_specs=[_row_spec(d), yf_spec, yn_spec, _row_spec(SEC_MERGE),
                  _whole_spec(w_branch.shape), _whole_spec(w_out.shape)],
        out_specs=_row_spec(d),
        out_shape=jax.ShapeDtypeStruct((t, d), F32),
        name="merge_out",
    )(x, y_fox, y_nsa, merge_logit, w_branch.astype(BF16), w_out.astype(BF16))


def _peer_q_kernel(x_ref, g_ref, w_ref, hn_ref, q_ref):
    hn = _rms(x_ref[...], g_ref[...])
    hn_ref[...] = hn
    q = jnp.dot(hn.astype(BF16), w_ref[...], preferred_element_type=F32)
    d = PEER_QDIM // 2
    for j in range(2 * PEER_HEADS):
        q_ref[j] = q[:, j * d:(j + 1) * d]


def peer_query(x, g, wq):
    t, d = x.shape
    hp, dq = 2 * PEER_HEADS, PEER_QDIM // 2
    return pl.pallas_call(
        _peer_q_kernel,
        grid=(t // DENSE_TM,),
        in_specs=[_row_spec(d), _whole_spec((1, d)), _whole_spec(wq.shape)],
        out_specs=[_row_spec(d), pl.BlockSpec((hp, DENSE_TM, dq), lambda i: (0, i, 0))],
        out_shape=[jax.ShapeDtypeStruct((t, d), F32), jax.ShapeDtypeStruct((hp, t, dq), F32)],
        name="peer_query",
    )(x, g.reshape(1, d), wq.astype(BF16))


N_CMP_ROWS = 256


def _compress_kernel(r_ref, pos_ref, w1_ref, w2_ref, o_ref):
    r = r_ref[...]
    nxt = jnp.concatenate([r[1:], jnp.zeros((1, r.shape[1]), F32)], axis=0)
    half = r.shape[1]
    hid = (jnp.dot((r + pos_ref[0:1, :]).astype(BF16), w1_ref[:half], preferred_element_type=F32)
           + jnp.dot((nxt + pos_ref[1:2, :]).astype(BF16), w1_ref[half:], preferred_element_type=F32))
    o_ref[...] = jnp.dot(_gelu(hid).astype(BF16), w2_ref[...], preferred_element_type=F32)


def compress_kv_pallas(kv, cmp_pos, w1, w2):
    _, B_, S_, G, Dh = kv.shape
    n_cmp = (S_ - CMP_LEN) // CMP_STRIDE + 1
    width = CMP_STRIDE * Dh
    rows = kv.transpose(0, 1, 3, 2, 4).reshape(2, B_ * G, S_ // CMP_STRIDE, width)
    out = pl.pallas_call(
        _compress_kernel,
        grid=(2, B_ * G),
        in_specs=[pl.BlockSpec((None, None, N_CMP_ROWS, width), lambda j, b: (j, b, 0, 0)),
                  pl.BlockSpec((None, 2, width), lambda j, b: (j, 0, 0)),
                  pl.BlockSpec((None, CMP_LEN * Dh, CMP_HIDDEN), lambda j, b: (j, 0, 0)),
                  pl.BlockSpec((None, CMP_HIDDEN, Dh), lambda j, b: (j, 0, 0))],
        out_specs=pl.BlockSpec((None, None, N_CMP_ROWS, Dh), lambda j, b: (j, b, 0, 0)),
        out_shape=jax.ShapeDtypeStruct((2, B_ * G, N_CMP_ROWS, Dh), F32),
        name="compress_kv",
    )(rows, cmp_pos.reshape(2, 2, width), w1.astype(BF16), w2.astype(BF16))
    return out[:, :, :n_cmp].reshape(2, B_, G, n_cmp, Dh).transpose(0, 1, 3, 2, 4)


def hybrid_mixer(x, norm_g, w_in, fox_f_bias, cmp_pos, cmp_w1, cmp_w2, w_branch, w_out):
    B_, S_, D = x.shape
    xf = x.reshape(B_ * S_, D)
    secs = in_projection(xf, norm_g, w_in)
    heads = lambda t, n: t.reshape(B_, S_, n, HEAD_DIM)
    fq, fk, fv, nq = secs[:4]
    kc, vc, ksl, vsl, kwn, vwn = (heads(t, NSA_GROUPS) for t in secs[4:10])
    merge_logit, small = secs[10:]
    f_logit = small[:, :FOX_HEADS].reshape(B_, S_, FOX_HEADS)
    gate_logit = small[:, 128:128 + NSA_HEADS * N_NSA_BRANCH].reshape(B_, S_, NSA_HEADS, N_NSA_BRANCH)
    log_f = jax.nn.log_sigmoid(f_logit + fox_f_bias.astype(F32))
    y_fox = fox_attention(heads(fq, FOX_HEADS), heads(fk, FOX_HEADS), heads(fv, FOX_HEADS), log_f)
    cos, sin = rope_tables(jnp.arange(S_, dtype=F32))
    q_nsa = apply_partial_rope(heads(nq, NSA_HEADS), cos, sin)
    k_slc = apply_partial_rope(ksl, cos, sin)
    k_win = apply_partial_rope(kwn, cos, sin)
    cmp = compress_kv_pallas(jnp.stack([kc, vc]), cmp_pos, cmp_w1, cmp_w2)
    n_cmp = cmp.shape[2]
    cmp_end = jnp.arange(n_cmp, dtype=F32) * CMP_STRIDE + (CMP_LEN - 1)
    k_cmp = apply_partial_rope(cmp[0], *rope_tables(cmp_end))
    y_nsa = nsa_attention(q_nsa, k_cmp, cmp[1], k_slc, vsl, k_win, vwn, gate_logit)
    return merge_out(xf, y_fox, y_nsa, merge_logit, w_branch, w_out)


def peer_ffn(h, norm_g, wq, subkeys, u, v, norm_final):
    hn, q = peer_query(h, norm_g, wq)
    idx2, w = peer_topk(q, subkeys)
    coef = peer_u(idx2, hn, w, pack_table(u))
    return peer_v(idx2, coef, pack_table(v), h, norm_final)


def kernel(x, norm_mix, w_in, fox_f_bias, nsa_cmp_pos, nsa_cmp_w1, nsa_cmp_w2, w_branch, w_out,
           norm_ffn, peer_wq, peer_subkeys, peer_u, peer_v, norm_final):
    B_, S_, D = x.shape
    assert norm_mix.shape[0] == 1, "single-layer trunk"
    h = hybrid_mixer(x, norm_mix[0], w_in[0], fox_f_bias[0], nsa_cmp_pos[0], nsa_cmp_w1[0], nsa_cmp_w2[0],
                     w_branch[0], w_out[0])
    out = peer_ffn(h, norm_ffn[0], peer_wq[0], peer_subkeys[0], peer_u[0], peer_v[0], norm_final)
    return out.reshape(B_, S_, D)
```

```python
import jax
import jax.numpy as jnp
import numpy as np
from jax import lax
from jax.experimental import pallas as pl
from jax.experimental.pallas import tpu as pltpu

D_MODEL = 1024
HEAD_DIM = 64
FOX_HEADS = 8
NSA_HEADS = 8
NSA_GROUPS = 2
NSA_HPG = NSA_HEADS // NSA_GROUPS
BRANCH_WIDTH = 512
N_BRANCH = 2
N_NSA_BRANCH = 3
ROPE_DIM = HEAD_DIM // 4
ROPE_THETA = 500000.0
CMP_LEN = 32
CMP_STRIDE = 16
CMP_HIDDEN = 2 * HEAD_DIM
SLC_LEN = 64
SLC_TOP = 16
WINDOW = 512
FORCED_SCORE = 1e9
NEG_INF = -1e30
PEER_HEADS = 8
N_KEYS = 128
PEER_QDIM = 256
PEER_TOPK = 16
RMS_EPS = 1e-6
KV_WIDTH = NSA_GROUPS * HEAD_DIM
IN_SPLITS = (BRANCH_WIDTH, BRANCH_WIDTH, BRANCH_WIDTH, FOX_HEADS, BRANCH_WIDTH,
             KV_WIDTH, KV_WIDTH, KV_WIDTH, KV_WIDTH, KV_WIDTH, KV_WIDTH,
             NSA_HEADS * N_NSA_BRANCH, N_BRANCH * D_MODEL)


BF16 = jnp.bfloat16
F32 = jnp.float32

PEER_SLOTS = PEER_HEADS * PEER_TOPK
PEER_TT = 128
PEER_CHUNK_SLOTS = 32
PEER_U_UNROLL = 8
PEER_V_UNROLL = 8
ROW_SUB = D_MODEL // 2 // 128
VMEM_LIMIT_PEER = N_KEYS * N_KEYS * D_MODEL * 2 + 16 * 1024 * 1024


def pack_table(tab):
    n, d = tab.shape
    return pl.pallas_call(
        _pack_kernel,
        grid=(n // PACK_TM,),
        in_specs=[pl.BlockSpec((PACK_TM, d), lambda i: (i, 0))],
        out_specs=pl.BlockSpec((PACK_TM * ROW_SUB, 128), lambda i: (i, 0)),
        out_shape=jax.ShapeDtypeStruct((n * ROW_SUB, 128), jnp.uint32),
        name="pack_table",
    )(tab)


PACK_TM = 256


def _pack_kernel(x_ref, o_ref):
    bits = lax.bitcast_convert_type(x_ref[...].astype(jnp.bfloat16).astype(jnp.float32), jnp.uint32)
    for r in range(ROW_SUB):
        even = bits[:, (2 * r) * 128:(2 * r + 1) * 128]
        odd = bits[:, (2 * r + 1) * 128:(2 * r + 2) * 128]
        o_ref[pl.ds(r, PACK_TM, stride=ROW_SUB), :] = (even >> 16) | (odd & jnp.uint32(0xFFFF0000))


def _expert_slab(tab_ref, row):
    return tab_ref[pl.ds(pl.multiple_of(row, ROW_SUB), ROW_SUB), :]


def _unpack(w):
    lo = lax.bitcast_convert_type(w << 16, jnp.float32)
    hi = lax.bitcast_convert_type(w & jnp.uint32(0xFFFF0000), jnp.float32)
    return lo, hi


def _gelu(x):
    return 0.5 * x * (1.0 + lax.erf(x * (2.0 ** -0.5)))


def _peer_u_kernel(idx_a_ref, idx_b_ref, x_ref, w_ref, tab_ref, out_ref):
    lane = lax.broadcasted_iota(jnp.int32, (2 * ROW_SUB, 128), 1)
    lower = lax.broadcasted_iota(jnp.int32, (2 * ROW_SUB, 128), 0) >= ROW_SUB
    pair_lane = jnp.where(lower, lane - PEER_SLOTS // 2, lane)

    def token(t):
        row = x_ref[pl.ds(t, 1), :]
        chunks = [row[:, c * 128:(c + 1) * 128] for c in range(2 * ROW_SUB)]
        xl = jnp.concatenate(chunks[0::2] * 2, axis=0)
        xh = jnp.concatenate(chunks[1::2] * 2, axis=0)
        acc = jnp.zeros((2 * ROW_SUB, 128), jnp.float32)
        base = t * (PEER_SLOTS // 2)
        for s in range(PEER_SLOTS // 2):
            w = jnp.concatenate([_expert_slab(tab_ref, idx_a_ref[base + s]),
                                 _expert_slab(tab_ref, idx_b_ref[base + s])], axis=0)
            lo, hi = _unpack(w)
            part = jnp.sum(lo * xl + hi * xh, axis=1, keepdims=True)
            acc = jnp.where(pair_lane == s, part, acc)
        out_ref[pl.ds(t, 1), :] = w_ref[pl.ds(t, 1), :] * _gelu(jnp.sum(acc, axis=0, keepdims=True))

    def trip(i, carry):
        for j in range(PEER_U_UNROLL):
            token(i * PEER_U_UNROLL + j)
        return carry

    lax.fori_loop(0, PEER_TT // PEER_U_UNROLL, trip, 0)


N_CHUNK = 2 * ROW_SUB


def _peer_v_kernel(idx_a_ref, idx_b_ref, coef_ref, expand_ref, h_ref, g_ref, tab_ref, out_ref,
                   ce_hi_ref, ce_lo_ref):
    half_slots = PEER_SLOTS // 2
    width = PEER_SLOTS * N_CHUNK
    coef = coef_ref[...]
    c_hi = _trunc_bf16(coef)
    expand = expand_ref[...]
    ce_hi_ref[...] = jnp.dot(c_hi.astype(BF16), expand, preferred_element_type=F32)
    ce_lo_ref[...] = jnp.dot((coef - c_hi).astype(BF16), expand, preferred_element_type=F32)
    own_chunk = ((lax.broadcasted_iota(jnp.int32, (N_CHUNK, width), 1) & (N_CHUNK - 1))
                 == lax.broadcasted_iota(jnp.int32, (N_CHUNK, width), 0))

    def token(t):
        base = t * half_slots
        slabs = [jnp.concatenate([_expert_slab(tab_ref, idx_a_ref[base + s]),
                                  _expert_slab(tab_ref, idx_b_ref[base + s])], axis=0) for s in range(half_slots)]
        gathered = pltpu.bitcast(jnp.concatenate(slabs, axis=0), BF16)
        sel = lambda ref: jnp.where(own_chunk, ref[pl.ds(t, 1), :], 0.0).astype(BF16)
        c = jnp.concatenate([sel(ce_hi_ref), sel(ce_lo_ref)], axis=0)
        o = jnp.dot(c, gathered, preferred_element_type=F32)
        o = o[:N_CHUNK] + o[N_CHUNK:]
        out_ref[pl.ds(t, 1), :] = jnp.concatenate([o[r:r + 1, :] for r in range(N_CHUNK)], axis=1)

    def trip(i, carry):
        for j in range(PEER_V_UNROLL):
            token(i * PEER_V_UNROLL + j)
        return carry

    lax.fori_loop(0, PEER_TT // PEER_V_UNROLL, trip, 0)
    out_ref[...] = _rms(h_ref[...] + out_ref[...], g_ref[...])


def _smem_spec():
    return pl.BlockSpec((PEER_TT * PEER_SLOTS // 2,), lambda i: (i,), memory_space=pltpu.SMEM)


def _slot_halves(a):
    return a[:, :PEER_SLOTS // 2].reshape(-1), a[:, PEER_SLOTS // 2:].reshape(-1)


def _table_spec(rows):
    return pl.BlockSpec((rows, 128), lambda i: (0, 0), pipeline_mode=pl.Buffered(1))


def peer_u(idx, x, w, tab):
    t = x.shape[0]
    return pl.pallas_call(
        _peer_u_kernel,
        grid=(t // PEER_TT,),
        in_specs=[_smem_spec(), _smem_spec(), pl.BlockSpec((PEER_TT, D_MODEL), lambda i: (i, 0)),
                  pl.BlockSpec((PEER_TT, PEER_SLOTS), lambda i: (i, 0)), _table_spec(tab.shape[0])],
        out_specs=pl.BlockSpec((PEER_TT, PEER_SLOTS), lambda i: (i, 0)),
        out_shape=jax.ShapeDtypeStruct((t, PEER_SLOTS), jnp.float32),
        compiler_params=pltpu.CompilerParams(vmem_limit_bytes=VMEM_LIMIT_PEER),
        name="peer_u",
    )(*_slot_halves(idx), x, w, tab)


def peer_v(idx, coef, tab, h, g):
    t = idx.shape[0]
    width = PEER_SLOTS * N_CHUNK
    half = PEER_SLOTS // 2
    col_slot = np.arange(width) // N_CHUNK
    col_slot = (col_slot // 2) + half * (col_slot % 2)
    expand = jnp.asarray(col_slot[None, :] == np.arange(PEER_SLOTS)[:, None], dtype=BF16)
    return pl.pallas_call(
        _peer_v_kernel,
        grid=(t // PEER_TT,),
        in_specs=[_smem_spec(), _smem_spec(), pl.BlockSpec((PEER_TT, PEER_SLOTS), lambda i: (i, 0)),
                  pl.BlockSpec((PEER_SLOTS, width), lambda i: (0, 0)),
                  pl.BlockSpec((PEER_TT, D_MODEL), lambda i: (i, 0)), pl.BlockSpec((1, D_MODEL), lambda i: (0, 0)),
                  _table_spec(tab.shape[0])],
        out_specs=pl.BlockSpec((PEER_TT, D_MODEL), lambda i: (i, 0)),
        out_shape=jax.ShapeDtypeStruct((t, D_MODEL), jnp.float32),
        scratch_shapes=[pltpu.VMEM((PEER_TT, width), F32), pltpu.VMEM((PEER_TT, width), F32)],
        compiler_params=pltpu.CompilerParams(vmem_limit_bytes=VMEM_LIMIT_PEER),
        name="peer_v",
    )(*_slot_halves(idx), coef, expand, h, g.reshape(1, D_MODEL), tab)


PK_TT = 128
PK_A_BLOCKS = ((0, 16), (1, 8), (2, 5), (3, 4))
PK_B_BLOCKS = ((0, 4, 16), (1, 4, 8), (2, 4, 5))
NEG_HUGE = -3.0e38
PK_HEADS_PER_TRIP = 8


def _extract_top(problems, flat, n):
    vals = [[] for _ in problems]
    picked = [[] for _ in problems]
    cur = [v for v, _ in problems]
    for _ in range(n):
        for i, (_, payload) in enumerate(problems):
            m = jnp.max(cur[i], axis=0, keepdims=True)
            pos = jnp.min(jnp.where(cur[i] == m, flat, 1e9), axis=0, keepdims=True)
            hit = flat == pos
            picked[i].append(pos if payload is None else jnp.sum(jnp.where(hit, payload, 0.0), axis=0, keepdims=True))
            cur[i] = jnp.where(hit, NEG_HUGE, cur[i])
            vals[i].append(m)
    return [jnp.concatenate(v, axis=0) for v in vals], [jnp.concatenate(p, axis=0) for p in picked]


def _candidate_grid(first, second, combine):
    up8 = lambda n: -(-n // 8) * 8
    return jnp.concatenate([combine(first[a:a + 1], second[0:up8(nb)]) for a, nb in PK_A_BLOCKS]
                           + [combine(first[0:up8(a1)], second[b:b + 1]) for b, _, a1 in PK_B_BLOCKS], axis=0)


def _candidate_valid_and_flat():
    up8 = lambda n: -(-n // 8) * 8
    valid, flat = [], []
    for a, nb in PK_A_BLOCKS:
        r = lax.broadcasted_iota(jnp.int32, (up8(nb), PK_TT), 0)
        valid.append(r < nb)
        flat.append(a * PEER_TOPK + r)
    for b, a0, a1 in PK_B_BLOCKS:
        r = lax.broadcasted_iota(jnp.int32, (up8(a1), PK_TT), 0)
        valid.append((r >= a0) & (r < a1))
        flat.append(r * PEER_TOPK + b)
    valid = jnp.concatenate(valid, axis=0)
    return valid, jnp.where(valid, jnp.concatenate(flat, axis=0).astype(F32), 1e9)


def _peer_topk_kernel(q_ref, ka_ref, eid_ref, w_ref):
    key_row = lax.broadcasted_iota(jnp.int32, (N_KEYS, PK_TT), 0).astype(F32)
    valid, flat = _candidate_valid_and_flat()

    def heads(i, carry):
        scores = []
        for j in range(2 * PK_HEADS_PER_TRIP):
            q = q_ref[2 * PK_HEADS_PER_TRIP * i + j]
            q_hi = _trunc_bf16(q)
            qa = jnp.concatenate([q_hi.astype(BF16), (q - q_hi).astype(BF16), q_hi.astype(BF16)], axis=1)
            scores.append(lax.dot_general(ka_ref[2 * PK_HEADS_PER_TRIP * i + j], qa, (((1,), (1,)), ((), ())),
                                          preferred_element_type=F32))
        sub_s, sub_i = _extract_top([(s, None) for s in scores], key_row, PEER_TOPK)
        cands = []
        for j in range(PK_HEADS_PER_TRIP):
            s1, s2, i1, i2 = sub_s[2 * j], sub_s[2 * j + 1], sub_i[2 * j], sub_i[2 * j + 1]
            cands.append((jnp.where(valid, _candidate_grid(s1, s2, lambda x, y: x + y), NEG_HUGE),
                          _candidate_grid(i1, i2, lambda x, y: x * N_KEYS + y)))
        top_s, top_e = _extract_top(cands, flat, PEER_TOPK)
        for j in range(PK_HEADS_PER_TRIP):
            e = jnp.exp(top_s[j] - top_s[j][0:1])
            w_ref[PK_HEADS_PER_TRIP * i + j] = e / jnp.sum(e, axis=0, keepdims=True)
            eid_ref[PK_HEADS_PER_TRIP * i + j] = top_e[j].astype(jnp.int32) * ROW_SUB
        return carry

    lax.fori_loop(0, PEER_HEADS // PK_HEADS_PER_TRIP, heads, 0)


def peer_topk(qt, subkeys):
    hp, t, d = qt.shape
    k_hi, k_lo = _split2(subkeys.reshape(hp, N_KEYS, d))
    ka = jnp.concatenate([k_hi, k_hi, k_lo], axis=2)
    eid, w = pl.pallas_call(
        _peer_topk_kernel,
        grid=(t // PK_TT,),
        in_specs=[pl.BlockSpec((hp, PK_TT, d), lambda i: (0, i, 0)),
                  pl.BlockSpec((hp, N_KEYS, 3 * d), lambda i: (0, 0, 0))],
        out_specs=[pl.BlockSpec((PEER_HEADS, PEER_TOPK, PK_TT), lambda i: (0, 0, i)),
                   pl.BlockSpec((PEER_HEADS, PEER_TOPK, PK_TT), lambda i: (0, 0, i))],
        out_shape=[jax.ShapeDtypeStruct((PEER_HEADS, PEER_TOPK, t), jnp.int32),
                   jax.ShapeDtypeStruct((PEER_HEADS, PEER_TOPK, t), F32)],
        name="peer_topk",
    )(qt, ka)
    to_slots = lambda a: a.transpose(2, 0, 1).reshape(t, PEER_SLOTS)
    return to_slots(eid), to_slots(w)


def split_columns(t, sizes):
    offs = np.cumsum((0,) + tuple(sizes))
    return [t[..., int(a):int(b)] for a, b in zip(offs[:-1], offs[1:])]


def rope_tables(pos):
    inv = jnp.power(ROPE_THETA, -jnp.arange(0, ROPE_DIM, 2, dtype=jnp.float32) / ROPE_DIM)
    ang = pos[:, None] * inv[None, :]
    return jnp.cos(ang), jnp.sin(ang)


def apply_partial_rope(x, cos, sin):
    half = ROPE_DIM // 2
    xr = x[..., :ROPE_DIM].astype(jnp.float32)
    x1, x2 = xr[..., :half], xr[..., half:]
    c, s = cos[:, None, :], sin[:, None, :]
    rot = jnp.concatenate([x1 * c - x2 * s, x1 * s + x2 * c], axis=-1)
    return jnp.concatenate([rot.astype(x.dtype), x[..., ROPE_DIM:]], axis=-1)


LOG2_E = 1.4426950408889634
LOGIT_SCALE = HEAD_DIM ** -0.5 * LOG2_E


def _logits(ks, qts):
    return [jnp.dot(k, qt, preferred_element_type=F32) for k, qt in zip(ks, qts)]


def _flash_update(ss, vts, mask, carries):
    stats, ps = [], []
    for s, (m, l, _) in zip(ss, carries):
        s_vis = s if mask is None else jnp.where(mask, s, NEG_INF)
        m_new = jnp.maximum(m, jnp.max(s_vis, axis=0, keepdims=True))
        alpha = jnp.exp2(m - m_new)
        p = jnp.exp2(s - m_new)
        if mask is not None:
            p = jnp.where(mask, p, 0.0)
        stats.append((m_new, alpha, alpha * l + jnp.sum(p, axis=0, keepdims=True)))
        ps.append(p.astype(BF16))
    return tuple((m_new, l, alpha * acc + jnp.dot(vt, p, preferred_element_type=F32))
                 for (m_new, alpha, l), p, vt, (_, _, acc) in zip(stats, ps, vts, carries))


def _causal_flash(key_tile, value_tile, qts, n_full, last_mask, n):
    def body(kt, carries):
        return _flash_update(_logits(key_tile(kt), qts), value_tile(kt), None, carries)

    carries = lax.fori_loop(0, n_full, body, tuple(_flash_init(n) for _ in qts))
    return _flash_update(_logits(key_tile(n_full), qts), value_tile(n_full), last_mask, carries)


def _flash_init(n):
    return (jnp.full((1, n), NEG_INF, F32), jnp.zeros((1, n), F32), jnp.zeros((HEAD_DIM, n), F32))


FOX_TQ = 256
FOX_KV = 512
FOX_DK = 256


FOX_HP = 4


def _fox_kernel(qt_ref, k_ref, vt_ref, o_ref):
    qi = pl.program_id(1)
    q0 = qi * FOX_TQ
    qts = [qt_ref[h] for h in range(FOX_HP)]
    qpos = q0 + lax.broadcasted_iota(jnp.int32, (1, FOX_TQ), 1)
    krow = lax.broadcasted_iota(jnp.int32, (FOX_KV, FOX_TQ), 0)

    n_full = q0 // FOX_KV
    causal = (n_full * FOX_KV + krow) <= qpos
    carry = _causal_flash(lambda kt: [k_ref[h, kt] for h in range(FOX_HP)],
                          lambda kt: [vt_ref[h, kt] for h in range(FOX_HP)], qts, n_full, causal, FOX_TQ)
    for h in range(FOX_HP):
        _, l, acc = carry[h]
        o_ref[h] = acc / l


def _trunc_bf16(x):
    bits = lax.bitcast_convert_type(x, jnp.uint32) & jnp.uint32(0xFFFF0000)
    return lax.bitcast_convert_type(bits, F32)


def _split3(c):
    c1 = _trunc_bf16(c)
    r = c - c1
    c2 = _trunc_bf16(r)
    return c1.astype(BF16), c2.astype(BF16), (r - c2).astype(BF16)


def _split2(x):
    hi = _trunc_bf16(x)
    return hi.astype(BF16), (x - hi).astype(BF16)


def _aug_q(q):
    hi, lo = _split2(q)
    return jnp.concatenate([hi, hi, lo], axis=-1)


def _aug_k(k):
    hi, lo = _split2(k)
    return jnp.concatenate([hi, lo, hi], axis=-1)


def fox_attention(q, k, v, log_f):
    B_, S_, H, Dh = q.shape
    KV_TILE = FOX_KV
    n_qt, n_kt = S_ // FOX_TQ, S_ // KV_TILE
    c = jnp.cumsum(log_f, axis=1)
    c1, c2, c3 = _split3(c * LOG2_E)
    j = jnp.arange(FOX_DK - 3 * Dh)
    pick = lambda t: t[..., None].astype(F32)
    terms = lambda first: jnp.where(j == first, pick(c1), jnp.where(j == first + 1, pick(c2), pick(c3)))
    q_extra = jnp.where(j < 3, 1.0, jnp.where(j < 6, terms(3), 0.0)).astype(BF16)
    k_extra = jnp.where(j < 3, -terms(0), jnp.where(j < 6, 1.0, 0.0)).astype(BF16)
    qa = jnp.concatenate([_aug_q(q * LOGIT_SCALE), q_extra], axis=-1)
    ka = jnp.concatenate([_aug_k(k), k_extra], axis=-1)
    HP, HG = FOX_HP, H // FOX_HP
    qt = qa.reshape(B_, n_qt, FOX_TQ, HG, HP, FOX_DK).transpose(0, 3, 1, 4, 5, 2).reshape(B_ * HG * n_qt, HP, FOX_DK, FOX_TQ)
    kk = ka.reshape(B_, n_kt, KV_TILE, HG, HP, FOX_DK).transpose(0, 3, 4, 1, 2, 5).reshape(B_ * HG, HP, n_kt, KV_TILE, FOX_DK)
    vt = v.astype(BF16).reshape(B_, n_kt, KV_TILE, HG, HP, Dh).transpose(0, 3, 4, 1, 5, 2).reshape(B_ * HG, HP, n_kt, Dh, KV_TILE)
    out = pl.pallas_call(
        _fox_kernel,
        grid=(B_ * HG, n_qt),
        in_specs=[pl.BlockSpec((None, HP, FOX_DK, FOX_TQ), lambda b, i: (b * n_qt + i, 0, 0, 0)),
                  pl.BlockSpec((None, HP, n_kt, KV_TILE, FOX_DK), lambda b, i: (b, 0, 0, 0, 0)),
                  pl.BlockSpec((None, HP, n_kt, Dh, KV_TILE), lambda b, i: (b, 0, 0, 0, 0))],
        out_specs=pl.BlockSpec((None, HP, Dh, FOX_TQ), lambda b, i: (b * n_qt + i, 0, 0, 0)),
        out_shape=jax.ShapeDtypeStruct((B_ * HG * n_qt, HP, Dh, FOX_TQ), F32),
        name="fox_attention",
    )(qt, kk, vt)
    return out.reshape(B_, HG, n_qt, HP, Dh, FOX_TQ)


NSA_TQ = 128
NSA_N = NSA_HPG * NSA_TQ
N_CMP_PAD = 256
N_SLC = 64
NSA_DK = 3 * HEAD_DIM
NSA_KV = 512
WIN_TILES = WINDOW // NSA_TQ + 1
WIN_KEYS = WIN_TILES * NSA_TQ


def _nsa_kernel(qt_ref, kc_ref, vct_ref, ks_ref, vst_ref, kw_ref, vwt_ref, gate_ref, ovl_ref, o_ref):
    qi = pl.program_id(1)
    q0 = qi * NSA_TQ
    qt = qt_ref[...]
    lane = lax.broadcasted_iota(jnp.int32, (1, NSA_N), 1)
    qpos = q0 + (lane & (NSA_TQ - 1))

    hn = NSA_N // 2
    qts = (qt[:, :hn], qt[:, hn:])
    qpos_h = qpos[:, :hn]

    w0 = jnp.maximum(qi - WINDOW // NSA_TQ, 0)
    k_w = jnp.concatenate([kw_ref[w0 + i] for i in range(WIN_TILES)], axis=0)
    vt_w = jnp.concatenate([vwt_ref[w0 + i] for i in range(WIN_TILES)], axis=1)
    s = jnp.dot(kc_ref[...], qt, preferred_element_type=F32)
    s_w = [jnp.dot(k_w, qts[h], preferred_element_type=F32) for h in range(2)]

    cmp_end = lax.broadcasted_iota(jnp.int32, (N_CMP_PAD, NSA_N), 0) * CMP_STRIDE + (CMP_LEN - 1)
    mc = cmp_end <= qpos
    m = jnp.max(jnp.where(mc, s, NEG_INF), axis=0, keepdims=True)
    p = jnp.where(mc, jnp.exp2(s - m), 0.0)
    l = jnp.sum(p, axis=0, keepdims=True)
    pc = p * jnp.where(l > 0.0, 1.0 / l, 0.0)
    o_cmp = jnp.dot(vct_ref[...], pc.astype(BF16), preferred_element_type=F32)
    pcs = (pc[:, 0:NSA_TQ] + pc[:, NSA_TQ:2 * NSA_TQ]) + (pc[:, 2 * NSA_TQ:3 * NSA_TQ] + pc[:, 3 * NSA_TQ:4 * NSA_TQ])
    pcs_hi = pcs.astype(BF16)
    pcs_lo = (pcs - pcs_hi.astype(F32)).astype(BF16)
    ovl = ovl_ref[...]
    imp = jnp.dot(ovl, pcs_hi, preferred_element_type=F32) + jnp.dot(ovl, pcs_lo, preferred_element_type=F32)

    rel = qpos_h - (w0 * NSA_TQ + lax.broadcasted_iota(jnp.int32, (WIN_KEYS, hn), 0))
    in_win = (rel >= 0) & (rel < WINDOW)
    outs_w = []
    for h in range(2):
        m = jnp.max(jnp.where(in_win, s_w[h], NEG_INF), axis=0, keepdims=True)
        p = jnp.where(in_win, jnp.exp2(s_w[h] - m), 0.0)
        l = jnp.sum(p, axis=0, keepdims=True)
        outs_w.append(jnp.dot(vt_w, p.astype(BF16), preferred_element_type=F32) / l)
    acc_w = jnp.concatenate(outs_w, axis=1)

    blk = lax.broadcasted_iota(jnp.int32, (N_SLC, NSA_TQ), 0)
    q_blk = (q0 + lax.broadcasted_iota(jnp.int32, (N_SLC, NSA_TQ), 1)) >> 6
    forced = (blk == 0) | (blk == q_blk) | (blk == q_blk - 1)
    score = jnp.where(forced, FORCED_SCORE, jnp.where(blk <= q_blk, imp, -1.0))
    rank = jnp.zeros((N_SLC, NSA_TQ), jnp.int32)
    for i in range(N_SLC):
        row = score[i:i + 1, :]
        beats = (row > score) | ((row == score) & (blk > i))
        rank = rank + beats.astype(jnp.int32)
    bias = jnp.where(rank < SLC_TOP, 0.0, NEG_INF).astype(BF16)
    bias = jnp.concatenate([bias] * (NSA_HPG // 2), axis=1)
    qts_s = [jnp.concatenate([qh, bias], axis=0) for qh in qts]
    krow = lax.broadcasted_iota(jnp.int32, (NSA_KV, hn), 0)

    n_full = q0 // NSA_KV
    causal = (n_full * NSA_KV + krow) <= qpos_h
    slc = _causal_flash(lambda kt: [ks_ref[kt]] * 2, lambda kt: [vst_ref[kt]] * 2, qts_s, n_full, causal, hn)
    acc_s = jnp.concatenate([slc[0][2] / slc[0][1], slc[1][2] / slc[1][1]], axis=1)

    g = jax.nn.sigmoid(gate_ref[...])
    o_ref[...] = g[0:1] * o_cmp + g[1:2] * acc_s + g[2:3] * acc_w


def nsa_attention(q, k_cmp, v_cmp, k_slc, v_slc, k_win, v_win, gate_logit):
    B_, S_, H, Dh = q.shape
    G = NSA_GROUPS
    n_qt = S_ // NSA_TQ
    n_cmp = k_cmp.shape[1]
    cs = np.arange(N_CMP_PAD)[None, :] * CMP_STRIDE
    ss = np.arange(N_SLC)[:, None] * SLC_LEN
    ov = np.clip(np.minimum(cs + CMP_LEN, ss + SLC_LEN) - np.maximum(cs, ss), 0, None) / CMP_LEN
    ov[:, n_cmp:] = 0.0
    ovl = jnp.asarray(ov, dtype=BF16)
    qs = _aug_q(q * LOGIT_SCALE).reshape(B_, n_qt, NSA_TQ, G, NSA_HPG, NSA_DK)
    qt = qs.transpose(0, 3, 1, 5, 4, 2).reshape(B_ * G * n_qt, NSA_DK, NSA_N)
    gt = gate_logit.astype(F32).reshape(B_, n_qt, NSA_TQ, G, NSA_HPG, N_NSA_BRANCH)
    gt = gt.transpose(0, 3, 1, 5, 4, 2).reshape(B_ * G * n_qt, N_NSA_BRANCH, NSA_N)
    tiles = lambda t, kv: t.reshape(B_, S_ // kv, kv, G, -1).transpose(0, 3, 1, 2, 4).reshape(B_ * G, S_ // kv, kv, t.shape[-1])
    keys = lambda t, kv: tiles(_aug_k(t), kv)
    block_onehot = (jnp.arange(S_)[:, None] // SLC_LEN == jnp.arange(N_SLC)[None, :]).astype(BF16)
    k_slc_aug = jnp.concatenate([_aug_k(k_slc), jnp.broadcast_to(block_onehot[None, :, None, :], (B_, S_, G, N_SLC))], axis=-1)
    vals = lambda t, kv: t.astype(BF16).reshape(B_, S_ // kv, kv, G, Dh).transpose(0, 3, 1, 4, 2).reshape(B_ * G, S_ // kv, Dh, kv)
    padc = ((0, 0), (0, N_CMP_PAD - n_cmp), (0, 0), (0, 0))
    kc = _aug_k(jnp.pad(k_cmp, padc)).transpose(0, 2, 1, 3).reshape(B_ * G, N_CMP_PAD, NSA_DK)
    vct = jnp.pad(v_cmp, padc).astype(BF16).transpose(0, 2, 3, 1).reshape(B_ * G, Dh, N_CMP_PAD)
    per_bg = lambda *blk: pl.BlockSpec((None,) + blk, lambda b, i: (b,) + (0,) * len(blk))
    per_tile = lambda *blk: pl.BlockSpec((None,) + blk, lambda b, i: (b * n_qt + i,) + (0,) * len(blk))
    out = pl.pallas_call(
        _nsa_kernel,
        grid=(B_ * G, n_qt),
        in_specs=[per_tile(NSA_DK, NSA_N),
                  per_bg(N_CMP_PAD, NSA_DK), per_bg(Dh, N_CMP_PAD),
                  per_bg(S_ // NSA_KV, NSA_KV, NSA_DK + N_SLC), per_bg(S_ // NSA_KV, Dh, NSA_KV),
                  per_bg(n_qt, NSA_TQ, NSA_DK), per_bg(n_qt, Dh, NSA_TQ),
                  per_tile(N_NSA_BRANCH, NSA_N),
                  pl.BlockSpec((N_SLC, N_CMP_PAD), lambda b, i: (0, 0))],
        out_specs=per_tile(Dh, NSA_N),
        out_shape=jax.ShapeDtypeStruct((B_ * G * n_qt, Dh, NSA_N), F32),
        name="nsa_attention",
    )(qt, kc, vct, tiles(k_slc_aug, NSA_KV), vals(v_slc, NSA_KV), keys(k_win, NSA_TQ), vals(v_win, NSA_TQ), gt, ovl)
    return out.reshape(B_, G, n_qt, Dh, NSA_N)


DENSE_TM = 256
SEC_MERGE = N_BRANCH * D_MODEL
SEC_SMALL = 256
IN_SECTIONS = (BRANCH_WIDTH,) * 4 + (KV_WIDTH,) * 6 + (SEC_MERGE, SEC_SMALL)


def _rms(x, g):
    return x * lax.rsqrt(jnp.mean(x * x, axis=-1, keepdims=True) + RMS_EPS) * g


def _row_spec(width):
    return pl.BlockSpec((DENSE_TM, width), lambda i: (i, 0))


def _whole_spec(shape):
    return pl.BlockSpec(shape, lambda i: (0,) * len(shape))


def _in_proj_kernel(x_ref, g_ref, w_ref, *o_refs):
    xn = _rms(x_ref[...], g_ref[...]).astype(BF16)
    off = 0
    for o_ref, width in zip(o_refs, IN_SECTIONS):
        o_ref[...] = jnp.dot(xn, w_ref[:, off:off + width], preferred_element_type=F32)
        off += width


def in_projection(x, g, w_in):
    t, d = x.shape
    fq, fk, fv, fl, nq, kc, vc, ksl, vsl, kwn, vwn, gl, ml = split_columns(w_in, IN_SPLITS)
    pad = lambda w: jnp.pad(w, ((0, 0), (0, 128 - w.shape[1])))
    w = jnp.concatenate([fq, fk, fv, nq, kc, vc, ksl, vsl, kwn, vwn, ml, pad(fl), pad(gl)], axis=1).astype(BF16)
    return pl.pallas_call(
        _in_proj_kernel,
        grid=(t // DENSE_TM,),
        in_specs=[_row_spec(d), _whole_spec((1, d)), _whole_spec(w.shape)],
        out_specs=[_row_spec(s) for s in IN_SECTIONS],
        out_shape=[jax.ShapeDtypeStruct((t, s), F32) for s in IN_SECTIONS],
        name="in_projection",
    )(x, g.reshape(1, d), w)


def _merge_kernel(x_ref, yf_ref, yn_ref, ml_ref, wb_ref, wo_ref, gq_ref, wq_ref, o_ref, hn_ref, q_ref):
    yf_t = yf_ref[...].reshape(BRANCH_WIDTH, DENSE_TM)
    yn_t = jnp.concatenate(
        [jnp.concatenate([yn_ref[g, j, :, h * NSA_TQ:(h + 1) * NSA_TQ]
                          for g in range(NSA_GROUPS) for h in range(NSA_HPG)], axis=0)
         for j in range(DENSE_TM // NSA_TQ)], axis=1)
    g = jax.nn.sigmoid(ml_ref[...])
    up_f = jnp.dot(yf_t.T.astype(BF16), wb_ref[0], preferred_element_type=F32)
    up_n = jnp.dot(yn_t.T.astype(BF16), wb_ref[1], preferred_element_type=F32)
    merged = g[:, :D_MODEL] * up_f + g[:, D_MODEL:] * up_n
    h = x_ref[...] + jnp.dot(merged.astype(BF16), wo_ref[...], preferred_element_type=F32)
    o_ref[...] = h
    hn = _rms(h, gq_ref[...])
    hn_ref[...] = hn
    q = jnp.dot(hn.astype(BF16), wq_ref[...], preferred_element_type=F32)
    dq = PEER_QDIM // 2
    for j in range(2 * PEER_HEADS):
        q_ref[j] = q[:, j * dq:(j + 1) * dq]


def merge_out(x, y_fox, y_nsa, merge_logit, w_branch, w_out, norm_ffn, wq):
    t, d = x.shape
    hp, dq = 2 * PEER_HEADS, PEER_QDIM // 2
    assert FOX_TQ == DENSE_TM and DENSE_TM % NSA_TQ == 0
    tiles_per_seq = y_fox.shape[2]
    nsa_per_tile = DENSE_TM // NSA_TQ
    yf_spec = pl.BlockSpec((None,) + y_fox.shape[1:2] + (None,) + y_fox.shape[3:],
                           lambda i: (i // tiles_per_seq, 0, i % tiles_per_seq, 0, 0, 0))
    yn_spec = pl.BlockSpec((None, NSA_GROUPS, nsa_per_tile, HEAD_DIM, NSA_N),
                           lambda i: (i // tiles_per_seq, 0, i % tiles_per_seq, 0, 0))
    return pl.pallas_call(
        _merge_kernel,
        grid=(t // DENSE_TM,),
        in_specs=[_row_spec(d), yf_spec, yn_spec, _row_spec(SEC_MERGE),
                  _whole_spec(w_branch.shape), _whole_spec(w_out.shape), _whole_spec((1, d)), _whole_spec(wq.shape)],
        out_specs=[_row_spec(d), _row_spec(d), pl.BlockSpec((hp, DENSE_TM, dq), lambda i: (0, i, 0))],
        out_shape=[jax.ShapeDtypeStruct((t, d), F32), jax.ShapeDtypeStruct((t, d), F32),
                   jax.ShapeDtypeStruct((hp, t, dq), F32)],
        name="merge_out",
    )(x, y_fox, y_nsa, merge_logit, w_branch.astype(BF16), w_out.astype(BF16), norm_ffn.reshape(1, d),
      wq.astype(BF16))


N_CMP_ROWS = 256


def _compress_kernel(r_ref, pos_ref, w1_ref, w2_ref, o_ref):
    r = r_ref[...]
    nxt = jnp.concatenate([r[1:], jnp.zeros((1, r.shape[1]), F32)], axis=0)
    half = r.shape[1]
    hid = (jnp.dot((r + pos_ref[0:1, :]).astype(BF16), w1_ref[:half], preferred_element_type=F32)
           + jnp.dot((nxt + pos_ref[1:2, :]).astype(BF16), w1_ref[half:], preferred_element_type=F32))
    o_ref[...] = jnp.dot(_gelu(hid).astype(BF16), w2_ref[...], preferred_element_type=F32)


def compress_kv_pallas(kv, cmp_pos, w1, w2):
    _, B_, S_, G, Dh = kv.shape
    n_cmp = (S_ - CMP_LEN) // CMP_STRIDE + 1
    width = CMP_STRIDE * Dh
    rows = kv.transpose(0, 1, 3, 2, 4).reshape(2, B_ * G, S_ // CMP_STRIDE, width)
    out = pl.pallas_call(
        _compress_kernel,
        grid=(2, B_ * G),
        in_specs=[pl.BlockSpec((None, None, N_CMP_ROWS, width), lambda j, b: (j, b, 0, 0)),
                  pl.BlockSpec((None, 2, width), lambda j, b: (j, 0, 0)),
                  pl.BlockSpec((None, CMP_LEN * Dh, CMP_HIDDEN), lambda j, b: (j, 0, 0)),
                  pl.BlockSpec((None, CMP_HIDDEN, Dh), lambda j, b: (j, 0, 0))],
        out_specs=pl.BlockSpec((None, None, N_CMP_ROWS, Dh), lambda j, b: (j, b, 0, 0)),
        out_shape=jax.ShapeDtypeStruct((2, B_ * G, N_CMP_ROWS, Dh), F32),
        name="compress_kv",
    )(rows, cmp_pos.reshape(2, 2, width), w1.astype(BF16), w2.astype(BF16))
    return out[:, :, :n_cmp].reshape(2, B_, G, n_cmp, Dh).transpose(0, 1, 3, 2, 4)


def hybrid_mixer(x, norm_g, w_in, fox_f_bias, cmp_pos, cmp_w1, cmp_w2, w_branch, w_out, norm_ffn, wq):
    B_, S_, D = x.shape
    xf = x.reshape(B_ * S_, D)
    secs = in_projection(xf, norm_g, w_in)
    heads = lambda t, n: t.reshape(B_, S_, n, HEAD_DIM)
    fq, fk, fv, nq = secs[:4]
    kc, vc, ksl, vsl, kwn, vwn = (heads(t, NSA_GROUPS) for t in secs[4:10])
    merge_logit, small = secs[10:]
    f_logit = small[:, :FOX_HEADS].reshape(B_, S_, FOX_HEADS)
    gate_logit = small[:, 128:128 + NSA_HEADS * N_NSA_BRANCH].reshape(B_, S_, NSA_HEADS, N_NSA_BRANCH)
    log_f = jax.nn.log_sigmoid(f_logit + fox_f_bias.astype(F32))
    y_fox = fox_attention(heads(fq, FOX_HEADS), heads(fk, FOX_HEADS), heads(fv, FOX_HEADS), log_f)
    cos, sin = rope_tables(jnp.arange(S_, dtype=F32))
    q_nsa = apply_partial_rope(heads(nq, NSA_HEADS), cos, sin)
    k_slc = apply_partial_rope(ksl, cos, sin)
    k_win = apply_partial_rope(kwn, cos, sin)
    cmp = compress_kv_pallas(jnp.stack([kc, vc]), cmp_pos, cmp_w1, cmp_w2)
    n_cmp = cmp.shape[2]
    cmp_end = jnp.arange(n_cmp, dtype=F32) * CMP_STRIDE + (CMP_LEN - 1)
    k_cmp = apply_partial_rope(cmp[0], *rope_tables(cmp_end))
    y_nsa = nsa_attention(q_nsa, k_cmp, cmp[1], k_slc, vsl, k_win, vwn, gate_logit)
    return merge_out(xf, y_fox, y_nsa, merge_logit, w_branch, w_out, norm_ffn, wq)


def peer_ffn(h, hn, q, subkeys, u, v, norm_final):
    idx2, w = peer_topk(q, subkeys)
    coef = peer_u(idx2, hn, w, pack_table(u))
    return peer_v(idx2, coef, pack_table(v), h, norm_final)


def kernel(x, norm_mix, w_in, fox_f_bias, nsa_cmp_pos, nsa_cmp_w1, nsa_cmp_w2, w_branch, w_out,
           norm_ffn, peer_wq, peer_subkeys, peer_u, peer_v, norm_final):
    B_, S_, D = x.shape
    assert norm_mix.shape[0] == 1, "single-layer trunk"
    h, hn, q = hybrid_mixer(x, norm_mix[0], w_in[0], fox_f_bias[0], nsa_cmp_pos[0], nsa_cmp_w1[0], nsa_cmp_w2[0],
                            w_branch[0], w_out[0], norm_ffn[0], peer_wq[0])
    out = peer_ffn(h, hn, q, peer_subkeys[0], peer_u[0], peer_v[0], norm_final)
    return out.reshape(B_, S_, D)
```

```python
import jax
import jax.numpy as jnp
import numpy as np
from jax import lax
from jax.experimental import pallas as pl
from jax.experimental.pallas import tpu as pltpu

D_MODEL = 1024
HEAD_DIM = 64
FOX_HEADS = 8
NSA_HEADS = 8
NSA_GROUPS = 2
NSA_HPG = NSA_HEADS // NSA_GROUPS
BRANCH_WIDTH = 512
N_BRANCH = 2
N_NSA_BRANCH = 3
ROPE_DIM = HEAD_DIM // 4
ROPE_THETA = 500000.0
CMP_LEN = 32
CMP_STRIDE = 16
CMP_HIDDEN = 2 * HEAD_DIM
SLC_LEN = 64
SLC_TOP = 16
WINDOW = 512
FORCED_SCORE = 1e9
NEG_INF = -1e30
PEER_HEADS = 8
N_KEYS = 128
PEER_QDIM = 256
PEER_TOPK = 16
RMS_EPS = 1e-6
KV_WIDTH = NSA_GROUPS * HEAD_DIM
IN_SPLITS = (BRANCH_WIDTH, BRANCH_WIDTH, BRANCH_WIDTH, FOX_HEADS, BRANCH_WIDTH,
             KV_WIDTH, KV_WIDTH, KV_WIDTH, KV_WIDTH, KV_WIDTH, KV_WIDTH,
             NSA_HEADS * N_NSA_BRANCH, N_BRANCH * D_MODEL)


BF16 = jnp.bfloat16
F32 = jnp.float32

PEER_SLOTS = PEER_HEADS * PEER_TOPK
PEER_TT = 128
PEER_CHUNK_SLOTS = 32
PEER_U_UNROLL = 8
PEER_V_UNROLL = 8
ROW_SUB = D_MODEL // 2 // 128
VMEM_LIMIT_PEER = N_KEYS * N_KEYS * D_MODEL * 2 + 16 * 1024 * 1024


def pack_table(tab):
    n, d = tab.shape
    return pl.pallas_call(
        _pack_kernel,
        grid=(n // PACK_TM,),
        in_specs=[pl.BlockSpec((PACK_TM, d), lambda i: (i, 0))],
        out_specs=pl.BlockSpec((PACK_TM * ROW_SUB, 128), lambda i: (i, 0)),
        out_shape=jax.ShapeDtypeStruct((n * ROW_SUB, 128), jnp.uint32),
        name="pack_table",
    )(tab)


PACK_TM = 256


def _pack_kernel(x_ref, o_ref):
    bits = lax.bitcast_convert_type(x_ref[...].astype(jnp.bfloat16).astype(jnp.float32), jnp.uint32)
    for r in range(ROW_SUB):
        even = bits[:, (2 * r) * 128:(2 * r + 1) * 128]
        odd = bits[:, (2 * r + 1) * 128:(2 * r + 2) * 128]
        o_ref[pl.ds(r, PACK_TM, stride=ROW_SUB), :] = (even >> 16) | (odd & jnp.uint32(0xFFFF0000))


def _expert_slab(tab_ref, row):
    return tab_ref[pl.ds(pl.multiple_of(row, ROW_SUB), ROW_SUB), :]


def _unpack(w):
    lo = lax.bitcast_convert_type(w << 16, jnp.float32)
    hi = lax.bitcast_convert_type(w & jnp.uint32(0xFFFF0000), jnp.float32)
    return lo, hi


def _gelu(x):
    return 0.5 * x * (1.0 + lax.erf(x * (2.0 ** -0.5)))


def _peer_u_kernel(idx_a_ref, idx_b_ref, x_ref, w_ref, tab_ref, out_ref):
    lane = lax.broadcasted_iota(jnp.int32, (2 * ROW_SUB, 128), 1)
    lower = lax.broadcasted_iota(jnp.int32, (2 * ROW_SUB, 128), 0) >= ROW_SUB
    pair_lane = jnp.where(lower, lane - PEER_SLOTS // 2, lane)

    def token(t):
        row = x_ref[pl.ds(t, 1), :]
        chunks = [row[:, c * 128:(c + 1) * 128] for c in range(2 * ROW_SUB)]
        xl = jnp.concatenate(chunks[0::2] * 2, axis=0)
        xh = jnp.concatenate(chunks[1::2] * 2, axis=0)
        acc = jnp.zeros((2 * ROW_SUB, 128), jnp.float32)
        base = t * (PEER_SLOTS // 2)
        for s in range(PEER_SLOTS // 2):
            w = jnp.concatenate([_expert_slab(tab_ref, idx_a_ref[base + s]),
                                 _expert_slab(tab_ref, idx_b_ref[base + s])], axis=0)
            lo, hi = _unpack(w)
            part = jnp.sum(lo * xl + hi * xh, axis=1, keepdims=True)
            acc = jnp.where(pair_lane == s, part, acc)
        out_ref[pl.ds(t, 1), :] = w_ref[pl.ds(t, 1), :] * _gelu(jnp.sum(acc, axis=0, keepdims=True))

    def trip(i, carry):
        for j in range(PEER_U_UNROLL):
            token(i * PEER_U_UNROLL + j)
        return carry

    lax.fori_loop(0, PEER_TT // PEER_U_UNROLL, trip, 0)


N_CHUNK = 2 * ROW_SUB


def _peer_v_kernel(idx_a_ref, idx_b_ref, coef_ref, expand_ref, h_ref, g_ref, tab_ref, out_ref,
                   ce_hi_ref, ce_lo_ref):
    half_slots = PEER_SLOTS // 2
    width = PEER_SLOTS * N_CHUNK
    coef = coef_ref[...]
    c_hi = _trunc_bf16(coef)
    expand = expand_ref[...]
    ce_hi_ref[...] = jnp.dot(c_hi.astype(BF16), expand, preferred_element_type=F32)
    ce_lo_ref[...] = jnp.dot((coef - c_hi).astype(BF16), expand, preferred_element_type=F32)
    own_chunk = ((lax.broadcasted_iota(jnp.int32, (N_CHUNK, width), 1) & (N_CHUNK - 1))
                 == lax.broadcasted_iota(jnp.int32, (N_CHUNK, width), 0))

    def token(t):
        base = t * half_slots
        slabs = [jnp.concatenate([_expert_slab(tab_ref, idx_a_ref[base + s]),
                                  _expert_slab(tab_ref, idx_b_ref[base + s])], axis=0) for s in range(half_slots)]
        gathered = pltpu.bitcast(jnp.concatenate(slabs, axis=0), BF16)
        sel = lambda ref: jnp.where(own_chunk, ref[pl.ds(t, 1), :], 0.0).astype(BF16)
        c = jnp.concatenate([sel(ce_hi_ref), sel(ce_lo_ref)], axis=0)
        o = jnp.dot(c, gathered, preferred_element_type=F32)
        o = o[:N_CHUNK] + o[N_CHUNK:]
        out_ref[pl.ds(t, 1), :] = jnp.concatenate([o[r:r + 1, :] for r in range(N_CHUNK)], axis=1)

    def trip(i, carry):
        for j in range(PEER_V_UNROLL):
            token(i * PEER_V_UNROLL + j)
        return carry

    lax.fori_loop(0, PEER_TT // PEER_V_UNROLL, trip, 0)
    out_ref[...] = _rms(h_ref[...] + out_ref[...], g_ref[...])


def _smem_spec():
    return pl.BlockSpec((PEER_TT * PEER_SLOTS // 2,), lambda i: (i,), memory_space=pltpu.SMEM)


def _slot_halves(a):
    return a[:, :PEER_SLOTS // 2].reshape(-1), a[:, PEER_SLOTS // 2:].reshape(-1)


def _table_spec(rows):
    return pl.BlockSpec((rows, 128), lambda i: (0, 0), pipeline_mode=pl.Buffered(1))


def peer_u(idx, x, w, tab):
    t = x.shape[0]
    return pl.pallas_call(
        _peer_u_kernel,
        grid=(t // PEER_TT,),
        in_specs=[_smem_spec(), _smem_spec(), pl.BlockSpec((PEER_TT, D_MODEL), lambda i: (i, 0)),
                  pl.BlockSpec((PEER_TT, PEER_SLOTS), lambda i: (i, 0)), _table_spec(tab.shape[0])],
        out_specs=pl.BlockSpec((PEER_TT, PEER_SLOTS), lambda i: (i, 0)),
        out_shape=jax.ShapeDtypeStruct((t, PEER_SLOTS), jnp.float32),
        compiler_params=pltpu.CompilerParams(vmem_limit_bytes=VMEM_LIMIT_PEER),
        name="peer_u",
    )(*_slot_halves(idx), x, w, tab)


def peer_v(idx, coef, tab, h, g):
    t = idx.shape[0]
    width = PEER_SLOTS * N_CHUNK
    half = PEER_SLOTS // 2
    col_slot = np.arange(width) // N_CHUNK
    col_slot = (col_slot // 2) + half * (col_slot % 2)
    expand = jnp.asarray(col_slot[None, :] == np.arange(PEER_SLOTS)[:, None], dtype=BF16)
    return pl.pallas_call(
        _peer_v_kernel,
        grid=(t // PEER_TT,),
        in_specs=[_smem_spec(), _smem_spec(), pl.BlockSpec((PEER_TT, PEER_SLOTS), lambda i: (i, 0)),
                  pl.BlockSpec((PEER_SLOTS, width), lambda i: (0, 0)),
                  pl.BlockSpec((PEER_TT, D_MODEL), lambda i: (i, 0)), pl.BlockSpec((1, D_MODEL), lambda i: (0, 0)),
                  _table_spec(tab.shape[0])],
        out_specs=pl.BlockSpec((PEER_TT, D_MODEL), lambda i: (i, 0)),
        out_shape=jax.ShapeDtypeStruct((t, D_MODEL), jnp.float32),
        scratch_shapes=[pltpu.VMEM((PEER_TT, width), F32), pltpu.VMEM((PEER_TT, width), F32)],
        compiler_params=pltpu.CompilerParams(vmem_limit_bytes=VMEM_LIMIT_PEER),
        name="peer_v",
    )(*_slot_halves(idx), coef, expand, h, g.reshape(1, D_MODEL), tab)


PK_TT = 128
PK_A_BLOCKS = ((0, 16), (1, 8), (2, 5), (3, 4))
PK_B_BLOCKS = ((0, 4, 16), (1, 4, 8), (2, 4, 5))
NEG_HUGE = -3.0e38
PK_HEADS_PER_TRIP = 8


def _extract_top(problems, flat, n):
    vals = [[] for _ in problems]
    picked = [[] for _ in problems]
    cur = [v for v, _ in problems]
    for _ in range(n):
        for i, (_, payload) in enumerate(problems):
            m = jnp.max(cur[i], axis=0, keepdims=True)
            pos = jnp.min(jnp.where(cur[i] == m, flat, 1e9), axis=0, keepdims=True)
            hit = flat == pos
            picked[i].append(pos if payload is None else jnp.sum(jnp.where(hit, payload, 0.0), axis=0, keepdims=True))
            cur[i] = jnp.where(hit, NEG_HUGE, cur[i])
            vals[i].append(m)
    return [jnp.concatenate(v, axis=0) for v in vals], [jnp.concatenate(p, axis=0) for p in picked]


def _candidate_grid(first, second, combine):
    up8 = lambda n: -(-n // 8) * 8
    return jnp.concatenate([combine(first[a:a + 1], second[0:up8(nb)]) for a, nb in PK_A_BLOCKS]
                           + [combine(first[0:up8(a1)], second[b:b + 1]) for b, _, a1 in PK_B_BLOCKS], axis=0)


def _candidate_valid_and_flat():
    up8 = lambda n: -(-n // 8) * 8
    valid, flat = [], []
    for a, nb in PK_A_BLOCKS:
        r = lax.broadcasted_iota(jnp.int32, (up8(nb), PK_TT), 0)
        valid.append(r < nb)
        flat.append(a * PEER_TOPK + r)
    for b, a0, a1 in PK_B_BLOCKS:
        r = lax.broadcasted_iota(jnp.int32, (up8(a1), PK_TT), 0)
        valid.append((r >= a0) & (r < a1))
        flat.append(r * PEER_TOPK + b)
    valid = jnp.concatenate(valid, axis=0)
    return valid, jnp.where(valid, jnp.concatenate(flat, axis=0).astype(F32), 1e9)


def _peer_topk_kernel(q_ref, ka_ref, eid_ref, w_ref):
    key_row = lax.broadcasted_iota(jnp.int32, (N_KEYS, PK_TT), 0).astype(F32)
    valid, flat = _candidate_valid_and_flat()

    def heads(i, carry):
        scores = []
        for j in range(2 * PK_HEADS_PER_TRIP):
            q = q_ref[2 * PK_HEADS_PER_TRIP * i + j]
            q_hi = _trunc_bf16(q)
            qa = jnp.concatenate([q_hi.astype(BF16), (q - q_hi).astype(BF16), q_hi.astype(BF16)], axis=1)
            scores.append(lax.dot_general(ka_ref[2 * PK_HEADS_PER_TRIP * i + j], qa, (((1,), (1,)), ((), ())),
                                          preferred_element_type=F32))
        sub_s, sub_i = _extract_top([(s, None) for s in scores], key_row, PEER_TOPK)
        cands = []
        for j in range(PK_HEADS_PER_TRIP):
            s1, s2, i1, i2 = sub_s[2 * j], sub_s[2 * j + 1], sub_i[2 * j], sub_i[2 * j + 1]
            cands.append((jnp.where(valid, _candidate_grid(s1, s2, lambda x, y: x + y), NEG_HUGE),
                          _candidate_grid(i1, i2, lambda x, y: x * N_KEYS + y)))
        top_s, top_e = _extract_top(cands, flat, PEER_TOPK)
        for j in range(PK_HEADS_PER_TRIP):
            e = jnp.exp(top_s[j] - top_s[j][0:1])
            w_ref[PK_HEADS_PER_TRIP * i + j] = e / jnp.sum(e, axis=0, keepdims=True)
            eid_ref[PK_HEADS_PER_TRIP * i + j] = top_e[j].astype(jnp.int32) * ROW_SUB
        return carry

    lax.fori_loop(0, PEER_HEADS // PK_HEADS_PER_TRIP, heads, 0)


def peer_topk(qt, subkeys):
    hp, t, d = qt.shape
    k_hi, k_lo = _split2(subkeys.reshape(hp, N_KEYS, d))
    ka = jnp.concatenate([k_hi, k_hi, k_lo], axis=2)
    eid, w = pl.pallas_call(
        _peer_topk_kernel,
        grid=(t // PK_TT,),
        in_specs=[pl.BlockSpec((hp, PK_TT, d), lambda i: (0, i, 0)),
                  pl.BlockSpec((hp, N_KEYS, 3 * d), lambda i: (0, 0, 0))],
        out_specs=[pl.BlockSpec((PEER_HEADS, PEER_TOPK, PK_TT), lambda i: (0, 0, i)),
                   pl.BlockSpec((PEER_HEADS, PEER_TOPK, PK_TT), lambda i: (0, 0, i))],
        out_shape=[jax.ShapeDtypeStruct((PEER_HEADS, PEER_TOPK, t), jnp.int32),
                   jax.ShapeDtypeStruct((PEER_HEADS, PEER_TOPK, t), F32)],
        name="peer_topk",
    )(qt, ka)
    to_slots = lambda a: a.transpose(2, 0, 1).reshape(t, PEER_SLOTS)
    return to_slots(eid), to_slots(w)


def split_columns(t, sizes):
    offs = np.cumsum((0,) + tuple(sizes))
    return [t[..., int(a):int(b)] for a, b in zip(offs[:-1], offs[1:])]


def rope_tables(pos):
    inv = jnp.power(ROPE_THETA, -jnp.arange(0, ROPE_DIM, 2, dtype=jnp.float32) / ROPE_DIM)
    ang = pos[:, None] * inv[None, :]
    return jnp.cos(ang), jnp.sin(ang)


def apply_partial_rope(x, cos, sin):
    half = ROPE_DIM // 2
    xr = x[..., :ROPE_DIM].astype(jnp.float32)
    x1, x2 = xr[..., :half], xr[..., half:]
    c, s = cos[:, None, :], sin[:, None, :]
    rot = jnp.concatenate([x1 * c - x2 * s, x1 * s + x2 * c], axis=-1)
    return jnp.concatenate([rot.astype(x.dtype), x[..., ROPE_DIM:]], axis=-1)


LOG2_E = 1.4426950408889634
LOGIT_SCALE = HEAD_DIM ** -0.5 * LOG2_E


def _logits(ks, qts):
    return [jnp.dot(k, qt, preferred_element_type=F32) for k, qt in zip(ks, qts)]


def _flash_update(ss, vts, mask, carries):
    stats, ps = [], []
    for s, (m, l, _) in zip(ss, carries):
        s_vis = s if mask is None else jnp.where(mask, s, NEG_INF)
        m_new = jnp.maximum(m, jnp.max(s_vis, axis=0, keepdims=True))
        alpha = jnp.exp2(m - m_new)
        p = jnp.exp2(s - m_new)
        if mask is not None:
            p = jnp.where(mask, p, 0.0)
        stats.append((m_new, alpha, alpha * l + jnp.sum(p, axis=0, keepdims=True)))
        ps.append(p.astype(BF16))
    return tuple((m_new, l, alpha * acc + jnp.dot(vt, p, preferred_element_type=F32))
                 for (m_new, alpha, l), p, vt, (_, _, acc) in zip(stats, ps, vts, carries))


def _causal_flash(key_tile, value_tile, qts, n_full, last_mask, n):
    def body(kt, carries):
        return _flash_update(_logits(key_tile(kt), qts), value_tile(kt), None, carries)

    carries = lax.fori_loop(0, n_full, body, tuple(_flash_init(n) for _ in qts))
    return _flash_update(_logits(key_tile(n_full), qts), value_tile(n_full), last_mask, carries)


def _flash_init(n):
    return (jnp.full((1, n), NEG_INF, F32), jnp.zeros((1, n), F32), jnp.zeros((HEAD_DIM, n), F32))


FOX_TQ = 256
FOX_KV = 512
FOX_DK = 256


FOX_HP = 4


def _fox_kernel(qt_ref, k_ref, vt_ref, o_ref):
    qi = pl.program_id(1)
    q0 = qi * FOX_TQ
    qts = [qt_ref[h] for h in range(FOX_HP)]
    qpos = q0 + lax.broadcasted_iota(jnp.int32, (1, FOX_TQ), 1)
    krow = lax.broadcasted_iota(jnp.int32, (FOX_KV, FOX_TQ), 0)

    n_full = q0 // FOX_KV
    causal = (n_full * FOX_KV + krow) <= qpos
    carry = _causal_flash(lambda kt: [k_ref[h, kt] for h in range(FOX_HP)],
                          lambda kt: [vt_ref[h, kt] for h in range(FOX_HP)], qts, n_full, causal, FOX_TQ)
    for h in range(FOX_HP):
        _, l, acc = carry[h]
        o_ref[h] = acc / l


def _trunc_bf16(x):
    bits = lax.bitcast_convert_type(x, jnp.uint32) & jnp.uint32(0xFFFF0000)
    return lax.bitcast_convert_type(bits, F32)


def _split3(c):
    c1 = _trunc_bf16(c)
    r = c - c1
    c2 = _trunc_bf16(r)
    return c1.astype(BF16), c2.astype(BF16), (r - c2).astype(BF16)


def _split2(x):
    hi = _trunc_bf16(x)
    return hi.astype(BF16), (x - hi).astype(BF16)


def _aug_q(q):
    hi, lo = _split2(q)
    return jnp.concatenate([hi, hi, lo], axis=-1)


def _aug_k(k):
    hi, lo = _split2(k)
    return jnp.concatenate([hi, lo, hi], axis=-1)


def fox_attention(q, k, v, log_f):
    B_, S_, H, Dh = q.shape
    KV_TILE = FOX_KV
    n_qt, n_kt = S_ // FOX_TQ, S_ // KV_TILE
    c = jnp.cumsum(log_f, axis=1)
    c1, c2, c3 = _split3(c * LOG2_E)
    j = jnp.arange(FOX_DK - 3 * Dh)
    pick = lambda t: t[..., None].astype(F32)
    terms = lambda first: jnp.where(j == first, pick(c1), jnp.where(j == first + 1, pick(c2), pick(c3)))
    q_extra = jnp.where(j < 3, 1.0, jnp.where(j < 6, terms(3), 0.0)).astype(BF16)
    k_extra = jnp.where(j < 3, -terms(0), jnp.where(j < 6, 1.0, 0.0)).astype(BF16)
    qa = jnp.concatenate([_aug_q(q * LOGIT_SCALE), q_extra], axis=-1)
    ka = jnp.concatenate([_aug_k(k), k_extra], axis=-1)
    HP, HG = FOX_HP, H // FOX_HP
    qt = qa.reshape(B_, n_qt, FOX_TQ, HG, HP, FOX_DK).transpose(0, 3, 1, 4, 5, 2).reshape(B_ * HG * n_qt, HP, FOX_DK, FOX_TQ)
    kk = ka.reshape(B_, n_kt, KV_TILE, HG, HP, FOX_DK).transpose(0, 3, 4, 1, 2, 5).reshape(B_ * HG, HP, n_kt, KV_TILE, FOX_DK)
    vt = v.astype(BF16).reshape(B_, n_kt, KV_TILE, HG, HP, Dh).transpose(0, 3, 4, 1, 5, 2).reshape(B_ * HG, HP, n_kt, Dh, KV_TILE)
    out = pl.pallas_call(
        _fox_kernel,
        grid=(B_ * HG, n_qt),
        in_specs=[pl.BlockSpec((None, HP, FOX_DK, FOX_TQ), lambda b, i: (b * n_qt + i, 0, 0, 0)),
                  pl.BlockSpec((None, HP, n_kt, KV_TILE, FOX_DK), lambda b, i: (b, 0, 0, 0, 0)),
                  pl.BlockSpec((None, HP, n_kt, Dh, KV_TILE), lambda b, i: (b, 0, 0, 0, 0))],
        out_specs=pl.BlockSpec((None, HP, Dh, FOX_TQ), lambda b, i: (b * n_qt + i, 0, 0, 0)),
        out_shape=jax.ShapeDtypeStruct((B_ * HG * n_qt, HP, Dh, FOX_TQ), F32),
        name="fox_attention",
    )(qt, kk, vt)
    return out.reshape(B_, HG, n_qt, HP, Dh, FOX_TQ)


NSA_TQ = 128
NSA_N = NSA_HPG * NSA_TQ
N_CMP_PAD = 256
N_SLC = 64
NSA_DK = 3 * HEAD_DIM
NSA_KV = 512
WIN_TILES = WINDOW // NSA_TQ + 1
WIN_KEYS = WIN_TILES * NSA_TQ


def _nsa_kernel(qt_ref, kc_ref, vct_ref, ks_ref, vst_ref, kw_ref, vwt_ref, gate_ref, ovl_ref, o_ref):
    qi = pl.program_id(1)
    q0 = qi * NSA_TQ
    qt = jnp.concatenate([qt_ref[h] for h in range(NSA_HPG)], axis=1)
    lane = lax.broadcasted_iota(jnp.int32, (1, NSA_N), 1)
    qpos = q0 + (lane & (NSA_TQ - 1))

    hn = NSA_N // 2
    qts = (qt[:, :hn], qt[:, hn:])
    qpos_h = qpos[:, :hn]

    w0 = jnp.maximum(qi - WINDOW // NSA_TQ, 0)
    k_w = jnp.concatenate([kw_ref[w0 + i] for i in range(WIN_TILES)], axis=0)
    vt_w = jnp.concatenate([vwt_ref[w0 + i] for i in range(WIN_TILES)], axis=1)
    s = jnp.dot(kc_ref[...], qt, preferred_element_type=F32)
    s_w = [jnp.dot(k_w, qts[h], preferred_element_type=F32) for h in range(2)]

    cmp_end = lax.broadcasted_iota(jnp.int32, (N_CMP_PAD, NSA_N), 0) * CMP_STRIDE + (CMP_LEN - 1)
    mc = cmp_end <= qpos
    m = jnp.max(jnp.where(mc, s, NEG_INF), axis=0, keepdims=True)
    p = jnp.where(mc, jnp.exp2(s - m), 0.0)
    l = jnp.sum(p, axis=0, keepdims=True)
    pc = p * jnp.where(l > 0.0, 1.0 / l, 0.0)
    o_cmp = jnp.dot(vct_ref[...], pc.astype(BF16), preferred_element_type=F32)
    pcs = (pc[:, 0:NSA_TQ] + pc[:, NSA_TQ:2 * NSA_TQ]) + (pc[:, 2 * NSA_TQ:3 * NSA_TQ] + pc[:, 3 * NSA_TQ:4 * NSA_TQ])
    pcs_hi = pcs.astype(BF16)
    pcs_lo = (pcs - pcs_hi.astype(F32)).astype(BF16)
    ovl = ovl_ref[...]
    imp = jnp.dot(ovl, pcs_hi, preferred_element_type=F32) + jnp.dot(ovl, pcs_lo, preferred_element_type=F32)

    rel = qpos_h - (w0 * NSA_TQ + lax.broadcasted_iota(jnp.int32, (WIN_KEYS, hn), 0))
    in_win = (rel >= 0) & (rel < WINDOW)
    outs_w = []
    for h in range(2):
        m = jnp.max(jnp.where(in_win, s_w[h], NEG_INF), axis=0, keepdims=True)
        p = jnp.where(in_win, jnp.exp2(s_w[h] - m), 0.0)
        l = jnp.sum(p, axis=0, keepdims=True)
        outs_w.append(jnp.dot(vt_w, p.astype(BF16), preferred_element_type=F32) / l)
    acc_w = jnp.concatenate(outs_w, axis=1)

    blk = lax.broadcasted_iota(jnp.int32, (N_SLC, NSA_TQ), 0)
    q_blk = (q0 + lax.broadcasted_iota(jnp.int32, (N_SLC, NSA_TQ), 1)) >> 6
    forced = (blk == 0) | (blk == q_blk) | (blk == q_blk - 1)
    score = jnp.where(forced, FORCED_SCORE, jnp.where(blk <= q_blk, imp, -1.0))
    rank = jnp.zeros((N_SLC, NSA_TQ), jnp.int32)
    for i in range(N_SLC):
        row = score[i:i + 1, :]
        beats = (row > score) | ((row == score) & (blk > i))
        rank = rank + beats.astype(jnp.int32)
    bias = jnp.where(rank < SLC_TOP, 0.0, NEG_INF).astype(BF16)
    bias = jnp.concatenate([bias] * (NSA_HPG // 2), axis=1)
    qts_s = [jnp.concatenate([qh, bias], axis=0) for qh in qts]
    krow = lax.broadcasted_iota(jnp.int32, (NSA_KV, hn), 0)

    n_full = q0 // NSA_KV
    causal = (n_full * NSA_KV + krow) <= qpos_h
    slc = _causal_flash(lambda kt: [ks_ref[kt]] * 2, lambda kt: [vst_ref[kt]] * 2, qts_s, n_full, causal, hn)
    acc_s = jnp.concatenate([slc[0][2] / slc[0][1], slc[1][2] / slc[1][1]], axis=1)

    g = jax.nn.sigmoid(gate_ref[...])
    o_ref[...] = g[0:1] * o_cmp + g[1:2] * acc_s + g[2:3] * acc_w


def nsa_attention(q, k_cmp, v_cmp, k_slc, v_slc, k_win, v_win, gate_logit):
    B_, S_, H, Dh = q.shape
    G = NSA_GROUPS
    n_qt = S_ // NSA_TQ
    n_cmp = k_cmp.shape[1]
    cs = np.arange(N_CMP_PAD)[None, :] * CMP_STRIDE
    ss = np.arange(N_SLC)[:, None] * SLC_LEN
    ov = np.clip(np.minimum(cs + CMP_LEN, ss + SLC_LEN) - np.maximum(cs, ss), 0, None) / CMP_LEN
    ov[:, n_cmp:] = 0.0
    ovl = jnp.asarray(ov, dtype=BF16)
    qs = _aug_q(q * LOGIT_SCALE).reshape(B_, n_qt, NSA_TQ, G, NSA_HPG, NSA_DK)
    qt = qs.transpose(0, 3, 1, 4, 5, 2).reshape(B_ * G * n_qt, NSA_HPG, NSA_DK, NSA_TQ)
    gt = gate_logit.astype(F32).reshape(B_, n_qt, NSA_TQ, G, NSA_HPG, N_NSA_BRANCH)
    gt = gt.transpose(0, 3, 1, 5, 4, 2).reshape(B_ * G * n_qt, N_NSA_BRANCH, NSA_N)
    tiles = lambda t, kv: t.reshape(B_, S_ // kv, kv, G, -1).transpose(0, 3, 1, 2, 4).reshape(B_ * G, S_ // kv, kv, t.shape[-1])
    keys = lambda t, kv: tiles(_aug_k(t), kv)
    block_onehot = (jnp.arange(S_)[:, None] // SLC_LEN == jnp.arange(N_SLC)[None, :]).astype(BF16)
    k_slc_aug = jnp.concatenate([_aug_k(k_slc), jnp.broadcast_to(block_onehot[None, :, None, :], (B_, S_, G, N_SLC))], axis=-1)
    vals = lambda t, kv: t.astype(BF16).reshape(B_, S_ // kv, kv, G, Dh).transpose(0, 3, 1, 4, 2).reshape(B_ * G, S_ // kv, Dh, kv)
    padc = ((0, 0), (0, N_CMP_PAD - n_cmp), (0, 0), (0, 0))
    kc = _aug_k(jnp.pad(k_cmp, padc)).transpose(0, 2, 1, 3).reshape(B_ * G, N_CMP_PAD, NSA_DK)
    vct = jnp.pad(v_cmp, padc).astype(BF16).transpose(0, 2, 3, 1).reshape(B_ * G, Dh, N_CMP_PAD)
    per_bg = lambda *blk: pl.BlockSpec((None,) + blk, lambda b, i: (b,) + (0,) * len(blk))
    per_tile = lambda *blk: pl.BlockSpec((None,) + blk, lambda b, i: (b * n_qt + i,) + (0,) * len(blk))
    out = pl.pallas_call(
        _nsa_kernel,
        grid=(B_ * G, n_qt),
        in_specs=[per_tile(NSA_HPG, NSA_DK, NSA_TQ),
                  per_bg(N_CMP_PAD, NSA_DK), per_bg(Dh, N_CMP_PAD),
                  per_bg(S_ // NSA_KV, NSA_KV, NSA_DK + N_SLC), per_bg(S_ // NSA_KV, Dh, NSA_KV),
                  per_bg(n_qt, NSA_TQ, NSA_DK), per_bg(n_qt, Dh, NSA_TQ),
                  per_tile(N_NSA_BRANCH, NSA_N),
                  pl.BlockSpec((N_SLC, N_CMP_PAD), lambda b, i: (0, 0))],
        out_specs=per_tile(Dh, NSA_N),
        out_shape=jax.ShapeDtypeStruct((B_ * G * n_qt, Dh, NSA_N), F32),
        name="nsa_attention",
    )(qt, kc, vct, tiles(k_slc_aug, NSA_KV), vals(v_slc, NSA_KV), keys(k_win, NSA_TQ), vals(v_win, NSA_TQ), gt, ovl)
    return out.reshape(B_, G, n_qt, Dh, NSA_N)


DENSE_TM = 256
SEC_MERGE = N_BRANCH * D_MODEL
SEC_SMALL = 256
IN_SECTIONS = (BRANCH_WIDTH,) * 4 + (KV_WIDTH,) * 6 + (SEC_MERGE, SEC_SMALL)


def _rms(x, g):
    return x * lax.rsqrt(jnp.mean(x * x, axis=-1, keepdims=True) + RMS_EPS) * g


def _row_spec(width):
    return pl.BlockSpec((DENSE_TM, width), lambda i: (i, 0))


def _whole_spec(shape):
    return pl.BlockSpec(shape, lambda i: (0,) * len(shape))


def _in_proj_kernel(x_ref, g_ref, w_ref, *o_refs):
    xn = _rms(x_ref[...], g_ref[...]).astype(BF16)
    off = 0
    for o_ref, width in zip(o_refs, IN_SECTIONS):
        o_ref[...] = jnp.dot(xn, w_ref[:, off:off + width], preferred_element_type=F32)
        off += width


def in_projection(x, g, w_in):
    t, d = x.shape
    fq, fk, fv, fl, nq, kc, vc, ksl, vsl, kwn, vwn, gl, ml = split_columns(w_in, IN_SPLITS)
    pad = lambda w: jnp.pad(w, ((0, 0), (0, 128 - w.shape[1])))
    w = jnp.concatenate([fq, fk, fv, nq, kc, vc, ksl, vsl, kwn, vwn, ml, pad(fl), pad(gl)], axis=1).astype(BF16)
    return pl.pallas_call(
        _in_proj_kernel,
        grid=(t // DENSE_TM,),
        in_specs=[_row_spec(d), _whole_spec((1, d)), _whole_spec(w.shape)],
        out_specs=[_row_spec(s) for s in IN_SECTIONS],
        out_shape=[jax.ShapeDtypeStruct((t, s), F32) for s in IN_SECTIONS],
        name="in_projection",
    )(x, g.reshape(1, d), w)


def _merge_kernel(x_ref, yf_ref, yn_ref, ml_ref, wb_ref, wo_ref, gq_ref, wq_ref, o_ref, hn_ref, q_ref):
    yf_t = yf_ref[...].reshape(BRANCH_WIDTH, DENSE_TM)
    yn_t = jnp.concatenate(
        [jnp.concatenate([yn_ref[g, j, :, h * NSA_TQ:(h + 1) * NSA_TQ]
                          for g in range(NSA_GROUPS) for h in range(NSA_HPG)], axis=0)
         for j in range(DENSE_TM // NSA_TQ)], axis=1)
    g = jax.nn.sigmoid(ml_ref[...])
    up_f = jnp.dot(yf_t.T.astype(BF16), wb_ref[0], preferred_element_type=F32)
    up_n = jnp.dot(yn_t.T.astype(BF16), wb_ref[1], preferred_element_type=F32)
    merged = g[:, :D_MODEL] * up_f + g[:, D_MODEL:] * up_n
    h = x_ref[...] + jnp.dot(merged.astype(BF16), wo_ref[...], preferred_element_type=F32)
    o_ref[...] = h
    hn = _rms(h, gq_ref[...])
    hn_ref[...] = hn
    q = jnp.dot(hn.astype(BF16), wq_ref[...], preferred_element_type=F32)
    dq = PEER_QDIM // 2
    for j in range(2 * PEER_HEADS):
        q_ref[j] = q[:, j * dq:(j + 1) * dq]


def merge_out(x, y_fox, y_nsa, merge_logit, w_branch, w_out, norm_ffn, wq):
    t, d = x.shape
    hp, dq = 2 * PEER_HEADS, PEER_QDIM // 2
    assert FOX_TQ == DENSE_TM and DENSE_TM % NSA_TQ == 0
    tiles_per_seq = y_fox.shape[2]
    nsa_per_tile = DENSE_TM // NSA_TQ
    yf_spec = pl.BlockSpec((None,) + y_fox.shape[1:2] + (None,) + y_fox.shape[3:],
                           lambda i: (i // tiles_per_seq, 0, i % tiles_per_seq, 0, 0, 0))
    yn_spec = pl.BlockSpec((None, NSA_GROUPS, nsa_per_tile, HEAD_DIM, NSA_N),
                           lambda i: (i // tiles_per_seq, 0, i % tiles_per_seq, 0, 0))
    return pl.pallas_call(
        _merge_kernel,
        grid=(t // DENSE_TM,),
        in_specs=[_row_spec(d), yf_spec, yn_spec, _row_spec(SEC_MERGE),
                  _whole_spec(w_branch.shape), _whole_spec(w_out.shape), _whole_spec((1, d)), _whole_spec(wq.shape)],
        out_specs=[_row_spec(d), _row_spec(d), pl.BlockSpec((hp, DENSE_TM, dq), lambda i: (0, i, 0))],
        out_shape=[jax.ShapeDtypeStruct((t, d), F32), jax.ShapeDtypeStruct((t, d), F32),
                   jax.ShapeDtypeStruct((hp, t, dq), F32)],
        name="merge_out",
    )(x, y_fox, y_nsa, merge_logit, w_branch.astype(BF16), w_out.astype(BF16), norm_ffn.reshape(1, d),
      wq.astype(BF16))


N_CMP_ROWS = 256


def _compress_kernel(r_ref, pos_ref, w1_ref, w2_ref, o_ref):
    r = r_ref[...]
    nxt = jnp.concatenate([r[1:], jnp.zeros((1, r.shape[1]), F32)], axis=0)
    half = r.shape[1]
    hid = (jnp.dot((r + pos_ref[0:1, :]).astype(BF16), w1_ref[:half], preferred_element_type=F32)
           + jnp.dot((nxt + pos_ref[1:2, :]).astype(BF16), w1_ref[half:], preferred_element_type=F32))
    o_ref[...] = jnp.dot(_gelu(hid).astype(BF16), w2_ref[...], preferred_element_type=F32)


def compress_kv_pallas(kv, cmp_pos, w1, w2):
    _, B_, S_, G, Dh = kv.shape
    n_cmp = (S_ - CMP_LEN) // CMP_STRIDE + 1
    width = CMP_STRIDE * Dh
    rows = kv.transpose(0, 1, 3, 2, 4).reshape(2, B_ * G, S_ // CMP_STRIDE, width)
    out = pl.pallas_call(
        _compress_kernel,
        grid=(2, B_ * G),
        in_specs=[pl.BlockSpec((None, None, N_CMP_ROWS, width), lambda j, b: (j, b, 0, 0)),
                  pl.BlockSpec((None, 2, width), lambda j, b: (j, 0, 0)),
                  pl.BlockSpec((None, CMP_LEN * Dh, CMP_HIDDEN), lambda j, b: (j, 0, 0)),
                  pl.BlockSpec((None, CMP_HIDDEN, Dh), lambda j, b: (j, 0, 0))],
        out_specs=pl.BlockSpec((None, None, N_CMP_ROWS, Dh), lambda j, b: (j, b, 0, 0)),
        out_shape=jax.ShapeDtypeStruct((2, B_ * G, N_CMP_ROWS, Dh), F32),
        name="compress_kv",
    )(rows, cmp_pos.reshape(2, 2, width), w1.astype(BF16), w2.astype(BF16))
    return out[:, :, :n_cmp].reshape(2, B_, G, n_cmp, Dh).transpose(0, 1, 3, 2, 4)


def hybrid_mixer(x, norm_g, w_in, fox_f_bias, cmp_pos, cmp_w1, cmp_w2, w_branch, w_out, norm_ffn, wq):
    B_, S_, D = x.shape
    xf = x.reshape(B_ * S_, D)
    secs = in_projection(xf, norm_g, w_in)
    heads = lambda t, n: t.reshape(B_, S_, n, HEAD_DIM)
    fq, fk, fv, nq = secs[:4]
    kc, vc, ksl, vsl, kwn, vwn = (heads(t, NSA_GROUPS) for t in secs[4:10])
    merge_logit, small = secs[10:]
    f_logit = small[:, :FOX_HEADS].reshape(B_, S_, FOX_HEADS)
    gate_logit = small[:, 128:128 + NSA_HEADS * N_NSA_BRANCH].reshape(B_, S_, NSA_HEADS, N_NSA_BRANCH)
    log_f = jax.nn.log_sigmoid(f_logit + fox_f_bias.astype(F32))
    y_fox = fox_attention(heads(fq, FOX_HEADS), heads(fk, FOX_HEADS), heads(fv, FOX_HEADS), log_f)
    cos, sin = rope_tables(jnp.arange(S_, dtype=F32))
    q_nsa = apply_partial_rope(heads(nq, NSA_HEADS), cos, sin)
    k_slc = apply_partial_rope(ksl, cos, sin)
    k_win = apply_partial_rope(kwn, cos, sin)
    cmp = compress_kv_pallas(jnp.stack([kc, vc]), cmp_pos, cmp_w1, cmp_w2)
    n_cmp = cmp.shape[2]
    cmp_end = jnp.arange(n_cmp, dtype=F32) * CMP_STRIDE + (CMP_LEN - 1)
    k_cmp = apply_partial_rope(cmp[0], *rope_tables(cmp_end))
    y_nsa = nsa_attention(q_nsa, k_cmp, cmp[1], k_slc, vsl, k_win, vwn, gate_logit)
    return merge_out(xf, y_fox, y_nsa, merge_logit, w_branch, w_out, norm_ffn, wq)


def peer_ffn(h, hn, q, subkeys, u, v, norm_final):
    idx2, w = peer_topk(q, subkeys)
    coef = peer_u(idx2, hn, w, pack_table(u))
    return peer_v(idx2, coef, pack_table(v), h, norm_final)


def kernel(x, norm_mix, w_in, fox_f_bias, nsa_cmp_pos, nsa_cmp_w1, nsa_cmp_w2, w_branch, w_out,
           norm_ffn, peer_wq, peer_subkeys, peer_u, peer_v, norm_final):
    B_, S_, D = x.shape
    assert norm_mix.shape[0] == 1, "single-layer trunk"
    h, hn, q = hybrid_mixer(x, norm_mix[0], w_in[0], fox_f_bias[0], nsa_cmp_pos[0], nsa_cmp_w1[0], nsa_cmp_w2[0],
                            w_branch[0], w_out[0], norm_ffn[0], peer_wq[0])
    out = peer_ffn(h, hn, q, peer_subkeys[0], peer_u[0], peer_v[0], norm_final)
    return out.reshape(B_, S_, D)
```

```python
import jax
import jax.numpy as jnp
import numpy as np
from jax import lax
from jax.experimental import pallas as pl
from jax.experimental.pallas import tpu as pltpu

D_MODEL = 1024
HEAD_DIM = 64
FOX_HEADS = 8
NSA_HEADS = 8
NSA_GROUPS = 2
NSA_HPG = NSA_HEADS // NSA_GROUPS
BRANCH_WIDTH = 512
N_BRANCH = 2
N_NSA_BRANCH = 3
ROPE_DIM = HEAD_DIM // 4
ROPE_THETA = 500000.0
CMP_LEN = 32
CMP_STRIDE = 16
CMP_HIDDEN = 2 * HEAD_DIM
SLC_LEN = 64
SLC_TOP = 16
WINDOW = 512
FORCED_SCORE = 1e9
NEG_INF = -1e30
PEER_HEADS = 8
N_KEYS = 128
PEER_QDIM = 256
PEER_TOPK = 16
RMS_EPS = 1e-6
KV_WIDTH = NSA_GROUPS * HEAD_DIM
IN_SPLITS = (BRANCH_WIDTH, BRANCH_WIDTH, BRANCH_WIDTH, FOX_HEADS, BRANCH_WIDTH,
             KV_WIDTH, KV_WIDTH, KV_WIDTH, KV_WIDTH, KV_WIDTH, KV_WIDTH,
             NSA_HEADS * N_NSA_BRANCH, N_BRANCH * D_MODEL)


BF16 = jnp.bfloat16
F32 = jnp.float32

PEER_SLOTS = PEER_HEADS * PEER_TOPK
PEER_TT = 128
PEER_CHUNK_SLOTS = 32
PEER_U_UNROLL = 8
PEER_V_UNROLL = 8
ROW_SUB = D_MODEL // 2 // 128
VMEM_LIMIT_PEER = N_KEYS * N_KEYS * D_MODEL * 2 + 16 * 1024 * 1024


def pack_table(tab):
    n, d = tab.shape
    return pl.pallas_call(
        _pack_kernel,
        grid=(n // PACK_TM,),
        in_specs=[pl.BlockSpec((PACK_TM, d), lambda i: (i, 0))],
        out_specs=pl.BlockSpec((PACK_TM * ROW_SUB, 128), lambda i: (i, 0)),
        out_shape=jax.ShapeDtypeStruct((n * ROW_SUB, 128), jnp.uint32),
        name="pack_table",
    )(tab)


PACK_TM = 256


def _pack_kernel(x_ref, o_ref):
    bits = lax.bitcast_convert_type(x_ref[...].astype(jnp.bfloat16).astype(jnp.float32), jnp.uint32)
    for r in range(ROW_SUB):
        even = bits[:, (2 * r) * 128:(2 * r + 1) * 128]
        odd = bits[:, (2 * r + 1) * 128:(2 * r + 2) * 128]
        o_ref[pl.ds(r, PACK_TM, stride=ROW_SUB), :] = (even >> 16) | (odd & jnp.uint32(0xFFFF0000))


def _expert_slab(tab_ref, row):
    return tab_ref[pl.ds(pl.multiple_of(row, ROW_SUB), ROW_SUB), :]


def _unpack(w):
    lo = lax.bitcast_convert_type(w << 16, jnp.float32)
    hi = lax.bitcast_convert_type(w & jnp.uint32(0xFFFF0000), jnp.float32)
    return lo, hi


def _gelu(x):
    return 0.5 * x * (1.0 + lax.erf(x * (2.0 ** -0.5)))


def _peer_u_kernel(idx_a_ref, idx_b_ref, x_ref, w_ref, tab_ref, out_ref):
    lane = lax.broadcasted_iota(jnp.int32, (2 * ROW_SUB, 128), 1)
    lower = lax.broadcasted_iota(jnp.int32, (2 * ROW_SUB, 128), 0) >= ROW_SUB
    pair_lane = jnp.where(lower, lane - PEER_SLOTS // 2, lane)

    def token(t):
        row = x_ref[pl.ds(t, 1), :]
        chunks = [row[:, c * 128:(c + 1) * 128] for c in range(2 * ROW_SUB)]
        xl = jnp.concatenate(chunks[0::2] * 2, axis=0)
        xh = jnp.concatenate(chunks[1::2] * 2, axis=0)
        acc = jnp.zeros((2 * ROW_SUB, 128), jnp.float32)
        base = t * (PEER_SLOTS // 2)
        for s in range(PEER_SLOTS // 2):
            w = jnp.concatenate([_expert_slab(tab_ref, idx_a_ref[base + s]),
                                 _expert_slab(tab_ref, idx_b_ref[base + s])], axis=0)
            lo, hi = _unpack(w)
            part = jnp.sum(lo * xl + hi * xh, axis=1, keepdims=True)
            acc = jnp.where(pair_lane == s, part, acc)
        out_ref[pl.ds(t, 1), :] = w_ref[pl.ds(t, 1), :] * _gelu(jnp.sum(acc, axis=0, keepdims=True))

    def trip(i, carry):
        for j in range(PEER_U_UNROLL):
            token(i * PEER_U_UNROLL + j)
        return carry

    lax.fori_loop(0, PEER_TT // PEER_U_UNROLL, trip, 0)


N_CHUNK = 2 * ROW_SUB


def _peer_v_kernel(idx_a_ref, idx_b_ref, coef_ref, expand_ref, h_ref, g_ref, tab_ref, out_ref,
                   ce_hi_ref, ce_lo_ref):
    half_slots = PEER_SLOTS // 2
    width = PEER_SLOTS * N_CHUNK
    coef = coef_ref[...]
    c_hi = _trunc_bf16(coef)
    expand = expand_ref[...]
    ce_hi_ref[...] = jnp.dot(c_hi.astype(BF16), expand, preferred_element_type=F32)
    ce_lo_ref[...] = jnp.dot((coef - c_hi).astype(BF16), expand, preferred_element_type=F32)
    own_chunk = ((lax.broadcasted_iota(jnp.int32, (N_CHUNK, width), 1) & (N_CHUNK - 1))
                 == lax.broadcasted_iota(jnp.int32, (N_CHUNK, width), 0))

    def token(t):
        base = t * half_slots
        slabs = [jnp.concatenate([_expert_slab(tab_ref, idx_a_ref[base + s]),
                                  _expert_slab(tab_ref, idx_b_ref[base + s])], axis=0) for s in range(half_slots)]
        gathered = pltpu.bitcast(jnp.concatenate(slabs, axis=0), BF16)
        sel = lambda ref: jnp.where(own_chunk, ref[pl.ds(t, 1), :], 0.0).astype(BF16)
        c = jnp.concatenate([sel(ce_hi_ref), sel(ce_lo_ref)], axis=0)
        o = jnp.dot(c, gathered, preferred_element_type=F32)
        o = o[:N_CHUNK] + o[N_CHUNK:]
        out_ref[pl.ds(t, 1), :] = jnp.concatenate([o[r:r + 1, :] for r in range(N_CHUNK)], axis=1)

    def trip(i, carry):
        for j in range(PEER_V_UNROLL):
            token(i * PEER_V_UNROLL + j)
        return carry

    lax.fori_loop(0, PEER_TT // PEER_V_UNROLL, trip, 0)
    out_ref[...] = _rms(h_ref[...] + out_ref[...], g_ref[...])


def _smem_spec():
    return pl.BlockSpec((PEER_TT * PEER_SLOTS // 2,), lambda i: (i,), memory_space=pltpu.SMEM)


def _slot_halves(a):
    return a[:, :PEER_SLOTS // 2].reshape(-1), a[:, PEER_SLOTS // 2:].reshape(-1)


def _table_spec(rows):
    return pl.BlockSpec((rows, 128), lambda i: (0, 0), pipeline_mode=pl.Buffered(1))


def peer_u(idx, x, w, tab):
    t = x.shape[0]
    return pl.pallas_call(
        _peer_u_kernel,
        grid=(t // PEER_TT,),
        in_specs=[_smem_spec(), _smem_spec(), pl.BlockSpec((PEER_TT, D_MODEL), lambda i: (i, 0)),
                  pl.BlockSpec((PEER_TT, PEER_SLOTS), lambda i: (i, 0)), _table_spec(tab.shape[0])],
        out_specs=pl.BlockSpec((PEER_TT, PEER_SLOTS), lambda i: (i, 0)),
        out_shape=jax.ShapeDtypeStruct((t, PEER_SLOTS), jnp.float32),
        compiler_params=pltpu.CompilerParams(vmem_limit_bytes=VMEM_LIMIT_PEER),
        name="peer_u",
    )(*_slot_halves(idx), x, w, tab)


def peer_v(idx, coef, tab, h, g):
    t = idx.shape[0]
    width = PEER_SLOTS * N_CHUNK
    half = PEER_SLOTS // 2
    col_slot = np.arange(width) // N_CHUNK
    col_slot = (col_slot // 2) + half * (col_slot % 2)
    expand = jnp.asarray(col_slot[None, :] == np.arange(PEER_SLOTS)[:, None], dtype=BF16)
    return pl.pallas_call(
        _peer_v_kernel,
        grid=(t // PEER_TT,),
        in_specs=[_smem_spec(), _smem_spec(), pl.BlockSpec((PEER_TT, PEER_SLOTS), lambda i: (i, 0)),
                  pl.BlockSpec((PEER_SLOTS, width), lambda i: (0, 0)),
                  pl.BlockSpec((PEER_TT, D_MODEL), lambda i: (i, 0)), pl.BlockSpec((1, D_MODEL), lambda i: (0, 0)),
                  _table_spec(tab.shape[0])],
        out_specs=pl.BlockSpec((PEER_TT, D_MODEL), lambda i: (i, 0)),
        out_shape=jax.ShapeDtypeStruct((t, D_MODEL), jnp.float32),
        scratch_shapes=[pltpu.VMEM((PEER_TT, width), F32), pltpu.VMEM((PEER_TT, width), F32)],
        compiler_params=pltpu.CompilerParams(vmem_limit_bytes=VMEM_LIMIT_PEER),
        name="peer_v",
    )(*_slot_halves(idx), coef, expand, h, g.reshape(1, D_MODEL), tab)


PK_TT = 128
PK_A_BLOCKS = ((0, 16), (1, 8), (2, 5), (3, 4))
PK_B_BLOCKS = ((0, 4, 16), (1, 4, 8), (2, 4, 5))
NEG_HUGE = -3.0e38
PK_HEADS_PER_TRIP = 8


def _extract_top(problems, flat, n):
    vals = [[] for _ in problems]
    picked = [[] for _ in problems]
    cur = [v for v, _ in problems]
    for _ in range(n):
        for i, (_, payload) in enumerate(problems):
            m = jnp.max(cur[i], axis=0, keepdims=True)
            pos = jnp.min(jnp.where(cur[i] == m, flat, 1e9), axis=0, keepdims=True)
            hit = flat == pos
            picked[i].append(pos if payload is None else jnp.sum(jnp.where(hit, payload, 0.0), axis=0, keepdims=True))
            cur[i] = jnp.where(hit, NEG_HUGE, cur[i])
            vals[i].append(m)
    return [jnp.concatenate(v, axis=0) for v in vals], [jnp.concatenate(p, axis=0) for p in picked]


def _candidate_grid(first, second, combine):
    up8 = lambda n: -(-n // 8) * 8
    return jnp.concatenate([combine(first[a:a + 1], second[0:up8(nb)]) for a, nb in PK_A_BLOCKS]
                           + [combine(first[0:up8(a1)], second[b:b + 1]) for b, _, a1 in PK_B_BLOCKS], axis=0)


def _candidate_valid_and_flat():
    up8 = lambda n: -(-n // 8) * 8
    valid, flat = [], []
    for a, nb in PK_A_BLOCKS:
        r = lax.broadcasted_iota(jnp.int32, (up8(nb), PK_TT), 0)
        valid.append(r < nb)
        flat.append(a * PEER_TOPK + r)
    for b, a0, a1 in PK_B_BLOCKS:
        r = lax.broadcasted_iota(jnp.int32, (up8(a1), PK_TT), 0)
        valid.append((r >= a0) & (r < a1))
        flat.append(r * PEER_TOPK + b)
    valid = jnp.concatenate(valid, axis=0)
    return valid, jnp.where(valid, jnp.concatenate(flat, axis=0).astype(F32), 1e9)


def _peer_topk_kernel(q_ref, ka_ref, eid_ref, w_ref):
    key_row = lax.broadcasted_iota(jnp.int32, (N_KEYS, PK_TT), 0).astype(F32)
    valid, flat = _candidate_valid_and_flat()

    def heads(i, carry):
        scores = []
        for j in range(2 * PK_HEADS_PER_TRIP):
            q = q_ref[2 * PK_HEADS_PER_TRIP * i + j]
            q_hi = _trunc_bf16(q)
            qa = jnp.concatenate([q_hi.astype(BF16), (q - q_hi).astype(BF16), q_hi.astype(BF16)], axis=1)
            scores.append(lax.dot_general(ka_ref[2 * PK_HEADS_PER_TRIP * i + j], qa, (((1,), (1,)), ((), ())),
                                          preferred_element_type=F32))
        sub_s, sub_i = _extract_top([(s, None) for s in scores], key_row, PEER_TOPK)
        cands = []
        for j in range(PK_HEADS_PER_TRIP):
            s1, s2, i1, i2 = sub_s[2 * j], sub_s[2 * j + 1], sub_i[2 * j], sub_i[2 * j + 1]
            cands.append((jnp.where(valid, _candidate_grid(s1, s2, lambda x, y: x + y), NEG_HUGE),
                          _candidate_grid(i1, i2, lambda x, y: x * N_KEYS + y)))
        top_s, top_e = _extract_top(cands, flat, PEER_TOPK)
        for j in range(PK_HEADS_PER_TRIP):
            e = jnp.exp(top_s[j] - top_s[j][0:1])
            w_ref[PK_HEADS_PER_TRIP * i + j] = e / jnp.sum(e, axis=0, keepdims=True)
            eid_ref[PK_HEADS_PER_TRIP * i + j] = top_e[j].astype(jnp.int32) * ROW_SUB
        return carry

    lax.fori_loop(0, PEER_HEADS // PK_HEADS_PER_TRIP, heads, 0)


def peer_topk(qt, subkeys):
    hp, t, d = qt.shape
    k_hi, k_lo = _split2(subkeys.reshape(hp, N_KEYS, d))
    ka = jnp.concatenate([k_hi, k_hi, k_lo], axis=2)
    eid, w = pl.pallas_call(
        _peer_topk_kernel,
        grid=(t // PK_TT,),
        in_specs=[pl.BlockSpec((hp, PK_TT, d), lambda i: (0, i, 0)),
                  pl.BlockSpec((hp, N_KEYS, 3 * d), lambda i: (0, 0, 0))],
        out_specs=[pl.BlockSpec((PEER_HEADS, PEER_TOPK, PK_TT), lambda i: (0, 0, i)),
                   pl.BlockSpec((PEER_HEADS, PEER_TOPK, PK_TT), lambda i: (0, 0, i))],
        out_shape=[jax.ShapeDtypeStruct((PEER_HEADS, PEER_TOPK, t), jnp.int32),
                   jax.ShapeDtypeStruct((PEER_HEADS, PEER_TOPK, t), F32)],
        name="peer_topk",
    )(qt, ka)
    to_slots = lambda a: a.transpose(2, 0, 1).reshape(t, PEER_SLOTS)
    return to_slots(eid), to_slots(w)


def split_columns(t, sizes):
    offs = np.cumsum((0,) + tuple(sizes))
    return [t[..., int(a):int(b)] for a, b in zip(offs[:-1], offs[1:])]


def rope_tables(pos):
    inv = jnp.power(ROPE_THETA, -jnp.arange(0, ROPE_DIM, 2, dtype=jnp.float32) / ROPE_DIM)
    ang = pos[:, None] * inv[None, :]
    return jnp.cos(ang), jnp.sin(ang)


def apply_partial_rope(x, cos, sin):
    half = ROPE_DIM // 2
    xr = x[..., :ROPE_DIM].astype(jnp.float32)
    x1, x2 = xr[..., :half], xr[..., half:]
    c, s = cos[:, None, :], sin[:, None, :]
    rot = jnp.concatenate([x1 * c - x2 * s, x1 * s + x2 * c], axis=-1)
    return jnp.concatenate([rot.astype(x.dtype), x[..., ROPE_DIM:]], axis=-1)


LOG2_E = 1.4426950408889634
LOGIT_SCALE = HEAD_DIM ** -0.5 * LOG2_E


def _logits(ks, qts):
    return [jnp.dot(k, qt, preferred_element_type=F32) for k, qt in zip(ks, qts)]


def _flash_update(ss, vts, mask, carries):
    stats, ps = [], []
    for s, (m, l, _) in zip(ss, carries):
        s_vis = s if mask is None else jnp.where(mask, s, NEG_INF)
        m_new = jnp.maximum(m, jnp.max(s_vis, axis=0, keepdims=True))
        alpha = jnp.exp2(m - m_new)
        p = jnp.exp2(s - m_new)
        if mask is not None:
            p = jnp.where(mask, p, 0.0)
        stats.append((m_new, alpha, alpha * l + jnp.sum(p, axis=0, keepdims=True)))
        ps.append(p.astype(BF16))
    return tuple((m_new, l, alpha * acc + jnp.dot(vt, p, preferred_element_type=F32))
                 for (m_new, alpha, l), p, vt, (_, _, acc) in zip(stats, ps, vts, carries))


def _causal_flash(key_tile, value_tile, qts, n_full, last_mask, n):
    def body(kt, carries):
        return _flash_update(_logits(key_tile(kt), qts), value_tile(kt), None, carries)

    carries = lax.fori_loop(0, n_full, body, tuple(_flash_init(n) for _ in qts))
    return _flash_update(_logits(key_tile(n_full), qts), value_tile(n_full), last_mask, carries)


def _flash_init(n):
    return (jnp.full((1, n), NEG_INF, F32), jnp.zeros((1, n), F32), jnp.zeros((HEAD_DIM, n), F32))


FOX_TQ = 256
FOX_KV = 512
FOX_DK = 256


FOX_HP = 4


def _fox_kernel(qt_ref, k_ref, vt_ref, o_ref):
    qi = pl.program_id(1)
    q0 = qi * FOX_TQ
    qts = [qt_ref[h] for h in range(FOX_HP)]
    qpos = q0 + lax.broadcasted_iota(jnp.int32, (1, FOX_TQ), 1)
    krow = lax.broadcasted_iota(jnp.int32, (FOX_KV, FOX_TQ), 0)

    n_full = q0 // FOX_KV
    causal = (n_full * FOX_KV + krow) <= qpos
    carry = _causal_flash(lambda kt: [k_ref[h, kt] for h in range(FOX_HP)],
                          lambda kt: [vt_ref[h, kt] for h in range(FOX_HP)], qts, n_full, causal, FOX_TQ)
    for h in range(FOX_HP):
        _, l, acc = carry[h]
        o_ref[h] = acc / l


def _trunc_bf16(x):
    bits = lax.bitcast_convert_type(x, jnp.uint32) & jnp.uint32(0xFFFF0000)
    return lax.bitcast_convert_type(bits, F32)


def _split3(c):
    c1 = _trunc_bf16(c)
    r = c - c1
    c2 = _trunc_bf16(r)
    return c1.astype(BF16), c2.astype(BF16), (r - c2).astype(BF16)


def _split2(x):
    hi = _trunc_bf16(x)
    return hi.astype(BF16), (x - hi).astype(BF16)


def _aug_q(q):
    hi, lo = _split2(q)
    return jnp.concatenate([hi, hi, lo], axis=-1)


def _aug_k(k):
    hi, lo = _split2(k)
    return jnp.concatenate([hi, lo, hi], axis=-1)


def fox_attention(q, k, v, log_f):
    B_, S_, H, Dh = q.shape
    KV_TILE = FOX_KV
    n_qt, n_kt = S_ // FOX_TQ, S_ // KV_TILE
    c = jnp.cumsum(log_f, axis=1)
    c1, c2, c3 = _split3(c * LOG2_E)
    j = jnp.arange(FOX_DK - 3 * Dh)
    pick = lambda t: t[..., None].astype(F32)
    terms = lambda first: jnp.where(j == first, pick(c1), jnp.where(j == first + 1, pick(c2), pick(c3)))
    q_extra = jnp.where(j < 3, 1.0, jnp.where(j < 6, terms(3), 0.0)).astype(BF16)
    k_extra = jnp.where(j < 3, -terms(0), jnp.where(j < 6, 1.0, 0.0)).astype(BF16)
    qa = jnp.concatenate([_aug_q(q * LOGIT_SCALE), q_extra], axis=-1)
    ka = jnp.concatenate([_aug_k(k), k_extra], axis=-1)
    HP, HG = FOX_HP, H // FOX_HP
    qt = qa.reshape(B_, n_qt, FOX_TQ, HG, HP, FOX_DK).transpose(0, 3, 1, 4, 5, 2).reshape(B_ * HG * n_qt, HP, FOX_DK, FOX_TQ)
    kk = ka.reshape(B_, n_kt, KV_TILE, HG, HP, FOX_DK).transpose(0, 3, 4, 1, 2, 5).reshape(B_ * HG, HP, n_kt, KV_TILE, FOX_DK)
    vt = v.astype(BF16).reshape(B_, n_kt, KV_TILE, HG, HP, Dh).transpose(0, 3, 4, 1, 5, 2).reshape(B_ * HG, HP, n_kt, Dh, KV_TILE)
    out = pl.pallas_call(
        _fox_kernel,
        grid=(B_ * HG, n_qt),
        in_specs=[pl.BlockSpec((None, HP, FOX_DK, FOX_TQ), lambda b, i: (b * n_qt + i, 0, 0, 0)),
                  pl.BlockSpec((None, HP, n_kt, KV_TILE, FOX_DK), lambda b, i: (b, 0, 0, 0, 0)),
                  pl.BlockSpec((None, HP, n_kt, Dh, KV_TILE), lambda b, i: (b, 0, 0, 0, 0))],
        out_specs=pl.BlockSpec((None, HP, Dh, FOX_TQ), lambda b, i: (b * n_qt + i, 0, 0, 0)),
        out_shape=jax.ShapeDtypeStruct((B_ * HG * n_qt, HP, Dh, FOX_TQ), F32),
        name="fox_attention",
    )(qt, kk, vt)
    return out.reshape(B_, HG, n_qt, HP, Dh, FOX_TQ)


NSA_TQ = 256
NSA_N = NSA_HPG * NSA_TQ
NSA_CH = 256
N_CMP_PAD = 256
N_SLC = 64
NSA_DK = 3 * HEAD_DIM
NSA_KV = 512
WIN_TILES = WINDOW // NSA_TQ + 1
WIN_KEYS = WIN_TILES * NSA_TQ


def _nsa_kernel(qt_ref, kc_ref, vct_ref, ks_ref, vst_ref, kw_ref, vwt_ref, gate_ref, ovl_ref, o_ref):
    qi = pl.program_id(1)
    q0 = qi * NSA_TQ
    qt = jnp.concatenate([qt_ref[h] for h in range(NSA_HPG)], axis=1)
    lane = lax.broadcasted_iota(jnp.int32, (1, NSA_N), 1)
    qpos = q0 + (lane & (NSA_TQ - 1))

    hn = NSA_CH
    n_ch = NSA_N // NSA_CH
    qts = [qt[:, c * hn:(c + 1) * hn] for c in range(n_ch)]
    qpos_h = qpos[:, :hn]

    w0 = jnp.maximum(qi - WINDOW // NSA_TQ, 0)
    k_w = jnp.concatenate([kw_ref[w0 + i] for i in range(WIN_TILES)], axis=0)
    vt_w = jnp.concatenate([vwt_ref[w0 + i] for i in range(WIN_TILES)], axis=1)
    s = jnp.dot(kc_ref[...], qt, preferred_element_type=F32)
    s_w = [jnp.dot(k_w, qts[c], preferred_element_type=F32) for c in range(n_ch)]

    cmp_end = lax.broadcasted_iota(jnp.int32, (N_CMP_PAD, NSA_N), 0) * CMP_STRIDE + (CMP_LEN - 1)
    mc = cmp_end <= qpos
    m = jnp.max(jnp.where(mc, s, NEG_INF), axis=0, keepdims=True)
    p = jnp.where(mc, jnp.exp2(s - m), 0.0)
    l = jnp.sum(p, axis=0, keepdims=True)
    pc = p * jnp.where(l > 0.0, 1.0 / l, 0.0)
    o_cmp = jnp.dot(vct_ref[...], pc.astype(BF16), preferred_element_type=F32)
    pcs = sum(pc[:, h * NSA_TQ:(h + 1) * NSA_TQ] for h in range(NSA_HPG))
    pcs_hi = pcs.astype(BF16)
    pcs_lo = (pcs - pcs_hi.astype(F32)).astype(BF16)
    ovl = ovl_ref[...]
    imp = jnp.dot(ovl, pcs_hi, preferred_element_type=F32) + jnp.dot(ovl, pcs_lo, preferred_element_type=F32)

    rel = qpos_h - (w0 * NSA_TQ + lax.broadcasted_iota(jnp.int32, (WIN_KEYS, hn), 0))
    in_win = (rel >= 0) & (rel < WINDOW)
    outs_w = []
    for c in range(n_ch):
        m = jnp.max(jnp.where(in_win, s_w[c], NEG_INF), axis=0, keepdims=True)
        p = jnp.where(in_win, jnp.exp2(s_w[c] - m), 0.0)
        l = jnp.sum(p, axis=0, keepdims=True)
        outs_w.append(jnp.dot(vt_w, p.astype(BF16), preferred_element_type=F32) / l)
    acc_w = jnp.concatenate(outs_w, axis=1)

    blk = lax.broadcasted_iota(jnp.int32, (N_SLC, NSA_TQ), 0)
    q_blk = (q0 + lax.broadcasted_iota(jnp.int32, (N_SLC, NSA_TQ), 1)) >> 6
    forced = (blk == 0) | (blk == q_blk) | (blk == q_blk - 1)
    score = jnp.where(forced, FORCED_SCORE, jnp.where(blk <= q_blk, imp, -1.0))
    rank = jnp.zeros((N_SLC, NSA_TQ), jnp.int32)
    for i in range(N_SLC):
        row = score[i:i + 1, :]
        beats = (row > score) | ((row == score) & (blk > i))
        rank = rank + beats.astype(jnp.int32)
    bias = jnp.where(rank < SLC_TOP, 0.0, NEG_INF).astype(BF16)
    bias = jnp.concatenate([bias] * (hn // NSA_TQ), axis=1)
    qts_s = [jnp.concatenate([qh, bias], axis=0) for qh in qts]
    krow = lax.broadcasted_iota(jnp.int32, (NSA_KV, hn), 0)

    n_full = q0 // NSA_KV
    causal = (n_full * NSA_KV + krow) <= qpos_h
    slc = _causal_flash(lambda kt: [ks_ref[kt]] * n_ch, lambda kt: [vst_ref[kt]] * n_ch, qts_s, n_full, causal, hn)
    acc_s = jnp.concatenate([acc / l for _, l, acc in slc], axis=1)

    g = jax.nn.sigmoid(gate_ref[...])
    o_ref[...] = g[0:1] * o_cmp + g[1:2] * acc_s + g[2:3] * acc_w


def nsa_attention(q, k_cmp, v_cmp, k_slc, v_slc, k_win, v_win, gate_logit):
    B_, S_, H, Dh = q.shape
    G = NSA_GROUPS
    n_qt = S_ // NSA_TQ
    n_cmp = k_cmp.shape[1]
    cs = np.arange(N_CMP_PAD)[None, :] * CMP_STRIDE
    ss = np.arange(N_SLC)[:, None] * SLC_LEN
    ov = np.clip(np.minimum(cs + CMP_LEN, ss + SLC_LEN) - np.maximum(cs, ss), 0, None) / CMP_LEN
    ov[:, n_cmp:] = 0.0
    ovl = jnp.asarray(ov, dtype=BF16)
    qs = _aug_q(q * LOGIT_SCALE).reshape(B_, n_qt, NSA_TQ, G, NSA_HPG, NSA_DK)
    qt = qs.transpose(0, 3, 1, 4, 5, 2).reshape(B_ * G * n_qt, NSA_HPG, NSA_DK, NSA_TQ)
    gt = gate_logit.astype(F32).reshape(B_, n_qt, NSA_TQ, G, NSA_HPG, N_NSA_BRANCH)
    gt = gt.transpose(0, 3, 1, 5, 4, 2).reshape(B_ * G * n_qt, N_NSA_BRANCH, NSA_N)
    tiles = lambda t, kv: t.reshape(B_, S_ // kv, kv, G, -1).transpose(0, 3, 1, 2, 4).reshape(B_ * G, S_ // kv, kv, t.shape[-1])
    keys = lambda t, kv: tiles(_aug_k(t), kv)
    block_onehot = (jnp.arange(S_)[:, None] // SLC_LEN == jnp.arange(N_SLC)[None, :]).astype(BF16)
    k_slc_aug = jnp.concatenate([_aug_k(k_slc), jnp.broadcast_to(block_onehot[None, :, None, :], (B_, S_, G, N_SLC))], axis=-1)
    vals = lambda t, kv: t.astype(BF16).reshape(B_, S_ // kv, kv, G, Dh).transpose(0, 3, 1, 4, 2).reshape(B_ * G, S_ // kv, Dh, kv)
    padc = ((0, 0), (0, N_CMP_PAD - n_cmp), (0, 0), (0, 0))
    kc = _aug_k(jnp.pad(k_cmp, padc)).transpose(0, 2, 1, 3).reshape(B_ * G, N_CMP_PAD, NSA_DK)
    vct = jnp.pad(v_cmp, padc).astype(BF16).transpose(0, 2, 3, 1).reshape(B_ * G, Dh, N_CMP_PAD)
    per_bg = lambda *blk: pl.BlockSpec((None,) + blk, lambda b, i: (b,) + (0,) * len(blk))
    per_tile = lambda *blk: pl.BlockSpec((None,) + blk, lambda b, i: (b * n_qt + i,) + (0,) * len(blk))
    out = pl.pallas_call(
        _nsa_kernel,
        grid=(B_ * G, n_qt),
        in_specs=[per_tile(NSA_HPG, NSA_DK, NSA_TQ),
                  per_bg(N_CMP_PAD, NSA_DK), per_bg(Dh, N_CMP_PAD),
                  per_bg(S_ // NSA_KV, NSA_KV, NSA_DK + N_SLC), per_bg(S_ // NSA_KV, Dh, NSA_KV),
                  per_bg(n_qt, NSA_TQ, NSA_DK), per_bg(n_qt, Dh, NSA_TQ),
                  per_tile(N_NSA_BRANCH, NSA_N),
                  pl.BlockSpec((N_SLC, N_CMP_PAD), lambda b, i: (0, 0))],
        out_specs=per_tile(Dh, NSA_N),
        out_shape=jax.ShapeDtypeStruct((B_ * G * n_qt, Dh, NSA_N), F32),
        name="nsa_attention",
    )(qt, kc, vct, tiles(k_slc_aug, NSA_KV), vals(v_slc, NSA_KV), keys(k_win, NSA_TQ), vals(v_win, NSA_TQ), gt, ovl)
    return out.reshape(B_, G, n_qt, Dh, NSA_N)


DENSE_TM = 256
SEC_MERGE = N_BRANCH * D_MODEL
SEC_SMALL = 256
IN_SECTIONS = (BRANCH_WIDTH,) * 4 + (KV_WIDTH,) * 6 + (SEC_MERGE, SEC_SMALL)


def _rms(x, g):
    return x * lax.rsqrt(jnp.mean(x * x, axis=-1, keepdims=True) + RMS_EPS) * g


def _row_spec(width):
    return pl.BlockSpec((DENSE_TM, width), lambda i: (i, 0))


def _whole_spec(shape):
    return pl.BlockSpec(shape, lambda i: (0,) * len(shape))


def _in_proj_kernel(x_ref, g_ref, w_ref, *o_refs):
    xn = _rms(x_ref[...], g_ref[...]).astype(BF16)
    off = 0
    for o_ref, width in zip(o_refs, IN_SECTIONS):
        o_ref[...] = jnp.dot(xn, w_ref[:, off:off + width], preferred_element_type=F32)
        off += width


def in_projection(x, g, w_in):
    t, d = x.shape
    fq, fk, fv, fl, nq, kc, vc, ksl, vsl, kwn, vwn, gl, ml = split_columns(w_in, IN_SPLITS)
    pad = lambda w: jnp.pad(w, ((0, 0), (0, 128 - w.shape[1])))
    w = jnp.concatenate([fq, fk, fv, nq, kc, vc, ksl, vsl, kwn, vwn, ml, pad(fl), pad(gl)], axis=1).astype(BF16)
    return pl.pallas_call(
        _in_proj_kernel,
        grid=(t // DENSE_TM,),
        in_specs=[_row_spec(d), _whole_spec((1, d)), _whole_spec(w.shape)],
        out_specs=[_row_spec(s) for s in IN_SECTIONS],
        out_shape=[jax.ShapeDtypeStruct((t, s), F32) for s in IN_SECTIONS],
        name="in_projection",
    )(x, g.reshape(1, d), w)


def _merge_kernel(x_ref, yf_ref, yn_ref, ml_ref, wb_ref, wo_ref, gq_ref, wq_ref, o_ref, hn_ref, q_ref):
    yf_t = yf_ref[...].reshape(BRANCH_WIDTH, DENSE_TM)
    yn_t = jnp.concatenate(
        [jnp.concatenate([yn_ref[g, j, :, h * NSA_TQ:(h + 1) * NSA_TQ]
                          for g in range(NSA_GROUPS) for h in range(NSA_HPG)], axis=0)
         for j in range(DENSE_TM // NSA_TQ)], axis=1)
    g = jax.nn.sigmoid(ml_ref[...])
    up_f = jnp.dot(yf_t.T.astype(BF16), wb_ref[0], preferred_element_type=F32)
    up_n = jnp.dot(yn_t.T.astype(BF16), wb_ref[1], preferred_element_type=F32)
    merged = g[:, :D_MODEL] * up_f + g[:, D_MODEL:] * up_n
    h = x_ref[...] + jnp.dot(merged.astype(BF16), wo_ref[...], preferred_element_type=F32)
    o_ref[...] = h
    hn = _rms(h, gq_ref[...])
    hn_ref[...] = hn
    q = jnp.dot(hn.astype(BF16), wq_ref[...], preferred_element_type=F32)
    dq = PEER_QDIM // 2
    for j in range(2 * PEER_HEADS):
        q_ref[j] = q[:, j * dq:(j + 1) * dq]


def merge_out(x, y_fox, y_nsa, merge_logit, w_branch, w_out, norm_ffn, wq):
    t, d = x.shape
    hp, dq = 2 * PEER_HEADS, PEER_QDIM // 2
    assert FOX_TQ == DENSE_TM and DENSE_TM % NSA_TQ == 0
    tiles_per_seq = y_fox.shape[2]
    nsa_per_tile = DENSE_TM // NSA_TQ
    yf_spec = pl.BlockSpec((None,) + y_fox.shape[1:2] + (None,) + y_fox.shape[3:],
                           lambda i: (i // tiles_per_seq, 0, i % tiles_per_seq, 0, 0, 0))
    yn_spec = pl.BlockSpec((None, NSA_GROUPS, nsa_per_tile, HEAD_DIM, NSA_N),
                           lambda i: (i // tiles_per_seq, 0, i % tiles_per_seq, 0, 0))
    return pl.pallas_call(
        _merge_kernel,
        grid=(t // DENSE_TM,),
        in_specs=[_row_spec(d), yf_spec, yn_spec, _row_spec(SEC_MERGE),
                  _whole_spec(w_branch.shape), _whole_spec(w_out.shape), _whole_spec((1, d)), _whole_spec(wq.shape)],
        out_specs=[_row_spec(d), _row_spec(d), pl.BlockSpec((hp, DENSE_TM, dq), lambda i: (0, i, 0))],
        out_shape=[jax.ShapeDtypeStruct((t, d), F32), jax.ShapeDtypeStruct((t, d), F32),
                   jax.ShapeDtypeStruct((hp, t, dq), F32)],
        name="merge_out",
    )(x, y_fox, y_nsa, merge_logit, w_branch.astype(BF16), w_out.astype(BF16), norm_ffn.reshape(1, d),
      wq.astype(BF16))


N_CMP_ROWS = 256


def _compress_kernel(r_ref, pos_ref, w1_ref, w2_ref, o_ref):
    r = r_ref[...]
    nxt = jnp.concatenate([r[1:], jnp.zeros((1, r.shape[1]), F32)], axis=0)
    half = r.shape[1]
    hid = (jnp.dot((r + pos_ref[0:1, :]).astype(BF16), w1_ref[:half], preferred_element_type=F32)
           + jnp.dot((nxt + pos_ref[1:2, :]).astype(BF16), w1_ref[half:], preferred_element_type=F32))
    o_ref[...] = jnp.dot(_gelu(hid).astype(BF16), w2_ref[...], preferred_element_type=F32)


def compress_kv_pallas(kv, cmp_pos, w1, w2):
    _, B_, S_, G, Dh = kv.shape
    n_cmp = (S_ - CMP_LEN) // CMP_STRIDE + 1
    width = CMP_STRIDE * Dh
    rows = kv.transpose(0, 1, 3, 2, 4).reshape(2, B_ * G, S_ // CMP_STRIDE, width)
    out = pl.pallas_call(
        _compress_kernel,
        grid=(2, B_ * G),
        in_specs=[pl.BlockSpec((None, None, N_CMP_ROWS, width), lambda j, b: (j, b, 0, 0)),
                  pl.BlockSpec((None, 2, width), lambda j, b: (j, 0, 0)),
                  pl.BlockSpec((None, CMP_LEN * Dh, CMP_HIDDEN), lambda j, b: (j, 0, 0)),
                  pl.BlockSpec((None, CMP_HIDDEN, Dh), lambda j, b: (j, 0, 0))],
        out_specs=pl.BlockSpec((None, None, N_CMP_ROWS, Dh), lambda j, b: (j, b, 0, 0)),
        out_shape=jax.ShapeDtypeStruct((2, B_ * G, N_CMP_ROWS, Dh), F32),
        name="compress_kv",
    )(rows, cmp_pos.reshape(2, 2, width), w1.astype(BF16), w2.astype(BF16))
    return out[:, :, :n_cmp].reshape(2, B_, G, n_cmp, Dh).transpose(0, 1, 3, 2, 4)


def hybrid_mixer(x, norm_g, w_in, fox_f_bias, cmp_pos, cmp_w1, cmp_w2, w_branch, w_out, norm_ffn, wq):
    B_, S_, D = x.shape
    xf = x.reshape(B_ * S_, D)
    secs = in_projection(xf, norm_g, w_in)
    heads = lambda t, n: t.reshape(B_, S_, n, HEAD_DIM)
    fq, fk, fv, nq = secs[:4]
    kc, vc, ksl, vsl, kwn, vwn = (heads(t, NSA_GROUPS) for t in secs[4:10])
    merge_logit, small = secs[10:]
    f_logit = small[:, :FOX_HEADS].reshape(B_, S_, FOX_HEADS)
    gate_logit = small[:, 128:128 + NSA_HEADS * N_NSA_BRANCH].reshape(B_, S_, NSA_HEADS, N_NSA_BRANCH)
    log_f = jax.nn.log_sigmoid(f_logit + fox_f_bias.astype(F32))
    y_fox = fox_attention(heads(fq, FOX_HEADS), heads(fk, FOX_HEADS), heads(fv, FOX_HEADS), log_f)
    cos, sin = rope_tables(jnp.arange(S_, dtype=F32))
    q_nsa = apply_partial_rope(heads(nq, NSA_HEADS), cos, sin)
    k_slc = apply_partial_rope(ksl, cos, sin)
    k_win = apply_partial_rope(kwn, cos, sin)
    cmp = compress_kv_pallas(jnp.stack([kc, vc]), cmp_pos, cmp_w1, cmp_w2)
    n_cmp = cmp.shape[2]
    cmp_end = jnp.arange(n_cmp, dtype=F32) * CMP_STRIDE + (CMP_LEN - 1)
    k_cmp = apply_partial_rope(cmp[0], *rope_tables(cmp_end))
    y_nsa = nsa_attention(q_nsa, k_cmp, cmp[1], k_slc, vsl, k_win, vwn, gate_logit)
    return merge_out(xf, y_fox, y_nsa, merge_logit, w_branch, w_out, norm_ffn, wq)


def peer_ffn(h, hn, q, subkeys, u, v, norm_final):
    idx2, w = peer_topk(q, subkeys)
    coef = peer_u(idx2, hn, w, pack_table(u))
    return peer_v(idx2, coef, pack_table(v), h, norm_final)


def kernel(x, norm_mix, w_in, fox_f_bias, nsa_cmp_pos, nsa_cmp_w1, nsa_cmp_w2, w_branch, w_out,
           norm_ffn, peer_wq, peer_subkeys, peer_u, peer_v, norm_final):
    B_, S_, D = x.shape
    assert norm_mix.shape[0] == 1, "single-layer trunk"
    h, hn, q = hybrid_mixer(x, norm_mix[0], w_in[0], fox_f_bias[0], nsa_cmp_pos[0], nsa_cmp_w1[0], nsa_cmp_w2[0],
                            w_branch[0], w_out[0], norm_ffn[0], peer_wq[0])
    out = peer_ffn(h, hn, q, peer_subkeys[0], peer_u[0], peer_v[0], norm_final)
    return out.reshape(B_, S_, D)
```

```python
import jax
import jax.numpy as jnp
import numpy as np
from jax import lax
from jax.experimental import pallas as pl
from jax.experimental.pallas import tpu as pltpu

D_MODEL = 1024
HEAD_DIM = 64
FOX_HEADS = 8
NSA_HEADS = 8
NSA_GROUPS = 2
NSA_HPG = NSA_HEADS // NSA_GROUPS
BRANCH_WIDTH = 512
N_BRANCH = 2
N_NSA_BRANCH = 3
ROPE_DIM = HEAD_DIM // 4
ROPE_THETA = 500000.0
CMP_LEN = 32
CMP_STRIDE = 16
CMP_HIDDEN = 2 * HEAD_DIM
SLC_LEN = 64
SLC_TOP = 16
WINDOW = 512
FORCED_SCORE = 1e9
NEG_INF = -1e30
PEER_HEADS = 8
N_KEYS = 128
PEER_QDIM = 256
PEER_TOPK = 16
RMS_EPS = 1e-6
KV_WIDTH = NSA_GROUPS * HEAD_DIM
IN_SPLITS = (BRANCH_WIDTH, BRANCH_WIDTH, BRANCH_WIDTH, FOX_HEADS, BRANCH_WIDTH,
             KV_WIDTH, KV_WIDTH, KV_WIDTH, KV_WIDTH, KV_WIDTH, KV_WIDTH,
             NSA_HEADS * N_NSA_BRANCH, N_BRANCH * D_MODEL)


BF16 = jnp.bfloat16
F32 = jnp.float32

PEER_SLOTS = PEER_HEADS * PEER_TOPK
PEER_TT = 128
PEER_CHUNK_SLOTS = 32
PEER_U_UNROLL = 8
PEER_V_UNROLL = 8
ROW_SUB = D_MODEL // 2 // 128
VMEM_LIMIT_PEER = N_KEYS * N_KEYS * D_MODEL * 2 + 16 * 1024 * 1024


def pack_table(tab):
    n, d = tab.shape
    return pl.pallas_call(
        _pack_kernel,
        grid=(n // PACK_TM,),
        in_specs=[pl.BlockSpec((PACK_TM, d), lambda i: (i, 0))],
        out_specs=pl.BlockSpec((PACK_TM * ROW_SUB, 128), lambda i: (i, 0)),
        out_shape=jax.ShapeDtypeStruct((n * ROW_SUB, 128), jnp.uint32),
        name="pack_table",
    )(tab)


PACK_TM = 256


def _pack_kernel(x_ref, o_ref):
    bits = lax.bitcast_convert_type(x_ref[...].astype(jnp.bfloat16).astype(jnp.float32), jnp.uint32)
    for r in range(ROW_SUB):
        even = bits[:, (2 * r) * 128:(2 * r + 1) * 128]
        odd = bits[:, (2 * r + 1) * 128:(2 * r + 2) * 128]
        o_ref[pl.ds(r, PACK_TM, stride=ROW_SUB), :] = (even >> 16) | (odd & jnp.uint32(0xFFFF0000))


def _expert_slab(tab_ref, row):
    return tab_ref[pl.ds(pl.multiple_of(row, ROW_SUB), ROW_SUB), :]


def _unpack(w):
    lo = lax.bitcast_convert_type(w << 16, jnp.float32)
    hi = lax.bitcast_convert_type(w & jnp.uint32(0xFFFF0000), jnp.float32)
    return lo, hi


def _gelu(x):
    return 0.5 * x * (1.0 + lax.erf(x * (2.0 ** -0.5)))


def _peer_u_kernel(idx_a_ref, idx_b_ref, x_ref, w_ref, tab_ref, out_ref):
    lane = lax.broadcasted_iota(jnp.int32, (2 * ROW_SUB, 128), 1)
    lower = lax.broadcasted_iota(jnp.int32, (2 * ROW_SUB, 128), 0) >= ROW_SUB
    pair_lane = jnp.where(lower, lane - PEER_SLOTS // 2, lane)

    def token(t):
        row = x_ref[pl.ds(t, 1), :]
        chunks = [row[:, c * 128:(c + 1) * 128] for c in range(2 * ROW_SUB)]
        xl = jnp.concatenate(chunks[0::2] * 2, axis=0)
        xh = jnp.concatenate(chunks[1::2] * 2, axis=0)
        acc = jnp.zeros((2 * ROW_SUB, 128), jnp.float32)
        base = t * (PEER_SLOTS // 2)
        for s in range(PEER_SLOTS // 2):
            w = jnp.concatenate([_expert_slab(tab_ref, idx_a_ref[base + s]),
                                 _expert_slab(tab_ref, idx_b_ref[base + s])], axis=0)
            lo, hi = _unpack(w)
            part = jnp.sum(lo * xl + hi * xh, axis=1, keepdims=True)
            acc = jnp.where(pair_lane == s, part, acc)
        out_ref[pl.ds(t, 1), :] = w_ref[pl.ds(t, 1), :] * _gelu(jnp.sum(acc, axis=0, keepdims=True))

    def trip(i, carry):
        for j in range(PEER_U_UNROLL):
            token(i * PEER_U_UNROLL + j)
        return carry

    lax.fori_loop(0, PEER_TT // PEER_U_UNROLL, trip, 0)


N_CHUNK = 2 * ROW_SUB


def _peer_v_kernel(idx_a_ref, idx_b_ref, coef_ref, expand_ref, h_ref, g_ref, tab_ref, out_ref,
                   ce_hi_ref, ce_lo_ref):
    half_slots = PEER_SLOTS // 2
    width = PEER_SLOTS * N_CHUNK
    coef = coef_ref[...]
    c_hi = _trunc_bf16(coef)
    expand = expand_ref[...]
    ce_hi_ref[...] = jnp.dot(c_hi.astype(BF16), expand, preferred_element_type=F32)
    ce_lo_ref[...] = jnp.dot((coef - c_hi).astype(BF16), expand, preferred_element_type=F32)
    own_chunk = ((lax.broadcasted_iota(jnp.int32, (N_CHUNK, width), 1) & (N_CHUNK - 1))
                 == lax.broadcasted_iota(jnp.int32, (N_CHUNK, width), 0))

    def token(t):
        base = t * half_slots
        slabs = [jnp.concatenate([_expert_slab(tab_ref, idx_a_ref[base + s]),
                                  _expert_slab(tab_ref, idx_b_ref[base + s])], axis=0) for s in range(half_slots)]
        gathered = pltpu.bitcast(jnp.concatenate(slabs, axis=0), BF16)
        sel = lambda ref: jnp.where(own_chunk, ref[pl.ds(t, 1), :], 0.0).astype(BF16)
        c = jnp.concatenate([sel(ce_hi_ref), sel(ce_lo_ref)], axis=0)
        o = jnp.dot(c, gathered, preferred_element_type=F32)
        o = o[:N_CHUNK] + o[N_CHUNK:]
        out_ref[pl.ds(t, 1), :] = jnp.concatenate([o[r:r + 1, :] for r in range(N_CHUNK)], axis=1)

    def trip(i, carry):
        for j in range(PEER_V_UNROLL):
            token(i * PEER_V_UNROLL + j)
        return carry

    lax.fori_loop(0, PEER_TT // PEER_V_UNROLL, trip, 0)
    out_ref[...] = _rms(h_ref[...] + out_ref[...], g_ref[...])


def _smem_spec():
    return pl.BlockSpec((PEER_TT * PEER_SLOTS // 2,), lambda i: (i,), memory_space=pltpu.SMEM)


def _slot_halves(a):
    return a[:, :PEER_SLOTS // 2].reshape(-1), a[:, PEER_SLOTS // 2:].reshape(-1)


def _table_spec(rows):
    return pl.BlockSpec((rows, 128), lambda i: (0, 0), pipeline_mode=pl.Buffered(1))


def peer_u(idx, x, w, tab):
    t = x.shape[0]
    return pl.pallas_call(
        _peer_u_kernel,
        grid=(t // PEER_TT,),
        in_specs=[_smem_spec(), _smem_spec(), pl.BlockSpec((PEER_TT, D_MODEL), lambda i: (i, 0)),
                  pl.BlockSpec((PEER_TT, PEER_SLOTS), lambda i: (i, 0)), _table_spec(tab.shape[0])],
        out_specs=pl.BlockSpec((PEER_TT, PEER_SLOTS), lambda i: (i, 0)),
        out_shape=jax.ShapeDtypeStruct((t, PEER_SLOTS), jnp.float32),
        compiler_params=pltpu.CompilerParams(vmem_limit_bytes=VMEM_LIMIT_PEER),
        name="peer_u",
    )(*_slot_halves(idx), x, w, tab)


def peer_v(idx, coef, tab, h, g):
    t = idx.shape[0]
    width = PEER_SLOTS * N_CHUNK
    half = PEER_SLOTS // 2
    col_slot = np.arange(width) // N_CHUNK
    col_slot = (col_slot // 2) + half * (col_slot % 2)
    expand = jnp.asarray(col_slot[None, :] == np.arange(PEER_SLOTS)[:, None], dtype=BF16)
    return pl.pallas_call(
        _peer_v_kernel,
        grid=(t // PEER_TT,),
        in_specs=[_smem_spec(), _smem_spec(), pl.BlockSpec((PEER_TT, PEER_SLOTS), lambda i: (i, 0)),
                  pl.BlockSpec((PEER_SLOTS, width), lambda i: (0, 0)),
                  pl.BlockSpec((PEER_TT, D_MODEL), lambda i: (i, 0)), pl.BlockSpec((1, D_MODEL), lambda i: (0, 0)),
                  _table_spec(tab.shape[0])],
        out_specs=pl.BlockSpec((PEER_TT, D_MODEL), lambda i: (i, 0)),
        out_shape=jax.ShapeDtypeStruct((t, D_MODEL), jnp.float32),
        scratch_shapes=[pltpu.VMEM((PEER_TT, width), F32), pltpu.VMEM((PEER_TT, width), F32)],
        compiler_params=pltpu.CompilerParams(vmem_limit_bytes=VMEM_LIMIT_PEER),
        name="peer_v",
    )(*_slot_halves(idx), coef, expand, h, g.reshape(1, D_MODEL), tab)


PK_TT = 128
PK_A_BLOCKS = ((0, 16), (1, 8), (2, 5), (3, 4))
PK_B_BLOCKS = ((0, 4, 16), (1, 4, 8), (2, 4, 5))
NEG_HUGE = -3.0e38
PK_HEADS_PER_TRIP = 8


def _extract_top(problems, flat, n):
    vals = [[] for _ in problems]
    picked = [[] for _ in problems]
    cur = [v for v, _ in problems]
    for _ in range(n):
        for i, (_, payload) in enumerate(problems):
            m = jnp.max(cur[i], axis=0, keepdims=True)
            pos = jnp.min(jnp.where(cur[i] == m, flat, 1e9), axis=0, keepdims=True)
            hit = flat == pos
            picked[i].append(pos if payload is None else jnp.sum(jnp.where(hit, payload, 0.0), axis=0, keepdims=True))
            cur[i] = jnp.where(hit, NEG_HUGE, cur[i])
            vals[i].append(m)
    return [jnp.concatenate(v, axis=0) for v in vals], [jnp.concatenate(p, axis=0) for p in picked]


def _candidate_grid(first, second, combine):
    up8 = lambda n: -(-n // 8) * 8
    return jnp.concatenate([combine(first[a:a + 1], second[0:up8(nb)]) for a, nb in PK_A_BLOCKS]
                           + [combine(first[0:up8(a1)], second[b:b + 1]) for b, _, a1 in PK_B_BLOCKS], axis=0)


def _candidate_valid_and_flat():
    up8 = lambda n: -(-n // 8) * 8
    valid, flat = [], []
    for a, nb in PK_A_BLOCKS:
        r = lax.broadcasted_iota(jnp.int32, (up8(nb), PK_TT), 0)
        valid.append(r < nb)
        flat.append(a * PEER_TOPK + r)
    for b, a0, a1 in PK_B_BLOCKS:
        r = lax.broadcasted_iota(jnp.int32, (up8(a1), PK_TT), 0)
        valid.append((r >= a0) & (r < a1))
        flat.append(r * PEER_TOPK + b)
    valid = jnp.concatenate(valid, axis=0)
    return valid, jnp.where(valid, jnp.concatenate(flat, axis=0).astype(F32), 1e9)


def _peer_topk_kernel(q_ref, ka_ref, eid_ref, w_ref):
    key_row = lax.broadcasted_iota(jnp.int32, (N_KEYS, PK_TT), 0).astype(F32)
    valid, flat = _candidate_valid_and_flat()

    def heads(i, carry):
        scores = []
        for j in range(2 * PK_HEADS_PER_TRIP):
            q = q_ref[2 * PK_HEADS_PER_TRIP * i + j]
            q_hi = _trunc_bf16(q)
            qa = jnp.concatenate([q_hi.astype(BF16), (q - q_hi).astype(BF16), q_hi.astype(BF16)], axis=1)
            scores.append(lax.dot_general(ka_ref[2 * PK_HEADS_PER_TRIP * i + j], qa, (((1,), (1,)), ((), ())),
                                          preferred_element_type=F32))
        sub_s, sub_i = _extract_top([(s, None) for s in scores], key_row, PEER_TOPK)
        cands = []
        for j in range(PK_HEADS_PER_TRIP):
            s1, s2, i1, i2 = sub_s[2 * j], sub_s[2 * j + 1], sub_i[2 * j], sub_i[2 * j + 1]
            cands.append((jnp.where(valid, _candidate_grid(s1, s2, lambda x, y: x + y), NEG_HUGE),
                          _candidate_grid(i1, i2, lambda x, y: x * N_KEYS + y)))
        top_s, top_e = _extract_top(cands, flat, PEER_TOPK)
        for j in range(PK_HEADS_PER_TRIP):
            e = jnp.exp(top_s[j] - top_s[j][0:1])
            w_ref[PK_HEADS_PER_TRIP * i + j] = e / jnp.sum(e, axis=0, keepdims=True)
            eid_ref[PK_HEADS_PER_TRIP * i + j] = top_e[j].astype(jnp.int32) * ROW_SUB
        return carry

    lax.fori_loop(0, PEER_HEADS // PK_HEADS_PER_TRIP, heads, 0)


def peer_topk(qt, subkeys):
    hp, t, d = qt.shape
    k_hi, k_lo = _split2(subkeys.reshape(hp, N_KEYS, d))
    ka = jnp.concatenate([k_hi, k_hi, k_lo], axis=2)
    eid, w = pl.pallas_call(
        _peer_topk_kernel,
        grid=(t // PK_TT,),
        in_specs=[pl.BlockSpec((hp, PK_TT, d), lambda i: (0, i, 0)),
                  pl.BlockSpec((hp, N_KEYS, 3 * d), lambda i: (0, 0, 0))],
        out_specs=[pl.BlockSpec((PEER_HEADS, PEER_TOPK, PK_TT), lambda i: (0, 0, i)),
                   pl.BlockSpec((PEER_HEADS, PEER_TOPK, PK_TT), lambda i: (0, 0, i))],
        out_shape=[jax.ShapeDtypeStruct((PEER_HEADS, PEER_TOPK, t), jnp.int32),
                   jax.ShapeDtypeStruct((PEER_HEADS, PEER_TOPK, t), F32)],
        name="peer_topk",
    )(qt, ka)
    to_slots = lambda a: a.transpose(2, 0, 1).reshape(t, PEER_SLOTS)
    return to_slots(eid), to_slots(w)


def split_columns(t, sizes):
    offs = np.cumsum((0,) + tuple(sizes))
    return [t[..., int(a):int(b)] for a, b in zip(offs[:-1], offs[1:])]


def rope_tables(pos):
    inv = jnp.power(ROPE_THETA, -jnp.arange(0, ROPE_DIM, 2, dtype=jnp.float32) / ROPE_DIM)
    ang = pos[:, None] * inv[None, :]
    return jnp.cos(ang), jnp.sin(ang)


def apply_partial_rope(x, cos, sin):
    half = ROPE_DIM // 2
    xr = x[..., :ROPE_DIM].astype(jnp.float32)
    x1, x2 = xr[..., :half], xr[..., half:]
    c, s = cos[:, None, :], sin[:, None, :]
    rot = jnp.concatenate([x1 * c - x2 * s, x1 * s + x2 * c], axis=-1)
    return jnp.concatenate([rot.astype(x.dtype), x[..., ROPE_DIM:]], axis=-1)


LOG2_E = 1.4426950408889634
LOGIT_SCALE = HEAD_DIM ** -0.5 * LOG2_E


def _logits(ks, qts):
    return [jnp.dot(k, qt, preferred_element_type=F32) for k, qt in zip(ks, qts)]


def _flash_update(ss, vts, mask, carries):
    stats, ps = [], []
    for s, (m, l, _) in zip(ss, carries):
        s_vis = s if mask is None else jnp.where(mask, s, NEG_INF)
        m_new = jnp.maximum(m, jnp.max(s_vis, axis=0, keepdims=True))
        alpha = jnp.exp2(m - m_new)
        p = jnp.exp2(s - m_new)
        if mask is not None:
            p = jnp.where(mask, p, 0.0)
        stats.append((m_new, alpha, alpha * l + jnp.sum(p, axis=0, keepdims=True)))
        ps.append(p.astype(BF16))
    return tuple((m_new, l, alpha * acc + jnp.dot(vt, p, preferred_element_type=F32))
                 for (m_new, alpha, l), p, vt, (_, _, acc) in zip(stats, ps, vts, carries))


def _causal_flash(key_tile, value_tile, qts, n_full, last_mask, n):
    def body(kt, carries):
        return _flash_update(_logits(key_tile(kt), qts), value_tile(kt), None, carries)

    carries = lax.fori_loop(0, n_full, body, tuple(_flash_init(n) for _ in qts))
    return _flash_update(_logits(key_tile(n_full), qts), value_tile(n_full), last_mask, carries)


def _flash_init(n):
    return (jnp.full((1, n), NEG_INF, F32), jnp.zeros((1, n), F32), jnp.zeros((HEAD_DIM, n), F32))


FOX_TQ = 256
FOX_KV = 512
FOX_DK = 256


FOX_HP = 8


def _fox_kernel(qt_ref, k_ref, vt_ref, o_ref):
    qi = pl.program_id(1)
    q0 = qi * FOX_TQ
    qts = [qt_ref[h] for h in range(FOX_HP)]
    qpos = q0 + lax.broadcasted_iota(jnp.int32, (1, FOX_TQ), 1)
    krow = lax.broadcasted_iota(jnp.int32, (FOX_KV, FOX_TQ), 0)

    n_full = q0 // FOX_KV
    causal = (n_full * FOX_KV + krow) <= qpos
    carry = _causal_flash(lambda kt: [k_ref[h, kt] for h in range(FOX_HP)],
                          lambda kt: [vt_ref[h, kt] for h in range(FOX_HP)], qts, n_full, causal, FOX_TQ)
    for h in range(FOX_HP):
        _, l, acc = carry[h]
        o_ref[h] = acc / l


def _trunc_bf16(x):
    bits = lax.bitcast_convert_type(x, jnp.uint32) & jnp.uint32(0xFFFF0000)
    return lax.bitcast_convert_type(bits, F32)


def _split3(c):
    c1 = _trunc_bf16(c)
    r = c - c1
    c2 = _trunc_bf16(r)
    return c1.astype(BF16), c2.astype(BF16), (r - c2).astype(BF16)


def _split2(x):
    hi = _trunc_bf16(x)
    return hi.astype(BF16), (x - hi).astype(BF16)


def _aug_q(q):
    hi, lo = _split2(q)
    return jnp.concatenate([hi, hi, lo], axis=-1)


def _aug_k(k):
    hi, lo = _split2(k)
    return jnp.concatenate([hi, lo, hi], axis=-1)


def fox_attention(q, k, v, log_f):
    B_, S_, H, Dh = q.shape
    KV_TILE = FOX_KV
    n_qt, n_kt = S_ // FOX_TQ, S_ // KV_TILE
    c = jnp.cumsum(log_f, axis=1)
    c1, c2, c3 = _split3(c * LOG2_E)
    j = jnp.arange(FOX_DK - 3 * Dh)
    pick = lambda t: t[..., None].astype(F32)
    terms = lambda first: jnp.where(j == first, pick(c1), jnp.where(j == first + 1, pick(c2), pick(c3)))
    q_extra = jnp.where(j < 3, 1.0, jnp.where(j < 6, terms(3), 0.0)).astype(BF16)
    k_extra = jnp.where(j < 3, -terms(0), jnp.where(j < 6, 1.0, 0.0)).astype(BF16)
    qa = jnp.concatenate([_aug_q(q * LOGIT_SCALE), q_extra], axis=-1)
    ka = jnp.concatenate([_aug_k(k), k_extra], axis=-1)
    HP, HG = FOX_HP, H // FOX_HP
    qt = qa.reshape(B_, n_qt, FOX_TQ, HG, HP, FOX_DK).transpose(0, 3, 1, 4, 5, 2).reshape(B_ * HG * n_qt, HP, FOX_DK, FOX_TQ)
    kk = ka.reshape(B_, n_kt, KV_TILE, HG, HP, FOX_DK).transpose(0, 3, 4, 1, 2, 5).reshape(B_ * HG, HP, n_kt, KV_TILE, FOX_DK)
    vt = v.astype(BF16).reshape(B_, n_kt, KV_TILE, HG, HP, Dh).transpose(0, 3, 4, 1, 5, 2).reshape(B_ * HG, HP, n_kt, Dh, KV_TILE)
    out = pl.pallas_call(
        _fox_kernel,
        grid=(B_ * HG, n_qt),
        in_specs=[pl.BlockSpec((None, HP, FOX_DK, FOX_TQ), lambda b, i: (b * n_qt + i, 0, 0, 0)),
                  pl.BlockSpec((None, HP, n_kt, KV_TILE, FOX_DK), lambda b, i: (b, 0, 0, 0, 0)),
                  pl.BlockSpec((None, HP, n_kt, Dh, KV_TILE), lambda b, i: (b, 0, 0, 0, 0))],
        out_specs=pl.BlockSpec((None, HP, Dh, FOX_TQ), lambda b, i: (b * n_qt + i, 0, 0, 0)),
        out_shape=jax.ShapeDtypeStruct((B_ * HG * n_qt, HP, Dh, FOX_TQ), F32),
        name="fox_attention",
    )(qt, kk, vt)
    return out.reshape(B_, HG, n_qt, HP, Dh, FOX_TQ)


NSA_TQ = 256
NSA_N = NSA_HPG * NSA_TQ
NSA_CH = 256
N_CMP_PAD = 256
N_SLC = 64
NSA_DK = 3 * HEAD_DIM
NSA_KV = 512
WIN_TILES = WINDOW // NSA_TQ + 1
WIN_KEYS = WIN_TILES * NSA_TQ


def _nsa_kernel(qt_ref, kc_ref, vct_ref, ks_ref, vst_ref, kw_ref, vwt_ref, gate_ref, ovl_ref, o_ref):
    qi = pl.program_id(1)
    q0 = qi * NSA_TQ
    qt = jnp.concatenate([qt_ref[h] for h in range(NSA_HPG)], axis=1)
    lane = lax.broadcasted_iota(jnp.int32, (1, NSA_N), 1)
    qpos = q0 + (lane & (NSA_TQ - 1))

    hn = NSA_CH
    n_ch = NSA_N // NSA_CH
    qts = [qt[:, c * hn:(c + 1) * hn] for c in range(n_ch)]
    qpos_h = qpos[:, :hn]

    w0 = jnp.maximum(qi - WINDOW // NSA_TQ, 0)
    k_w = jnp.concatenate([kw_ref[w0 + i] for i in range(WIN_TILES)], axis=0)
    vt_w = jnp.concatenate([vwt_ref[w0 + i] for i in range(WIN_TILES)], axis=1)
    s = jnp.dot(kc_ref[...], qt, preferred_element_type=F32)
    s_w = [jnp.dot(k_w, qts[c], preferred_element_type=F32) for c in range(n_ch)]

    cmp_end = lax.broadcasted_iota(jnp.int32, (N_CMP_PAD, NSA_N), 0) * CMP_STRIDE + (CMP_LEN - 1)
    mc = cmp_end <= qpos
    m = jnp.max(jnp.where(mc, s, NEG_INF), axis=0, keepdims=True)
    p = jnp.where(mc, jnp.exp2(s - m), 0.0)
    l = jnp.sum(p, axis=0, keepdims=True)
    pc = p * jnp.where(l > 0.0, 1.0 / l, 0.0)
    o_cmp = jnp.dot(vct_ref[...], pc.astype(BF16), preferred_element_type=F32)
    pcs = sum(pc[:, h * NSA_TQ:(h + 1) * NSA_TQ] for h in range(NSA_HPG))
    pcs_hi = pcs.astype(BF16)
    pcs_lo = (pcs - pcs_hi.astype(F32)).astype(BF16)
    ovl = ovl_ref[...]
    imp = jnp.dot(ovl, pcs_hi, preferred_element_type=F32) + jnp.dot(ovl, pcs_lo, preferred_element_type=F32)

    rel = qpos_h - (w0 * NSA_TQ + lax.broadcasted_iota(jnp.int32, (WIN_KEYS, hn), 0))
    in_win = (rel >= 0) & (rel < WINDOW)
    outs_w = []
    for c in range(n_ch):
        m = jnp.max(jnp.where(in_win, s_w[c], NEG_INF), axis=0, keepdims=True)
        p = jnp.where(in_win, jnp.exp2(s_w[c] - m), 0.0)
        l = jnp.sum(p, axis=0, keepdims=True)
        outs_w.append(jnp.dot(vt_w, p.astype(BF16), preferred_element_type=F32) / l)
    acc_w = jnp.concatenate(outs_w, axis=1)

    blk = lax.broadcasted_iota(jnp.int32, (N_SLC, NSA_TQ), 0)
    q_blk = (q0 + lax.broadcasted_iota(jnp.int32, (N_SLC, NSA_TQ), 1)) >> 6
    forced = (blk == 0) | (blk == q_blk) | (blk == q_blk - 1)
    score = jnp.where(forced, FORCED_SCORE, jnp.where(blk <= q_blk, imp, -1.0))
    rank = jnp.zeros((N_SLC, NSA_TQ), jnp.int32)
    for i in range(N_SLC):
        row = score[i:i + 1, :]
        beats = (row > score) | ((row == score) & (blk > i))
        rank = rank + beats.astype(jnp.int32)
    bias = jnp.where(rank < SLC_TOP, 0.0, NEG_INF).astype(BF16)
    bias = jnp.concatenate([bias] * (hn // NSA_TQ), axis=1)
    qts_s = [jnp.concatenate([qh, bias], axis=0) for qh in qts]
    krow = lax.broadcasted_iota(jnp.int32, (NSA_KV, hn), 0)

    n_full = q0 // NSA_KV
    causal = (n_full * NSA_KV + krow) <= qpos_h
    slc = _causal_flash(lambda kt: [ks_ref[kt]] * n_ch, lambda kt: [vst_ref[kt]] * n_ch, qts_s, n_full, causal, hn)
    acc_s = jnp.concatenate([acc / l for _, l, acc in slc], axis=1)

    g = jax.nn.sigmoid(gate_ref[...])
    o_ref[...] = g[0:1] * o_cmp + g[1:2] * acc_s + g[2:3] * acc_w


def nsa_attention(q, k_cmp, v_cmp, k_slc, v_slc, k_win, v_win, gate_logit):
    B_, S_, H, Dh = q.shape
    G = NSA_GROUPS
    n_qt = S_ // NSA_TQ
    n_cmp = k_cmp.shape[1]
    cs = np.arange(N_CMP_PAD)[None, :] * CMP_STRIDE
    ss = np.arange(N_SLC)[:, None] * SLC_LEN
    ov = np.clip(np.minimum(cs + CMP_LEN, ss + SLC_LEN) - np.maximum(cs, ss), 0, None) / CMP_LEN
    ov[:, n_cmp:] = 0.0
    ovl = jnp.asarray(ov, dtype=BF16)
    qs = _aug_q(q * LOGIT_SCALE).reshape(B_, n_qt, NSA_TQ, G, NSA_HPG, NSA_DK)
    qt = qs.transpose(0, 3, 1, 4, 5, 2).reshape(B_ * G * n_qt, NSA_HPG, NSA_DK, NSA_TQ)
    gt = gate_logit.astype(F32).reshape(B_, n_qt, NSA_TQ, G, NSA_HPG, N_NSA_BRANCH)
    gt = gt.transpose(0, 3, 1, 5, 4, 2).reshape(B_ * G * n_qt, N_NSA_BRANCH, NSA_N)
    tiles = lambda t, kv: t.reshape(B_, S_ // kv, kv, G, -1).transpose(0, 3, 1, 2, 4).reshape(B_ * G, S_ // kv, kv, t.shape[-1])
    keys = lambda t, kv: tiles(_aug_k(t), kv)
    block_onehot = (jnp.arange(S_)[:, None] // SLC_LEN == jnp.arange(N_SLC)[None, :]).astype(BF16)
    k_slc_aug = jnp.concatenate([_aug_k(k_slc), jnp.broadcast_to(block_onehot[None, :, None, :], (B_, S_, G, N_SLC))], axis=-1)
    vals = lambda t, kv: t.astype(BF16).reshape(B_, S_ // kv, kv, G, Dh).transpose(0, 3, 1, 4, 2).reshape(B_ * G, S_ // kv, Dh, kv)
    padc = ((0, 0), (0, N_CMP_PAD - n_cmp), (0, 0), (0, 0))
    kc = _aug_k(jnp.pad(k_cmp, padc)).transpose(0, 2, 1, 3).reshape(B_ * G, N_CMP_PAD, NSA_DK)
    vct = jnp.pad(v_cmp, padc).astype(BF16).transpose(0, 2, 3, 1).reshape(B_ * G, Dh, N_CMP_PAD)
    per_bg = lambda *blk: pl.BlockSpec((None,) + blk, lambda b, i: (b,) + (0,) * len(blk))
    per_tile = lambda *blk: pl.BlockSpec((None,) + blk, lambda b, i: (b * n_qt + i,) + (0,) * len(blk))
    out = pl.pallas_call(
        _nsa_kernel,
        grid=(B_ * G, n_qt),
        in_specs=[per_tile(NSA_HPG, NSA_DK, NSA_TQ),
                  per_bg(N_CMP_PAD, NSA_DK), per_bg(Dh, N_CMP_PAD),
                  per_bg(S_ // NSA_KV, NSA_KV, NSA_DK + N_SLC), per_bg(S_ // NSA_KV, Dh, NSA_KV),
                  per_bg(n_qt, NSA_TQ, NSA_DK), per_bg(n_qt, Dh, NSA_TQ),
                  per_tile(N_NSA_BRANCH, NSA_N),
                  pl.BlockSpec((N_SLC, N_CMP_PAD), lambda b, i: (0, 0))],
        out_specs=per_tile(Dh, NSA_N),
        out_shape=jax.ShapeDtypeStruct((B_ * G * n_qt, Dh, NSA_N), F32),
        name="nsa_attention",
    )(qt, kc, vct, tiles(k_slc_aug, NSA_KV), vals(v_slc, NSA_KV), keys(k_win, NSA_TQ), vals(v_win, NSA_TQ), gt, ovl)
    return out.reshape(B_, G, n_qt, Dh, NSA_N)


DENSE_TM = 256
SEC_MERGE = N_BRANCH * D_MODEL
SEC_SMALL = 256
IN_SECTIONS = (BRANCH_WIDTH,) * 4 + (KV_WIDTH,) * 6 + (SEC_MERGE, SEC_SMALL)


def _rms(x, g):
    return x * lax.rsqrt(jnp.mean(x * x, axis=-1, keepdims=True) + RMS_EPS) * g


def _row_spec(width):
    return pl.BlockSpec((DENSE_TM, width), lambda i: (i, 0))


def _whole_spec(shape):
    return pl.BlockSpec(shape, lambda i: (0,) * len(shape))


def _in_proj_kernel(x_ref, g_ref, w_ref, *o_refs):
    xn = _rms(x_ref[...], g_ref[...]).astype(BF16)
    off = 0
    for o_ref, width in zip(o_refs, IN_SECTIONS):
        o_ref[...] = jnp.dot(xn, w_ref[:, off:off + width], preferred_element_type=F32)
        off += width


def in_projection(x, g, w_in):
    t, d = x.shape
    fq, fk, fv, fl, nq, kc, vc, ksl, vsl, kwn, vwn, gl, ml = split_columns(w_in, IN_SPLITS)
    pad = lambda w: jnp.pad(w, ((0, 0), (0, 128 - w.shape[1])))
    w = jnp.concatenate([fq, fk, fv, nq, kc, vc, ksl, vsl, kwn, vwn, ml, pad(fl), pad(gl)], axis=1).astype(BF16)
    return pl.pallas_call(
        _in_proj_kernel,
        grid=(t // DENSE_TM,),
        in_specs=[_row_spec(d), _whole_spec((1, d)), _whole_spec(w.shape)],
        out_specs=[_row_spec(s) for s in IN_SECTIONS],
        out_shape=[jax.ShapeDtypeStruct((t, s), F32) for s in IN_SECTIONS],
        name="in_projection",
    )(x, g.reshape(1, d), w)


def _merge_kernel(x_ref, yf_ref, yn_ref, ml_ref, wb_ref, wo_ref, gq_ref, wq_ref, o_ref, hn_ref, q_ref):
    yf_t = yf_ref[...].reshape(BRANCH_WIDTH, DENSE_TM)
    yn_t = jnp.concatenate(
        [jnp.concatenate([yn_ref[g, j, :, h * NSA_TQ:(h + 1) * NSA_TQ]
                          for g in range(NSA_GROUPS) for h in range(NSA_HPG)], axis=0)
         for j in range(DENSE_TM // NSA_TQ)], axis=1)
    g = jax.nn.sigmoid(ml_ref[...])
    up_f = jnp.dot(yf_t.T.astype(BF16), wb_ref[0], preferred_element_type=F32)
    up_n = jnp.dot(yn_t.T.astype(BF16), wb_ref[1], preferred_element_type=F32)
    merged = g[:, :D_MODEL] * up_f + g[:, D_MODEL:] * up_n
    h = x_ref[...] + jnp.dot(merged.astype(BF16), wo_ref[...], preferred_element_type=F32)
    o_ref[...] = h
    hn = _rms(h, gq_ref[...])
    hn_ref[...] = hn
    q = jnp.dot(hn.astype(BF16), wq_ref[...], preferred_element_type=F32)
    dq = PEER_QDIM // 2
    for j in range(2 * PEER_HEADS):
        q_ref[j] = q[:, j * dq:(j + 1) * dq]


def merge_out(x, y_fox, y_nsa, merge_logit, w_branch, w_out, norm_ffn, wq):
    t, d = x.shape
    hp, dq = 2 * PEER_HEADS, PEER_QDIM // 2
    assert FOX_TQ == DENSE_TM and DENSE_TM % NSA_TQ == 0
    tiles_per_seq = y_fox.shape[2]
    nsa_per_tile = DENSE_TM // NSA_TQ
    yf_spec = pl.BlockSpec((None,) + y_fox.shape[1:2] + (None,) + y_fox.shape[3:],
                           lambda i: (i // tiles_per_seq, 0, i % tiles_per_seq, 0, 0, 0))
    yn_spec = pl.BlockSpec((None, NSA_GROUPS, nsa_per_tile, HEAD_DIM, NSA_N),
                           lambda i: (i // tiles_per_seq, 0, i % tiles_per_seq, 0, 0))
    return pl.pallas_call(
        _merge_kernel,
        grid=(t // DENSE_TM,),
        in_specs=[_row_spec(d), yf_spec, yn_spec, _row_spec(SEC_MERGE),
                  _whole_spec(w_branch.shape), _whole_spec(w_out.shape), _whole_spec((1, d)), _whole_spec(wq.shape)],
        out_specs=[_row_spec(d), _row_spec(d), pl.BlockSpec((hp, DENSE_TM, dq), lambda i: (0, i, 0))],
        out_shape=[jax.ShapeDtypeStruct((t, d), F32), jax.ShapeDtypeStruct((t, d), F32),
                   jax.ShapeDtypeStruct((hp, t, dq), F32)],
        name="merge_out",
    )(x, y_fox, y_nsa, merge_logit, w_branch.astype(BF16), w_out.astype(BF16), norm_ffn.reshape(1, d),
      wq.astype(BF16))


N_CMP_ROWS = 256


def _compress_kernel(r_ref, pos_ref, w1_ref, w2_ref, o_ref):
    r = r_ref[...]
    nxt = jnp.concatenate([r[1:], jnp.zeros((1, r.shape[1]), F32)], axis=0)
    half = r.shape[1]
    hid = (jnp.dot((r + pos_ref[0:1, :]).astype(BF16), w1_ref[:half], preferred_element_type=F32)
           + jnp.dot((nxt + pos_ref[1:2, :]).astype(BF16), w1_ref[half:], preferred_element_type=F32))
    o_ref[...] = jnp.dot(_gelu(hid).astype(BF16), w2_ref[...], preferred_element_type=F32)


def compress_kv_pallas(kv, cmp_pos, w1, w2):
    _, B_, S_, G, Dh = kv.shape
    n_cmp = (S_ - CMP_LEN) // CMP_STRIDE + 1
    width = CMP_STRIDE * Dh
    rows = kv.transpose(0, 1, 3, 2, 4).reshape(2, B_ * G, S_ // CMP_STRIDE, width)
    out = pl.pallas_call(
        _compress_kernel,
        grid=(2, B_ * G),
        in_specs=[pl.BlockSpec((None, None, N_CMP_ROWS, width), lambda j, b: (j, b, 0, 0)),
                  pl.BlockSpec((None, 2, width), lambda j, b: (j, 0, 0)),
                  pl.BlockSpec((None, CMP_LEN * Dh, CMP_HIDDEN), lambda j, b: (j, 0, 0)),
                  pl.BlockSpec((None, CMP_HIDDEN, Dh), lambda j, b: (j, 0, 0))],
        out_specs=pl.BlockSpec((None, None, N_CMP_ROWS, Dh), lambda j, b: (j, b, 0, 0)),
        out_shape=jax.ShapeDtypeStruct((2, B_ * G, N_CMP_ROWS, Dh), F32),
        name="compress_kv",
    )(rows, cmp_pos.reshape(2, 2, width), w1.astype(BF16), w2.astype(BF16))
    return out[:, :, :n_cmp].reshape(2, B_, G, n_cmp, Dh).transpose(0, 1, 3, 2, 4)


def hybrid_mixer(x, norm_g, w_in, fox_f_bias, cmp_pos, cmp_w1, cmp_w2, w_branch, w_out, norm_ffn, wq):
    B_, S_, D = x.shape
    xf = x.reshape(B_ * S_, D)
    secs = in_projection(xf, norm_g, w_in)
    heads = lambda t, n: t.reshape(B_, S_, n, HEAD_DIM)
    fq, fk, fv, nq = secs[:4]
    kc, vc, ksl, vsl, kwn, vwn = (heads(t, NSA_GROUPS) for t in secs[4:10])
    merge_logit, small = secs[10:]
    f_logit = small[:, :FOX_HEADS].reshape(B_, S_, FOX_HEADS)
    gate_logit = small[:, 128:128 + NSA_HEADS * N_NSA_BRANCH].reshape(B_, S_, NSA_HEADS, N_NSA_BRANCH)
    log_f = jax.nn.log_sigmoid(f_logit + fox_f_bias.astype(F32))
    y_fox = fox_attention(heads(fq, FOX_HEADS), heads(fk, FOX_HEADS), heads(fv, FOX_HEADS), log_f)
    cos, sin = rope_tables(jnp.arange(S_, dtype=F32))
    q_nsa = apply_partial_rope(heads(nq, NSA_HEADS), cos, sin)
    k_slc = apply_partial_rope(ksl, cos, sin)
    k_win = apply_partial_rope(kwn, cos, sin)
    cmp = compress_kv_pallas(jnp.stack([kc, vc]), cmp_pos, cmp_w1, cmp_w2)
    n_cmp = cmp.shape[2]
    cmp_end = jnp.arange(n_cmp, dtype=F32) * CMP_STRIDE + (CMP_LEN - 1)
    k_cmp = apply_partial_rope(cmp[0], *rope_tables(cmp_end))
    y_nsa = nsa_attention(q_nsa, k_cmp, cmp[1], k_slc, vsl, k_win, vwn, gate_logit)
    return merge_out(xf, y_fox, y_nsa, merge_logit, w_branch, w_out, norm_ffn, wq)


def peer_ffn(h, hn, q, subkeys, u, v, norm_final):
    idx2, w = peer_topk(q, subkeys)
    coef = peer_u(idx2, hn, w, pack_table(u))
    return peer_v(idx2, coef, pack_table(v), h, norm_final)


def kernel(x, norm_mix, w_in, fox_f_bias, nsa_cmp_pos, nsa_cmp_w1, nsa_cmp_w2, w_branch, w_out,
           norm_ffn, peer_wq, peer_subkeys, peer_u, peer_v, norm_final):
    B_, S_, D = x.shape
    assert norm_mix.shape[0] == 1, "single-layer trunk"
    h, hn, q = hybrid_mixer(x, norm_mix[0], w_in[0], fox_f_bias[0], nsa_cmp_pos[0], nsa_cmp_w1[0], nsa_cmp_w2[0],
                            w_branch[0], w_out[0], norm_ffn[0], peer_wq[0])
    out = peer_ffn(h, hn, q, peer_subkeys[0], peer_u[0], peer_v[0], norm_final)
    return out.reshape(B_, S_, D)
```

```python
import jax
import jax.numpy as jnp
import numpy as np
from jax import lax
from jax.experimental import pallas as pl
from jax.experimental.pallas import tpu as pltpu

D_MODEL = 1024
HEAD_DIM = 64
FOX_HEADS = 8
NSA_HEADS = 8
NSA_GROUPS = 2
NSA_HPG = NSA_HEADS // NSA_GROUPS
BRANCH_WIDTH = 512
N_BRANCH = 2
N_NSA_BRANCH = 3
ROPE_DIM = HEAD_DIM // 4
ROPE_THETA = 500000.0
CMP_LEN = 32
CMP_STRIDE = 16
CMP_HIDDEN = 2 * HEAD_DIM
SLC_LEN = 64
SLC_TOP = 16
WINDOW = 512
FORCED_SCORE = 1e9
NEG_INF = -1e30
PEER_HEADS = 8
N_KEYS = 128
PEER_QDIM = 256
PEER_TOPK = 16
RMS_EPS = 1e-6
KV_WIDTH = NSA_GROUPS * HEAD_DIM
IN_SPLITS = (BRANCH_WIDTH, BRANCH_WIDTH, BRANCH_WIDTH, FOX_HEADS, BRANCH_WIDTH,
             KV_WIDTH, KV_WIDTH, KV_WIDTH, KV_WIDTH, KV_WIDTH, KV_WIDTH,
             NSA_HEADS * N_NSA_BRANCH, N_BRANCH * D_MODEL)


BF16 = jnp.bfloat16
F32 = jnp.float32

PEER_SLOTS = PEER_HEADS * PEER_TOPK
PEER_TT = 128
PEER_CHUNK_SLOTS = 32
PEER_U_UNROLL = 16
PEER_V_UNROLL = 16
ROW_SUB = D_MODEL // 2 // 128
VMEM_LIMIT_PEER = N_KEYS * N_KEYS * D_MODEL * 2 + 16 * 1024 * 1024


def pack_table(tab):
    n, d = tab.shape
    return pl.pallas_call(
        _pack_kernel,
        grid=(n // PACK_TM,),
        in_specs=[pl.BlockSpec((PACK_TM, d), lambda i: (i, 0))],
        out_specs=pl.BlockSpec((PACK_TM * ROW_SUB, 128), lambda i: (i, 0)),
        out_shape=jax.ShapeDtypeStruct((n * ROW_SUB, 128), jnp.uint32),
        name="pack_table",
    )(tab)


PACK_TM = 256


def _pack_kernel(x_ref, o_ref):
    bits = lax.bitcast_convert_type(x_ref[...].astype(jnp.bfloat16).astype(jnp.float32), jnp.uint32)
    for r in range(ROW_SUB):
        even = bits[:, (2 * r) * 128:(2 * r + 1) * 128]
        odd = bits[:, (2 * r + 1) * 128:(2 * r + 2) * 128]
        o_ref[pl.ds(r, PACK_TM, stride=ROW_SUB), :] = (even >> 16) | (odd & jnp.uint32(0xFFFF0000))


def _expert_slab(tab_ref, row):
    return tab_ref[pl.ds(pl.multiple_of(row, ROW_SUB), ROW_SUB), :]


def _unpack(w):
    lo = lax.bitcast_convert_type(w << 16, jnp.float32)
    hi = lax.bitcast_convert_type(w & jnp.uint32(0xFFFF0000), jnp.float32)
    return lo, hi


def _gelu(x):
    return 0.5 * x * (1.0 + lax.erf(x * (2.0 ** -0.5)))


def _gather_token(idx_a_ref, idx_b_ref, tab_ref, t):
    base = t * (PEER_SLOTS // 2)
    slabs = [jnp.concatenate([_expert_slab(tab_ref, idx_a_ref[base + s]),
                              _expert_slab(tab_ref, idx_b_ref[base + s])], axis=0) for s in range(PEER_SLOTS // 2)]
    return pltpu.bitcast(jnp.concatenate(slabs, axis=0), BF16)


def _own_chunk(width):
    return ((lax.broadcasted_iota(jnp.int32, (N_CHUNK, width), 1) & (N_CHUNK - 1))
            == lax.broadcasted_iota(jnp.int32, (N_CHUNK, width), 0))


def _peer_u_kernel(idx_a_ref, idx_b_ref, x_ref, w_ref, tab_ref, out_ref):
    lane = lax.broadcasted_iota(jnp.int32, (2 * ROW_SUB, 128), 1)
    lower = lax.broadcasted_iota(jnp.int32, (2 * ROW_SUB, 128), 0) >= ROW_SUB
    pair_lane = jnp.where(lower, lane - PEER_SLOTS // 2, lane)

    def token(t):
        row = x_ref[pl.ds(t, 1), :]
        chunks = [row[:, c * 128:(c + 1) * 128] for c in range(2 * ROW_SUB)]
        xl = jnp.concatenate(chunks[0::2] * 2, axis=0)
        xh = jnp.concatenate(chunks[1::2] * 2, axis=0)
        acc = jnp.zeros((2 * ROW_SUB, 128), jnp.float32)
        base = t * (PEER_SLOTS // 2)
        for s in range(PEER_SLOTS // 2):
            w = jnp.concatenate([_expert_slab(tab_ref, idx_a_ref[base + s]),
                                 _expert_slab(tab_ref, idx_b_ref[base + s])], axis=0)
            lo, hi = _unpack(w)
            part = jnp.sum(lo * xl + hi * xh, axis=1, keepdims=True)
            acc = jnp.where(pair_lane == s, part, acc)
        out_ref[pl.ds(t, 1), :] = w_ref[pl.ds(t, 1), :] * _gelu(jnp.sum(acc, axis=0, keepdims=True))

    def trip(i, carry):
        for j in range(PEER_U_UNROLL):
            token(i * PEER_U_UNROLL + j)
        return carry

    lax.fori_loop(0, PEER_TT // PEER_U_UNROLL, trip, 0)


N_CHUNK = 2 * ROW_SUB


def _peer_v_kernel(idx_a_ref, idx_b_ref, coef_ref, expand_ref, h_ref, g_ref, tab_ref, out_ref,
                   ce_hi_ref, ce_lo_ref):
    width = PEER_SLOTS * N_CHUNK
    coef = coef_ref[...]
    c_hi = _trunc_bf16(coef)
    expand = expand_ref[...]
    ce_hi_ref[...] = jnp.dot(c_hi.astype(BF16), expand, preferred_element_type=F32)
    ce_lo_ref[...] = jnp.dot((coef - c_hi).astype(BF16), expand, preferred_element_type=F32)
    own_chunk = _own_chunk(width)

    def token(t):
        gathered = _gather_token(idx_a_ref, idx_b_ref, tab_ref, t)
        sel = lambda ref: jnp.where(own_chunk, ref[pl.ds(t, 1), :], 0.0).astype(BF16)
        c = jnp.concatenate([sel(ce_hi_ref), sel(ce_lo_ref)], axis=0)
        o = jnp.dot(c, gathered, preferred_element_type=F32)
        o = o[:N_CHUNK] + o[N_CHUNK:]
        out_ref[pl.ds(t, 1), :] = jnp.concatenate([o[r:r + 1, :] for r in range(N_CHUNK)], axis=1)

    def trip(i, carry):
        for j in range(PEER_V_UNROLL):
            token(i * PEER_V_UNROLL + j)
        return carry

    lax.fori_loop(0, PEER_TT // PEER_V_UNROLL, trip, 0)
    out_ref[...] = _rms(h_ref[...] + out_ref[...], g_ref[...])


def _smem_spec():
    return pl.BlockSpec((PEER_TT * PEER_SLOTS // 2,), lambda i: (i,), memory_space=pltpu.SMEM)


def _slot_halves(a):
    return a[:, :PEER_SLOTS // 2].reshape(-1), a[:, PEER_SLOTS // 2:].reshape(-1)


def _slot_columns():
    half = PEER_SLOTS // 2
    block = np.arange(PEER_SLOTS * N_CHUNK) // N_CHUNK
    col_slot = block // 2 + half * (block % 2)
    return jnp.asarray(col_slot[None, :] == np.arange(PEER_SLOTS)[:, None], dtype=BF16)


def _table_spec(rows):
    return pl.BlockSpec((rows, 128), lambda i: (0, 0), pipeline_mode=pl.Buffered(1))


def peer_u(idx, x, w, tab):
    t = x.shape[0]
    return pl.pallas_call(
        _peer_u_kernel,
        grid=(t // PEER_TT,),
        in_specs=[_smem_spec(), _smem_spec(), pl.BlockSpec((PEER_TT, D_MODEL), lambda i: (i, 0)),
                  pl.BlockSpec((PEER_TT, PEER_SLOTS), lambda i: (i, 0)), _table_spec(tab.shape[0])],
        out_specs=pl.BlockSpec((PEER_TT, PEER_SLOTS), lambda i: (i, 0)),
        out_shape=jax.ShapeDtypeStruct((t, PEER_SLOTS), jnp.float32),
        compiler_params=pltpu.CompilerParams(vmem_limit_bytes=VMEM_LIMIT_PEER),
        name="peer_u",
    )(*_slot_halves(idx), x, w, tab)


def peer_v(idx, coef, tab, h, g):
    t = idx.shape[0]
    width = PEER_SLOTS * N_CHUNK
    expand = _slot_columns()
    return pl.pallas_call(
        _peer_v_kernel,
        grid=(t // PEER_TT,),
        in_specs=[_smem_spec(), _smem_spec(), pl.BlockSpec((PEER_TT, PEER_SLOTS), lambda i: (i, 0)),
                  pl.BlockSpec((PEER_SLOTS, width), lambda i: (0, 0)),
                  pl.BlockSpec((PEER_TT, D_MODEL), lambda i: (i, 0)), pl.BlockSpec((1, D_MODEL), lambda i: (0, 0)),
                  _table_spec(tab.shape[0])],
        out_specs=pl.BlockSpec((PEER_TT, D_MODEL), lambda i: (i, 0)),
        out_shape=jax.ShapeDtypeStruct((t, D_MODEL), jnp.float32),
        scratch_shapes=[pltpu.VMEM((PEER_TT, width), F32), pltpu.VMEM((PEER_TT, width), F32)],
        compiler_params=pltpu.CompilerParams(vmem_limit_bytes=VMEM_LIMIT_PEER),
        name="peer_v",
    )(*_slot_halves(idx), coef, expand, h, g.reshape(1, D_MODEL), tab)


PK_TT = 128
PK_A_BLOCKS = ((0, 16), (1, 8), (2, 5), (3, 4))
PK_B_BLOCKS = ((0, 4, 16), (1, 4, 8), (2, 4, 5))
NEG_HUGE = -3.0e38
PK_HEADS_PER_TRIP = 8


def _extract_top(problems, flat, n):
    vals = [[] for _ in problems]
    picked = [[] for _ in problems]
    cur = [v for v, _ in problems]
    for _ in range(n):
        for i, (_, payload) in enumerate(problems):
            m = jnp.max(cur[i], axis=0, keepdims=True)
            pos = jnp.min(jnp.where(cur[i] == m, flat, 1e9), axis=0, keepdims=True)
            hit = flat == pos
            picked[i].append(pos if payload is None else jnp.sum(jnp.where(hit, payload, 0.0), axis=0, keepdims=True))
            cur[i] = jnp.where(hit, NEG_HUGE, cur[i])
            vals[i].append(m)
    return [jnp.concatenate(v, axis=0) for v in vals], [jnp.concatenate(p, axis=0) for p in picked]


def _candidate_grid(first, second, combine):
    up8 = lambda n: -(-n // 8) * 8
    return jnp.concatenate([combine(first[a:a + 1], second[0:up8(nb)]) for a, nb in PK_A_BLOCKS]
                           + [combine(first[0:up8(a1)], second[b:b + 1]) for b, _, a1 in PK_B_BLOCKS], axis=0)


def _candidate_valid_and_flat():
    up8 = lambda n: -(-n // 8) * 8
    valid, flat = [], []
    for a, nb in PK_A_BLOCKS:
        r = lax.broadcasted_iota(jnp.int32, (up8(nb), PK_TT), 0)
        valid.append(r < nb)
        flat.append(a * PEER_TOPK + r)
    for b, a0, a1 in PK_B_BLOCKS:
        r = lax.broadcasted_iota(jnp.int32, (up8(a1), PK_TT), 0)
        valid.append((r >= a0) & (r < a1))
        flat.append(r * PEER_TOPK + b)
    valid = jnp.concatenate(valid, axis=0)
    return valid, jnp.where(valid, jnp.concatenate(flat, axis=0).astype(F32), 1e9)


def _peer_topk_kernel(q_ref, ka_ref, eid_ref, w_ref):
    key_row = lax.broadcasted_iota(jnp.int32, (N_KEYS, PK_TT), 0).astype(F32)
    valid, flat = _candidate_valid_and_flat()

    def heads(i, carry):
        scores = []
        for j in range(2 * PK_HEADS_PER_TRIP):
            q = q_ref[2 * PK_HEADS_PER_TRIP * i + j]
            q_hi = _trunc_bf16(q)
            qa = jnp.concatenate([q_hi.astype(BF16), (q - q_hi).astype(BF16), q_hi.astype(BF16)], axis=1)
            scores.append(lax.dot_general(ka_ref[2 * PK_HEADS_PER_TRIP * i + j], qa, (((1,), (1,)), ((), ())),
                                          preferred_element_type=F32))
        sub_s, sub_i = _extract_top([(s, None) for s in scores], key_row, PEER_TOPK)
        cands = []
        for j in range(PK_HEADS_PER_TRIP):
            s1, s2, i1, i2 = sub_s[2 * j], sub_s[2 * j + 1], sub_i[2 * j], sub_i[2 * j + 1]
            cands.append((jnp.where(valid, _candidate_grid(s1, s2, lambda x, y: x + y), NEG_HUGE),
                          _candidate_grid(i1, i2, lambda x, y: x * N_KEYS + y)))
        top_s, top_e = _extract_top(cands, flat, PEER_TOPK)
        for j in range(PK_HEADS_PER_TRIP):
            e = jnp.exp(top_s[j] - top_s[j][0:1])
            w_ref[PK_HEADS_PER_TRIP * i + j] = e / jnp.sum(e, axis=0, keepdims=True)
            eid_ref[PK_HEADS_PER_TRIP * i + j] = top_e[j].astype(jnp.int32) * ROW_SUB
        return carry

    lax.fori_loop(0, PEER_HEADS // PK_HEADS_PER_TRIP, heads, 0)


def peer_topk(qt, subkeys):
    hp, t, d = qt.shape
    k_hi, k_lo = _split2(subkeys.reshape(hp, N_KEYS, d))
    ka = jnp.concatenate([k_hi, k_hi, k_lo], axis=2)
    eid, w = pl.pallas_call(
        _peer_topk_kernel,
        grid=(t // PK_TT,),
        in_specs=[pl.BlockSpec((hp, PK_TT, d), lambda i: (0, i, 0)),
                  pl.BlockSpec((hp, N_KEYS, 3 * d), lambda i: (0, 0, 0))],
        out_specs=[pl.BlockSpec((PEER_HEADS, PEER_TOPK, PK_TT), lambda i: (0, 0, i)),
                   pl.BlockSpec((PEER_HEADS, PEER_TOPK, PK_TT), lambda i: (0, 0, i))],
        out_shape=[jax.ShapeDtypeStruct((PEER_HEADS, PEER_TOPK, t), jnp.int32),
                   jax.ShapeDtypeStruct((PEER_HEADS, PEER_TOPK, t), F32)],
        name="peer_topk",
    )(qt, ka)
    to_slots = lambda a: a.transpose(2, 0, 1).reshape(t, PEER_SLOTS)
    return to_slots(eid), to_slots(w)


def split_columns(t, sizes):
    offs = np.cumsum((0,) + tuple(sizes))
    return [t[..., int(a):int(b)] for a, b in zip(offs[:-1], offs[1:])]


def rope_tables(pos):
    inv = jnp.power(ROPE_THETA, -jnp.arange(0, ROPE_DIM, 2, dtype=jnp.float32) / ROPE_DIM)
    ang = pos[:, None] * inv[None, :]
    return jnp.cos(ang), jnp.sin(ang)


def apply_partial_rope(x, cos, sin):
    half = ROPE_DIM // 2
    xr = x[..., :ROPE_DIM].astype(jnp.float32)
    x1, x2 = xr[..., :half], xr[..., half:]
    c, s = cos[:, None, :], sin[:, None, :]
    rot = jnp.concatenate([x1 * c - x2 * s, x1 * s + x2 * c], axis=-1)
    return jnp.concatenate([rot.astype(x.dtype), x[..., ROPE_DIM:]], axis=-1)


LOG2_E = 1.4426950408889634
LOGIT_SCALE = HEAD_DIM ** -0.5 * LOG2_E


def _logits(ks, qts):
    return [jnp.dot(k, qt, preferred_element_type=F32) for k, qt in zip(ks, qts)]


def _flash_update(ss, vts, mask, carries):
    stats, ps = [], []
    for s, (m, l, _) in zip(ss, carries):
        s_vis = s if mask is None else jnp.where(mask, s, NEG_INF)
        m_new = jnp.maximum(m, jnp.max(s_vis, axis=0, keepdims=True))
        alpha = jnp.exp2(m - m_new)
        p = jnp.exp2(s - m_new)
        if mask is not None:
            p = jnp.where(mask, p, 0.0)
        stats.append((m_new, alpha, alpha * l + jnp.sum(p, axis=0, keepdims=True)))
        ps.append(p.astype(BF16))
    return tuple((m_new, l, alpha * acc + jnp.dot(vt, p, preferred_element_type=F32))
                 for (m_new, alpha, l), p, vt, (_, _, acc) in zip(stats, ps, vts, carries))


def _causal_flash(key_tile, value_tile, qts, n_full, last_mask, n):
    def body(kt, carries):
        return _flash_update(_logits(key_tile(kt), qts), value_tile(kt), None, carries)

    carries = lax.fori_loop(0, n_full, body, tuple(_flash_init(n) for _ in qts))
    return _flash_update(_logits(key_tile(n_full), qts), value_tile(n_full), last_mask, carries)


def _flash_init(n):
    return (jnp.full((1, n), NEG_INF, F32), jnp.zeros((1, n), F32), jnp.zeros((HEAD_DIM, n), F32))


FOX_TQ = 256
FOX_KV = 512
FOX_DK = 256


FOX_HP = 4


def _fox_kernel(qt_ref, k_ref, vt_ref, o_ref):
    qi = pl.program_id(1)
    q0 = qi * FOX_TQ
    qts = [qt_ref[h] for h in range(FOX_HP)]
    qpos = q0 + lax.broadcasted_iota(jnp.int32, (1, FOX_TQ), 1)
    krow = lax.broadcasted_iota(jnp.int32, (FOX_KV, FOX_TQ), 0)

    n_full = q0 // FOX_KV
    causal = (n_full * FOX_KV + krow) <= qpos
    carry = _causal_flash(lambda kt: [k_ref[h, kt] for h in range(FOX_HP)],
                          lambda kt: [vt_ref[h, kt] for h in range(FOX_HP)], qts, n_full, causal, FOX_TQ)
    for h in range(FOX_HP):
        _, l, acc = carry[h]
        o_ref[h] = acc / l


def _trunc_bf16(x):
    bits = lax.bitcast_convert_type(x, jnp.uint32) & jnp.uint32(0xFFFF0000)
    return lax.bitcast_convert_type(bits, F32)


def _split3(c):
    c1 = _trunc_bf16(c)
    r = c - c1
    c2 = _trunc_bf16(r)
    return c1.astype(BF16), c2.astype(BF16), (r - c2).astype(BF16)


def _split2(x):
    hi = _trunc_bf16(x)
    return hi.astype(BF16), (x - hi).astype(BF16)


def _aug_q(q):
    hi, lo = _split2(q)
    return jnp.concatenate([hi, hi, lo], axis=-1)


def _aug_k(k):
    hi, lo = _split2(k)
    return jnp.concatenate([hi, lo, hi], axis=-1)


def fox_attention(q, k, v, log_f):
    B_, S_, H, Dh = q.shape
    KV_TILE = FOX_KV
    n_qt, n_kt = S_ // FOX_TQ, S_ // KV_TILE
    c = jnp.cumsum(log_f, axis=1)
    c1, c2, c3 = _split3(c * LOG2_E)
    j = jnp.arange(FOX_DK - 3 * Dh)
    pick = lambda t: t[..., None].astype(F32)
    terms = lambda first: jnp.where(j == first, pick(c1), jnp.where(j == first + 1, pick(c2), pick(c3)))
    q_extra = jnp.where(j < 3, 1.0, jnp.where(j < 6, terms(3), 0.0)).astype(BF16)
    k_extra = jnp.where(j < 3, -terms(0), jnp.where(j < 6, 1.0, 0.0)).astype(BF16)
    qa = jnp.concatenate([_aug_q(q * LOGIT_SCALE), q_extra], axis=-1)
    ka = jnp.concatenate([_aug_k(k), k_extra], axis=-1)
    HP, HG = FOX_HP, H // FOX_HP
    qt = qa.reshape(B_, n_qt, FOX_TQ, HG, HP, FOX_DK).transpose(0, 3, 1, 4, 5, 2).reshape(B_ * HG * n_qt, HP, FOX_DK, FOX_TQ)
    kk = ka.reshape(B_, n_kt, KV_TILE, HG, HP, FOX_DK).transpose(0, 3, 4, 1, 2, 5).reshape(B_ * HG, HP, n_kt, KV_TILE, FOX_DK)
    vt = v.astype(BF16).reshape(B_, n_kt, KV_TILE, HG, HP, Dh).transpose(0, 3, 4, 1, 5, 2).reshape(B_ * HG, HP, n_kt, Dh, KV_TILE)
    out = pl.pallas_call(
        _fox_kernel,
        grid=(B_ * HG, n_qt),
        in_specs=[pl.BlockSpec((None, HP, FOX_DK, FOX_TQ), lambda b, i: (b * n_qt + i, 0, 0, 0)),
                  pl.BlockSpec((None, HP, n_kt, KV_TILE, FOX_DK), lambda b, i: (b, 0, 0, 0, 0)),
                  pl.BlockSpec((None, HP, n_kt, Dh, KV_TILE), lambda b, i: (b, 0, 0, 0, 0))],
        out_specs=pl.BlockSpec((None, HP, Dh, FOX_TQ), lambda b, i: (b * n_qt + i, 0, 0, 0)),
        out_shape=jax.ShapeDtypeStruct((B_ * HG * n_qt, HP, Dh, FOX_TQ), F32),
        name="fox_attention",
    )(qt, kk, vt)
    return out.reshape(B_, HG, n_qt, HP, Dh, FOX_TQ)


NSA_TQ = 256
NSA_N = NSA_HPG * NSA_TQ
NSA_CH = 256
N_CMP_PAD = 256
N_SLC = 64
NSA_DK = 3 * HEAD_DIM
NSA_KV = 512
WIN_TILES = WINDOW // NSA_TQ + 1
WIN_KEYS = WIN_TILES * NSA_TQ


def _nsa_kernel(qt_ref, kc_ref, vct_ref, ks_ref, vst_ref, kw_ref, vwt_ref, gate_ref, ovl_ref, o_ref):
    qi = pl.program_id(1)
    q0 = qi * NSA_TQ
    qt = jnp.concatenate([qt_ref[h] for h in range(NSA_HPG)], axis=1)
    lane = lax.broadcasted_iota(jnp.int32, (1, NSA_N), 1)
    qpos = q0 + (lane & (NSA_TQ - 1))

    hn = NSA_CH
    n_ch = NSA_N // NSA_CH
    qts = [qt[:, c * hn:(c + 1) * hn] for c in range(n_ch)]
    qpos_h = qpos[:, :hn]

    w0 = jnp.maximum(qi - WINDOW // NSA_TQ, 0)
    k_w = jnp.concatenate([kw_ref[w0 + i] for i in range(WIN_TILES)], axis=0)
    vt_w = jnp.concatenate([vwt_ref[w0 + i] for i in range(WIN_TILES)], axis=1)
    s = jnp.dot(kc_ref[...], qt, preferred_element_type=F32)
    s_w = [jnp.dot(k_w, qts[c], preferred_element_type=F32) for c in range(n_ch)]

    cmp_end = lax.broadcasted_iota(jnp.int32, (N_CMP_PAD, NSA_N), 0) * CMP_STRIDE + (CMP_LEN - 1)
    mc = cmp_end <= qpos
    m = jnp.max(jnp.where(mc, s, NEG_INF), axis=0, keepdims=True)
    p = jnp.where(mc, jnp.exp2(s - m), 0.0)
    l = jnp.sum(p, axis=0, keepdims=True)
    pc = p * jnp.where(l > 0.0, 1.0 / l, 0.0)
    o_cmp = jnp.dot(vct_ref[...], pc.astype(BF16), preferred_element_type=F32)
    pcs = sum(pc[:, h * NSA_TQ:(h + 1) * NSA_TQ] for h in range(NSA_HPG))
    pcs_hi = pcs.astype(BF16)
    pcs_lo = (pcs - pcs_hi.astype(F32)).astype(BF16)
    ovl = ovl_ref[...]
    imp = jnp.dot(ovl, pcs_hi, preferred_element_type=F32) + jnp.dot(ovl, pcs_lo, preferred_element_type=F32)

    rel = qpos_h - (w0 * NSA_TQ + lax.broadcasted_iota(jnp.int32, (WIN_KEYS, hn), 0))
    in_win = (rel >= 0) & (rel < WINDOW)
    outs_w = []
    for c in range(n_ch):
        m = jnp.max(jnp.where(in_win, s_w[c], NEG_INF), axis=0, keepdims=True)
        p = jnp.where(in_win, jnp.exp2(s_w[c] - m), 0.0)
        l = jnp.sum(p, axis=0, keepdims=True)
        outs_w.append(jnp.dot(vt_w, p.astype(BF16), preferred_element_type=F32) / l)
    acc_w = jnp.concatenate(outs_w, axis=1)

    blk = lax.broadcasted_iota(jnp.int32, (N_SLC, NSA_TQ), 0)
    q_blk = (q0 + lax.broadcasted_iota(jnp.int32, (N_SLC, NSA_TQ), 1)) >> 6
    forced = (blk == 0) | (blk == q_blk) | (blk == q_blk - 1)
    score = jnp.where(forced, FORCED_SCORE, jnp.where(blk <= q_blk, imp, -1.0))
    rank = jnp.zeros((N_SLC, NSA_TQ), jnp.int32)
    for i in range(N_SLC):
        row = score[i:i + 1, :]
        beats = (row > score) | ((row == score) & (blk > i))
        rank = rank + beats.astype(jnp.int32)
    bias = jnp.where(rank < SLC_TOP, 0.0, NEG_INF).astype(BF16)
    bias = jnp.concatenate([bias] * (hn // NSA_TQ), axis=1)
    qts_s = [jnp.concatenate([qh, bias], axis=0) for qh in qts]
    krow = lax.broadcasted_iota(jnp.int32, (NSA_KV, hn), 0)

    n_full = q0 // NSA_KV
    causal = (n_full * NSA_KV + krow) <= qpos_h
    slc = _causal_flash(lambda kt: [ks_ref[kt]] * n_ch, lambda kt: [vst_ref[kt]] * n_ch, qts_s, n_full, causal, hn)
    acc_s = jnp.concatenate([acc / l for _, l, acc in slc], axis=1)

    g = jax.nn.sigmoid(gate_ref[...])
    o_ref[...] = g[0:1] * o_cmp + g[1:2] * acc_s + g[2:3] * acc_w


def nsa_attention(q, k_cmp, v_cmp, k_slc, v_slc, k_win, v_win, gate_logit):
    B_, S_, H, Dh = q.shape
    G = NSA_GROUPS
    n_qt = S_ // NSA_TQ
    n_cmp = k_cmp.shape[1]
    cs = np.arange(N_CMP_PAD)[None, :] * CMP_STRIDE
    ss = np.arange(N_SLC)[:, None] * SLC_LEN
    ov = np.clip(np.minimum(cs + CMP_LEN, ss + SLC_LEN) - np.maximum(cs, ss), 0, None) / CMP_LEN
    ov[:, n_cmp:] = 0.0
    ovl = jnp.asarray(ov, dtype=BF16)
    qs = _aug_q(q * LOGIT_SCALE).reshape(B_, n_qt, NSA_TQ, G, NSA_HPG, NSA_DK)
    qt = qs.transpose(0, 3, 1, 4, 5, 2).reshape(B_ * G * n_qt, NSA_HPG, NSA_DK, NSA_TQ)
    gt = gate_logit.astype(F32).reshape(B_, n_qt, NSA_TQ, G, NSA_HPG, N_NSA_BRANCH)
    gt = gt.transpose(0, 3, 1, 5, 4, 2).reshape(B_ * G * n_qt, N_NSA_BRANCH, NSA_N)
    tiles = lambda t, kv: t.reshape(B_, S_ // kv, kv, G, -1).transpose(0, 3, 1, 2, 4).reshape(B_ * G, S_ // kv, kv, t.shape[-1])
    keys = lambda t, kv: tiles(_aug_k(t), kv)
    block_onehot = (jnp.arange(S_)[:, None] // SLC_LEN == jnp.arange(N_SLC)[None, :]).astype(BF16)
    k_slc_aug = jnp.concatenate([_aug_k(k_slc), jnp.broadcast_to(block_onehot[None, :, None, :], (B_, S_, G, N_SLC))], axis=-1)
    vals = lambda t, kv: t.astype(BF16).reshape(B_, S_ // kv, kv, G, Dh).transpose(0, 3, 1, 4, 2).reshape(B_ * G, S_ // kv, Dh, kv)
    padc = ((0, 0), (0, N_CMP_PAD - n_cmp), (0, 0), (0, 0))
    kc = _aug_k(jnp.pad(k_cmp, padc)).transpose(0, 2, 1, 3).reshape(B_ * G, N_CMP_PAD, NSA_DK)
    vct = jnp.pad(v_cmp, padc).astype(BF16).transpose(0, 2, 3, 1).reshape(B_ * G, Dh, N_CMP_PAD)
    per_bg = lambda *blk: pl.BlockSpec((None,) + blk, lambda b, i: (b,) + (0,) * len(blk))
    per_tile = lambda *blk: pl.BlockSpec((None,) + blk, lambda b, i: (b * n_qt + i,) + (0,) * len(blk))
    out = pl.pallas_call(
        _nsa_kernel,
        grid=(B_ * G, n_qt),
        in_specs=[per_tile(NSA_HPG, NSA_DK, NSA_TQ),
                  per_bg(N_CMP_PAD, NSA_DK), per_bg(Dh, N_CMP_PAD),
                  per_bg(S_ // NSA_KV, NSA_KV, NSA_DK + N_SLC), per_bg(S_ // NSA_KV, Dh, NSA_KV),
                  per_bg(n_qt, NSA_TQ, NSA_DK), per_bg(n_qt, Dh, NSA_TQ),
                  per_tile(N_NSA_BRANCH, NSA_N),
                  pl.BlockSpec((N_SLC, N_CMP_PAD), lambda b, i: (0, 0))],
        out_specs=per_tile(Dh, NSA_N),
        out_shape=jax.ShapeDtypeStruct((B_ * G * n_qt, Dh, NSA_N), F32),
        name="nsa_attention",
    )(qt, kc, vct, tiles(k_slc_aug, NSA_KV), vals(v_slc, NSA_KV), keys(k_win, NSA_TQ), vals(v_win, NSA_TQ), gt, ovl)
    return out.reshape(B_, G, n_qt, Dh, NSA_N)


DENSE_TM = 256
SEC_MERGE = N_BRANCH * D_MODEL
SEC_SMALL = 256
IN_SECTIONS = (BRANCH_WIDTH,) * 4 + (KV_WIDTH,) * 6 + (SEC_MERGE, SEC_SMALL)


def _rms(x, g):
    return x * lax.rsqrt(jnp.mean(x * x, axis=-1, keepdims=True) + RMS_EPS) * g


def _row_spec(width):
    return pl.BlockSpec((DENSE_TM, width), lambda i: (i, 0))


def _whole_spec(shape):
    return pl.BlockSpec(shape, lambda i: (0,) * len(shape))


def _in_proj_kernel(x_ref, g_ref, w_ref, *o_refs):
    xn = _rms(x_ref[...], g_ref[...]).astype(BF16)
    off = 0
    for o_ref, width in zip(o_refs, IN_SECTIONS):
        o_ref[...] = jnp.dot(xn, w_ref[:, off:off + width], preferred_element_type=F32)
        off += width


def in_projection(x, g, w_in):
    t, d = x.shape
    fq, fk, fv, fl, nq, kc, vc, ksl, vsl, kwn, vwn, gl, ml = split_columns(w_in, IN_SPLITS)
    pad = lambda w: jnp.pad(w, ((0, 0), (0, 128 - w.shape[1])))
    w = jnp.concatenate([fq, fk, fv, nq, kc, vc, ksl, vsl, kwn, vwn, ml, pad(fl), pad(gl)], axis=1).astype(BF16)
    return pl.pallas_call(
        _in_proj_kernel,
        grid=(t // DENSE_TM,),
        in_specs=[_row_spec(d), _whole_spec((1, d)), _whole_spec(w.shape)],
        out_specs=[_row_spec(s) for s in IN_SECTIONS],
        out_shape=[jax.ShapeDtypeStruct((t, s), F32) for s in IN_SECTIONS],
        name="in_projection",
    )(x, g.reshape(1, d), w)


def _merge_kernel(x_ref, yf_ref, yn_ref, ml_ref, wb_ref, wo_ref, gq_ref, wq_ref, o_ref, hn_ref, q_ref):
    yf_t = yf_ref[...].reshape(BRANCH_WIDTH, DENSE_TM)
    yn_t = jnp.concatenate(
        [jnp.concatenate([yn_ref[g, j, :, h * NSA_TQ:(h + 1) * NSA_TQ]
                          for g in range(NSA_GROUPS) for h in range(NSA_HPG)], axis=0)
         for j in range(DENSE_TM // NSA_TQ)], axis=1)
    g = jax.nn.sigmoid(ml_ref[...])
    up_f = jnp.dot(yf_t.T.astype(BF16), wb_ref[0], preferred_element_type=F32)
    up_n = jnp.dot(yn_t.T.astype(BF16), wb_ref[1], preferred_element_type=F32)
    merged = g[:, :D_MODEL] * up_f + g[:, D_MODEL:] * up_n
    h = x_ref[...] + jnp.dot(merged.astype(BF16), wo_ref[...], preferred_element_type=F32)
    o_ref[...] = h
    hn = _rms(h, gq_ref[...])
    hn_ref[...] = hn
    q = jnp.dot(hn.astype(BF16), wq_ref[...], preferred_element_type=F32)
    dq = PEER_QDIM // 2
    for j in range(2 * PEER_HEADS):
        q_ref[j] = q[:, j * dq:(j + 1) * dq]


def merge_out(x, y_fox, y_nsa, merge_logit, w_branch, w_out, norm_ffn, wq):
    t, d = x.shape
    hp, dq = 2 * PEER_HEADS, PEER_QDIM // 2
    assert FOX_TQ == DENSE_TM and DENSE_TM % NSA_TQ == 0
    tiles_per_seq = y_fox.shape[2]
    nsa_per_tile = DENSE_TM // NSA_TQ
    yf_spec = pl.BlockSpec((None,) + y_fox.shape[1:2] + (None,) + y_fox.shape[3:],
                           lambda i: (i // tiles_per_seq, 0, i % tiles_per_seq, 0, 0, 0))
    yn_spec = pl.BlockSpec((None, NSA_GROUPS, nsa_per_tile, HEAD_DIM, NSA_N),
                           lambda i: (i // tiles_per_seq, 0, i % tiles_per_seq, 0, 0))
    return pl.pallas_call(
        _merge_kernel,
        grid=(t // DENSE_TM,),
        in_specs=[_row_spec(d), yf_spec, yn_spec, _row_spec(SEC_MERGE),
                  _whole_spec(w_branch.shape), _whole_spec(w_out.shape), _whole_spec((1, d)), _whole_spec(wq.shape)],
        out_specs=[_row_spec(d), _row_spec(d), pl.BlockSpec((hp, DENSE_TM, dq), lambda i: (0, i, 0))],
        out_shape=[jax.ShapeDtypeStruct((t, d), F32), jax.ShapeDtypeStruct((t, d), F32),
                   jax.ShapeDtypeStruct((hp, t, dq), F32)],
        name="merge_out",
    )(x, y_fox, y_nsa, merge_logit, w_branch.astype(BF16), w_out.astype(BF16), norm_ffn.reshape(1, d),
      wq.astype(BF16))


N_CMP_ROWS = 256


def _compress_kernel(r_ref, pos_ref, w1_ref, w2_ref, o_ref):
    r = r_ref[...]
    nxt = jnp.concatenate([r[1:], jnp.zeros((1, r.shape[1]), F32)], axis=0)
    half = r.shape[1]
    hid = (jnp.dot((r + pos_ref[0:1, :]).astype(BF16), w1_ref[:half], preferred_element_type=F32)
           + jnp.dot((nxt + pos_ref[1:2, :]).astype(BF16), w1_ref[half:], preferred_element_type=F32))
    o_ref[...] = jnp.dot(_gelu(hid).astype(BF16), w2_ref[...], preferred_element_type=F32)


def compress_kv_pallas(kv, cmp_pos, w1, w2):
    _, B_, S_, G, Dh = kv.shape
    n_cmp = (S_ - CMP_LEN) // CMP_STRIDE + 1
    width = CMP_STRIDE * Dh
    rows = kv.transpose(0, 1, 3, 2, 4).reshape(2, B_ * G, S_ // CMP_STRIDE, width)
    out = pl.pallas_call(
        _compress_kernel,
        grid=(2, B_ * G),
        in_specs=[pl.BlockSpec((None, None, N_CMP_ROWS, width), lambda j, b: (j, b, 0, 0)),
                  pl.BlockSpec((None, 2, width), lambda j, b: (j, 0, 0)),
                  pl.BlockSpec((None, CMP_LEN * Dh, CMP_HIDDEN), lambda j, b: (j, 0, 0)),
                  pl.BlockSpec((None, CMP_HIDDEN, Dh), lambda j, b: (j, 0, 0))],
        out_specs=pl.BlockSpec((None, None, N_CMP_ROWS, Dh), lambda j, b: (j, b, 0, 0)),
        out_shape=jax.ShapeDtypeStruct((2, B_ * G, N_CMP_ROWS, Dh), F32),
        name="compress_kv",
    )(rows, cmp_pos.reshape(2, 2, width), w1.astype(BF16), w2.astype(BF16))
    return out[:, :, :n_cmp].reshape(2, B_, G, n_cmp, Dh).transpose(0, 1, 3, 2, 4)


def hybrid_mixer(x, norm_g, w_in, fox_f_bias, cmp_pos, cmp_w1, cmp_w2, w_branch, w_out, norm_ffn, wq):
    B_, S_, D = x.shape
    xf = x.reshape(B_ * S_, D)
    secs = in_projection(xf, norm_g, w_in)
    heads = lambda t, n: t.reshape(B_, S_, n, HEAD_DIM)
    fq, fk, fv, nq = secs[:4]
    kc, vc, ksl, vsl, kwn, vwn = (heads(t, NSA_GROUPS) for t in secs[4:10])
    merge_logit, small = secs[10:]
    f_logit = small[:, :FOX_HEADS].reshape(B_, S_, FOX_HEADS)
    gate_logit = small[:, 128:128 + NSA_HEADS * N_NSA_BRANCH].reshape(B_, S_, NSA_HEADS, N_NSA_BRANCH)
    log_f = jax.nn.log_sigmoid(f_logit + fox_f_bias.astype(F32))
    y_fox = fox_attention(heads(fq, FOX_HEADS), heads(fk, FOX_HEADS), heads(fv, FOX_HEADS), log_f)
    cos, sin = rope_tables(jnp.arange(S_, dtype=F32))
    q_nsa = apply_partial_rope(heads(nq, NSA_HEADS), cos, sin)
    k_slc = apply_partial_rope(ksl, cos, sin)
    k_win = apply_partial_rope(kwn, cos, sin)
    cmp = compress_kv_pallas(jnp.stack([kc, vc]), cmp_pos, cmp_w1, cmp_w2)
    n_cmp = cmp.shape[2]
    cmp_end = jnp.arange(n_cmp, dtype=F32) * CMP_STRIDE + (CMP_LEN - 1)
    k_cmp = apply_partial_rope(cmp[0], *rope_tables(cmp_end))
    y_nsa = nsa_attention(q_nsa, k_cmp, cmp[1], k_slc, vsl, k_win, vwn, gate_logit)
    return merge_out(xf, y_fox, y_nsa, merge_logit, w_branch, w_out, norm_ffn, wq)


def peer_ffn(h, hn, q, subkeys, u, v, norm_final):
    idx2, w = peer_topk(q, subkeys)
    coef = peer_u(idx2, hn, w, pack_table(u))
    return peer_v(idx2, coef, pack_table(v), h, norm_final)


def kernel(x, norm_mix, w_in, fox_f_bias, nsa_cmp_pos, nsa_cmp_w1, nsa_cmp_w2, w_branch, w_out,
           norm_ffn, peer_wq, peer_subkeys, peer_u, peer_v, norm_final):
    B_, S_, D = x.shape
    assert norm_mix.shape[0] == 1, "single-layer trunk"
    h, hn, q = hybrid_mixer(x, norm_mix[0], w_in[0], fox_f_bias[0], nsa_cmp_pos[0], nsa_cmp_w1[0], nsa_cmp_w2[0],
                            w_branch[0], w_out[0], norm_ffn[0], peer_wq[0])
    out = peer_ffn(h, hn, q, peer_subkeys[0], peer_u[0], peer_v[0], norm_final)
    return out.reshape(B_, S_, D)
```

```python
import jax
import jax.numpy as jnp
import numpy as np
from jax import lax
from jax.experimental import pallas as pl
from jax.experimental.pallas import tpu as pltpu

D_MODEL = 1024
HEAD_DIM = 64
FOX_HEADS = 8
NSA_HEADS = 8
NSA_GROUPS = 2
NSA_HPG = NSA_HEADS // NSA_GROUPS
BRANCH_WIDTH = 512
N_BRANCH = 2
N_NSA_BRANCH = 3
ROPE_DIM = HEAD_DIM // 4
ROPE_THETA = 500000.0
CMP_LEN = 32
CMP_STRIDE = 16
CMP_HIDDEN = 2 * HEAD_DIM
SLC_LEN = 64
SLC_TOP = 16
WINDOW = 512
FORCED_SCORE = 1e9
NEG_INF = -1e30
PEER_HEADS = 8
N_KEYS = 128
PEER_QDIM = 256
PEER_TOPK = 16
RMS_EPS = 1e-6
KV_WIDTH = NSA_GROUPS * HEAD_DIM
IN_SPLITS = (BRANCH_WIDTH, BRANCH_WIDTH, BRANCH_WIDTH, FOX_HEADS, BRANCH_WIDTH,
             KV_WIDTH, KV_WIDTH, KV_WIDTH, KV_WIDTH, KV_WIDTH, KV_WIDTH,
             NSA_HEADS * N_NSA_BRANCH, N_BRANCH * D_MODEL)


BF16 = jnp.bfloat16
F32 = jnp.float32

PEER_SLOTS = PEER_HEADS * PEER_TOPK
PEER_TT = 128
PEER_CHUNK_SLOTS = 32
PEER_U_UNROLL = 32
PEER_V_UNROLL = 32
ROW_SUB = D_MODEL // 2 // 128
VMEM_LIMIT_PEER = N_KEYS * N_KEYS * D_MODEL * 2 + 16 * 1024 * 1024


def pack_table(tab):
    n, d = tab.shape
    return pl.pallas_call(
        _pack_kernel,
        grid=(n // PACK_TM,),
        in_specs=[pl.BlockSpec((PACK_TM, d), lambda i: (i, 0))],
        out_specs=pl.BlockSpec((PACK_TM * ROW_SUB, 128), lambda i: (i, 0)),
        out_shape=jax.ShapeDtypeStruct((n * ROW_SUB, 128), jnp.uint32),
        name="pack_table",
    )(tab)


PACK_TM = 256


def _pack_kernel(x_ref, o_ref):
    bits = lax.bitcast_convert_type(x_ref[...].astype(jnp.bfloat16).astype(jnp.float32), jnp.uint32)
    for r in range(ROW_SUB):
        even = bits[:, (2 * r) * 128:(2 * r + 1) * 128]
        odd = bits[:, (2 * r + 1) * 128:(2 * r + 2) * 128]
        o_ref[pl.ds(r, PACK_TM, stride=ROW_SUB), :] = (even >> 16) | (odd & jnp.uint32(0xFFFF0000))


def _expert_slab(tab_ref, row):
    return tab_ref[pl.ds(pl.multiple_of(row, ROW_SUB), ROW_SUB), :]


def _unpack(w):
    lo = lax.bitcast_convert_type(w << 16, jnp.float32)
    hi = lax.bitcast_convert_type(w & jnp.uint32(0xFFFF0000), jnp.float32)
    return lo, hi


def _gelu(x):
    return 0.5 * x * (1.0 + lax.erf(x * (2.0 ** -0.5)))


def _gather_token(idx_a_ref, idx_b_ref, tab_ref, t):
    base = t * (PEER_SLOTS // 2)
    slabs = [jnp.concatenate([_expert_slab(tab_ref, idx_a_ref[base + s]),
                              _expert_slab(tab_ref, idx_b_ref[base + s])], axis=0) for s in range(PEER_SLOTS // 2)]
    return pltpu.bitcast(jnp.concatenate(slabs, axis=0), BF16)


def _own_chunk(width):
    return ((lax.broadcasted_iota(jnp.int32, (N_CHUNK, width), 1) & (N_CHUNK - 1))
            == lax.broadcasted_iota(jnp.int32, (N_CHUNK, width), 0))


def _peer_u_kernel(idx_a_ref, idx_b_ref, x_ref, w_ref, tab_ref, out_ref):
    lane = lax.broadcasted_iota(jnp.int32, (2 * ROW_SUB, 128), 1)
    lower = lax.broadcasted_iota(jnp.int32, (2 * ROW_SUB, 128), 0) >= ROW_SUB
    pair_lane = jnp.where(lower, lane - PEER_SLOTS // 2, lane)

    def token(t):
        row = x_ref[pl.ds(t, 1), :]
        chunks = [row[:, c * 128:(c + 1) * 128] for c in range(2 * ROW_SUB)]
        xl = jnp.concatenate(chunks[0::2] * 2, axis=0)
        xh = jnp.concatenate(chunks[1::2] * 2, axis=0)
        acc = jnp.zeros((2 * ROW_SUB, 128), jnp.float32)
        base = t * (PEER_SLOTS // 2)
        for s in range(PEER_SLOTS // 2):
            w = jnp.concatenate([_expert_slab(tab_ref, idx_a_ref[base + s]),
                                 _expert_slab(tab_ref, idx_b_ref[base + s])], axis=0)
            lo, hi = _unpack(w)
            part = jnp.sum(lo * xl + hi * xh, axis=1, keepdims=True)
            acc = jnp.where(pair_lane == s, part, acc)
        out_ref[pl.ds(t, 1), :] = w_ref[pl.ds(t, 1), :] * _gelu(jnp.sum(acc, axis=0, keepdims=True))

    def trip(i, carry):
        for j in range(PEER_U_UNROLL):
            token(i * PEER_U_UNROLL + j)
        return carry

    lax.fori_loop(0, PEER_TT // PEER_U_UNROLL, trip, 0)


N_CHUNK = 2 * ROW_SUB


def _peer_v_kernel(idx_a_ref, idx_b_ref, coef_ref, expand_ref, h_ref, g_ref, tab_ref, out_ref,
                   ce_hi_ref, ce_lo_ref):
    width = PEER_SLOTS * N_CHUNK
    coef = coef_ref[...]
    c_hi = _trunc_bf16(coef)
    expand = expand_ref[...]
    ce_hi_ref[...] = jnp.dot(c_hi.astype(BF16), expand, preferred_element_type=F32)
    ce_lo_ref[...] = jnp.dot((coef - c_hi).astype(BF16), expand, preferred_element_type=F32)
    own_chunk = _own_chunk(width)

    def token(t):
        gathered = _gather_token(idx_a_ref, idx_b_ref, tab_ref, t)
        sel = lambda ref: jnp.where(own_chunk, ref[pl.ds(t, 1), :], 0.0).astype(BF16)
        c = jnp.concatenate([sel(ce_hi_ref), sel(ce_lo_ref)], axis=0)
        o = jnp.dot(c, gathered, preferred_element_type=F32)
        o = o[:N_CHUNK] + o[N_CHUNK:]
        out_ref[pl.ds(t, 1), :] = jnp.concatenate([o[r:r + 1, :] for r in range(N_CHUNK)], axis=1)

    def trip(i, carry):
        for j in range(PEER_V_UNROLL):
            token(i * PEER_V_UNROLL + j)
        return carry

    lax.fori_loop(0, PEER_TT // PEER_V_UNROLL, trip, 0)
    out_ref[...] = _rms(h_ref[...] + out_ref[...], g_ref[...])


def _smem_spec():
    return pl.BlockSpec((PEER_TT * PEER_SLOTS // 2,), lambda i: (i,), memory_space=pltpu.SMEM)


def _slot_halves(a):
    return a[:, :PEER_SLOTS // 2].reshape(-1), a[:, PEER_SLOTS // 2:].reshape(-1)


def _slot_columns():
    half = PEER_SLOTS // 2
    block = np.arange(PEER_SLOTS * N_CHUNK) // N_CHUNK
    col_slot = block // 2 + half * (block % 2)
    return jnp.asarray(col_slot[None, :] == np.arange(PEER_SLOTS)[:, None], dtype=BF16)


def _table_spec(rows):
    return pl.BlockSpec((rows, 128), lambda i: (0, 0), pipeline_mode=pl.Buffered(1))


def peer_u(idx, x, w, tab):
    t = x.shape[0]
    return pl.pallas_call(
        _peer_u_kernel,
        grid=(t // PEER_TT,),
        in_specs=[_smem_spec(), _smem_spec(), pl.BlockSpec((PEER_TT, D_MODEL), lambda i: (i, 0)),
                  pl.BlockSpec((PEER_TT, PEER_SLOTS), lambda i: (i, 0)), _table_spec(tab.shape[0])],
        out_specs=pl.BlockSpec((PEER_TT, PEER_SLOTS), lambda i: (i, 0)),
        out_shape=jax.ShapeDtypeStruct((t, PEER_SLOTS), jnp.float32),
        compiler_params=pltpu.CompilerParams(vmem_limit_bytes=VMEM_LIMIT_PEER),
        name="peer_u",
    )(*_slot_halves(idx), x, w, tab)


def peer_v(idx, coef, tab, h, g):
    t = idx.shape[0]
    width = PEER_SLOTS * N_CHUNK
    expand = _slot_columns()
    return pl.pallas_call(
        _peer_v_kernel,
        grid=(t // PEER_TT,),
        in_specs=[_smem_spec(), _smem_spec(), pl.BlockSpec((PEER_TT, PEER_SLOTS), lambda i: (i, 0)),
                  pl.BlockSpec((PEER_SLOTS, width), lambda i: (0, 0)),
                  pl.BlockSpec((PEER_TT, D_MODEL), lambda i: (i, 0)), pl.BlockSpec((1, D_MODEL), lambda i: (0, 0)),
                  _table_spec(tab.shape[0])],
        out_specs=pl.BlockSpec((PEER_TT, D_MODEL), lambda i: (i, 0)),
        out_shape=jax.ShapeDtypeStruct((t, D_MODEL), jnp.float32),
        scratch_shapes=[pltpu.VMEM((PEER_TT, width), F32), pltpu.VMEM((PEER_TT, width), F32)],
        compiler_params=pltpu.CompilerParams(vmem_limit_bytes=VMEM_LIMIT_PEER),
        name="peer_v",
    )(*_slot_halves(idx), coef, expand, h, g.reshape(1, D_MODEL), tab)


PK_TT = 128
PK_A_BLOCKS = ((0, 16), (1, 8), (2, 5), (3, 4))
PK_B_BLOCKS = ((0, 4, 16), (1, 4, 8), (2, 4, 5))
NEG_HUGE = -3.0e38
PK_HEADS_PER_TRIP = 8


def _extract_top(problems, flat, n):
    vals = [[] for _ in problems]
    picked = [[] for _ in problems]
    cur = [v for v, _ in problems]
    for _ in range(n):
        for i, (_, payload) in enumerate(problems):
            m = jnp.max(cur[i], axis=0, keepdims=True)
            pos = jnp.min(jnp.where(cur[i] == m, flat, 1e9), axis=0, keepdims=True)
            hit = flat == pos
            picked[i].append(pos if payload is None else jnp.sum(jnp.where(hit, payload, 0.0), axis=0, keepdims=True))
            cur[i] = jnp.where(hit, NEG_HUGE, cur[i])
            vals[i].append(m)
    return [jnp.concatenate(v, axis=0) for v in vals], [jnp.concatenate(p, axis=0) for p in picked]


def _candidate_grid(first, second, combine):
    up8 = lambda n: -(-n // 8) * 8
    return jnp.concatenate([combine(first[a:a + 1], second[0:up8(nb)]) for a, nb in PK_A_BLOCKS]
                           + [combine(first[0:up8(a1)], second[b:b + 1]) for b, _, a1 in PK_B_BLOCKS], axis=0)


def _candidate_valid_and_flat():
    up8 = lambda n: -(-n // 8) * 8
    valid, flat = [], []
    for a, nb in PK_A_BLOCKS:
        r = lax.broadcasted_iota(jnp.int32, (up8(nb), PK_TT), 0)
        valid.append(r < nb)
        flat.append(a * PEER_TOPK + r)
    for b, a0, a1 in PK_B_BLOCKS:
        r = lax.broadcasted_iota(jnp.int32, (up8(a1), PK_TT), 0)
        valid.append((r >= a0) & (r < a1))
        flat.append(r * PEER_TOPK + b)
    valid = jnp.concatenate(valid, axis=0)
    return valid, jnp.where(valid, jnp.concatenate(flat, axis=0).astype(F32), 1e9)


def _peer_topk_kernel(q_ref, ka_ref, eid_ref, w_ref):
    key_row = lax.broadcasted_iota(jnp.int32, (N_KEYS, PK_TT), 0).astype(F32)
    valid, flat = _candidate_valid_and_flat()

    def heads(i, carry):
        scores = []
        for j in range(2 * PK_HEADS_PER_TRIP):
            q = q_ref[2 * PK_HEADS_PER_TRIP * i + j]
            q_hi = _trunc_bf16(q)
            qa = jnp.concatenate([q_hi.astype(BF16), (q - q_hi).astype(BF16), q_hi.astype(BF16)], axis=1)
            scores.append(lax.dot_general(ka_ref[2 * PK_HEADS_PER_TRIP * i + j], qa, (((1,), (1,)), ((), ())),
                                          preferred_element_type=F32))
        sub_s, sub_i = _extract_top([(s, None) for s in scores], key_row, PEER_TOPK)
        cands = []
        for j in range(PK_HEADS_PER_TRIP):
            s1, s2, i1, i2 = sub_s[2 * j], sub_s[2 * j + 1], sub_i[2 * j], sub_i[2 * j + 1]
            cands.append((jnp.where(valid, _candidate_grid(s1, s2, lambda x, y: x + y), NEG_HUGE),
                          _candidate_grid(i1, i2, lambda x, y: x * N_KEYS + y)))
        top_s, top_e = _extract_top(cands, flat, PEER_TOPK)
        for j in range(PK_HEADS_PER_TRIP):
            e = jnp.exp(top_s[j] - top_s[j][0:1])
            w_ref[PK_HEADS_PER_TRIP * i + j] = e / jnp.sum(e, axis=0, keepdims=True)
            eid_ref[PK_HEADS_PER_TRIP * i + j] = top_e[j].astype(jnp.int32) * ROW_SUB
        return carry

    lax.fori_loop(0, PEER_HEADS // PK_HEADS_PER_TRIP, heads, 0)


def peer_topk(qt, subkeys):
    hp, t, d = qt.shape
    k_hi, k_lo = _split2(subkeys.reshape(hp, N_KEYS, d))
    ka = jnp.concatenate([k_hi, k_hi, k_lo], axis=2)
    eid, w = pl.pallas_call(
        _peer_topk_kernel,
        grid=(t // PK_TT,),
        in_specs=[pl.BlockSpec((hp, PK_TT, d), lambda i: (0, i, 0)),
                  pl.BlockSpec((hp, N_KEYS, 3 * d), lambda i: (0, 0, 0))],
        out_specs=[pl.BlockSpec((PEER_HEADS, PEER_TOPK, PK_TT), lambda i: (0, 0, i)),
                   pl.BlockSpec((PEER_HEADS, PEER_TOPK, PK_TT), lambda i: (0, 0, i))],
        out_shape=[jax.ShapeDtypeStruct((PEER_HEADS, PEER_TOPK, t), jnp.int32),
                   jax.ShapeDtypeStruct((PEER_HEADS, PEER_TOPK, t), F32)],
        name="peer_topk",
    )(qt, ka)
    to_slots = lambda a: a.transpose(2, 0, 1).reshape(t, PEER_SLOTS)
    return to_slots(eid), to_slots(w)


def split_columns(t, sizes):
    offs = np.cumsum((0,) + tuple(sizes))
    return [t[..., int(a):int(b)] for a, b in zip(offs[:-1], offs[1:])]


def rope_tables(pos):
    inv = jnp.power(ROPE_THETA, -jnp.arange(0, ROPE_DIM, 2, dtype=jnp.float32) / ROPE_DIM)
    ang = pos[:, None] * inv[None, :]
    return jnp.cos(ang), jnp.sin(ang)


def apply_partial_rope(x, cos, sin):
    half = ROPE_DIM // 2
    xr = x[..., :ROPE_DIM].astype(jnp.float32)
    x1, x2 = xr[..., :half], xr[..., half:]
    c, s = cos[:, None, :], sin[:, None, :]
    rot = jnp.concatenate([x1 * c - x2 * s, x1 * s + x2 * c], axis=-1)
    return jnp.concatenate([rot.astype(x.dtype), x[..., ROPE_DIM:]], axis=-1)


LOG2_E = 1.4426950408889634
LOGIT_SCALE = HEAD_DIM ** -0.5 * LOG2_E


def _logits(ks, qts):
    return [jnp.dot(k, qt, preferred_element_type=F32) for k, qt in zip(ks, qts)]


def _flash_update(ss, vts, mask, carries):
    stats, ps = [], []
    for s, (m, l, _) in zip(ss, carries):
        s_vis = s if mask is None else jnp.where(mask, s, NEG_INF)
        m_new = jnp.maximum(m, jnp.max(s_vis, axis=0, keepdims=True))
        alpha = jnp.exp2(m - m_new)
        p = jnp.exp2(s - m_new)
        if mask is not None:
            p = jnp.where(mask, p, 0.0)
        stats.append((m_new, alpha, alpha * l + jnp.sum(p, axis=0, keepdims=True)))
        ps.append(p.astype(BF16))
    return tuple((m_new, l, alpha * acc + jnp.dot(vt, p, preferred_element_type=F32))
                 for (m_new, alpha, l), p, vt, (_, _, acc) in zip(stats, ps, vts, carries))


def _causal_flash(key_tile, value_tile, qts, n_full, last_mask, n):
    def body(kt, carries):
        return _flash_update(_logits(key_tile(kt), qts), value_tile(kt), None, carries)

    carries = lax.fori_loop(0, n_full, body, tuple(_flash_init(n) for _ in qts))
    return _flash_update(_logits(key_tile(n_full), qts), value_tile(n_full), last_mask, carries)


def _flash_init(n):
    return (jnp.full((1, n), NEG_INF, F32), jnp.zeros((1, n), F32), jnp.zeros((HEAD_DIM, n), F32))


FOX_TQ = 256
FOX_KV = 512
FOX_DK = 256


FOX_HP = 4


def _fox_kernel(qt_ref, k_ref, vt_ref, o_ref):
    qi = pl.program_id(1)
    q0 = qi * FOX_TQ
    qts = [qt_ref[h] for h in range(FOX_HP)]
    qpos = q0 + lax.broadcasted_iota(jnp.int32, (1, FOX_TQ), 1)
    krow = lax.broadcasted_iota(jnp.int32, (FOX_KV, FOX_TQ), 0)

    n_full = q0 // FOX_KV
    causal = (n_full * FOX_KV + krow) <= qpos
    carry = _causal_flash(lambda kt: [k_ref[h, kt] for h in range(FOX_HP)],
                          lambda kt: [vt_ref[h, kt] for h in range(FOX_HP)], qts, n_full, causal, FOX_TQ)
    for h in range(FOX_HP):
        _, l, acc = carry[h]
        o_ref[h] = acc / l


def _trunc_bf16(x):
    bits = lax.bitcast_convert_type(x, jnp.uint32) & jnp.uint32(0xFFFF0000)
    return lax.bitcast_convert_type(bits, F32)


def _split3(c):
    c1 = _trunc_bf16(c)
    r = c - c1
    c2 = _trunc_bf16(r)
    return c1.astype(BF16), c2.astype(BF16), (r - c2).astype(BF16)


def _split2(x):
    hi = _trunc_bf16(x)
    return hi.astype(BF16), (x - hi).astype(BF16)


def _aug_q(q):
    hi, lo = _split2(q)
    return jnp.concatenate([hi, hi, lo], axis=-1)


def _aug_k(k):
    hi, lo = _split2(k)
    return jnp.concatenate([hi, lo, hi], axis=-1)


def fox_attention(q, k, v, log_f):
    B_, S_, H, Dh = q.shape
    KV_TILE = FOX_KV
    n_qt, n_kt = S_ // FOX_TQ, S_ // KV_TILE
    c = jnp.cumsum(log_f, axis=1)
    c1, c2, c3 = _split3(c * LOG2_E)
    j = jnp.arange(FOX_DK - 3 * Dh)
    pick = lambda t: t[..., None].astype(F32)
    terms = lambda first: jnp.where(j == first, pick(c1), jnp.where(j == first + 1, pick(c2), pick(c3)))
    q_extra = jnp.where(j < 3, 1.0, jnp.where(j < 6, terms(3), 0.0)).astype(BF16)
    k_extra = jnp.where(j < 3, -terms(0), jnp.where(j < 6, 1.0, 0.0)).astype(BF16)
    qa = jnp.concatenate([_aug_q(q * LOGIT_SCALE), q_extra], axis=-1)
    ka = jnp.concatenate([_aug_k(k), k_extra], axis=-1)
    HP, HG = FOX_HP, H // FOX_HP
    qt = qa.reshape(B_, n_qt, FOX_TQ, HG, HP, FOX_DK).transpose(0, 3, 1, 4, 5, 2).reshape(B_ * HG * n_qt, HP, FOX_DK, FOX_TQ)
    kk = ka.reshape(B_, n_kt, KV_TILE, HG, HP, FOX_DK).transpose(0, 3, 4, 1, 2, 5).reshape(B_ * HG, HP, n_kt, KV_TILE, FOX_DK)
    vt = v.astype(BF16).reshape(B_, n_kt, KV_TILE, HG, HP, Dh).transpose(0, 3, 4, 1, 5, 2).reshape(B_ * HG, HP, n_kt, Dh, KV_TILE)
    out = pl.pallas_call(
        _fox_kernel,
        grid=(B_ * HG, n_qt),
        in_specs=[pl.BlockSpec((None, HP, FOX_DK, FOX_TQ), lambda b, i: (b * n_qt + i, 0, 0, 0)),
                  pl.BlockSpec((None, HP, n_kt, KV_TILE, FOX_DK), lambda b, i: (b, 0, 0, 0, 0)),
                  pl.BlockSpec((None, HP, n_kt, Dh, KV_TILE), lambda b, i: (b, 0, 0, 0, 0))],
        out_specs=pl.BlockSpec((None, HP, Dh, FOX_TQ), lambda b, i: (b * n_qt + i, 0, 0, 0)),
        out_shape=jax.ShapeDtypeStruct((B_ * HG * n_qt, HP, Dh, FOX_TQ), F32),
        name="fox_attention",
    )(qt, kk, vt)
    return out.reshape(B_, HG, n_qt, HP, Dh, FOX_TQ)


NSA_TQ = 256
NSA_N = NSA_HPG * NSA_TQ
NSA_CH = 256
N_CMP_PAD = 256
N_SLC = 64
NSA_DK = 3 * HEAD_DIM
NSA_KV = 512
WIN_TILES = WINDOW // NSA_TQ + 1
WIN_KEYS = WIN_TILES * NSA_TQ


def _nsa_kernel(qt_ref, kc_ref, vct_ref, ks_ref, vst_ref, kw_ref, vwt_ref, gate_ref, ovl_ref, o_ref):
    qi = pl.program_id(1)
    q0 = qi * NSA_TQ
    qt = jnp.concatenate([qt_ref[h] for h in range(NSA_HPG)], axis=1)
    lane = lax.broadcasted_iota(jnp.int32, (1, NSA_N), 1)
    qpos = q0 + (lane & (NSA_TQ - 1))

    hn = NSA_CH
    n_ch = NSA_N // NSA_CH
    qts = [qt[:, c * hn:(c + 1) * hn] for c in range(n_ch)]
    qpos_h = qpos[:, :hn]

    w0 = jnp.maximum(qi - WINDOW // NSA_TQ, 0)
    k_w = jnp.concatenate([kw_ref[w0 + i] for i in range(WIN_TILES)], axis=0)
    vt_w = jnp.concatenate([vwt_ref[w0 + i] for i in range(WIN_TILES)], axis=1)
    s = jnp.dot(kc_ref[...], qt, preferred_element_type=F32)
    s_w = [jnp.dot(k_w, qts[c], preferred_element_type=F32) for c in range(n_ch)]

    cmp_end = lax.broadcasted_iota(jnp.int32, (N_CMP_PAD, NSA_N), 0) * CMP_STRIDE + (CMP_LEN - 1)
    mc = cmp_end <= qpos
    m = jnp.max(jnp.where(mc, s, NEG_INF), axis=0, keepdims=True)
    p = jnp.where(mc, jnp.exp2(s - m), 0.0)
    l = jnp.sum(p, axis=0, keepdims=True)
    pc = p * jnp.where(l > 0.0, 1.0 / l, 0.0)
    o_cmp = jnp.dot(vct_ref[...], pc.astype(BF16), preferred_element_type=F32)
    pcs = sum(pc[:, h * NSA_TQ:(h + 1) * NSA_TQ] for h in range(NSA_HPG))
    pcs_hi = pcs.astype(BF16)
    pcs_lo = (pcs - pcs_hi.astype(F32)).astype(BF16)
    ovl = ovl_ref[...]
    imp = jnp.dot(ovl, pcs_hi, preferred_element_type=F32) + jnp.dot(ovl, pcs_lo, preferred_element_type=F32)

    rel = qpos_h - (w0 * NSA_TQ + lax.broadcasted_iota(jnp.int32, (WIN_KEYS, hn), 0))
    in_win = (rel >= 0) & (rel < WINDOW)
    outs_w = []
    for c in range(n_ch):
        m = jnp.max(jnp.where(in_win, s_w[c], NEG_INF), axis=0, keepdims=True)
        p = jnp.where(in_win, jnp.exp2(s_w[c] - m), 0.0)
        l = jnp.sum(p, axis=0, keepdims=True)
        outs_w.append(jnp.dot(vt_w, p.astype(BF16), preferred_element_type=F32) / l)
    acc_w = jnp.concatenate(outs_w, axis=1)

    blk = lax.broadcasted_iota(jnp.int32, (N_SLC, NSA_TQ), 0)
    q_blk = (q0 + lax.broadcasted_iota(jnp.int32, (N_SLC, NSA_TQ), 1)) >> 6
    forced = (blk == 0) | (blk == q_blk) | (blk == q_blk - 1)
    score = jnp.where(forced, FORCED_SCORE, jnp.where(blk <= q_blk, imp, -1.0))
    rank = jnp.zeros((N_SLC, NSA_TQ), jnp.int32)
    for i in range(N_SLC):
        row = score[i:i + 1, :]
        beats = (row > score) | ((row == score) & (blk > i))
        rank = rank + beats.astype(jnp.int32)
    bias = jnp.where(rank < SLC_TOP, 0.0, NEG_INF).astype(BF16)
    bias = jnp.concatenate([bias] * (hn // NSA_TQ), axis=1)
    qts_s = [jnp.concatenate([qh, bias], axis=0) for qh in qts]
    krow = lax.broadcasted_iota(jnp.int32, (NSA_KV, hn), 0)

    n_full = q0 // NSA_KV
    causal = (n_full * NSA_KV + krow) <= qpos_h
    slc = _causal_flash(lambda kt: [ks_ref[kt]] * n_ch, lambda kt: [vst_ref[kt]] * n_ch, qts_s, n_full, causal, hn)
    acc_s = jnp.concatenate([acc / l for _, l, acc in slc], axis=1)

    g = jax.nn.sigmoid(gate_ref[...])
    o_ref[...] = g[0:1] * o_cmp + g[1:2] * acc_s + g[2:3] * acc_w


def nsa_attention(q, k_cmp, v_cmp, k_slc, v_slc, k_win, v_win, gate_logit):
    B_, S_, H, Dh = q.shape
    G = NSA_GROUPS
    n_qt = S_ // NSA_TQ
    n_cmp = k_cmp.shape[1]
    cs = np.arange(N_CMP_PAD)[None, :] * CMP_STRIDE
    ss = np.arange(N_SLC)[:, None] * SLC_LEN
    ov = np.clip(np.minimum(cs + CMP_LEN, ss + SLC_LEN) - np.maximum(cs, ss), 0, None) / CMP_LEN
    ov[:, n_cmp:] = 0.0
    ovl = jnp.asarray(ov, dtype=BF16)
    qs = _aug_q(q * LOGIT_SCALE).reshape(B_, n_qt, NSA_TQ, G, NSA_HPG, NSA_DK)
    qt = qs.transpose(0, 3, 1, 4, 5, 2).reshape(B_ * G * n_qt, NSA_HPG, NSA_DK, NSA_TQ)
    gt = gate_logit.astype(F32).reshape(B_, n_qt, NSA_TQ, G, NSA_HPG, N_NSA_BRANCH)
    gt = gt.transpose(0, 3, 1, 5, 4, 2).reshape(B_ * G * n_qt, N_NSA_BRANCH, NSA_N)
    tiles = lambda t, kv: t.reshape(B_, S_ // kv, kv, G, -1).transpose(0, 3, 1, 2, 4).reshape(B_ * G, S_ // kv, kv, t.shape[-1])
    keys = lambda t, kv: tiles(_aug_k(t), kv)
    block_onehot = (jnp.arange(S_)[:, None] // SLC_LEN == jnp.arange(N_SLC)[None, :]).astype(BF16)
    k_slc_aug = jnp.concatenate([_aug_k(k_slc), jnp.broadcast_to(block_onehot[None, :, None, :], (B_, S_, G, N_SLC))], axis=-1)
    vals = lambda t, kv: t.astype(BF16).reshape(B_, S_ // kv, kv, G, Dh).transpose(0, 3, 1, 4, 2).reshape(B_ * G, S_ // kv, Dh, kv)
    padc = ((0, 0), (0, N_CMP_PAD - n_cmp), (0, 0), (0, 0))
    kc = _aug_k(jnp.pad(k_cmp, padc)).transpose(0, 2, 1, 3).reshape(B_ * G, N_CMP_PAD, NSA_DK)
    vct = jnp.pad(v_cmp, padc).astype(BF16).transpose(0, 2, 3, 1).reshape(B_ * G, Dh, N_CMP_PAD)
    per_bg = lambda *blk: pl.BlockSpec((None,) + blk, lambda b, i: (b,) + (0,) * len(blk))
    per_tile = lambda *blk: pl.BlockSpec((None,) + blk, lambda b, i: (b * n_qt + i,) + (0,) * len(blk))
    out = pl.pallas_call(
        _nsa_kernel,
        grid=(B_ * G, n_qt),
        in_specs=[per_tile(NSA_HPG, NSA_DK, NSA_TQ),
                  per_bg(N_CMP_PAD, NSA_DK), per_bg(Dh, N_CMP_PAD),
                  per_bg(S_ // NSA_KV, NSA_KV, NSA_DK + N_SLC), per_bg(S_ // NSA_KV, Dh, NSA_KV),
                  per_bg(n_qt, NSA_TQ, NSA_DK), per_bg(n_qt, Dh, NSA_TQ),
                  per_tile(N_NSA_BRANCH, NSA_N),
                  pl.BlockSpec((N_SLC, N_CMP_PAD), lambda b, i: (0, 0))],
        out_specs=per_tile(Dh, NSA_N),
        out_shape=jax.ShapeDtypeStruct((B_ * G * n_qt, Dh, NSA_N), F32),
        name="nsa_attention",
    )(qt, kc, vct, tiles(k_slc_aug, NSA_KV), vals(v_slc, NSA_KV), keys(k_win, NSA_TQ), vals(v_win, NSA_TQ), gt, ovl)
    return out.reshape(B_, G, n_qt, Dh, NSA_N)


DENSE_TM = 256
SEC_MERGE = N_BRANCH * D_MODEL
SEC_SMALL = 256
IN_SECTIONS = (BRANCH_WIDTH,) * 4 + (KV_WIDTH,) * 6 + (SEC_MERGE, SEC_SMALL)


def _rms(x, g):
    return x * lax.rsqrt(jnp.mean(x * x, axis=-1, keepdims=True) + RMS_EPS) * g


def _row_spec(width):
    return pl.BlockSpec((DENSE_TM, width), lambda i: (i, 0))


def _whole_spec(shape):
    return pl.BlockSpec(shape, lambda i: (0,) * len(shape))


def _in_proj_kernel(x_ref, g_ref, w_ref, *o_refs):
    xn = _rms(x_ref[...], g_ref[...]).astype(BF16)
    off = 0
    for o_ref, width in zip(o_refs, IN_SECTIONS):
        o_ref[...] = jnp.dot(xn, w_ref[:, off:off + width], preferred_element_type=F32)
        off += width


def in_projection(x, g, w_in):
    t, d = x.shape
    fq, fk, fv, fl, nq, kc, vc, ksl, vsl, kwn, vwn, gl, ml = split_columns(w_in, IN_SPLITS)
    pad = lambda w: jnp.pad(w, ((0, 0), (0, 128 - w.shape[1])))
    w = jnp.concatenate([fq, fk, fv, nq, kc, vc, ksl, vsl, kwn, vwn, ml, pad(fl), pad(gl)], axis=1).astype(BF16)
    return pl.pallas_call(
        _in_proj_kernel,
        grid=(t // DENSE_TM,),
        in_specs=[_row_spec(d), _whole_spec((1, d)), _whole_spec(w.shape)],
        out_specs=[_row_spec(s) for s in IN_SECTIONS],
        out_shape=[jax.ShapeDtypeStruct((t, s), F32) for s in IN_SECTIONS],
        name="in_projection",
    )(x, g.reshape(1, d), w)


def _merge_kernel(x_ref, yf_ref, yn_ref, ml_ref, wb_ref, wo_ref, gq_ref, wq_ref, o_ref, hn_ref, q_ref):
    yf_t = yf_ref[...].reshape(BRANCH_WIDTH, DENSE_TM)
    yn_t = jnp.concatenate(
        [jnp.concatenate([yn_ref[g, j, :, h * NSA_TQ:(h + 1) * NSA_TQ]
                          for g in range(NSA_GROUPS) for h in range(NSA_HPG)], axis=0)
         for j in range(DENSE_TM // NSA_TQ)], axis=1)
    g = jax.nn.sigmoid(ml_ref[...])
    up_f = jnp.dot(yf_t.T.astype(BF16), wb_ref[0], preferred_element_type=F32)
    up_n = jnp.dot(yn_t.T.astype(BF16), wb_ref[1], preferred_element_type=F32)
    merged = g[:, :D_MODEL] * up_f + g[:, D_MODEL:] * up_n
    h = x_ref[...] + jnp.dot(merged.astype(BF16), wo_ref[...], preferred_element_type=F32)
    o_ref[...] = h
    hn = _rms(h, gq_ref[...])
    hn_ref[...] = hn
    q = jnp.dot(hn.astype(BF16), wq_ref[...], preferred_element_type=F32)
    dq = PEER_QDIM // 2
    for j in range(2 * PEER_HEADS):
        q_ref[j] = q[:, j * dq:(j + 1) * dq]


def merge_out(x, y_fox, y_nsa, merge_logit, w_branch, w_out, norm_ffn, wq):
    t, d = x.shape
    hp, dq = 2 * PEER_HEADS, PEER_QDIM // 2
    assert FOX_TQ == DENSE_TM and DENSE_TM % NSA_TQ == 0
    tiles_per_seq = y_fox.shape[2]
    nsa_per_tile = DENSE_TM // NSA_TQ
    yf_spec = pl.BlockSpec((None,) + y_fox.shape[1:2] + (None,) + y_fox.shape[3:],
                           lambda i: (i // tiles_per_seq, 0, i % tiles_per_seq, 0, 0, 0))
    yn_spec = pl.BlockSpec((None, NSA_GROUPS, nsa_per_tile, HEAD_DIM, NSA_N),
                           lambda i: (i // tiles_per_seq, 0, i % tiles_per_seq, 0, 0))
    return pl.pallas_call(
        _merge_kernel,
        grid=(t // DENSE_TM,),
        in_specs=[_row_spec(d), yf_spec, yn_spec, _row_spec(SEC_MERGE),
                  _whole_spec(w_branch.shape), _whole_spec(w_out.shape), _whole_spec((1, d)), _whole_spec(wq.shape)],
        out_specs=[_row_spec(d), _row_spec(d), pl.BlockSpec((hp, DENSE_TM, dq), lambda i: (0, i, 0))],
        out_shape=[jax.ShapeDtypeStruct((t, d), F32), jax.ShapeDtypeStruct((t, d), F32),
                   jax.ShapeDtypeStruct((hp, t, dq), F32)],
        name="merge_out",
    )(x, y_fox, y_nsa, merge_logit, w_branch.astype(BF16), w_out.astype(BF16), norm_ffn.reshape(1, d),
      wq.astype(BF16))


N_CMP_ROWS = 256


def _compress_kernel(r_ref, pos_ref, w1_ref, w2_ref, o_ref):
    r = r_ref[...]
    nxt = jnp.concatenate([r[1:], jnp.zeros((1, r.shape[1]), F32)], axis=0)
    half = r.shape[1]
    hid = (jnp.dot((r + pos_ref[0:1, :]).astype(BF16), w1_ref[:half], preferred_element_type=F32)
           + jnp.dot((nxt + pos_ref[1:2, :]).astype(BF16), w1_ref[half:], preferred_element_type=F32))
    o_ref[...] = jnp.dot(_gelu(hid).astype(BF16), w2_ref[...], preferred_element_type=F32)


def compress_kv_pallas(kv, cmp_pos, w1, w2):
    _, B_, S_, G, Dh = kv.shape
    n_cmp = (S_ - CMP_LEN) // CMP_STRIDE + 1
    width = CMP_STRIDE * Dh
    rows = kv.transpose(0, 1, 3, 2, 4).reshape(2, B_ * G, S_ // CMP_STRIDE, width)
    out = pl.pallas_call(
        _compress_kernel,
        grid=(2, B_ * G),
        in_specs=[pl.BlockSpec((None, None, N_CMP_ROWS, width), lambda j, b: (j, b, 0, 0)),
                  pl.BlockSpec((None, 2, width), lambda j, b: (j, 0, 0)),
                  pl.BlockSpec((None, CMP_LEN * Dh, CMP_HIDDEN), lambda j, b: (j, 0, 0)),
                  pl.BlockSpec((None, CMP_HIDDEN, Dh), lambda j, b: (j, 0, 0))],
        out_specs=pl.BlockSpec((None, None, N_CMP_ROWS, Dh), lambda j, b: (j, b, 0, 0)),
        out_shape=jax.ShapeDtypeStruct((2, B_ * G, N_CMP_ROWS, Dh), F32),
        name="compress_kv",
    )(rows, cmp_pos.reshape(2, 2, width), w1.astype(BF16), w2.astype(BF16))
    return out[:, :, :n_cmp].reshape(2, B_, G, n_cmp, Dh).transpose(0, 1, 3, 2, 4)


def hybrid_mixer(x, norm_g, w_in, fox_f_bias, cmp_pos, cmp_w1, cmp_w2, w_branch, w_out, norm_ffn, wq):
    B_, S_, D = x.shape
    xf = x.reshape(B_ * S_, D)
    secs = in_projection(xf, norm_g, w_in)
    heads = lambda t, n: t.reshape(B_, S_, n, HEAD_DIM)
    fq, fk, fv, nq = secs[:4]
    kc, vc, ksl, vsl, kwn, vwn = (heads(t, NSA_GROUPS) for t in secs[4:10])
    merge_logit, small = secs[10:]
    f_logit = small[:, :FOX_HEADS].reshape(B_, S_, FOX_HEADS)
    gate_logit = small[:, 128:128 + NSA_HEADS * N_NSA_BRANCH].reshape(B_, S_, NSA_HEADS, N_NSA_BRANCH)
    log_f = jax.nn.log_sigmoid(f_logit + fox_f_bias.astype(F32))
    y_fox = fox_attention(heads(fq, FOX_HEADS), heads(fk, FOX_HEADS), heads(fv, FOX_HEADS), log_f)
    cos, sin = rope_tables(jnp.arange(S_, dtype=F32))
    q_nsa = apply_partial_rope(heads(nq, NSA_HEADS), cos, sin)
    k_slc = apply_partial_rope(ksl, cos, sin)
    k_win = apply_partial_rope(kwn, cos, sin)
    cmp = compress_kv_pallas(jnp.stack([kc, vc]), cmp_pos, cmp_w1, cmp_w2)
    n_cmp = cmp.shape[2]
    cmp_end = jnp.arange(n_cmp, dtype=F32) * CMP_STRIDE + (CMP_LEN - 1)
    k_cmp = apply_partial_rope(cmp[0], *rope_tables(cmp_end))
    y_nsa = nsa_attention(q_nsa, k_cmp, cmp[1], k_slc, vsl, k_win, vwn, gate_logit)
    return merge_out(xf, y_fox, y_nsa, merge_logit, w_branch, w_out, norm_ffn, wq)


def peer_ffn(h, hn, q, subkeys, u, v, norm_final):
    idx2, w = peer_topk(q, subkeys)
    coef = peer_u(idx2, hn, w, pack_table(u))
    return peer_v(idx2, coef, pack_table(v), h, norm_final)


def kernel(x, norm_mix, w_in, fox_f_bias, nsa_cmp_pos, nsa_cmp_w1, nsa_cmp_w2, w_branch, w_out,
           norm_ffn, peer_wq, peer_subkeys, peer_u, peer_v, norm_final):
    B_, S_, D = x.shape
    assert norm_mix.shape[0] == 1, "single-layer trunk"
    h, hn, q = hybrid_mixer(x, norm_mix[0], w_in[0], fox_f_bias[0], nsa_cmp_pos[0], nsa_cmp_w1[0], nsa_cmp_w2[0],
                            w_branch[0], w_out[0], norm_ffn[0], peer_wq[0])
    out = peer_ffn(h, hn, q, peer_subkeys[0], peer_u[0], peer_v[0], norm_final)
    return out.reshape(B_, S_, D)
```

```python
import jax
import jax.numpy as jnp
import numpy as np
from jax import lax
from jax.experimental import pallas as pl
from jax.experimental.pallas import tpu as pltpu

D_MODEL = 1024
HEAD_DIM = 64
FOX_HEADS = 8
NSA_HEADS = 8
NSA_GROUPS = 2
NSA_HPG = NSA_HEADS // NSA_GROUPS
BRANCH_WIDTH = 512
N_BRANCH = 2
N_NSA_BRANCH = 3
ROPE_DIM = HEAD_DIM // 4
ROPE_THETA = 500000.0
CMP_LEN = 32
CMP_STRIDE = 16
CMP_HIDDEN = 2 * HEAD_DIM
SLC_LEN = 64
SLC_TOP = 16
WINDOW = 512
FORCED_SCORE = 1e9
NEG_INF = -1e30
PEER_HEADS = 8
N_KEYS = 128
PEER_QDIM = 256
PEER_TOPK = 16
RMS_EPS = 1e-6
KV_WIDTH = NSA_GROUPS * HEAD_DIM
IN_SPLITS = (BRANCH_WIDTH, BRANCH_WIDTH, BRANCH_WIDTH, FOX_HEADS, BRANCH_WIDTH,
             KV_WIDTH, KV_WIDTH, KV_WIDTH, KV_WIDTH, KV_WIDTH, KV_WIDTH,
             NSA_HEADS * N_NSA_BRANCH, N_BRANCH * D_MODEL)


BF16 = jnp.bfloat16
F32 = jnp.float32

PEER_SLOTS = PEER_HEADS * PEER_TOPK
PEER_TT = 128
PEER_CHUNK_SLOTS = 32
PEER_U_UNROLL = 128
PEER_V_UNROLL = 128
ROW_SUB = D_MODEL // 2 // 128
VMEM_LIMIT_PEER = N_KEYS * N_KEYS * D_MODEL * 2 + 16 * 1024 * 1024


def pack_table(tab):
    n, d = tab.shape
    return pl.pallas_call(
        _pack_kernel,
        grid=(n // PACK_TM,),
        in_specs=[pl.BlockSpec((PACK_TM, d), lambda i: (i, 0))],
        out_specs=pl.BlockSpec((PACK_TM * ROW_SUB, 128), lambda i: (i, 0)),
        out_shape=jax.ShapeDtypeStruct((n * ROW_SUB, 128), jnp.uint32),
        name="pack_table",
    )(tab)


PACK_TM = 256


def _pack_kernel(x_ref, o_ref):
    bits = lax.bitcast_convert_type(x_ref[...].astype(jnp.bfloat16).astype(jnp.float32), jnp.uint32)
    for r in range(ROW_SUB):
        even = bits[:, (2 * r) * 128:(2 * r + 1) * 128]
        odd = bits[:, (2 * r + 1) * 128:(2 * r + 2) * 128]
        o_ref[pl.ds(r, PACK_TM, stride=ROW_SUB), :] = (even >> 16) | (odd & jnp.uint32(0xFFFF0000))


def _expert_slab(tab_ref, row):
    return tab_ref[pl.ds(pl.multiple_of(row, ROW_SUB), ROW_SUB), :]


def _unpack(w):
    lo = lax.bitcast_convert_type(w << 16, jnp.float32)
    hi = lax.bitcast_convert_type(w & jnp.uint32(0xFFFF0000), jnp.float32)
    return lo, hi


def _gelu(x):
    return 0.5 * x * (1.0 + lax.erf(x * (2.0 ** -0.5)))


def _gather_token(idx_a_ref, idx_b_ref, tab_ref, t):
    base = t * (PEER_SLOTS // 2)
    slabs = [jnp.concatenate([_expert_slab(tab_ref, idx_a_ref[base + s]),
                              _expert_slab(tab_ref, idx_b_ref[base + s])], axis=0) for s in range(PEER_SLOTS // 2)]
    return pltpu.bitcast(jnp.concatenate(slabs, axis=0), BF16)


def _own_chunk(width):
    return ((lax.broadcasted_iota(jnp.int32, (N_CHUNK, width), 1) & (N_CHUNK - 1))
            == lax.broadcasted_iota(jnp.int32, (N_CHUNK, width), 0))


def _peer_u_kernel(idx_a_ref, idx_b_ref, x_ref, w_ref, tab_ref, out_ref):
    lane = lax.broadcasted_iota(jnp.int32, (2 * ROW_SUB, 128), 1)
    lower = lax.broadcasted_iota(jnp.int32, (2 * ROW_SUB, 128), 0) >= ROW_SUB
    pair_lane = jnp.where(lower, lane - PEER_SLOTS // 2, lane)

    def token(t):
        row = x_ref[pl.ds(t, 1), :]
        chunks = [row[:, c * 128:(c + 1) * 128] for c in range(2 * ROW_SUB)]
        xl = jnp.concatenate(chunks[0::2] * 2, axis=0)
        xh = jnp.concatenate(chunks[1::2] * 2, axis=0)
        acc = jnp.zeros((2 * ROW_SUB, 128), jnp.float32)
        base = t * (PEER_SLOTS // 2)
        for s in range(PEER_SLOTS // 2):
            w = jnp.concatenate([_expert_slab(tab_ref, idx_a_ref[base + s]),
                                 _expert_slab(tab_ref, idx_b_ref[base + s])], axis=0)
            lo, hi = _unpack(w)
            part = jnp.sum(lo * xl + hi * xh, axis=1, keepdims=True)
            acc = jnp.where(pair_lane == s, part, acc)
        out_ref[pl.ds(t, 1), :] = w_ref[pl.ds(t, 1), :] * _gelu(jnp.sum(acc, axis=0, keepdims=True))

    def trip(i, carry):
        for j in range(PEER_U_UNROLL):
            token(i * PEER_U_UNROLL + j)
        return carry

    lax.fori_loop(0, PEER_TT // PEER_U_UNROLL, trip, 0)


N_CHUNK = 2 * ROW_SUB


def _peer_v_kernel(idx_a_ref, idx_b_ref, coef_ref, expand_ref, h_ref, g_ref, tab_ref, out_ref,
                   ce_hi_ref, ce_lo_ref):
    width = PEER_SLOTS * N_CHUNK
    coef = coef_ref[...]
    c_hi = _trunc_bf16(coef)
    expand = expand_ref[...]
    ce_hi_ref[...] = jnp.dot(c_hi.astype(BF16), expand, preferred_element_type=F32)
    ce_lo_ref[...] = jnp.dot((coef - c_hi).astype(BF16), expand, preferred_element_type=F32)
    own_chunk = _own_chunk(width)

    def token(t):
        gathered = _gather_token(idx_a_ref, idx_b_ref, tab_ref, t)
        sel = lambda ref: jnp.where(own_chunk, ref[pl.ds(t, 1), :], 0.0).astype(BF16)
        c = jnp.concatenate([sel(ce_hi_ref), sel(ce_lo_ref)], axis=0)
        o = jnp.dot(c, gathered, preferred_element_type=F32)
        o = o[:N_CHUNK] + o[N_CHUNK:]
        out_ref[pl.ds(t, 1), :] = jnp.concatenate([o[r:r + 1, :] for r in range(N_CHUNK)], axis=1)

    def trip(i, carry):
        for j in range(PEER_V_UNROLL):
            token(i * PEER_V_UNROLL + j)
        return carry

    lax.fori_loop(0, PEER_TT // PEER_V_UNROLL, trip, 0)
    out_ref[...] = _rms(h_ref[...] + out_ref[...], g_ref[...])


def _smem_spec():
    return pl.BlockSpec((PEER_TT * PEER_SLOTS // 2,), lambda i: (i,), memory_space=pltpu.SMEM)


def _slot_halves(a):
    return a[:, :PEER_SLOTS // 2].reshape(-1), a[:, PEER_SLOTS // 2:].reshape(-1)


def _slot_columns():
    half = PEER_SLOTS // 2
    block = np.arange(PEER_SLOTS * N_CHUNK) // N_CHUNK
    col_slot = block // 2 + half * (block % 2)
    return jnp.asarray(col_slot[None, :] == np.arange(PEER_SLOTS)[:, None], dtype=BF16)


def _table_spec(rows):
    return pl.BlockSpec((rows, 128), lambda i: (0, 0), pipeline_mode=pl.Buffered(1))


def peer_u(idx, x, w, tab):
    t = x.shape[0]
    return pl.pallas_call(
        _peer_u_kernel,
        grid=(t // PEER_TT,),
        in_specs=[_smem_spec(), _smem_spec(), pl.BlockSpec((PEER_TT, D_MODEL), lambda i: (i, 0)),
                  pl.BlockSpec((PEER_TT, PEER_SLOTS), lambda i: (i, 0)), _table_spec(tab.shape[0])],
        out_specs=pl.BlockSpec((PEER_TT, PEER_SLOTS), lambda i: (i, 0)),
        out_shape=jax.ShapeDtypeStruct((t, PEER_SLOTS), jnp.float32),
        compiler_params=pltpu.CompilerParams(vmem_limit_bytes=VMEM_LIMIT_PEER),
        name="peer_u",
    )(*_slot_halves(idx), x, w, tab)


def peer_v(idx, coef, tab, h, g):
    t = idx.shape[0]
    width = PEER_SLOTS * N_CHUNK
    expand = _slot_columns()
    return pl.pallas_call(
        _peer_v_kernel,
        grid=(t // PEER_TT,),
        in_specs=[_smem_spec(), _smem_spec(), pl.BlockSpec((PEER_TT, PEER_SLOTS), lambda i: (i, 0)),
                  pl.BlockSpec((PEER_SLOTS, width), lambda i: (0, 0)),
                  pl.BlockSpec((PEER_TT, D_MODEL), lambda i: (i, 0)), pl.BlockSpec((1, D_MODEL), lambda i: (0, 0)),
                  _table_spec(tab.shape[0])],
        out_specs=pl.BlockSpec((PEER_TT, D_MODEL), lambda i: (i, 0)),
        out_shape=jax.ShapeDtypeStruct((t, D_MODEL), jnp.float32),
        scratch_shapes=[pltpu.VMEM((PEER_TT, width), F32), pltpu.VMEM((PEER_TT, width), F32)],
        compiler_params=pltpu.CompilerParams(vmem_limit_bytes=VMEM_LIMIT_PEER),
        name="peer_v",
    )(*_slot_halves(idx), coef, expand, h, g.reshape(1, D_MODEL), tab)


PK_TT = 128
PK_A_BLOCKS = ((0, 16), (1, 8), (2, 5), (3, 4))
PK_B_BLOCKS = ((0, 4, 16), (1, 4, 8), (2, 4, 5))
NEG_HUGE = -3.0e38
PK_HEADS_PER_TRIP = 8


def _extract_top(problems, flat, n):
    vals = [[] for _ in problems]
    picked = [[] for _ in problems]
    cur = [v for v, _ in problems]
    for _ in range(n):
        for i, (_, payload) in enumerate(problems):
            m = jnp.max(cur[i], axis=0, keepdims=True)
            pos = jnp.min(jnp.where(cur[i] == m, flat, 1e9), axis=0, keepdims=True)
            hit = flat == pos
            picked[i].append(pos if payload is None else jnp.sum(jnp.where(hit, payload, 0.0), axis=0, keepdims=True))
            cur[i] = jnp.where(hit, NEG_HUGE, cur[i])
            vals[i].append(m)
    return [jnp.concatenate(v, axis=0) for v in vals], [jnp.concatenate(p, axis=0) for p in picked]


def _candidate_grid(first, second, combine):
    up8 = lambda n: -(-n // 8) * 8
    return jnp.concatenate([combine(first[a:a + 1], second[0:up8(nb)]) for a, nb in PK_A_BLOCKS]
                           + [combine(first[0:up8(a1)], second[b:b + 1]) for b, _, a1 in PK_B_BLOCKS], axis=0)


def _candidate_valid_and_flat():
    up8 = lambda n: -(-n // 8) * 8
    valid, flat = [], []
    for a, nb in PK_A_BLOCKS:
        r = lax.broadcasted_iota(jnp.int32, (up8(nb), PK_TT), 0)
        valid.append(r < nb)
        flat.append(a * PEER_TOPK + r)
    for b, a0, a1 in PK_B_BLOCKS:
        r = lax.broadcasted_iota(jnp.int32, (up8(a1), PK_TT), 0)
        valid.append((r >= a0) & (r < a1))
        flat.append(r * PEER_TOPK + b)
    valid = jnp.concatenate(valid, axis=0)
    return valid, jnp.where(valid, jnp.concatenate(flat, axis=0).astype(F32), 1e9)


def _peer_topk_kernel(q_ref, ka_ref, eid_ref, w_ref):
    key_row = lax.broadcasted_iota(jnp.int32, (N_KEYS, PK_TT), 0).astype(F32)
    valid, flat = _candidate_valid_and_flat()

    def heads(i, carry):
        scores = []
        for j in range(2 * PK_HEADS_PER_TRIP):
            q = q_ref[2 * PK_HEADS_PER_TRIP * i + j]
            q_hi = _trunc_bf16(q)
            qa = jnp.concatenate([q_hi.astype(BF16), (q - q_hi).astype(BF16), q_hi.astype(BF16)], axis=1)
            scores.append(lax.dot_general(ka_ref[2 * PK_HEADS_PER_TRIP * i + j], qa, (((1,), (1,)), ((), ())),
                                          preferred_element_type=F32))
        sub_s, sub_i = _extract_top([(s, None) for s in scores], key_row, PEER_TOPK)
        cands = []
        for j in range(PK_HEADS_PER_TRIP):
            s1, s2, i1, i2 = sub_s[2 * j], sub_s[2 * j + 1], sub_i[2 * j], sub_i[2 * j + 1]
            cands.append((jnp.where(valid, _candidate_grid(s1, s2, lambda x, y: x + y), NEG_HUGE),
                          _candidate_grid(i1, i2, lambda x, y: x * N_KEYS + y)))
        top_s, top_e = _extract_top(cands, flat, PEER_TOPK)
        for j in range(PK_HEADS_PER_TRIP):
            e = jnp.exp(top_s[j] - top_s[j][0:1])
            w_ref[PK_HEADS_PER_TRIP * i + j] = e / jnp.sum(e, axis=0, keepdims=True)
            eid_ref[PK_HEADS_PER_TRIP * i + j] = top_e[j].astype(jnp.int32) * ROW_SUB
        return carry

    lax.fori_loop(0, PEER_HEADS // PK_HEADS_PER_TRIP, heads, 0)


def peer_topk(qt, subkeys):
    hp, t, d = qt.shape
    k_hi, k_lo = _split2(subkeys.reshape(hp, N_KEYS, d))
    ka = jnp.concatenate([k_hi, k_hi, k_lo], axis=2)
    eid, w = pl.pallas_call(
        _peer_topk_kernel,
        grid=(t // PK_TT,),
        in_specs=[pl.BlockSpec((hp, PK_TT, d), lambda i: (0, i, 0)),
                  pl.BlockSpec((hp, N_KEYS, 3 * d), lambda i: (0, 0, 0))],
        out_specs=[pl.BlockSpec((PEER_HEADS, PEER_TOPK, PK_TT), lambda i: (0, 0, i)),
                   pl.BlockSpec((PEER_HEADS, PEER_TOPK, PK_TT), lambda i: (0, 0, i))],
        out_shape=[jax.ShapeDtypeStruct((PEER_HEADS, PEER_TOPK, t), jnp.int32),
                   jax.ShapeDtypeStruct((PEER_HEADS, PEER_TOPK, t), F32)],
        name="peer_topk",
    )(qt, ka)
    to_slots = lambda a: a.transpose(2, 0, 1).reshape(t, PEER_SLOTS)
    return to_slots(eid), to_slots(w)


def split_columns(t, sizes):
    offs = np.cumsum((0,) + tuple(sizes))
    return [t[..., int(a):int(b)] for a, b in zip(offs[:-1], offs[1:])]


def rope_tables(pos):
    inv = jnp.power(ROPE_THETA, -jnp.arange(0, ROPE_DIM, 2, dtype=jnp.float32) / ROPE_DIM)
    ang = pos[:, None] * inv[None, :]
    return jnp.cos(ang), jnp.sin(ang)


def apply_partial_rope(x, cos, sin):
    half = ROPE_DIM // 2
    xr = x[..., :ROPE_DIM].astype(jnp.float32)
    x1, x2 = xr[..., :half], xr[..., half:]
    c, s = cos[:, None, :], sin[:, None, :]
    rot = jnp.concatenate([x1 * c - x2 * s, x1 * s + x2 * c], axis=-1)
    return jnp.concatenate([rot.astype(x.dtype), x[..., ROPE_DIM:]], axis=-1)


LOG2_E = 1.4426950408889634
LOGIT_SCALE = HEAD_DIM ** -0.5 * LOG2_E


def _logits(ks, qts):
    return [jnp.dot(k, qt, preferred_element_type=F32) for k, qt in zip(ks, qts)]


def _flash_update(ss, vts, mask, carries):
    stats, ps = [], []
    for s, (m, l, _) in zip(ss, carries):
        s_vis = s if mask is None else jnp.where(mask, s, NEG_INF)
        m_new = jnp.maximum(m, jnp.max(s_vis, axis=0, keepdims=True))
        alpha = jnp.exp2(m - m_new)
        p = jnp.exp2(s - m_new)
        if mask is not None:
            p = jnp.where(mask, p, 0.0)
        stats.append((m_new, alpha, alpha * l + jnp.sum(p, axis=0, keepdims=True)))
        ps.append(p.astype(BF16))
    return tuple((m_new, l, alpha * acc + jnp.dot(vt, p, preferred_element_type=F32))
                 for (m_new, alpha, l), p, vt, (_, _, acc) in zip(stats, ps, vts, carries))


def _causal_flash(key_tile, value_tile, qts, n_full, last_mask, n):
    def body(kt, carries):
        return _flash_update(_logits(key_tile(kt), qts), value_tile(kt), None, carries)

    carries = lax.fori_loop(0, n_full, body, tuple(_flash_init(n) for _ in qts))
    return _flash_update(_logits(key_tile(n_full), qts), value_tile(n_full), last_mask, carries)


def _flash_init(n):
    return (jnp.full((1, n), NEG_INF, F32), jnp.zeros((1, n), F32), jnp.zeros((HEAD_DIM, n), F32))


FOX_TQ = 256
FOX_KV = 512
FOX_DK = 256


FOX_HP = 4


def _fox_kernel(qt_ref, k_ref, vt_ref, o_ref):
    qi = pl.program_id(1)
    q0 = qi * FOX_TQ
    qts = [qt_ref[h] for h in range(FOX_HP)]
    qpos = q0 + lax.broadcasted_iota(jnp.int32, (1, FOX_TQ), 1)
    krow = lax.broadcasted_iota(jnp.int32, (FOX_KV, FOX_TQ), 0)

    n_full = q0 // FOX_KV
    causal = (n_full * FOX_KV + krow) <= qpos
    carry = _causal_flash(lambda kt: [k_ref[h, kt] for h in range(FOX_HP)],
                          lambda kt: [vt_ref[h, kt] for h in range(FOX_HP)], qts, n_full, causal, FOX_TQ)
    for h in range(FOX_HP):
        _, l, acc = carry[h]
        o_ref[h] = acc / l


def _trunc_bf16(x):
    bits = lax.bitcast_convert_type(x, jnp.uint32) & jnp.uint32(0xFFFF0000)
    return lax.bitcast_convert_type(bits, F32)


def _split3(c):
    c1 = _trunc_bf16(c)
    r = c - c1
    c2 = _trunc_bf16(r)
    return c1.astype(BF16), c2.astype(BF16), (r - c2).astype(BF16)


def _split2(x):
    hi = _trunc_bf16(x)
    return hi.astype(BF16), (x - hi).astype(BF16)


def _aug_q(q):
    hi, lo = _split2(q)
    return jnp.concatenate([hi, hi, lo], axis=-1)


def _aug_k(k):
    hi, lo = _split2(k)
    return jnp.concatenate([hi, lo, hi], axis=-1)


def fox_attention(q, k, v, log_f):
    B_, S_, H, Dh = q.shape
    KV_TILE = FOX_KV
    n_qt, n_kt = S_ // FOX_TQ, S_ // KV_TILE
    c = jnp.cumsum(log_f, axis=1)
    c1, c2, c3 = _split3(c * LOG2_E)
    j = jnp.arange(FOX_DK - 3 * Dh)
    pick = lambda t: t[..., None].astype(F32)
    terms = lambda first: jnp.where(j == first, pick(c1), jnp.where(j == first + 1, pick(c2), pick(c3)))
    q_extra = jnp.where(j < 3, 1.0, jnp.where(j < 6, terms(3), 0.0)).astype(BF16)
    k_extra = jnp.where(j < 3, -terms(0), jnp.where(j < 6, 1.0, 0.0)).astype(BF16)
    qa = jnp.concatenate([_aug_q(q * LOGIT_SCALE), q_extra], axis=-1)
    ka = jnp.concatenate([_aug_k(k), k_extra], axis=-1)
    HP, HG = FOX_HP, H // FOX_HP
    qt = qa.reshape(B_, n_qt, FOX_TQ, HG, HP, FOX_DK).transpose(0, 3, 1, 4, 5, 2).reshape(B_ * HG * n_qt, HP, FOX_DK, FOX_TQ)
    kk = ka.reshape(B_, n_kt, KV_TILE, HG, HP, FOX_DK).transpose(0, 3, 4, 1, 2, 5).reshape(B_ * HG, HP, n_kt, KV_TILE, FOX_DK)
    vt = v.astype(BF16).reshape(B_, n_kt, KV_TILE, HG, HP, Dh).transpose(0, 3, 4, 1, 5, 2).reshape(B_ * HG, HP, n_kt, Dh, KV_TILE)
    out = pl.pallas_call(
        _fox_kernel,
        grid=(B_ * HG, n_qt),
        in_specs=[pl.BlockSpec((None, HP, FOX_DK, FOX_TQ), lambda b, i: (b * n_qt + i, 0, 0, 0)),
                  pl.BlockSpec((None, HP, n_kt, KV_TILE, FOX_DK), lambda b, i: (b, 0, 0, 0, 0)),
                  pl.BlockSpec((None, HP, n_kt, Dh, KV_TILE), lambda b, i: (b, 0, 0, 0, 0))],
        out_specs=pl.BlockSpec((None, HP, Dh, FOX_TQ), lambda b, i: (b * n_qt + i, 0, 0, 0)),
        out_shape=jax.ShapeDtypeStruct((B_ * HG * n_qt, HP, Dh, FOX_TQ), F32),
        name="fox_attention",
    )(qt, kk, vt)
    return out.reshape(B_, HG, n_qt, HP, Dh, FOX_TQ)


NSA_TQ = 256
NSA_N = NSA_HPG * NSA_TQ
NSA_CH = 256
N_CMP_PAD = 256
N_SLC = 64
NSA_DK = 3 * HEAD_DIM
NSA_KV = 512
WIN_TILES = WINDOW // NSA_TQ + 1
WIN_KEYS = WIN_TILES * NSA_TQ


def _nsa_kernel(qt_ref, kc_ref, vct_ref, ks_ref, vst_ref, kw_ref, vwt_ref, gate_ref, ovl_ref, o_ref):
    qi = pl.program_id(1)
    q0 = qi * NSA_TQ
    qt = jnp.concatenate([qt_ref[h] for h in range(NSA_HPG)], axis=1)
    lane = lax.broadcasted_iota(jnp.int32, (1, NSA_N), 1)
    qpos = q0 + (lane & (NSA_TQ - 1))

    hn = NSA_CH
    n_ch = NSA_N // NSA_CH
    qts = [qt[:, c * hn:(c + 1) * hn] for c in range(n_ch)]
    qpos_h = qpos[:, :hn]

    w0 = jnp.maximum(qi - WINDOW // NSA_TQ, 0)
    k_w = jnp.concatenate([kw_ref[w0 + i] for i in range(WIN_TILES)], axis=0)
    vt_w = jnp.concatenate([vwt_ref[w0 + i] for i in range(WIN_TILES)], axis=1)
    s = jnp.dot(kc_ref[...], qt, preferred_element_type=F32)
    s_w = [jnp.dot(k_w, qts[c], preferred_element_type=F32) for c in range(n_ch)]

    cmp_end = lax.broadcasted_iota(jnp.int32, (N_CMP_PAD, NSA_N), 0) * CMP_STRIDE + (CMP_LEN - 1)
    mc = cmp_end <= qpos
    m = jnp.max(jnp.where(mc, s, NEG_INF), axis=0, keepdims=True)
    p = jnp.where(mc, jnp.exp2(s - m), 0.0)
    l = jnp.sum(p, axis=0, keepdims=True)
    pc = p * jnp.where(l > 0.0, 1.0 / l, 0.0)
    o_cmp = jnp.dot(vct_ref[...], pc.astype(BF16), preferred_element_type=F32)
    pcs = sum(pc[:, h * NSA_TQ:(h + 1) * NSA_TQ] for h in range(NSA_HPG))
    pcs_hi = pcs.astype(BF16)
    pcs_lo = (pcs - pcs_hi.astype(F32)).astype(BF16)
    ovl = ovl_ref[...]
    imp = jnp.dot(ovl, pcs_hi, preferred_element_type=F32) + jnp.dot(ovl, pcs_lo, preferred_element_type=F32)

    rel = qpos_h - (w0 * NSA_TQ + lax.broadcasted_iota(jnp.int32, (WIN_KEYS, hn), 0))
    in_win = (rel >= 0) & (rel < WINDOW)
    outs_w = []
    for c in range(n_ch):
        m = jnp.max(jnp.where(in_win, s_w[c], NEG_INF), axis=0, keepdims=True)
        p = jnp.where(in_win, jnp.exp2(s_w[c] - m), 0.0)
        l = jnp.sum(p, axis=0, keepdims=True)
        outs_w.append(jnp.dot(vt_w, p.astype(BF16), preferred_element_type=F32) / l)
    acc_w = jnp.concatenate(outs_w, axis=1)

    blk = lax.broadcasted_iota(jnp.int32, (N_SLC, NSA_TQ), 0)
    q_blk = (q0 + lax.broadcasted_iota(jnp.int32, (N_SLC, NSA_TQ), 1)) >> 6
    forced = (blk == 0) | (blk == q_blk) | (blk == q_blk - 1)
    score = jnp.where(forced, FORCED_SCORE, jnp.where(blk <= q_blk, imp, -1.0))
    rank = jnp.zeros((N_SLC, NSA_TQ), jnp.int32)
    for i in range(N_SLC):
        row = score[i:i + 1, :]
        beats = (row > score) | ((row == score) & (blk > i))
        rank = rank + beats.astype(jnp.int32)
    bias = jnp.where(rank < SLC_TOP, 0.0, NEG_INF).astype(BF16)
    bias = jnp.concatenate([bias] * (hn // NSA_TQ), axis=1)
    qts_s = [jnp.concatenate([qh, bias], axis=0) for qh in qts]
    krow = lax.broadcasted_iota(jnp.int32, (NSA_KV, hn), 0)

    n_full = q0 // NSA_KV
    causal = (n_full * NSA_KV + krow) <= qpos_h
    slc = _causal_flash(lambda kt: [ks_ref[kt]] * n_ch, lambda kt: [vst_ref[kt]] * n_ch, qts_s, n_full, causal, hn)
    acc_s = jnp.concatenate([acc / l for _, l, acc in slc], axis=1)

    g = jax.nn.sigmoid(gate_ref[...])
    o_ref[...] = g[0:1] * o_cmp + g[1:2] * acc_s + g[2:3] * acc_w


def nsa_attention(q, k_cmp, v_cmp, k_slc, v_slc, k_win, v_win, gate_logit):
    B_, S_, H, Dh = q.shape
    G = NSA_GROUPS
    n_qt = S_ // NSA_TQ
    n_cmp = k_cmp.shape[1]
    cs = np.arange(N_CMP_PAD)[None, :] * CMP_STRIDE
    ss = np.arange(N_SLC)[:, None] * SLC_LEN
    ov = np.clip(np.minimum(cs + CMP_LEN, ss + SLC_LEN) - np.maximum(cs, ss), 0, None) / CMP_LEN
    ov[:, n_cmp:] = 0.0
    ovl = jnp.asarray(ov, dtype=BF16)
    qs = _aug_q(q * LOGIT_SCALE).reshape(B_, n_qt, NSA_TQ, G, NSA_HPG, NSA_DK)
    qt = qs.transpose(0, 3, 1, 4, 5, 2).reshape(B_ * G * n_qt, NSA_HPG, NSA_DK, NSA_TQ)
    gt = gate_logit.astype(F32).reshape(B_, n_qt, NSA_TQ, G, NSA_HPG, N_NSA_BRANCH)
    gt = gt.transpose(0, 3, 1, 5, 4, 2).reshape(B_ * G * n_qt, N_NSA_BRANCH, NSA_N)
    tiles = lambda t, kv: t.reshape(B_, S_ // kv, kv, G, -1).transpose(0, 3, 1, 2, 4).reshape(B_ * G, S_ // kv, kv, t.shape[-1])
    keys = lambda t, kv: tiles(_aug_k(t), kv)
    block_onehot = (jnp.arange(S_)[:, None] // SLC_LEN == jnp.arange(N_SLC)[None, :]).astype(BF16)
    k_slc_aug = jnp.concatenate([_aug_k(k_slc), jnp.broadcast_to(block_onehot[None, :, None, :], (B_, S_, G, N_SLC))], axis=-1)
    vals = lambda t, kv: t.astype(BF16).reshape(B_, S_ // kv, kv, G, Dh).transpose(0, 3, 1, 4, 2).reshape(B_ * G, S_ // kv, Dh, kv)
    padc = ((0, 0), (0, N_CMP_PAD - n_cmp), (0, 0), (0, 0))
    kc = _aug_k(jnp.pad(k_cmp, padc)).transpose(0, 2, 1, 3).reshape(B_ * G, N_CMP_PAD, NSA_DK)
    vct = jnp.pad(v_cmp, padc).astype(BF16).transpose(0, 2, 3, 1).reshape(B_ * G, Dh, N_CMP_PAD)
    per_bg = lambda *blk: pl.BlockSpec((None,) + blk, lambda b, i: (b,) + (0,) * len(blk))
    per_tile = lambda *blk: pl.BlockSpec((None,) + blk, lambda b, i: (b * n_qt + i,) + (0,) * len(blk))
    out = pl.pallas_call(
        _nsa_kernel,
        grid=(B_ * G, n_qt),
        in_specs=[per_tile(NSA_HPG, NSA_DK, NSA_TQ),
                  per_bg(N_CMP_PAD, NSA_DK), per_bg(Dh, N_CMP_PAD),
                  per_bg(S_ // NSA_KV, NSA_KV, NSA_DK + N_SLC), per_bg(S_ // NSA_KV, Dh, NSA_KV),
                  per_bg(n_qt, NSA_TQ, NSA_DK), per_bg(n_qt, Dh, NSA_TQ),
                  per_tile(N_NSA_BRANCH, NSA_N),
                  pl.BlockSpec((N_SLC, N_CMP_PAD), lambda b, i: (0, 0))],
        out_specs=per_tile(Dh, NSA_N),
        out_shape=jax.ShapeDtypeStruct((B_ * G * n_qt, Dh, NSA_N), F32),
        name="nsa_attention",
    )(qt, kc, vct, tiles(k_slc_aug, NSA_KV), vals(v_slc, NSA_KV), keys(k_win, NSA_TQ), vals(v_win, NSA_TQ), gt, ovl)
    return out.reshape(B_, G, n_qt, Dh, NSA_N)


DENSE_TM = 256
SEC_MERGE = N_BRANCH * D_MODEL
SEC_SMALL = 256
IN_SECTIONS = (BRANCH_WIDTH,) * 4 + (KV_WIDTH,) * 6 + (SEC_MERGE, SEC_SMALL)


def _rms(x, g):
    return x * lax.rsqrt(jnp.mean(x * x, axis=-1, keepdims=True) + RMS_EPS) * g


def _row_spec(width):
    return pl.BlockSpec((DENSE_TM, width), lambda i: (i, 0))


def _whole_spec(shape):
    return pl.BlockSpec(shape, lambda i: (0,) * len(shape))


def _in_proj_kernel(x_ref, g_ref, w_ref, *o_refs):
    xn = _rms(x_ref[...], g_ref[...]).astype(BF16)
    off = 0
    for o_ref, width in zip(o_refs, IN_SECTIONS):
        o_ref[...] = jnp.dot(xn, w_ref[:, off:off + width], preferred_element_type=F32)
        off += width


def in_projection(x, g, w_in):
    t, d = x.shape
    fq, fk, fv, fl, nq, kc, vc, ksl, vsl, kwn, vwn, gl, ml = split_columns(w_in, IN_SPLITS)
    pad = lambda w: jnp.pad(w, ((0, 0), (0, 128 - w.shape[1])))
    w = jnp.concatenate([fq, fk, fv, nq, kc, vc, ksl, vsl, kwn, vwn, ml, pad(fl), pad(gl)], axis=1).astype(BF16)
    return pl.pallas_call(
        _in_proj_kernel,
        grid=(t // DENSE_TM,),
        in_specs=[_row_spec(d), _whole_spec((1, d)), _whole_spec(w.shape)],
        out_specs=[_row_spec(s) for s in IN_SECTIONS],
        out_shape=[jax.ShapeDtypeStruct((t, s), F32) for s in IN_SECTIONS],
        name="in_projection",
    )(x, g.reshape(1, d), w)


def _merge_kernel(x_ref, yf_ref, yn_ref, ml_ref, wb_ref, wo_ref, gq_ref, wq_ref, o_ref, hn_ref, q_ref):
    yf_t = yf_ref[...].reshape(BRANCH_WIDTH, DENSE_TM)
    yn_t = jnp.concatenate(
        [jnp.concatenate([yn_ref[g, j, :, h * NSA_TQ:(h + 1) * NSA_TQ]
                          for g in range(NSA_GROUPS) for h in range(NSA_HPG)], axis=0)
         for j in range(DENSE_TM // NSA_TQ)], axis=1)
    g = jax.nn.sigmoid(ml_ref[...])
    up_f = jnp.dot(yf_t.T.astype(BF16), wb_ref[0], preferred_element_type=F32)
    up_n = jnp.dot(yn_t.T.astype(BF16), wb_ref[1], preferred_element_type=F32)
    merged = g[:, :D_MODEL] * up_f + g[:, D_MODEL:] * up_n
    h = x_ref[...] + jnp.dot(merged.astype(BF16), wo_ref[...], preferred_element_type=F32)
    o_ref[...] = h
    hn = _rms(h, gq_ref[...])
    hn_ref[...] = hn
    q = jnp.dot(hn.astype(BF16), wq_ref[...], preferred_element_type=F32)
    dq = PEER_QDIM // 2
    for j in range(2 * PEER_HEADS):
        q_ref[j] = q[:, j * dq:(j + 1) * dq]


def merge_out(x, y_fox, y_nsa, merge_logit, w_branch, w_out, norm_ffn, wq):
    t, d = x.shape
    hp, dq = 2 * PEER_HEADS, PEER_QDIM // 2
    assert FOX_TQ == DENSE_TM and DENSE_TM % NSA_TQ == 0
    tiles_per_seq = y_fox.shape[2]
    nsa_per_tile = DENSE_TM // NSA_TQ
    yf_spec = pl.BlockSpec((None,) + y_fox.shape[1:2] + (None,) + y_fox.shape[3:],
                           lambda i: (i // tiles_per_seq, 0, i % tiles_per_seq, 0, 0, 0))
    yn_spec = pl.BlockSpec((None, NSA_GROUPS, nsa_per_tile, HEAD_DIM, NSA_N),
                           lambda i: (i // tiles_per_seq, 0, i % tiles_per_seq, 0, 0))
    return pl.pallas_call(
        _merge_kernel,
        grid=(t // DENSE_TM,),
        in_specs=[_row_spec(d), yf_spec, yn_spec, _row_spec(SEC_MERGE),
                  _whole_spec(w_branch.shape), _whole_spec(w_out.shape), _whole_spec((1, d)), _whole_spec(wq.shape)],
        out_specs=[_row_spec(d), _row_spec(d), pl.BlockSpec((hp, DENSE_TM, dq), lambda i: (0, i, 0))],
        out_shape=[jax.ShapeDtypeStruct((t, d), F32), jax.ShapeDtypeStruct((t, d), F32),
                   jax.ShapeDtypeStruct((hp, t, dq), F32)],
        name="merge_out",
    )(x, y_fox, y_nsa, merge_logit, w_branch.astype(BF16), w_out.astype(BF16), norm_ffn.reshape(1, d),
      wq.astype(BF16))


N_CMP_ROWS = 256


def _compress_kernel(r_ref, pos_ref, w1_ref, w2_ref, o_ref):
    r = r_ref[...]
    nxt = jnp.concatenate([r[1:], jnp.zeros((1, r.shape[1]), F32)], axis=0)
    half = r.shape[1]
    hid = (jnp.dot((r + pos_ref[0:1, :]).astype(BF16), w1_ref[:half], preferred_element_type=F32)
           + jnp.dot((nxt + pos_ref[1:2, :]).astype(BF16), w1_ref[half:], preferred_element_type=F32))
    o_ref[...] = jnp.dot(_gelu(hid).astype(BF16), w2_ref[...], preferred_element_type=F32)


def compress_kv_pallas(kv, cmp_pos, w1, w2):
    _, B_, S_, G, Dh = kv.shape
    n_cmp = (S_ - CMP_LEN) // CMP_STRIDE + 1
    width = CMP_STRIDE * Dh
    rows = kv.transpose(0, 1, 3, 2, 4).reshape(2, B_ * G, S_ // CMP_STRIDE, width)
    out = pl.pallas_call(
        _compress_kernel,
        grid=(2, B_ * G),
        in_specs=[pl.BlockSpec((None, None, N_CMP_ROWS, width), lambda j, b: (j, b, 0, 0)),
                  pl.BlockSpec((None, 2, width), lambda j, b: (j, 0, 0)),
                  pl.BlockSpec((None, CMP_LEN * Dh, CMP_HIDDEN), lambda j, b: (j, 0, 0)),
                  pl.BlockSpec((None, CMP_HIDDEN, Dh), lambda j, b: (j, 0, 0))],
        out_specs=pl.BlockSpec((None, None, N_CMP_ROWS, Dh), lambda j, b: (j, b, 0, 0)),
        out_shape=jax.ShapeDtypeStruct((2, B_ * G, N_CMP_ROWS, Dh), F32),
        name="compress_kv",
    )(rows, cmp_pos.reshape(2, 2, width), w1.astype(BF16), w2.astype(BF16))
    return out[:, :, :n_cmp].reshape(2, B_, G, n_cmp, Dh).transpose(0, 1, 3, 2, 4)


def hybrid_mixer(x, norm_g, w_in, fox_f_bias, cmp_pos, cmp_w1, cmp_w2, w_branch, w_out, norm_ffn, wq):
    B_, S_, D = x.shape
    xf = x.reshape(B_ * S_, D)
    secs = in_projection(xf, norm_g, w_in)
    heads = lambda t, n: t.reshape(B_, S_, n, HEAD_DIM)
    fq, fk, fv, nq = secs[:4]
    kc, vc, ksl, vsl, kwn, vwn = (heads(t, NSA_GROUPS) for t in secs[4:10])
    merge_logit, small = secs[10:]
    f_logit = small[:, :FOX_HEADS].reshape(B_, S_, FOX_HEADS)
    gate_logit = small[:, 128:128 + NSA_HEADS * N_NSA_BRANCH].reshape(B_, S_, NSA_HEADS, N_NSA_BRANCH)
    log_f = jax.nn.log_sigmoid(f_logit + fox_f_bias.astype(F32))
    y_fox = fox_attention(heads(fq, FOX_HEADS), heads(fk, FOX_HEADS), heads(fv, FOX_HEADS), log_f)
    cos, sin = rope_tables(jnp.arange(S_, dtype=F32))
    q_nsa = apply_partial_rope(heads(nq, NSA_HEADS), cos, sin)
    k_slc = apply_partial_rope(ksl, cos, sin)
    k_win = apply_partial_rope(kwn, cos, sin)
    cmp = compress_kv_pallas(jnp.stack([kc, vc]), cmp_pos, cmp_w1, cmp_w2)
    n_cmp = cmp.shape[2]
    cmp_end = jnp.arange(n_cmp, dtype=F32) * CMP_STRIDE + (CMP_LEN - 1)
    k_cmp = apply_partial_rope(cmp[0], *rope_tables(cmp_end))
    y_nsa = nsa_attention(q_nsa, k_cmp, cmp[1], k_slc, vsl, k_win, vwn, gate_logit)
    return merge_out(xf, y_fox, y_nsa, merge_logit, w_branch, w_out, norm_ffn, wq)


def peer_ffn(h, hn, q, subkeys, u, v, norm_final):
    idx2, w = peer_topk(q, subkeys)
    coef = peer_u(idx2, hn, w, pack_table(u))
    return peer_v(idx2, coef, pack_table(v), h, norm_final)


def kernel(x, norm_mix, w_in, fox_f_bias, nsa_cmp_pos, nsa_cmp_w1, nsa_cmp_w2, w_branch, w_out,
           norm_ffn, peer_wq, peer_subkeys, peer_u, peer_v, norm_final):
    B_, S_, D = x.shape
    assert norm_mix.shape[0] == 1, "single-layer trunk"
    h, hn, q = hybrid_mixer(x, norm_mix[0], w_in[0], fox_f_bias[0], nsa_cmp_pos[0], nsa_cmp_w1[0], nsa_cmp_w2[0],
                            w_branch[0], w_out[0], norm_ffn[0], peer_wq[0])
    out = peer_ffn(h, hn, q, peer_subkeys[0], peer_u[0], peer_v[0], norm_final)
    return out.reshape(B_, S_, D)
```

```python
import jax
import jax.numpy as jnp
import numpy as np
from jax import lax
from jax.experimental import pallas as pl
from jax.experimental.pallas import tpu as pltpu

D_MODEL = 1024
HEAD_DIM = 64
FOX_HEADS = 8
NSA_HEADS = 8
NSA_GROUPS = 2
NSA_HPG = NSA_HEADS // NSA_GROUPS
BRANCH_WIDTH = 512
N_BRANCH = 2
N_NSA_BRANCH = 3
ROPE_DIM = HEAD_DIM // 4
ROPE_THETA = 500000.0
CMP_LEN = 32
CMP_STRIDE = 16
CMP_HIDDEN = 2 * HEAD_DIM
SLC_LEN = 64
SLC_TOP = 16
WINDOW = 512
FORCED_SCORE = 1e9
NEG_INF = -1e30
PEER_HEADS = 8
N_KEYS = 128
PEER_QDIM = 256
PEER_TOPK = 16
RMS_EPS = 1e-6
KV_WIDTH = NSA_GROUPS * HEAD_DIM
IN_SPLITS = (BRANCH_WIDTH, BRANCH_WIDTH, BRANCH_WIDTH, FOX_HEADS, BRANCH_WIDTH,
             KV_WIDTH, KV_WIDTH, KV_WIDTH, KV_WIDTH, KV_WIDTH, KV_WIDTH,
             NSA_HEADS * N_NSA_BRANCH, N_BRANCH * D_MODEL)


BF16 = jnp.bfloat16
F32 = jnp.float32

PEER_SLOTS = PEER_HEADS * PEER_TOPK
PEER_TT = 128
PEER_IDX_REFS = PEER_HEADS
IDX_W = PEER_SLOTS // PEER_IDX_REFS
HALF_REFS = PEER_IDX_REFS // 2
PEER_U_UNROLL = 128
PEER_V_UNROLL = 128
ROW_SUB = D_MODEL // 2 // 128
VMEM_LIMIT_PEER = N_KEYS * N_KEYS * D_MODEL * 2 + 16 * 1024 * 1024


def pack_table(tab):
    n, d = tab.shape
    return pl.pallas_call(
        _pack_kernel,
        grid=(n // PACK_TM,),
        in_specs=[pl.BlockSpec((PACK_TM, d), lambda i: (i, 0))],
        out_specs=pl.BlockSpec((PACK_TM * ROW_SUB, 128), lambda i: (i, 0)),
        out_shape=jax.ShapeDtypeStruct((n * ROW_SUB, 128), jnp.uint32),
        name="pack_table",
    )(tab)


PACK_TM = 256


def _pack_kernel(x_ref, o_ref):
    bits = lax.bitcast_convert_type(x_ref[...].astype(jnp.bfloat16).astype(jnp.float32), jnp.uint32)
    for r in range(ROW_SUB):
        even = bits[:, (2 * r) * 128:(2 * r + 1) * 128]
        odd = bits[:, (2 * r + 1) * 128:(2 * r + 2) * 128]
        o_ref[pl.ds(r, PACK_TM, stride=ROW_SUB), :] = (even >> 16) | (odd & jnp.uint32(0xFFFF0000))


def _expert_slab(tab_ref, row):
    return tab_ref[pl.ds(pl.multiple_of(row, ROW_SUB), ROW_SUB), :]


def _unpack(w):
    lo = lax.bitcast_convert_type(w << 16, jnp.float32)
    hi = lax.bitcast_convert_type(w & jnp.uint32(0xFFFF0000), jnp.float32)
    return lo, hi


def _gelu(x):
    return 0.5 * x * (1.0 + lax.erf(x * (2.0 ** -0.5)))


def _pair_order():
    return [k * IDX_W + q for q in range(IDX_W) for k in range(HALF_REFS)]


def _pair_slab(idx_refs, tab_ref, t, s):
    k, q = divmod(s, IDX_W)
    off = t * IDX_W + q
    return jnp.concatenate([_expert_slab(tab_ref, idx_refs[k][off]),
                            _expert_slab(tab_ref, idx_refs[k + HALF_REFS][off])], axis=0)


def _gather_token(idx_refs, tab_ref, t):
    slabs = [_pair_slab(idx_refs, tab_ref, t, s) for s in _pair_order()]
    return pltpu.bitcast(jnp.concatenate(slabs, axis=0), BF16)


def _own_chunk(width):
    return ((lax.broadcasted_iota(jnp.int32, (N_CHUNK, width), 1) & (N_CHUNK - 1))
            == lax.broadcasted_iota(jnp.int32, (N_CHUNK, width), 0))


def _peer_u_kernel(*refs):
    idx_refs, (x_ref, w_ref, tab_ref, out_ref) = refs[:PEER_IDX_REFS], refs[PEER_IDX_REFS:]
    lane = lax.broadcasted_iota(jnp.int32, (2 * ROW_SUB, 128), 1)
    lower = lax.broadcasted_iota(jnp.int32, (2 * ROW_SUB, 128), 0) >= ROW_SUB
    pair_lane = jnp.where(lower, lane - PEER_SLOTS // 2, lane)

    def token(t):
        row = x_ref[pl.ds(t, 1), :]
        chunks = [row[:, c * 128:(c + 1) * 128] for c in range(2 * ROW_SUB)]
        xl = jnp.concatenate(chunks[0::2] * 2, axis=0)
        xh = jnp.concatenate(chunks[1::2] * 2, axis=0)
        acc = jnp.zeros((2 * ROW_SUB, 128), jnp.float32)
        for s in _pair_order():
            lo, hi = _unpack(_pair_slab(idx_refs, tab_ref, t, s))
            part = jnp.sum(lo * xl + hi * xh, axis=1, keepdims=True)
            acc = jnp.where(pair_lane == s, part, acc)
        out_ref[pl.ds(t, 1), :] = w_ref[pl.ds(t, 1), :] * _gelu(jnp.sum(acc, axis=0, keepdims=True))

    def trip(i, carry):
        for j in range(PEER_U_UNROLL):
            token(i * PEER_U_UNROLL + j)
        return carry

    lax.fori_loop(0, PEER_TT // PEER_U_UNROLL, trip, 0)


N_CHUNK = 2 * ROW_SUB


def _peer_v_kernel(*refs):
    idx_refs = refs[:PEER_IDX_REFS]
    coef_ref, expand_ref, h_ref, g_ref, tab_ref, out_ref, ce_hi_ref, ce_lo_ref = refs[PEER_IDX_REFS:]
    width = PEER_SLOTS * N_CHUNK
    coef = coef_ref[...]
    c_hi = _trunc_bf16(coef)
    expand = expand_ref[...]
    ce_hi_ref[...] = jnp.dot(c_hi.astype(BF16), expand, preferred_element_type=F32)
    ce_lo_ref[...] = jnp.dot((coef - c_hi).astype(BF16), expand, preferred_element_type=F32)
    own_chunk = _own_chunk(width)

    def token(t):
        gathered = _gather_token(idx_refs, tab_ref, t)
        sel = lambda ref: jnp.where(own_chunk, ref[pl.ds(t, 1), :], 0.0).astype(BF16)
        c = jnp.concatenate([sel(ce_hi_ref), sel(ce_lo_ref)], axis=0)
        o = jnp.dot(c, gathered, preferred_element_type=F32)
        o = o[:N_CHUNK] + o[N_CHUNK:]
        out_ref[pl.ds(t, 1), :] = jnp.concatenate([o[r:r + 1, :] for r in range(N_CHUNK)], axis=1)

    def trip(i, carry):
        for j in range(PEER_V_UNROLL):
            token(i * PEER_V_UNROLL + j)
        return carry

    lax.fori_loop(0, PEER_TT // PEER_V_UNROLL, trip, 0)
    out_ref[...] = _rms(h_ref[...] + out_ref[...], g_ref[...])


def _smem_specs():
    return [pl.BlockSpec((PEER_TT * IDX_W,), lambda i: (i,), memory_space=pltpu.SMEM)] * PEER_IDX_REFS


def _slot_groups(a):
    return [a[:, g * IDX_W:(g + 1) * IDX_W].reshape(-1) for g in range(PEER_IDX_REFS)]


def _slot_columns():
    half = PEER_SLOTS // 2
    block = np.arange(PEER_SLOTS * N_CHUNK) // N_CHUNK
    col_slot = np.asarray(_pair_order())[block // 2] + half * (block % 2)
    return jnp.asarray(col_slot[None, :] == np.arange(PEER_SLOTS)[:, None], dtype=BF16)


def _table_spec(rows):
    return pl.BlockSpec((rows, 128), lambda i: (0, 0), pipeline_mode=pl.Buffered(1))


def peer_u(idx, x, w, tab):
    t = x.shape[0]
    return pl.pallas_call(
        _peer_u_kernel,
        grid=(t // PEER_TT,),
        in_specs=_smem_specs() + [pl.BlockSpec((PEER_TT, D_MODEL), lambda i: (i, 0)),
                                  pl.BlockSpec((PEER_TT, PEER_SLOTS), lambda i: (i, 0)), _table_spec(tab.shape[0])],
        out_specs=pl.BlockSpec((PEER_TT, PEER_SLOTS), lambda i: (i, 0)),
        out_shape=jax.ShapeDtypeStruct((t, PEER_SLOTS), jnp.float32),
        compiler_params=pltpu.CompilerParams(vmem_limit_bytes=VMEM_LIMIT_PEER),
        name="peer_u",
    )(*_slot_groups(idx), x, w, tab)


def peer_v(idx, coef, tab, h, g):
    t = idx.shape[0]
    width = PEER_SLOTS * N_CHUNK
    expand = _slot_columns()
    return pl.pallas_call(
        _peer_v_kernel,
        grid=(t // PEER_TT,),
        in_specs=_smem_specs() + [
            pl.BlockSpec((PEER_TT, PEER_SLOTS), lambda i: (i, 0)), pl.BlockSpec((PEER_SLOTS, width), lambda i: (0, 0)),
            pl.BlockSpec((PEER_TT, D_MODEL), lambda i: (i, 0)), pl.BlockSpec((1, D_MODEL), lambda i: (0, 0)),
            _table_spec(tab.shape[0])],
        out_specs=pl.BlockSpec((PEER_TT, D_MODEL), lambda i: (i, 0)),
        out_shape=jax.ShapeDtypeStruct((t, D_MODEL), jnp.float32),
        scratch_shapes=[pltpu.VMEM((PEER_TT, width), F32), pltpu.VMEM((PEER_TT, width), F32)],
        compiler_params=pltpu.CompilerParams(vmem_limit_bytes=VMEM_LIMIT_PEER),
        name="peer_v",
    )(*_slot_groups(idx), coef, expand, h, g.reshape(1, D_MODEL), tab)


PK_TT = 128
PK_A_BLOCKS = ((0, 16), (1, 8), (2, 5), (3, 4))
PK_B_BLOCKS = ((0, 4, 16), (1, 4, 8), (2, 4, 5))
NEG_HUGE = -3.0e38
PK_HEADS_PER_TRIP = 8


def _extract_top(problems, flat, n):
    vals = [[] for _ in problems]
    picked = [[] for _ in problems]
    cur = [v for v, _ in problems]
    for _ in range(n):
        for i, (_, payload) in enumerate(problems):
            m = jnp.max(cur[i], axis=0, keepdims=True)
            pos = jnp.min(jnp.where(cur[i] == m, flat, 1e9), axis=0, keepdims=True)
            hit = flat == pos
            picked[i].append(pos if payload is None else jnp.sum(jnp.where(hit, payload, 0.0), axis=0, keepdims=True))
            cur[i] = jnp.where(hit, NEG_HUGE, cur[i])
            vals[i].append(m)
    return [jnp.concatenate(v, axis=0) for v in vals], [jnp.concatenate(p, axis=0) for p in picked]


def _candidate_grid(first, second, combine):
    up8 = lambda n: -(-n // 8) * 8
    return jnp.concatenate([combine(first[a:a + 1], second[0:up8(nb)]) for a, nb in PK_A_BLOCKS]
                           + [combine(first[0:up8(a1)], second[b:b + 1]) for b, _, a1 in PK_B_BLOCKS], axis=0)


def _candidate_valid_and_flat():
    up8 = lambda n: -(-n // 8) * 8
    valid, flat = [], []
    for a, nb in PK_A_BLOCKS:
        r = lax.broadcasted_iota(jnp.int32, (up8(nb), PK_TT), 0)
        valid.append(r < nb)
        flat.append(a * PEER_TOPK + r)
    for b, a0, a1 in PK_B_BLOCKS:
        r = lax.broadcasted_iota(jnp.int32, (up8(a1), PK_TT), 0)
        valid.append((r >= a0) & (r < a1))
        flat.append(r * PEER_TOPK + b)
    valid = jnp.concatenate(valid, axis=0)
    return valid, jnp.where(valid, jnp.concatenate(flat, axis=0).astype(F32), 1e9)


def _peer_topk_kernel(q_ref, ka_ref, eid_ref, w_ref):
    key_row = lax.broadcasted_iota(jnp.int32, (N_KEYS, PK_TT), 0).astype(F32)
    valid, flat = _candidate_valid_and_flat()

    def heads(i, carry):
        scores = []
        for j in range(2 * PK_HEADS_PER_TRIP):
            q = q_ref[2 * PK_HEADS_PER_TRIP * i + j]
            q_hi = _trunc_bf16(q)
            qa = jnp.concatenate([q_hi.astype(BF16), (q - q_hi).astype(BF16), q_hi.astype(BF16)], axis=1)
            scores.append(lax.dot_general(ka_ref[2 * PK_HEADS_PER_TRIP * i + j], qa, (((1,), (1,)), ((), ())),
                                          preferred_element_type=F32))
        sub_s, sub_i = _extract_top([(s, None) for s in scores], key_row, PEER_TOPK)
        cands = []
        for j in range(PK_HEADS_PER_TRIP):
            s1, s2, i1, i2 = sub_s[2 * j], sub_s[2 * j + 1], sub_i[2 * j], sub_i[2 * j + 1]
            cands.append((jnp.where(valid, _candidate_grid(s1, s2, lambda x, y: x + y), NEG_HUGE),
                          _candidate_grid(i1, i2, lambda x, y: x * N_KEYS + y)))
        top_s, top_e = _extract_top(cands, flat, PEER_TOPK)
        for j in range(PK_HEADS_PER_TRIP):
            e = jnp.exp(top_s[j] - top_s[j][0:1])
            w_ref[PK_HEADS_PER_TRIP * i + j] = e / jnp.sum(e, axis=0, keepdims=True)
            eid_ref[PK_HEADS_PER_TRIP * i + j] = top_e[j].astype(jnp.int32) * ROW_SUB
        return carry

    lax.fori_loop(0, PEER_HEADS // PK_HEADS_PER_TRIP, heads, 0)


def peer_topk(qt, subkeys):
    hp, t, d = qt.shape
    k_hi, k_lo = _split2(subkeys.reshape(hp, N_KEYS, d))
    ka = jnp.concatenate([k_hi, k_hi, k_lo], axis=2)
    eid, w = pl.pallas_call(
        _peer_topk_kernel,
        grid=(t // PK_TT,),
        in_specs=[pl.BlockSpec((hp, PK_TT, d), lambda i: (0, i, 0)),
                  pl.BlockSpec((hp, N_KEYS, 3 * d), lambda i: (0, 0, 0))],
        out_specs=[pl.BlockSpec((PEER_HEADS, PEER_TOPK, PK_TT), lambda i: (0, 0, i)),
                   pl.BlockSpec((PEER_HEADS, PEER_TOPK, PK_TT), lambda i: (0, 0, i))],
        out_shape=[jax.ShapeDtypeStruct((PEER_HEADS, PEER_TOPK, t), jnp.int32),
                   jax.ShapeDtypeStruct((PEER_HEADS, PEER_TOPK, t), F32)],
        name="peer_topk",
    )(qt, ka)
    to_slots = lambda a: a.transpose(2, 0, 1).reshape(t, PEER_SLOTS)
    return to_slots(eid), to_slots(w)


def split_columns(t, sizes):
    offs = np.cumsum((0,) + tuple(sizes))
    return [t[..., int(a):int(b)] for a, b in zip(offs[:-1], offs[1:])]


def rope_tables(pos):
    inv = jnp.power(ROPE_THETA, -jnp.arange(0, ROPE_DIM, 2, dtype=jnp.float32) / ROPE_DIM)
    ang = pos[:, None] * inv[None, :]
    return jnp.cos(ang), jnp.sin(ang)


def apply_partial_rope(x, cos, sin):
    half = ROPE_DIM // 2
    xr = x[..., :ROPE_DIM].astype(jnp.float32)
    x1, x2 = xr[..., :half], xr[..., half:]
    c, s = cos[:, None, :], sin[:, None, :]
    rot = jnp.concatenate([x1 * c - x2 * s, x1 * s + x2 * c], axis=-1)
    return jnp.concatenate([rot.astype(x.dtype), x[..., ROPE_DIM:]], axis=-1)


LOG2_E = 1.4426950408889634
LOGIT_SCALE = HEAD_DIM ** -0.5 * LOG2_E


def _logits(ks, qts):
    return [jnp.dot(k, qt, preferred_element_type=F32) for k, qt in zip(ks, qts)]


def _flash_update(ss, vts, mask, carries):
    stats, ps = [], []
    for s, (m, l, _) in zip(ss, carries):
        s_vis = s if mask is None else jnp.where(mask, s, NEG_INF)
        m_new = jnp.maximum(m, jnp.max(s_vis, axis=0, keepdims=True))
        alpha = jnp.exp2(m - m_new)
        p = jnp.exp2(s - m_new)
        if mask is not None:
            p = jnp.where(mask, p, 0.0)
        stats.append((m_new, alpha, alpha * l + jnp.sum(p, axis=0, keepdims=True)))
        ps.append(p.astype(BF16))
    return tuple((m_new, l, alpha * acc + jnp.dot(vt, p, preferred_element_type=F32))
                 for (m_new, alpha, l), p, vt, (_, _, acc) in zip(stats, ps, vts, carries))


def _causal_flash(key_tile, value_tile, qts, n_full, last_mask, n):
    def body(kt, carries):
        return _flash_update(_logits(key_tile(kt), qts), value_tile(kt), None, carries)

    carries = lax.fori_loop(0, n_full, body, tuple(_flash_init(n) for _ in qts))
    return _flash_update(_logits(key_tile(n_full), qts), value_tile(n_full), last_mask, carries)


def _flash_init(n):
    return (jnp.full((1, n), NEG_INF, F32), jnp.zeros((1, n), F32), jnp.zeros((HEAD_DIM, n), F32))


FOX_TQ = 256
FOX_KV = 512
FOX_DK = 256


FOX_HP = 4


def _fox_kernel(qt_ref, k_ref, vt_ref, o_ref):
    qi = pl.program_id(1)
    q0 = qi * FOX_TQ
    qts = [qt_ref[h] for h in range(FOX_HP)]
    qpos = q0 + lax.broadcasted_iota(jnp.int32, (1, FOX_TQ), 1)
    krow = lax.broadcasted_iota(jnp.int32, (FOX_KV, FOX_TQ), 0)

    n_full = q0 // FOX_KV
    causal = (n_full * FOX_KV + krow) <= qpos
    carry = _causal_flash(lambda kt: [k_ref[h, kt] for h in range(FOX_HP)],
                          lambda kt: [vt_ref[h, kt] for h in range(FOX_HP)], qts, n_full, causal, FOX_TQ)
    for h in range(FOX_HP):
        _, l, acc = carry[h]
        o_ref[h] = acc / l


def _trunc_bf16(x):
    bits = lax.bitcast_convert_type(x, jnp.uint32) & jnp.uint32(0xFFFF0000)
    return lax.bitcast_convert_type(bits, F32)


def _split3(c):
    c1 = _trunc_bf16(c)
    r = c - c1
    c2 = _trunc_bf16(r)
    return c1.astype(BF16), c2.astype(BF16), (r - c2).astype(BF16)


def _split2(x):
    hi = _trunc_bf16(x)
    return hi.astype(BF16), (x - hi).astype(BF16)


def _aug_q(q):
    hi, lo = _split2(q)
    return jnp.concatenate([hi, hi, lo], axis=-1)


def _aug_k(k):
    hi, lo = _split2(k)
    return jnp.concatenate([hi, lo, hi], axis=-1)


def fox_attention(q, k, v, log_f):
    B_, S_, H, Dh = q.shape
    KV_TILE = FOX_KV
    n_qt, n_kt = S_ // FOX_TQ, S_ // KV_TILE
    c = jnp.cumsum(log_f, axis=1)
    c1, c2, c3 = _split3(c * LOG2_E)
    j = jnp.arange(FOX_DK - 3 * Dh)
    pick = lambda t: t[..., None].astype(F32)
    terms = lambda first: jnp.where(j == first, pick(c1), jnp.where(j == first + 1, pick(c2), pick(c3)))
    q_extra = jnp.where(j < 3, 1.0, jnp.where(j < 6, terms(3), 0.0)).astype(BF16)
    k_extra = jnp.where(j < 3, -terms(0), jnp.where(j < 6, 1.0, 0.0)).astype(BF16)
    qa = jnp.concatenate([_aug_q(q * LOGIT_SCALE), q_extra], axis=-1)
    ka = jnp.concatenate([_aug_k(k), k_extra], axis=-1)
    HP, HG = FOX_HP, H // FOX_HP
    qt = qa.reshape(B_, n_qt, FOX_TQ, HG, HP, FOX_DK).transpose(0, 3, 1, 4, 5, 2).reshape(B_ * HG * n_qt, HP, FOX_DK, FOX_TQ)
    kk = ka.reshape(B_, n_kt, KV_TILE, HG, HP, FOX_DK).transpose(0, 3, 4, 1, 2, 5).reshape(B_ * HG, HP, n_kt, KV_TILE, FOX_DK)
    vt = v.astype(BF16).reshape(B_, n_kt, KV_TILE, HG, HP, Dh).transpose(0, 3, 4, 1, 5, 2).reshape(B_ * HG, HP, n_kt, Dh, KV_TILE)
    out = pl.pallas_call(
        _fox_kernel,
        grid=(B_ * HG, n_qt),
        in_specs=[pl.BlockSpec((None, HP, FOX_DK, FOX_TQ), lambda b, i: (b * n_qt + i, 0, 0, 0)),
                  pl.BlockSpec((None, HP, n_kt, KV_TILE, FOX_DK), lambda b, i: (b, 0, 0, 0, 0)),
                  pl.BlockSpec((None, HP, n_kt, Dh, KV_TILE), lambda b, i: (b, 0, 0, 0, 0))],
        out_specs=pl.BlockSpec((None, HP, Dh, FOX_TQ), lambda b, i: (b * n_qt + i, 0, 0, 0)),
        out_shape=jax.ShapeDtypeStruct((B_ * HG * n_qt, HP, Dh, FOX_TQ), F32),
        name="fox_attention",
    )(qt, kk, vt)
    return out.reshape(B_, HG, n_qt, HP, Dh, FOX_TQ)


NSA_TQ = 256
NSA_N = NSA_HPG * NSA_TQ
NSA_CH = 256
N_CMP_PAD = 256
N_SLC = 64
NSA_DK = 3 * HEAD_DIM
NSA_KV = 512
WIN_TILES = WINDOW // NSA_TQ + 1
WIN_KEYS = WIN_TILES * NSA_TQ


def _nsa_kernel(qt_ref, kc_ref, vct_ref, ks_ref, vst_ref, kw_ref, vwt_ref, gate_ref, ovl_ref, o_ref):
    qi = pl.program_id(1)
    q0 = qi * NSA_TQ
    qt = jnp.concatenate([qt_ref[h] for h in range(NSA_HPG)], axis=1)
    lane = lax.broadcasted_iota(jnp.int32, (1, NSA_N), 1)
    qpos = q0 + (lane & (NSA_TQ - 1))

    hn = NSA_CH
    n_ch = NSA_N // NSA_CH
    qts = [qt[:, c * hn:(c + 1) * hn] for c in range(n_ch)]
    qpos_h = qpos[:, :hn]

    w0 = jnp.maximum(qi - WINDOW // NSA_TQ, 0)
    k_w = jnp.concatenate([kw_ref[w0 + i] for i in range(WIN_TILES)], axis=0)
    vt_w = jnp.concatenate([vwt_ref[w0 + i] for i in range(WIN_TILES)], axis=1)
    s = jnp.dot(kc_ref[...], qt, preferred_element_type=F32)
    s_w = [jnp.dot(k_w, qts[c], preferred_element_type=F32) for c in range(n_ch)]

    cmp_end = lax.broadcasted_iota(jnp.int32, (N_CMP_PAD, NSA_N), 0) * CMP_STRIDE + (CMP_LEN - 1)
    mc = cmp_end <= qpos
    m = jnp.max(jnp.where(mc, s, NEG_INF), axis=0, keepdims=True)
    p = jnp.where(mc, jnp.exp2(s - m), 0.0)
    l = jnp.sum(p, axis=0, keepdims=True)
    pc = p * jnp.where(l > 0.0, 1.0 / l, 0.0)
    o_cmp = jnp.dot(vct_ref[...], pc.astype(BF16), preferred_element_type=F32)
    pcs = sum(pc[:, h * NSA_TQ:(h + 1) * NSA_TQ] for h in range(NSA_HPG))
    pcs_hi = pcs.astype(BF16)
    pcs_lo = (pcs - pcs_hi.astype(F32)).astype(BF16)
    ovl = ovl_ref[...]
    imp = jnp.dot(ovl, pcs_hi, preferred_element_type=F32) + jnp.dot(ovl, pcs_lo, preferred_element_type=F32)

    rel = qpos_h - (w0 * NSA_TQ + lax.broadcasted_iota(jnp.int32, (WIN_KEYS, hn), 0))
    in_win = (rel >= 0) & (rel < WINDOW)
    outs_w = []
    for c in range(n_ch):
        m = jnp.max(jnp.where(in_win, s_w[c], NEG_INF), axis=0, keepdims=True)
        p = jnp.where(in_win, jnp.exp2(s_w[c] - m), 0.0)
        l = jnp.sum(p, axis=0, keepdims=True)
        outs_w.append(jnp.dot(vt_w, p.astype(BF16), preferred_element_type=F32) / l)
    acc_w = jnp.concatenate(outs_w, axis=1)

    blk = lax.broadcasted_iota(jnp.int32, (N_SLC, NSA_TQ), 0)
    q_blk = (q0 + lax.broadcasted_iota(jnp.int32, (N_SLC, NSA_TQ), 1)) >> 6
    forced = (blk == 0) | (blk == q_blk) | (blk == q_blk - 1)
    score = jnp.where(forced, FORCED_SCORE, jnp.where(blk <= q_blk, imp, -1.0))
    rank = jnp.zeros((N_SLC, NSA_TQ), jnp.int32)
    for i in range(N_SLC):
        row = score[i:i + 1, :]
        beats = (row > score) | ((row == score) & (blk > i))
        rank = rank + beats.astype(jnp.int32)
    bias = jnp.where(rank < SLC_TOP, 0.0, NEG_INF).astype(BF16)
    bias = jnp.concatenate([bias] * (hn // NSA_TQ), axis=1)
    qts_s = [jnp.concatenate([qh, bias], axis=0) for qh in qts]
    krow = lax.broadcasted_iota(jnp.int32, (NSA_KV, hn), 0)

    n_full = q0 // NSA_KV
    causal = (n_full * NSA_KV + krow) <= qpos_h
    slc = _causal_flash(lambda kt: [ks_ref[kt]] * n_ch, lambda kt: [vst_ref[kt]] * n_ch, qts_s, n_full, causal, hn)
    acc_s = jnp.concatenate([acc / l for _, l, acc in slc], axis=1)

    g = jax.nn.sigmoid(gate_ref[...])
    o_ref[...] = g[0:1] * o_cmp + g[1:2] * acc_s + g[2:3] * acc_w


def nsa_attention(q, k_cmp, v_cmp, k_slc, v_slc, k_win, v_win, gate_logit):
    B_, S_, H, Dh = q.shape
    G = NSA_GROUPS
    n_qt = S_ // NSA_TQ
    n_cmp = k_cmp.shape[1]
    cs = np.arange(N_CMP_PAD)[None, :] * CMP_STRIDE
    ss = np.arange(N_SLC)[:, None] * SLC_LEN
    ov = np.clip(np.minimum(cs + CMP_LEN, ss + SLC_LEN) - np.maximum(cs, ss), 0, None) / CMP_LEN
    ov[:, n_cmp:] = 0.0
    ovl = jnp.asarray(ov, dtype=BF16)
    qs = _aug_q(q * LOGIT_SCALE).reshape(B_, n_qt, NSA_TQ, G, NSA_HPG, NSA_DK)
    qt = qs.transpose(0, 3, 1, 4, 5, 2).reshape(B_ * G * n_qt, NSA_HPG, NSA_DK, NSA_TQ)
    gt = gate_logit.astype(F32).reshape(B_, n_qt, NSA_TQ, G, NSA_HPG, N_NSA_BRANCH)
    gt = gt.transpose(0, 3, 1, 5, 4, 2).reshape(B_ * G * n_qt, N_NSA_BRANCH, NSA_N)
    tiles = lambda t, kv: t.reshape(B_, S_ // kv, kv, G, -1).transpose(0, 3, 1, 2, 4).reshape(B_ * G, S_ // kv, kv, t.shape[-1])
    keys = lambda t, kv: tiles(_aug_k(t), kv)
    block_onehot = (jnp.arange(S_)[:, None] // SLC_LEN == jnp.arange(N_SLC)[None, :]).astype(BF16)
    k_slc_aug = jnp.concatenate([_aug_k(k_slc), jnp.broadcast_to(block_onehot[None, :, None, :], (B_, S_, G, N_SLC))], axis=-1)
    vals = lambda t, kv: t.astype(BF16).reshape(B_, S_ // kv, kv, G, Dh).transpose(0, 3, 1, 4, 2).reshape(B_ * G, S_ // kv, Dh, kv)
    padc = ((0, 0), (0, N_CMP_PAD - n_cmp), (0, 0), (0, 0))
    kc = _aug_k(jnp.pad(k_cmp, padc)).transpose(0, 2, 1, 3).reshape(B_ * G, N_CMP_PAD, NSA_DK)
    vct = jnp.pad(v_cmp, padc).astype(BF16).transpose(0, 2, 3, 1).reshape(B_ * G, Dh, N_CMP_PAD)
    per_bg = lambda *blk: pl.BlockSpec((None,) + blk, lambda b, i: (b,) + (0,) * len(blk))
    per_tile = lambda *blk: pl.BlockSpec((None,) + blk, lambda b, i: (b * n_qt + i,) + (0,) * len(blk))
    out = pl.pallas_call(
        _nsa_kernel,
        grid=(B_ * G, n_qt),
        in_specs=[per_tile(NSA_HPG, NSA_DK, NSA_TQ),
                  per_bg(N_CMP_PAD, NSA_DK), per_bg(Dh, N_CMP_PAD),
                  per_bg(S_ // NSA_KV, NSA_KV, NSA_DK + N_SLC), per_bg(S_ // NSA_KV, Dh, NSA_KV),
                  per_bg(n_qt, NSA_TQ, NSA_DK), per_bg(n_qt, Dh, NSA_TQ),
                  per_tile(N_NSA_BRANCH, NSA_N),
                  pl.BlockSpec((N_SLC, N_CMP_PAD), lambda b, i: (0, 0))],
        out_specs=per_tile(Dh, NSA_N),
        out_shape=jax.ShapeDtypeStruct((B_ * G * n_qt, Dh, NSA_N), F32),
        name="nsa_attention",
    )(qt, kc, vct, tiles(k_slc_aug, NSA_KV), vals(v_slc, NSA_KV), keys(k_win, NSA_TQ), vals(v_win, NSA_TQ), gt, ovl)
    return out.reshape(B_, G, n_qt, Dh, NSA_N)


DENSE_TM = 256
SEC_MERGE = N_BRANCH * D_MODEL
SEC_SMALL = 256
IN_SECTIONS = (BRANCH_WIDTH,) * 4 + (KV_WIDTH,) * 6 + (SEC_MERGE, SEC_SMALL)


def _rms(x, g):
    return x * lax.rsqrt(jnp.mean(x * x, axis=-1, keepdims=True) + RMS_EPS) * g


def _row_spec(width):
    return pl.BlockSpec((DENSE_TM, width), lambda i: (i, 0))


def _whole_spec(shape):
    return pl.BlockSpec(shape, lambda i: (0,) * len(shape))


def _in_proj_kernel(x_ref, g_ref, w_ref, *o_refs):
    xn = _rms(x_ref[...], g_ref[...]).astype(BF16)
    off = 0
    for o_ref, width in zip(o_refs, IN_SECTIONS):
        o_ref[...] = jnp.dot(xn, w_ref[:, off:off + width], preferred_element_type=F32)
        off += width


def in_projection(x, g, w_in):
    t, d = x.shape
    fq, fk, fv, fl, nq, kc, vc, ksl, vsl, kwn, vwn, gl, ml = split_columns(w_in, IN_SPLITS)
    pad = lambda w: jnp.pad(w, ((0, 0), (0, 128 - w.shape[1])))
    w = jnp.concatenate([fq, fk, fv, nq, kc, vc, ksl, vsl, kwn, vwn, ml, pad(fl), pad(gl)], axis=1).astype(BF16)
    return pl.pallas_call(
        _in_proj_kernel,
        grid=(t // DENSE_TM,),
        in_specs=[_row_spec(d), _whole_spec((1, d)), _whole_spec(w.shape)],
        out_specs=[_row_spec(s) for s in IN_SECTIONS],
        out_shape=[jax.ShapeDtypeStruct((t, s), F32) for s in IN_SECTIONS],
        name="in_projection",
    )(x, g.reshape(1, d), w)


def _merge_kernel(x_ref, yf_ref, yn_ref, ml_ref, wb_ref, wo_ref, gq_ref, wq_ref, o_ref, hn_ref, q_ref):
    yf_t = yf_ref[...].reshape(BRANCH_WIDTH, DENSE_TM)
    yn_t = jnp.concatenate(
        [jnp.concatenate([yn_ref[g, j, :, h * NSA_TQ:(h + 1) * NSA_TQ]
                          for g in range(NSA_GROUPS) for h in range(NSA_HPG)], axis=0)
         for j in range(DENSE_TM // NSA_TQ)], axis=1)
    g = jax.nn.sigmoid(ml_ref[...])
    up_f = jnp.dot(yf_t.T.astype(BF16), wb_ref[0], preferred_element_type=F32)
    up_n = jnp.dot(yn_t.T.astype(BF16), wb_ref[1], preferred_element_type=F32)
    merged = g[:, :D_MODEL] * up_f + g[:, D_MODEL:] * up_n
    h = x_ref[...] + jnp.dot(merged.astype(BF16), wo_ref[...], preferred_element_type=F32)
    o_ref[...] = h
    hn = _rms(h, gq_ref[...])
    hn_ref[...] = hn
    q = jnp.dot(hn.astype(BF16), wq_ref[...], preferred_element_type=F32)
    dq = PEER_QDIM // 2
    for j in range(2 * PEER_HEADS):
        q_ref[j] = q[:, j * dq:(j + 1) * dq]


def merge_out(x, y_fox, y_nsa, merge_logit, w_branch, w_out, norm_ffn, wq):
    t, d = x.shape
    hp, dq = 2 * PEER_HEADS, PEER_QDIM // 2
    assert FOX_TQ == DENSE_TM and DENSE_TM % NSA_TQ == 0
    tiles_per_seq = y_fox.shape[2]
    nsa_per_tile = DENSE_TM // NSA_TQ
    yf_spec = pl.BlockSpec((None,) + y_fox.shape[1:2] + (None,) + y_fox.shape[3:],
                           lambda i: (i // tiles_per_seq, 0, i % tiles_per_seq, 0, 0, 0))
    yn_spec = pl.BlockSpec((None, NSA_GROUPS, nsa_per_tile, HEAD_DIM, NSA_N),
                           lambda i: (i // tiles_per_seq, 0, i % tiles_per_seq, 0, 0))
    return pl.pallas_call(
        _merge_kernel,
        grid=(t // DENSE_TM,),
        in_specs=[_row_spec(d), yf_spec, yn_spec, _row_spec(SEC_MERGE),
                  _whole_spec(w_branch.shape), _whole_spec(w_out.shape), _whole_spec((1, d)), _whole_spec(wq.shape)],
        out_specs=[_row_spec(d), _row_spec(d), pl.BlockSpec((hp, DENSE_TM, dq), lambda i: (0, i, 0))],
        out_shape=[jax.ShapeDtypeStruct((t, d), F32), jax.ShapeDtypeStruct((t, d), F32),
                   jax.ShapeDtypeStruct((hp, t, dq), F32)],
        name="merge_out",
    )(x, y_fox, y_nsa, merge_logit, w_branch.astype(BF16), w_out.astype(BF16), norm_ffn.reshape(1, d),
      wq.astype(BF16))


N_CMP_ROWS = 256


def _compress_kernel(r_ref, pos_ref, w1_ref, w2_ref, o_ref):
    r = r_ref[...]
    nxt = jnp.concatenate([r[1:], jnp.zeros((1, r.shape[1]), F32)], axis=0)
    half = r.shape[1]
    hid = (jnp.dot((r + pos_ref[0:1, :]).astype(BF16), w1_ref[:half], preferred_element_type=F32)
           + jnp.dot((nxt + pos_ref[1:2, :]).astype(BF16), w1_ref[half:], preferred_element_type=F32))
    o_ref[...] = jnp.dot(_gelu(hid).astype(BF16), w2_ref[...], preferred_element_type=F32)


def compress_kv_pallas(kv, cmp_pos, w1, w2):
    _, B_, S_, G, Dh = kv.shape
    n_cmp = (S_ - CMP_LEN) // CMP_STRIDE + 1
    width = CMP_STRIDE * Dh
    rows = kv.transpose(0, 1, 3, 2, 4).reshape(2, B_ * G, S_ // CMP_STRIDE, width)
    out = pl.pallas_call(
        _compress_kernel,
        grid=(2, B_ * G),
        in_specs=[pl.BlockSpec((None, None, N_CMP_ROWS, width), lambda j, b: (j, b, 0, 0)),
                  pl.BlockSpec((None, 2, width), lambda j, b: (j, 0, 0)),
                  pl.BlockSpec((None, CMP_LEN * Dh, CMP_HIDDEN), lambda j, b: (j, 0, 0)),
                  pl.BlockSpec((None, CMP_HIDDEN, Dh), lambda j, b: (j, 0, 0))],
        out_specs=pl.BlockSpec((None, None, N_CMP_ROWS, Dh), lambda j, b: (j, b, 0, 0)),
        out_shape=jax.ShapeDtypeStruct((2, B_ * G, N_CMP_ROWS, Dh), F32),
        name="compress_kv",
    )(rows, cmp_pos.reshape(2, 2, width), w1.astype(BF16), w2.astype(BF16))
    return out[:, :, :n_cmp].reshape(2, B_, G, n_cmp, Dh).transpose(0, 1, 3, 2, 4)


def hybrid_mixer(x, norm_g, w_in, fox_f_bias, cmp_pos, cmp_w1, cmp_w2, w_branch, w_out, norm_ffn, wq):
    B_, S_, D = x.shape
    xf = x.reshape(B_ * S_, D)
    secs = in_projection(xf, norm_g, w_in)
    heads = lambda t, n: t.reshape(B_, S_, n, HEAD_DIM)
    fq, fk, fv, nq = secs[:4]
    kc, vc, ksl, vsl, kwn, vwn = (heads(t, NSA_GROUPS) for t in secs[4:10])
    merge_logit, small = secs[10:]
    f_logit = small[:, :FOX_HEADS].reshape(B_, S_, FOX_HEADS)
    gate_logit = small[:, 128:128 + NSA_HEADS * N_NSA_BRANCH].reshape(B_, S_, NSA_HEADS, N_NSA_BRANCH)
    log_f = jax.nn.log_sigmoid(f_logit + fox_f_bias.astype(F32))
    y_fox = fox_attention(heads(fq, FOX_HEADS), heads(fk, FOX_HEADS), heads(fv, FOX_HEADS), log_f)
    cos, sin = rope_tables(jnp.arange(S_, dtype=F32))
    q_nsa = apply_partial_rope(heads(nq, NSA_HEADS), cos, sin)
    k_slc = apply_partial_rope(ksl, cos, sin)
    k_win = apply_partial_rope(kwn, cos, sin)
    cmp = compress_kv_pallas(jnp.stack([kc, vc]), cmp_pos, cmp_w1, cmp_w2)
    n_cmp = cmp.shape[2]
    cmp_end = jnp.arange(n_cmp, dtype=F32) * CMP_STRIDE + (CMP_LEN - 1)
    k_cmp = apply_partial_rope(cmp[0], *rope_tables(cmp_end))
    y_nsa = nsa_attention(q_nsa, k_cmp, cmp[1], k_slc, vsl, k_win, vwn, gate_logit)
    return merge_out(xf, y_fox, y_nsa, merge_logit, w_branch, w_out, norm_ffn, wq)


def peer_ffn(h, hn, q, subkeys, u, v, norm_final):
    idx2, w = peer_topk(q, subkeys)
    coef = peer_u(idx2, hn, w, pack_table(u))
    return peer_v(idx2, coef, pack_table(v), h, norm_final)


def kernel(x, norm_mix, w_in, fox_f_bias, nsa_cmp_pos, nsa_cmp_w1, nsa_cmp_w2, w_branch, w_out,
           norm_ffn, peer_wq, peer_subkeys, peer_u, peer_v, norm_final):
    B_, S_, D = x.shape
    assert norm_mix.shape[0] == 1, "single-layer trunk"
    h, hn, q = hybrid_mixer(x, norm_mix[0], w_in[0], fox_f_bias[0], nsa_cmp_pos[0], nsa_cmp_w1[0], nsa_cmp_w2[0],
                            w_branch[0], w_out[0], norm_ffn[0], peer_wq[0])
    out = peer_ffn(h, hn, q, peer_subkeys[0], peer_u[0], peer_v[0], norm_final)
    return out.reshape(B_, S_, D)
```

```python
import jax
import jax.numpy as jnp
import numpy as np
from jax import lax
from jax.experimental import pallas as pl
from jax.experimental.pallas import tpu as pltpu

D_MODEL = 1024
HEAD_DIM = 64
FOX_HEADS = 8
NSA_HEADS = 8
NSA_GROUPS = 2
NSA_HPG = NSA_HEADS // NSA_GROUPS
BRANCH_WIDTH = 512
N_BRANCH = 2
N_NSA_BRANCH = 3
ROPE_DIM = HEAD_DIM // 4
ROPE_THETA = 500000.0
CMP_LEN = 32
CMP_STRIDE = 16
CMP_HIDDEN = 2 * HEAD_DIM
SLC_LEN = 64
SLC_TOP = 16
WINDOW = 512
FORCED_SCORE = 1e9
NEG_INF = -1e30
PEER_HEADS = 8
N_KEYS = 128
PEER_QDIM = 256
PEER_TOPK = 16
RMS_EPS = 1e-6
KV_WIDTH = NSA_GROUPS * HEAD_DIM
IN_SPLITS = (BRANCH_WIDTH, BRANCH_WIDTH, BRANCH_WIDTH, FOX_HEADS, BRANCH_WIDTH,
             KV_WIDTH, KV_WIDTH, KV_WIDTH, KV_WIDTH, KV_WIDTH, KV_WIDTH,
             NSA_HEADS * N_NSA_BRANCH, N_BRANCH * D_MODEL)


BF16 = jnp.bfloat16
F32 = jnp.float32

PEER_SLOTS = PEER_HEADS * PEER_TOPK
PEER_TT = 128
PEER_IDX_REFS = PEER_HEADS
IDX_W = PEER_SLOTS // PEER_IDX_REFS
HALF_REFS = PEER_IDX_REFS // 2
PEER_U_UNROLL = 128
PEER_V_UNROLL = 128
ROW_SUB = D_MODEL // 2 // 128
VMEM_LIMIT_PEER = N_KEYS * N_KEYS * D_MODEL * 2 + 16 * 1024 * 1024


def pack_table(tab):
    n, d = tab.shape
    steps = n // PACK_TM
    return pl.pallas_call(
        _pack_kernel,
        grid=(steps + 1,),
        in_specs=[pl.BlockSpec((PACK_TM, d), lambda i: (i % steps, 0))],
        out_specs=pl.BlockSpec((PACK_TM * ROW_SUB, 128), lambda i: (i, 0)),
        out_shape=jax.ShapeDtypeStruct(((n + 2) * ROW_SUB, 128), jnp.uint32),
        name="pack_table",
    )(tab)


PACK_TM = 256


def _pack_kernel(x_ref, o_ref):
    bits = lax.bitcast_convert_type(x_ref[...].astype(jnp.bfloat16).astype(jnp.float32), jnp.uint32)
    for r in range(ROW_SUB):
        even = bits[:, (2 * r) * 128:(2 * r + 1) * 128]
        odd = bits[:, (2 * r + 1) * 128:(2 * r + 2) * 128]
        o_ref[pl.ds(r, PACK_TM, stride=ROW_SUB), :] = (even >> 16) | (odd & jnp.uint32(0xFFFF0000))


def _unpack(w):
    lo = lax.bitcast_convert_type(w << 16, jnp.float32)
    hi = lax.bitcast_convert_type(w & jnp.uint32(0xFFFF0000), jnp.float32)
    return lo, hi


def _gelu(x):
    return 0.5 * x * (1.0 + lax.erf(x * (2.0 ** -0.5)))


def _pair_order():
    return [k * IDX_W + q for q in range(IDX_W) for k in range(HALF_REFS)]


def _pair_slab(idx_refs, tab_ref, t, s):
    k, q = divmod(s, IDX_W)
    off = t * IDX_W + q
    window = lambda row: tab_ref[pl.ds(pl.multiple_of(row, ROW_SUB), 2 * ROW_SUB), :]
    upper_rows = lax.broadcasted_iota(jnp.int32, (2 * ROW_SUB, 128), 0) < ROW_SUB
    return jnp.where(upper_rows, window(idx_refs[k][off]), window(idx_refs[k + HALF_REFS][off]))


def _gather_token(idx_refs, tab_ref, t):
    slabs = [_pair_slab(idx_refs, tab_ref, t, s) for s in _pair_order()]
    return pltpu.bitcast(jnp.concatenate(slabs, axis=0), BF16)


def _own_chunk(width):
    return ((lax.broadcasted_iota(jnp.int32, (N_CHUNK, width), 1) & (N_CHUNK - 1))
            == lax.broadcasted_iota(jnp.int32, (N_CHUNK, width), 0))


def _peer_u_kernel(*refs):
    idx_refs, (x_ref, w_ref, tab_ref, out_ref) = refs[:PEER_IDX_REFS], refs[PEER_IDX_REFS:]
    lane = lax.broadcasted_iota(jnp.int32, (2 * ROW_SUB, 128), 1)
    lower = lax.broadcasted_iota(jnp.int32, (2 * ROW_SUB, 128), 0) >= ROW_SUB
    pair_lane = jnp.where(lower, lane - PEER_SLOTS // 2, lane)

    def token(t):
        row = x_ref[pl.ds(t, 1), :]
        chunks = [row[:, c * 128:(c + 1) * 128] for c in range(2 * ROW_SUB)]
        xl = jnp.concatenate(chunks[0::2] * 2, axis=0)
        xh = jnp.concatenate(chunks[1::2] * 2, axis=0)
        acc = jnp.zeros((2 * ROW_SUB, 128), jnp.float32)
        for s in _pair_order():
            lo, hi = _unpack(_pair_slab(idx_refs, tab_ref, t, s))
            part = jnp.sum(lo * xl + hi * xh, axis=1, keepdims=True)
            acc = jnp.where(pair_lane == s, part, acc)
        out_ref[pl.ds(t, 1), :] = w_ref[pl.ds(t, 1), :] * _gelu(jnp.sum(acc, axis=0, keepdims=True))

    def trip(i, carry):
        for j in range(PEER_U_UNROLL):
            token(i * PEER_U_UNROLL + j)
        return carry

    lax.fori_loop(0, PEER_TT // PEER_U_UNROLL, trip, 0)


N_CHUNK = 2 * ROW_SUB


def _peer_v_kernel(*refs):
    idx_refs = refs[:PEER_IDX_REFS]
    coef_ref, expand_ref, h_ref, g_ref, tab_ref, out_ref, ce_hi_ref, ce_lo_ref = refs[PEER_IDX_REFS:]
    width = PEER_SLOTS * N_CHUNK
    coef = coef_ref[...]
    c_hi = _trunc_bf16(coef)
    expand = expand_ref[...]
    ce_hi_ref[...] = jnp.dot(c_hi.astype(BF16), expand, preferred_element_type=F32)
    ce_lo_ref[...] = jnp.dot((coef - c_hi).astype(BF16), expand, preferred_element_type=F32)
    own_chunk = _own_chunk(width)

    def token(t):
        gathered = _gather_token(idx_refs, tab_ref, t)
        sel = lambda ref: jnp.where(own_chunk, ref[pl.ds(t, 1), :], 0.0).astype(BF16)
        c = jnp.concatenate([sel(ce_hi_ref), sel(ce_lo_ref)], axis=0)
        o = jnp.dot(c, gathered, preferred_element_type=F32)
        o = o[:N_CHUNK] + o[N_CHUNK:]
        out_ref[pl.ds(t, 1), :] = jnp.concatenate([o[r:r + 1, :] for r in range(N_CHUNK)], axis=1)

    def trip(i, carry):
        for j in range(PEER_V_UNROLL):
            token(i * PEER_V_UNROLL + j)
        return carry

    lax.fori_loop(0, PEER_TT // PEER_V_UNROLL, trip, 0)
    out_ref[...] = _rms(h_ref[...] + out_ref[...], g_ref[...])


def _smem_specs():
    return [pl.BlockSpec((PEER_TT * IDX_W,), lambda i: (i,), memory_space=pltpu.SMEM)] * PEER_IDX_REFS


def _slot_groups(a):
    n_rows = N_KEYS * N_KEYS * ROW_SUB
    early = jnp.where(a == 0, n_rows - ROW_SUB, a - ROW_SUB)
    return [(a if g < HALF_REFS else early)[:, g * IDX_W:(g + 1) * IDX_W].reshape(-1) for g in range(PEER_IDX_REFS)]


def _slot_columns():
    half = PEER_SLOTS // 2
    block = np.arange(PEER_SLOTS * N_CHUNK) // N_CHUNK
    col_slot = np.asarray(_pair_order())[block // 2] + half * (block % 2)
    return jnp.asarray(col_slot[None, :] == np.arange(PEER_SLOTS)[:, None], dtype=BF16)


def _table_spec(rows):
    return pl.BlockSpec((rows, 128), lambda i: (0, 0), pipeline_mode=pl.Buffered(1))


def peer_u(idx, x, w, tab):
    t = x.shape[0]
    return pl.pallas_call(
        _peer_u_kernel,
        grid=(t // PEER_TT,),
        in_specs=_smem_specs() + [pl.BlockSpec((PEER_TT, D_MODEL), lambda i: (i, 0)),
                                  pl.BlockSpec((PEER_TT, PEER_SLOTS), lambda i: (i, 0)), _table_spec(tab.shape[0])],
        out_specs=pl.BlockSpec((PEER_TT, PEER_SLOTS), lambda i: (i, 0)),
        out_shape=jax.ShapeDtypeStruct((t, PEER_SLOTS), jnp.float32),
        compiler_params=pltpu.CompilerParams(vmem_limit_bytes=VMEM_LIMIT_PEER),
        name="peer_u",
    )(*_slot_groups(idx), x, w, tab)


def peer_v(idx, coef, tab, h, g):
    t = idx.shape[0]
    width = PEER_SLOTS * N_CHUNK
    expand = _slot_columns()
    return pl.pallas_call(
        _peer_v_kernel,
        grid=(t // PEER_TT,),
        in_specs=_smem_specs() + [
            pl.BlockSpec((PEER_TT, PEER_SLOTS), lambda i: (i, 0)), pl.BlockSpec((PEER_SLOTS, width), lambda i: (0, 0)),
            pl.BlockSpec((PEER_TT, D_MODEL), lambda i: (i, 0)), pl.BlockSpec((1, D_MODEL), lambda i: (0, 0)),
            _table_spec(tab.shape[0])],
        out_specs=pl.BlockSpec((PEER_TT, D_MODEL), lambda i: (i, 0)),
        out_shape=jax.ShapeDtypeStruct((t, D_MODEL), jnp.float32),
        scratch_shapes=[pltpu.VMEM((PEER_TT, width), F32), pltpu.VMEM((PEER_TT, width), F32)],
        compiler_params=pltpu.CompilerParams(vmem_limit_bytes=VMEM_LIMIT_PEER),
        name="peer_v",
    )(*_slot_groups(idx), coef, expand, h, g.reshape(1, D_MODEL), tab)


PK_TT = 128
PK_A_BLOCKS = ((0, 16), (1, 8), (2, 5), (3, 4))
PK_B_BLOCKS = ((0, 4, 16), (1, 4, 8), (2, 4, 5))
NEG_HUGE = -3.0e38
PK_HEADS_PER_TRIP = 8


def _extract_top(problems, flat, n):
    vals = [[] for _ in problems]
    picked = [[] for _ in problems]
    cur = [v for v, _ in problems]
    for _ in range(n):
        for i, (_, payload) in enumerate(problems):
            m = jnp.max(cur[i], axis=0, keepdims=True)
            pos = jnp.min(jnp.where(cur[i] == m, flat, 1e9), axis=0, keepdims=True)
            hit = flat == pos
            picked[i].append(pos if payload is None else jnp.sum(jnp.where(hit, payload, 0.0), axis=0, keepdims=True))
            cur[i] = jnp.where(hit, NEG_HUGE, cur[i])
            vals[i].append(m)
    return [jnp.concatenate(v, axis=0) for v in vals], [jnp.concatenate(p, axis=0) for p in picked]


def _candidate_grid(first, second, combine):
    up8 = lambda n: -(-n // 8) * 8
    return jnp.concatenate([combine(first[a:a + 1], second[0:up8(nb)]) for a, nb in PK_A_BLOCKS]
                           + [combine(first[0:up8(a1)], second[b:b + 1]) for b, _, a1 in PK_B_BLOCKS], axis=0)


def _candidate_valid_and_flat():
    up8 = lambda n: -(-n // 8) * 8
    valid, flat = [], []
    for a, nb in PK_A_BLOCKS:
        r = lax.broadcasted_iota(jnp.int32, (up8(nb), PK_TT), 0)
        valid.append(r < nb)
        flat.append(a * PEER_TOPK + r)
    for b, a0, a1 in PK_B_BLOCKS:
        r = lax.broadcasted_iota(jnp.int32, (up8(a1), PK_TT), 0)
        valid.append((r >= a0) & (r < a1))
        flat.append(r * PEER_TOPK + b)
    valid = jnp.concatenate(valid, axis=0)
    return valid, jnp.where(valid, jnp.concatenate(flat, axis=0).astype(F32), 1e9)


def _peer_topk_kernel(q_ref, ka_ref, eid_ref, w_ref):
    key_row = lax.broadcasted_iota(jnp.int32, (N_KEYS, PK_TT), 0).astype(F32)
    valid, flat = _candidate_valid_and_flat()

    def heads(i, carry):
        scores = []
        for j in range(2 * PK_HEADS_PER_TRIP):
            q = q_ref[2 * PK_HEADS_PER_TRIP * i + j]
            q_hi = _trunc_bf16(q)
            qa = jnp.concatenate([q_hi.astype(BF16), (q - q_hi).astype(BF16), q_hi.astype(BF16)], axis=1)
            scores.append(lax.dot_general(ka_ref[2 * PK_HEADS_PER_TRIP * i + j], qa, (((1,), (1,)), ((), ())),
                                          preferred_element_type=F32))
        sub_s, sub_i = _extract_top([(s, None) for s in scores], key_row, PEER_TOPK)
        cands = []
        for j in range(PK_HEADS_PER_TRIP):
            s1, s2, i1, i2 = sub_s[2 * j], sub_s[2 * j + 1], sub_i[2 * j], sub_i[2 * j + 1]
            cands.append((jnp.where(valid, _candidate_grid(s1, s2, lambda x, y: x + y), NEG_HUGE),
                          _candidate_grid(i1, i2, lambda x, y: x * N_KEYS + y)))
        top_s, top_e = _extract_top(cands, flat, PEER_TOPK)
        for j in range(PK_HEADS_PER_TRIP):
            e = jnp.exp(top_s[j] - top_s[j][0:1])
            w_ref[PK_HEADS_PER_TRIP * i + j] = e / jnp.sum(e, axis=0, keepdims=True)
            eid_ref[PK_HEADS_PER_TRIP * i + j] = top_e[j].astype(jnp.int32) * ROW_SUB
        return carry

    lax.fori_loop(0, PEER_HEADS // PK_HEADS_PER_TRIP, heads, 0)


def peer_topk(qt, subkeys):
    hp, t, d = qt.shape
    k_hi, k_lo = _split2(subkeys.reshape(hp, N_KEYS, d))
    ka = jnp.concatenate([k_hi, k_hi, k_lo], axis=2)
    eid, w = pl.pallas_call(
        _peer_topk_kernel,
        grid=(t // PK_TT,),
        in_specs=[pl.BlockSpec((hp, PK_TT, d), lambda i: (0, i, 0)),
                  pl.BlockSpec((hp, N_KEYS, 3 * d), lambda i: (0, 0, 0))],
        out_specs=[pl.BlockSpec((PEER_HEADS, PEER_TOPK, PK_TT), lambda i: (0, 0, i)),
                   pl.BlockSpec((PEER_HEADS, PEER_TOPK, PK_TT), lambda i: (0, 0, i))],
        out_shape=[jax.ShapeDtypeStruct((PEER_HEADS, PEER_TOPK, t), jnp.int32),
                   jax.ShapeDtypeStruct((PEER_HEADS, PEER_TOPK, t), F32)],
        name="peer_topk",
    )(qt, ka)
    to_slots = lambda a: a.transpose(2, 0, 1).reshape(t, PEER_SLOTS)
    return to_slots(eid), to_slots(w)


def split_columns(t, sizes):
    offs = np.cumsum((0,) + tuple(sizes))
    return [t[..., int(a):int(b)] for a, b in zip(offs[:-1], offs[1:])]


def rope_tables(pos):
    inv = jnp.power(ROPE_THETA, -jnp.arange(0, ROPE_DIM, 2, dtype=jnp.float32) / ROPE_DIM)
    ang = pos[:, None] * inv[None, :]
    return jnp.cos(ang), jnp.sin(ang)


def apply_partial_rope(x, cos, sin):
    half = ROPE_DIM // 2
    xr = x[..., :ROPE_DIM].astype(jnp.float32)
    x1, x2 = xr[..., :half], xr[..., half:]
    c, s = cos[:, None, :], sin[:, None, :]
    rot = jnp.concatenate([x1 * c - x2 * s, x1 * s + x2 * c], axis=-1)
    return jnp.concatenate([rot.astype(x.dtype), x[..., ROPE_DIM:]], axis=-1)


LOG2_E = 1.4426950408889634
LOGIT_SCALE = HEAD_DIM ** -0.5 * LOG2_E


def _logits(ks, qts):
    return [jnp.dot(k, qt, preferred_element_type=F32) for k, qt in zip(ks, qts)]


def _flash_update(ss, vts, mask, carries):
    stats, ps = [], []
    for s, (m, l, _) in zip(ss, carries):
        s_vis = s if mask is None else jnp.where(mask, s, NEG_INF)
        m_new = jnp.maximum(m, jnp.max(s_vis, axis=0, keepdims=True))
        alpha = jnp.exp2(m - m_new)
        p = jnp.exp2(s - m_new)
        if mask is not None:
            p = jnp.where(mask, p, 0.0)
        stats.append((m_new, alpha, alpha * l + jnp.sum(p, axis=0, keepdims=True)))
        ps.append(p.astype(BF16))
    return tuple((m_new, l, alpha * acc + jnp.dot(vt, p, preferred_element_type=F32))
                 for (m_new, alpha, l), p, vt, (_, _, acc) in zip(stats, ps, vts, carries))


def _causal_flash(key_tile, value_tile, qts, n_full, last_mask, n):
    def body(kt, carries):
        return _flash_update(_logits(key_tile(kt), qts), value_tile(kt), None, carries)

    carries = lax.fori_loop(0, n_full, body, tuple(_flash_init(n) for _ in qts))
    return _flash_update(_logits(key_tile(n_full), qts), value_tile(n_full), last_mask, carries)


def _flash_init(n):
    return (jnp.full((1, n), NEG_INF, F32), jnp.zeros((1, n), F32), jnp.zeros((HEAD_DIM, n), F32))


FOX_TQ = 256
FOX_KV = 512
FOX_DK = 256


FOX_HP = 4


def _fox_kernel(qt_ref, k_ref, vt_ref, o_ref):
    qi = pl.program_id(1)
    q0 = qi * FOX_TQ
    qts = [qt_ref[h] for h in range(FOX_HP)]
    qpos = q0 + lax.broadcasted_iota(jnp.int32, (1, FOX_TQ), 1)
    krow = lax.broadcasted_iota(jnp.int32, (FOX_KV, FOX_TQ), 0)

    n_full = q0 // FOX_KV
    causal = (n_full * FOX_KV + krow) <= qpos
    carry = _causal_flash(lambda kt: [k_ref[h, kt] for h in range(FOX_HP)],
                          lambda kt: [vt_ref[h, kt] for h in range(FOX_HP)], qts, n_full, causal, FOX_TQ)
    for h in range(FOX_HP):
        _, l, acc = carry[h]
        o_ref[h] = acc / l


def _trunc_bf16(x):
    bits = lax.bitcast_convert_type(x, jnp.uint32) & jnp.uint32(0xFFFF0000)
    return lax.bitcast_convert_type(bits, F32)


def _split3(c):
    c1 = _trunc_bf16(c)
    r = c - c1
    c2 = _trunc_bf16(r)
    return c1.astype(BF16), c2.astype(BF16), (r - c2).astype(BF16)


def _split2(x):
    hi = _trunc_bf16(x)
    return hi.astype(BF16), (x - hi).astype(BF16)


def _aug_q(q):
    hi, lo = _split2(q)
    return jnp.concatenate([hi, hi, lo], axis=-1)


def _aug_k(k):
    hi, lo = _split2(k)
    return jnp.concatenate([hi, lo, hi], axis=-1)


def fox_attention(q, k, v, log_f):
    B_, S_, H, Dh = q.shape
    KV_TILE = FOX_KV
    n_qt, n_kt = S_ // FOX_TQ, S_ // KV_TILE
    c = jnp.cumsum(log_f, axis=1)
    c1, c2, c3 = _split3(c * LOG2_E)
    j = jnp.arange(FOX_DK - 3 * Dh)
    pick = lambda t: t[..., None].astype(F32)
    terms = lambda first: jnp.where(j == first, pick(c1), jnp.where(j == first + 1, pick(c2), pick(c3)))
    q_extra = jnp.where(j < 3, 1.0, jnp.where(j < 6, terms(3), 0.0)).astype(BF16)
    k_extra = jnp.where(j < 3, -terms(0), jnp.where(j < 6, 1.0, 0.0)).astype(BF16)
    qa = jnp.concatenate([_aug_q(q * LOGIT_SCALE), q_extra], axis=-1)
    ka = jnp.concatenate([_aug_k(k), k_extra], axis=-1)
    HP, HG = FOX_HP, H // FOX_HP
    qt = qa.reshape(B_, n_qt, FOX_TQ, HG, HP, FOX_DK).transpose(0, 3, 1, 4, 5, 2).reshape(B_ * HG * n_qt, HP, FOX_DK, FOX_TQ)
    kk = ka.reshape(B_, n_kt, KV_TILE, HG, HP, FOX_DK).transpose(0, 3, 4, 1, 2, 5).reshape(B_ * HG, HP, n_kt, KV_TILE, FOX_DK)
    vt = v.astype(BF16).reshape(B_, n_kt, KV_TILE, HG, HP, Dh).transpose(0, 3, 4, 1, 5, 2).reshape(B_ * HG, HP, n_kt, Dh, KV_TILE)
    out = pl.pallas_call(
        _fox_kernel,
        grid=(B_ * HG, n_qt),
        in_specs=[pl.BlockSpec((None, HP, FOX_DK, FOX_TQ), lambda b, i: (b * n_qt + i, 0, 0, 0)),
                  pl.BlockSpec((None, HP, n_kt, KV_TILE, FOX_DK), lambda b, i: (b, 0, 0, 0, 0)),
                  pl.BlockSpec((None, HP, n_kt, Dh, KV_TILE), lambda b, i: (b, 0, 0, 0, 0))],
        out_specs=pl.BlockSpec((None, HP, Dh, FOX_TQ), lambda b, i: (b * n_qt + i, 0, 0, 0)),
        out_shape=jax.ShapeDtypeStruct((B_ * HG * n_qt, HP, Dh, FOX_TQ), F32),
        name="fox_attention",
    )(qt, kk, vt)
    return out.reshape(B_, HG, n_qt, HP, Dh, FOX_TQ)


NSA_TQ = 256
NSA_N = NSA_HPG * NSA_TQ
NSA_CH = 256
N_CMP_PAD = 256
N_SLC = 64
NSA_DK = 3 * HEAD_DIM
NSA_KV = 512
WIN_TILES = WINDOW // NSA_TQ + 1
WIN_KEYS = WIN_TILES * NSA_TQ


def _nsa_kernel(qt_ref, kc_ref, vct_ref, ks_ref, vst_ref, kw_ref, vwt_ref, gate_ref, ovl_ref, o_ref):
    qi = pl.program_id(1)
    q0 = qi * NSA_TQ
    qt = jnp.concatenate([qt_ref[h] for h in range(NSA_HPG)], axis=1)
    lane = lax.broadcasted_iota(jnp.int32, (1, NSA_N), 1)
    qpos = q0 + (lane & (NSA_TQ - 1))

    hn = NSA_CH
    n_ch = NSA_N // NSA_CH
    qts = [qt[:, c * hn:(c + 1) * hn] for c in range(n_ch)]
    qpos_h = qpos[:, :hn]

    w0 = jnp.maximum(qi - WINDOW // NSA_TQ, 0)
    k_w = jnp.concatenate([kw_ref[w0 + i] for i in range(WIN_TILES)], axis=0)
    vt_w = jnp.concatenate([vwt_ref[w0 + i] for i in range(WIN_TILES)], axis=1)
    s = jnp.dot(kc_ref[...], qt, preferred_element_type=F32)
    s_w = [jnp.dot(k_w, qts[c], preferred_element_type=F32) for c in range(n_ch)]

    cmp_end = lax.broadcasted_iota(jnp.int32, (N_CMP_PAD, NSA_N), 0) * CMP_STRIDE + (CMP_LEN - 1)
    mc = cmp_end <= qpos
    m = jnp.max(jnp.where(mc, s, NEG_INF), axis=0, keepdims=True)
    p = jnp.where(mc, jnp.exp2(s - m), 0.0)
    l = jnp.sum(p, axis=0, keepdims=True)
    pc = p * jnp.where(l > 0.0, 1.0 / l, 0.0)
    o_cmp = jnp.dot(vct_ref[...], pc.astype(BF16), preferred_element_type=F32)
    pcs = sum(pc[:, h * NSA_TQ:(h + 1) * NSA_TQ] for h in range(NSA_HPG))
    pcs_hi = pcs.astype(BF16)
    pcs_lo = (pcs - pcs_hi.astype(F32)).astype(BF16)
    ovl = ovl_ref[...]
    imp = jnp.dot(ovl, pcs_hi, preferred_element_type=F32) + jnp.dot(ovl, pcs_lo, preferred_element_type=F32)

    rel = qpos_h - (w0 * NSA_TQ + lax.broadcasted_iota(jnp.int32, (WIN_KEYS, hn), 0))
    in_win = (rel >= 0) & (rel < WINDOW)
    outs_w = []
    for c in range(n_ch):
        m = jnp.max(jnp.where(in_win, s_w[c], NEG_INF), axis=0, keepdims=True)
        p = jnp.where(in_win, jnp.exp2(s_w[c] - m), 0.0)
        l = jnp.sum(p, axis=0, keepdims=True)
        outs_w.append(jnp.dot(vt_w, p.astype(BF16), preferred_element_type=F32) / l)
    acc_w = jnp.concatenate(outs_w, axis=1)

    blk = lax.broadcasted_iota(jnp.int32, (N_SLC, NSA_TQ), 0)
    q_blk = (q0 + lax.broadcasted_iota(jnp.int32, (N_SLC, NSA_TQ), 1)) >> 6
    forced = (blk == 0) | (blk == q_blk) | (blk == q_blk - 1)
    score = jnp.where(forced, FORCED_SCORE, jnp.where(blk <= q_blk, imp, -1.0))
    rank = jnp.zeros((N_SLC, NSA_TQ), jnp.int32)
    for i in range(N_SLC):
        row = score[i:i + 1, :]
        beats = (row > score) | ((row == score) & (blk > i))
        rank = rank + beats.astype(jnp.int32)
    bias = jnp.where(rank < SLC_TOP, 0.0, NEG_INF).astype(BF16)
    bias = jnp.concatenate([bias] * (hn // NSA_TQ), axis=1)
    qts_s = [jnp.concatenate([qh, bias], axis=0) for qh in qts]
    krow = lax.broadcasted_iota(jnp.int32, (NSA_KV, hn), 0)

    n_full = q0 // NSA_KV
    causal = (n_full * NSA_KV + krow) <= qpos_h
    slc = _causal_flash(lambda kt: [ks_ref[kt]] * n_ch, lambda kt: [vst_ref[kt]] * n_ch, qts_s, n_full, causal, hn)
    acc_s = jnp.concatenate([acc / l for _, l, acc in slc], axis=1)

    g = jax.nn.sigmoid(gate_ref[...])
    o_ref[...] = g[0:1] * o_cmp + g[1:2] * acc_s + g[2:3] * acc_w


def nsa_attention(q, k_cmp, v_cmp, k_slc, v_slc, k_win, v_win, gate_logit):
    B_, S_, H, Dh = q.shape
    G = NSA_GROUPS
    n_qt = S_ // NSA_TQ
    n_cmp = k_cmp.shape[1]
    cs = np.arange(N_CMP_PAD)[None, :] * CMP_STRIDE
    ss = np.arange(N_SLC)[:, None] * SLC_LEN
    ov = np.clip(np.minimum(cs + CMP_LEN, ss + SLC_LEN) - np.maximum(cs, ss), 0, None) / CMP_LEN
    ov[:, n_cmp:] = 0.0
    ovl = jnp.asarray(ov, dtype=BF16)
    qs = _aug_q(q * LOGIT_SCALE).reshape(B_, n_qt, NSA_TQ, G, NSA_HPG, NSA_DK)
    qt = qs.transpose(0, 3, 1, 4, 5, 2).reshape(B_ * G * n_qt, NSA_HPG, NSA_DK, NSA_TQ)
    gt = gate_logit.astype(F32).reshape(B_, n_qt, NSA_TQ, G, NSA_HPG, N_NSA_BRANCH)
    gt = gt.transpose(0, 3, 1, 5, 4, 2).reshape(B_ * G * n_qt, N_NSA_BRANCH, NSA_N)
    tiles = lambda t, kv: t.reshape(B_, S_ // kv, kv, G, -1).transpose(0, 3, 1, 2, 4).reshape(B_ * G, S_ // kv, kv, t.shape[-1])
    keys = lambda t, kv: tiles(_aug_k(t), kv)
    block_onehot = (jnp.arange(S_)[:, None] // SLC_LEN == jnp.arange(N_SLC)[None, :]).astype(BF16)
    k_slc_aug = jnp.concatenate([_aug_k(k_slc), jnp.broadcast_to(block_onehot[None, :, None, :], (B_, S_, G, N_SLC))], axis=-1)
    vals = lambda t, kv: t.astype(BF16).reshape(B_, S_ // kv, kv, G, Dh).transpose(0, 3, 1, 4, 2).reshape(B_ * G, S_ // kv, Dh, kv)
    padc = ((0, 0), (0, N_CMP_PAD - n_cmp), (0, 0), (0, 0))
    kc = _aug_k(jnp.pad(k_cmp, padc)).transpose(0, 2, 1, 3).reshape(B_ * G, N_CMP_PAD, NSA_DK)
    vct = jnp.pad(v_cmp, padc).astype(BF16).transpose(0, 2, 3, 1).reshape(B_ * G, Dh, N_CMP_PAD)
    per_bg = lambda *blk: pl.BlockSpec((None,) + blk, lambda b, i: (b,) + (0,) * len(blk))
    per_tile = lambda *blk: pl.BlockSpec((None,) + blk, lambda b, i: (b * n_qt + i,) + (0,) * len(blk))
    out = pl.pallas_call(
        _nsa_kernel,
        grid=(B_ * G, n_qt),
        in_specs=[per_tile(NSA_HPG, NSA_DK, NSA_TQ),
                  per_bg(N_CMP_PAD, NSA_DK), per_bg(Dh, N_CMP_PAD),
                  per_bg(S_ // NSA_KV, NSA_KV, NSA_DK + N_SLC), per_bg(S_ // NSA_KV, Dh, NSA_KV),
                  per_bg(n_qt, NSA_TQ, NSA_DK), per_bg(n_qt, Dh, NSA_TQ),
                  per_tile(N_NSA_BRANCH, NSA_N),
                  pl.BlockSpec((N_SLC, N_CMP_PAD), lambda b, i: (0, 0))],
        out_specs=per_tile(Dh, NSA_N),
        out_shape=jax.ShapeDtypeStruct((B_ * G * n_qt, Dh, NSA_N), F32),
        name="nsa_attention",
    )(qt, kc, vct, tiles(k_slc_aug, NSA_KV), vals(v_slc, NSA_KV), keys(k_win, NSA_TQ), vals(v_win, NSA_TQ), gt, ovl)
    return out.reshape(B_, G, n_qt, Dh, NSA_N)


DENSE_TM = 256
SEC_MERGE = N_BRANCH * D_MODEL
SEC_SMALL = 256
IN_SECTIONS = (BRANCH_WIDTH,) * 4 + (KV_WIDTH,) * 6 + (SEC_MERGE, SEC_SMALL)


def _rms(x, g):
    return x * lax.rsqrt(jnp.mean(x * x, axis=-1, keepdims=True) + RMS_EPS) * g


def _row_spec(width):
    return pl.BlockSpec((DENSE_TM, width), lambda i: (i, 0))


def _whole_spec(shape):
    return pl.BlockSpec(shape, lambda i: (0,) * len(shape))


def _in_proj_kernel(x_ref, g_ref, w_ref, *o_refs):
    xn = _rms(x_ref[...], g_ref[...]).astype(BF16)
    off = 0
    for o_ref, width in zip(o_refs, IN_SECTIONS):
        o_ref[...] = jnp.dot(xn, w_ref[:, off:off + width], preferred_element_type=F32)
        off += width


def in_projection(x, g, w_in):
    t, d = x.shape
    fq, fk, fv, fl, nq, kc, vc, ksl, vsl, kwn, vwn, gl, ml = split_columns(w_in, IN_SPLITS)
    pad = lambda w: jnp.pad(w, ((0, 0), (0, 128 - w.shape[1])))
    w = jnp.concatenate([fq, fk, fv, nq, kc, vc, ksl, vsl, kwn, vwn, ml, pad(fl), pad(gl)], axis=1).astype(BF16)
    return pl.pallas_call(
        _in_proj_kernel,
        grid=(t // DENSE_TM,),
        in_specs=[_row_spec(d), _whole_spec((1, d)), _whole_spec(w.shape)],
        out_specs=[_row_spec(s) for s in IN_SECTIONS],
        out_shape=[jax.ShapeDtypeStruct((t, s), F32) for s in IN_SECTIONS],
        name="in_projection",
    )(x, g.reshape(1, d), w)


def _merge_kernel(x_ref, yf_ref, yn_ref, ml_ref, wb_ref, wo_ref, gq_ref, wq_ref, o_ref, hn_ref, q_ref):
    yf_t = yf_ref[...].reshape(BRANCH_WIDTH, DENSE_TM)
    yn_t = jnp.concatenate(
        [jnp.concatenate([yn_ref[g, j, :, h * NSA_TQ:(h + 1) * NSA_TQ]
                          for g in range(NSA_GROUPS) for h in range(NSA_HPG)], axis=0)
         for j in range(DENSE_TM // NSA_TQ)], axis=1)
    g = jax.nn.sigmoid(ml_ref[...])
    up_f = jnp.dot(yf_t.T.astype(BF16), wb_ref[0], preferred_element_type=F32)
    up_n = jnp.dot(yn_t.T.astype(BF16), wb_ref[1], preferred_element_type=F32)
    merged = g[:, :D_MODEL] * up_f + g[:, D_MODEL:] * up_n
    h = x_ref[...] + jnp.dot(merged.astype(BF16), wo_ref[...], preferred_element_type=F32)
    o_ref[...] = h
    hn = _rms(h, gq_ref[...])
    hn_ref[...] = hn
    q = jnp.dot(hn.astype(BF16), wq_ref[...], preferred_element_type=F32)
    dq = PEER_QDIM // 2
    for j in range(2 * PEER_HEADS):
        q_ref[j] = q[:, j * dq:(j + 1) * dq]


def merge_out(x, y_fox, y_nsa, merge_logit, w_branch, w_out, norm_ffn, wq):
    t, d = x.shape
    hp, dq = 2 * PEER_HEADS, PEER_QDIM // 2
    assert FOX_TQ == DENSE_TM and DENSE_TM % NSA_TQ == 0
    tiles_per_seq = y_fox.shape[2]
    nsa_per_tile = DENSE_TM // NSA_TQ
    yf_spec = pl.BlockSpec((None,) + y_fox.shape[1:2] + (None,) + y_fox.shape[3:],
                           lambda i: (i // tiles_per_seq, 0, i % tiles_per_seq, 0, 0, 0))
    yn_spec = pl.BlockSpec((None, NSA_GROUPS, nsa_per_tile, HEAD_DIM, NSA_N),
                           lambda i: (i // tiles_per_seq, 0, i % tiles_per_seq, 0, 0))
    return pl.pallas_call(
        _merge_kernel,
        grid=(t // DENSE_TM,),
        in_specs=[_row_spec(d), yf_spec, yn_spec, _row_spec(SEC_MERGE),
                  _whole_spec(w_branch.shape), _whole_spec(w_out.shape), _whole_spec((1, d)), _whole_spec(wq.shape)],
        out_specs=[_row_spec(d), _row_spec(d), pl.BlockSpec((hp, DENSE_TM, dq), lambda i: (0, i, 0))],
        out_shape=[jax.ShapeDtypeStruct((t, d), F32), jax.ShapeDtypeStruct((t, d), F32),
                   jax.ShapeDtypeStruct((hp, t, dq), F32)],
        name="merge_out",
    )(x, y_fox, y_nsa, merge_logit, w_branch.astype(BF16), w_out.astype(BF16), norm_ffn.reshape(1, d),
      wq.astype(BF16))


N_CMP_ROWS = 256


def _compress_kernel(r_ref, pos_ref, w1_ref, w2_ref, o_ref):
    r = r_ref[...]
    nxt = jnp.concatenate([r[1:], jnp.zeros((1, r.shape[1]), F32)], axis=0)
    half = r.shape[1]
    hid = (jnp.dot((r + pos_ref[0:1, :]).astype(BF16), w1_ref[:half], preferred_element_type=F32)
           + jnp.dot((nxt + pos_ref[1:2, :]).astype(BF16), w1_ref[half:], preferred_element_type=F32))
    o_ref[...] = jnp.dot(_gelu(hid).astype(BF16), w2_ref[...], preferred_element_type=F32)


def compress_kv_pallas(kv, cmp_pos, w1, w2):
    _, B_, S_, G, Dh = kv.shape
    n_cmp = (S_ - CMP_LEN) // CMP_STRIDE + 1
    width = CMP_STRIDE * Dh
    rows = kv.transpose(0, 1, 3, 2, 4).reshape(2, B_ * G, S_ // CMP_STRIDE, width)
    out = pl.pallas_call(
        _compress_kernel,
        grid=(2, B_ * G),
        in_specs=[pl.BlockSpec((None, None, N_CMP_ROWS, width), lambda j, b: (j, b, 0, 0)),
                  pl.BlockSpec((None, 2, width), lambda j, b: (j, 0, 0)),
                  pl.BlockSpec((None, CMP_LEN * Dh, CMP_HIDDEN), lambda j, b: (j, 0, 0)),
                  pl.BlockSpec((None, CMP_HIDDEN, Dh), lambda j, b: (j, 0, 0))],
        out_specs=pl.BlockSpec((None, None, N_CMP_ROWS, Dh), lambda j, b: (j, b, 0, 0)),
        out_shape=jax.ShapeDtypeStruct((2, B_ * G, N_CMP_ROWS, Dh), F32),
        name="compress_kv",
    )(rows, cmp_pos.reshape(2, 2, width), w1.astype(BF16), w2.astype(BF16))
    return out[:, :, :n_cmp].reshape(2, B_, G, n_cmp, Dh).transpose(0, 1, 3, 2, 4)


def hybrid_mixer(x, norm_g, w_in, fox_f_bias, cmp_pos, cmp_w1, cmp_w2, w_branch, w_out, norm_ffn, wq):
    B_, S_, D = x.shape
    xf = x.reshape(B_ * S_, D)
    secs = in_projection(xf, norm_g, w_in)
    heads = lambda t, n: t.reshape(B_, S_, n, HEAD_DIM)
    fq, fk, fv, nq = secs[:4]
    kc, vc, ksl, vsl, kwn, vwn = (heads(t, NSA_GROUPS) for t in secs[4:10])
    merge_logit, small = secs[10:]
    f_logit = small[:, :FOX_HEADS].reshape(B_, S_, FOX_HEADS)
    gate_logit = small[:, 128:128 + NSA_HEADS * N_NSA_BRANCH].reshape(B_, S_, NSA_HEADS, N_NSA_BRANCH)
    log_f = jax.nn.log_sigmoid(f_logit + fox_f_bias.astype(F32))
    y_fox = fox_attention(heads(fq, FOX_HEADS), heads(fk, FOX_HEADS), heads(fv, FOX_HEADS), log_f)
    cos, sin = rope_tables(jnp.arange(S_, dtype=F32))
    q_nsa = apply_partial_rope(heads(nq, NSA_HEADS), cos, sin)
    k_slc = apply_partial_rope(ksl, cos, sin)
    k_win = apply_partial_rope(kwn, cos, sin)
    cmp = compress_kv_pallas(jnp.stack([kc, vc]), cmp_pos, cmp_w1, cmp_w2)
    n_cmp = cmp.shape[2]
    cmp_end = jnp.arange(n_cmp, dtype=F32) * CMP_STRIDE + (CMP_LEN - 1)
    k_cmp = apply_partial_rope(cmp[0], *rope_tables(cmp_end))
    y_nsa = nsa_attention(q_nsa, k_cmp, cmp[1], k_slc, vsl, k_win, vwn, gate_logit)
    return merge_out(xf, y_fox, y_nsa, merge_logit, w_branch, w_out, norm_ffn, wq)


def peer_ffn(h, hn, q, subkeys, u, v, norm_final):
    idx2, w = peer_topk(q, subkeys)
    coef = peer_u(idx2, hn, w, pack_table(u))
    return peer_v(idx2, coef, pack_table(v), h, norm_final)


def kernel(x, norm_mix, w_in, fox_f_bias, nsa_cmp_pos, nsa_cmp_w1, nsa_cmp_w2, w_branch, w_out,
           norm_ffn, peer_wq, peer_subkeys, peer_u, peer_v, norm_final):
    B_, S_, D = x.shape
    assert norm_mix.shape[0] == 1, "single-layer trunk"
    h, hn, q = hybrid_mixer(x, norm_mix[0], w_in[0], fox_f_bias[0], nsa_cmp_pos[0], nsa_cmp_w1[0], nsa_cmp_w2[0],
                            w_branch[0], w_out[0], norm_ffn[0], peer_wq[0])
    out = peer_ffn(h, hn, q, peer_subkeys[0], peer_u[0], peer_v[0], norm_final)
    return out.reshape(B_, S_, D)
```

```python
import jax
import jax.numpy as jnp
import numpy as np
from jax import lax
from jax.experimental import pallas as pl
from jax.experimental.pallas import tpu as pltpu

D_MODEL = 1024
HEAD_DIM = 64
FOX_HEADS = 8
NSA_HEADS = 8
NSA_GROUPS = 2
NSA_HPG = NSA_HEADS // NSA_GROUPS
BRANCH_WIDTH = 512
N_BRANCH = 2
N_NSA_BRANCH = 3
ROPE_DIM = HEAD_DIM // 4
ROPE_THETA = 500000.0
CMP_LEN = 32
CMP_STRIDE = 16
CMP_HIDDEN = 2 * HEAD_DIM
SLC_LEN = 64
SLC_TOP = 16
WINDOW = 512
FORCED_SCORE = 1e9
NEG_INF = -1e30
PEER_HEADS = 8
N_KEYS = 128
PEER_QDIM = 256
PEER_TOPK = 16
RMS_EPS = 1e-6
KV_WIDTH = NSA_GROUPS * HEAD_DIM
IN_SPLITS = (BRANCH_WIDTH, BRANCH_WIDTH, BRANCH_WIDTH, FOX_HEADS, BRANCH_WIDTH,
             KV_WIDTH, KV_WIDTH, KV_WIDTH, KV_WIDTH, KV_WIDTH, KV_WIDTH,
             NSA_HEADS * N_NSA_BRANCH, N_BRANCH * D_MODEL)


BF16 = jnp.bfloat16
F32 = jnp.float32

PEER_SLOTS = PEER_HEADS * PEER_TOPK
PEER_TT = 128
PEER_IDX_REFS = PEER_HEADS
IDX_W = PEER_SLOTS // PEER_IDX_REFS
HALF_REFS = PEER_IDX_REFS // 2
PEER_U_UNROLL = 128
PEER_V_UNROLL = 128
ROW_SUB = D_MODEL // 2 // 128
VMEM_LIMIT_PEER = N_KEYS * N_KEYS * D_MODEL * 2 + 16 * 1024 * 1024


def pack_table(tab):
    n, d = tab.shape
    steps = n // PACK_TM
    return pl.pallas_call(
        _pack_kernel,
        grid=(steps + 1,),
        in_specs=[pl.BlockSpec((PACK_TM, d), lambda i: (i % steps, 0))],
        out_specs=pl.BlockSpec((PACK_TM * ROW_SUB, 128), lambda i: (i, 0)),
        out_shape=jax.ShapeDtypeStruct(((n + 2) * ROW_SUB, 128), jnp.uint32),
        name="pack_table",
    )(tab)


PACK_TM = 256


def _pack_kernel(x_ref, o_ref):
    bits = lax.bitcast_convert_type(x_ref[...].astype(jnp.bfloat16).astype(jnp.float32), jnp.uint32)
    for r in range(ROW_SUB):
        even = bits[:, (2 * r) * 128:(2 * r + 1) * 128]
        odd = bits[:, (2 * r + 1) * 128:(2 * r + 2) * 128]
        o_ref[pl.ds(r, PACK_TM, stride=ROW_SUB), :] = (even >> 16) | (odd & jnp.uint32(0xFFFF0000))


def _unpack(w):
    lo = lax.bitcast_convert_type(w << 16, jnp.float32)
    hi = lax.bitcast_convert_type(w & jnp.uint32(0xFFFF0000), jnp.float32)
    return lo, hi


def _gelu(x):
    return 0.5 * x * (1.0 + lax.erf(x * (2.0 ** -0.5)))


def _pair_order():
    return [k * IDX_W + q for q in range(IDX_W) for k in range(HALF_REFS)]


def _pair_slab(idx_refs, tab_ref, t, s):
    k, q = divmod(s, IDX_W)
    off = t * IDX_W + q
    window = lambda row: tab_ref[pl.ds(pl.multiple_of(row, ROW_SUB), 2 * ROW_SUB), :]
    upper_rows = lax.broadcasted_iota(jnp.int32, (2 * ROW_SUB, 128), 0) < ROW_SUB
    return jnp.where(upper_rows, window(idx_refs[k][off]), window(idx_refs[k + HALF_REFS][off]))


def _gather_token(idx_refs, tab_ref, t):
    slabs = [_pair_slab(idx_refs, tab_ref, t, s) for s in _pair_order()]
    return pltpu.bitcast(jnp.concatenate(slabs, axis=0), BF16)


def _own_chunk(width):
    return ((lax.broadcasted_iota(jnp.int32, (N_CHUNK, width), 1) & (N_CHUNK - 1))
            == lax.broadcasted_iota(jnp.int32, (N_CHUNK, width), 0))


def _peer_u_kernel(*refs):
    idx_refs, (x_ref, w_ref, tab_ref, out_ref) = refs[:PEER_IDX_REFS], refs[PEER_IDX_REFS:]
    lane = lax.broadcasted_iota(jnp.int32, (2 * ROW_SUB, 128), 1)
    lower = lax.broadcasted_iota(jnp.int32, (2 * ROW_SUB, 128), 0) >= ROW_SUB
    pair_lane = jnp.where(lower, lane - PEER_SLOTS // 2, lane)

    def token(t):
        row = x_ref[pl.ds(t, 1), :]
        chunks = [row[:, c * 128:(c + 1) * 128] for c in range(2 * ROW_SUB)]
        xl = jnp.concatenate(chunks[0::2] * 2, axis=0)
        xh = jnp.concatenate(chunks[1::2] * 2, axis=0)
        acc = jnp.zeros((2 * ROW_SUB, 128), jnp.float32)
        for s in _pair_order():
            lo, hi = _unpack(_pair_slab(idx_refs, tab_ref, t, s))
            part = jnp.sum(lo * xl + hi * xh, axis=1, keepdims=True)
            acc = jnp.where(pair_lane == s, part, acc)
        out_ref[pl.ds(t, 1), :] = w_ref[pl.ds(t, 1), :] * _gelu(jnp.sum(acc, axis=0, keepdims=True))

    def trip(i, carry):
        for j in range(PEER_U_UNROLL):
            token(i * PEER_U_UNROLL + j)
        return carry

    lax.fori_loop(0, PEER_TT // PEER_U_UNROLL, trip, 0)


N_CHUNK = 2 * ROW_SUB


def _peer_v_kernel(*refs):
    idx_refs = refs[:PEER_IDX_REFS]
    coef_ref, expand_ref, h_ref, g_ref, tab_ref, out_ref, ce_hi_ref, ce_lo_ref = refs[PEER_IDX_REFS:]
    width = PEER_SLOTS * N_CHUNK
    coef = coef_ref[...]
    c_hi = _trunc_bf16(coef)
    expand = expand_ref[...]
    ce_hi_ref[...] = jnp.dot(c_hi.astype(BF16), expand, preferred_element_type=F32)
    ce_lo_ref[...] = jnp.dot((coef - c_hi).astype(BF16), expand, preferred_element_type=F32)
    own_chunk = _own_chunk(width)

    def token(t):
        gathered = _gather_token(idx_refs, tab_ref, t)
        sel = lambda ref: jnp.where(own_chunk, ref[pl.ds(t, 1), :], 0.0).astype(BF16)
        c = jnp.concatenate([sel(ce_hi_ref), sel(ce_lo_ref)], axis=0)
        o = jnp.dot(c, gathered, preferred_element_type=F32)
        o = o[:N_CHUNK] + o[N_CHUNK:]
        out_ref[pl.ds(t, 1), :] = jnp.concatenate([o[r:r + 1, :] for r in range(N_CHUNK)], axis=1)

    def trip(i, carry):
        for j in range(PEER_V_UNROLL):
            token(i * PEER_V_UNROLL + j)
        return carry

    lax.fori_loop(0, PEER_TT // PEER_V_UNROLL, trip, 0)
    out_ref[...] = _rms(h_ref[...] + out_ref[...], g_ref[...])


def _smem_specs():
    return [pl.BlockSpec((PEER_TT * IDX_W,), lambda i: (i,), memory_space=pltpu.SMEM)] * PEER_IDX_REFS


def _slot_groups(a):
    n_rows = N_KEYS * N_KEYS * ROW_SUB
    early = jnp.where(a == 0, n_rows - ROW_SUB, a - ROW_SUB)
    return [(a if g < HALF_REFS else early)[:, g * IDX_W:(g + 1) * IDX_W].reshape(-1) for g in range(PEER_IDX_REFS)]


def _slot_columns():
    half = PEER_SLOTS // 2
    block = np.arange(PEER_SLOTS * N_CHUNK) // N_CHUNK
    col_slot = np.asarray(_pair_order())[block // 2] + half * (block % 2)
    return jnp.asarray(col_slot[None, :] == np.arange(PEER_SLOTS)[:, None], dtype=BF16)


def _table_spec(rows):
    return pl.BlockSpec((rows, 128), lambda i: (0, 0), pipeline_mode=pl.Buffered(1))


def peer_u(idx, x, w, tab):
    t = x.shape[0]
    return pl.pallas_call(
        _peer_u_kernel,
        grid=(t // PEER_TT,),
        in_specs=_smem_specs() + [pl.BlockSpec((PEER_TT, D_MODEL), lambda i: (i, 0)),
                                  pl.BlockSpec((PEER_TT, PEER_SLOTS), lambda i: (i, 0)), _table_spec(tab.shape[0])],
        out_specs=pl.BlockSpec((PEER_TT, PEER_SLOTS), lambda i: (i, 0)),
        out_shape=jax.ShapeDtypeStruct((t, PEER_SLOTS), jnp.float32),
        compiler_params=pltpu.CompilerParams(vmem_limit_bytes=VMEM_LIMIT_PEER),
        name="peer_u",
    )(*_slot_groups(idx), x, w, tab)


def peer_v(idx, coef, tab, h, g):
    t = idx.shape[0]
    width = PEER_SLOTS * N_CHUNK
    expand = _slot_columns()
    return pl.pallas_call(
        _peer_v_kernel,
        grid=(t // PEER_TT,),
        in_specs=_smem_specs() + [
            pl.BlockSpec((PEER_TT, PEER_SLOTS), lambda i: (i, 0)), pl.BlockSpec((PEER_SLOTS, width), lambda i: (0, 0)),
            pl.BlockSpec((PEER_TT, D_MODEL), lambda i: (i, 0)), pl.BlockSpec((1, D_MODEL), lambda i: (0, 0)),
            _table_spec(tab.shape[0])],
        out_specs=pl.BlockSpec((PEER_TT, D_MODEL), lambda i: (i, 0)),
        out_shape=jax.ShapeDtypeStruct((t, D_MODEL), jnp.float32),
        scratch_shapes=[pltpu.VMEM((PEER_TT, width), F32), pltpu.VMEM((PEER_TT, width), F32)],
        compiler_params=pltpu.CompilerParams(vmem_limit_bytes=VMEM_LIMIT_PEER),
        name="peer_v",
    )(*_slot_groups(idx), coef, expand, h, g.reshape(1, D_MODEL), tab)


PK_TT = 128
PK_A_BLOCKS = ((0, 16), (1, 8), (2, 5), (3, 4))
PK_B_BLOCKS = ((0, 4, 16), (1, 4, 8), (2, 4, 5))
NEG_HUGE = -3.0e38
PK_HEADS_PER_TRIP = 8


def _extract_top(problems, flat, n):
    vals = [[] for _ in problems]
    picked = [[] for _ in problems]
    cur = [v for v, _ in problems]
    for _ in range(n):
        for i, (_, payload) in enumerate(problems):
            m = jnp.max(cur[i], axis=0, keepdims=True)
            pos = jnp.min(jnp.where(cur[i] == m, flat, 1e9), axis=0, keepdims=True)
            hit = flat == pos
            picked[i].append(pos if payload is None else jnp.sum(jnp.where(hit, payload, 0.0), axis=0, keepdims=True))
            cur[i] = jnp.where(hit, NEG_HUGE, cur[i])
            vals[i].append(m)
    return [jnp.concatenate(v, axis=0) for v in vals], [jnp.concatenate(p, axis=0) for p in picked]


def _candidate_grid(first, second, combine):
    up8 = lambda n: -(-n // 8) * 8
    return jnp.concatenate([combine(first[a:a + 1], second[0:up8(nb)]) for a, nb in PK_A_BLOCKS]
                           + [combine(first[0:up8(a1)], second[b:b + 1]) for b, _, a1 in PK_B_BLOCKS], axis=0)


def _candidate_valid_and_flat():
    up8 = lambda n: -(-n // 8) * 8
    valid, flat = [], []
    for a, nb in PK_A_BLOCKS:
        r = lax.broadcasted_iota(jnp.int32, (up8(nb), PK_TT), 0)
        valid.append(r < nb)
        flat.append(a * PEER_TOPK + r)
    for b, a0, a1 in PK_B_BLOCKS:
        r = lax.broadcasted_iota(jnp.int32, (up8(a1), PK_TT), 0)
        valid.append((r >= a0) & (r < a1))
        flat.append(r * PEER_TOPK + b)
    valid = jnp.concatenate(valid, axis=0)
    return valid, jnp.where(valid, jnp.concatenate(flat, axis=0).astype(F32), 1e9)


def _peer_topk_kernel(q_ref, ka_ref, eid_ref, w_ref):
    key_row = lax.broadcasted_iota(jnp.int32, (N_KEYS, PK_TT), 0).astype(F32)
    valid, flat = _candidate_valid_and_flat()

    def heads(i, carry):
        scores = []
        for j in range(2 * PK_HEADS_PER_TRIP):
            q = q_ref[2 * PK_HEADS_PER_TRIP * i + j]
            q_hi = _trunc_bf16(q)
            qa = jnp.concatenate([q_hi.astype(BF16), (q - q_hi).astype(BF16), q_hi.astype(BF16)], axis=1)
            scores.append(lax.dot_general(ka_ref[2 * PK_HEADS_PER_TRIP * i + j], qa, (((1,), (1,)), ((), ())),
                                          preferred_element_type=F32))
        sub_s, sub_i = _extract_top([(s, None) for s in scores], key_row, PEER_TOPK)
        cands = []
        for j in range(PK_HEADS_PER_TRIP):
            s1, s2, i1, i2 = sub_s[2 * j], sub_s[2 * j + 1], sub_i[2 * j], sub_i[2 * j + 1]
            cands.append((jnp.where(valid, _candidate_grid(s1, s2, lambda x, y: x + y), NEG_HUGE),
                          _candidate_grid(i1, i2, lambda x, y: x * N_KEYS + y)))
        top_s, top_e = _extract_top(cands, flat, PEER_TOPK)
        for j in range(PK_HEADS_PER_TRIP):
            e = jnp.exp(top_s[j] - top_s[j][0:1])
            w_ref[PK_HEADS_PER_TRIP * i + j] = e / jnp.sum(e, axis=0, keepdims=True)
            eid_ref[PK_HEADS_PER_TRIP * i + j] = top_e[j].astype(jnp.int32) * ROW_SUB
        return carry

    lax.fori_loop(0, PEER_HEADS // PK_HEADS_PER_TRIP, heads, 0)


def peer_topk(qt, subkeys):
    hp, t, d = qt.shape
    k_hi, k_lo = _split2(subkeys.reshape(hp, N_KEYS, d))
    ka = jnp.concatenate([k_hi, k_hi, k_lo], axis=2)
    eid, w = pl.pallas_call(
        _peer_topk_kernel,
        grid=(t // PK_TT,),
        in_specs=[pl.BlockSpec((hp, PK_TT, d), lambda i: (0, i, 0)),
                  pl.BlockSpec((hp, N_KEYS, 3 * d), lambda i: (0, 0, 0))],
        out_specs=[pl.BlockSpec((PEER_HEADS, PEER_TOPK, PK_TT), lambda i: (0, 0, i)),
                   pl.BlockSpec((PEER_HEADS, PEER_TOPK, PK_TT), lambda i: (0, 0, i))],
        out_shape=[jax.ShapeDtypeStruct((PEER_HEADS, PEER_TOPK, t), jnp.int32),
                   jax.ShapeDtypeStruct((PEER_HEADS, PEER_TOPK, t), F32)],
        name="peer_topk",
    )(qt, ka)
    to_slots = lambda a: a.transpose(2, 0, 1).reshape(t, PEER_SLOTS)
    return to_slots(eid), to_slots(w)


def split_columns(t, sizes):
    offs = np.cumsum((0,) + tuple(sizes))
    return [t[..., int(a):int(b)] for a, b in zip(offs[:-1], offs[1:])]


def rope_tables(pos):
    inv = jnp.power(ROPE_THETA, -jnp.arange(0, ROPE_DIM, 2, dtype=jnp.float32) / ROPE_DIM)
    ang = pos[:, None] * inv[None, :]
    return jnp.cos(ang), jnp.sin(ang)


def apply_partial_rope(x, cos, sin):
    half = ROPE_DIM // 2
    xr = x[..., :ROPE_DIM].astype(jnp.float32)
    x1, x2 = xr[..., :half], xr[..., half:]
    c, s = cos[:, None, :], sin[:, None, :]
    rot = jnp.concatenate([x1 * c - x2 * s, x1 * s + x2 * c], axis=-1)
    return jnp.concatenate([rot.astype(x.dtype), x[..., ROPE_DIM:]], axis=-1)


LOG2_E = 1.4426950408889634
LOGIT_SCALE = HEAD_DIM ** -0.5 * LOG2_E


def _logits(ks, qts):
    return [jnp.dot(k, qt, preferred_element_type=F32) for k, qt in zip(ks, qts)]


def _flash_update(ss, vts, mask, carries):
    stats, ps = [], []
    for s, (m, l, _) in zip(ss, carries):
        s_vis = s if mask is None else jnp.where(mask, s, NEG_INF)
        m_new = jnp.maximum(m, jnp.max(s_vis, axis=0, keepdims=True))
        alpha = jnp.exp2(m - m_new)
        p = jnp.exp2(s - m_new)
        if mask is not None:
            p = jnp.where(mask, p, 0.0)
        stats.append((m_new, alpha, alpha * l + jnp.sum(p, axis=0, keepdims=True)))
        ps.append(p.astype(BF16))
    return tuple((m_new, l, alpha * acc + jnp.dot(vt, p, preferred_element_type=F32))
                 for (m_new, alpha, l), p, vt, (_, _, acc) in zip(stats, ps, vts, carries))


def _causal_flash(key_tile, value_tile, qts, n_full, last_mask, n):
    def body(kt, carries):
        return _flash_update(_logits(key_tile(kt), qts), value_tile(kt), None, carries)

    carries = lax.fori_loop(0, n_full, body, tuple(_flash_init(n) for _ in qts))
    return _flash_update(_logits(key_tile(n_full), qts), value_tile(n_full), last_mask, carries)


def _flash_init(n):
    return (jnp.full((1, n), NEG_INF, F32), jnp.zeros((1, n), F32), jnp.zeros((HEAD_DIM, n), F32))


FOX_TQ = 256
FOX_KV = 512
FOX_DK = 256


FOX_HP = 4


def _fox_kernel(qt_ref, k_ref, vt_ref, o_ref):
    qi = pl.program_id(1)
    q0 = qi * FOX_TQ
    qts = [qt_ref[h] for h in range(FOX_HP)]
    qpos = q0 + lax.broadcasted_iota(jnp.int32, (1, FOX_TQ), 1)
    krow = lax.broadcasted_iota(jnp.int32, (FOX_KV, FOX_TQ), 0)

    n_full = q0 // FOX_KV
    causal = (n_full * FOX_KV + krow) <= qpos
    carry = _causal_flash(lambda kt: [k_ref[h, kt] for h in range(FOX_HP)],
                          lambda kt: [vt_ref[h, kt] for h in range(FOX_HP)], qts, n_full, causal, FOX_TQ)
    for h in range(FOX_HP):
        _, l, acc = carry[h]
        o_ref[h] = acc / l


def _trunc_bf16(x):
    bits = lax.bitcast_convert_type(x, jnp.uint32) & jnp.uint32(0xFFFF0000)
    return lax.bitcast_convert_type(bits, F32)


def _split3(c):
    c1 = _trunc_bf16(c)
    r = c - c1
    c2 = _trunc_bf16(r)
    return c1.astype(BF16), c2.astype(BF16), (r - c2).astype(BF16)


def _split2(x):
    hi = _trunc_bf16(x)
    return hi.astype(BF16), (x - hi).astype(BF16)


def _aug_q(q):
    hi, lo = _split2(q)
    return jnp.concatenate([hi, hi, lo], axis=-1)


def _aug_k(k):
    hi, lo = _split2(k)
    return jnp.concatenate([hi, lo, hi], axis=-1)


def fox_attention(q, k, v, log_f):
    B_, S_, H, Dh = q.shape
    KV_TILE = FOX_KV
    n_qt, n_kt = S_ // FOX_TQ, S_ // KV_TILE
    c = jnp.cumsum(log_f, axis=1)
    c1, c2, c3 = _split3(c * LOG2_E)
    j = jnp.arange(FOX_DK - 3 * Dh)
    pick = lambda t: t[..., None].astype(F32)
    terms = lambda first: jnp.where(j == first, pick(c1), jnp.where(j == first + 1, pick(c2), pick(c3)))
    q_extra = jnp.where(j < 3, 1.0, jnp.where(j < 6, terms(3), 0.0)).astype(BF16)
    k_extra = jnp.where(j < 3, -terms(0), jnp.where(j < 6, 1.0, 0.0)).astype(BF16)
    qa = jnp.concatenate([_aug_q(q * LOGIT_SCALE), q_extra], axis=-1)
    ka = jnp.concatenate([_aug_k(k), k_extra], axis=-1)
    HP, HG = FOX_HP, H // FOX_HP
    qt = qa.reshape(B_, n_qt, FOX_TQ, HG, HP, FOX_DK).transpose(0, 3, 1, 4, 5, 2).reshape(B_ * HG * n_qt, HP, FOX_DK, FOX_TQ)
    kk = ka.reshape(B_, n_kt, KV_TILE, HG, HP, FOX_DK).transpose(0, 3, 4, 1, 2, 5).reshape(B_ * HG, HP, n_kt, KV_TILE, FOX_DK)
    vt = v.astype(BF16).reshape(B_, n_kt, KV_TILE, HG, HP, Dh).transpose(0, 3, 4, 1, 5, 2).reshape(B_ * HG, HP, n_kt, Dh, KV_TILE)
    out = pl.pallas_call(
        _fox_kernel,
        grid=(B_ * HG, n_qt),
        in_specs=[pl.BlockSpec((None, HP, FOX_DK, FOX_TQ), lambda b, i: (b * n_qt + i, 0, 0, 0)),
                  pl.BlockSpec((None, HP, n_kt, KV_TILE, FOX_DK), lambda b, i: (b, 0, 0, 0, 0)),
                  pl.BlockSpec((None, HP, n_kt, Dh, KV_TILE), lambda b, i: (b, 0, 0, 0, 0))],
        out_specs=pl.BlockSpec((None, HP, Dh, FOX_TQ), lambda b, i: (b * n_qt + i, 0, 0, 0)),
        out_shape=jax.ShapeDtypeStruct((B_ * HG * n_qt, HP, Dh, FOX_TQ), F32),
        name="fox_attention",
    )(qt, kk, vt)
    return out.reshape(B_, HG, n_qt, HP, Dh, FOX_TQ)


NSA_TQ = 256
NSA_N = NSA_HPG * NSA_TQ
NSA_CH = 256
N_CMP_PAD = 256
N_SLC = 64
NSA_DK = 3 * HEAD_DIM
NSA_KV = 512
WIN_TILES = WINDOW // NSA_TQ + 1
WIN_KEYS = WIN_TILES * NSA_TQ


def _nsa_kernel(qt_ref, kc_ref, vct_ref, ks_ref, vst_ref, kw_ref, vwt_ref, gate_ref, ovl_ref, o_ref):
    qi = pl.program_id(1)
    q0 = qi * NSA_TQ
    qt = jnp.concatenate([qt_ref[h] for h in range(NSA_HPG)], axis=1)
    lane = lax.broadcasted_iota(jnp.int32, (1, NSA_N), 1)
    qpos = q0 + (lane & (NSA_TQ - 1))

    hn = NSA_CH
    n_ch = NSA_N // NSA_CH
    qts = [qt[:, c * hn:(c + 1) * hn] for c in range(n_ch)]
    qpos_h = qpos[:, :hn]

    w0 = jnp.maximum(qi - WINDOW // NSA_TQ, 0)
    k_w = jnp.concatenate([kw_ref[w0 + i] for i in range(WIN_TILES)], axis=0)
    vt_w = jnp.concatenate([vwt_ref[w0 + i] for i in range(WIN_TILES)], axis=1)
    s = jnp.dot(kc_ref[...], qt, preferred_element_type=F32)
    s_w = [jnp.dot(k_w, qts[c], preferred_element_type=F32) for c in range(n_ch)]

    cmp_end = lax.broadcasted_iota(jnp.int32, (N_CMP_PAD, NSA_N), 0) * CMP_STRIDE + (CMP_LEN - 1)
    mc = cmp_end <= qpos
    m = jnp.max(jnp.where(mc, s, NEG_INF), axis=0, keepdims=True)
    p = jnp.where(mc, jnp.exp2(s - m), 0.0)
    l = jnp.sum(p, axis=0, keepdims=True)
    pc = p * jnp.where(l > 0.0, 1.0 / l, 0.0)
    o_cmp = jnp.dot(vct_ref[...], pc.astype(BF16), preferred_element_type=F32)
    pcs = sum(pc[:, h * NSA_TQ:(h + 1) * NSA_TQ] for h in range(NSA_HPG))
    pcs_hi = pcs.astype(BF16)
    pcs_lo = (pcs - pcs_hi.astype(F32)).astype(BF16)
    ovl = ovl_ref[...]
    imp = jnp.dot(ovl, pcs_hi, preferred_element_type=F32) + jnp.dot(ovl, pcs_lo, preferred_element_type=F32)

    rel = qpos_h - (w0 * NSA_TQ + lax.broadcasted_iota(jnp.int32, (WIN_KEYS, hn), 0))
    in_win = (rel >= 0) & (rel < WINDOW)
    outs_w = []
    for c in range(n_ch):
        m = jnp.max(jnp.where(in_win, s_w[c], NEG_INF), axis=0, keepdims=True)
        p = jnp.where(in_win, jnp.exp2(s_w[c] - m), 0.0)
        l = jnp.sum(p, axis=0, keepdims=True)
        outs_w.append(jnp.dot(vt_w, p.astype(BF16), preferred_element_type=F32) / l)
    acc_w = jnp.concatenate(outs_w, axis=1)

    blk = lax.broadcasted_iota(jnp.int32, (N_SLC, NSA_TQ), 0)
    q_blk = (q0 + lax.broadcasted_iota(jnp.int32, (N_SLC, NSA_TQ), 1)) >> 6
    forced = (blk == 0) | (blk == q_blk) | (blk == q_blk - 1)
    score = jnp.where(forced, FORCED_SCORE, jnp.where(blk <= q_blk, imp, -1.0))
    rank = jnp.zeros((N_SLC, NSA_TQ), jnp.int32)
    for i in range(N_SLC):
        row = score[i:i + 1, :]
        beats = (row > score) | ((row == score) & (blk > i))
        rank = rank + beats.astype(jnp.int32)
    bias = jnp.where(rank < SLC_TOP, 0.0, NEG_INF).astype(BF16)
    bias = jnp.concatenate([bias] * (hn // NSA_TQ), axis=1)
    qts_s = [jnp.concatenate([qh, bias], axis=0) for qh in qts]
    krow = lax.broadcasted_iota(jnp.int32, (NSA_KV, hn), 0)

    n_full = q0 // NSA_KV
    causal = (n_full * NSA_KV + krow) <= qpos_h
    slc = _causal_flash(lambda kt: [ks_ref[kt]] * n_ch, lambda kt: [vst_ref[kt]] * n_ch, qts_s, n_full, causal, hn)
    acc_s = jnp.concatenate([acc / l for _, l, acc in slc], axis=1)

    g = jax.nn.sigmoid(gate_ref[...])
    o_ref[...] = g[0:1] * o_cmp + g[1:2] * acc_s + g[2:3] * acc_w


def nsa_attention(q, k_cmp, v_cmp, k_slc, v_slc, k_win, v_win, gate_logit):
    B_, S_, H, Dh = q.shape
    G = NSA_GROUPS
    n_qt = S_ // NSA_TQ
    n_cmp = k_cmp.shape[1]
    cs = np.arange(N_CMP_PAD)[None, :] * CMP_STRIDE
    ss = np.arange(N_SLC)[:, None] * SLC_LEN
    ov = np.clip(np.minimum(cs + CMP_LEN, ss + SLC_LEN) - np.maximum(cs, ss), 0, None) / CMP_LEN
    ov[:, n_cmp:] = 0.0
    ovl = jnp.asarray(ov, dtype=BF16)
    qs = _aug_q(q * LOGIT_SCALE).reshape(B_, n_qt, NSA_TQ, G, NSA_HPG, NSA_DK)
    qt = qs.transpose(0, 3, 1, 4, 5, 2).reshape(B_ * G * n_qt, NSA_HPG, NSA_DK, NSA_TQ)
    gt = gate_logit.astype(F32).reshape(B_, n_qt, NSA_TQ, G, NSA_HPG, N_NSA_BRANCH)
    gt = gt.transpose(0, 3, 1, 5, 4, 2).reshape(B_ * G * n_qt, N_NSA_BRANCH, NSA_N)
    tiles = lambda t, kv: t.reshape(B_, S_ // kv, kv, G, -1).transpose(0, 3, 1, 2, 4).reshape(B_ * G, S_ // kv, kv, t.shape[-1])
    keys = lambda t, kv: tiles(_aug_k(t), kv)
    block_onehot = (jnp.arange(S_)[:, None] // SLC_LEN == jnp.arange(N_SLC)[None, :]).astype(BF16)
    k_slc_aug = jnp.concatenate([_aug_k(k_slc), jnp.broadcast_to(block_onehot[None, :, None, :], (B_, S_, G, N_SLC))], axis=-1)
    vals = lambda t, kv: t.astype(BF16).reshape(B_, S_ // kv, kv, G, Dh).transpose(0, 3, 1, 4, 2).reshape(B_ * G, S_ // kv, Dh, kv)
    padc = ((0, 0), (0, N_CMP_PAD - n_cmp), (0, 0), (0, 0))
    kc = _aug_k(jnp.pad(k_cmp, padc)).transpose(0, 2, 1, 3).reshape(B_ * G, N_CMP_PAD, NSA_DK)
    vct = jnp.pad(v_cmp, padc).astype(BF16).transpose(0, 2, 3, 1).reshape(B_ * G, Dh, N_CMP_PAD)
    per_bg = lambda *blk: pl.BlockSpec((None,) + blk, lambda b, i: (b,) + (0,) * len(blk))
    per_tile = lambda *blk: pl.BlockSpec((None,) + blk, lambda b, i: (b * n_qt + i,) + (0,) * len(blk))
    out = pl.pallas_call(
        _nsa_kernel,
        grid=(B_ * G, n_qt),
        in_specs=[per_tile(NSA_HPG, NSA_DK, NSA_TQ),
                  per_bg(N_CMP_PAD, NSA_DK), per_bg(Dh, N_CMP_PAD),
                  per_bg(S_ // NSA_KV, NSA_KV, NSA_DK + N_SLC), per_bg(S_ // NSA_KV, Dh, NSA_KV),
                  per_bg(n_qt, NSA_TQ, NSA_DK), per_bg(n_qt, Dh, NSA_TQ),
                  per_tile(N_NSA_BRANCH, NSA_N),
                  pl.BlockSpec((N_SLC, N_CMP_PAD), lambda b, i: (0, 0))],
        out_specs=per_tile(Dh, NSA_N),
        out_shape=jax.ShapeDtypeStruct((B_ * G * n_qt, Dh, NSA_N), F32),
        name="nsa_attention",
    )(qt, kc, vct, tiles(k_slc_aug, NSA_KV), vals(v_slc, NSA_KV), keys(k_win, NSA_TQ), vals(v_win, NSA_TQ), gt, ovl)
    return out.reshape(B_, G, n_qt, Dh, NSA_N)


DENSE_TM = 256
SEC_MERGE = N_BRANCH * D_MODEL
SEC_SMALL = 256
IN_SECTIONS = (BRANCH_WIDTH,) * 4 + (KV_WIDTH,) * 6 + (SEC_MERGE, SEC_SMALL)
IN_DTYPES = (F32, F32, BF16, F32) + (F32, F32, F32, BF16, F32, BF16) + (F32, F32)


def _rms(x, g):
    return x * lax.rsqrt(jnp.mean(x * x, axis=-1, keepdims=True) + RMS_EPS) * g


def _row_spec(width):
    return pl.BlockSpec((DENSE_TM, width), lambda i: (i, 0))


def _whole_spec(shape):
    return pl.BlockSpec(shape, lambda i: (0,) * len(shape))


def _in_proj_kernel(x_ref, g_ref, w_ref, *o_refs):
    xn = _rms(x_ref[...], g_ref[...]).astype(BF16)
    off = 0
    for o_ref, width in zip(o_refs, IN_SECTIONS):
        o_ref[...] = jnp.dot(xn, w_ref[:, off:off + width], preferred_element_type=F32).astype(o_ref.dtype)
        off += width


def in_projection(x, g, w_in):
    t, d = x.shape
    fq, fk, fv, fl, nq, kc, vc, ksl, vsl, kwn, vwn, gl, ml = split_columns(w_in, IN_SPLITS)
    pad = lambda w: jnp.pad(w, ((0, 0), (0, 128 - w.shape[1])))
    w = jnp.concatenate([fq, fk, fv, nq, kc, vc, ksl, vsl, kwn, vwn, ml, pad(fl), pad(gl)], axis=1).astype(BF16)
    return pl.pallas_call(
        _in_proj_kernel,
        grid=(t // DENSE_TM,),
        in_specs=[_row_spec(d), _whole_spec((1, d)), _whole_spec(w.shape)],
        out_specs=[_row_spec(s) for s in IN_SECTIONS],
        out_shape=[jax.ShapeDtypeStruct((t, s), dt) for s, dt in zip(IN_SECTIONS, IN_DTYPES)],
        name="in_projection",
    )(x, g.reshape(1, d), w)


def _merge_kernel(x_ref, yf_ref, yn_ref, ml_ref, wb_ref, wo_ref, gq_ref, wq_ref, o_ref, hn_ref, q_ref):
    yf_t = yf_ref[...].reshape(BRANCH_WIDTH, DENSE_TM)
    yn_t = jnp.concatenate(
        [jnp.concatenate([yn_ref[g, j, :, h * NSA_TQ:(h + 1) * NSA_TQ]
                          for g in range(NSA_GROUPS) for h in range(NSA_HPG)], axis=0)
         for j in range(DENSE_TM // NSA_TQ)], axis=1)
    g = jax.nn.sigmoid(ml_ref[...])
    up_f = jnp.dot(yf_t.T.astype(BF16), wb_ref[0], preferred_element_type=F32)
    up_n = jnp.dot(yn_t.T.astype(BF16), wb_ref[1], preferred_element_type=F32)
    merged = g[:, :D_MODEL] * up_f + g[:, D_MODEL:] * up_n
    h = x_ref[...] + jnp.dot(merged.astype(BF16), wo_ref[...], preferred_element_type=F32)
    o_ref[...] = h
    hn = _rms(h, gq_ref[...])
    hn_ref[...] = hn
    q = jnp.dot(hn.astype(BF16), wq_ref[...], preferred_element_type=F32)
    dq = PEER_QDIM // 2
    for j in range(2 * PEER_HEADS):
        q_ref[j] = q[:, j * dq:(j + 1) * dq]


def merge_out(x, y_fox, y_nsa, merge_logit, w_branch, w_out, norm_ffn, wq):
    t, d = x.shape
    hp, dq = 2 * PEER_HEADS, PEER_QDIM // 2
    assert FOX_TQ == DENSE_TM and DENSE_TM % NSA_TQ == 0
    tiles_per_seq = y_fox.shape[2]
    nsa_per_tile = DENSE_TM // NSA_TQ
    yf_spec = pl.BlockSpec((None,) + y_fox.shape[1:2] + (None,) + y_fox.shape[3:],
                           lambda i: (i // tiles_per_seq, 0, i % tiles_per_seq, 0, 0, 0))
    yn_spec = pl.BlockSpec((None, NSA_GROUPS, nsa_per_tile, HEAD_DIM, NSA_N),
                           lambda i: (i // tiles_per_seq, 0, i % tiles_per_seq, 0, 0))
    return pl.pallas_call(
        _merge_kernel,
        grid=(t // DENSE_TM,),
        in_specs=[_row_spec(d), yf_spec, yn_spec, _row_spec(SEC_MERGE),
                  _whole_spec(w_branch.shape), _whole_spec(w_out.shape), _whole_spec((1, d)), _whole_spec(wq.shape)],
        out_specs=[_row_spec(d), _row_spec(d), pl.BlockSpec((hp, DENSE_TM, dq), lambda i: (0, i, 0))],
        out_shape=[jax.ShapeDtypeStruct((t, d), F32), jax.ShapeDtypeStruct((t, d), F32),
                   jax.ShapeDtypeStruct((hp, t, dq), F32)],
        name="merge_out",
    )(x, y_fox, y_nsa, merge_logit, w_branch.astype(BF16), w_out.astype(BF16), norm_ffn.reshape(1, d),
      wq.astype(BF16))


N_CMP_ROWS = 256


def _compress_kernel(r_ref, pos_ref, w1_ref, w2_ref, o_ref):
    r = r_ref[...]
    nxt = jnp.concatenate([r[1:], jnp.zeros((1, r.shape[1]), F32)], axis=0)
    half = r.shape[1]
    hid = (jnp.dot((r + pos_ref[0:1, :]).astype(BF16), w1_ref[:half], preferred_element_type=F32)
           + jnp.dot((nxt + pos_ref[1:2, :]).astype(BF16), w1_ref[half:], preferred_element_type=F32))
    o_ref[...] = jnp.dot(_gelu(hid).astype(BF16), w2_ref[...], preferred_element_type=F32)


def compress_kv_pallas(kv, cmp_pos, w1, w2):
    _, B_, S_, G, Dh = kv.shape
    n_cmp = (S_ - CMP_LEN) // CMP_STRIDE + 1
    width = CMP_STRIDE * Dh
    rows = kv.transpose(0, 1, 3, 2, 4).reshape(2, B_ * G, S_ // CMP_STRIDE, width)
    out = pl.pallas_call(
        _compress_kernel,
        grid=(2, B_ * G),
        in_specs=[pl.BlockSpec((None, None, N_CMP_ROWS, width), lambda j, b: (j, b, 0, 0)),
                  pl.BlockSpec((None, 2, width), lambda j, b: (j, 0, 0)),
                  pl.BlockSpec((None, CMP_LEN * Dh, CMP_HIDDEN), lambda j, b: (j, 0, 0)),
                  pl.BlockSpec((None, CMP_HIDDEN, Dh), lambda j, b: (j, 0, 0))],
        out_specs=pl.BlockSpec((None, None, N_CMP_ROWS, Dh), lambda j, b: (j, b, 0, 0)),
        out_shape=jax.ShapeDtypeStruct((2, B_ * G, N_CMP_ROWS, Dh), F32),
        name="compress_kv",
    )(rows, cmp_pos.reshape(2, 2, width), w1.astype(BF16), w2.astype(BF16))
    return out[:, :, :n_cmp].reshape(2, B_, G, n_cmp, Dh).transpose(0, 1, 3, 2, 4)


def hybrid_mixer(x, norm_g, w_in, fox_f_bias, cmp_pos, cmp_w1, cmp_w2, w_branch, w_out, norm_ffn, wq):
    B_, S_, D = x.shape
    xf = x.reshape(B_ * S_, D)
    secs = in_projection(xf, norm_g, w_in)
    heads = lambda t, n: t.reshape(B_, S_, n, HEAD_DIM)
    fq, fk, fv, nq = secs[:4]
    kc, vc, ksl, vsl, kwn, vwn = (heads(t, NSA_GROUPS) for t in secs[4:10])
    merge_logit, small = secs[10:]
    f_logit = small[:, :FOX_HEADS].reshape(B_, S_, FOX_HEADS)
    gate_logit = small[:, 128:128 + NSA_HEADS * N_NSA_BRANCH].reshape(B_, S_, NSA_HEADS, N_NSA_BRANCH)
    log_f = jax.nn.log_sigmoid(f_logit + fox_f_bias.astype(F32))
    y_fox = fox_attention(heads(fq, FOX_HEADS), heads(fk, FOX_HEADS), heads(fv, FOX_HEADS), log_f)
    cos, sin = rope_tables(jnp.arange(S_, dtype=F32))
    q_nsa = apply_partial_rope(heads(nq, NSA_HEADS), cos, sin)
    k_slc = apply_partial_rope(ksl, cos, sin)
    k_win = apply_partial_rope(kwn, cos, sin)
    cmp = compress_kv_pallas(jnp.stack([kc, vc]), cmp_pos, cmp_w1, cmp_w2)
    n_cmp = cmp.shape[2]
    cmp_end = jnp.arange(n_cmp, dtype=F32) * CMP_STRIDE + (CMP_LEN - 1)
    k_cmp = apply_partial_rope(cmp[0], *rope_tables(cmp_end))
    y_nsa = nsa_attention(q_nsa, k_cmp, cmp[1], k_slc, vsl, k_win, vwn, gate_logit)
    return merge_out(xf, y_fox, y_nsa, merge_logit, w_branch, w_out, norm_ffn, wq)


def peer_ffn(h, hn, q, subkeys, u, v, norm_final):
    idx2, w = peer_topk(q, subkeys)
    coef = peer_u(idx2, hn, w, pack_table(u))
    return peer_v(idx2, coef, pack_table(v), h, norm_final)


def kernel(x, norm_mix, w_in, fox_f_bias, nsa_cmp_pos, nsa_cmp_w1, nsa_cmp_w2, w_branch, w_out,
           norm_ffn, peer_wq, peer_subkeys, peer_u, peer_v, norm_final):
    B_, S_, D = x.shape
    assert norm_mix.shape[0] == 1, "single-layer trunk"
    h, hn, q = hybrid_mixer(x, norm_mix[0], w_in[0], fox_f_bias[0], nsa_cmp_pos[0], nsa_cmp_w1[0], nsa_cmp_w2[0],
                            w_branch[0], w_out[0], norm_ffn[0], peer_wq[0])
    out = peer_ffn(h, hn, q, peer_subkeys[0], peer_u[0], peer_v[0], norm_final)
    return out.reshape(B_, S_, D)
```
